```python
import jax, jax.numpy as jnp
from jax import lax
import numpy as np

D_MODEL = 1024
BATCH = 16
SEQ = 256
DEPTH = 4
DEC_BATCH = 2
DEC_SEQ = 4096
PAST_LEN = 512

GRID_W = 64
N_MIXERS = 4
DEEPNORM_ALPHA = (2.0 * DEPTH) ** 0.25
DEEPNORM_BETA = (8.0 * DEPTH) ** -0.25
ADA_CHUNKS = 6
NORM_EPS = 1e-5

M_HEADS = 4
M_DK = 128
M_DV = 256
M_CHUNK = 64
M_FGATE_BIAS = 3.0
G_HEADS = 4
G_DK = 128
G_DV = 256
G_GATE_RANK = 16
G_GATE_NORM = 16.0
G_CHUNK = 32
NA_HEADS = 16
NA_HD = 64
NA_ROWS = 8
NA_COLS = 16
NA_BAND = 2 * NA_COLS
NA_NCB = GRID_W // NA_COLS
MLA_HEADS = 16
MLA_Q_LORA = 512
MLA_KV_LORA = 256
MLA_NOPE = 64
MLA_ROPE = 32
MLA_VD = 64
ROPE_BASE = 10000.0
Q_BLOCK = 128
PEER_HEADS = 8
PEER_NKEYS = 128
PEER_EXPERTS = PEER_NKEYS * PEER_NKEYS
PEER_KDIM = 256
PEER_HALF = PEER_KDIM // 2
PEER_TOPK = 16
PEER_TBLOCK = 128

kernel_name = 'hybrid_diffusion_trunk_step'


def _f32(a):
    return a.astype(jnp.float32)


def layer_norm(x, g, b):
    xf = _f32(x)
    mu = jnp.mean(xf, -1, keepdims=True)
    var = jnp.mean(jnp.square(xf - mu), -1, keepdims=True)
    return ((xf - mu) * lax.rsqrt(var + NORM_EPS) * _f32(g) + _f32(b)).astype(x.dtype)


def rms_norm(x, g):
    xf = _f32(x)
    return (xf * lax.rsqrt(jnp.mean(jnp.square(xf), -1, keepdims=True) + NORM_EPS) * _f32(g)).astype(x.dtype)


def adaln(cvec, w, b):
    m = jax.nn.silu(cvec) @ w + b
    return jnp.split(m[:, None, :], ADA_CHUNKS, axis=-1)


def modulate(x, shift, scale):
    return x * (1 + scale) + shift


def _chunks(a, L):
    Bn, T = a.shape[0], a.shape[1]
    return jnp.swapaxes(a.reshape(Bn, T // L, L, *a.shape[2:]), 0, 1)


def _unchunks(a):
    a = jnp.swapaxes(a, 0, 1)
    return a.reshape(a.shape[0], a.shape[1] * a.shape[2], *a.shape[3:])


def _rev(a):
    return jnp.flip(a, axis=1)


def axial_rope(x):
    T, R = x.shape[1], x.shape[-1]
    ra = R // 2
    t = jnp.arange(T)
    inv = 1.0 / (ROPE_BASE ** (jnp.arange(0, ra, 2, dtype=jnp.float32) / ra))
    def rot(xa, pos):
        ang = pos.astype(jnp.float32)[:, None] * inv[None, :]
        cos = jnp.cos(ang)[None, :, None, :]
        sin = jnp.sin(ang)[None, :, None, :]
        x1, x2 = xa[..., :ra // 2], xa[..., ra // 2:]
        return jnp.concatenate([x1 * cos - x2 * sin, x1 * sin + x2 * cos], -1)
    xf = _f32(x)
    out = jnp.concatenate([rot(xf[..., :ra], t // GRID_W), rot(xf[..., ra:], t % GRID_W)], -1)
    return out.astype(x.dtype)


def blocked_attention(q, k, v):
    Bn, Lq, H, dq = q.shape
    scale = dq ** -0.5
    qb = jnp.swapaxes(q.reshape(Bn, Lq // Q_BLOCK, Q_BLOCK, H, dq), 0, 1)
    def one(qi):
        s = _f32(jnp.einsum('bqhd,bkhd->bhqk', qi, k)) * scale
        p = jax.nn.softmax(s, axis=-1).astype(v.dtype)
        return jnp.einsum('bhqk,bkhd->bqhd', p, v)
    return _unchunks(lax.map(one, qb))


def mlstm_scan(q, k, v, logi, logf, C0, n0, m0):
    L = M_CHUNK
    tri = jnp.tril(jnp.ones((L, L), dtype=bool))
    xs = tuple(_chunks(_f32(a), L) for a in (q, k, v, logi, logf))
    def step(carry, inp):
        C, n, m = carry
        qc, kc, vc, li, lf = inp
        b = jnp.swapaxes(jnp.cumsum(lf, axis=1), 1, 2)
        li = jnp.swapaxes(li, 1, 2)
        dmat = jnp.where(tri, b[..., :, None] - b[..., None, :] + li[..., None, :], -jnp.inf)
        inter = b + m[..., None]
        mt = jnp.maximum(inter, jnp.max(dmat, axis=-1))
        s = jnp.einsum('blhd,bshd->bhls', qc, kc) * jnp.exp(dmat - mt[..., None])
        ei = jnp.exp(inter - mt)
        num = jnp.einsum('bhls,bshv->bhlv', s, vc) + ei[..., None] * jnp.einsum('blhd,bhdv->bhlv', qc, C)
        den = jnp.sum(s, -1) + ei * jnp.einsum('blhd,bhd->bhl', qc, n)
        h = num / jnp.maximum(jnp.abs(den), jnp.exp(-mt))[..., None]
        bl = b[..., -1]
        g = bl[..., None] - b + li
        m_new = jnp.maximum(bl + m, jnp.max(g, -1))
        wk = jnp.exp(g - m_new[..., None])
        dec = jnp.exp(bl + m - m_new)
        C_new = dec[..., None, None] * C + jnp.einsum('bhs,bshd,bshv->bhdv', wk, kc, vc)
        n_new = dec[..., None] * n + jnp.einsum('bhs,bshd->bhd', wk, kc)
        return (C_new, n_new, m_new), jnp.swapaxes(h, 1, 2)
    (C, n, m), hs = lax.scan(step, (_f32(C0), _f32(n0), _f32(m0)), xs)
    return _unchunks(hs), C, n, m


def mlstm_mixer(h, C0, n0, m0, w_in, b_gate, norm_w, w_out):
    Bn, T, _ = h.shape
    hk, hv = M_HEADS * M_DK, M_HEADS * M_DV
    p = h @ w_in
    q = p[..., :hk].reshape(Bn, T, M_HEADS, M_DK)
    k = p[..., hk:2 * hk].reshape(Bn, T, M_HEADS, M_DK) * (M_DK ** -0.5)
    v = p[..., 2 * hk:2 * hk + hv].reshape(Bn, T, M_HEADS, M_DV)
    o = p[..., 2 * hk + hv:2 * hk + 2 * hv]
    gates = (_f32(p[..., 2 * hk + 2 * hv:]) + _f32(b_gate)).reshape(Bn, T, 4, M_HEADS)
    hf, Cf, nf, mf = mlstm_scan(q, k, v, gates[:, :, 0], jax.nn.log_sigmoid(gates[:, :, 1]),
                                C0[:, 0], n0[:, 0], m0[:, 0])
    hb, Cb, nb, mb = mlstm_scan(_rev(q), _rev(k), _rev(v), _rev(gates[:, :, 2]),
                                _rev(jax.nn.log_sigmoid(gates[:, :, 3])), C0[:, 1], n0[:, 1], m0[:, 1])
    hs = hf + _rev(hb)
    mu = jnp.mean(hs, -1, keepdims=True)
    var = jnp.mean(jnp.square(hs - mu), -1, keepdims=True)
    hn = (hs - mu) * lax.rsqrt(var + NORM_EPS) * _f32(norm_w).reshape(M_HEADS, M_DV)
    y = (jax.nn.sigmoid(_f32(o)) * hn.reshape(Bn, T, hv)).astype(h.dtype) @ w_out
    return y, jnp.stack([Cf, Cb], 1), jnp.stack([nf, nb], 1), jnp.stack([mf, mb], 1)


def gla_scan(q, k, v, loga, S0):
    L = G_CHUNK
    tri = jnp.tril(jnp.ones((L, L), dtype=bool))
    xs = tuple(_chunks(_f32(a), L) for a in (q, k, v, loga))
    def step(S, inp):
        qc, kc, vc, la = inp
        bc = jnp.cumsum(la, axis=1)
        qd = qc * jnp.exp(bc)
        kd = kc * jnp.exp(-bc)
        a = jnp.where(tri, jnp.einsum('blhd,bshd->bhls', qd, kd), 0.0)
        o = jnp.einsum('bhls,bshv->blhv', a, vc) + jnp.einsum('blhd,bhdv->blhv', qd, S)
        bl = bc[:, -1]
        kl = kc * jnp.exp(bl[:, None] - bc)
        S_new = jnp.exp(bl)[..., None] * S + jnp.einsum('blhd,blhv->bhdv', kl, vc)
        return S_new, o
    S, os_ = lax.scan(step, _f32(S0), xs)
    return _unchunks(os_), S


def gla_mixer(h, S0, w_in, w_gate2, b_gate2, norm_w, w_out):
    Bn, T, _ = h.shape
    hk, hv = G_HEADS * G_DK, G_HEADS * G_DV
    p = h @ w_in
    q = p[..., :hk].reshape(Bn, T, G_HEADS, G_DK) * (G_DK ** -0.5)
    k = p[..., hk:2 * hk].reshape(Bn, T, G_HEADS, G_DK)
    v = p[..., 2 * hk:2 * hk + hv].reshape(Bn, T, G_HEADS, G_DV)
    g = p[..., 2 * hk + hv:2 * hk + 2 * hv]
    gr = p[..., 2 * hk + 2 * hv:].reshape(Bn, T, 2, G_GATE_RANK)
    loga = jax.nn.log_sigmoid(_f32(jnp.einsum('btzr,zrk->btzk', gr, w_gate2)) + _f32(b_gate2)) / G_GATE_NORM
    loga = loga.reshape(Bn, T, 2, G_HEADS, G_DK)
    of, Sf = gla_scan(q, k, v, loga[:, :, 0], S0[:, 0])
    ob, Sb = gla_scan(_rev(q), _rev(k), _rev(v), _rev(loga[:, :, 1]), S0[:, 1])
    o = of + _rev(ob)
    on = o * lax.rsqrt(jnp.mean(jnp.square(o), -1, keepdims=True) + NORM_EPS) * _f32(norm_w)
    y = (jax.nn.silu(_f32(g)) * on.reshape(Bn, T, hv)).astype(h.dtype) @ w_out
    return y, jnp.stack([Sf, Sb], 1)


def na_project(h, w_in):
    Bn, T, _ = h.shape
    qkv = (h @ w_in).reshape(Bn, T, 3, NA_HEADS, NA_HD)
    return qkv[:, :, 0], qkv[:, :, 1], qkv[:, :, 2]


def na_latent(q, k, v, k_ctx, v_ctx, rpb):
    Bn, T, H, Dh = q.shape
    rows = T // GRID_W
    wr = min(NA_ROWS, rows)
    scale = Dh ** -0.5
    col_q = np.arange(GRID_W).reshape(NA_NCB, NA_COLS)
    band = np.clip(np.arange(NA_NCB) * NA_COLS - NA_COLS // 2, 0, GRID_W - NA_BAND)[:, None] + np.arange(NA_BAND)[None, :]
    cs = np.clip(col_q - NA_COLS // 2, 0, GRID_W - NA_COLS)
    col_ok = (band[:, None, :] >= cs[:, :, None]) & (band[:, None, :] < cs[:, :, None] + NA_COLS)
    dc = np.clip(band[:, None, :] - col_q[:, :, None], -(NA_COLS - 1), NA_COLS - 1) + (NA_COLS - 1)
    rpb_c = _f32(rpb)[:, :, dc]
    qg = q.reshape(Bn, rows, GRID_W, H, Dh)
    kg = k.reshape(Bn, rows, GRID_W, H, Dh)
    vg = v.reshape(Bn, rows, GRID_W, H, Dh)
    nloc = wr * NA_BAND
    def one_row(args):
        qr, r = args
        start = jnp.clip(r - wr // 2, 0, rows - wr)
        kb = lax.dynamic_slice_in_dim(kg, start, wr, axis=1)[:, :, band]
        vb = lax.dynamic_slice_in_dim(vg, start, wr, axis=1)[:, :, band]
        dr = start + jnp.arange(wr) - r + (NA_ROWS - 1)
        bias = jnp.transpose(rpb_c[:, dr], (0, 2, 3, 1, 4))
        qc = qr.reshape(Bn, NA_NCB, NA_COLS, H, Dh)
        s_loc = _f32(jnp.einsum('bcqhd,bacwhd->bhcqaw', qc, kb)) * scale + bias[None]
        s_loc = jnp.where(col_ok[:, :, None, :], s_loc, -jnp.inf).reshape(Bn, H, NA_NCB, NA_COLS, nloc)
        s_ctx = _f32(jnp.einsum('bcqhd,bkhd->bhcqk', qc, k_ctx)) * scale
        p = jax.nn.softmax(jnp.concatenate([s_loc, s_ctx], -1), axis=-1).astype(v.dtype)
        p_loc = p[..., :nloc].reshape(Bn, H, NA_NCB, NA_COLS, wr, NA_BAND)
        o = jnp.einsum('bhcqaw,bacwhd->bcqhd', p_loc, vb) + jnp.einsum('bhcqk,bkhd->bcqhd', p[..., nloc:], v_ctx)
        return o.reshape(Bn, GRID_W, H, Dh)
    o = lax.map(one_row, (jnp.swapaxes(qg, 0, 1), jnp.arange(rows)))
    return jnp.swapaxes(o, 0, 1).reshape(Bn, T, H, Dh)


def mla_project(h, w_in, q_norm, w_qup, kv_norm):
    Bn, T, _ = h.shape
    p = h @ w_in
    cq = p[..., :MLA_Q_LORA]
    ckv = p[..., MLA_Q_LORA:MLA_Q_LORA + MLA_KV_LORA]
    kpe = p[..., MLA_Q_LORA + MLA_KV_LORA:]
    q = (rms_norm(cq, q_norm) @ w_qup).reshape(Bn, T, MLA_HEADS, MLA_NOPE + MLA_ROPE)
    return q, rms_norm(ckv, kv_norm), kpe


def mla_keys(ckv, kpe, w_kvup):
    Bn, L, _ = ckv.shape
    kv = (ckv @ w_kvup).reshape(Bn, L, MLA_HEADS, MLA_NOPE + MLA_VD)
    k = jnp.concatenate([kv[..., :MLA_NOPE],
                         jnp.broadcast_to(kpe[:, :, None, :], (Bn, L, MLA_HEADS, MLA_ROPE)).astype(kv.dtype)], -1)
    return k, kv[..., MLA_NOPE:]


def peer_ffn(x, w_q, subkeys, u_tab, v_tab):
    Bn, T, D = x.shape
    xb = x.reshape(Bn * T // PEER_TBLOCK, PEER_TBLOCK, D)
    def block(xt):
        qh = _f32(xt @ w_q).reshape(PEER_TBLOCK, PEER_HEADS, 2, PEER_HALF)
        s = jnp.einsum('thpd,hpnd->thpn', qh, _f32(subkeys))
        v_top, i_top = lax.top_k(s, PEER_TOPK)
        cand = (v_top[:, :, 0, :, None] + v_top[:, :, 1, None, :]).reshape(PEER_TBLOCK, PEER_HEADS, PEER_TOPK * PEER_TOPK)
        cidx = (i_top[:, :, 0, :, None] * PEER_NKEYS + i_top[:, :, 1, None, :]).reshape(PEER_TBLOCK, PEER_HEADS, PEER_TOPK * PEER_TOPK)
        sc, pos = lax.top_k(cand, PEER_TOPK)
        eidx = jnp.take_along_axis(cidx, pos, axis=-1).reshape(PEER_TBLOCK, PEER_HEADS * PEER_TOPK)
        gate = jax.nn.softmax(sc, axis=-1).reshape(PEER_TBLOCK, PEER_HEADS * PEER_TOPK)
        act = jax.nn.gelu(_f32(jnp.einsum('td,ted->te', xt, u_tab[eidx])))
        return jnp.einsum('te,ted->td', (gate * act).astype(xt.dtype), v_tab[eidx])
    return lax.map(block, xb).reshape(Bn, T, D)


def setup_inputs(seed: int = 0) -> dict:
    key = jax.random.key(seed)
    keys = jax.random.split(key, 64)
    counter = [0]
    def nrm(shape, scale):
        kk = keys[counter[0]]
        counter[0] += 1
        return jax.random.normal(kk, shape, jnp.float32) * scale
    D = D_MODEL
    fan = D ** -0.5
    inp = {}
    inp['x_prompt'] = nrm((BATCH, SEQ, D), 1.0)
    inp['x_sample'] = nrm((DEC_BATCH, DEC_SEQ, D), 1.0)
    inp['c'] = nrm((DEC_BATCH, D), 1.0)
    inp['c_ctx'] = nrm((D,), 1.0)
    inp['state_mlstm_C'] = nrm((DEC_BATCH, 2, M_HEADS, M_DK, M_DV), 0.1)
    inp['state_mlstm_n'] = nrm((DEC_BATCH, 2, M_HEADS, M_DK), 0.1)
    inp['state_mlstm_m'] = nrm((DEC_BATCH, 2, M_HEADS), 1.0)
    inp['state_gla_S'] = nrm((DEC_BATCH, 2, G_HEADS, G_DK, G_DV), 0.1)
    inp['cache_na_k'] = nrm((DEC_BATCH, PAST_LEN, NA_HEADS, NA_HD), 1.0)
    inp['cache_na_v'] = nrm((DEC_BATCH, PAST_LEN, NA_HEADS, NA_HD), 1.0)
    inp['cache_mla_ckv'] = nrm((DEC_BATCH, PAST_LEN, MLA_KV_LORA), 1.0)
    inp['cache_mla_kpe'] = nrm((DEC_BATCH, PAST_LEN, MLA_ROPE), 1.0)
    inp['ada_w'] = nrm((DEPTH, D, ADA_CHUNKS * D), fan)
    inp['ada_b'] = nrm((DEPTH, ADA_CHUNKS * D), 0.1)
    inp['ln_mix_g'] = 1.0 + nrm((DEPTH, D), 0.05)
    inp['ln_mix_b'] = nrm((DEPTH, D), 0.05)
    inp['ln_ffn_g'] = 1.0 + nrm((DEPTH, D), 0.05)
    inp['ln_ffn_b'] = nrm((DEPTH, D), 0.05)
    m_cols = 2 * M_HEADS * M_DK + 2 * M_HEADS * M_DV + 4 * M_HEADS
    inp['mlstm_w_in'] = nrm((D, m_cols), fan)
    inp['mlstm_b_gate'] = nrm((4 * M_HEADS,), 0.1) + jnp.repeat(jnp.array([0.0, M_FGATE_BIAS, 0.0, M_FGATE_BIAS], jnp.float32), M_HEADS)
    inp['mlstm_norm_w'] = 1.0 + nrm((M_HEADS * M_DV,), 0.05)
    inp['mlstm_w_out'] = nrm((M_HEADS * M_DV, D), (M_HEADS * M_DV) ** -0.5 * DEEPNORM_BETA)
    g_cols = 2 * G_HEADS * G_DK + 2 * G_HEADS * G_DV + 2 * G_GATE_RANK
    inp['gla_w_in'] = nrm((D, g_cols), fan)
    inp['gla_w_gate2'] = nrm((2, G_GATE_RANK, G_HEADS * G_DK), G_GATE_RANK ** -0.5)
    inp['gla_b_gate2'] = nrm((2, G_HEADS * G_DK), 0.1)
    inp['gla_norm_w'] = 1.0 + nrm((G_DV,), 0.05)
    inp['gla_w_out'] = nrm((G_HEADS * G_DV, D), (G_HEADS * G_DV) ** -0.5 * DEEPNORM_BETA)
    inp['na_w_in'] = nrm((D, 3 * NA_HEADS * NA_HD), fan)
    inp['na_rpb'] = nrm((NA_HEADS, 2 * NA_ROWS - 1, 2 * NA_COLS - 1), 0.5)
    inp['na_w_out'] = nrm((NA_HEADS * NA_HD, D), (NA_HEADS * NA_HD) ** -0.5 * DEEPNORM_BETA)
    inp['mla_w_in'] = nrm((D, MLA_Q_LORA + MLA_KV_LORA + MLA_ROPE), fan)
    inp['mla_q_norm'] = 1.0 + nrm((MLA_Q_LORA,), 0.05)
    inp['mla_w_qup'] = nrm((MLA_Q_LORA, MLA_HEADS * (MLA_NOPE + MLA_ROPE)), MLA_Q_LORA ** -0.5)
    inp['mla_kv_norm'] = 1.0 + nrm((MLA_KV_LORA,), 0.05)
    inp['mla_w_kvup'] = nrm((MLA_KV_LORA, MLA_HEADS * (MLA_NOPE + MLA_VD)), MLA_KV_LORA ** -0.5)
    inp['mla_w_out'] = nrm((MLA_HEADS * MLA_VD, D), (MLA_HEADS * MLA_VD) ** -0.5 * DEEPNORM_BETA)
    inp['peer_w_q'] = nrm((DEPTH, D, PEER_HEADS * PEER_KDIM), fan)
    inp['peer_subkeys'] = nrm((DEPTH, PEER_HEADS, 2, PEER_NKEYS, PEER_HALF), PEER_HALF ** -0.5)
    inp['peer_u'] = nrm((DEPTH, PEER_EXPERTS, D), fan)
    inp['peer_v'] = nrm((DEPTH, PEER_EXPERTS, D), DEEPNORM_BETA)
    return inp


def reference(x_prompt, x_sample, c, c_ctx, state_mlstm_C, state_mlstm_n, state_mlstm_m, state_gla_S,
              cache_na_k, cache_na_v, cache_mla_ckv, cache_mla_kpe,
              ada_w, ada_b, ln_mix_g, ln_mix_b, ln_ffn_g, ln_ffn_b,
              mlstm_w_in, mlstm_b_gate, mlstm_norm_w, mlstm_w_out,
              gla_w_in, gla_w_gate2, gla_b_gate2, gla_norm_w, gla_w_out,
              na_w_in, na_rpb, na_w_out,
              mla_w_in, mla_q_norm, mla_w_qup, mla_kv_norm, mla_w_kvup, mla_w_out,
              peer_w_q, peer_subkeys, peer_u, peer_v):
    xp, xs = x_prompt, x_sample
    Bp, Lp = xp.shape[0], xp.shape[1]
    Bs, Ts = xs.shape[0], xs.shape[1]
    new_mlstm_C = new_mlstm_n = new_mlstm_m = new_gla_S = None
    new_na_k = new_na_v = new_mla_ckv = new_mla_kpe = None
    for l in range(DEPTH):
        kind = l % N_MIXERS
        mp = adaln(c_ctx[None, :], ada_w[l], ada_b[l])
        ms = adaln(c, ada_w[l], ada_b[l])
        hp = modulate(xp, mp[0], mp[1])
        hs = modulate(xs, ms[0], ms[1])
        if kind == 0:
            zC = jnp.zeros((Bp, 2, M_HEADS, M_DK, M_DV), jnp.float32)
            zn = jnp.zeros((Bp, 2, M_HEADS, M_DK), jnp.float32)
            zm = jnp.zeros((Bp, 2, M_HEADS), jnp.float32)
            yp, new_mlstm_C, new_mlstm_n, new_mlstm_m = mlstm_mixer(hp, zC, zn, zm, mlstm_w_in, mlstm_b_gate, mlstm_norm_w, mlstm_w_out)
            ys, _, _, _ = mlstm_mixer(hs, state_mlstm_C, state_mlstm_n, state_mlstm_m, mlstm_w_in, mlstm_b_gate, mlstm_norm_w, mlstm_w_out)
        elif kind == 1:
            zS = jnp.zeros((Bp, 2, G_HEADS, G_DK, G_DV), jnp.float32)
            yp, new_gla_S = gla_mixer(hp, zS, gla_w_in, gla_w_gate2, gla_b_gate2, gla_norm_w, gla_w_out)
            ys, _ = gla_mixer(hs, state_gla_S, gla_w_in, gla_w_gate2, gla_b_gate2, gla_norm_w, gla_w_out)
        elif kind == 2:
            qp, kp, vp = na_project(hp, na_w_in)
            yp = blocked_attention(qp, kp, vp).reshape(Bp, Lp, NA_HEADS * NA_HD) @ na_w_out
            new_na_k, new_na_v = kp, vp
            ql, kl, vl = na_project(hs, na_w_in)
            ys = na_latent(ql, kl, vl, cache_na_k, cache_na_v, na_rpb).reshape(Bs, Ts, NA_HEADS * NA_HD) @ na_w_out
        else:
            qp, ckvp, kpep = mla_project(hp, mla_w_in, mla_q_norm, mla_w_qup, mla_kv_norm)
            kp, vp = mla_keys(ckvp, kpep, mla_w_kvup)
            yp = blocked_attention(qp, kp, vp).reshape(Bp, Lp, MLA_HEADS * MLA_VD) @ mla_w_out
            new_mla_ckv, new_mla_kpe = ckvp, kpep
            ql, ckvl, kpel = mla_project(hs, mla_w_in, mla_q_norm, mla_w_qup, mla_kv_norm)
            ql = jnp.concatenate([ql[..., :MLA_NOPE], axial_rope(ql[..., MLA_NOPE:])], -1)
            kpel = axial_rope(kpel[:, :, None, :])[:, :, 0]
            kl, vl = mla_keys(ckvl, kpel, mla_w_kvup)
            kc, vc = mla_keys(cache_mla_ckv, cache_mla_kpe, mla_w_kvup)
            ys = blocked_attention(ql, jnp.concatenate([kl, kc], 1), jnp.concatenate([vl, vc], 1))
            ys = ys.reshape(Bs, Ts, MLA_HEADS * MLA_VD) @ mla_w_out
        xp = layer_norm(DEEPNORM_ALPHA * xp + mp[2] * yp, ln_mix_g[l], ln_mix_b[l])
        xs = layer_norm(DEEPNORM_ALPHA * xs + ms[2] * ys, ln_mix_g[l], ln_mix_b[l])
        fp = peer_ffn(modulate(xp, mp[3], mp[4]), peer_w_q[l], peer_subkeys[l], peer_u[l], peer_v[l])
        fs = peer_ffn(modulate(xs, ms[3], ms[4]), peer_w_q[l], peer_subkeys[l], peer_u[l], peer_v[l])
        xp = layer_norm(DEEPNORM_ALPHA * xp + mp[5] * fp, ln_ffn_g[l], ln_ffn_b[l])
        xs = layer_norm(DEEPNORM_ALPHA * xs + ms[5] * fs, ln_ffn_g[l], ln_ffn_b[l])
    return (xp, xs, new_mlstm_C, new_mlstm_n, new_mlstm_m, new_gla_S, new_na_k, new_na_v, new_mla_ckv, new_mla_kpe)
```

```python
import functools

import numpy as np
import jax
import jax.numpy as jnp
from jax import lax
from jax.experimental import pallas as pl
from jax.experimental.pallas import tpu as pltpu

D_MODEL = 1024
DEPTH = 4
GRID_W = 64
DEEPNORM_ALPHA = (2.0 * DEPTH) ** 0.25
ADA_CHUNKS = 6
NORM_EPS = 1e-5
SEG = 4096
NSEG = 3

M_HEADS, M_DK, M_DV = 4, 128, 256
G_HEADS, G_DK, G_DV = 4, 128, 256
G_GATE_RANK = 16
G_GATE_NORM = 16.0
NA_HEADS, NA_HD, NA_ROWS, NA_COLS = 16, 64, 8, 16
MLA_HEADS, MLA_Q_LORA, MLA_KV_LORA, MLA_NOPE, MLA_ROPE, MLA_VD = 16, 512, 256, 64, 32, 64
ROPE_BASE = 10000.0
PEER_HEADS, PEER_NKEYS, PEER_HALF, PEER_TOPK = 8, 128, 128, 16

V7X_VMEM_LIMIT = 56 * 1024 * 1024
F32 = jnp.float32
BF16 = jnp.bfloat16
NEG_INF = float("-inf")


def _params(sem, vmem=V7X_VMEM_LIMIT):
    return pltpu.CompilerParams(dimension_semantics=sem, vmem_limit_bytes=vmem)


def _dot(a, b, dims=((1,), (0,))):
    return lax.dot_general(a, b, (dims, ((), ())), preferred_element_type=F32)


def _split3(a):
    hi = a.astype(BF16)
    r1 = a - hi.astype(F32)
    mid = r1.astype(BF16)
    lo = (r1 - mid.astype(F32)).astype(BF16)
    return hi, mid, lo


def _dot_exact_lhs(m01, a):
    hi, mid, lo = _split3(a)
    return _dot(m01, hi) + _dot(m01, mid) + _dot(m01, lo)


def _dot_exact_rhs(a, m01):
    hi, mid, lo = _split3(a)
    return _dot(hi, m01) + _dot(mid, m01) + _dot(lo, m01)


def _log_sigmoid(x):
    return jnp.minimum(x, 0.0) - jnp.log(1.0 + jnp.exp(-jnp.abs(x)))


def _sigmoid(x):
    return 1.0 / (1.0 + jnp.exp(-x))


def _adaln_kernel(c_ref, w_ref, b_ref, o_ref):
    cv = c_ref[...]
    a = cv * _sigmoid(cv)
    o_ref[0] = lax.dot_general(a, w_ref[0], (((1,), (0,)), ((), ())), precision=lax.Precision.HIGHEST,
                               preferred_element_type=F32) + b_ref[0]


def adaln_all(cond8, ada_w, ada_b):
    tn = 1024
    n = ada_w.shape[-1]
    return pl.pallas_call(
        _adaln_kernel,
        grid=(DEPTH, n // tn),
        in_specs=[pl.BlockSpec((8, D_MODEL), lambda l, j: (0, 0)),
                  pl.BlockSpec((1, D_MODEL, tn), lambda l, j: (l, 0, j)),
                  pl.BlockSpec((1, 1, tn), lambda l, j: (l, 0, j))],
        out_specs=pl.BlockSpec((1, 8, tn), lambda l, j: (l, 0, j)),
        out_shape=jax.ShapeDtypeStruct((DEPTH, 8, n), F32),
        compiler_params=_params(("arbitrary", "arbitrary")),
        name="adaln",
    )(cond8, ada_w, ada_b.reshape(DEPTH, 1, n))


def _modmm_kernel(x_ref, sh_ref, sc_ref, w_ref, o_ref, xm_ref):
    @pl.when(pl.program_id(2) == 0)
    def _():
        xm_ref[...] = (x_ref[0] * (1.0 + sc_ref[0]) + sh_ref[0]).astype(BF16)

    o_ref[0] = _dot(xm_ref[...], w_ref[...]).astype(o_ref.dtype)


def mod_matmul(x3, mod3, shift_chunk, w_bf16, tm=512, tn=None, out_dtype=F32):
    nseg, seg, d = x3.shape
    n = w_bf16.shape[1]
    tn = n if tn is None else tn
    return pl.pallas_call(
        _modmm_kernel,
        grid=(nseg, seg // tm, n // tn),
        in_specs=[pl.BlockSpec((1, tm, d), lambda s, i, j: (s, i, 0)),
                  pl.BlockSpec((1, 1, d), lambda s, i, j: (s, 0, shift_chunk)),
                  pl.BlockSpec((1, 1, d), lambda s, i, j: (s, 0, shift_chunk + 1)),
                  pl.BlockSpec((d, tn), lambda s, i, j: (0, j))],
        out_specs=pl.BlockSpec((1, tm, tn), lambda s, i, j: (s, i, j)),
        out_shape=jax.ShapeDtypeStruct((nseg, seg, n), out_dtype),
        scratch_shapes=[pltpu.VMEM((tm, d), BF16)],
        compiler_params=_params(("arbitrary", "arbitrary", "arbitrary")),
        name="mod_matmul",
    )(x3, mod3, mod3, w_bf16)


def _layer_norm_rows(y, g, b):
    mu = jnp.mean(y, axis=-1, keepdims=True)
    yc = y - mu
    var = jnp.mean(yc * yc, axis=-1, keepdims=True)
    return yc * lax.rsqrt(var + NORM_EPS) * g + b


def _outproj_kernel(*refs, mode):
    if mode == "plain":
        y_ref, x_ref, gate_ref, w_ref, g_ref, b_ref, o_ref = refs
        yin = y_ref[0].astype(BF16)
    else:
        ya_ref, yb_ref, og_ref, nw_ref, x_ref, gate_ref, w_ref, g_ref, b_ref, o_ref = refs
        hs = ya_ref[0] + yb_ref[0]
        og = og_ref[0]
        parts = []
        for h in range(4):
            seg = hs[:, h * 256:(h + 1) * 256]
            nw = nw_ref[:, h * 256:(h + 1) * 256]
            if mode == "mlstm":
                mu = jnp.mean(seg, axis=-1, keepdims=True)
                sc = seg - mu
                var = jnp.mean(sc * sc, axis=-1, keepdims=True)
                parts.append(sc * lax.rsqrt(var + NORM_EPS) * nw)
            else:
                ms = jnp.mean(seg * seg, axis=-1, keepdims=True)
                parts.append(seg * lax.rsqrt(ms + NORM_EPS) * nw)
        hn = jnp.concatenate(parts, axis=-1)
        act = _sigmoid(og) if mode == "mlstm" else og * _sigmoid(og)
        yin = (act * hn).astype(BF16)
    y = _dot(yin, w_ref[...])
    z = DEEPNORM_ALPHA * x_ref[0] + gate_ref[0] * y
    o_ref[0] = _layer_norm_rows(z, g_ref[...], b_ref[...])


def outproj_ln(mode, ys, x3, mod3, gate_chunk, w_bf16, ln_g, ln_b, norm_w=None, og=None, og_col=0, tm=512):
    nseg, seg, d = x3.shape
    k = w_bf16.shape[0]
    tok = lambda s, i: (s, i, 0)
    if mode == "plain":
        args = [ys]
        specs = [pl.BlockSpec((1, tm, k), tok)]
    else:
        args = [ys[0], ys[1], og, norm_w.reshape(1, k)]
        specs = [pl.BlockSpec((1, tm, k), tok), pl.BlockSpec((1, tm, k), tok),
                 pl.BlockSpec((1, tm, k), lambda s, i: (s, i, og_col)),
                 pl.BlockSpec((1, k), lambda s, i: (0, 0))]
    args += [x3, mod3, w_bf16, ln_g.reshape(1, d), ln_b.reshape(1, d)]
    specs += [pl.BlockSpec((1, tm, d), tok),
              pl.BlockSpec((1, 1, d), lambda s, i: (s, 0, gate_chunk)),
              pl.BlockSpec((k, d), lambda s, i: (0, 0)),
              pl.BlockSpec((1, d), lambda s, i: (0, 0)),
              pl.BlockSpec((1, d), lambda s, i: (0, 0))]
    return pl.pallas_call(
        functools.partial(_outproj_kernel, mode=mode),
        grid=(nseg, seg // tm),
        in_specs=specs,
        out_specs=pl.BlockSpec((1, tm, d), tok),
        out_shape=jax.ShapeDtypeStruct((nseg, seg, d), F32),
        compiler_params=_params(("arbitrary", "arbitrary")),
        name="outproj_ln_" + mode,
    )(*args)


def _tri(n, lower):
    r = lax.broadcasted_iota(jnp.int32, (n, n), 0)
    c = lax.broadcasted_iota(jnp.int32, (n, n), 1)
    return (c <= r) if lower else (c >= r)


def _mlstm_kernel(pf_ref, pb_ref, gf_ref, gb_ref, gtf_ref, gtb_ref, bias_ref, biast_ref,
                  c0_ref, n0_ref, m0_ref, hf_ref, hb_ref, co_ref, no_ref, mo_ref,
                  c_s, n_s, m_s, *, L):
    c = pl.program_id(1)

    @pl.when(c == 0)
    def _():
        c_s[...] = c0_ref[0]
        n_s[...] = n0_ref[0]
        m_s[...] = m0_ref[0]

    for d in range(2):
        p_ref, g_ref, gt_ref, h_ref = ((pf_ref, gf_ref, gtf_ref, hf_ref) if d == 0
                                       else (pb_ref, gb_ref, gtb_ref, hb_ref))
        mask = _tri(L, lower=(d == 0))
        mcol = mask.astype(BF16)
        mrow = _tri(L, lower=(d != 0)).astype(BF16)
        g = g_ref[0] + bias_ref[...]
        gt = gt_ref[0] + biast_ref[...]
        li_c = g[:, d * 8:d * 8 + 4]
        lf_c = _log_sigmoid(g[:, d * 8 + 4:d * 8 + 8])
        li_r = gt[d * 8:d * 8 + 4, :]
        lf_r = _log_sigmoid(gt[d * 8 + 4:d * 8 + 8, :])
        b_c = _dot_exact_lhs(mcol, lf_c)
        b_r = _dot_exact_rhs(lf_r, mrow)
        last = L - 1 if d == 0 else 0
        for h in range(M_HEADS):
            u = d * M_HEADS + h
            q = p_ref[0, :, h * M_DK:(h + 1) * M_DK]
            k = p_ref[0, :, 512 + h * M_DK:512 + (h + 1) * M_DK] * (M_DK ** -0.5)
            v = p_ref[0, :, 1024 + h * M_DV:1024 + (h + 1) * M_DV].astype(BF16)
            qb = q.astype(BF16)
            bc, br = b_c[:, h:h + 1], b_r[h:h + 1, :]
            lic, lir = li_c[:, h:h + 1], li_r[h:h + 1, :]
            m_prev = m_s[u:u + 1, 0:1]
            dmat = jnp.where(mask, bc - br + lir, NEG_INF)
            inter = bc + m_prev
            mt = jnp.maximum(inter, jnp.max(dmat, axis=-1, keepdims=True))
            smat = _dot(qb, k.astype(BF16), ((1,), (1,))) * jnp.exp(dmat - mt)
            ei = jnp.exp(inter - mt)
            cmat = c_s[u]
            num = _dot(smat.astype(BF16), v) + ei * _dot(qb, cmat.astype(BF16))
            nrow = n_s[u:u + 1, :]
            den = jnp.sum(smat, axis=-1, keepdims=True) + ei * jnp.sum(q * nrow, axis=-1, keepdims=True)
            h_ref[0, :, h * M_DV:(h + 1) * M_DV] = num / jnp.maximum(jnp.abs(den), jnp.exp(-mt))
            tot = br[:, last:last + 1]
            g_c = tot - bc + lic
            g_r = tot - br + lir
            m_new = jnp.maximum(tot + m_prev, jnp.max(g_r, axis=-1, keepdims=True))
            kw = k * jnp.exp(g_c - m_new)
            dec = jnp.exp(tot + m_prev - m_new)
            c_s[u] = dec * cmat + _dot(kw.astype(BF16), v, ((0,), (0,)))
            n_s[u:u + 1, :] = dec * nrow + jnp.sum(kw, axis=0, keepdims=True)
            m_s[u:u + 1, :] = jnp.broadcast_to(m_new, (1, 128))

    @pl.when(c == pl.num_programs(1) - 1)
    def _():
        co_ref[0] = c_s[...]
        no_ref[0] = n_s[...]
        mo_ref[0] = m_s[...]


def mlstm_scan(p, b0, nb, t, g, gt, bias, c0, n0, m0, L):
    nc = t // L
    hshape = jax.ShapeDtypeStruct((nb, t, M_HEADS * M_DV), F32)
    fwd = lambda b, c: (b + b0, c, 0)
    bwd = lambda b, c: (b + b0, nc - 1 - c, 0)
    st4 = lambda b, c: (b, 0, 0, 0)
    st3 = lambda b, c: (b, 0, 0)
    return pl.pallas_call(
        functools.partial(_mlstm_kernel, L=L),
        grid=(nb, nc),
        in_specs=[pl.BlockSpec((1, L, 2048), fwd), pl.BlockSpec((1, L, 2048), bwd),
                  pl.BlockSpec((1, L, 16), lambda b, c: (b, c, 0)),
                  pl.BlockSpec((1, L, 16), lambda b, c: (b, nc - 1 - c, 0)),
                  pl.BlockSpec((1, 16, L), lambda b, c: (b, 0, c)),
                  pl.BlockSpec((1, 16, L), lambda b, c: (b, 0, nc - 1 - c)),
                  pl.BlockSpec((1, 16), lambda b, c: (0, 0)),
                  pl.BlockSpec((16, 1), lambda b, c: (0, 0)),
                  pl.BlockSpec((1, 8, M_DK, M_DV), st4),
                  pl.BlockSpec((1, 8, M_DK), st3),
                  pl.BlockSpec((1, 8, M_DK), st3)],
        out_specs=[pl.BlockSpec((1, L, 1024), lambda b, c: (b, c, 0)),
                   pl.BlockSpec((1, L, 1024), lambda b, c: (b, nc - 1 - c, 0)),
                   pl.BlockSpec((1, 8, M_DK, M_DV), st4),
                   pl.BlockSpec((1, 8, M_DK), st3),
                   pl.BlockSpec((1, 8, M_DK), st3)],
        out_shape=[hshape, hshape,
                   jax.ShapeDtypeStruct((nb, 8, M_DK, M_DV), F32),
                   jax.ShapeDtypeStruct((nb, 8, M_DK), F32),
                   jax.ShapeDtypeStruct((nb, 8, M_DK), F32)],
        scratch_shapes=[pltpu.VMEM((8, M_DK, M_DV), F32), pltpu.VMEM((8, M_DK), F32),
                        pltpu.VMEM((8, M_DK), F32)],
        compiler_params=_params(("arbitrary", "arbitrary")),
        name="mlstm_scan",
    )(p, p, g, g, gt, gt, bias.reshape(1, 16), bias.reshape(16, 1), c0, n0, m0)


def _gla_kernel(pf_ref, pb_ref, gf_ref, gb_ref, w2_ref, b2_ref, s0_ref, of_ref, ob_ref, so_ref, s_s, *, L):
    c = pl.program_id(1)

    @pl.when(c == 0)
    def _():
        s_s[...] = s0_ref[0]

    for d in range(2):
        p_ref, g_ref, o_ref = (pf_ref, gf_ref, of_ref) if d == 0 else (pb_ref, gb_ref, ob_ref)
        mask = _tri(L, lower=(d == 0))
        mcol = mask.astype(BF16)
        gr = g_ref[0][:, d * G_GATE_RANK:(d + 1) * G_GATE_RANK]
        pre = lax.dot_general(gr, w2_ref[d], (((1,), (0,)), ((), ())), precision=lax.Precision.HIGHEST,
                              preferred_element_type=F32) + b2_ref[d]
        la = _log_sigmoid(pre) * (1.0 / G_GATE_NORM)
        bc_all = _dot_exact_lhs(mcol, la)
        last = L - 1 if d == 0 else 0
        for h in range(G_HEADS):
            u = d * G_HEADS + h
            q = p_ref[0, :, h * G_DK:(h + 1) * G_DK] * (G_DK ** -0.5)
            k = p_ref[0, :, 512 + h * G_DK:512 + (h + 1) * G_DK]
            v = p_ref[0, :, 1024 + h * G_DV:1024 + (h + 1) * G_DV].astype(BF16)
            bc = bc_all[:, h * G_DK:(h + 1) * G_DK]
            qd = (q * jnp.exp(bc)).astype(BF16)
            kd = (k * jnp.exp(-bc)).astype(BF16)
            a = jnp.where(mask, _dot(qd, kd, ((1,), (1,))), 0.0)
            st = s_s[u]
            o_ref[0, :, h * G_DV:(h + 1) * G_DV] = (_dot(a.astype(BF16), v)
                                                    + _dot(qd, st.astype(BF16), ((1,), (1,))))
            bl = bc[last:last + 1, :]
            kl = (k * jnp.exp(bl - bc)).astype(BF16)
            s_s[u] = st * jnp.exp(bl) + _dot(v, kl, ((0,), (0,)))

    @pl.when(c == pl.num_programs(1) - 1)
    def _():
        so_ref[0] = s_s[...]


def gla_scan(p, b0, nb, t, gr, w2, b2, s0t, L):
    nc = t // L
    oshape = jax.ShapeDtypeStruct((nb, t, G_HEADS * G_DV), F32)
    st4 = lambda b, c: (b, 0, 0, 0)
    return pl.pallas_call(
        functools.partial(_gla_kernel, L=L),
        grid=(nb, nc),
        in_specs=[pl.BlockSpec((1, L, 2048), lambda b, c: (b + b0, c, 0)),
                  pl.BlockSpec((1, L, 2048), lambda b, c: (b + b0, nc - 1 - c, 0)),
                  pl.BlockSpec((1, L, 32), lambda b, c: (b, c, 0)),
                  pl.BlockSpec((1, L, 32), lambda b, c: (b, nc - 1 - c, 0)),
                  pl.BlockSpec((2, G_GATE_RANK, 512), lambda b, c: (0, 0, 0)),
                  pl.BlockSpec((2, 1, 512), lambda b, c: (0, 0, 0)),
                  pl.BlockSpec((1, 8, G_DV, G_DK), st4)],
        out_specs=[pl.BlockSpec((1, L, 1024), lambda b, c: (b, c, 0)),
                   pl.BlockSpec((1, L, 1024), lambda b, c: (b, nc - 1 - c, 0)),
                   pl.BlockSpec((1, 8, G_DV, G_DK), st4)],
        out_shape=[oshape, oshape, jax.ShapeDtypeStruct((nb, 8, G_DV, G_DK), F32)],
        scratch_shapes=[pltpu.VMEM((8, G_DV, G_DK), F32)],
        compiler_params=_params(("arbitrary", "arbitrary")),
        name="gla_scan",
    )(p, p, gr, gr, w2, b2.reshape(2, 1, 512), s0t)


def _attn_kernel(q_ref, k_ref, v_ref, o_ref, *, scale):
    s = _dot(q_ref[0, 0], k_ref[0, 0], ((1,), (1,))) * scale
    m = jnp.max(s, axis=-1, keepdims=True)
    p = jnp.exp(s - m)
    l = jnp.sum(p, axis=-1, keepdims=True)
    o_ref[0, 0] = _dot(p.astype(BF16), v_ref[0, 0]) / l


def attention(q, k, v, tq):
    b, h, lq, dq = q.shape
    lk, dv = k.shape[2], v.shape[3]
    return pl.pallas_call(
        functools.partial(_attn_kernel, scale=dq ** -0.5),
        grid=(b, h, lq // tq),
        in_specs=[pl.BlockSpec((1, 1, tq, dq), lambda b, h, i: (b, h, i, 0)),
                  pl.BlockSpec((1, 1, lk, dq), lambda b, h, i: (b, h, 0, 0)),
                  pl.BlockSpec((1, 1, lk, dv), lambda b, h, i: (b, h, 0, 0))],
        out_specs=pl.BlockSpec((1, 1, tq, dv), lambda b, h, i: (b, h, i, 0)),
        out_shape=jax.ShapeDtypeStruct((b, h, lq, dv), F32),
        compiler_params=_params(("arbitrary", "arbitrary", "arbitrary")),
        name="attention",
    )(q, k, v)


NA_RB = 8


def _na_kernel(q_ref, k_ref, v_ref, kc_ref, vc_ref, bias_ref, o_ref, *, rows):
    j = pl.program_id(2)
    scale = NA_HD ** -0.5
    kc, vc = kc_ref[0, 0], vc_ref[0, 0]
    for a in range(NA_RB):
        r = j * NA_RB + a
        start = jnp.clip(r - NA_ROWS // 2, 0, rows - NA_ROWS)
        dr0 = start - r + (NA_ROWS - 1)
        off = pl.multiple_of(start * GRID_W, GRID_W)
        qa = q_ref[0, 0, a * GRID_W:(a + 1) * GRID_W, :]
        kl = k_ref[0, 0, pl.ds(off, NA_ROWS * GRID_W), :]
        vl = v_ref[0, 0, pl.ds(off, NA_ROWS * GRID_W), :]
        s_loc = _dot(qa, kl, ((1,), (1,))) * scale + bias_ref[0, dr0]
        s_ctx = _dot(qa, kc, ((1,), (1,))) * scale
        m = jnp.maximum(jnp.max(s_loc, axis=-1, keepdims=True), jnp.max(s_ctx, axis=-1, keepdims=True))
        p_loc = jnp.exp(s_loc - m)
        p_ctx = jnp.exp(s_ctx - m)
        l = jnp.sum(p_loc, axis=-1, keepdims=True) + jnp.sum(p_ctx, axis=-1, keepdims=True)
        o = _dot(p_loc.astype(BF16), vl) + _dot(p_ctx.astype(BF16), vc)
        o_ref[0, 0, a * GRID_W:(a + 1) * GRID_W, :] = o / l


def na_bias_table(rpb):
    cq = np.arange(GRID_W)[:, None]
    ck = np.arange(GRID_W)[None, :]
    cs = np.clip(cq - NA_COLS // 2, 0, GRID_W - NA_COLS)
    ok = (ck >= cs) & (ck < cs + NA_COLS)
    dc = np.clip(ck - cq, -(NA_COLS - 1), NA_COLS - 1) + (NA_COLS - 1)
    t = jnp.where(ok[None, None], rpb.astype(F32)[:, :, dc], NEG_INF)
    rows = np.arange(NA_ROWS)[:, None] + np.arange(NA_ROWS)[None, :]
    tf = t[:, rows]
    return jnp.transpose(tf, (0, 1, 3, 2, 4)).reshape(NA_HEADS, NA_ROWS, GRID_W, NA_ROWS * GRID_W)


def na_attention(q, k, v, kc, vc, bias):
    b, h, t, dh = q.shape
    lc = kc.shape[2]
    rows = t // GRID_W
    full = lambda b, h, j: (b, h, 0, 0)
    return pl.pallas_call(
        functools.partial(_na_kernel, rows=rows),
        grid=(b, h, rows // NA_RB),
        in_specs=[pl.BlockSpec((1, 1, NA_RB * GRID_W, dh), lambda b, h, j: (b, h, j, 0)),
                  pl.BlockSpec((1, 1, t, dh), full), pl.BlockSpec((1, 1, t, dh), full),
                  pl.BlockSpec((1, 1, lc, dh), full), pl.BlockSpec((1, 1, lc, dh), full),
                  pl.BlockSpec((1, NA_ROWS, GRID_W, NA_ROWS * GRID_W), lambda b, h, j: (h, 0, 0, 0))],
        out_specs=pl.BlockSpec((1, 1, NA_RB * GRID_W, dh), lambda b, h, j: (b, h, j, 0)),
        out_shape=jax.ShapeDtypeStruct((b, h, t, dh), F32),
        compiler_params=_params(("arbitrary", "arbitrary", "arbitrary")),
        name="na_attention",
    )(q, k, v, kc, vc, bias)


def _rms_rows(x, g):
    return x * lax.rsqrt(jnp.mean(x * x, axis=-1, keepdims=True) + NORM_EPS) * g


def _mla_q_kernel(cq_ref, g_ref, w_ref, cos_ref, sin_ref, o_ref):
    r = _dot(_rms_rows(cq_ref[0], g_ref[...]).astype(BF16), w_ref[...])
    nn = MLA_HEADS * MLA_NOPE
    nr = MLA_HEADS * MLA_ROPE
    o_ref[0, :, :nn] = r[:, :nn]
    o_ref[0, :, nn:] = r[:, nn:nn + nr] * cos_ref[0] + r[:, nn + nr:] * sin_ref[0]


def mla_q(p, q_norm, w_q3, cos_q, sin_q, tm=512):
    nseg, seg, _ = p.shape
    nout = MLA_HEADS * (MLA_NOPE + MLA_ROPE)
    nr = MLA_HEADS * MLA_ROPE
    tok = lambda s, i: (s, i, 0)
    return pl.pallas_call(
        _mla_q_kernel,
        grid=(nseg, seg // tm),
        in_specs=[pl.BlockSpec((1, tm, MLA_Q_LORA), tok),
                  pl.BlockSpec((1, MLA_Q_LORA), lambda s, i: (0, 0)),
                  pl.BlockSpec(w_q3.shape, lambda s, i: (0, 0)),
                  pl.BlockSpec((1, tm, nr), tok), pl.BlockSpec((1, tm, nr), tok)],
        out_specs=pl.BlockSpec((1, tm, nout), tok),
        out_shape=jax.ShapeDtypeStruct((nseg, seg, nout), F32),
        compiler_params=_params(("arbitrary", "arbitrary")),
        name="mla_q",
    )(p, q_norm.reshape(1, -1), w_q3, cos_q, sin_q)


def _mla_kv_kernel(ckv_ref, kpe_ref, g_ref, w_ref, cos_ref, sin_ref, ckvn_ref, kpeo_ref, kv_ref):
    cn = _rms_rows(ckv_ref[0], g_ref[...])
    ckvn_ref[0] = cn
    kv_ref[0] = _dot(cn.astype(BF16), w_ref[...])
    kp = kpe_ref[0]
    kpeo_ref[0] = kp[:, :MLA_ROPE] * cos_ref[0] + kp[:, MLA_ROPE:2 * MLA_ROPE] * sin_ref[0]


def mla_kv(p, kv_norm, w_kv, cos_k, sin_k, tm=512):
    nseg, seg, _ = p.shape
    nkv = w_kv.shape[1]
    tok = lambda s, i: (s, i, 0)
    return pl.pallas_call(
        _mla_kv_kernel,
        grid=(nseg, seg // tm),
        in_specs=[pl.BlockSpec((1, tm, MLA_KV_LORA), lambda s, i: (s, i, MLA_Q_LORA // MLA_KV_LORA)),
                  pl.BlockSpec((1, tm, 128), lambda s, i: (s, i, (MLA_Q_LORA + MLA_KV_LORA) // 128)),
                  pl.BlockSpec((1, MLA_KV_LORA), lambda s, i: (0, 0)),
                  pl.BlockSpec(w_kv.shape, lambda s, i: (0, 0)),
                  pl.BlockSpec((1, tm, MLA_ROPE), tok), pl.BlockSpec((1, tm, MLA_ROPE), tok)],
        out_specs=[pl.BlockSpec((1, tm, MLA_KV_LORA), tok), pl.BlockSpec((1, tm, MLA_ROPE), tok),
                   pl.BlockSpec((1, tm, nkv), tok)],
        out_shape=[jax.ShapeDtypeStruct((nseg, seg, MLA_KV_LORA), F32),
                   jax.ShapeDtypeStruct((nseg, seg, MLA_ROPE), F32),
                   jax.ShapeDtypeStruct((nseg, seg, nkv), F32)],
        compiler_params=_params(("arbitrary", "arbitrary")),
        name="mla_kv",
    )(p, p, kv_norm.reshape(1, -1), w_kv, cos_k, sin_k)


def _mm_kernel(a_ref, w_ref, o_ref):
    o_ref[...] = _dot(a_ref[...].astype(BF16), w_ref[...])


def matmul(a, w_bf16, tm):
    m, k = a.shape
    n = w_bf16.shape[1]
    return pl.pallas_call(
        _mm_kernel,
        grid=(m // tm,),
        in_specs=[pl.BlockSpec((tm, k), lambda i: (i, 0)), pl.BlockSpec((k, n), lambda i: (0, 0))],
        out_specs=pl.BlockSpec((tm, n), lambda i: (i, 0)),
        out_shape=jax.ShapeDtypeStruct((m, n), F32),
        compiler_params=_params(("arbitrary",)),
        name="matmul",
    )(a, w_bf16)


PEER_RT = 128
NOT_TOP = 99.0


def _top16(s):
    key = lax.broadcasted_iota(jnp.int32, s.shape, 0).astype(F32)
    rank = jnp.full(s.shape, NOT_TOP, F32)
    vals = []
    for r in range(PEER_TOPK):
        m = jnp.max(s, axis=0, keepdims=True)
        first = jnp.min(jnp.where(s == m, key, 1e9), axis=0, keepdims=True)
        hit = key == first
        rank = jnp.where(hit, float(r), rank)
        s = jnp.where(hit, NEG_INF, s)
        vals.append(m)
    return vals, rank


def _pair_topk(av, bv):
    n = av[0].shape[-1]
    a_lo, a_hi = jnp.concatenate(av[:8], 0), jnp.concatenate(av[8:], 0)
    b_lo, b_hi = jnp.concatenate(bv[:8], 0), jnp.concatenate(bv[8:], 0)
    row = lax.broadcasted_iota(jnp.int32, (8, n), 0).astype(F32)

    no_pos = 1e8

    def rows_b(a, b_blk, boff, nvalid):
        ok = row < nvalid
        return jnp.where(ok, av[a] + b_blk, NEG_INF), jnp.where(ok, a * 16.0 + boff + row, no_pos)

    def rows_a(b, a_blk, aoff, lo, hi):
        ok = (row >= lo) & (row < hi)
        return jnp.where(ok, a_blk + bv[b], NEG_INF), jnp.where(ok, (aoff + row) * 16.0 + b, no_pos)

    groups = [rows_b(0, b_lo, 0, 8), rows_b(0, b_hi, 8, 8), rows_b(1, b_lo, 0, 8), rows_b(2, b_lo, 0, 5),
              rows_b(3, b_lo, 0, 4), rows_a(0, a_lo, 0, 4, 8), rows_a(0, a_hi, 8, 0, 8),
              rows_a(1, a_lo, 0, 4, 8), rows_a(2, a_lo, 0, 4, 5)]
    cands = [g[0] for g in groups]
    poss = [g[1] for g in groups]
    sels = [jnp.zeros((8, n), F32) for _ in groups]
    top = av[0] + bv[0]
    z = jnp.zeros((1, n), F32)
    for _ in range(PEER_TOPK):
        m = functools.reduce(jnp.maximum, cands)
        m = jnp.max(m, axis=0, keepdims=True)
        first = functools.reduce(jnp.minimum, [jnp.where(c == m, p, 1e9) for c, p in zip(cands, poss)])
        first = jnp.min(first, axis=0, keepdims=True)
        hits = [p == first for p in poss]
        cands = [jnp.where(hh, NEG_INF, c) for hh, c in zip(hits, cands)]
        sels = [jnp.where(hh, 1.0, s) for hh, s in zip(hits, sels)]
        z = z + jnp.exp(m - top)
    cnt = lambda x: jnp.sum(x, axis=0, keepdims=True)
    cut_lo = sels[5] + sels[7] + sels[8]
    for a, c in enumerate([cnt(sels[0]) + cnt(sels[1]), cnt(sels[2]), cnt(sels[3]), cnt(sels[4])]):
        cut_lo = cut_lo + jnp.where(row == a, c, 0.0)
    return cut_lo, sels[6], z


def _peer_route_kernel(x_ref, sh_ref, sc_ref, wq_ref, sk_ref, xm_ref, e1_ref, cut_ref, e2_ref, r2_ref, q_s, *, tm):
    xm = (x_ref[0] * (1.0 + sc_ref[0]) + sh_ref[0]).astype(BF16)
    xm_ref[0] = xm
    q = _dot(xm, wq_ref[...])
    for hp in range(2 * PEER_HEADS):
        q_s[hp] = q[:, hp * PEER_HALF:(hp + 1) * PEER_HALF]

    def body(it, carry):
        h = it // (tm // PEER_RT)
        t0 = pl.multiple_of((it % (tm // PEER_RT)) * PEER_RT, PEER_RT)
        tok = pl.ds(t0, PEER_RT)

        def scores(hp):
            return lax.dot_general(sk_ref[hp], q_s[hp, tok, :], (((1,), (1,)), ((), ())),
                                   precision=lax.Precision.HIGHEST, preferred_element_type=F32)

        s1, s2 = scores(2 * h), scores(2 * h + 1)
        av, rank1 = _top16(s1)
        bv, rank2 = _top16(s2)
        cut_lo, cut_hi, z = _pair_topk(av, bv)
        cut = jnp.zeros_like(s1)
        for r in range(PEER_TOPK):
            src = cut_lo if r < 8 else cut_hi
            cut = jnp.where(rank1 == float(r), src[r % 8:r % 8 + 1, :], cut)
        e1_ref[0, h, :, tok] = jnp.exp(s1 - av[0]) / z
        cut_ref[0, h, :, tok] = cut
        e2_ref[0, h, :, tok] = jnp.exp(s2 - bv[0])
        r2_ref[0, h, :, tok] = rank2
        return carry

    lax.fori_loop(0, PEER_HEADS * (tm // PEER_RT), body, 0)


def peer_route(x3, mod3, shift_chunk, wq_bf16, subkeys, tm=256):
    nseg, seg, d = x3.shape
    tok = lambda s, i: (s, i, 0)
    rshape = jax.ShapeDtypeStruct((nseg, PEER_HEADS, PEER_NKEYS, seg), F32)
    rspec = pl.BlockSpec((1, PEER_HEADS, PEER_NKEYS, tm), lambda s, i: (s, 0, 0, i))
    return pl.pallas_call(
        functools.partial(_peer_route_kernel, tm=tm),
        grid=(nseg, seg // tm),
        in_specs=[pl.BlockSpec((1, tm, d), tok),
                  pl.BlockSpec((1, 1, d), lambda s, i: (s, 0, shift_chunk)),
                  pl.BlockSpec((1, 1, d), lambda s, i: (s, 0, shift_chunk + 1)),
                  pl.BlockSpec(wq_bf16.shape, lambda s, i: (0, 0)),
                  pl.BlockSpec((2 * PEER_HEADS, PEER_NKEYS, PEER_HALF), lambda s, i: (0, 0, 0))],
        out_specs=[pl.BlockSpec((1, tm, d), tok), rspec, rspec, rspec, rspec],
        out_shape=[jax.ShapeDtypeStruct((nseg, seg, d), BF16), rshape, rshape, rshape, rshape],
        scratch_shapes=[pltpu.VMEM((2 * PEER_HEADS, tm, PEER_HALF), F32)],
        compiler_params=_params(("arbitrary", "arbitrary")),
        name="peer_route",
    )(x3, mod3, mod3, wq_bf16, subkeys.reshape(2 * PEER_HEADS, PEER_NKEYS, PEER_HALF))


PEER_CE = 1024


def _gelu_tanh(x):
    return 0.5 * x * (1.0 + jnp.tanh(0.7978845608028654 * (x + 0.044715 * x * x * x)))


def _peer_dense_kernel(xm_ref, u_ref, vt_ref, e1_ref, cut_ref, e2_ref, r2_ref, x_ref, gate_ref, g_ref, b_ref,
                       o_ref, acc_s, at_s, w_s, *, tm):
    e = pl.program_id(2)

    @pl.when(e == 0)
    def _():
        acc_s[...] = jnp.zeros_like(acc_s)

    at_s[...] = _dot(u_ref[...], xm_ref[0], ((1,), (1,)))

    for ii in range(PEER_CE // PEER_NKEYS):
        rows = slice(ii * PEER_NKEYS, (ii + 1) * PEER_NKEYS)
        for tt in range(tm // PEER_RT):
            tok = slice(tt * PEER_RT, (tt + 1) * PEER_RT)
            gmat = jnp.zeros((PEER_NKEYS, PEER_RT), F32)
            for h in range(PEER_HEADS):
                e1 = e1_ref[0, h, ii:ii + 1, tok]
                cut = cut_ref[0, h, ii:ii + 1, tok]
                gmat = gmat + e1 * jnp.where(r2_ref[0, h, :, tok] < cut, e2_ref[0, h, :, tok], 0.0)
            w_s[rows, tok] = (gmat * _gelu_tanh(at_s[rows, tok])).astype(BF16)
    acc_s[...] += _dot(vt_ref[...], w_s[...])

    @pl.when(e == pl.num_programs(2) - 1)
    def _():
        z = DEEPNORM_ALPHA * x_ref[0] + gate_ref[0] * acc_s[...].T
        o_ref[0] = _layer_norm_rows(z, g_ref[...], b_ref[...])


def peer_dense(xm, u_bf16, vt_bf16, e1, cut, e2, r2, x3, mod3, gate_chunk, ln_g, ln_b, tm=512):
    nseg, seg, d = x3.shape
    ne = u_bf16.shape[0]
    nb = PEER_CE // PEER_NKEYS
    tok = lambda s, i, e: (s, i, 0)
    chunk = pl.BlockSpec((1, PEER_HEADS, nb, tm), lambda s, i, e: (s, 0, e, i))
    full = pl.BlockSpec((1, PEER_HEADS, PEER_NKEYS, tm), lambda s, i, e: (s, 0, 0, i))
    return pl.pallas_call(
        functools.partial(_peer_dense_kernel, tm=tm),
        grid=(nseg, seg // tm, ne // PEER_CE),
        in_specs=[pl.BlockSpec((1, tm, d), tok),
                  pl.BlockSpec((PEER_CE, d), lambda s, i, e: (e, 0)),
                  pl.BlockSpec((d, PEER_CE), lambda s, i, e: (0, e)),
                  chunk, chunk, full, full,
                  pl.BlockSpec((1, tm, d), tok),
                  pl.BlockSpec((1, 1, d), lambda s, i, e: (s, 0, gate_chunk)),
                  pl.BlockSpec((1, d), lambda s, i, e: (0, 0)),
                  pl.BlockSpec((1, d), lambda s, i, e: (0, 0))],
        out_specs=pl.BlockSpec((1, tm, d), tok),
        out_shape=jax.ShapeDtypeStruct((nseg, seg, d), F32),
        scratch_shapes=[pltpu.VMEM((d, tm), F32), pltpu.VMEM((PEER_CE, tm), F32), pltpu.VMEM((PEER_CE, tm), BF16)],
        compiler_params=_params(("arbitrary", "arbitrary", "arbitrary")),
        name="peer_dense",
    )(xm, u_bf16, vt_bf16, e1, cut, e2, r2, x3, mod3, ln_g.reshape(1, d), ln_b.reshape(1, d))


def peer_layer(x3, mod3, wq, subkeys, u_tab, v_tab, ln_g, ln_b):
    xm, e1, cut, e2, r2 = peer_route(x3, mod3, 3, wq.astype(BF16), subkeys)
    return peer_dense(xm, u_tab.astype(BF16), v_tab.T.astype(BF16), e1, cut, e2, r2, x3, mod3, 5, ln_g, ln_b)


def _pad_cols(w, n):
    return jnp.pad(w, ((0, 0), (0, n - w.shape[1])))


def _stream(prompt_part, sample_part):
    return jnp.concatenate([prompt_part.reshape(1, -1, prompt_part.shape[-1]), sample_part], axis=0)


def _head_major(a, heads):
    b, t, _ = a.shape
    return jnp.transpose(a.reshape(b, t, heads, -1), (0, 2, 1, 3))


def _token_major(a):
    b, h, t, dh = a.shape
    return jnp.transpose(a, (0, 2, 1, 3)).reshape(b, t, h * dh)


MLSTM_CHUNK = 128
GLA_CHUNK = 32
NPROJ = 3200


def mlstm_layer(x3, mod3, bp, lp, st_c, st_n, st_m, w_in, b_gate, norm_w, w_out, ln_g, ln_b):
    nseg, seg, _ = x3.shape
    bs = nseg - 1
    p = mod_matmul(x3, mod3, 0, _pad_cols(w_in, NPROJ).astype(BF16))
    graw = p[:, :, 3072:3088]
    gp = graw[0].reshape(bp, lp, 16)
    zc = jnp.zeros((bp, 8, M_DK, M_DV), F32)
    zn = jnp.zeros((bp, 8, M_DK), F32)
    hfp, hbp, c_new, n_new, m_new = mlstm_scan(p.reshape(nseg * bp, lp, NPROJ), 0, bp, lp, gp,
                                               jnp.swapaxes(gp, 1, 2), b_gate, zc, zn, zn, min(MLSTM_CHUNK, lp))
    gs = graw[1:]
    hfs, hbs, _, _, _ = mlstm_scan(p, 1, bs, seg, gs, jnp.swapaxes(gs, 1, 2), b_gate,
                                   st_c.reshape(bs, 8, M_DK, M_DV), st_n.reshape(bs, 8, M_DK),
                                   jnp.broadcast_to(st_m.reshape(bs, 8, 1), (bs, 8, M_DK)), MLSTM_CHUNK)
    x3 = outproj_ln("mlstm", (_stream(hfp, hfs), _stream(hbp, hbs)), x3, mod3, 2, w_out.astype(BF16), ln_g, ln_b,
                    norm_w=norm_w, og=p, og_col=2)
    return (x3, c_new.reshape(bp, 2, M_HEADS, M_DK, M_DV), n_new.reshape(bp, 2, M_HEADS, M_DK),
            m_new[:, :, 0].reshape(bp, 2, M_HEADS))


def gla_layer(x3, mod3, bp, lp, st_s, w_in, w_gate2, b_gate2, norm_w, w_out, ln_g, ln_b):
    nseg, seg, _ = x3.shape
    bs = nseg - 1
    p = mod_matmul(x3, mod3, 0, _pad_cols(w_in, NPROJ).astype(BF16))
    gr = p[:, :, 3072:3104]
    zs = jnp.zeros((bp, 8, G_DV, G_DK), F32)
    ofp, obp, s_new = gla_scan(p.reshape(nseg * bp, lp, NPROJ), 0, bp, lp, gr[0].reshape(bp, lp, 32),
                               w_gate2, b_gate2, zs, GLA_CHUNK)
    s0t = jnp.swapaxes(st_s.reshape(bs, 8, G_DK, G_DV), -1, -2)
    ofs, obs, _ = gla_scan(p, 1, bs, seg, gr[1:], w_gate2, b_gate2, s0t, GLA_CHUNK)
    x3 = outproj_ln("gla", (_stream(ofp, ofs), _stream(obp, obs)), x3, mod3, 2, w_out.astype(BF16), ln_g, ln_b,
                    norm_w=jnp.tile(norm_w, G_HEADS), og=p, og_col=2)
    return x3, jnp.swapaxes(s_new, -1, -2).reshape(bp, 2, G_HEADS, G_DK, G_DV)


def na_layer(x3, mod3, bp, lp, cache_k, cache_v, w_in, rpb, w_out, ln_g, ln_b):
    nseg, seg, _ = x3.shape
    bs = nseg - 1
    hd = NA_HEADS * NA_HD
    p = mod_matmul(x3, mod3, 0, w_in.astype(BF16))
    pp = p[0].reshape(bp, lp, 3 * hd)
    hm = lambda a: _head_major(a, NA_HEADS).astype(BF16)
    yp = attention(hm(pp[..., :hd]), hm(pp[..., hd:2 * hd]), hm(pp[..., 2 * hd:]), lp)
    ps = p[1:]
    ys = na_attention(hm(ps[..., :hd]), hm(ps[..., hd:2 * hd]), hm(ps[..., 2 * hd:]),
                      hm(cache_k.reshape(bs, -1, hd)), hm(cache_v.reshape(bs, -1, hd)), na_bias_table(rpb))
    x3 = outproj_ln("plain", _stream(_token_major(yp), _token_major(ys)), x3, mod3, 2, w_out.astype(BF16), ln_g, ln_b)
    return (x3, pp[..., hd:2 * hd].reshape(bp, lp, NA_HEADS, NA_HD), pp[..., 2 * hd:].reshape(bp, lp, NA_HEADS, NA_HD))


def _rope_rotated_cols(w):
    q = MLA_ROPE // 4
    return jnp.concatenate([-w[..., q:2 * q], w[..., :q], -w[..., 3 * q:], w[..., 2 * q:3 * q]], axis=-1)


def _rope_tables(ts):
    ra = MLA_ROPE // 2
    t = np.arange(ts)
    inv = 1.0 / (ROPE_BASE ** (np.arange(0, ra, 2, dtype=np.float32) / ra))
    ang_r = (t // GRID_W).astype(np.float32)[:, None] * inv[None, :]
    ang_c = (t % GRID_W).astype(np.float32)[:, None] * inv[None, :]
    ang = np.concatenate([ang_r, ang_r, ang_c, ang_c], axis=-1).astype(np.float32)
    return jnp.cos(jnp.asarray(ang)), jnp.sin(jnp.asarray(ang))


def mla_layer(x3, mod3, bp, lp, cache_ckv, cache_kpe, w_in, q_norm, w_qup, kv_norm, w_kvup, w_out, ln_g, ln_b):
    nseg, seg, _ = x3.shape
    bs = nseg - 1
    nq = MLA_Q_LORA + MLA_KV_LORA
    w_ext = jnp.concatenate([w_in, _rope_rotated_cols(w_in[:, nq:])], axis=1)
    p = mod_matmul(x3, mod3, 0, _pad_cols(w_ext, 896).astype(BF16))
    cos_t, sin_t = _rope_tables(seg)
    cos3 = jnp.concatenate([jnp.ones((1, seg, MLA_ROPE), F32), jnp.broadcast_to(cos_t, (bs, seg, MLA_ROPE))], 0)
    sin3 = jnp.concatenate([jnp.zeros((1, seg, MLA_ROPE), F32), jnp.broadcast_to(sin_t, (bs, seg, MLA_ROPE))], 0)
    wq = w_qup.reshape(MLA_Q_LORA, MLA_HEADS, MLA_NOPE + MLA_ROPE)
    wq_rope = wq[:, :, MLA_NOPE:]
    w_q3 = jnp.concatenate([wq[:, :, :MLA_NOPE].reshape(MLA_Q_LORA, -1), wq_rope.reshape(MLA_Q_LORA, -1),
                            _rope_rotated_cols(wq_rope).reshape(MLA_Q_LORA, -1)], axis=1).astype(BF16)
    q_all = mla_q(p, q_norm, w_q3, jnp.tile(cos3, (1, 1, MLA_HEADS)), jnp.tile(sin3, (1, 1, MLA_HEADS)))
    wkv = w_kvup.reshape(MLA_KV_LORA, MLA_HEADS, MLA_NOPE + MLA_VD)
    w_kv2 = jnp.concatenate([wkv[:, :, :MLA_NOPE].reshape(MLA_KV_LORA, -1),
                             wkv[:, :, MLA_NOPE:].reshape(MLA_KV_LORA, -1)], axis=1).astype(BF16)
    ckvn, kpe, kv = mla_kv(p, kv_norm, w_kv2, cos3, sin3)
    kvc = matmul(cache_ckv.reshape(-1, MLA_KV_LORA), w_kv2, 512).reshape(bs, -1, w_kv2.shape[1])
    nn = MLA_HEADS * MLA_NOPE

    def heads(q_rows, kv_rows, kpe_rows):
        b, t, _ = q_rows.shape
        tk = kv_rows.shape[1]
        qh = jnp.concatenate([q_rows[..., :nn].reshape(b, t, MLA_HEADS, MLA_NOPE),
                              q_rows[..., nn:].reshape(b, t, MLA_HEADS, MLA_ROPE)], -1)
        kh = jnp.concatenate([kv_rows[..., :nn].reshape(b, tk, MLA_HEADS, MLA_NOPE),
                              jnp.broadcast_to(kpe_rows[:, :, None, :], (b, tk, MLA_HEADS, MLA_ROPE))], -1)
        vh = kv_rows[..., nn:].reshape(b, tk, MLA_HEADS, MLA_VD)
        tr = lambda a: jnp.transpose(a, (0, 2, 1, 3)).astype(BF16)
        return tr(qh), tr(kh), tr(vh)

    yp = attention(*heads(q_all[0].reshape(bp, lp, -1), kv[0].reshape(bp, lp, -1), kpe[0].reshape(bp, lp, -1)), lp)
    ys = attention(*heads(q_all[1:], jnp.concatenate([kv[1:], kvc], 1), jnp.concatenate([kpe[1:], cache_kpe], 1)), 256)
    x3 = outproj_ln("plain", _stream(_token_major(yp), _token_major(ys)), x3, mod3, 2, w_out.astype(BF16), ln_g, ln_b)
    return x3, ckvn[0].reshape(bp, lp, MLA_KV_LORA), kpe[0].reshape(bp, lp, MLA_ROPE)


def kernel(x_prompt, x_sample, c, c_ctx, state_mlstm_C, state_mlstm_n, state_mlstm_m, state_gla_S, cache_na_k, cache_na_v, cache_mla_ckv, cache_mla_kpe, ada_w, ada_b, ln_mix_g, ln_mix_b, ln_ffn_g, ln_ffn_b, mlstm_w_in, mlstm_b_gate, mlstm_norm_w, mlstm_w_out, gla_w_in, gla_w_gate2, gla_b_gate2, gla_norm_w, gla_w_out, na_w_in, na_rpb, na_w_out, mla_w_in, mla_q_norm, mla_w_qup, mla_kv_norm, mla_w_kvup, mla_w_out, peer_w_q, peer_subkeys, peer_u, peer_v):
    bp, lp, d = x_prompt.shape
    bs, ts, _ = x_sample.shape
    assert bp * lp == ts and bs + 1 <= 8
    x3 = _stream(x_prompt, x_sample)
    cond8 = jnp.zeros((8, d), F32).at[0].set(c_ctx).at[1:1 + bs].set(c)
    mods = adaln_all(cond8, ada_w, ada_b)
    outs = {}
    for l in range(DEPTH):
        mod3 = mods[l].reshape(8, 1, ADA_CHUNKS * d)
        kind = l % 4
        if kind == 0:
            x3, outs["C"], outs["n"], outs["m"] = mlstm_layer(
                x3, mod3, bp, lp, state_mlstm_C, state_mlstm_n, state_mlstm_m, mlstm_w_in, mlstm_b_gate,
                mlstm_norm_w, mlstm_w_out, ln_mix_g[l], ln_mix_b[l])
        elif kind == 1:
            x3, outs["S"] = gla_layer(x3, mod3, bp, lp, state_gla_S, gla_w_in, gla_w_gate2, gla_b_gate2,
                                      gla_norm_w, gla_w_out, ln_mix_g[l], ln_mix_b[l])
        elif kind == 2:
            x3, outs["nk"], outs["nv"] = na_layer(x3, mod3, bp, lp, cache_na_k, cache_na_v, na_w_in, na_rpb,
                                                  na_w_out, ln_mix_g[l], ln_mix_b[l])
        else:
            x3, outs["ckv"], outs["kpe"] = mla_layer(x3, mod3, bp, lp, cache_mla_ckv, cache_mla_kpe, mla_w_in,
                                                     mla_q_norm, mla_w_qup, mla_kv_norm, mla_w_kvup, mla_w_out,
                                                     ln_mix_g[l], ln_mix_b[l])
        x3 = peer_layer(x3, mod3, peer_w_q[l], peer_subkeys[l], peer_u[l], peer_v[l], ln_ffn_g[l], ln_ffn_b[l])
    return (x3[0].reshape(bp, lp, d), x3[1:], outs["C"], outs["n"], outs["m"], outs["S"], outs["nk"], outs["nv"],
            outs["ckv"], outs["kpe"])
```

```python
import functools

import numpy as np
import jax
import jax.numpy as jnp
from jax import lax
from jax.experimental import pallas as pl
from jax.experimental.pallas import tpu as pltpu

D_MODEL = 1024
DEPTH = 4
GRID_W = 64
DEEPNORM_ALPHA = (2.0 * DEPTH) ** 0.25
ADA_CHUNKS = 6
NORM_EPS = 1e-5
SEG = 4096
NSEG = 3

M_HEADS, M_DK, M_DV = 4, 128, 256
G_HEADS, G_DK, G_DV = 4, 128, 256
G_GATE_RANK = 16
G_GATE_NORM = 16.0
NA_HEADS, NA_HD, NA_ROWS, NA_COLS = 16, 64, 8, 16
MLA_HEADS, MLA_Q_LORA, MLA_KV_LORA, MLA_NOPE, MLA_ROPE, MLA_VD = 16, 512, 256, 64, 32, 64
ROPE_BASE = 10000.0
PEER_HEADS, PEER_NKEYS, PEER_HALF, PEER_TOPK = 8, 128, 128, 16

V7X_VMEM_LIMIT = 56 * 1024 * 1024
F32 = jnp.float32
BF16 = jnp.bfloat16
NEG_INF = float("-inf")


def _params(sem, vmem=V7X_VMEM_LIMIT):
    return pltpu.CompilerParams(dimension_semantics=sem, vmem_limit_bytes=vmem)


def _dot(a, b, dims=((1,), (0,))):
    return lax.dot_general(a, b, (dims, ((), ())), preferred_element_type=F32)


def _split3(a):
    hi = a.astype(BF16)
    r1 = a - hi.astype(F32)
    mid = r1.astype(BF16)
    lo = (r1 - mid.astype(F32)).astype(BF16)
    return hi, mid, lo


def _dot_exact_lhs(m01, a):
    hi, mid, lo = _split3(a)
    return _dot(m01, hi) + _dot(m01, mid) + _dot(m01, lo)


def _dot_exact_rhs(a, m01):
    hi, mid, lo = _split3(a)
    return _dot(hi, m01) + _dot(mid, m01) + _dot(lo, m01)


def _log_sigmoid(x):
    return jnp.minimum(x, 0.0) - jnp.log(1.0 + jnp.exp(-jnp.abs(x)))


def _sigmoid(x):
    return 1.0 / (1.0 + jnp.exp(-x))


def _adaln_kernel(c_ref, w_ref, b_ref, o_ref):
    cv = c_ref[...]
    a = cv * _sigmoid(cv)
    o_ref[0] = lax.dot_general(a, w_ref[0], (((1,), (0,)), ((), ())), precision=lax.Precision.HIGHEST,
                               preferred_element_type=F32) + b_ref[0]


def adaln_all(cond8, ada_w, ada_b):
    tn = 1024
    n = ada_w.shape[-1]
    return pl.pallas_call(
        _adaln_kernel,
        grid=(DEPTH, n // tn),
        in_specs=[pl.BlockSpec((8, D_MODEL), lambda l, j: (0, 0)),
                  pl.BlockSpec((1, D_MODEL, tn), lambda l, j: (l, 0, j)),
                  pl.BlockSpec((1, 1, tn), lambda l, j: (l, 0, j))],
        out_specs=pl.BlockSpec((1, 8, tn), lambda l, j: (l, 0, j)),
        out_shape=jax.ShapeDtypeStruct((DEPTH, 8, n), F32),
        compiler_params=_params(("arbitrary", "arbitrary")),
        name="adaln",
    )(cond8, ada_w, ada_b.reshape(DEPTH, 1, n))


def _modmm_kernel(x_ref, sh_ref, sc_ref, w_ref, o_ref, xm_ref):
    @pl.when(pl.program_id(2) == 0)
    def _():
        xm_ref[...] = (x_ref[0] * (1.0 + sc_ref[0]) + sh_ref[0]).astype(BF16)

    o_ref[0] = _dot(xm_ref[...], w_ref[...]).astype(o_ref.dtype)


def mod_matmul(x3, mod3, shift_chunk, w_bf16, tm=512, tn=None, out_dtype=F32):
    nseg, seg, d = x3.shape
    n = w_bf16.shape[1]
    tn = n if tn is None else tn
    return pl.pallas_call(
        _modmm_kernel,
        grid=(nseg, seg // tm, n // tn),
        in_specs=[pl.BlockSpec((1, tm, d), lambda s, i, j: (s, i, 0)),
                  pl.BlockSpec((1, 1, d), lambda s, i, j: (s, 0, shift_chunk)),
                  pl.BlockSpec((1, 1, d), lambda s, i, j: (s, 0, shift_chunk + 1)),
                  pl.BlockSpec((d, tn), lambda s, i, j: (0, j))],
        out_specs=pl.BlockSpec((1, tm, tn), lambda s, i, j: (s, i, j)),
        out_shape=jax.ShapeDtypeStruct((nseg, seg, n), out_dtype),
        scratch_shapes=[pltpu.VMEM((tm, d), BF16)],
        compiler_params=_params(("arbitrary", "arbitrary", "arbitrary")),
        name="mod_matmul",
    )(x3, mod3, mod3, w_bf16)


def _layer_norm_rows(y, g, b):
    mu = jnp.mean(y, axis=-1, keepdims=True)
    yc = y - mu
    var = jnp.mean(yc * yc, axis=-1, keepdims=True)
    return yc * lax.rsqrt(var + NORM_EPS) * g + b


def _outproj_kernel(*refs, mode):
    if mode == "plain":
        y_ref, x_ref, gate_ref, w_ref, g_ref, b_ref, o_ref = refs
        yin = y_ref[0].astype(BF16)
    else:
        ya_ref, yb_ref, og_ref, nw_ref, x_ref, gate_ref, w_ref, g_ref, b_ref, o_ref = refs
        hs = ya_ref[0] + yb_ref[0]
        og = og_ref[0]
        parts = []
        for h in range(4):
            seg = hs[:, h * 256:(h + 1) * 256]
            nw = nw_ref[:, h * 256:(h + 1) * 256]
            if mode == "mlstm":
                mu = jnp.mean(seg, axis=-1, keepdims=True)
                sc = seg - mu
                var = jnp.mean(sc * sc, axis=-1, keepdims=True)
                parts.append(sc * lax.rsqrt(var + NORM_EPS) * nw)
            else:
                ms = jnp.mean(seg * seg, axis=-1, keepdims=True)
                parts.append(seg * lax.rsqrt(ms + NORM_EPS) * nw)
        hn = jnp.concatenate(parts, axis=-1)
        act = _sigmoid(og) if mode == "mlstm" else og * _sigmoid(og)
        yin = (act * hn).astype(BF16)
    y = _dot(yin, w_ref[...])
    z = DEEPNORM_ALPHA * x_ref[0] + gate_ref[0] * y
    o_ref[0] = _layer_norm_rows(z, g_ref[...], b_ref[...])


def outproj_ln(mode, ys, x3, mod3, gate_chunk, w_bf16, ln_g, ln_b, norm_w=None, og=None, og_col=0, tm=512):
    nseg, seg, d = x3.shape
    k = w_bf16.shape[0]
    tok = lambda s, i: (s, i, 0)
    if mode == "plain":
        args = [ys]
        specs = [pl.BlockSpec((1, tm, k), tok)]
    else:
        args = [ys[0], ys[1], og, norm_w.reshape(1, k)]
        specs = [pl.BlockSpec((1, tm, k), tok), pl.BlockSpec((1, tm, k), tok),
                 pl.BlockSpec((1, tm, k), lambda s, i: (s, i, og_col)),
                 pl.BlockSpec((1, k), lambda s, i: (0, 0))]
    args += [x3, mod3, w_bf16, ln_g.reshape(1, d), ln_b.reshape(1, d)]
    specs += [pl.BlockSpec((1, tm, d), tok),
              pl.BlockSpec((1, 1, d), lambda s, i: (s, 0, gate_chunk)),
              pl.BlockSpec((k, d), lambda s, i: (0, 0)),
              pl.BlockSpec((1, d), lambda s, i: (0, 0)),
              pl.BlockSpec((1, d), lambda s, i: (0, 0))]
    return pl.pallas_call(
        functools.partial(_outproj_kernel, mode=mode),
        grid=(nseg, seg // tm),
        in_specs=specs,
        out_specs=pl.BlockSpec((1, tm, d), tok),
        out_shape=jax.ShapeDtypeStruct((nseg, seg, d), F32),
        compiler_params=_params(("arbitrary", "arbitrary")),
        name="outproj_ln_" + mode,
    )(*args)


def _tri(n, lower):
    r = lax.broadcasted_iota(jnp.int32, (n, n), 0)
    c = lax.broadcasted_iota(jnp.int32, (n, n), 1)
    return (c <= r) if lower else (c >= r)


def _mlstm_kernel(pf_ref, pb_ref, gf_ref, gb_ref, gtf_ref, gtb_ref, bias_ref, biast_ref,
                  c0_ref, n0_ref, m0_ref, hf_ref, hb_ref, co_ref, no_ref, mo_ref,
                  c_s, n_s, m_s, *, L):
    c = pl.program_id(1)

    @pl.when(c == 0)
    def _():
        c_s[...] = c0_ref[0]
        n_s[...] = n0_ref[0]
        m_s[...] = m0_ref[0]

    for d in range(2):
        p_ref, g_ref, gt_ref, h_ref = ((pf_ref, gf_ref, gtf_ref, hf_ref) if d == 0
                                       else (pb_ref, gb_ref, gtb_ref, hb_ref))
        mask = _tri(L, lower=(d == 0))
        mcol = mask.astype(BF16)
        mrow = _tri(L, lower=(d != 0)).astype(BF16)
        g = g_ref[0] + bias_ref[...]
        gt = gt_ref[0] + biast_ref[...]
        li_c = g[:, d * 8:d * 8 + 4]
        lf_c = _log_sigmoid(g[:, d * 8 + 4:d * 8 + 8])
        li_r = gt[d * 8:d * 8 + 4, :]
        lf_r = _log_sigmoid(gt[d * 8 + 4:d * 8 + 8, :])
        b_c = _dot_exact_lhs(mcol, lf_c)
        b_r = _dot_exact_rhs(lf_r, mrow)
        last = L - 1 if d == 0 else 0
        for h in range(M_HEADS):
            u = d * M_HEADS + h
            q = p_ref[0, :, h * M_DK:(h + 1) * M_DK]
            k = p_ref[0, :, 512 + h * M_DK:512 + (h + 1) * M_DK] * (M_DK ** -0.5)
            v = p_ref[0, :, 1024 + h * M_DV:1024 + (h + 1) * M_DV].astype(BF16)
            qb = q.astype(BF16)
            bc, br = b_c[:, h:h + 1], b_r[h:h + 1, :]
            lic, lir = li_c[:, h:h + 1], li_r[h:h + 1, :]
            m_prev = m_s[u:u + 1, 0:1]
            dmat = jnp.where(mask, bc - br + lir, NEG_INF)
            inter = bc + m_prev
            mt = jnp.maximum(inter, jnp.max(dmat, axis=-1, keepdims=True))
            smat = _dot(qb, k.astype(BF16), ((1,), (1,))) * jnp.exp(dmat - mt)
            ei = jnp.exp(inter - mt)
            cmat = c_s[u]
            num = _dot(smat.astype(BF16), v) + ei * _dot(qb, cmat.astype(BF16))
            nrow = n_s[u:u + 1, :]
            den = jnp.sum(smat, axis=-1, keepdims=True) + ei * jnp.sum(q * nrow, axis=-1, keepdims=True)
            h_ref[0, :, h * M_DV:(h + 1) * M_DV] = num / jnp.maximum(jnp.abs(den), jnp.exp(-mt))
            tot = br[:, last:last + 1]
            g_c = tot - bc + lic
            g_r = tot - br + lir
            m_new = jnp.maximum(tot + m_prev, jnp.max(g_r, axis=-1, keepdims=True))
            kw = k * jnp.exp(g_c - m_new)
            dec = jnp.exp(tot + m_prev - m_new)
            c_s[u] = dec * cmat + _dot(kw.astype(BF16), v, ((0,), (0,)))
            n_s[u:u + 1, :] = dec * nrow + jnp.sum(kw, axis=0, keepdims=True)
            m_s[u:u + 1, :] = jnp.broadcast_to(m_new, (1, 128))

    @pl.when(c == pl.num_programs(1) - 1)
    def _():
        co_ref[0] = c_s[...]
        no_ref[0] = n_s[...]
        mo_ref[0] = m_s[...]


def mlstm_scan(p, b0, nb, t, g, gt, bias, c0, n0, m0, L):
    nc = t // L
    hshape = jax.ShapeDtypeStruct((nb, t, M_HEADS * M_DV), F32)
    fwd = lambda b, c: (b + b0, c, 0)
    bwd = lambda b, c: (b + b0, nc - 1 - c, 0)
    st4 = lambda b, c: (b, 0, 0, 0)
    st3 = lambda b, c: (b, 0, 0)
    return pl.pallas_call(
        functools.partial(_mlstm_kernel, L=L),
        grid=(nb, nc),
        in_specs=[pl.BlockSpec((1, L, 2048), fwd), pl.BlockSpec((1, L, 2048), bwd),
                  pl.BlockSpec((1, L, 16), lambda b, c: (b, c, 0)),
                  pl.BlockSpec((1, L, 16), lambda b, c: (b, nc - 1 - c, 0)),
                  pl.BlockSpec((1, 16, L), lambda b, c: (b, 0, c)),
                  pl.BlockSpec((1, 16, L), lambda b, c: (b, 0, nc - 1 - c)),
                  pl.BlockSpec((1, 16), lambda b, c: (0, 0)),
                  pl.BlockSpec((16, 1), lambda b, c: (0, 0)),
                  pl.BlockSpec((1, 8, M_DK, M_DV), st4),
                  pl.BlockSpec((1, 8, M_DK), st3),
                  pl.BlockSpec((1, 8, M_DK), st3)],
        out_specs=[pl.BlockSpec((1, L, 1024), lambda b, c: (b, c, 0)),
                   pl.BlockSpec((1, L, 1024), lambda b, c: (b, nc - 1 - c, 0)),
                   pl.BlockSpec((1, 8, M_DK, M_DV), st4),
                   pl.BlockSpec((1, 8, M_DK), st3),
                   pl.BlockSpec((1, 8, M_DK), st3)],
        out_shape=[hshape, hshape,
                   jax.ShapeDtypeStruct((nb, 8, M_DK, M_DV), F32),
                   jax.ShapeDtypeStruct((nb, 8, M_DK), F32),
                   jax.ShapeDtypeStruct((nb, 8, M_DK), F32)],
        scratch_shapes=[pltpu.VMEM((8, M_DK, M_DV), F32), pltpu.VMEM((8, M_DK), F32),
                        pltpu.VMEM((8, M_DK), F32)],
        compiler_params=_params(("arbitrary", "arbitrary")),
        name="mlstm_scan",
    )(p, p, g, g, gt, gt, bias.reshape(1, 16), bias.reshape(16, 1), c0, n0, m0)


def _gla_kernel(pf_ref, pb_ref, gf_ref, gb_ref, w2_ref, b2_ref, s0_ref, of_ref, ob_ref, so_ref, s_s, *, L):
    c = pl.program_id(1)

    @pl.when(c == 0)
    def _():
        s_s[...] = s0_ref[0]

    for d in range(2):
        p_ref, g_ref, o_ref = (pf_ref, gf_ref, of_ref) if d == 0 else (pb_ref, gb_ref, ob_ref)
        mask = _tri(L, lower=(d == 0))
        mcol = mask.astype(BF16)
        gr = g_ref[0][:, d * G_GATE_RANK:(d + 1) * G_GATE_RANK]
        pre = lax.dot_general(gr, w2_ref[d], (((1,), (0,)), ((), ())), precision=lax.Precision.HIGHEST,
                              preferred_element_type=F32) + b2_ref[d]
        la = _log_sigmoid(pre) * (1.0 / G_GATE_NORM)
        bc_all = _dot_exact_lhs(mcol, la)
        last = L - 1 if d == 0 else 0
        for h in range(G_HEADS):
            u = d * G_HEADS + h
            q = p_ref[0, :, h * G_DK:(h + 1) * G_DK] * (G_DK ** -0.5)
            k = p_ref[0, :, 512 + h * G_DK:512 + (h + 1) * G_DK]
            v = p_ref[0, :, 1024 + h * G_DV:1024 + (h + 1) * G_DV].astype(BF16)
            bc = bc_all[:, h * G_DK:(h + 1) * G_DK]
            qd = (q * jnp.exp(bc)).astype(BF16)
            kd = (k * jnp.exp(-bc)).astype(BF16)
            a = jnp.where(mask, _dot(qd, kd, ((1,), (1,))), 0.0)
            st = s_s[u]
            o_ref[0, :, h * G_DV:(h + 1) * G_DV] = (_dot(a.astype(BF16), v)
                                                    + _dot(qd, st.astype(BF16), ((1,), (1,))))
            bl = bc[last:last + 1, :]
            kl = (k * jnp.exp(bl - bc)).astype(BF16)
            s_s[u] = st * jnp.exp(bl) + _dot(v, kl, ((0,), (0,)))

    @pl.when(c == pl.num_programs(1) - 1)
    def _():
        so_ref[0] = s_s[...]


def gla_scan(p, b0, nb, t, gr, w2, b2, s0t, L):
    nc = t // L
    oshape = jax.ShapeDtypeStruct((nb, t, G_HEADS * G_DV), F32)
    st4 = lambda b, c: (b, 0, 0, 0)
    return pl.pallas_call(
        functools.partial(_gla_kernel, L=L),
        grid=(nb, nc),
        in_specs=[pl.BlockSpec((1, L, 2048), lambda b, c: (b + b0, c, 0)),
                  pl.BlockSpec((1, L, 2048), lambda b, c: (b + b0, nc - 1 - c, 0)),
                  pl.BlockSpec((1, L, 32), lambda b, c: (b, c, 0)),
                  pl.BlockSpec((1, L, 32), lambda b, c: (b, nc - 1 - c, 0)),
                  pl.BlockSpec((2, G_GATE_RANK, 512), lambda b, c: (0, 0, 0)),
                  pl.BlockSpec((2, 1, 512), lambda b, c: (0, 0, 0)),
                  pl.BlockSpec((1, 8, G_DV, G_DK), st4)],
        out_specs=[pl.BlockSpec((1, L, 1024), lambda b, c: (b, c, 0)),
                   pl.BlockSpec((1, L, 1024), lambda b, c: (b, nc - 1 - c, 0)),
                   pl.BlockSpec((1, 8, G_DV, G_DK), st4)],
        out_shape=[oshape, oshape, jax.ShapeDtypeStruct((nb, 8, G_DV, G_DK), F32)],
        scratch_shapes=[pltpu.VMEM((8, G_DV, G_DK), F32)],
        compiler_params=_params(("arbitrary", "arbitrary")),
        name="gla_scan",
    )(p, p, gr, gr, w2, b2.reshape(2, 1, 512), s0t)


def _attn_kernel(q_ref, k_ref, v_ref, o_ref, *, scale):
    s = _dot(q_ref[0, 0], k_ref[0, 0], ((1,), (1,))) * scale
    m = jnp.max(s, axis=-1, keepdims=True)
    p = jnp.exp(s - m)
    l = jnp.sum(p, axis=-1, keepdims=True)
    o_ref[0, 0] = _dot(p.astype(BF16), v_ref[0, 0]) / l


def attention(q, k, v, tq):
    b, h, lq, dq = q.shape
    lk, dv = k.shape[2], v.shape[3]
    return pl.pallas_call(
        functools.partial(_attn_kernel, scale=dq ** -0.5),
        grid=(b, h, lq // tq),
        in_specs=[pl.BlockSpec((1, 1, tq, dq), lambda b, h, i: (b, h, i, 0)),
                  pl.BlockSpec((1, 1, lk, dq), lambda b, h, i: (b, h, 0, 0)),
                  pl.BlockSpec((1, 1, lk, dv), lambda b, h, i: (b, h, 0, 0))],
        out_specs=pl.BlockSpec((1, 1, tq, dv), lambda b, h, i: (b, h, i, 0)),
        out_shape=jax.ShapeDtypeStruct((b, h, lq, dv), F32),
        compiler_params=_params(("arbitrary", "arbitrary", "arbitrary")),
        name="attention",
    )(q, k, v)


NA_RB = 8


def _na_kernel(q_ref, k_ref, v_ref, kc_ref, vc_ref, bias_ref, o_ref, *, rows):
    j = pl.program_id(2)
    scale = NA_HD ** -0.5
    kc, vc = kc_ref[0, 0], vc_ref[0, 0]
    for a in range(NA_RB):
        r = j * NA_RB + a
        start = jnp.clip(r - NA_ROWS // 2, 0, rows - NA_ROWS)
        dr0 = start - r + (NA_ROWS - 1)
        off = pl.multiple_of(start * GRID_W, GRID_W)
        qa = q_ref[0, 0, a * GRID_W:(a + 1) * GRID_W, :]
        kl = k_ref[0, 0, pl.ds(off, NA_ROWS * GRID_W), :]
        vl = v_ref[0, 0, pl.ds(off, NA_ROWS * GRID_W), :]
        s_loc = _dot(qa, kl, ((1,), (1,))) * scale + bias_ref[0, dr0]
        s_ctx = _dot(qa, kc, ((1,), (1,))) * scale
        m = jnp.maximum(jnp.max(s_loc, axis=-1, keepdims=True), jnp.max(s_ctx, axis=-1, keepdims=True))
        p_loc = jnp.exp(s_loc - m)
        p_ctx = jnp.exp(s_ctx - m)
        l = jnp.sum(p_loc, axis=-1, keepdims=True) + jnp.sum(p_ctx, axis=-1, keepdims=True)
        o = _dot(p_loc.astype(BF16), vl) + _dot(p_ctx.astype(BF16), vc)
        o_ref[0, 0, a * GRID_W:(a + 1) * GRID_W, :] = o / l


def na_bias_table(rpb):
    cq = np.arange(GRID_W)[:, None]
    ck = np.arange(GRID_W)[None, :]
    cs = np.clip(cq - NA_COLS // 2, 0, GRID_W - NA_COLS)
    ok = (ck >= cs) & (ck < cs + NA_COLS)
    dc = np.clip(ck - cq, -(NA_COLS - 1), NA_COLS - 1) + (NA_COLS - 1)
    t = jnp.where(ok[None, None], rpb.astype(F32)[:, :, dc], NEG_INF)
    rows = np.arange(NA_ROWS)[:, None] + np.arange(NA_ROWS)[None, :]
    tf = t[:, rows]
    return jnp.transpose(tf, (0, 1, 3, 2, 4)).reshape(NA_HEADS, NA_ROWS, GRID_W, NA_ROWS * GRID_W)


def na_attention(q, k, v, kc, vc, bias):
    b, h, t, dh = q.shape
    lc = kc.shape[2]
    rows = t // GRID_W
    full = lambda b, h, j: (b, h, 0, 0)
    return pl.pallas_call(
        functools.partial(_na_kernel, rows=rows),
        grid=(b, h, rows // NA_RB),
        in_specs=[pl.BlockSpec((1, 1, NA_RB * GRID_W, dh), lambda b, h, j: (b, h, j, 0)),
                  pl.BlockSpec((1, 1, t, dh), full), pl.BlockSpec((1, 1, t, dh), full),
                  pl.BlockSpec((1, 1, lc, dh), full), pl.BlockSpec((1, 1, lc, dh), full),
                  pl.BlockSpec((1, NA_ROWS, GRID_W, NA_ROWS * GRID_W), lambda b, h, j: (h, 0, 0, 0))],
        out_specs=pl.BlockSpec((1, 1, NA_RB * GRID_W, dh), lambda b, h, j: (b, h, j, 0)),
        out_shape=jax.ShapeDtypeStruct((b, h, t, dh), F32),
        compiler_params=_params(("arbitrary", "arbitrary", "arbitrary")),
        name="na_attention",
    )(q, k, v, kc, vc, bias)


def _rms_rows(x, g):
    return x * lax.rsqrt(jnp.mean(x * x, axis=-1, keepdims=True) + NORM_EPS) * g


def _mla_q_kernel(cq_ref, g_ref, w_ref, cos_ref, sin_ref, o_ref):
    r = _dot(_rms_rows(cq_ref[0], g_ref[...]).astype(BF16), w_ref[...])
    nn = MLA_HEADS * MLA_NOPE
    nr = MLA_HEADS * MLA_ROPE
    o_ref[0, :, :nn] = r[:, :nn]
    o_ref[0, :, nn:] = r[:, nn:nn + nr] * cos_ref[0] + r[:, nn + nr:] * sin_ref[0]


def mla_q(p, q_norm, w_q3, cos_q, sin_q, tm=512):
    nseg, seg, _ = p.shape
    nout = MLA_HEADS * (MLA_NOPE + MLA_ROPE)
    nr = MLA_HEADS * MLA_ROPE
    tok = lambda s, i: (s, i, 0)
    return pl.pallas_call(
        _mla_q_kernel,
        grid=(nseg, seg // tm),
        in_specs=[pl.BlockSpec((1, tm, MLA_Q_LORA), tok),
                  pl.BlockSpec((1, MLA_Q_LORA), lambda s, i: (0, 0)),
                  pl.BlockSpec(w_q3.shape, lambda s, i: (0, 0)),
                  pl.BlockSpec((1, tm, nr), tok), pl.BlockSpec((1, tm, nr), tok)],
        out_specs=pl.BlockSpec((1, tm, nout), tok),
        out_shape=jax.ShapeDtypeStruct((nseg, seg, nout), F32),
        compiler_params=_params(("arbitrary", "arbitrary")),
        name="mla_q",
    )(p, q_norm.reshape(1, -1), w_q3, cos_q, sin_q)


def _mla_kv_kernel(ckv_ref, kpe_ref, g_ref, w_ref, cos_ref, sin_ref, ckvn_ref, kpeo_ref, kv_ref):
    cn = _rms_rows(ckv_ref[0], g_ref[...])
    ckvn_ref[0] = cn
    kv_ref[0] = _dot(cn.astype(BF16), w_ref[...])
    kp = kpe_ref[0]
    kpeo_ref[0] = kp[:, :MLA_ROPE] * cos_ref[0] + kp[:, MLA_ROPE:2 * MLA_ROPE] * sin_ref[0]


def mla_kv(p, kv_norm, w_kv, cos_k, sin_k, tm=512):
    nseg, seg, _ = p.shape
    nkv = w_kv.shape[1]
    tok = lambda s, i: (s, i, 0)
    return pl.pallas_call(
        _mla_kv_kernel,
        grid=(nseg, seg // tm),
        in_specs=[pl.BlockSpec((1, tm, MLA_KV_LORA), lambda s, i: (s, i, MLA_Q_LORA // MLA_KV_LORA)),
                  pl.BlockSpec((1, tm, 128), lambda s, i: (s, i, (MLA_Q_LORA + MLA_KV_LORA) // 128)),
                  pl.BlockSpec((1, MLA_KV_LORA), lambda s, i: (0, 0)),
                  pl.BlockSpec(w_kv.shape, lambda s, i: (0, 0)),
                  pl.BlockSpec((1, tm, MLA_ROPE), tok), pl.BlockSpec((1, tm, MLA_ROPE), tok)],
        out_specs=[pl.BlockSpec((1, tm, MLA_KV_LORA), tok), pl.BlockSpec((1, tm, MLA_ROPE), tok),
                   pl.BlockSpec((1, tm, nkv), tok)],
        out_shape=[jax.ShapeDtypeStruct((nseg, seg, MLA_KV_LORA), F32),
                   jax.ShapeDtypeStruct((nseg, seg, MLA_ROPE), F32),
                   jax.ShapeDtypeStruct((nseg, seg, nkv), F32)],
        compiler_params=_params(("arbitrary", "arbitrary")),
        name="mla_kv",
    )(p, p, kv_norm.reshape(1, -1), w_kv, cos_k, sin_k)


def _mm_kernel(a_ref, w_ref, o_ref):
    o_ref[...] = _dot(a_ref[...].astype(BF16), w_ref[...])


def matmul(a, w_bf16, tm):
    m, k = a.shape
    n = w_bf16.shape[1]
    return pl.pallas_call(
        _mm_kernel,
        grid=(m // tm,),
        in_specs=[pl.BlockSpec((tm, k), lambda i: (i, 0)), pl.BlockSpec((k, n), lambda i: (0, 0))],
        out_specs=pl.BlockSpec((tm, n), lambda i: (i, 0)),
        out_shape=jax.ShapeDtypeStruct((m, n), F32),
        compiler_params=_params(("arbitrary",)),
        name="matmul",
    )(a, w_bf16)


PEER_RT = 128
NOT_TOP = 99.0


def _top16(s, exact):
    key = lax.broadcasted_iota(jnp.int32, s.shape, 0).astype(F32)
    rank = jnp.full(s.shape, NOT_TOP, F32)
    vals = []
    for r in range(PEER_TOPK):
        m = jnp.max(s, axis=0, keepdims=True)
        hit = s == m
        if exact:
            hit = key == jnp.min(jnp.where(hit, key, 1e9), axis=0, keepdims=True)
        rank = jnp.where(hit, float(r), rank)
        s = jnp.where(hit, NEG_INF, s)
        vals.append(m)
    return vals, rank


def _pair_topk(av, bv, exact):
    n = av[0].shape[-1]
    a_lo, a_hi = jnp.concatenate(av[:8], 0), jnp.concatenate(av[8:], 0)
    b_lo, b_hi = jnp.concatenate(bv[:8], 0), jnp.concatenate(bv[8:], 0)
    row = lax.broadcasted_iota(jnp.int32, (8, n), 0).astype(F32)

    no_pos = 1e8

    def rows_b(a, b_blk, boff, nvalid):
        ok = row < nvalid
        return jnp.where(ok, av[a] + b_blk, NEG_INF), jnp.where(ok, a * 16.0 + boff + row, no_pos)

    def rows_a(b, a_blk, aoff, lo, hi):
        ok = (row >= lo) & (row < hi)
        return jnp.where(ok, a_blk + bv[b], NEG_INF), jnp.where(ok, (aoff + row) * 16.0 + b, no_pos)

    groups = [rows_b(0, b_lo, 0, 8), rows_b(0, b_hi, 8, 8), rows_b(1, b_lo, 0, 8), rows_b(2, b_lo, 0, 5),
              rows_b(3, b_lo, 0, 4), rows_a(0, a_lo, 0, 4, 8), rows_a(0, a_hi, 8, 0, 8),
              rows_a(1, a_lo, 0, 4, 8), rows_a(2, a_lo, 0, 4, 5)]
    cands = [g[0] for g in groups]
    poss = [g[1] for g in groups]
    sels = [jnp.zeros((8, n), F32) for _ in groups]
    top = av[0] + bv[0]
    z = jnp.zeros((1, n), F32)
    for _ in range(PEER_TOPK):
        m = functools.reduce(jnp.maximum, cands)
        m = jnp.max(m, axis=0, keepdims=True)
        hits = [c == m for c in cands]
        if exact:
            first = functools.reduce(jnp.minimum, [jnp.where(hh, p, 1e9) for hh, p in zip(hits, poss)])
            first = jnp.min(first, axis=0, keepdims=True)
            hits = [p == first for p in poss]
        cands = [jnp.where(hh, NEG_INF, c) for hh, c in zip(hits, cands)]
        sels = [jnp.where(hh, 1.0, s) for hh, s in zip(hits, sels)]
        z = z + jnp.exp(m - top)
    cnt = lambda x: jnp.sum(x, axis=0, keepdims=True)
    cut_lo = sels[5] + sels[7] + sels[8]
    for a, c in enumerate([cnt(sels[0]) + cnt(sels[1]), cnt(sels[2]), cnt(sels[3]), cnt(sels[4])]):
        cut_lo = cut_lo + jnp.where(row == a, c, 0.0)
    return cut_lo, sels[6], z, cnt(cut_lo) + cnt(sels[6])


def _peer_route_kernel(x_ref, sh_ref, sc_ref, wq_ref, sk_ref, xm_ref, e1_ref, cut_ref, e2_ref, r2_ref, q_s, *, tm):
    xm = (x_ref[0] * (1.0 + sc_ref[0]) + sh_ref[0]).astype(BF16)
    xm_ref[0] = xm
    q = _dot(xm, wq_ref[...])
    for hp in range(2 * PEER_HEADS):
        q_s[hp] = q[:, hp * PEER_HALF:(hp + 1) * PEER_HALF]

    def route(h, tok, exact):
        def scores(hp):
            return lax.dot_general(sk_ref[hp], q_s[hp, tok, :], (((1,), (1,)), ((), ())),
                                   precision=lax.Precision.HIGHEST, preferred_element_type=F32)

        s1, s2 = scores(2 * h), scores(2 * h + 1)
        av, rank1 = _top16(s1, exact)
        bv, rank2 = _top16(s2, exact)
        cut_lo, cut_hi, z, nsel = _pair_topk(av, bv, exact)
        cut = jnp.zeros_like(s1)
        for r in range(PEER_TOPK):
            src = cut_lo if r < 8 else cut_hi
            cut = jnp.where(rank1 == float(r), src[r % 8:r % 8 + 1, :], cut)
        e1_ref[0, h, :, tok] = (jnp.exp(s1 - av[0]) / z).astype(BF16)
        cut_ref[0, h, :, tok] = cut.astype(BF16)
        e2_ref[0, h, :, tok] = jnp.exp(s2 - bv[0]).astype(BF16)
        r2_ref[0, h, :, tok] = rank2.astype(BF16)
        ranked = lambda rk: jnp.sum(jnp.where(rk < PEER_TOPK, 1.0, 0.0), axis=0, keepdims=True)
        return ranked(rank1), ranked(rank2), nsel

    def body(h, carry):
        toks = [pl.ds(t0, PEER_RT) for t0 in range(0, tm, PEER_RT)]
        counts = [route(h, tok, exact=False) for tok in toks]
        for tok, cnts in zip(toks, counts):
            bad = functools.reduce(jnp.maximum, [jnp.abs(cn - PEER_TOPK) for cn in cnts])

            @pl.when(jnp.max(bad) > 0.0)
            def _():
                route(h, tok, exact=True)
        return carry

    lax.fori_loop(0, PEER_HEADS, body, 0)


def peer_route(x3, mod3, shift_chunk, wq_bf16, subkeys, tm=256):
    nseg, seg, d = x3.shape
    tok = lambda s, i: (s, i, 0)
    rshape = jax.ShapeDtypeStruct((nseg, PEER_HEADS, PEER_NKEYS, seg), BF16)
    rspec = pl.BlockSpec((1, PEER_HEADS, PEER_NKEYS, tm), lambda s, i: (s, 0, 0, i))
    return pl.pallas_call(
        functools.partial(_peer_route_kernel, tm=tm),
        grid=(nseg, seg // tm),
        in_specs=[pl.BlockSpec((1, tm, d), tok),
                  pl.BlockSpec((1, 1, d), lambda s, i: (s, 0, shift_chunk)),
                  pl.BlockSpec((1, 1, d), lambda s, i: (s, 0, shift_chunk + 1)),
                  pl.BlockSpec(wq_bf16.shape, lambda s, i: (0, 0)),
                  pl.BlockSpec((2 * PEER_HEADS, PEER_NKEYS, PEER_HALF), lambda s, i: (0, 0, 0))],
        out_specs=[pl.BlockSpec((1, tm, d), tok), rspec, rspec, rspec, rspec],
        out_shape=[jax.ShapeDtypeStruct((nseg, seg, d), BF16)] + [rshape] * 4,
        scratch_shapes=[pltpu.VMEM((2 * PEER_HEADS, tm, PEER_HALF), F32)],
        compiler_params=_params(("arbitrary", "arbitrary")),
        name="peer_route",
    )(x3, mod3, mod3, wq_bf16, subkeys.reshape(2 * PEER_HEADS, PEER_NKEYS, PEER_HALF))


PEER_CE = 1024


def _gelu_tanh(x):
    return 0.5 * x * (1.0 + jnp.tanh(0.7978845608028654 * (x + 0.044715 * x * x * x)))


def _peer_dense_kernel(xm_ref, u_ref, vt_ref, e1_ref, cut_ref, e2_ref, r2_ref, x_ref, gate_ref, g_ref, b_ref,
                       o_ref, acc_s, at_s, w_s, *, tm):
    e = pl.program_id(2)

    @pl.when(e == 0)
    def _():
        acc_s[...] = jnp.zeros_like(acc_s)

    at_s[...] = _dot(u_ref[...], xm_ref[0], ((1,), (1,)))

    tile = (PEER_NKEYS, PEER_RT)
    for ip in range(0, PEER_CE // PEER_NKEYS, 2):
        for tt in range(tm // PEER_RT):
            tok = slice(tt * PEER_RT, (tt + 1) * PEER_RT)
            gmats = [jnp.zeros(tile, BF16), jnp.zeros(tile, BF16)]
            for h in range(PEER_HEADS):
                e2 = e2_ref[0, h, :, tok]
                r2 = r2_ref[0, h, :, tok]
                for k in range(2):
                    e1 = jnp.broadcast_to(e1_ref[0, h, ip + k:ip + k + 1, tok], tile)
                    cut = jnp.broadcast_to(cut_ref[0, h, ip + k:ip + k + 1, tok], tile)
                    gmats[k] = gmats[k] + e1 * jnp.where(r2 < cut, e2, jnp.zeros_like(e2))
            for k in range(2):
                rows = slice((ip + k) * PEER_NKEYS, (ip + k + 1) * PEER_NKEYS)
                w_s[rows, tok] = gmats[k] * _gelu_tanh(at_s[rows, tok]).astype(BF16)
    acc_s[...] += _dot(vt_ref[...], w_s[...])

    @pl.when(e == pl.num_programs(2) - 1)
    def _():
        z = DEEPNORM_ALPHA * x_ref[0] + gate_ref[0] * acc_s[...].T
        o_ref[0] = _layer_norm_rows(z, g_ref[...], b_ref[...])


def peer_dense(xm, u_bf16, vt_bf16, e1, cut, e2, r2, x3, mod3, gate_chunk, ln_g, ln_b, tm=512):
    nseg, seg, d = x3.shape
    ne = u_bf16.shape[0]
    nb = PEER_CE // PEER_NKEYS
    tok = lambda s, i, e: (s, i, 0)
    chunk = pl.BlockSpec((1, PEER_HEADS, nb, tm), lambda s, i, e: (s, 0, e, i))
    full = pl.BlockSpec((1, PEER_HEADS, PEER_NKEYS, tm), lambda s, i, e: (s, 0, 0, i))
    return pl.pallas_call(
        functools.partial(_peer_dense_kernel, tm=tm),
        grid=(nseg, seg // tm, ne // PEER_CE),
        in_specs=[pl.BlockSpec((1, tm, d), tok),
                  pl.BlockSpec((PEER_CE, d), lambda s, i, e: (e, 0)),
                  pl.BlockSpec((d, PEER_CE), lambda s, i, e: (0, e)),
                  chunk, chunk, full, full,
                  pl.BlockSpec((1, tm, d), tok),
                  pl.BlockSpec((1, 1, d), lambda s, i, e: (s, 0, gate_chunk)),
                  pl.BlockSpec((1, d), lambda s, i, e: (0, 0)),
                  pl.BlockSpec((1, d), lambda s, i, e: (0, 0))],
        out_specs=pl.BlockSpec((1, tm, d), tok),
        out_shape=jax.ShapeDtypeStruct((nseg, seg, d), F32),
        scratch_shapes=[pltpu.VMEM((d, tm), F32), pltpu.VMEM((PEER_CE, tm), F32), pltpu.VMEM((PEER_CE, tm), BF16)],
        compiler_params=_params(("arbitrary", "arbitrary", "arbitrary")),
        name="peer_dense",
    )(xm, u_bf16, vt_bf16, e1, cut, e2, r2, x3, mod3, ln_g.reshape(1, d), ln_b.reshape(1, d))


def peer_layer(x3, mod3, wq, subkeys, u_tab, v_tab, ln_g, ln_b):
    xm, e1, cut, e2, r2 = peer_route(x3, mod3, 3, wq.astype(BF16), subkeys)
    return peer_dense(xm, u_tab.astype(BF16), v_tab.T.astype(BF16), e1, cut, e2, r2, x3, mod3, 5, ln_g, ln_b)


def _pad_cols(w, n):
    return jnp.pad(w, ((0, 0), (0, n - w.shape[1])))


def _stream(prompt_part, sample_part):
    return jnp.concatenate([prompt_part.reshape(1, -1, prompt_part.shape[-1]), sample_part], axis=0)


def _head_major(a, heads):
    b, t, _ = a.shape
    return jnp.transpose(a.reshape(b, t, heads, -1), (0, 2, 1, 3))


def _token_major(a):
    b, h, t, dh = a.shape
    return jnp.transpose(a, (0, 2, 1, 3)).reshape(b, t, h * dh)


MLSTM_CHUNK = 128
GLA_CHUNK = 32
NPROJ = 3200


def mlstm_layer(x3, mod3, bp, lp, st_c, st_n, st_m, w_in, b_gate, norm_w, w_out, ln_g, ln_b):
    nseg, seg, _ = x3.shape
    bs = nseg - 1
    p = mod_matmul(x3, mod3, 0, _pad_cols(w_in, NPROJ).astype(BF16))
    graw = p[:, :, 3072:3088]
    gp = graw[0].reshape(bp, lp, 16)
    zc = jnp.zeros((bp, 8, M_DK, M_DV), F32)
    zn = jnp.zeros((bp, 8, M_DK), F32)
    hfp, hbp, c_new, n_new, m_new = mlstm_scan(p.reshape(nseg * bp, lp, NPROJ), 0, bp, lp, gp,
                                               jnp.swapaxes(gp, 1, 2), b_gate, zc, zn, zn, min(MLSTM_CHUNK, lp))
    gs = graw[1:]
    hfs, hbs, _, _, _ = mlstm_scan(p, 1, bs, seg, gs, jnp.swapaxes(gs, 1, 2), b_gate,
                                   st_c.reshape(bs, 8, M_DK, M_DV), st_n.reshape(bs, 8, M_DK),
                                   jnp.broadcast_to(st_m.reshape(bs, 8, 1), (bs, 8, M_DK)), MLSTM_CHUNK)
    x3 = outproj_ln("mlstm", (_stream(hfp, hfs), _stream(hbp, hbs)), x3, mod3, 2, w_out.astype(BF16), ln_g, ln_b,
                    norm_w=norm_w, og=p, og_col=2)
    return (x3, c_new.reshape(bp, 2, M_HEADS, M_DK, M_DV), n_new.reshape(bp, 2, M_HEADS, M_DK),
            m_new[:, :, 0].reshape(bp, 2, M_HEADS))


def gla_layer(x3, mod3, bp, lp, st_s, w_in, w_gate2, b_gate2, norm_w, w_out, ln_g, ln_b):
    nseg, seg, _ = x3.shape
    bs = nseg - 1
    p = mod_matmul(x3, mod3, 0, _pad_cols(w_in, NPROJ).astype(BF16))
    gr = p[:, :, 3072:3104]
    zs = jnp.zeros((bp, 8, G_DV, G_DK), F32)
    ofp, obp, s_new = gla_scan(p.reshape(nseg * bp, lp, NPROJ), 0, bp, lp, gr[0].reshape(bp, lp, 32),
                               w_gate2, b_gate2, zs, GLA_CHUNK)
    s0t = jnp.swapaxes(st_s.reshape(bs, 8, G_DK, G_DV), -1, -2)
    ofs, obs, _ = gla_scan(p, 1, bs, seg, gr[1:], w_gate2, b_gate2, s0t, GLA_CHUNK)
    x3 = outproj_ln("gla", (_stream(ofp, ofs), _stream(obp, obs)), x3, mod3, 2, w_out.astype(BF16), ln_g, ln_b,
                    norm_w=jnp.tile(norm_w, G_HEADS), og=p, og_col=2)
    return x3, jnp.swapaxes(s_new, -1, -2).reshape(bp, 2, G_HEADS, G_DK, G_DV)


def na_layer(x3, mod3, bp, lp, cache_k, cache_v, w_in, rpb, w_out, ln_g, ln_b):
    nseg, seg, _ = x3.shape
    bs = nseg - 1
    hd = NA_HEADS * NA_HD
    p = mod_matmul(x3, mod3, 0, w_in.astype(BF16))
    pp = p[0].reshape(bp, lp, 3 * hd)
    hm = lambda a: _head_major(a, NA_HEADS).astype(BF16)
    yp = attention(hm(pp[..., :hd]), hm(pp[..., hd:2 * hd]), hm(pp[..., 2 * hd:]), lp)
    ps = p[1:]
    ys = na_attention(hm(ps[..., :hd]), hm(ps[..., hd:2 * hd]), hm(ps[..., 2 * hd:]),
                      hm(cache_k.reshape(bs, -1, hd)), hm(cache_v.reshape(bs, -1, hd)), na_bias_table(rpb))
    x3 = outproj_ln("plain", _stream(_token_major(yp), _token_major(ys)), x3, mod3, 2, w_out.astype(BF16), ln_g, ln_b)
    return (x3, pp[..., hd:2 * hd].reshape(bp, lp, NA_HEADS, NA_HD), pp[..., 2 * hd:].reshape(bp, lp, NA_HEADS, NA_HD))


def _rope_rotated_cols(w):
    q = MLA_ROPE // 4
    return jnp.concatenate([-w[..., q:2 * q], w[..., :q], -w[..., 3 * q:], w[..., 2 * q:3 * q]], axis=-1)


def _rope_tables(ts):
    ra = MLA_ROPE // 2
    t = np.arange(ts)
    inv = 1.0 / (ROPE_BASE ** (np.arange(0, ra, 2, dtype=np.float32) / ra))
    ang_r = (t // GRID_W).astype(np.float32)[:, None] * inv[None, :]
    ang_c = (t % GRID_W).astype(np.float32)[:, None] * inv[None, :]
    ang = np.concatenate([ang_r, ang_r, ang_c, ang_c], axis=-1).astype(np.float32)
    return jnp.cos(jnp.asarray(ang)), jnp.sin(jnp.asarray(ang))


def mla_layer(x3, mod3, bp, lp, cache_ckv, cache_kpe, w_in, q_norm, w_qup, kv_norm, w_kvup, w_out, ln_g, ln_b):
    nseg, seg, _ = x3.shape
    bs = nseg - 1
    nq = MLA_Q_LORA + MLA_KV_LORA
    w_ext = jnp.concatenate([w_in, _rope_rotated_cols(w_in[:, nq:])], axis=1)
    p = mod_matmul(x3, mod3, 0, _pad_cols(w_ext, 896).astype(BF16))
    cos_t, sin_t = _rope_tables(seg)
    cos3 = jnp.concatenate([jnp.ones((1, seg, MLA_ROPE), F32), jnp.broadcast_to(cos_t, (bs, seg, MLA_ROPE))], 0)
    sin3 = jnp.concatenate([jnp.zeros((1, seg, MLA_ROPE), F32), jnp.broadcast_to(sin_t, (bs, seg, MLA_ROPE))], 0)
    wq = w_qup.reshape(MLA_Q_LORA, MLA_HEADS, MLA_NOPE + MLA_ROPE)
    wq_rope = wq[:, :, MLA_NOPE:]
    w_q3 = jnp.concatenate([wq[:, :, :MLA_NOPE].reshape(MLA_Q_LORA, -1), wq_rope.reshape(MLA_Q_LORA, -1),
                            _rope_rotated_cols(wq_rope).reshape(MLA_Q_LORA, -1)], axis=1).astype(BF16)
    q_all = mla_q(p, q_norm, w_q3, jnp.tile(cos3, (1, 1, MLA_HEADS)), jnp.tile(sin3, (1, 1, MLA_HEADS)))
    wkv = w_kvup.reshape(MLA_KV_LORA, MLA_HEADS, MLA_NOPE + MLA_VD)
    w_kv2 = jnp.concatenate([wkv[:, :, :MLA_NOPE].reshape(MLA_KV_LORA, -1),
                             wkv[:, :, MLA_NOPE:].reshape(MLA_KV_LORA, -1)], axis=1).astype(BF16)
    ckvn, kpe, kv = mla_kv(p, kv_norm, w_kv2, cos3, sin3)
    kvc = matmul(cache_ckv.reshape(-1, MLA_KV_LORA), w_kv2, 512).reshape(bs, -1, w_kv2.shape[1])
    nn = MLA_HEADS * MLA_NOPE

    def heads(q_rows, kv_rows, kpe_rows):
        b, t, _ = q_rows.shape
        tk = kv_rows.shape[1]
        qh = jnp.concatenate([q_rows[..., :nn].reshape(b, t, MLA_HEADS, MLA_NOPE),
                              q_rows[..., nn:].reshape(b, t, MLA_HEADS, MLA_ROPE)], -1)
        kh = jnp.concatenate([kv_rows[..., :nn].reshape(b, tk, MLA_HEADS, MLA_NOPE),
                              jnp.broadcast_to(kpe_rows[:, :, None, :], (b, tk, MLA_HEADS, MLA_ROPE))], -1)
        vh = kv_rows[..., nn:].reshape(b, tk, MLA_HEADS, MLA_VD)
        tr = lambda a: jnp.transpose(a, (0, 2, 1, 3)).astype(BF16)
        return tr(qh), tr(kh), tr(vh)

    yp = attention(*heads(q_all[0].reshape(bp, lp, -1), kv[0].reshape(bp, lp, -1), kpe[0].reshape(bp, lp, -1)), lp)
    ys = attention(*heads(q_all[1:], jnp.concatenate([kv[1:], kvc], 1), jnp.concatenate([kpe[1:], cache_kpe], 1)), 256)
    x3 = outproj_ln("plain", _stream(_token_major(yp), _token_major(ys)), x3, mod3, 2, w_out.astype(BF16), ln_g, ln_b)
    return x3, ckvn[0].reshape(bp, lp, MLA_KV_LORA), kpe[0].reshape(bp, lp, MLA_ROPE)


def kernel(x_prompt, x_sample, c, c_ctx, state_mlstm_C, state_mlstm_n, state_mlstm_m, state_gla_S, cache_na_k, cache_na_v, cache_mla_ckv, cache_mla_kpe, ada_w, ada_b, ln_mix_g, ln_mix_b, ln_ffn_g, ln_ffn_b, mlstm_w_in, mlstm_b_gate, mlstm_norm_w, mlstm_w_out, gla_w_in, gla_w_gate2, gla_b_gate2, gla_norm_w, gla_w_out, na_w_in, na_rpb, na_w_out, mla_w_in, mla_q_norm, mla_w_qup, mla_kv_norm, mla_w_kvup, mla_w_out, peer_w_q, peer_subkeys, peer_u, peer_v):
    bp, lp, d = x_prompt.shape
    bs, ts, _ = x_sample.shape
    assert bp * lp == ts and bs + 1 <= 8
    x3 = _stream(x_prompt, x_sample)
    cond8 = jnp.zeros((8, d), F32).at[0].set(c_ctx).at[1:1 + bs].set(c)
    mods = adaln_all(cond8, ada_w, ada_b)
    outs = {}
    for l in range(DEPTH):
        mod3 = mods[l].reshape(8, 1, ADA_CHUNKS * d)
        kind = l % 4
        if kind == 0:
            x3, outs["C"], outs["n"], outs["m"] = mlstm_layer(
                x3, mod3, bp, lp, state_mlstm_C, state_mlstm_n, state_mlstm_m, mlstm_w_in, mlstm_b_gate,
                mlstm_norm_w, mlstm_w_out, ln_mix_g[l], ln_mix_b[l])
        elif kind == 1:
            x3, outs["S"] = gla_layer(x3, mod3, bp, lp, state_gla_S, gla_w_in, gla_w_gate2, gla_b_gate2,
                                      gla_norm_w, gla_w_out, ln_mix_g[l], ln_mix_b[l])
        elif kind == 2:
            x3, outs["nk"], outs["nv"] = na_layer(x3, mod3, bp, lp, cache_na_k, cache_na_v, na_w_in, na_rpb,
                                                  na_w_out, ln_mix_g[l], ln_mix_b[l])
        else:
            x3, outs["ckv"], outs["kpe"] = mla_layer(x3, mod3, bp, lp, cache_mla_ckv, cache_mla_kpe, mla_w_in,
                                                     mla_q_norm, mla_w_qup, mla_kv_norm, mla_w_kvup, mla_w_out,
                                                     ln_mix_g[l], ln_mix_b[l])
        x3 = peer_layer(x3, mod3, peer_w_q[l], peer_subkeys[l], peer_u[l], peer_v[l], ln_ffn_g[l], ln_ffn_b[l])
    return (x3[0].reshape(bp, lp, d), x3[1:], outs["C"], outs["n"], outs["m"], outs["S"], outs["nk"], outs["nv"],
            outs["ckv"], outs["kpe"])
```

```python
import functools

import numpy as np
import jax
import jax.numpy as jnp
from jax import lax
from jax.experimental import pallas as pl
from jax.experimental.pallas import tpu as pltpu

D_MODEL = 1024
DEPTH = 4
GRID_W = 64
DEEPNORM_ALPHA = (2.0 * DEPTH) ** 0.25
ADA_CHUNKS = 6
NORM_EPS = 1e-5
SEG = 4096
NSEG = 3

M_HEADS, M_DK, M_DV = 4, 128, 256
G_HEADS, G_DK, G_DV = 4, 128, 256
G_GATE_RANK = 16
G_GATE_NORM = 16.0
NA_HEADS, NA_HD, NA_ROWS, NA_COLS = 16, 64, 8, 16
MLA_HEADS, MLA_Q_LORA, MLA_KV_LORA, MLA_NOPE, MLA_ROPE, MLA_VD = 16, 512, 256, 64, 32, 64
ROPE_BASE = 10000.0
PEER_HEADS, PEER_NKEYS, PEER_HALF, PEER_TOPK = 8, 128, 128, 16

V7X_VMEM_LIMIT = 56 * 1024 * 1024
F32 = jnp.float32
BF16 = jnp.bfloat16
NEG_INF = float("-inf")


def _params(sem, vmem=V7X_VMEM_LIMIT):
    return pltpu.CompilerParams(dimension_semantics=sem, vmem_limit_bytes=vmem)


def _dot(a, b, dims=((1,), (0,))):
    return lax.dot_general(a, b, (dims, ((), ())), preferred_element_type=F32)


def _split3(a):
    hi = a.astype(BF16)
    r1 = a - hi.astype(F32)
    mid = r1.astype(BF16)
    lo = (r1 - mid.astype(F32)).astype(BF16)
    return hi, mid, lo


def _dot_exact_lhs(m01, a):
    hi, mid, lo = _split3(a)
    return _dot(m01, hi) + _dot(m01, mid) + _dot(m01, lo)


def _dot_exact_rhs(a, m01):
    hi, mid, lo = _split3(a)
    return _dot(hi, m01) + _dot(mid, m01) + _dot(lo, m01)


def _log_sigmoid(x):
    return jnp.minimum(x, 0.0) - jnp.log(1.0 + jnp.exp(-jnp.abs(x)))


def _sigmoid(x):
    return 1.0 / (1.0 + jnp.exp(-x))


def _adaln_kernel(c_ref, w_ref, b_ref, o_ref):
    cv = c_ref[...]
    a = cv * _sigmoid(cv)
    o_ref[0] = lax.dot_general(a, w_ref[0], (((1,), (0,)), ((), ())), precision=lax.Precision.HIGHEST,
                               preferred_element_type=F32) + b_ref[0]


def adaln_all(cond8, ada_w, ada_b):
    tn = 1024
    n = ada_w.shape[-1]
    return pl.pallas_call(
        _adaln_kernel,
        grid=(DEPTH, n // tn),
        in_specs=[pl.BlockSpec((8, D_MODEL), lambda l, j: (0, 0)),
                  pl.BlockSpec((1, D_MODEL, tn), lambda l, j: (l, 0, j)),
                  pl.BlockSpec((1, 1, tn), lambda l, j: (l, 0, j))],
        out_specs=pl.BlockSpec((1, 8, tn), lambda l, j: (l, 0, j)),
        out_shape=jax.ShapeDtypeStruct((DEPTH, 8, n), F32),
        compiler_params=_params(("arbitrary", "arbitrary")),
        name="adaln",
    )(cond8, ada_w, ada_b.reshape(DEPTH, 1, n))


def _modmm_kernel(x_ref, sh_ref, sc_ref, w_ref, o_ref, xm_ref):
    @pl.when(pl.program_id(2) == 0)
    def _():
        xm_ref[...] = (x_ref[0] * (1.0 + sc_ref[0]) + sh_ref[0]).astype(BF16)

    o_ref[0] = _dot(xm_ref[...], w_ref[...]).astype(o_ref.dtype)


def mod_matmul(x3, mod3, shift_chunk, w_bf16, tm=512, tn=None, out_dtype=F32):
    nseg, seg, d = x3.shape
    n = w_bf16.shape[1]
    tn = n if tn is None else tn
    return pl.pallas_call(
        _modmm_kernel,
        grid=(nseg, seg // tm, n // tn),
        in_specs=[pl.BlockSpec((1, tm, d), lambda s, i, j: (s, i, 0)),
                  pl.BlockSpec((1, 1, d), lambda s, i, j: (s, 0, shift_chunk)),
                  pl.BlockSpec((1, 1, d), lambda s, i, j: (s, 0, shift_chunk + 1)),
                  pl.BlockSpec((d, tn), lambda s, i, j: (0, j))],
        out_specs=pl.BlockSpec((1, tm, tn), lambda s, i, j: (s, i, j)),
        out_shape=jax.ShapeDtypeStruct((nseg, seg, n), out_dtype),
        scratch_shapes=[pltpu.VMEM((tm, d), BF16)],
        compiler_params=_params(("arbitrary", "arbitrary", "arbitrary")),
        name="mod_matmul",
    )(x3, mod3, mod3, w_bf16)


def _layer_norm_rows(y, g, b):
    mu = jnp.mean(y, axis=-1, keepdims=True)
    yc = y - mu
    var = jnp.mean(yc * yc, axis=-1, keepdims=True)
    return yc * lax.rsqrt(var + NORM_EPS) * g + b


def _outproj_kernel(*refs, mode):
    if mode == "plain":
        y_ref, x_ref, gate_ref, w_ref, g_ref, b_ref, o_ref = refs
        yin = y_ref[0].astype(BF16)
    else:
        ya_ref, yb_ref, og_ref, nw_ref, x_ref, gate_ref, w_ref, g_ref, b_ref, o_ref = refs
        hs = ya_ref[0] + yb_ref[0]
        og = og_ref[0]
        parts = []
        for h in range(4):
            seg = hs[:, h * 256:(h + 1) * 256]
            nw = nw_ref[:, h * 256:(h + 1) * 256]
            if mode == "mlstm":
                mu = jnp.mean(seg, axis=-1, keepdims=True)
                sc = seg - mu
                var = jnp.mean(sc * sc, axis=-1, keepdims=True)
                parts.append(sc * lax.rsqrt(var + NORM_EPS) * nw)
            else:
                ms = jnp.mean(seg * seg, axis=-1, keepdims=True)
                parts.append(seg * lax.rsqrt(ms + NORM_EPS) * nw)
        hn = jnp.concatenate(parts, axis=-1)
        act = _sigmoid(og) if mode == "mlstm" else og * _sigmoid(og)
        yin = (act * hn).astype(BF16)
    y = _dot(yin, w_ref[...])
    z = DEEPNORM_ALPHA * x_ref[0] + gate_ref[0] * y
    o_ref[0] = _layer_norm_rows(z, g_ref[...], b_ref[...])


def outproj_ln(mode, ys, x3, mod3, gate_chunk, w_bf16, ln_g, ln_b, norm_w=None, og=None, og_col=0, tm=512):
    nseg, seg, d = x3.shape
    k = w_bf16.shape[0]
    tok = lambda s, i: (s, i, 0)
    if mode == "plain":
        args = [ys]
        specs = [pl.BlockSpec((1, tm, k), tok)]
    else:
        args = [ys[0], ys[1], og, norm_w.reshape(1, k)]
        specs = [pl.BlockSpec((1, tm, k), tok), pl.BlockSpec((1, tm, k), tok),
                 pl.BlockSpec((1, tm, k), lambda s, i: (s, i, og_col)),
                 pl.BlockSpec((1, k), lambda s, i: (0, 0))]
    args += [x3, mod3, w_bf16, ln_g.reshape(1, d), ln_b.reshape(1, d)]
    specs += [pl.BlockSpec((1, tm, d), tok),
              pl.BlockSpec((1, 1, d), lambda s, i: (s, 0, gate_chunk)),
              pl.BlockSpec((k, d), lambda s, i: (0, 0)),
              pl.BlockSpec((1, d), lambda s, i: (0, 0)),
              pl.BlockSpec((1, d), lambda s, i: (0, 0))]
    return pl.pallas_call(
        functools.partial(_outproj_kernel, mode=mode),
        grid=(nseg, seg // tm),
        in_specs=specs,
        out_specs=pl.BlockSpec((1, tm, d), tok),
        out_shape=jax.ShapeDtypeStruct((nseg, seg, d), F32),
        compiler_params=_params(("arbitrary", "arbitrary")),
        name="outproj_ln_" + mode,
    )(*args)


def _tri(n, lower):
    r = lax.broadcasted_iota(jnp.int32, (n, n), 0)
    c = lax.broadcasted_iota(jnp.int32, (n, n), 1)
    return (c <= r) if lower else (c >= r)


def _mlstm_kernel(pf_ref, pb_ref, gf_ref, gb_ref, gtf_ref, gtb_ref, bias_ref, biast_ref,
                  c0_ref, n0_ref, m0_ref, hf_ref, hb_ref, co_ref, no_ref, mo_ref,
                  c_s, n_s, m_s, *, L):
    c = pl.program_id(1)

    @pl.when(c == 0)
    def _():
        c_s[...] = c0_ref[0]
        n_s[...] = n0_ref[0]
        m_s[...] = m0_ref[0]

    for d in range(2):
        p_ref, g_ref, gt_ref, h_ref = ((pf_ref, gf_ref, gtf_ref, hf_ref) if d == 0
                                       else (pb_ref, gb_ref, gtb_ref, hb_ref))
        mask = _tri(L, lower=(d == 0))
        mcol = mask.astype(BF16)
        mrow = _tri(L, lower=(d != 0)).astype(BF16)
        g = g_ref[0] + bias_ref[...]
        gt = gt_ref[0] + biast_ref[...]
        li_c = g[:, d * 8:d * 8 + 4]
        lf_c = _log_sigmoid(g[:, d * 8 + 4:d * 8 + 8])
        li_r = gt[d * 8:d * 8 + 4, :]
        lf_r = _log_sigmoid(gt[d * 8 + 4:d * 8 + 8, :])
        b_c = _dot_exact_lhs(mcol, lf_c)
        b_r = _dot_exact_rhs(lf_r, mrow)
        last = L - 1 if d == 0 else 0
        for h in range(M_HEADS):
            u = d * M_HEADS + h
            q = p_ref[0, :, h * M_DK:(h + 1) * M_DK]
            k = p_ref[0, :, 512 + h * M_DK:512 + (h + 1) * M_DK] * (M_DK ** -0.5)
            v = p_ref[0, :, 1024 + h * M_DV:1024 + (h + 1) * M_DV].astype(BF16)
            qb = q.astype(BF16)
            bc, br = b_c[:, h:h + 1], b_r[h:h + 1, :]
            lic, lir = li_c[:, h:h + 1], li_r[h:h + 1, :]
            m_prev = m_s[u:u + 1, 0:1]
            dmat = jnp.where(mask, bc - br + lir, NEG_INF)
            inter = bc + m_prev
            mt = jnp.maximum(inter, jnp.max(dmat, axis=-1, keepdims=True))
            smat = _dot(qb, k.astype(BF16), ((1,), (1,))) * jnp.exp(dmat - mt)
            ei = jnp.exp(inter - mt)
            cmat = c_s[u]
            num = _dot(smat.astype(BF16), v) + ei * _dot(qb, cmat.astype(BF16))
            nrow = n_s[u:u + 1, :]
            den = jnp.sum(smat, axis=-1, keepdims=True) + ei * jnp.sum(q * nrow, axis=-1, keepdims=True)
            h_ref[0, :, h * M_DV:(h + 1) * M_DV] = num / jnp.maximum(jnp.abs(den), jnp.exp(-mt))
            tot = br[:, last:last + 1]
            g_c = tot - bc + lic
            g_r = tot - br + lir
            m_new = jnp.maximum(tot + m_prev, jnp.max(g_r, axis=-1, keepdims=True))
            kw = k * jnp.exp(g_c - m_new)
            dec = jnp.exp(tot + m_prev - m_new)
            c_s[u] = dec * cmat + _dot(kw.astype(BF16), v, ((0,), (0,)))
            n_s[u:u + 1, :] = dec * nrow + jnp.sum(kw, axis=0, keepdims=True)
            m_s[u:u + 1, :] = jnp.broadcast_to(m_new, (1, 128))

    @pl.when(c == pl.num_programs(1) - 1)
    def _():
        co_ref[0] = c_s[...]
        no_ref[0] = n_s[...]
        mo_ref[0] = m_s[...]


def mlstm_scan(p, b0, nb, t, g, gt, bias, c0, n0, m0, L):
    nc = t // L
    hshape = jax.ShapeDtypeStruct((nb, t, M_HEADS * M_DV), F32)
    fwd = lambda b, c: (b + b0, c, 0)
    bwd = lambda b, c: (b + b0, nc - 1 - c, 0)
    st4 = lambda b, c: (b, 0, 0, 0)
    st3 = lambda b, c: (b, 0, 0)
    return pl.pallas_call(
        functools.partial(_mlstm_kernel, L=L),
        grid=(nb, nc),
        in_specs=[pl.BlockSpec((1, L, 2048), fwd), pl.BlockSpec((1, L, 2048), bwd),
                  pl.BlockSpec((1, L, 16), lambda b, c: (b, c, 0)),
                  pl.BlockSpec((1, L, 16), lambda b, c: (b, nc - 1 - c, 0)),
                  pl.BlockSpec((1, 16, L), lambda b, c: (b, 0, c)),
                  pl.BlockSpec((1, 16, L), lambda b, c: (b, 0, nc - 1 - c)),
                  pl.BlockSpec((1, 16), lambda b, c: (0, 0)),
                  pl.BlockSpec((16, 1), lambda b, c: (0, 0)),
                  pl.BlockSpec((1, 8, M_DK, M_DV), st4),
                  pl.BlockSpec((1, 8, M_DK), st3),
                  pl.BlockSpec((1, 8, M_DK), st3)],
        out_specs=[pl.BlockSpec((1, L, 1024), lambda b, c: (b, c, 0)),
                   pl.BlockSpec((1, L, 1024), lambda b, c: (b, nc - 1 - c, 0)),
                   pl.BlockSpec((1, 8, M_DK, M_DV), st4),
                   pl.BlockSpec((1, 8, M_DK), st3),
                   pl.BlockSpec((1, 8, M_DK), st3)],
        out_shape=[hshape, hshape,
                   jax.ShapeDtypeStruct((nb, 8, M_DK, M_DV), F32),
                   jax.ShapeDtypeStruct((nb, 8, M_DK), F32),
                   jax.ShapeDtypeStruct((nb, 8, M_DK), F32)],
        scratch_shapes=[pltpu.VMEM((8, M_DK, M_DV), F32), pltpu.VMEM((8, M_DK), F32),
                        pltpu.VMEM((8, M_DK), F32)],
        compiler_params=_params(("arbitrary", "arbitrary")),
        name="mlstm_scan",
    )(p, p, g, g, gt, gt, bias.reshape(1, 16), bias.reshape(16, 1), c0, n0, m0)


def _gla_kernel(pf_ref, pb_ref, gf_ref, gb_ref, w2_ref, b2_ref, s0_ref, of_ref, ob_ref, so_ref, s_s, *, L):
    c = pl.program_id(1)

    @pl.when(c == 0)
    def _():
        s_s[...] = s0_ref[0]

    for d in range(2):
        p_ref, g_ref, o_ref = (pf_ref, gf_ref, of_ref) if d == 0 else (pb_ref, gb_ref, ob_ref)
        mask = _tri(L, lower=(d == 0))
        mcol = mask.astype(BF16)
        gr = g_ref[0][:, d * G_GATE_RANK:(d + 1) * G_GATE_RANK]
        pre = lax.dot_general(gr, w2_ref[d], (((1,), (0,)), ((), ())), precision=lax.Precision.HIGHEST,
                              preferred_element_type=F32) + b2_ref[d]
        la = _log_sigmoid(pre) * (1.0 / G_GATE_NORM)
        bc_all = _dot_exact_lhs(mcol, la)
        last = L - 1 if d == 0 else 0
        for h in range(G_HEADS):
            u = d * G_HEADS + h
            q = p_ref[0, :, h * G_DK:(h + 1) * G_DK] * (G_DK ** -0.5)
            k = p_ref[0, :, 512 + h * G_DK:512 + (h + 1) * G_DK]
            v = p_ref[0, :, 1024 + h * G_DV:1024 + (h + 1) * G_DV].astype(BF16)
            bc = bc_all[:, h * G_DK:(h + 1) * G_DK]
            qd = (q * jnp.exp(bc)).astype(BF16)
            kd = (k * jnp.exp(-bc)).astype(BF16)
            a = jnp.where(mask, _dot(qd, kd, ((1,), (1,))), 0.0)
            st = s_s[u]
            o_ref[0, :, h * G_DV:(h + 1) * G_DV] = (_dot(a.astype(BF16), v)
                                                    + _dot(qd, st.astype(BF16), ((1,), (1,))))
            bl = bc[last:last + 1, :]
            kl = (k * jnp.exp(bl - bc)).astype(BF16)
            s_s[u] = st * jnp.exp(bl) + _dot(v, kl, ((0,), (0,)))

    @pl.when(c == pl.num_programs(1) - 1)
    def _():
        so_ref[0] = s_s[...]


def gla_scan(p, b0, nb, t, gr, w2, b2, s0t, L):
    nc = t // L
    oshape = jax.ShapeDtypeStruct((nb, t, G_HEADS * G_DV), F32)
    st4 = lambda b, c: (b, 0, 0, 0)
    return pl.pallas_call(
        functools.partial(_gla_kernel, L=L),
        grid=(nb, nc),
        in_specs=[pl.BlockSpec((1, L, 2048), lambda b, c: (b + b0, c, 0)),
                  pl.BlockSpec((1, L, 2048), lambda b, c: (b + b0, nc - 1 - c, 0)),
                  pl.BlockSpec((1, L, 32), lambda b, c: (b, c, 0)),
                  pl.BlockSpec((1, L, 32), lambda b, c: (b, nc - 1 - c, 0)),
                  pl.BlockSpec((2, G_GATE_RANK, 512), lambda b, c: (0, 0, 0)),
                  pl.BlockSpec((2, 1, 512), lambda b, c: (0, 0, 0)),
                  pl.BlockSpec((1, 8, G_DV, G_DK), st4)],
        out_specs=[pl.BlockSpec((1, L, 1024), lambda b, c: (b, c, 0)),
                   pl.BlockSpec((1, L, 1024), lambda b, c: (b, nc - 1 - c, 0)),
                   pl.BlockSpec((1, 8, G_DV, G_DK), st4)],
        out_shape=[oshape, oshape, jax.ShapeDtypeStruct((nb, 8, G_DV, G_DK), F32)],
        scratch_shapes=[pltpu.VMEM((8, G_DV, G_DK), F32)],
        compiler_params=_params(("arbitrary", "arbitrary")),
        name="gla_scan",
    )(p, p, gr, gr, w2, b2.reshape(2, 1, 512), s0t)


def _attn_kernel(q_ref, k_ref, v_ref, o_ref, *, scale):
    s = _dot(q_ref[0, 0], k_ref[0, 0], ((1,), (1,))) * scale
    m = jnp.max(s, axis=-1, keepdims=True)
    p = jnp.exp(s - m)
    l = jnp.sum(p, axis=-1, keepdims=True)
    o_ref[0, 0] = _dot(p.astype(BF16), v_ref[0, 0]) / l


def attention(q, k, v, tq):
    b, h, lq, dq = q.shape
    lk, dv = k.shape[2], v.shape[3]
    return pl.pallas_call(
        functools.partial(_attn_kernel, scale=dq ** -0.5),
        grid=(b, h, lq // tq),
        in_specs=[pl.BlockSpec((1, 1, tq, dq), lambda b, h, i: (b, h, i, 0)),
                  pl.BlockSpec((1, 1, lk, dq), lambda b, h, i: (b, h, 0, 0)),
                  pl.BlockSpec((1, 1, lk, dv), lambda b, h, i: (b, h, 0, 0))],
        out_specs=pl.BlockSpec((1, 1, tq, dv), lambda b, h, i: (b, h, i, 0)),
        out_shape=jax.ShapeDtypeStruct((b, h, lq, dv), F32),
        compiler_params=_params(("arbitrary", "arbitrary", "arbitrary")),
        name="attention",
    )(q, k, v)


NA_RB = 8


def _na_kernel(q_ref, k_ref, v_ref, kc_ref, vc_ref, bias_ref, o_ref, *, rows):
    j = pl.program_id(2)
    scale = NA_HD ** -0.5
    kc, vc = kc_ref[0, 0], vc_ref[0, 0]
    for a in range(NA_RB):
        r = j * NA_RB + a
        start = jnp.clip(r - NA_ROWS // 2, 0, rows - NA_ROWS)
        dr0 = start - r + (NA_ROWS - 1)
        off = pl.multiple_of(start * GRID_W, GRID_W)
        qa = q_ref[0, 0, a * GRID_W:(a + 1) * GRID_W, :]
        kl = k_ref[0, 0, pl.ds(off, NA_ROWS * GRID_W), :]
        vl = v_ref[0, 0, pl.ds(off, NA_ROWS * GRID_W), :]
        s_loc = _dot(qa, kl, ((1,), (1,))) * scale + bias_ref[0, dr0]
        s_ctx = _dot(qa, kc, ((1,), (1,))) * scale
        m = jnp.maximum(jnp.max(s_loc, axis=-1, keepdims=True), jnp.max(s_ctx, axis=-1, keepdims=True))
        p_loc = jnp.exp(s_loc - m)
        p_ctx = jnp.exp(s_ctx - m)
        l = jnp.sum(p_loc, axis=-1, keepdims=True) + jnp.sum(p_ctx, axis=-1, keepdims=True)
        o = _dot(p_loc.astype(BF16), vl) + _dot(p_ctx.astype(BF16), vc)
        o_ref[0, 0, a * GRID_W:(a + 1) * GRID_W, :] = o / l


def na_bias_table(rpb):
    cq = np.arange(GRID_W)[:, None]
    ck = np.arange(GRID_W)[None, :]
    cs = np.clip(cq - NA_COLS // 2, 0, GRID_W - NA_COLS)
    ok = (ck >= cs) & (ck < cs + NA_COLS)
    dc = np.clip(ck - cq, -(NA_COLS - 1), NA_COLS - 1) + (NA_COLS - 1)
    t = jnp.where(ok[None, None], rpb.astype(F32)[:, :, dc], NEG_INF)
    rows = np.arange(NA_ROWS)[:, None] + np.arange(NA_ROWS)[None, :]
    tf = t[:, rows]
    return jnp.transpose(tf, (0, 1, 3, 2, 4)).reshape(NA_HEADS, NA_ROWS, GRID_W, NA_ROWS * GRID_W)


def na_attention(q, k, v, kc, vc, bias):
    b, h, t, dh = q.shape
    lc = kc.shape[2]
    rows = t // GRID_W
    full = lambda b, h, j: (b, h, 0, 0)
    return pl.pallas_call(
        functools.partial(_na_kernel, rows=rows),
        grid=(b, h, rows // NA_RB),
        in_specs=[pl.BlockSpec((1, 1, NA_RB * GRID_W, dh), lambda b, h, j: (b, h, j, 0)),
                  pl.BlockSpec((1, 1, t, dh), full), pl.BlockSpec((1, 1, t, dh), full),
                  pl.BlockSpec((1, 1, lc, dh), full), pl.BlockSpec((1, 1, lc, dh), full),
                  pl.BlockSpec((1, NA_ROWS, GRID_W, NA_ROWS * GRID_W), lambda b, h, j: (h, 0, 0, 0))],
        out_specs=pl.BlockSpec((1, 1, NA_RB * GRID_W, dh), lambda b, h, j: (b, h, j, 0)),
        out_shape=jax.ShapeDtypeStruct((b, h, t, dh), F32),
        compiler_params=_params(("arbitrary", "arbitrary", "arbitrary")),
        name="na_attention",
    )(q, k, v, kc, vc, bias)


def _rms_rows(x, g):
    return x * lax.rsqrt(jnp.mean(x * x, axis=-1, keepdims=True) + NORM_EPS) * g


def _mla_q_kernel(cq_ref, g_ref, w_ref, cos_ref, sin_ref, o_ref):
    r = _dot(_rms_rows(cq_ref[0], g_ref[...]).astype(BF16), w_ref[...])
    nn = MLA_HEADS * MLA_NOPE
    nr = MLA_HEADS * MLA_ROPE
    o_ref[0, :, :nn] = r[:, :nn]
    o_ref[0, :, nn:] = r[:, nn:nn + nr] * cos_ref[0] + r[:, nn + nr:] * sin_ref[0]


def mla_q(p, q_norm, w_q3, cos_q, sin_q, tm=512):
    nseg, seg, _ = p.shape
    nout = MLA_HEADS * (MLA_NOPE + MLA_ROPE)
    nr = MLA_HEADS * MLA_ROPE
    tok = lambda s, i: (s, i, 0)
    return pl.pallas_call(
        _mla_q_kernel,
        grid=(nseg, seg // tm),
        in_specs=[pl.BlockSpec((1, tm, MLA_Q_LORA), tok),
                  pl.BlockSpec((1, MLA_Q_LORA), lambda s, i: (0, 0)),
                  pl.BlockSpec(w_q3.shape, lambda s, i: (0, 0)),
                  pl.BlockSpec((1, tm, nr), tok), pl.BlockSpec((1, tm, nr), tok)],
        out_specs=pl.BlockSpec((1, tm, nout), tok),
        out_shape=jax.ShapeDtypeStruct((nseg, seg, nout), F32),
        compiler_params=_params(("arbitrary", "arbitrary")),
        name="mla_q",
    )(p, q_norm.reshape(1, -1), w_q3, cos_q, sin_q)


def _mla_kv_kernel(ckv_ref, kpe_ref, g_ref, w_ref, cos_ref, sin_ref, ckvn_ref, kpeo_ref, kv_ref):
    cn = _rms_rows(ckv_ref[0], g_ref[...])
    ckvn_ref[0] = cn
    kv_ref[0] = _dot(cn.astype(BF16), w_ref[...])
    kp = kpe_ref[0]
    kpeo_ref[0] = kp[:, :MLA_ROPE] * cos_ref[0] + kp[:, MLA_ROPE:2 * MLA_ROPE] * sin_ref[0]


def mla_kv(p, kv_norm, w_kv, cos_k, sin_k, tm=512):
    nseg, seg, _ = p.shape
    nkv = w_kv.shape[1]
    tok = lambda s, i: (s, i, 0)
    return pl.pallas_call(
        _mla_kv_kernel,
        grid=(nseg, seg // tm),
        in_specs=[pl.BlockSpec((1, tm, MLA_KV_LORA), lambda s, i: (s, i, MLA_Q_LORA // MLA_KV_LORA)),
                  pl.BlockSpec((1, tm, 128), lambda s, i: (s, i, (MLA_Q_LORA + MLA_KV_LORA) // 128)),
                  pl.BlockSpec((1, MLA_KV_LORA), lambda s, i: (0, 0)),
                  pl.BlockSpec(w_kv.shape, lambda s, i: (0, 0)),
                  pl.BlockSpec((1, tm, MLA_ROPE), tok), pl.BlockSpec((1, tm, MLA_ROPE), tok)],
        out_specs=[pl.BlockSpec((1, tm, MLA_KV_LORA), tok), pl.BlockSpec((1, tm, MLA_ROPE), tok),
                   pl.BlockSpec((1, tm, nkv), tok)],
        out_shape=[jax.ShapeDtypeStruct((nseg, seg, MLA_KV_LORA), F32),
                   jax.ShapeDtypeStruct((nseg, seg, MLA_ROPE), F32),
                   jax.ShapeDtypeStruct((nseg, seg, nkv), F32)],
        compiler_params=_params(("arbitrary", "arbitrary")),
        name="mla_kv",
    )(p, p, kv_norm.reshape(1, -1), w_kv, cos_k, sin_k)


def _mm_kernel(a_ref, w_ref, o_ref):
    o_ref[...] = _dot(a_ref[...].astype(BF16), w_ref[...])


def matmul(a, w_bf16, tm):
    m, k = a.shape
    n = w_bf16.shape[1]
    return pl.pallas_call(
        _mm_kernel,
        grid=(m // tm,),
        in_specs=[pl.BlockSpec((tm, k), lambda i: (i, 0)), pl.BlockSpec((k, n), lambda i: (0, 0))],
        out_specs=pl.BlockSpec((tm, n), lambda i: (i, 0)),
        out_shape=jax.ShapeDtypeStruct((m, n), F32),
        compiler_params=_params(("arbitrary",)),
        name="matmul",
    )(a, w_bf16)


PEER_RT = 128
NOT_TOP = 99.0


def _top16(s, exact):
    key = lax.broadcasted_iota(jnp.int32, s.shape, 0).astype(F32)
    rank = jnp.full(s.shape, NOT_TOP, F32)
    vals = []
    for r in range(PEER_TOPK):
        m = jnp.max(s, axis=0, keepdims=True)
        hit = s == m
        if exact:
            hit = key == jnp.min(jnp.where(hit, key, 1e9), axis=0, keepdims=True)
        rank = jnp.where(hit, float(r), rank)
        s = jnp.where(hit, NEG_INF, s)
        vals.append(m)
    return vals, rank


def _pair_topk(av, bv, exact):
    n = av[0].shape[-1]
    a_lo, a_hi = jnp.concatenate(av[:8], 0), jnp.concatenate(av[8:], 0)
    b_lo, b_hi = jnp.concatenate(bv[:8], 0), jnp.concatenate(bv[8:], 0)
    row = lax.broadcasted_iota(jnp.int32, (8, n), 0).astype(F32)

    no_pos = 1e8

    def rows_b(a, b_blk, boff, nvalid):
        ok = row < nvalid
        return jnp.where(ok, av[a] + b_blk, NEG_INF), jnp.where(ok, a * 16.0 + boff + row, no_pos)

    def rows_a(b, a_blk, aoff, lo, hi):
        ok = (row >= lo) & (row < hi)
        return jnp.where(ok, a_blk + bv[b], NEG_INF), jnp.where(ok, (aoff + row) * 16.0 + b, no_pos)

    groups = [rows_b(0, b_lo, 0, 8), rows_b(0, b_hi, 8, 8), rows_b(1, b_lo, 0, 8), rows_b(2, b_lo, 0, 5),
              rows_b(3, b_lo, 0, 4), rows_a(0, a_lo, 0, 4, 8), rows_a(0, a_hi, 8, 0, 8),
              rows_a(1, a_lo, 0, 4, 8), rows_a(2, a_lo, 0, 4, 5)]
    cands = [g[0] for g in groups]
    poss = [g[1] for g in groups]
    sels = [jnp.zeros((8, n), F32) for _ in groups]
    top = av[0] + bv[0]
    z = jnp.zeros((1, n), F32)
    for _ in range(PEER_TOPK):
        m = functools.reduce(jnp.maximum, cands)
        m = jnp.max(m, axis=0, keepdims=True)
        hits = [c == m for c in cands]
        if exact:
            first = functools.reduce(jnp.minimum, [jnp.where(hh, p, 1e9) for hh, p in zip(hits, poss)])
            first = jnp.min(first, axis=0, keepdims=True)
            hits = [p == first for p in poss]
        cands = [jnp.where(hh, NEG_INF, c) for hh, c in zip(hits, cands)]
        sels = [jnp.where(hh, 1.0, s) for hh, s in zip(hits, sels)]
        z = z + jnp.exp(m - top)
    cnt = lambda x: jnp.sum(x, axis=0, keepdims=True)
    cut_lo = sels[5] + sels[7] + sels[8]
    for a, c in enumerate([cnt(sels[0]) + cnt(sels[1]), cnt(sels[2]), cnt(sels[3]), cnt(sels[4])]):
        cut_lo = cut_lo + jnp.where(row == a, c, 0.0)
    return cut_lo, sels[6], z, cnt(cut_lo) + cnt(sels[6])


def _peer_route_kernel(x_ref, sh_ref, sc_ref, wq_ref, sk_ref, xm_ref, e1_ref, cut_ref, e2_ref, r2_ref, q_s, *, tm):
    xm = (x_ref[0] * (1.0 + sc_ref[0]) + sh_ref[0]).astype(BF16)
    xm_ref[0] = xm
    q = _dot(xm, wq_ref[...])
    for hp in range(2 * PEER_HEADS):
        q_s[hp] = q[:, hp * PEER_HALF:(hp + 1) * PEER_HALF]

    def route(h, tok, exact):
        def scores(hp):
            return lax.dot_general(sk_ref[hp], q_s[hp, tok, :], (((1,), (1,)), ((), ())),
                                   precision=lax.Precision.HIGHEST, preferred_element_type=F32)

        s1, s2 = scores(2 * h), scores(2 * h + 1)
        av, rank1 = _top16(s1, exact)
        bv, rank2 = _top16(s2, exact)
        cut_lo, cut_hi, z, nsel = _pair_topk(av, bv, exact)
        cut = jnp.zeros_like(s1)
        for r in range(PEER_TOPK):
            src = cut_lo if r < 8 else cut_hi
            cut = jnp.where(rank1 == float(r), src[r % 8:r % 8 + 1, :], cut)
        e1_ref[0, h, :, tok] = (jnp.exp(s1 - av[0]) / z).astype(BF16)
        cut_ref[0, h, :, tok] = cut.astype(BF16)
        e2_ref[0, h, :, tok] = jnp.exp(s2 - bv[0]).astype(BF16)
        r2_ref[0, h, :, tok] = rank2.astype(BF16)
        ranked = lambda rk: jnp.sum(jnp.where(rk < PEER_TOPK, 1.0, 0.0), axis=0, keepdims=True)
        return ranked(rank1), ranked(rank2), nsel

    def body(h, carry):
        toks = [pl.ds(t0, PEER_RT) for t0 in range(0, tm, PEER_RT)]
        counts = [route(h, tok, exact=False) for tok in toks]
        for tok, cnts in zip(toks, counts):
            bad = functools.reduce(jnp.maximum, [jnp.abs(cn - PEER_TOPK) for cn in cnts])

            @pl.when(jnp.max(bad) > 0.0)
            def _():
                route(h, tok, exact=True)
        return carry

    lax.fori_loop(0, PEER_HEADS, body, 0)


def peer_route(x3, mod3, shift_chunk, wq_bf16, subkeys, tm=256):
    nseg, seg, d = x3.shape
    tok = lambda s, i: (s, i, 0)
    rshape = jax.ShapeDtypeStruct((nseg, PEER_HEADS, PEER_NKEYS, seg), BF16)
    rspec = pl.BlockSpec((1, PEER_HEADS, PEER_NKEYS, tm), lambda s, i: (s, 0, 0, i))
    return pl.pallas_call(
        functools.partial(_peer_route_kernel, tm=tm),
        grid=(nseg, seg // tm),
        in_specs=[pl.BlockSpec((1, tm, d), tok),
                  pl.BlockSpec((1, 1, d), lambda s, i: (s, 0, shift_chunk)),
                  pl.BlockSpec((1, 1, d), lambda s, i: (s, 0, shift_chunk + 1)),
                  pl.BlockSpec(wq_bf16.shape, lambda s, i: (0, 0)),
                  pl.BlockSpec((2 * PEER_HEADS, PEER_NKEYS, PEER_HALF), lambda s, i: (0, 0, 0))],
        out_specs=[pl.BlockSpec((1, tm, d), tok), rspec, rspec, rspec, rspec],
        out_shape=[jax.ShapeDtypeStruct((nseg, seg, d), BF16)] + [rshape] * 4,
        scratch_shapes=[pltpu.VMEM((2 * PEER_HEADS, tm, PEER_HALF), F32)],
        compiler_params=_params(("arbitrary", "arbitrary")),
        name="peer_route",
    )(x3, mod3, mod3, wq_bf16, subkeys.reshape(2 * PEER_HEADS, PEER_NKEYS, PEER_HALF))


PEER_CE = 1024


def _gelu_tanh(x):
    return 0.5 * x * (1.0 + jnp.tanh(0.7978845608028654 * (x + 0.044715 * x * x * x)))


def _peer_dense_kernel(xm_ref, u_ref, vt_ref, e1_ref, cut_ref, e2_ref, r2_ref, x_ref, gate_ref, g_ref, b_ref,
                       o_ref, acc_s, at_s, w_s, e2_s, r2_s, e1b_s, cutb_s, *, tm):
    e = pl.program_id(2)
    nb = PEER_CE // PEER_NKEYS
    ntt = tm // PEER_RT

    @pl.when(e == 0)
    def _():
        acc_s[...] = jnp.zeros_like(acc_s)
        e2_s[...] = e2_ref[0]
        r2_s[...] = r2_ref[0]

    for h in range(PEER_HEADS):
        for ii in range(nb):
            e1b_s[ii, h] = jnp.broadcast_to(e1_ref[0, h, ii:ii + 1, :], (16, tm))
            cutb_s[ii, h] = jnp.broadcast_to(cut_ref[0, h, ii:ii + 1, :], (16, tm))

    packed = (PEER_NKEYS // 16, 16, PEER_RT)
    ng = 2
    tw = 2 * PEER_RT

    def activations(tp):
        tok = slice(tp * tw, (tp + 1) * tw)
        at_s[:, tok] = _dot(u_ref[0], xm_ref[0, tok, :], ((1,), (1,)))

    def gate_tiles(tt, i0):
        tok = slice(tt * PEER_RT, (tt + 1) * PEER_RT)
        gmats = [jnp.zeros(packed, BF16) for _ in range(ng)]
        for h in range(PEER_HEADS):
            e2 = e2_s[h, :, tok].reshape(packed)
            r2 = r2_s[h, :, tok].reshape(packed)
            for k in range(ng):
                e1 = e1b_s[i0 + k, h, :, tok][None]
                cut = cutb_s[i0 + k, h, :, tok][None]
                gmats[k] = gmats[k] + e1 * jnp.where(r2 < cut, e2, jnp.zeros_like(e2))
        for k in range(ng):
            rows = slice((i0 + k) * PEER_NKEYS, (i0 + k + 1) * PEER_NKEYS)
            act = _gelu_tanh(at_s[rows, tok]).astype(BF16)
            w_s[rows, tok] = gmats[k].reshape(PEER_NKEYS, PEER_RT) * act

    activations(0)
    for tp in range(tm // tw):
        if (tp + 1) * tw < tm:
            activations(tp + 1)
        for tt in (2 * tp, 2 * tp + 1):
            for i0 in range(0, nb, ng):
                gate_tiles(tt, i0)
        tok = slice(tp * tw, (tp + 1) * tw)
        acc_s[:, tok] += _dot(vt_ref[0], w_s[:, tok])

    @pl.when(e == pl.num_programs(2) - 1)
    def _():
        z = DEEPNORM_ALPHA * x_ref[0] + gate_ref[0] * acc_s[...].T
        o_ref[0] = _layer_norm_rows(z, g_ref[...], b_ref[...])


def peer_dense(xm, u_all, vt_all, l, e1, cut, e2, r2, x3, mod3, gate_chunk, ln_g, ln_b, tm=1024):
    nseg, seg, d = x3.shape
    ne = u_all.shape[1]
    nb = PEER_CE // PEER_NKEYS
    tok = lambda s, i, e: (s, i, 0)
    chunk = pl.BlockSpec((1, PEER_HEADS, nb, tm), lambda s, i, e: (s, 0, e, i))
    full = pl.BlockSpec((1, PEER_HEADS, PEER_NKEYS, tm), lambda s, i, e: (s, 0, 0, i))
    return pl.pallas_call(
        functools.partial(_peer_dense_kernel, tm=tm),
        grid=(nseg, seg // tm, ne // PEER_CE),
        in_specs=[pl.BlockSpec((1, tm, d), tok),
                  pl.BlockSpec((1, PEER_CE, d), lambda s, i, e: (l, e, 0)),
                  pl.BlockSpec((1, d, PEER_CE), lambda s, i, e: (l, 0, e)),
                  chunk, chunk, full, full,
                  pl.BlockSpec((1, tm, d), tok),
                  pl.BlockSpec((1, 1, d), lambda s, i, e: (s, 0, gate_chunk)),
                  pl.BlockSpec((1, d), lambda s, i, e: (0, 0)),
                  pl.BlockSpec((1, d), lambda s, i, e: (0, 0))],
        out_specs=pl.BlockSpec((1, tm, d), tok),
        out_shape=jax.ShapeDtypeStruct((nseg, seg, d), F32),
        scratch_shapes=[pltpu.VMEM((d, tm), F32), pltpu.VMEM((PEER_CE, tm), F32), pltpu.VMEM((PEER_CE, tm), BF16),
                        pltpu.VMEM((PEER_HEADS, PEER_NKEYS, tm), BF16), pltpu.VMEM((PEER_HEADS, PEER_NKEYS, tm), BF16),
                        pltpu.VMEM((nb, PEER_HEADS, 16, tm), BF16), pltpu.VMEM((nb, PEER_HEADS, 16, tm), BF16)],
        compiler_params=_params(("arbitrary", "arbitrary", "arbitrary")),
        name="peer_dense",
    )(xm, u_all, vt_all, e1, cut, e2, r2, x3, mod3, ln_g.reshape(1, d), ln_b.reshape(1, d))


def peer_layer(x3, mod3, l, wq, subkeys, u_all, vt_all, ln_g, ln_b):
    xm, e1, cut, e2, r2 = peer_route(x3, mod3, 3, wq.astype(BF16), subkeys)
    return peer_dense(xm, u_all, vt_all, l, e1, cut, e2, r2, x3, mod3, 5, ln_g, ln_b)


def _pad_cols(w, n):
    return jnp.pad(w, ((0, 0), (0, n - w.shape[1])))


def _stream(prompt_part, sample_part):
    return jnp.concatenate([prompt_part.reshape(1, -1, prompt_part.shape[-1]), sample_part], axis=0)


def _head_major(a, heads):
    b, t, _ = a.shape
    return jnp.transpose(a.reshape(b, t, heads, -1), (0, 2, 1, 3))


def _token_major(a):
    b, h, t, dh = a.shape
    return jnp.transpose(a, (0, 2, 1, 3)).reshape(b, t, h * dh)


MLSTM_CHUNK = 128
GLA_CHUNK = 32
NPROJ = 3200


def mlstm_layer(x3, mod3, bp, lp, st_c, st_n, st_m, w_in, b_gate, norm_w, w_out, ln_g, ln_b):
    nseg, seg, _ = x3.shape
    bs = nseg - 1
    p = mod_matmul(x3, mod3, 0, _pad_cols(w_in, NPROJ).astype(BF16))
    graw = p[:, :, 3072:3088]
    gp = graw[0].reshape(bp, lp, 16)
    zc = jnp.zeros((bp, 8, M_DK, M_DV), F32)
    zn = jnp.zeros((bp, 8, M_DK), F32)
    hfp, hbp, c_new, n_new, m_new = mlstm_scan(p.reshape(nseg * bp, lp, NPROJ), 0, bp, lp, gp,
                                               jnp.swapaxes(gp, 1, 2), b_gate, zc, zn, zn, min(MLSTM_CHUNK, lp))
    gs = graw[1:]
    hfs, hbs, _, _, _ = mlstm_scan(p, 1, bs, seg, gs, jnp.swapaxes(gs, 1, 2), b_gate,
                                   st_c.reshape(bs, 8, M_DK, M_DV), st_n.reshape(bs, 8, M_DK),
                                   jnp.broadcast_to(st_m.reshape(bs, 8, 1), (bs, 8, M_DK)), MLSTM_CHUNK)
    x3 = outproj_ln("mlstm", (_stream(hfp, hfs), _stream(hbp, hbs)), x3, mod3, 2, w_out.astype(BF16), ln_g, ln_b,
                    norm_w=norm_w, og=p, og_col=2)
    return (x3, c_new.reshape(bp, 2, M_HEADS, M_DK, M_DV), n_new.reshape(bp, 2, M_HEADS, M_DK),
            m_new[:, :, 0].reshape(bp, 2, M_HEADS))


def gla_layer(x3, mod3, bp, lp, st_s, w_in, w_gate2, b_gate2, norm_w, w_out, ln_g, ln_b):
    nseg, seg, _ = x3.shape
    bs = nseg - 1
    p = mod_matmul(x3, mod3, 0, _pad_cols(w_in, NPROJ).astype(BF16))
    gr = p[:, :, 3072:3104]
    zs = jnp.zeros((bp, 8, G_DV, G_DK), F32)
    ofp, obp, s_new = gla_scan(p.reshape(nseg * bp, lp, NPROJ), 0, bp, lp, gr[0].reshape(bp, lp, 32),
                               w_gate2, b_gate2, zs, GLA_CHUNK)
    s0t = jnp.swapaxes(st_s.reshape(bs, 8, G_DK, G_DV), -1, -2)
    ofs, obs, _ = gla_scan(p, 1, bs, seg, gr[1:], w_gate2, b_gate2, s0t, GLA_CHUNK)
    x3 = outproj_ln("gla", (_stream(ofp, ofs), _stream(obp, obs)), x3, mod3, 2, w_out.astype(BF16), ln_g, ln_b,
                    norm_w=jnp.tile(norm_w, G_HEADS), og=p, og_col=2)
    return x3, jnp.swapaxes(s_new, -1, -2).reshape(bp, 2, G_HEADS, G_DK, G_DV)


def na_layer(x3, mod3, bp, lp, cache_k, cache_v, w_in, rpb, w_out, ln_g, ln_b):
    nseg, seg, _ = x3.shape
    bs = nseg - 1
    hd = NA_HEADS * NA_HD
    p = mod_matmul(x3, mod3, 0, w_in.astype(BF16))
    pp = p[0].reshape(bp, lp, 3 * hd)
    hm = lambda a: _head_major(a, NA_HEADS).astype(BF16)
    yp = attention(hm(pp[..., :hd]), hm(pp[..., hd:2 * hd]), hm(pp[..., 2 * hd:]), lp)
    ps = p[1:]
    ys = na_attention(hm(ps[..., :hd]), hm(ps[..., hd:2 * hd]), hm(ps[..., 2 * hd:]),
                      hm(cache_k.reshape(bs, -1, hd)), hm(cache_v.reshape(bs, -1, hd)), na_bias_table(rpb))
    x3 = outproj_ln("plain", _stream(_token_major(yp), _token_major(ys)), x3, mod3, 2, w_out.astype(BF16), ln_g, ln_b)
    return (x3, pp[..., hd:2 * hd].reshape(bp, lp, NA_HEADS, NA_HD), pp[..., 2 * hd:].reshape(bp, lp, NA_HEADS, NA_HD))


def _rope_rotated_cols(w):
    q = MLA_ROPE // 4
    return jnp.concatenate([-w[..., q:2 * q], w[..., :q], -w[..., 3 * q:], w[..., 2 * q:3 * q]], axis=-1)


def _rope_tables(ts):
    ra = MLA_ROPE // 2
    t = np.arange(ts)
    inv = 1.0 / (ROPE_BASE ** (np.arange(0, ra, 2, dtype=np.float32) / ra))
    ang_r = (t // GRID_W).astype(np.float32)[:, None] * inv[None, :]
    ang_c = (t % GRID_W).astype(np.float32)[:, None] * inv[None, :]
    ang = np.concatenate([ang_r, ang_r, ang_c, ang_c], axis=-1).astype(np.float32)
    return jnp.cos(jnp.asarray(ang)), jnp.sin(jnp.asarray(ang))


def mla_layer(x3, mod3, bp, lp, cache_ckv, cache_kpe, w_in, q_norm, w_qup, kv_norm, w_kvup, w_out, ln_g, ln_b):
    nseg, seg, _ = x3.shape
    bs = nseg - 1
    nq = MLA_Q_LORA + MLA_KV_LORA
    w_ext = jnp.concatenate([w_in, _rope_rotated_cols(w_in[:, nq:])], axis=1)
    p = mod_matmul(x3, mod3, 0, _pad_cols(w_ext, 896).astype(BF16))
    cos_t, sin_t = _rope_tables(seg)
    cos3 = jnp.concatenate([jnp.ones((1, seg, MLA_ROPE), F32), jnp.broadcast_to(cos_t, (bs, seg, MLA_ROPE))], 0)
    sin3 = jnp.concatenate([jnp.zeros((1, seg, MLA_ROPE), F32), jnp.broadcast_to(sin_t, (bs, seg, MLA_ROPE))], 0)
    wq = w_qup.reshape(MLA_Q_LORA, MLA_HEADS, MLA_NOPE + MLA_ROPE)
    wq_rope = wq[:, :, MLA_NOPE:]
    w_q3 = jnp.concatenate([wq[:, :, :MLA_NOPE].reshape(MLA_Q_LORA, -1), wq_rope.reshape(MLA_Q_LORA, -1),
                            _rope_rotated_cols(wq_rope).reshape(MLA_Q_LORA, -1)], axis=1).astype(BF16)
    q_all = mla_q(p, q_norm, w_q3, jnp.tile(cos3, (1, 1, MLA_HEADS)), jnp.tile(sin3, (1, 1, MLA_HEADS)))
    wkv = w_kvup.reshape(MLA_KV_LORA, MLA_HEADS, MLA_NOPE + MLA_VD)
    w_kv2 = jnp.concatenate([wkv[:, :, :MLA_NOPE].reshape(MLA_KV_LORA, -1),
                             wkv[:, :, MLA_NOPE:].reshape(MLA_KV_LORA, -1)], axis=1).astype(BF16)
    ckvn, kpe, kv = mla_kv(p, kv_norm, w_kv2, cos3, sin3)
    kvc = matmul(cache_ckv.reshape(-1, MLA_KV_LORA), w_kv2, 512).reshape(bs, -1, w_kv2.shape[1])
    nn = MLA_HEADS * MLA_NOPE

    def heads(q_rows, kv_rows, kpe_rows):
        b, t, _ = q_rows.shape
        tk = kv_rows.shape[1]
        qh = jnp.concatenate([q_rows[..., :nn].reshape(b, t, MLA_HEADS, MLA_NOPE),
                              q_rows[..., nn:].reshape(b, t, MLA_HEADS, MLA_ROPE)], -1)
        kh = jnp.concatenate([kv_rows[..., :nn].reshape(b, tk, MLA_HEADS, MLA_NOPE),
                              jnp.broadcast_to(kpe_rows[:, :, None, :], (b, tk, MLA_HEADS, MLA_ROPE))], -1)
        vh = kv_rows[..., nn:].reshape(b, tk, MLA_HEADS, MLA_VD)
        tr = lambda a: jnp.transpose(a, (0, 2, 1, 3)).astype(BF16)
        return tr(qh), tr(kh), tr(vh)

    yp = attention(*heads(q_all[0].reshape(bp, lp, -1), kv[0].reshape(bp, lp, -1), kpe[0].reshape(bp, lp, -1)), lp)
    ys = attention(*heads(q_all[1:], jnp.concatenate([kv[1:], kvc], 1), jnp.concatenate([kpe[1:], cache_kpe], 1)), 256)
    x3 = outproj_ln("plain", _stream(_token_major(yp), _token_major(ys)), x3, mod3, 2, w_out.astype(BF16), ln_g, ln_b)
    return x3, ckvn[0].reshape(bp, lp, MLA_KV_LORA), kpe[0].reshape(bp, lp, MLA_ROPE)


def kernel(x_prompt, x_sample, c, c_ctx, state_mlstm_C, state_mlstm_n, state_mlstm_m, state_gla_S, cache_na_k, cache_na_v, cache_mla_ckv, cache_mla_kpe, ada_w, ada_b, ln_mix_g, ln_mix_b, ln_ffn_g, ln_ffn_b, mlstm_w_in, mlstm_b_gate, mlstm_norm_w, mlstm_w_out, gla_w_in, gla_w_gate2, gla_b_gate2, gla_norm_w, gla_w_out, na_w_in, na_rpb, na_w_out, mla_w_in, mla_q_norm, mla_w_qup, mla_kv_norm, mla_w_kvup, mla_w_out, peer_w_q, peer_subkeys, peer_u, peer_v):
    bp, lp, d = x_prompt.shape
    bs, ts, _ = x_sample.shape
    assert bp * lp == ts and bs + 1 <= 8
    x3 = _stream(x_prompt, x_sample)
    cond8 = jnp.zeros((8, d), F32).at[0].set(c_ctx).at[1:1 + bs].set(c)
    mods = adaln_all(cond8, ada_w, ada_b)
    u_all = peer_u.astype(BF16)
    vt_all = jnp.swapaxes(peer_v, 1, 2).astype(BF16)
    outs = {}
    for l in range(DEPTH):
        mod3 = mods[l].reshape(8, 1, ADA_CHUNKS * d)
        kind = l % 4
        if kind == 0:
            x3, outs["C"], outs["n"], outs["m"] = mlstm_layer(
                x3, mod3, bp, lp, state_mlstm_C, state_mlstm_n, state_mlstm_m, mlstm_w_in, mlstm_b_gate,
                mlstm_norm_w, mlstm_w_out, ln_mix_g[l], ln_mix_b[l])
        elif kind == 1:
            x3, outs["S"] = gla_layer(x3, mod3, bp, lp, state_gla_S, gla_w_in, gla_w_gate2, gla_b_gate2,
                                      gla_norm_w, gla_w_out, ln_mix_g[l], ln_mix_b[l])
        elif kind == 2:
            x3, outs["nk"], outs["nv"] = na_layer(x3, mod3, bp, lp, cache_na_k, cache_na_v, na_w_in, na_rpb,
                                                  na_w_out, ln_mix_g[l], ln_mix_b[l])
        else:
            x3, outs["ckv"], outs["kpe"] = mla_layer(x3, mod3, bp, lp, cache_mla_ckv, cache_mla_kpe, mla_w_in,
                                                     mla_q_norm, mla_w_qup, mla_kv_norm, mla_w_kvup, mla_w_out,
                                                     ln_mix_g[l], ln_mix_b[l])
        x3 = peer_layer(x3, mod3, l, peer_w_q[l], peer_subkeys[l], u_all, vt_all, ln_ffn_g[l], ln_ffn_b[l])
    return (x3[0].reshape(bp, lp, d), x3[1:], outs["C"], outs["n"], outs["m"], outs["S"], outs["nk"], outs["nv"],
            outs["ckv"], outs["kpe"])
```

```python
import functools

import numpy as np
import jax
import jax.numpy as jnp
from jax import lax
from jax.experimental import pallas as pl
from jax.experimental.pallas import tpu as pltpu

D_MODEL = 1024
DEPTH = 4
GRID_W = 64
DEEPNORM_ALPHA = (2.0 * DEPTH) ** 0.25
ADA_CHUNKS = 6
NORM_EPS = 1e-5
SEG = 4096
NSEG = 3

M_HEADS, M_DK, M_DV = 4, 128, 256
G_HEADS, G_DK, G_DV = 4, 128, 256
G_GATE_RANK = 16
G_GATE_NORM = 16.0
NA_HEADS, NA_HD, NA_ROWS, NA_COLS = 16, 64, 8, 16
MLA_HEADS, MLA_Q_LORA, MLA_KV_LORA, MLA_NOPE, MLA_ROPE, MLA_VD = 16, 512, 256, 64, 32, 64
ROPE_BASE = 10000.0
PEER_HEADS, PEER_NKEYS, PEER_HALF, PEER_TOPK = 8, 128, 128, 16

V7X_VMEM_LIMIT = 56 * 1024 * 1024
F32 = jnp.float32
BF16 = jnp.bfloat16
NEG_INF = float("-inf")


def _params(sem, vmem=V7X_VMEM_LIMIT):
    return pltpu.CompilerParams(dimension_semantics=sem, vmem_limit_bytes=vmem)


def _dot(a, b, dims=((1,), (0,))):
    return lax.dot_general(a, b, (dims, ((), ())), preferred_element_type=F32)


def _split3(a):
    hi = a.astype(BF16)
    r1 = a - hi.astype(F32)
    mid = r1.astype(BF16)
    lo = (r1 - mid.astype(F32)).astype(BF16)
    return hi, mid, lo


def _dot_exact_lhs(m01, a):
    hi, mid, lo = _split3(a)
    return _dot(m01, hi) + _dot(m01, mid) + _dot(m01, lo)


def _dot_exact_rhs(a, m01):
    hi, mid, lo = _split3(a)
    return _dot(hi, m01) + _dot(mid, m01) + _dot(lo, m01)


def _log_sigmoid(x):
    return jnp.minimum(x, 0.0) - jnp.log(1.0 + jnp.exp(-jnp.abs(x)))


def _sigmoid(x):
    return 1.0 / (1.0 + jnp.exp(-x))


def _adaln_kernel(c_ref, w_ref, b_ref, o_ref):
    cv = c_ref[...]
    a = cv * _sigmoid(cv)
    o_ref[0] = lax.dot_general(a, w_ref[0], (((1,), (0,)), ((), ())), precision=lax.Precision.HIGHEST,
                               preferred_element_type=F32) + b_ref[0]


def adaln_all(cond8, ada_w, ada_b):
    tn = 1024
    n = ada_w.shape[-1]
    return pl.pallas_call(
        _adaln_kernel,
        grid=(DEPTH, n // tn),
        in_specs=[pl.BlockSpec((8, D_MODEL), lambda l, j: (0, 0)),
                  pl.BlockSpec((1, D_MODEL, tn), lambda l, j: (l, 0, j)),
                  pl.BlockSpec((1, 1, tn), lambda l, j: (l, 0, j))],
        out_specs=pl.BlockSpec((1, 8, tn), lambda l, j: (l, 0, j)),
        out_shape=jax.ShapeDtypeStruct((DEPTH, 8, n), F32),
        compiler_params=_params(("arbitrary", "arbitrary")),
        name="adaln",
    )(cond8, ada_w, ada_b.reshape(DEPTH, 1, n))


def _modmm_kernel(x_ref, sh_ref, sc_ref, w_ref, o_ref, xm_ref):
    @pl.when(pl.program_id(2) == 0)
    def _():
        xm_ref[...] = (x_ref[0] * (1.0 + sc_ref[0]) + sh_ref[0]).astype(BF16)

    o_ref[0] = _dot(xm_ref[...], w_ref[...]).astype(o_ref.dtype)


def mod_matmul(x3, mod3, shift_chunk, w_bf16, tm=512, tn=None, out_dtype=F32):
    nseg, seg, d = x3.shape
    n = w_bf16.shape[1]
    tn = n if tn is None else tn
    return pl.pallas_call(
        _modmm_kernel,
        grid=(nseg, seg // tm, n // tn),
        in_specs=[pl.BlockSpec((1, tm, d), lambda s, i, j: (s, i, 0)),
                  pl.BlockSpec((1, 1, d), lambda s, i, j: (s, 0, shift_chunk)),
                  pl.BlockSpec((1, 1, d), lambda s, i, j: (s, 0, shift_chunk + 1)),
                  pl.BlockSpec((d, tn), lambda s, i, j: (0, j))],
        out_specs=pl.BlockSpec((1, tm, tn), lambda s, i, j: (s, i, j)),
        out_shape=jax.ShapeDtypeStruct((nseg, seg, n), out_dtype),
        scratch_shapes=[pltpu.VMEM((tm, d), BF16)],
        compiler_params=_params(("arbitrary", "arbitrary", "arbitrary")),
        name="mod_matmul",
    )(x3, mod3, mod3, w_bf16)


def _layer_norm_rows(y, g, b):
    mu = jnp.mean(y, axis=-1, keepdims=True)
    yc = y - mu
    var = jnp.mean(yc * yc, axis=-1, keepdims=True)
    return yc * lax.rsqrt(var + NORM_EPS) * g + b


def _outproj_kernel(*refs, mode):
    if mode == "plain":
        y_ref, x_ref, gate_ref, w_ref, g_ref, b_ref, o_ref = refs
        yin = y_ref[0].astype(BF16)
    else:
        ya_ref, yb_ref, og_ref, nw_ref, x_ref, gate_ref, w_ref, g_ref, b_ref, o_ref = refs
        hs = ya_ref[0] + yb_ref[0]
        og = og_ref[0]
        parts = []
        for h in range(4):
            seg = hs[:, h * 256:(h + 1) * 256]
            nw = nw_ref[:, h * 256:(h + 1) * 256]
            if mode == "mlstm":
                mu = jnp.mean(seg, axis=-1, keepdims=True)
                sc = seg - mu
                var = jnp.mean(sc * sc, axis=-1, keepdims=True)
                parts.append(sc * lax.rsqrt(var + NORM_EPS) * nw)
            else:
                ms = jnp.mean(seg * seg, axis=-1, keepdims=True)
                parts.append(seg * lax.rsqrt(ms + NORM_EPS) * nw)
        hn = jnp.concatenate(parts, axis=-1)
        act = _sigmoid(og) if mode == "mlstm" else og * _sigmoid(og)
        yin = (act * hn).astype(BF16)
    y = _dot(yin, w_ref[...])
    z = DEEPNORM_ALPHA * x_ref[0] + gate_ref[0] * y
    o_ref[0] = _layer_norm_rows(z, g_ref[...], b_ref[...])


def outproj_ln(mode, ys, x3, mod3, gate_chunk, w_bf16, ln_g, ln_b, norm_w=None, og=None, og_col=0, tm=512):
    nseg, seg, d = x3.shape
    k = w_bf16.shape[0]
    tok = lambda s, i: (s, i, 0)
    if mode == "plain":
        args = [ys]
        specs = [pl.BlockSpec((1, tm, k), tok)]
    else:
        args = [ys[0], ys[1], og, norm_w.reshape(1, k)]
        specs = [pl.BlockSpec((1, tm, k), tok), pl.BlockSpec((1, tm, k), tok),
                 pl.BlockSpec((1, tm, k), lambda s, i: (s, i, og_col)),
                 pl.BlockSpec((1, k), lambda s, i: (0, 0))]
    args += [x3, mod3, w_bf16, ln_g.reshape(1, d), ln_b.reshape(1, d)]
    specs += [pl.BlockSpec((1, tm, d), tok),
              pl.BlockSpec((1, 1, d), lambda s, i: (s, 0, gate_chunk)),
              pl.BlockSpec((k, d), lambda s, i: (0, 0)),
              pl.BlockSpec((1, d), lambda s, i: (0, 0)),
              pl.BlockSpec((1, d), lambda s, i: (0, 0))]
    return pl.pallas_call(
        functools.partial(_outproj_kernel, mode=mode),
        grid=(nseg, seg // tm),
        in_specs=specs,
        out_specs=pl.BlockSpec((1, tm, d), tok),
        out_shape=jax.ShapeDtypeStruct((nseg, seg, d), F32),
        compiler_params=_params(("arbitrary", "arbitrary")),
        name="outproj_ln_" + mode,
    )(*args)


def _tri(n, lower):
    r = lax.broadcasted_iota(jnp.int32, (n, n), 0)
    c = lax.broadcasted_iota(jnp.int32, (n, n), 1)
    return (c <= r) if lower else (c >= r)


def _mlstm_kernel(pf_ref, pb_ref, gf_ref, gb_ref, gtf_ref, gtb_ref, bias_ref, biast_ref,
                  c0_ref, n0_ref, m0_ref, hf_ref, hb_ref, co_ref, no_ref, mo_ref,
                  c_s, n_s, m_s, *, L):
    c = pl.program_id(1)

    @pl.when(c == 0)
    def _():
        c_s[...] = c0_ref[0]
        n_s[...] = n0_ref[0]
        m_s[...] = m0_ref[0]

    for d in range(2):
        p_ref, g_ref, gt_ref, h_ref = ((pf_ref, gf_ref, gtf_ref, hf_ref) if d == 0
                                       else (pb_ref, gb_ref, gtb_ref, hb_ref))
        mask = _tri(L, lower=(d == 0))
        mcol = mask.astype(BF16)
        mrow = _tri(L, lower=(d != 0)).astype(BF16)
        g = g_ref[0] + bias_ref[...]
        gt = gt_ref[0] + biast_ref[...]
        li_c = g[:, d * 8:d * 8 + 4]
        lf_c = _log_sigmoid(g[:, d * 8 + 4:d * 8 + 8])
        li_r = gt[d * 8:d * 8 + 4, :]
        lf_r = _log_sigmoid(gt[d * 8 + 4:d * 8 + 8, :])
        b_c = _dot_exact_lhs(mcol, lf_c)
        b_r = _dot_exact_rhs(lf_r, mrow)
        last = L - 1 if d == 0 else 0
        for h in range(M_HEADS):
            u = d * M_HEADS + h
            q = p_ref[0, :, h * M_DK:(h + 1) * M_DK]
            k = p_ref[0, :, 512 + h * M_DK:512 + (h + 1) * M_DK] * (M_DK ** -0.5)
            v = p_ref[0, :, 1024 + h * M_DV:1024 + (h + 1) * M_DV].astype(BF16)
            qb = q.astype(BF16)
            bc, br = b_c[:, h:h + 1], b_r[h:h + 1, :]
            lic, lir = li_c[:, h:h + 1], li_r[h:h + 1, :]
            m_prev = m_s[u:u + 1, 0:1]
            dmat = jnp.where(mask, bc - br + lir, NEG_INF)
            inter = bc + m_prev
            mt = jnp.maximum(inter, jnp.max(dmat, axis=-1, keepdims=True))
            smat = _dot(qb, k.astype(BF16), ((1,), (1,))) * jnp.exp(dmat - mt)
            ei = jnp.exp(inter - mt)
            cmat = c_s[u]
            num = _dot(smat.astype(BF16), v) + ei * _dot(qb, cmat.astype(BF16))
            nrow = n_s[u:u + 1, :]
            den = jnp.sum(smat, axis=-1, keepdims=True) + ei * jnp.sum(q * nrow, axis=-1, keepdims=True)
            h_ref[0, :, h * M_DV:(h + 1) * M_DV] = num / jnp.maximum(jnp.abs(den), jnp.exp(-mt))
            tot = br[:, last:last + 1]
            g_c = tot - bc + lic
            g_r = tot - br + lir
            m_new = jnp.maximum(tot + m_prev, jnp.max(g_r, axis=-1, keepdims=True))
            kw = k * jnp.exp(g_c - m_new)
            dec = jnp.exp(tot + m_prev - m_new)
            c_s[u] = dec * cmat + _dot(kw.astype(BF16), v, ((0,), (0,)))
            n_s[u:u + 1, :] = dec * nrow + jnp.sum(kw, axis=0, keepdims=True)
            m_s[u:u + 1, :] = jnp.broadcast_to(m_new, (1, 128))

    @pl.when(c == pl.num_programs(1) - 1)
    def _():
        co_ref[0] = c_s[...]
        no_ref[0] = n_s[...]
        mo_ref[0] = m_s[...]


def mlstm_scan(p, b0, nb, t, g, gt, bias, c0, n0, m0, L):
    nc = t // L
    hshape = jax.ShapeDtypeStruct((nb, t, M_HEADS * M_DV), F32)
    fwd = lambda b, c: (b + b0, c, 0)
    bwd = lambda b, c: (b + b0, nc - 1 - c, 0)
    st4 = lambda b, c: (b, 0, 0, 0)
    st3 = lambda b, c: (b, 0, 0)
    return pl.pallas_call(
        functools.partial(_mlstm_kernel, L=L),
        grid=(nb, nc),
        in_specs=[pl.BlockSpec((1, L, 2048), fwd), pl.BlockSpec((1, L, 2048), bwd),
                  pl.BlockSpec((1, L, 16), lambda b, c: (b, c, 0)),
                  pl.BlockSpec((1, L, 16), lambda b, c: (b, nc - 1 - c, 0)),
                  pl.BlockSpec((1, 16, L), lambda b, c: (b, 0, c)),
                  pl.BlockSpec((1, 16, L), lambda b, c: (b, 0, nc - 1 - c)),
                  pl.BlockSpec((1, 16), lambda b, c: (0, 0)),
                  pl.BlockSpec((16, 1), lambda b, c: (0, 0)),
                  pl.BlockSpec((1, 8, M_DK, M_DV), st4),
                  pl.BlockSpec((1, 8, M_DK), st3),
                  pl.BlockSpec((1, 8, M_DK), st3)],
        out_specs=[pl.BlockSpec((1, L, 1024), lambda b, c: (b, c, 0)),
                   pl.BlockSpec((1, L, 1024), lambda b, c: (b, nc - 1 - c, 0)),
                   pl.BlockSpec((1, 8, M_DK, M_DV), st4),
                   pl.BlockSpec((1, 8, M_DK), st3),
                   pl.BlockSpec((1, 8, M_DK), st3)],
        out_shape=[hshape, hshape,
                   jax.ShapeDtypeStruct((nb, 8, M_DK, M_DV), F32),
                   jax.ShapeDtypeStruct((nb, 8, M_DK), F32),
                   jax.ShapeDtypeStruct((nb, 8, M_DK), F32)],
        scratch_shapes=[pltpu.VMEM((8, M_DK, M_DV), F32), pltpu.VMEM((8, M_DK), F32),
                        pltpu.VMEM((8, M_DK), F32)],
        compiler_params=_params(("arbitrary", "arbitrary")),
        name="mlstm_scan",
    )(p, p, g, g, gt, gt, bias.reshape(1, 16), bias.reshape(16, 1), c0, n0, m0)


def _gla_kernel(pf_ref, pb_ref, gf_ref, gb_ref, w2_ref, b2_ref, s0_ref, of_ref, ob_ref, so_ref, s_s, *, L):
    c = pl.program_id(1)

    @pl.when(c == 0)
    def _():
        s_s[...] = s0_ref[0]

    for d in range(2):
        p_ref, g_ref, o_ref = (pf_ref, gf_ref, of_ref) if d == 0 else (pb_ref, gb_ref, ob_ref)
        mask = _tri(L, lower=(d == 0))
        mcol = mask.astype(BF16)
        gr = g_ref[0][:, d * G_GATE_RANK:(d + 1) * G_GATE_RANK]
        pre = lax.dot_general(gr, w2_ref[d], (((1,), (0,)), ((), ())), precision=lax.Precision.HIGHEST,
                              preferred_element_type=F32) + b2_ref[d]
        la = _log_sigmoid(pre) * (1.0 / G_GATE_NORM)
        bc_all = _dot_exact_lhs(mcol, la)
        last = L - 1 if d == 0 else 0
        for h in range(G_HEADS):
            u = d * G_HEADS + h
            q = p_ref[0, :, h * G_DK:(h + 1) * G_DK] * (G_DK ** -0.5)
            k = p_ref[0, :, 512 + h * G_DK:512 + (h + 1) * G_DK]
            v = p_ref[0, :, 1024 + h * G_DV:1024 + (h + 1) * G_DV].astype(BF16)
            bc = bc_all[:, h * G_DK:(h + 1) * G_DK]
            qd = (q * jnp.exp(bc)).astype(BF16)
            kd = (k * jnp.exp(-bc)).astype(BF16)
            a = jnp.where(mask, _dot(qd, kd, ((1,), (1,))), 0.0)
            st = s_s[u]
            o_ref[0, :, h * G_DV:(h + 1) * G_DV] = (_dot(a.astype(BF16), v)
                                                    + _dot(qd, st.astype(BF16), ((1,), (1,))))
            bl = bc[last:last + 1, :]
            kl = (k * jnp.exp(bl - bc)).astype(BF16)
            s_s[u] = st * jnp.exp(bl) + _dot(v, kl, ((0,), (0,)))

    @pl.when(c == pl.num_programs(1) - 1)
    def _():
        so_ref[0] = s_s[...]


def gla_scan(p, b0, nb, t, gr, w2, b2, s0t, L):
    nc = t // L
    oshape = jax.ShapeDtypeStruct((nb, t, G_HEADS * G_DV), F32)
    st4 = lambda b, c: (b, 0, 0, 0)
    return pl.pallas_call(
        functools.partial(_gla_kernel, L=L),
        grid=(nb, nc),
        in_specs=[pl.BlockSpec((1, L, 2048), lambda b, c: (b + b0, c, 0)),
                  pl.BlockSpec((1, L, 2048), lambda b, c: (b + b0, nc - 1 - c, 0)),
                  pl.BlockSpec((1, L, 32), lambda b, c: (b, c, 0)),
                  pl.BlockSpec((1, L, 32), lambda b, c: (b, nc - 1 - c, 0)),
                  pl.BlockSpec((2, G_GATE_RANK, 512), lambda b, c: (0, 0, 0)),
                  pl.BlockSpec((2, 1, 512), lambda b, c: (0, 0, 0)),
                  pl.BlockSpec((1, 8, G_DV, G_DK), st4)],
        out_specs=[pl.BlockSpec((1, L, 1024), lambda b, c: (b, c, 0)),
                   pl.BlockSpec((1, L, 1024), lambda b, c: (b, nc - 1 - c, 0)),
                   pl.BlockSpec((1, 8, G_DV, G_DK), st4)],
        out_shape=[oshape, oshape, jax.ShapeDtypeStruct((nb, 8, G_DV, G_DK), F32)],
        scratch_shapes=[pltpu.VMEM((8, G_DV, G_DK), F32)],
        compiler_params=_params(("arbitrary", "arbitrary")),
        name="gla_scan",
    )(p, p, gr, gr, w2, b2.reshape(2, 1, 512), s0t)


def _attn_kernel(q_ref, k_ref, v_ref, o_ref, *, scale):
    s = _dot(q_ref[0, 0], k_ref[0, 0], ((1,), (1,))) * scale
    m = jnp.max(s, axis=-1, keepdims=True)
    p = jnp.exp(s - m)
    l = jnp.sum(p, axis=-1, keepdims=True)
    o_ref[0, 0] = _dot(p.astype(BF16), v_ref[0, 0]) / l


def attention(q, k, v, tq):
    b, h, lq, dq = q.shape
    lk, dv = k.shape[2], v.shape[3]
    return pl.pallas_call(
        functools.partial(_attn_kernel, scale=dq ** -0.5),
        grid=(b, h, lq // tq),
        in_specs=[pl.BlockSpec((1, 1, tq, dq), lambda b, h, i: (b, h, i, 0)),
                  pl.BlockSpec((1, 1, lk, dq), lambda b, h, i: (b, h, 0, 0)),
                  pl.BlockSpec((1, 1, lk, dv), lambda b, h, i: (b, h, 0, 0))],
        out_specs=pl.BlockSpec((1, 1, tq, dv), lambda b, h, i: (b, h, i, 0)),
        out_shape=jax.ShapeDtypeStruct((b, h, lq, dv), F32),
        compiler_params=_params(("arbitrary", "arbitrary", "arbitrary")),
        name="attention",
    )(q, k, v)


NA_RB = 8


def _na_kernel(q_ref, k_ref, v_ref, kc_ref, vc_ref, bias_ref, o_ref, *, rows):
    j = pl.program_id(2)
    scale = NA_HD ** -0.5
    kc, vc = kc_ref[0, 0], vc_ref[0, 0]
    for a in range(NA_RB):
        r = j * NA_RB + a
        start = jnp.clip(r - NA_ROWS // 2, 0, rows - NA_ROWS)
        dr0 = start - r + (NA_ROWS - 1)
        off = pl.multiple_of(start * GRID_W, GRID_W)
        qa = q_ref[0, 0, a * GRID_W:(a + 1) * GRID_W, :]
        kl = k_ref[0, 0, pl.ds(off, NA_ROWS * GRID_W), :]
        vl = v_ref[0, 0, pl.ds(off, NA_ROWS * GRID_W), :]
        s_loc = _dot(qa, kl, ((1,), (1,))) * scale + bias_ref[0, dr0]
        s_ctx = _dot(qa, kc, ((1,), (1,))) * scale
        m = jnp.maximum(jnp.max(s_loc, axis=-1, keepdims=True), jnp.max(s_ctx, axis=-1, keepdims=True))
        p_loc = jnp.exp(s_loc - m)
        p_ctx = jnp.exp(s_ctx - m)
        l = jnp.sum(p_loc, axis=-1, keepdims=True) + jnp.sum(p_ctx, axis=-1, keepdims=True)
        o = _dot(p_loc.astype(BF16), vl) + _dot(p_ctx.astype(BF16), vc)
        o_ref[0, 0, a * GRID_W:(a + 1) * GRID_W, :] = o / l


def na_bias_table(rpb):
    cq = np.arange(GRID_W)[:, None]
    ck = np.arange(GRID_W)[None, :]
    cs = np.clip(cq - NA_COLS // 2, 0, GRID_W - NA_COLS)
    ok = (ck >= cs) & (ck < cs + NA_COLS)
    dc = np.clip(ck - cq, -(NA_COLS - 1), NA_COLS - 1) + (NA_COLS - 1)
    t = jnp.where(ok[None, None], rpb.astype(F32)[:, :, dc], NEG_INF)
    rows = np.arange(NA_ROWS)[:, None] + np.arange(NA_ROWS)[None, :]
    tf = t[:, rows]
    return jnp.transpose(tf, (0, 1, 3, 2, 4)).reshape(NA_HEADS, NA_ROWS, GRID_W, NA_ROWS * GRID_W)


def na_attention(q, k, v, kc, vc, bias):
    b, h, t, dh = q.shape
    lc = kc.shape[2]
    rows = t // GRID_W
    full = lambda b, h, j: (b, h, 0, 0)
    return pl.pallas_call(
        functools.partial(_na_kernel, rows=rows),
        grid=(b, h, rows // NA_RB),
        in_specs=[pl.BlockSpec((1, 1, NA_RB * GRID_W, dh), lambda b, h, j: (b, h, j, 0)),
                  pl.BlockSpec((1, 1, t, dh), full), pl.BlockSpec((1, 1, t, dh), full),
                  pl.BlockSpec((1, 1, lc, dh), full), pl.BlockSpec((1, 1, lc, dh), full),
                  pl.BlockSpec((1, NA_ROWS, GRID_W, NA_ROWS * GRID_W), lambda b, h, j: (h, 0, 0, 0))],
        out_specs=pl.BlockSpec((1, 1, NA_RB * GRID_W, dh), lambda b, h, j: (b, h, j, 0)),
        out_shape=jax.ShapeDtypeStruct((b, h, t, dh), F32),
        compiler_params=_params(("arbitrary", "arbitrary", "arbitrary")),
        name="na_attention",
    )(q, k, v, kc, vc, bias)


def _rms_rows(x, g):
    return x * lax.rsqrt(jnp.mean(x * x, axis=-1, keepdims=True) + NORM_EPS) * g


def _mla_q_kernel(cq_ref, g_ref, w_ref, cos_ref, sin_ref, o_ref):
    r = _dot(_rms_rows(cq_ref[0], g_ref[...]).astype(BF16), w_ref[...])
    nn = MLA_HEADS * MLA_NOPE
    nr = MLA_HEADS * MLA_ROPE
    o_ref[0, :, :nn] = r[:, :nn]
    o_ref[0, :, nn:] = r[:, nn:nn + nr] * cos_ref[0] + r[:, nn + nr:] * sin_ref[0]


def mla_q(p, q_norm, w_q3, cos_q, sin_q, tm=512):
    nseg, seg, _ = p.shape
    nout = MLA_HEADS * (MLA_NOPE + MLA_ROPE)
    nr = MLA_HEADS * MLA_ROPE
    tok = lambda s, i: (s, i, 0)
    return pl.pallas_call(
        _mla_q_kernel,
        grid=(nseg, seg // tm),
        in_specs=[pl.BlockSpec((1, tm, MLA_Q_LORA), tok),
                  pl.BlockSpec((1, MLA_Q_LORA), lambda s, i: (0, 0)),
                  pl.BlockSpec(w_q3.shape, lambda s, i: (0, 0)),
                  pl.BlockSpec((1, tm, nr), tok), pl.BlockSpec((1, tm, nr), tok)],
        out_specs=pl.BlockSpec((1, tm, nout), tok),
        out_shape=jax.ShapeDtypeStruct((nseg, seg, nout), F32),
        compiler_params=_params(("arbitrary", "arbitrary")),
        name="mla_q",
    )(p, q_norm.reshape(1, -1), w_q3, cos_q, sin_q)


def _mla_kv_kernel(ckv_ref, kpe_ref, g_ref, w_ref, cos_ref, sin_ref, ckvn_ref, kpeo_ref, kv_ref):
    cn = _rms_rows(ckv_ref[0], g_ref[...])
    ckvn_ref[0] = cn
    kv_ref[0] = _dot(cn.astype(BF16), w_ref[...])
    kp = kpe_ref[0]
    kpeo_ref[0] = kp[:, :MLA_ROPE] * cos_ref[0] + kp[:, MLA_ROPE:2 * MLA_ROPE] * sin_ref[0]


def mla_kv(p, kv_norm, w_kv, cos_k, sin_k, tm=512):
    nseg, seg, _ = p.shape
    nkv = w_kv.shape[1]
    tok = lambda s, i: (s, i, 0)
    return pl.pallas_call(
        _mla_kv_kernel,
        grid=(nseg, seg // tm),
        in_specs=[pl.BlockSpec((1, tm, MLA_KV_LORA), lambda s, i: (s, i, MLA_Q_LORA // MLA_KV_LORA)),
                  pl.BlockSpec((1, tm, 128), lambda s, i: (s, i, (MLA_Q_LORA + MLA_KV_LORA) // 128)),
                  pl.BlockSpec((1, MLA_KV_LORA), lambda s, i: (0, 0)),
                  pl.BlockSpec(w_kv.shape, lambda s, i: (0, 0)),
                  pl.BlockSpec((1, tm, MLA_ROPE), tok), pl.BlockSpec((1, tm, MLA_ROPE), tok)],
        out_specs=[pl.BlockSpec((1, tm, MLA_KV_LORA), tok), pl.BlockSpec((1, tm, MLA_ROPE), tok),
                   pl.BlockSpec((1, tm, nkv), tok)],
        out_shape=[jax.ShapeDtypeStruct((nseg, seg, MLA_KV_LORA), F32),
                   jax.ShapeDtypeStruct((nseg, seg, MLA_ROPE), F32),
                   jax.ShapeDtypeStruct((nseg, seg, nkv), F32)],
        compiler_params=_params(("arbitrary", "arbitrary")),
        name="mla_kv",
    )(p, p, kv_norm.reshape(1, -1), w_kv, cos_k, sin_k)


def _mm_kernel(a_ref, w_ref, o_ref):
    o_ref[...] = _dot(a_ref[...].astype(BF16), w_ref[...])


def matmul(a, w_bf16, tm):
    m, k = a.shape
    n = w_bf16.shape[1]
    return pl.pallas_call(
        _mm_kernel,
        grid=(m // tm,),
        in_specs=[pl.BlockSpec((tm, k), lambda i: (i, 0)), pl.BlockSpec((k, n), lambda i: (0, 0))],
        out_specs=pl.BlockSpec((tm, n), lambda i: (i, 0)),
        out_shape=jax.ShapeDtypeStruct((m, n), F32),
        compiler_params=_params(("arbitrary",)),
        name="matmul",
    )(a, w_bf16)


PEER_RT = 128
NOT_TOP = 99.0


def _top16(s, exact):
    key = lax.broadcasted_iota(jnp.int32, s.shape, 0).astype(F32)
    rank = jnp.full(s.shape, NOT_TOP, F32)
    vals = []
    for r in range(PEER_TOPK):
        m = jnp.max(s, axis=0, keepdims=True)
        hit = s == m
        if exact:
            hit = key == jnp.min(jnp.where(hit, key, 1e9), axis=0, keepdims=True)
        rank = jnp.where(hit, float(r), rank)
        s = jnp.where(hit, NEG_INF, s)
        vals.append(m)
    return vals, rank


def _pair_topk(av, bv, exact):
    n = av[0].shape[-1]
    a_lo, a_hi = jnp.concatenate(av[:8], 0), jnp.concatenate(av[8:], 0)
    b_lo, b_hi = jnp.concatenate(bv[:8], 0), jnp.concatenate(bv[8:], 0)
    row = lax.broadcasted_iota(jnp.int32, (8, n), 0).astype(F32)

    no_pos = 1e8

    def rows_b(a, b_blk, boff, nvalid):
        ok = row < nvalid
        return jnp.where(ok, av[a] + b_blk, NEG_INF), jnp.where(ok, a * 16.0 + boff + row, no_pos)

    def rows_a(b, a_blk, aoff, lo, hi):
        ok = (row >= lo) & (row < hi)
        return jnp.where(ok, a_blk + bv[b], NEG_INF), jnp.where(ok, (aoff + row) * 16.0 + b, no_pos)

    groups = [rows_b(0, b_lo, 0, 8), rows_b(0, b_hi, 8, 8), rows_b(1, b_lo, 0, 8), rows_b(2, b_lo, 0, 5),
              rows_b(3, b_lo, 0, 4), rows_a(0, a_lo, 0, 4, 8), rows_a(0, a_hi, 8, 0, 8),
              rows_a(1, a_lo, 0, 4, 8), rows_a(2, a_lo, 0, 4, 5)]
    cands = [g[0] for g in groups]
    poss = [g[1] for g in groups]
    sels = [jnp.zeros((8, n), F32) for _ in groups]
    top = av[0] + bv[0]
    z = jnp.zeros((1, n), F32)
    for _ in range(PEER_TOPK):
        m = functools.reduce(jnp.maximum, cands)
        m = jnp.max(m, axis=0, keepdims=True)
        hits = [c == m for c in cands]
        if exact:
            first = functools.reduce(jnp.minimum, [jnp.where(hh, p, 1e9) for hh, p in zip(hits, poss)])
            first = jnp.min(first, axis=0, keepdims=True)
            hits = [p == first for p in poss]
        cands = [jnp.where(hh, NEG_INF, c) for hh, c in zip(hits, cands)]
        sels = [jnp.where(hh, 1.0, s) for hh, s in zip(hits, sels)]
        z = z + jnp.exp(m - top)
    cnt = lambda x: jnp.sum(x, axis=0, keepdims=True)
    cut_lo = sels[5] + sels[7] + sels[8]
    for a, c in enumerate([cnt(sels[0]) + cnt(sels[1]), cnt(sels[2]), cnt(sels[3]), cnt(sels[4])]):
        cut_lo = cut_lo + jnp.where(row == a, c, 0.0)
    return cut_lo, sels[6], z, cnt(cut_lo) + cnt(sels[6])


def _peer_route_kernel(x_ref, sh_ref, sc_ref, wq_ref, sk_ref, xm_ref, e1_ref, cut_ref, e2_ref, r2_ref, q_s, *, tm):
    xm = (x_ref[0] * (1.0 + sc_ref[0]) + sh_ref[0]).astype(BF16)
    xm_ref[0] = xm
    q = _dot(xm, wq_ref[...])
    for hp in range(2 * PEER_HEADS):
        q_s[hp] = q[:, hp * PEER_HALF:(hp + 1) * PEER_HALF]

    def route(h, tok, exact):
        def scores(hp):
            return lax.dot_general(sk_ref[hp], q_s[hp, tok, :], (((1,), (1,)), ((), ())),
                                   precision=lax.Precision.HIGHEST, preferred_element_type=F32)

        s1, s2 = scores(2 * h), scores(2 * h + 1)
        av, rank1 = _top16(s1, exact)
        bv, rank2 = _top16(s2, exact)
        cut_lo, cut_hi, z, nsel = _pair_topk(av, bv, exact)
        cut = jnp.zeros_like(s1)
        for r in range(PEER_TOPK):
            src = cut_lo if r < 8 else cut_hi
            cut = jnp.where(rank1 == float(r), src[r % 8:r % 8 + 1, :], cut)
        e1_ref[0, h, :, tok] = (jnp.exp(s1 - av[0]) / z).astype(BF16)
        cut_ref[0, h, :, tok] = cut.astype(BF16)
        e2_ref[0, h, :, tok] = jnp.exp(s2 - bv[0]).astype(BF16)
        r2_ref[0, h, :, tok] = rank2.astype(BF16)
        ranked = lambda rk: jnp.sum(jnp.where(rk < PEER_TOPK, 1.0, 0.0), axis=0, keepdims=True)
        return ranked(rank1), ranked(rank2), nsel

    def body(h, carry):
        toks = [pl.ds(t0, PEER_RT) for t0 in range(0, tm, PEER_RT)]
        counts = [route(h, tok, exact=False) for tok in toks]
        for tok, cnts in zip(toks, counts):
            bad = functools.reduce(jnp.maximum, [jnp.abs(cn - PEER_TOPK) for cn in cnts])

            @pl.when(jnp.max(bad) > 0.0)
            def _():
                route(h, tok, exact=True)
        return carry

    lax.fori_loop(0, PEER_HEADS, body, 0)


def peer_route(x3, mod3, shift_chunk, wq_bf16, subkeys, tm=256):
    nseg, seg, d = x3.shape
    tok = lambda s, i: (s, i, 0)
    rshape = jax.ShapeDtypeStruct((nseg, PEER_HEADS, PEER_NKEYS, seg), BF16)
    rspec = pl.BlockSpec((1, PEER_HEADS, PEER_NKEYS, tm), lambda s, i: (s, 0, 0, i))
    return pl.pallas_call(
        functools.partial(_peer_route_kernel, tm=tm),
        grid=(nseg, seg // tm),
        in_specs=[pl.BlockSpec((1, tm, d), tok),
                  pl.BlockSpec((1, 1, d), lambda s, i: (s, 0, shift_chunk)),
                  pl.BlockSpec((1, 1, d), lambda s, i: (s, 0, shift_chunk + 1)),
                  pl.BlockSpec(wq_bf16.shape, lambda s, i: (0, 0)),
                  pl.BlockSpec((2 * PEER_HEADS, PEER_NKEYS, PEER_HALF), lambda s, i: (0, 0, 0))],
        out_specs=[pl.BlockSpec((1, tm, d), tok), rspec, rspec, rspec, rspec],
        out_shape=[jax.ShapeDtypeStruct((nseg, seg, d), BF16)] + [rshape] * 4,
        scratch_shapes=[pltpu.VMEM((2 * PEER_HEADS, tm, PEER_HALF), F32)],
        compiler_params=_params(("arbitrary", "arbitrary")),
        name="peer_route",
    )(x3, mod3, mod3, wq_bf16, subkeys.reshape(2 * PEER_HEADS, PEER_NKEYS, PEER_HALF))


PEER_CE = 1024


def _gelu_tanh(x):
    return 0.5 * x * (1.0 + jnp.tanh(0.7978845608028654 * (x + 0.044715 * x * x * x)))


def _peer_dense_kernel(xm_ref, u_ref, vt_ref, e1_ref, cut_ref, e2_ref, r2_ref, x_ref, gate_ref, g_ref, b_ref,
                       o_ref, acc_s, at_s, w_s, e2_s, r2_s, e1b_s, cutb_s, *, tm):
    e = pl.program_id(2)
    nb = PEER_CE // PEER_NKEYS
    ntt = tm // PEER_RT

    @pl.when(e == 0)
    def _():
        acc_s[...] = jnp.zeros_like(acc_s)
        e2_s[...] = e2_ref[0]
        r2_s[...] = r2_ref[0]

    for h in range(PEER_HEADS):
        for ii in range(nb):
            e1b_s[ii, h] = jnp.broadcast_to(e1_ref[0, h, ii:ii + 1, :], (16, tm))
            cutb_s[ii, h] = jnp.broadcast_to(cut_ref[0, h, ii:ii + 1, :], (16, tm))

    packed = (PEER_NKEYS // 16, 16, PEER_RT)
    ng = 2
    tw = 2 * PEER_RT

    def activations(tp):
        tok = slice(tp * tw, (tp + 1) * tw)
        at_s[:, tok] = _dot(u_ref[0], xm_ref[0, tok, :], ((1,), (1,)))

    def gate_tiles(tt, i0):
        tok = slice(tt * PEER_RT, (tt + 1) * PEER_RT)
        gmats = [jnp.zeros(packed, BF16) for _ in range(ng)]
        for h in range(PEER_HEADS):
            e2 = e2_s[h, :, tok].reshape(packed)
            r2 = r2_s[h, :, tok].reshape(packed)
            for k in range(ng):
                e1 = e1b_s[i0 + k, h, :, tok][None]
                cut = cutb_s[i0 + k, h, :, tok][None]
                gmats[k] = gmats[k] + e1 * jnp.where(r2 < cut, e2, jnp.zeros_like(e2))
        for k in range(ng):
            rows = slice((i0 + k) * PEER_NKEYS, (i0 + k + 1) * PEER_NKEYS)
            act = _gelu_tanh(at_s[rows, tok]).astype(BF16)
            w_s[rows, tok] = gmats[k].reshape(PEER_NKEYS, PEER_RT) * act

    activations(0)
    for tp in range(tm // tw):
        if (tp + 1) * tw < tm:
            activations(tp + 1)
        for tt in (2 * tp, 2 * tp + 1):
            for i0 in range(0, nb, ng):
                gate_tiles(tt, i0)
        tok = slice(tp * tw, (tp + 1) * tw)
        acc_s[:, tok] += _dot(vt_ref[0], w_s[:, tok])

    @pl.when(e == pl.num_programs(2) - 1)
    def _():
        z = DEEPNORM_ALPHA * x_ref[0] + gate_ref[0] * acc_s[...].T
        o_ref[0] = _layer_norm_rows(z, g_ref[...], b_ref[...])


def peer_dense(xm, u_all, vt_all, l, e1, cut, e2, r2, x3, mod3, gate_chunk, ln_g, ln_b, tm=512):
    nseg, seg, d = x3.shape
    ne = u_all.shape[1]
    nb = PEER_CE // PEER_NKEYS
    tok = lambda s, i, e: (s, i, 0)
    chunk = pl.BlockSpec((1, PEER_HEADS, nb, tm), lambda s, i, e: (s, 0, e, i))
    full = pl.BlockSpec((1, PEER_HEADS, PEER_NKEYS, tm), lambda s, i, e: (s, 0, 0, i))
    return pl.pallas_call(
        functools.partial(_peer_dense_kernel, tm=tm),
        grid=(nseg, seg // tm, ne // PEER_CE),
        in_specs=[pl.BlockSpec((1, tm, d), tok),
                  pl.BlockSpec((1, PEER_CE, d), lambda s, i, e: (l, e, 0)),
                  pl.BlockSpec((1, d, PEER_CE), lambda s, i, e: (l, 0, e)),
                  chunk, chunk, full, full,
                  pl.BlockSpec((1, tm, d), tok),
                  pl.BlockSpec((1, 1, d), lambda s, i, e: (s, 0, gate_chunk)),
                  pl.BlockSpec((1, d), lambda s, i, e: (0, 0)),
                  pl.BlockSpec((1, d), lambda s, i, e: (0, 0))],
        out_specs=pl.BlockSpec((1, tm, d), tok),
        out_shape=jax.ShapeDtypeStruct((nseg, seg, d), F32),
        scratch_shapes=[pltpu.VMEM((d, tm), F32), pltpu.VMEM((PEER_CE, tm), F32), pltpu.VMEM((PEER_CE, tm), BF16),
                        pltpu.VMEM((PEER_HEADS, PEER_NKEYS, tm), BF16), pltpu.VMEM((PEER_HEADS, PEER_NKEYS, tm), BF16),
                        pltpu.VMEM((nb, PEER_HEADS, 16, tm), BF16), pltpu.VMEM((nb, PEER_HEADS, 16, tm), BF16)],
        compiler_params=_params(("arbitrary", "arbitrary", "arbitrary")),
        name="peer_dense",
    )(xm, u_all, vt_all, e1, cut, e2, r2, x3, mod3, ln_g.reshape(1, d), ln_b.reshape(1, d))


def peer_layer(x3, mod3, l, wq, subkeys, u_all, vt_all, ln_g, ln_b):
    xm, e1, cut, e2, r2 = peer_route(x3, mod3, 3, wq.astype(BF16), subkeys)
    return peer_dense(xm, u_all, vt_all, l, e1, cut, e2, r2, x3, mod3, 5, ln_g, ln_b)


def _pad_cols(w, n):
    return jnp.pad(w, ((0, 0), (0, n - w.shape[1])))


def _stream(prompt_part, sample_part):
    return jnp.concatenate([prompt_part.reshape(1, -1, prompt_part.shape[-1]), sample_part], axis=0)


def _head_major(a, heads):
    b, t, _ = a.shape
    return jnp.transpose(a.reshape(b, t, heads, -1), (0, 2, 1, 3))


def _token_major(a):
    b, h, t, dh = a.shape
    return jnp.transpose(a, (0, 2, 1, 3)).reshape(b, t, h * dh)


MLSTM_CHUNK = 128
GLA_CHUNK = 32
NPROJ = 3200


def mlstm_layer(x3, mod3, bp, lp, st_c, st_n, st_m, w_in, b_gate, norm_w, w_out, ln_g, ln_b):
    nseg, seg, _ = x3.shape
    bs = nseg - 1
    p = mod_matmul(x3, mod3, 0, _pad_cols(w_in, NPROJ).astype(BF16))
    graw = p[:, :, 3072:3088]
    gp = graw[0].reshape(bp, lp, 16)
    zc = jnp.zeros((bp, 8, M_DK, M_DV), F32)
    zn = jnp.zeros((bp, 8, M_DK), F32)
    hfp, hbp, c_new, n_new, m_new = mlstm_scan(p.reshape(nseg * bp, lp, NPROJ), 0, bp, lp, gp,
                                               jnp.swapaxes(gp, 1, 2), b_gate, zc, zn, zn, min(MLSTM_CHUNK, lp))
    gs = graw[1:]
    hfs, hbs, _, _, _ = mlstm_scan(p, 1, bs, seg, gs, jnp.swapaxes(gs, 1, 2), b_gate,
                                   st_c.reshape(bs, 8, M_DK, M_DV), st_n.reshape(bs, 8, M_DK),
                                   jnp.broadcast_to(st_m.reshape(bs, 8, 1), (bs, 8, M_DK)), MLSTM_CHUNK)
    x3 = outproj_ln("mlstm", (_stream(hfp, hfs), _stream(hbp, hbs)), x3, mod3, 2, w_out.astype(BF16), ln_g, ln_b,
                    norm_w=norm_w, og=p, og_col=2)
    return (x3, c_new.reshape(bp, 2, M_HEADS, M_DK, M_DV), n_new.reshape(bp, 2, M_HEADS, M_DK),
            m_new[:, :, 0].reshape(bp, 2, M_HEADS))


def gla_layer(x3, mod3, bp, lp, st_s, w_in, w_gate2, b_gate2, norm_w, w_out, ln_g, ln_b):
    nseg, seg, _ = x3.shape
    bs = nseg - 1
    p = mod_matmul(x3, mod3, 0, _pad_cols(w_in, NPROJ).astype(BF16))
    gr = p[:, :, 3072:3104]
    zs = jnp.zeros((bp, 8, G_DV, G_DK), F32)
    ofp, obp, s_new = gla_scan(p.reshape(nseg * bp, lp, NPROJ), 0, bp, lp, gr[0].reshape(bp, lp, 32),
                               w_gate2, b_gate2, zs, GLA_CHUNK)
    s0t = jnp.swapaxes(st_s.reshape(bs, 8, G_DK, G_DV), -1, -2)
    ofs, obs, _ = gla_scan(p, 1, bs, seg, gr[1:], w_gate2, b_gate2, s0t, GLA_CHUNK)
    x3 = outproj_ln("gla", (_stream(ofp, ofs), _stream(obp, obs)), x3, mod3, 2, w_out.astype(BF16), ln_g, ln_b,
                    norm_w=jnp.tile(norm_w, G_HEADS), og=p, og_col=2)
    return x3, jnp.swapaxes(s_new, -1, -2).reshape(bp, 2, G_HEADS, G_DK, G_DV)


def na_layer(x3, mod3, bp, lp, cache_k, cache_v, w_in, rpb, w_out, ln_g, ln_b):
    nseg, seg, _ = x3.shape
    bs = nseg - 1
    hd = NA_HEADS * NA_HD
    p = mod_matmul(x3, mod3, 0, w_in.astype(BF16))
    pp = p[0].reshape(bp, lp, 3 * hd)
    hm = lambda a: _head_major(a, NA_HEADS).astype(BF16)
    yp = attention(hm(pp[..., :hd]), hm(pp[..., hd:2 * hd]), hm(pp[..., 2 * hd:]), lp)
    ps = p[1:]
    ys = na_attention(hm(ps[..., :hd]), hm(ps[..., hd:2 * hd]), hm(ps[..., 2 * hd:]),
                      hm(cache_k.reshape(bs, -1, hd)), hm(cache_v.reshape(bs, -1, hd)), na_bias_table(rpb))
    x3 = outproj_ln("plain", _stream(_token_major(yp), _token_major(ys)), x3, mod3, 2, w_out.astype(BF16), ln_g, ln_b)
    return (x3, pp[..., hd:2 * hd].reshape(bp, lp, NA_HEADS, NA_HD), pp[..., 2 * hd:].reshape(bp, lp, NA_HEADS, NA_HD))


def _rope_rotated_cols(w):
    q = MLA_ROPE // 4
    return jnp.concatenate([-w[..., q:2 * q], w[..., :q], -w[..., 3 * q:], w[..., 2 * q:3 * q]], axis=-1)


def _rope_tables(ts):
    ra = MLA_ROPE // 2
    t = np.arange(ts)
    inv = 1.0 / (ROPE_BASE ** (np.arange(0, ra, 2, dtype=np.float32) / ra))
    ang_r = (t // GRID_W).astype(np.float32)[:, None] * inv[None, :]
    ang_c = (t % GRID_W).astype(np.float32)[:, None] * inv[None, :]
    ang = np.concatenate([ang_r, ang_r, ang_c, ang_c], axis=-1).astype(np.float32)
    return jnp.cos(jnp.asarray(ang)), jnp.sin(jnp.asarray(ang))


def mla_layer(x3, mod3, bp, lp, cache_ckv, cache_kpe, w_in, q_norm, w_qup, kv_norm, w_kvup, w_out, ln_g, ln_b):
    nseg, seg, _ = x3.shape
    bs = nseg - 1
    nq = MLA_Q_LORA + MLA_KV_LORA
    w_ext = jnp.concatenate([w_in, _rope_rotated_cols(w_in[:, nq:])], axis=1)
    p = mod_matmul(x3, mod3, 0, _pad_cols(w_ext, 896).astype(BF16))
    cos_t, sin_t = _rope_tables(seg)
    cos3 = jnp.concatenate([jnp.ones((1, seg, MLA_ROPE), F32), jnp.broadcast_to(cos_t, (bs, seg, MLA_ROPE))], 0)
    sin3 = jnp.concatenate([jnp.zeros((1, seg, MLA_ROPE), F32), jnp.broadcast_to(sin_t, (bs, seg, MLA_ROPE))], 0)
    wq = w_qup.reshape(MLA_Q_LORA, MLA_HEADS, MLA_NOPE + MLA_ROPE)
    wq_rope = wq[:, :, MLA_NOPE:]
    w_q3 = jnp.concatenate([wq[:, :, :MLA_NOPE].reshape(MLA_Q_LORA, -1), wq_rope.reshape(MLA_Q_LORA, -1),
                            _rope_rotated_cols(wq_rope).reshape(MLA_Q_LORA, -1)], axis=1).astype(BF16)
    q_all = mla_q(p, q_norm, w_q3, jnp.tile(cos3, (1, 1, MLA_HEADS)), jnp.tile(sin3, (1, 1, MLA_HEADS)))
    wkv = w_kvup.reshape(MLA_KV_LORA, MLA_HEADS, MLA_NOPE + MLA_VD)
    w_kv2 = jnp.concatenate([wkv[:, :, :MLA_NOPE].reshape(MLA_KV_LORA, -1),
                             wkv[:, :, MLA_NOPE:].reshape(MLA_KV_LORA, -1)], axis=1).astype(BF16)
    ckvn, kpe, kv = mla_kv(p, kv_norm, w_kv2, cos3, sin3)
    kvc = matmul(cache_ckv.reshape(-1, MLA_KV_LORA), w_kv2, 512).reshape(bs, -1, w_kv2.shape[1])
    nn = MLA_HEADS * MLA_NOPE

    def heads(q_rows, kv_rows, kpe_rows):
        b, t, _ = q_rows.shape
        tk = kv_rows.shape[1]
        qh = jnp.concatenate([q_rows[..., :nn].reshape(b, t, MLA_HEADS, MLA_NOPE),
                              q_rows[..., nn:].reshape(b, t, MLA_HEADS, MLA_ROPE)], -1)
        kh = jnp.concatenate([kv_rows[..., :nn].reshape(b, tk, MLA_HEADS, MLA_NOPE),
                              jnp.broadcast_to(kpe_rows[:, :, None, :], (b, tk, MLA_HEADS, MLA_ROPE))], -1)
        vh = kv_rows[..., nn:].reshape(b, tk, MLA_HEADS, MLA_VD)
        tr = lambda a: jnp.transpose(a, (0, 2, 1, 3)).astype(BF16)
        return tr(qh), tr(kh), tr(vh)

    yp = attention(*heads(q_all[0].reshape(bp, lp, -1), kv[0].reshape(bp, lp, -1), kpe[0].reshape(bp, lp, -1)), lp)
    ys = attention(*heads(q_all[1:], jnp.concatenate([kv[1:], kvc], 1), jnp.concatenate([kpe[1:], cache_kpe], 1)), 256)
    x3 = outproj_ln("plain", _stream(_token_major(yp), _token_major(ys)), x3, mod3, 2, w_out.astype(BF16), ln_g, ln_b)
    return x3, ckvn[0].reshape(bp, lp, MLA_KV_LORA), kpe[0].reshape(bp, lp, MLA_ROPE)


def kernel(x_prompt, x_sample, c, c_ctx, state_mlstm_C, state_mlstm_n, state_mlstm_m, state_gla_S, cache_na_k, cache_na_v, cache_mla_ckv, cache_mla_kpe, ada_w, ada_b, ln_mix_g, ln_mix_b, ln_ffn_g, ln_ffn_b, mlstm_w_in, mlstm_b_gate, mlstm_norm_w, mlstm_w_out, gla_w_in, gla_w_gate2, gla_b_gate2, gla_norm_w, gla_w_out, na_w_in, na_rpb, na_w_out, mla_w_in, mla_q_norm, mla_w_qup, mla_kv_norm, mla_w_kvup, mla_w_out, peer_w_q, peer_subkeys, peer_u, peer_v):
    bp, lp, d = x_prompt.shape
    bs, ts, _ = x_sample.shape
    assert bp * lp == ts and bs + 1 <= 8
    x3 = _stream(x_prompt, x_sample)
    cond8 = jnp.zeros((8, d), F32).at[0].set(c_ctx).at[1:1 + bs].set(c)
    mods = adaln_all(cond8, ada_w, ada_b)
    u_all = peer_u.astype(BF16)
    vt_all = jnp.swapaxes(peer_v, 1, 2).astype(BF16)
    outs = {}
    for l in range(DEPTH):
        mod3 = mods[l].reshape(8, 1, ADA_CHUNKS * d)
        kind = l % 4
        if kind == 0:
            x3, outs["C"], outs["n"], outs["m"] = mlstm_layer(
                x3, mod3, bp, lp, state_mlstm_C, state_mlstm_n, state_mlstm_m, mlstm_w_in, mlstm_b_gate,
                mlstm_norm_w, mlstm_w_out, ln_mix_g[l], ln_mix_b[l])
        elif kind == 1:
            x3, outs["S"] = gla_layer(x3, mod3, bp, lp, state_gla_S, gla_w_in, gla_w_gate2, gla_b_gate2,
                                      gla_norm_w, gla_w_out, ln_mix_g[l], ln_mix_b[l])
        elif kind == 2:
            x3, outs["nk"], outs["nv"] = na_layer(x3, mod3, bp, lp, cache_na_k, cache_na_v, na_w_in, na_rpb,
                                                  na_w_out, ln_mix_g[l], ln_mix_b[l])
        else:
            x3, outs["ckv"], outs["kpe"] = mla_layer(x3, mod3, bp, lp, cache_mla_ckv, cache_mla_kpe, mla_w_in,
                                                     mla_q_norm, mla_w_qup, mla_kv_norm, mla_w_kvup, mla_w_out,
                                                     ln_mix_g[l], ln_mix_b[l])
        x3 = peer_layer(x3, mod3, l, peer_w_q[l], peer_subkeys[l], u_all, vt_all, ln_ffn_g[l], ln_ffn_b[l])
    return (x3[0].reshape(bp, lp, d), x3[1:], outs["C"], outs["n"], outs["m"], outs["S"], outs["nk"], outs["nv"],
            outs["ckv"], outs["kpe"])
```

```python
import functools

import numpy as np
import jax
import jax.numpy as jnp
from jax import lax
from jax.experimental import pallas as pl
from jax.experimental.pallas import tpu as pltpu

D_MODEL = 1024
DEPTH = 4
GRID_W = 64
DEEPNORM_ALPHA = (2.0 * DEPTH) ** 0.25
ADA_CHUNKS = 6
NORM_EPS = 1e-5
SEG = 4096
NSEG = 3

M_HEADS, M_DK, M_DV = 4, 128, 256
G_HEADS, G_DK, G_DV = 4, 128, 256
G_GATE_RANK = 16
G_GATE_NORM = 16.0
NA_HEADS, NA_HD, NA_ROWS, NA_COLS = 16, 64, 8, 16
MLA_HEADS, MLA_Q_LORA, MLA_KV_LORA, MLA_NOPE, MLA_ROPE, MLA_VD = 16, 512, 256, 64, 32, 64
ROPE_BASE = 10000.0
PEER_HEADS, PEER_NKEYS, PEER_HALF, PEER_TOPK = 8, 128, 128, 16

V7X_VMEM_LIMIT = 56 * 1024 * 1024
F32 = jnp.float32
BF16 = jnp.bfloat16
NEG_INF = float("-inf")


def _params(sem, vmem=V7X_VMEM_LIMIT):
    return pltpu.CompilerParams(dimension_semantics=sem, vmem_limit_bytes=vmem)


def _dot(a, b, dims=((1,), (0,))):
    return lax.dot_general(a, b, (dims, ((), ())), preferred_element_type=F32)


def _split3(a):
    hi = a.astype(BF16)
    r1 = a - hi.astype(F32)
    mid = r1.astype(BF16)
    lo = (r1 - mid.astype(F32)).astype(BF16)
    return hi, mid, lo


def _dot_exact_lhs(m01, a):
    hi, mid, lo = _split3(a)
    return _dot(m01, hi) + _dot(m01, mid) + _dot(m01, lo)


def _dot_exact_rhs(a, m01):
    hi, mid, lo = _split3(a)
    return _dot(hi, m01) + _dot(mid, m01) + _dot(lo, m01)


def _log_sigmoid(x):
    return jnp.minimum(x, 0.0) - jnp.log(1.0 + jnp.exp(-jnp.abs(x)))


def _sigmoid(x):
    return 1.0 / (1.0 + jnp.exp(-x))


def _adaln_kernel(c_ref, w_ref, b_ref, o_ref):
    cv = c_ref[...]
    a = cv * _sigmoid(cv)
    o_ref[0] = lax.dot_general(a, w_ref[0], (((1,), (0,)), ((), ())), precision=lax.Precision.HIGHEST,
                               preferred_element_type=F32) + b_ref[0]


def adaln_all(cond8, ada_w, ada_b):
    tn = 1024
    n = ada_w.shape[-1]
    return pl.pallas_call(
        _adaln_kernel,
        grid=(DEPTH, n // tn),
        in_specs=[pl.BlockSpec((8, D_MODEL), lambda l, j: (0, 0)),
                  pl.BlockSpec((1, D_MODEL, tn), lambda l, j: (l, 0, j)),
                  pl.BlockSpec((1, 1, tn), lambda l, j: (l, 0, j))],
        out_specs=pl.BlockSpec((1, 8, tn), lambda l, j: (l, 0, j)),
        out_shape=jax.ShapeDtypeStruct((DEPTH, 8, n), F32),
        compiler_params=_params(("arbitrary", "arbitrary")),
        name="adaln",
    )(cond8, ada_w, ada_b.reshape(DEPTH, 1, n))


def _modmm_kernel(x_ref, sh_ref, sc_ref, w_ref, o_ref, xm_ref):
    @pl.when(pl.program_id(2) == 0)
    def _():
        xm_ref[...] = (x_ref[0] * (1.0 + sc_ref[0]) + sh_ref[0]).astype(BF16)

    o_ref[0] = _dot(xm_ref[...], w_ref[...]).astype(o_ref.dtype)


def mod_matmul(x3, mod3, shift_chunk, w_bf16, tm=512, tn=None, out_dtype=F32):
    nseg, seg, d = x3.shape
    n = w_bf16.shape[1]
    tn = n if tn is None else tn
    return pl.pallas_call(
        _modmm_kernel,
        grid=(nseg, seg // tm, n // tn),
        in_specs=[pl.BlockSpec((1, tm, d), lambda s, i, j: (s, i, 0)),
                  pl.BlockSpec((1, 1, d), lambda s, i, j: (s, 0, shift_chunk)),
                  pl.BlockSpec((1, 1, d), lambda s, i, j: (s, 0, shift_chunk + 1)),
                  pl.BlockSpec((d, tn), lambda s, i, j: (0, j))],
        out_specs=pl.BlockSpec((1, tm, tn), lambda s, i, j: (s, i, j)),
        out_shape=jax.ShapeDtypeStruct((nseg, seg, n), out_dtype),
        scratch_shapes=[pltpu.VMEM((tm, d), BF16)],
        compiler_params=_params(("arbitrary", "arbitrary", "arbitrary")),
        name="mod_matmul",
    )(x3, mod3, mod3, w_bf16)


def _layer_norm_rows(y, g, b):
    mu = jnp.mean(y, axis=-1, keepdims=True)
    yc = y - mu
    var = jnp.mean(yc * yc, axis=-1, keepdims=True)
    return yc * lax.rsqrt(var + NORM_EPS) * g + b


def _outproj_kernel(*refs, mode):
    if mode == "plain":
        y_ref, x_ref, gate_ref, w_ref, g_ref, b_ref, o_ref = refs
        yin = y_ref[0].astype(BF16)
    else:
        ya_ref, yb_ref, og_ref, nw_ref, x_ref, gate_ref, w_ref, g_ref, b_ref, o_ref = refs
        hs = ya_ref[0] + yb_ref[0]
        og = og_ref[0]
        parts = []
        for h in range(4):
            seg = hs[:, h * 256:(h + 1) * 256]
            nw = nw_ref[:, h * 256:(h + 1) * 256]
            if mode == "mlstm":
                mu = jnp.mean(seg, axis=-1, keepdims=True)
                sc = seg - mu
                var = jnp.mean(sc * sc, axis=-1, keepdims=True)
                parts.append(sc * lax.rsqrt(var + NORM_EPS) * nw)
            else:
                ms = jnp.mean(seg * seg, axis=-1, keepdims=True)
                parts.append(seg * lax.rsqrt(ms + NORM_EPS) * nw)
        hn = jnp.concatenate(parts, axis=-1)
        act = _sigmoid(og) if mode == "mlstm" else og * _sigmoid(og)
        yin = (act * hn).astype(BF16)
    y = _dot(yin, w_ref[...])
    z = DEEPNORM_ALPHA * x_ref[0] + gate_ref[0] * y
    o_ref[0] = _layer_norm_rows(z, g_ref[...], b_ref[...])


def outproj_ln(mode, ys, x3, mod3, gate_chunk, w_bf16, ln_g, ln_b, norm_w=None, og=None, og_col=0, tm=512):
    nseg, seg, d = x3.shape
    k = w_bf16.shape[0]
    tok = lambda s, i: (s, i, 0)
    if mode == "plain":
        args = [ys]
        specs = [pl.BlockSpec((1, tm, k), tok)]
    else:
        args = [ys[0], ys[1], og, norm_w.reshape(1, k)]
        specs = [pl.BlockSpec((1, tm, k), tok), pl.BlockSpec((1, tm, k), tok),
                 pl.BlockSpec((1, tm, k), lambda s, i: (s, i, og_col)),
                 pl.BlockSpec((1, k), lambda s, i: (0, 0))]
    args += [x3, mod3, w_bf16, ln_g.reshape(1, d), ln_b.reshape(1, d)]
    specs += [pl.BlockSpec((1, tm, d), tok),
              pl.BlockSpec((1, 1, d), lambda s, i: (s, 0, gate_chunk)),
              pl.BlockSpec((k, d), lambda s, i: (0, 0)),
              pl.BlockSpec((1, d), lambda s, i: (0, 0)),
              pl.BlockSpec((1, d), lambda s, i: (0, 0))]
    return pl.pallas_call(
        functools.partial(_outproj_kernel, mode=mode),
        grid=(nseg, seg // tm),
        in_specs=specs,
        out_specs=pl.BlockSpec((1, tm, d), tok),
        out_shape=jax.ShapeDtypeStruct((nseg, seg, d), F32),
        compiler_params=_params(("arbitrary", "arbitrary")),
        name="outproj_ln_" + mode,
    )(*args)


def _tri(n, lower):
    r = lax.broadcasted_iota(jnp.int32, (n, n), 0)
    c = lax.broadcasted_iota(jnp.int32, (n, n), 1)
    return (c <= r) if lower else (c >= r)


def _mlstm_kernel(pf_ref, pb_ref, gf_ref, gb_ref, gtf_ref, gtb_ref, bias_ref, biast_ref,
                  c0_ref, n0_ref, m0_ref, hf_ref, hb_ref, co_ref, no_ref, mo_ref,
                  c_s, n_s, m_s, *, L):
    c = pl.program_id(1)

    @pl.when(c == 0)
    def _():
        c_s[...] = c0_ref[0]
        n_s[...] = n0_ref[0]
        m_s[...] = m0_ref[0]

    for d in range(2):
        p_ref, g_ref, gt_ref, h_ref = ((pf_ref, gf_ref, gtf_ref, hf_ref) if d == 0
                                       else (pb_ref, gb_ref, gtb_ref, hb_ref))
        mask = _tri(L, lower=(d == 0))
        mcol = mask.astype(BF16)
        mrow = _tri(L, lower=(d != 0)).astype(BF16)
        g = g_ref[0] + bias_ref[...]
        gt = gt_ref[0] + biast_ref[...]
        li_c = g[:, d * 8:d * 8 + 4]
        lf_c = _log_sigmoid(g[:, d * 8 + 4:d * 8 + 8])
        li_r = gt[d * 8:d * 8 + 4, :]
        lf_r = _log_sigmoid(gt[d * 8 + 4:d * 8 + 8, :])
        b_c = _dot_exact_lhs(mcol, lf_c)
        b_r = _dot_exact_rhs(lf_r, mrow)
        last = L - 1 if d == 0 else 0
        for h in range(M_HEADS):
            u = d * M_HEADS + h
            q = p_ref[0, :, h * M_DK:(h + 1) * M_DK]
            k = p_ref[0, :, 512 + h * M_DK:512 + (h + 1) * M_DK] * (M_DK ** -0.5)
            v = p_ref[0, :, 1024 + h * M_DV:1024 + (h + 1) * M_DV].astype(BF16)
            qb = q.astype(BF16)
            bc, br = b_c[:, h:h + 1], b_r[h:h + 1, :]
            lic, lir = li_c[:, h:h + 1], li_r[h:h + 1, :]
            m_prev = m_s[u:u + 1, 0:1]
            dmat = jnp.where(mask, bc - br + lir, NEG_INF)
            inter = bc + m_prev
            mt = jnp.maximum(inter, jnp.max(dmat, axis=-1, keepdims=True))
            smat = _dot(qb, k.astype(BF16), ((1,), (1,))) * jnp.exp(dmat - mt)
            ei = jnp.exp(inter - mt)
            cmat = c_s[u]
            num = _dot(smat.astype(BF16), v) + ei * _dot(qb, cmat.astype(BF16))
            nrow = n_s[u:u + 1, :]
            den = jnp.sum(smat, axis=-1, keepdims=True) + ei * jnp.sum(q * nrow, axis=-1, keepdims=True)
            h_ref[0, :, h * M_DV:(h + 1) * M_DV] = num / jnp.maximum(jnp.abs(den), jnp.exp(-mt))
            tot = br[:, last:last + 1]
            g_c = tot - bc + lic
            g_r = tot - br + lir
            m_new = jnp.maximum(tot + m_prev, jnp.max(g_r, axis=-1, keepdims=True))
            kw = k * jnp.exp(g_c - m_new)
            dec = jnp.exp(tot + m_prev - m_new)
            c_s[u] = dec * cmat + _dot(kw.astype(BF16), v, ((0,), (0,)))
            n_s[u:u + 1, :] = dec * nrow + jnp.sum(kw, axis=0, keepdims=True)
            m_s[u:u + 1, :] = jnp.broadcast_to(m_new, (1, 128))

    @pl.when(c == pl.num_programs(1) - 1)
    def _():
        co_ref[0] = c_s[...]
        no_ref[0] = n_s[...]
        mo_ref[0] = m_s[...]


def mlstm_scan(p, b0, nb, t, g, gt, bias, c0, n0, m0, L):
    nc = t // L
    hshape = jax.ShapeDtypeStruct((nb, t, M_HEADS * M_DV), F32)
    fwd = lambda b, c: (b + b0, c, 0)
    bwd = lambda b, c: (b + b0, nc - 1 - c, 0)
    st4 = lambda b, c: (b, 0, 0, 0)
    st3 = lambda b, c: (b, 0, 0)
    return pl.pallas_call(
        functools.partial(_mlstm_kernel, L=L),
        grid=(nb, nc),
        in_specs=[pl.BlockSpec((1, L, 2048), fwd), pl.BlockSpec((1, L, 2048), bwd),
                  pl.BlockSpec((1, L, 16), lambda b, c: (b, c, 0)),
                  pl.BlockSpec((1, L, 16), lambda b, c: (b, nc - 1 - c, 0)),
                  pl.BlockSpec((1, 16, L), lambda b, c: (b, 0, c)),
                  pl.BlockSpec((1, 16, L), lambda b, c: (b, 0, nc - 1 - c)),
                  pl.BlockSpec((1, 16), lambda b, c: (0, 0)),
                  pl.BlockSpec((16, 1), lambda b, c: (0, 0)),
                  pl.BlockSpec((1, 8, M_DK, M_DV), st4),
                  pl.BlockSpec((1, 8, M_DK), st3),
                  pl.BlockSpec((1, 8, M_DK), st3)],
        out_specs=[pl.BlockSpec((1, L, 1024), lambda b, c: (b, c, 0)),
                   pl.BlockSpec((1, L, 1024), lambda b, c: (b, nc - 1 - c, 0)),
                   pl.BlockSpec((1, 8, M_DK, M_DV), st4),
                   pl.BlockSpec((1, 8, M_DK), st3),
                   pl.BlockSpec((1, 8, M_DK), st3)],
        out_shape=[hshape, hshape,
                   jax.ShapeDtypeStruct((nb, 8, M_DK, M_DV), F32),
                   jax.ShapeDtypeStruct((nb, 8, M_DK), F32),
                   jax.ShapeDtypeStruct((nb, 8, M_DK), F32)],
        scratch_shapes=[pltpu.VMEM((8, M_DK, M_DV), F32), pltpu.VMEM((8, M_DK), F32),
                        pltpu.VMEM((8, M_DK), F32)],
        compiler_params=_params(("arbitrary", "arbitrary")),
        name="mlstm_scan",
    )(p, p, g, g, gt, gt, bias.reshape(1, 16), bias.reshape(16, 1), c0, n0, m0)


def _gla_kernel(pf_ref, pb_ref, gf_ref, gb_ref, w2_ref, b2_ref, s0_ref, of_ref, ob_ref, so_ref, s_s, *, L):
    c = pl.program_id(1)

    @pl.when(c == 0)
    def _():
        s_s[...] = s0_ref[0]

    for d in range(2):
        p_ref, g_ref, o_ref = (pf_ref, gf_ref, of_ref) if d == 0 else (pb_ref, gb_ref, ob_ref)
        mask = _tri(L, lower=(d == 0))
        mcol = mask.astype(BF16)
        gr = g_ref[0][:, d * G_GATE_RANK:(d + 1) * G_GATE_RANK]
        pre = lax.dot_general(gr, w2_ref[d], (((1,), (0,)), ((), ())), precision=lax.Precision.HIGHEST,
                              preferred_element_type=F32) + b2_ref[d]
        la = _log_sigmoid(pre) * (1.0 / G_GATE_NORM)
        bc_all = _dot_exact_lhs(mcol, la)
        last = L - 1 if d == 0 else 0
        for h in range(G_HEADS):
            u = d * G_HEADS + h
            q = p_ref[0, :, h * G_DK:(h + 1) * G_DK] * (G_DK ** -0.5)
            k = p_ref[0, :, 512 + h * G_DK:512 + (h + 1) * G_DK]
            v = p_ref[0, :, 1024 + h * G_DV:1024 + (h + 1) * G_DV].astype(BF16)
            bc = bc_all[:, h * G_DK:(h + 1) * G_DK]
            qd = (q * jnp.exp(bc)).astype(BF16)
            kd = (k * jnp.exp(-bc)).astype(BF16)
            a = jnp.where(mask, _dot(qd, kd, ((1,), (1,))), 0.0)
            st = s_s[u]
            o_ref[0, :, h * G_DV:(h + 1) * G_DV] = (_dot(a.astype(BF16), v)
                                                    + _dot(qd, st.astype(BF16), ((1,), (1,))))
            bl = bc[last:last + 1, :]
            kl = (k * jnp.exp(bl - bc)).astype(BF16)
            s_s[u] = st * jnp.exp(bl) + _dot(v, kl, ((0,), (0,)))

    @pl.when(c == pl.num_programs(1) - 1)
    def _():
        so_ref[0] = s_s[...]


def gla_scan(p, b0, nb, t, gr, w2, b2, s0t, L):
    nc = t // L
    oshape = jax.ShapeDtypeStruct((nb, t, G_HEADS * G_DV), F32)
    st4 = lambda b, c: (b, 0, 0, 0)
    return pl.pallas_call(
        functools.partial(_gla_kernel, L=L),
        grid=(nb, nc),
        in_specs=[pl.BlockSpec((1, L, 2048), lambda b, c: (b + b0, c, 0)),
                  pl.BlockSpec((1, L, 2048), lambda b, c: (b + b0, nc - 1 - c, 0)),
                  pl.BlockSpec((1, L, 32), lambda b, c: (b, c, 0)),
                  pl.BlockSpec((1, L, 32), lambda b, c: (b, nc - 1 - c, 0)),
                  pl.BlockSpec((2, G_GATE_RANK, 512), lambda b, c: (0, 0, 0)),
                  pl.BlockSpec((2, 1, 512), lambda b, c: (0, 0, 0)),
                  pl.BlockSpec((1, 8, G_DV, G_DK), st4)],
        out_specs=[pl.BlockSpec((1, L, 1024), lambda b, c: (b, c, 0)),
                   pl.BlockSpec((1, L, 1024), lambda b, c: (b, nc - 1 - c, 0)),
                   pl.BlockSpec((1, 8, G_DV, G_DK), st4)],
        out_shape=[oshape, oshape, jax.ShapeDtypeStruct((nb, 8, G_DV, G_DK), F32)],
        scratch_shapes=[pltpu.VMEM((8, G_DV, G_DK), F32)],
        compiler_params=_params(("arbitrary", "arbitrary")),
        name="gla_scan",
    )(p, p, gr, gr, w2, b2.reshape(2, 1, 512), s0t)


def _attn_kernel(q_ref, k_ref, v_ref, o_ref, *, scale):
    s = _dot(q_ref[0, 0], k_ref[0, 0], ((1,), (1,))) * scale
    m = jnp.max(s, axis=-1, keepdims=True)
    p = jnp.exp(s - m)
    l = jnp.sum(p, axis=-1, keepdims=True)
    o_ref[0, 0] = _dot(p.astype(BF16), v_ref[0, 0]) / l


def attention(q, k, v, tq):
    b, h, lq, dq = q.shape
    lk, dv = k.shape[2], v.shape[3]
    return pl.pallas_call(
        functools.partial(_attn_kernel, scale=dq ** -0.5),
        grid=(b, h, lq // tq),
        in_specs=[pl.BlockSpec((1, 1, tq, dq), lambda b, h, i: (b, h, i, 0)),
                  pl.BlockSpec((1, 1, lk, dq), lambda b, h, i: (b, h, 0, 0)),
                  pl.BlockSpec((1, 1, lk, dv), lambda b, h, i: (b, h, 0, 0))],
        out_specs=pl.BlockSpec((1, 1, tq, dv), lambda b, h, i: (b, h, i, 0)),
        out_shape=jax.ShapeDtypeStruct((b, h, lq, dv), F32),
        compiler_params=_params(("arbitrary", "arbitrary", "arbitrary")),
        name="attention",
    )(q, k, v)


NA_RB = 8


def _na_kernel(q_ref, k_ref, v_ref, kc_ref, vc_ref, bias_ref, o_ref, *, rows):
    j = pl.program_id(2)
    scale = NA_HD ** -0.5
    kc, vc = kc_ref[0, 0], vc_ref[0, 0]
    for a in range(NA_RB):
        r = j * NA_RB + a
        start = jnp.clip(r - NA_ROWS // 2, 0, rows - NA_ROWS)
        dr0 = start - r + (NA_ROWS - 1)
        off = pl.multiple_of(start * GRID_W, GRID_W)
        qa = q_ref[0, 0, a * GRID_W:(a + 1) * GRID_W, :]
        kl = k_ref[0, 0, pl.ds(off, NA_ROWS * GRID_W), :]
        vl = v_ref[0, 0, pl.ds(off, NA_ROWS * GRID_W), :]
        s_loc = _dot(qa, kl, ((1,), (1,))) * scale + bias_ref[0, dr0]
        s_ctx = _dot(qa, kc, ((1,), (1,))) * scale
        m = jnp.maximum(jnp.max(s_loc, axis=-1, keepdims=True), jnp.max(s_ctx, axis=-1, keepdims=True))
        p_loc = jnp.exp(s_loc - m)
        p_ctx = jnp.exp(s_ctx - m)
        l = jnp.sum(p_loc, axis=-1, keepdims=True) + jnp.sum(p_ctx, axis=-1, keepdims=True)
        o = _dot(p_loc.astype(BF16), vl) + _dot(p_ctx.astype(BF16), vc)
        o_ref[0, 0, a * GRID_W:(a + 1) * GRID_W, :] = o / l


def na_bias_table(rpb):
    cq = np.arange(GRID_W)[:, None]
    ck = np.arange(GRID_W)[None, :]
    cs = np.clip(cq - NA_COLS // 2, 0, GRID_W - NA_COLS)
    ok = (ck >= cs) & (ck < cs + NA_COLS)
    dc = np.clip(ck - cq, -(NA_COLS - 1), NA_COLS - 1) + (NA_COLS - 1)
    t = jnp.where(ok[None, None], rpb.astype(F32)[:, :, dc], NEG_INF)
    rows = np.arange(NA_ROWS)[:, None] + np.arange(NA_ROWS)[None, :]
    tf = t[:, rows]
    return jnp.transpose(tf, (0, 1, 3, 2, 4)).reshape(NA_HEADS, NA_ROWS, GRID_W, NA_ROWS * GRID_W)


def na_attention(q, k, v, kc, vc, bias):
    b, h, t, dh = q.shape
    lc = kc.shape[2]
    rows = t // GRID_W
    full = lambda b, h, j: (b, h, 0, 0)
    return pl.pallas_call(
        functools.partial(_na_kernel, rows=rows),
        grid=(b, h, rows // NA_RB),
        in_specs=[pl.BlockSpec((1, 1, NA_RB * GRID_W, dh), lambda b, h, j: (b, h, j, 0)),
                  pl.BlockSpec((1, 1, t, dh), full), pl.BlockSpec((1, 1, t, dh), full),
                  pl.BlockSpec((1, 1, lc, dh), full), pl.BlockSpec((1, 1, lc, dh), full),
                  pl.BlockSpec((1, NA_ROWS, GRID_W, NA_ROWS * GRID_W), lambda b, h, j: (h, 0, 0, 0))],
        out_specs=pl.BlockSpec((1, 1, NA_RB * GRID_W, dh), lambda b, h, j: (b, h, j, 0)),
        out_shape=jax.ShapeDtypeStruct((b, h, t, dh), F32),
        compiler_params=_params(("arbitrary", "arbitrary", "arbitrary")),
        name="na_attention",
    )(q, k, v, kc, vc, bias)


def _rms_rows(x, g):
    return x * lax.rsqrt(jnp.mean(x * x, axis=-1, keepdims=True) + NORM_EPS) * g


def _mla_q_kernel(cq_ref, g_ref, w_ref, cos_ref, sin_ref, o_ref):
    r = _dot(_rms_rows(cq_ref[0], g_ref[...]).astype(BF16), w_ref[...])
    nn = MLA_HEADS * MLA_NOPE
    nr = MLA_HEADS * MLA_ROPE
    o_ref[0, :, :nn] = r[:, :nn]
    o_ref[0, :, nn:] = r[:, nn:nn + nr] * cos_ref[0] + r[:, nn + nr:] * sin_ref[0]


def mla_q(p, q_norm, w_q3, cos_q, sin_q, tm=512):
    nseg, seg, _ = p.shape
    nout = MLA_HEADS * (MLA_NOPE + MLA_ROPE)
    nr = MLA_HEADS * MLA_ROPE
    tok = lambda s, i: (s, i, 0)
    return pl.pallas_call(
        _mla_q_kernel,
        grid=(nseg, seg // tm),
        in_specs=[pl.BlockSpec((1, tm, MLA_Q_LORA), tok),
                  pl.BlockSpec((1, MLA_Q_LORA), lambda s, i: (0, 0)),
                  pl.BlockSpec(w_q3.shape, lambda s, i: (0, 0)),
                  pl.BlockSpec((1, tm, nr), tok), pl.BlockSpec((1, tm, nr), tok)],
        out_specs=pl.BlockSpec((1, tm, nout), tok),
        out_shape=jax.ShapeDtypeStruct((nseg, seg, nout), F32),
        compiler_params=_params(("arbitrary", "arbitrary")),
        name="mla_q",
    )(p, q_norm.reshape(1, -1), w_q3, cos_q, sin_q)


def _mla_kv_kernel(ckv_ref, kpe_ref, g_ref, w_ref, cos_ref, sin_ref, ckvn_ref, kpeo_ref, kv_ref):
    cn = _rms_rows(ckv_ref[0], g_ref[...])
    ckvn_ref[0] = cn
    kv_ref[0] = _dot(cn.astype(BF16), w_ref[...])
    kp = kpe_ref[0]
    kpeo_ref[0] = kp[:, :MLA_ROPE] * cos_ref[0] + kp[:, MLA_ROPE:2 * MLA_ROPE] * sin_ref[0]


def mla_kv(p, kv_norm, w_kv, cos_k, sin_k, tm=512):
    nseg, seg, _ = p.shape
    nkv = w_kv.shape[1]
    tok = lambda s, i: (s, i, 0)
    return pl.pallas_call(
        _mla_kv_kernel,
        grid=(nseg, seg // tm),
        in_specs=[pl.BlockSpec((1, tm, MLA_KV_LORA), lambda s, i: (s, i, MLA_Q_LORA // MLA_KV_LORA)),
                  pl.BlockSpec((1, tm, 128), lambda s, i: (s, i, (MLA_Q_LORA + MLA_KV_LORA) // 128)),
                  pl.BlockSpec((1, MLA_KV_LORA), lambda s, i: (0, 0)),
                  pl.BlockSpec(w_kv.shape, lambda s, i: (0, 0)),
                  pl.BlockSpec((1, tm, MLA_ROPE), tok), pl.BlockSpec((1, tm, MLA_ROPE), tok)],
        out_specs=[pl.BlockSpec((1, tm, MLA_KV_LORA), tok), pl.BlockSpec((1, tm, MLA_ROPE), tok),
                   pl.BlockSpec((1, tm, nkv), tok)],
        out_shape=[jax.ShapeDtypeStruct((nseg, seg, MLA_KV_LORA), F32),
                   jax.ShapeDtypeStruct((nseg, seg, MLA_ROPE), F32),
                   jax.ShapeDtypeStruct((nseg, seg, nkv), F32)],
        compiler_params=_params(("arbitrary", "arbitrary")),
        name="mla_kv",
    )(p, p, kv_norm.reshape(1, -1), w_kv, cos_k, sin_k)


def _mm_kernel(a_ref, w_ref, o_ref):
    o_ref[...] = _dot(a_ref[...].astype(BF16), w_ref[...])


def matmul(a, w_bf16, tm):
    m, k = a.shape
    n = w_bf16.shape[1]
    return pl.pallas_call(
        _mm_kernel,
        grid=(m // tm,),
        in_specs=[pl.BlockSpec((tm, k), lambda i: (i, 0)), pl.BlockSpec((k, n), lambda i: (0, 0))],
        out_specs=pl.BlockSpec((tm, n), lambda i: (i, 0)),
        out_shape=jax.ShapeDtypeStruct((m, n), F32),
        compiler_params=_params(("arbitrary",)),
        name="matmul",
    )(a, w_bf16)


PEER_RT = 128
NOT_TOP = 99.0


def _top16(s, exact):
    key = lax.broadcasted_iota(jnp.int32, s.shape, 0).astype(F32)
    rank = jnp.full(s.shape, NOT_TOP, F32)
    vals = []
    for r in range(PEER_TOPK):
        m = jnp.max(s, axis=0, keepdims=True)
        hit = s == m
        if exact:
            hit = key == jnp.min(jnp.where(hit, key, 1e9), axis=0, keepdims=True)
        rank = jnp.where(hit, float(r), rank)
        s = jnp.where(hit, NEG_INF, s)
        vals.append(m)
    return vals, rank


def _pair_topk(av, bv, exact):
    n = av[0].shape[-1]
    a_lo, a_hi = jnp.concatenate(av[:8], 0), jnp.concatenate(av[8:], 0)
    b_lo, b_hi = jnp.concatenate(bv[:8], 0), jnp.concatenate(bv[8:], 0)
    row = lax.broadcasted_iota(jnp.int32, (8, n), 0).astype(F32)

    no_pos = 1e8

    def rows_b(a, b_blk, boff, nvalid):
        ok = row < nvalid
        return jnp.where(ok, av[a] + b_blk, NEG_INF), jnp.where(ok, a * 16.0 + boff + row, no_pos)

    def rows_a(b, a_blk, aoff, lo, hi):
        ok = (row >= lo) & (row < hi)
        return jnp.where(ok, a_blk + bv[b], NEG_INF), jnp.where(ok, (aoff + row) * 16.0 + b, no_pos)

    groups = [rows_b(0, b_lo, 0, 8), rows_b(0, b_hi, 8, 8), rows_b(1, b_lo, 0, 8), rows_b(2, b_lo, 0, 5),
              rows_b(3, b_lo, 0, 4), rows_a(0, a_lo, 0, 4, 8), rows_a(0, a_hi, 8, 0, 8),
              rows_a(1, a_lo, 0, 4, 8), rows_a(2, a_lo, 0, 4, 5)]
    cands = [g[0] for g in groups]
    poss = [g[1] for g in groups]
    sels = [jnp.zeros((8, n), F32) for _ in groups]
    top = av[0] + bv[0]
    z = jnp.zeros((1, n), F32)
    for _ in range(PEER_TOPK):
        m = functools.reduce(jnp.maximum, cands)
        m = jnp.max(m, axis=0, keepdims=True)
        hits = [c == m for c in cands]
        if exact:
            first = functools.reduce(jnp.minimum, [jnp.where(hh, p, 1e9) for hh, p in zip(hits, poss)])
            first = jnp.min(first, axis=0, keepdims=True)
            hits = [p == first for p in poss]
        cands = [jnp.where(hh, NEG_INF, c) for hh, c in zip(hits, cands)]
        sels = [jnp.where(hh, 1.0, s) for hh, s in zip(hits, sels)]
        z = z + jnp.exp(m - top)
    cnt = lambda x: jnp.sum(x, axis=0, keepdims=True)
    cut_lo = sels[5] + sels[7] + sels[8]
    for a, c in enumerate([cnt(sels[0]) + cnt(sels[1]), cnt(sels[2]), cnt(sels[3]), cnt(sels[4])]):
        cut_lo = cut_lo + jnp.where(row == a, c, 0.0)
    return cut_lo, sels[6], z, cnt(cut_lo) + cnt(sels[6])


def _peer_route_kernel(x_ref, sh_ref, sc_ref, wq_ref, sk_ref, xm_ref, e1_ref, cut_ref, e2_ref, r2_ref, q_s, *, tm):
    xm = (x_ref[0] * (1.0 + sc_ref[0]) + sh_ref[0]).astype(BF16)
    xm_ref[0] = xm
    q = _dot(xm, wq_ref[...])
    for hp in range(2 * PEER_HEADS):
        q_s[hp] = q[:, hp * PEER_HALF:(hp + 1) * PEER_HALF]

    def route(h, tok, exact):
        def scores(hp):
            return lax.dot_general(sk_ref[hp], q_s[hp, tok, :], (((1,), (1,)), ((), ())),
                                   precision=lax.Precision.HIGHEST, preferred_element_type=F32)

        s1, s2 = scores(2 * h), scores(2 * h + 1)
        av, rank1 = _top16(s1, exact)
        bv, rank2 = _top16(s2, exact)
        cut_lo, cut_hi, z, nsel = _pair_topk(av, bv, exact)
        cut = jnp.zeros_like(s1)
        for r in range(PEER_TOPK):
            src = cut_lo if r < 8 else cut_hi
            cut = jnp.where(rank1 == float(r), src[r % 8:r % 8 + 1, :], cut)
        e1_ref[0, h, :, tok] = (jnp.exp(s1 - av[0]) / z).astype(BF16)
        cut_ref[0, h, :, tok] = cut.astype(BF16)
        e2_ref[0, h, :, tok] = jnp.exp(s2 - bv[0]).astype(BF16)
        r2_ref[0, h, :, tok] = rank2.astype(BF16)
        ranked = lambda rk: jnp.sum(jnp.where(rk < PEER_TOPK, 1.0, 0.0), axis=0, keepdims=True)
        return ranked(rank1), ranked(rank2), nsel

    def body(h, carry):
        toks = [pl.ds(t0, PEER_RT) for t0 in range(0, tm, PEER_RT)]
        counts = [route(h, tok, exact=False) for tok in toks]
        for tok, cnts in zip(toks, counts):
            bad = functools.reduce(jnp.maximum, [jnp.abs(cn - PEER_TOPK) for cn in cnts])

            @pl.when(jnp.max(bad) > 0.0)
            def _():
                route(h, tok, exact=True)
        return carry

    lax.fori_loop(0, PEER_HEADS, body, 0)


def peer_route(x3, mod3, shift_chunk, wq_bf16, subkeys, tm=256):
    nseg, seg, d = x3.shape
    tok = lambda s, i: (s, i, 0)
    rshape = jax.ShapeDtypeStruct((nseg, PEER_HEADS, PEER_NKEYS, seg), BF16)
    rspec = pl.BlockSpec((1, PEER_HEADS, PEER_NKEYS, tm), lambda s, i: (s, 0, 0, i))
    return pl.pallas_call(
        functools.partial(_peer_route_kernel, tm=tm),
        grid=(nseg, seg // tm),
        in_specs=[pl.BlockSpec((1, tm, d), tok),
                  pl.BlockSpec((1, 1, d), lambda s, i: (s, 0, shift_chunk)),
                  pl.BlockSpec((1, 1, d), lambda s, i: (s, 0, shift_chunk + 1)),
                  pl.BlockSpec(wq_bf16.shape, lambda s, i: (0, 0)),
                  pl.BlockSpec((2 * PEER_HEADS, PEER_NKEYS, PEER_HALF), lambda s, i: (0, 0, 0))],
        out_specs=[pl.BlockSpec((1, tm, d), tok), rspec, rspec, rspec, rspec],
        out_shape=[jax.ShapeDtypeStruct((nseg, seg, d), BF16)] + [rshape] * 4,
        scratch_shapes=[pltpu.VMEM((2 * PEER_HEADS, tm, PEER_HALF), F32)],
        compiler_params=_params(("arbitrary", "arbitrary")),
        name="peer_route",
    )(x3, mod3, mod3, wq_bf16, subkeys.reshape(2 * PEER_HEADS, PEER_NKEYS, PEER_HALF))


PEER_CE = 1024


def _gelu_tanh(x):
    return 0.5 * x * (1.0 + jnp.tanh(0.7978845608028654 * (x + 0.044715 * x * x * x)))


def _peer_dense_kernel(xm_ref, u_ref, vt_ref, e1_ref, cut_ref, e2_ref, r2_ref, x_ref, gate_ref, g_ref, b_ref,
                       o_ref, acc_s, at_s, w_s, e2_s, r2_s, e1b_s, cutb_s, *, tm):
    e = pl.program_id(2)
    nb = PEER_CE // PEER_NKEYS
    ntt = tm // PEER_RT

    @pl.when(e == 0)
    def _():
        acc_s[...] = jnp.zeros_like(acc_s)
        e2_s[...] = e2_ref[0]
        r2_s[...] = r2_ref[0]

    for h in range(PEER_HEADS):
        for ii in range(nb):
            e1b_s[ii, h] = jnp.broadcast_to(e1_ref[0, h, ii:ii + 1, :], (16, tm))
            cutb_s[ii, h] = jnp.broadcast_to(cut_ref[0, h, ii:ii + 1, :], (16, tm))

    packed = (PEER_NKEYS // 16, 16, PEER_RT)
    ng = 2
    tw = 2 * PEER_RT

    def activations(tp):
        tok = slice(tp * tw, (tp + 1) * tw)
        at_s[:, tok] = _dot(u_ref[0], xm_ref[0, tok, :], ((1,), (1,)))

    def gate_tiles(tt, i0):
        tok = slice(tt * PEER_RT, (tt + 1) * PEER_RT)
        gmats = [jnp.zeros(packed, BF16) for _ in range(ng)]
        for h in range(PEER_HEADS):
            e2 = e2_s[h, :, tok].reshape(packed)
            r2 = r2_s[h, :, tok].reshape(packed)
            for k in range(ng):
                e1 = e1b_s[i0 + k, h, :, tok][None]
                cut = cutb_s[i0 + k, h, :, tok][None]
                gmats[k] = gmats[k] + e1 * jnp.where(r2 < cut, e2, jnp.zeros_like(e2))
        for k in range(ng):
            rows = slice((i0 + k) * PEER_NKEYS, (i0 + k + 1) * PEER_NKEYS)
            act = _gelu_tanh(at_s[rows, tok]).astype(BF16)
            w_s[rows, tok] = gmats[k].reshape(PEER_NKEYS, PEER_RT) * act

    at_s[...] = _dot(u_ref[0], xm_ref[0], ((1,), (1,)))
    for tt in range(ntt):
        for i0 in range(0, nb, ng):
            gate_tiles(tt, i0)
    acc_s[...] += _dot(vt_ref[0], w_s[...])

    @pl.when(e == pl.num_programs(2) - 1)
    def _():
        z = DEEPNORM_ALPHA * x_ref[0] + gate_ref[0] * acc_s[...].T
        o_ref[0] = _layer_norm_rows(z, g_ref[...], b_ref[...])


def peer_dense(xm, u_all, vt_all, l, e1, cut, e2, r2, x3, mod3, gate_chunk, ln_g, ln_b, tm=512):
    nseg, seg, d = x3.shape
    ne = u_all.shape[1]
    nb = PEER_CE // PEER_NKEYS
    tok = lambda s, i, e: (s, i, 0)
    chunk = pl.BlockSpec((1, PEER_HEADS, nb, tm), lambda s, i, e: (s, 0, e, i))
    full = pl.BlockSpec((1, PEER_HEADS, PEER_NKEYS, tm), lambda s, i, e: (s, 0, 0, i))
    return pl.pallas_call(
        functools.partial(_peer_dense_kernel, tm=tm),
        grid=(nseg, seg // tm, ne // PEER_CE),
        in_specs=[pl.BlockSpec((1, tm, d), tok),
                  pl.BlockSpec((1, PEER_CE, d), lambda s, i, e: (l, e, 0)),
                  pl.BlockSpec((1, d, PEER_CE), lambda s, i, e: (l, 0, e)),
                  chunk, chunk, full, full,
                  pl.BlockSpec((1, tm, d), tok),
                  pl.BlockSpec((1, 1, d), lambda s, i, e: (s, 0, gate_chunk)),
                  pl.BlockSpec((1, d), lambda s, i, e: (0, 0)),
                  pl.BlockSpec((1, d), lambda s, i, e: (0, 0))],
        out_specs=pl.BlockSpec((1, tm, d), tok),
        out_shape=jax.ShapeDtypeStruct((nseg, seg, d), F32),
        scratch_shapes=[pltpu.VMEM((d, tm), F32), pltpu.VMEM((PEER_CE, tm), F32), pltpu.VMEM((PEER_CE, tm), BF16),
                        pltpu.VMEM((PEER_HEADS, PEER_NKEYS, tm), BF16), pltpu.VMEM((PEER_HEADS, PEER_NKEYS, tm), BF16),
                        pltpu.VMEM((nb, PEER_HEADS, 16, tm), BF16), pltpu.VMEM((nb, PEER_HEADS, 16, tm), BF16)],
        compiler_params=_params(("arbitrary", "arbitrary", "arbitrary")),
        name="peer_dense",
    )(xm, u_all, vt_all, e1, cut, e2, r2, x3, mod3, ln_g.reshape(1, d), ln_b.reshape(1, d))


def peer_layer(x3, mod3, l, wq, subkeys, u_all, vt_all, ln_g, ln_b):
    xm, e1, cut, e2, r2 = peer_route(x3, mod3, 3, wq.astype(BF16), subkeys)
    return peer_dense(xm, u_all, vt_all, l, e1, cut, e2, r2, x3, mod3, 5, ln_g, ln_b)


def _pad_cols(w, n):
    return jnp.pad(w, ((0, 0), (0, n - w.shape[1])))


def _stream(prompt_part, sample_part):
    return jnp.concatenate([prompt_part.reshape(1, -1, prompt_part.shape[-1]), sample_part], axis=0)


def _head_major(a, heads):
    b, t, _ = a.shape
    return jnp.transpose(a.reshape(b, t, heads, -1), (0, 2, 1, 3))


def _token_major(a):
    b, h, t, dh = a.shape
    return jnp.transpose(a, (0, 2, 1, 3)).reshape(b, t, h * dh)


MLSTM_CHUNK = 128
GLA_CHUNK = 32
NPROJ = 3200


def mlstm_layer(x3, mod3, bp, lp, st_c, st_n, st_m, w_in, b_gate, norm_w, w_out, ln_g, ln_b):
    nseg, seg, _ = x3.shape
    bs = nseg - 1
    p = mod_matmul(x3, mod3, 0, _pad_cols(w_in, NPROJ).astype(BF16))
    graw = p[:, :, 3072:3088]
    gp = graw[0].reshape(bp, lp, 16)
    zc = jnp.zeros((bp, 8, M_DK, M_DV), F32)
    zn = jnp.zeros((bp, 8, M_DK), F32)
    hfp, hbp, c_new, n_new, m_new = mlstm_scan(p.reshape(nseg * bp, lp, NPROJ), 0, bp, lp, gp,
                                               jnp.swapaxes(gp, 1, 2), b_gate, zc, zn, zn, min(MLSTM_CHUNK, lp))
    gs = graw[1:]
    hfs, hbs, _, _, _ = mlstm_scan(p, 1, bs, seg, gs, jnp.swapaxes(gs, 1, 2), b_gate,
                                   st_c.reshape(bs, 8, M_DK, M_DV), st_n.reshape(bs, 8, M_DK),
                                   jnp.broadcast_to(st_m.reshape(bs, 8, 1), (bs, 8, M_DK)), MLSTM_CHUNK)
    x3 = outproj_ln("mlstm", (_stream(hfp, hfs), _stream(hbp, hbs)), x3, mod3, 2, w_out.astype(BF16), ln_g, ln_b,
                    norm_w=norm_w, og=p, og_col=2)
    return (x3, c_new.reshape(bp, 2, M_HEADS, M_DK, M_DV), n_new.reshape(bp, 2, M_HEADS, M_DK),
            m_new[:, :, 0].reshape(bp, 2, M_HEADS))


def gla_layer(x3, mod3, bp, lp, st_s, w_in, w_gate2, b_gate2, norm_w, w_out, ln_g, ln_b):
    nseg, seg, _ = x3.shape
    bs = nseg - 1
    p = mod_matmul(x3, mod3, 0, _pad_cols(w_in, NPROJ).astype(BF16))
    gr = p[:, :, 3072:3104]
    zs = jnp.zeros((bp, 8, G_DV, G_DK), F32)
    ofp, obp, s_new = gla_scan(p.reshape(nseg * bp, lp, NPROJ), 0, bp, lp, gr[0].reshape(bp, lp, 32),
                               w_gate2, b_gate2, zs, GLA_CHUNK)
    s0t = jnp.swapaxes(st_s.reshape(bs, 8, G_DK, G_DV), -1, -2)
    ofs, obs, _ = gla_scan(p, 1, bs, seg, gr[1:], w_gate2, b_gate2, s0t, GLA_CHUNK)
    x3 = outproj_ln("gla", (_stream(ofp, ofs), _stream(obp, obs)), x3, mod3, 2, w_out.astype(BF16), ln_g, ln_b,
                    norm_w=jnp.tile(norm_w, G_HEADS), og=p, og_col=2)
    return x3, jnp.swapaxes(s_new, -1, -2).reshape(bp, 2, G_HEADS, G_DK, G_DV)


def na_layer(x3, mod3, bp, lp, cache_k, cache_v, w_in, rpb, w_out, ln_g, ln_b):
    nseg, seg, _ = x3.shape
    bs = nseg - 1
    hd = NA_HEADS * NA_HD
    p = mod_matmul(x3, mod3, 0, w_in.astype(BF16))
    pp = p[0].reshape(bp, lp, 3 * hd)
    hm = lambda a: _head_major(a, NA_HEADS).astype(BF16)
    yp = attention(hm(pp[..., :hd]), hm(pp[..., hd:2 * hd]), hm(pp[..., 2 * hd:]), lp)
    ps = p[1:]
    ys = na_attention(hm(ps[..., :hd]), hm(ps[..., hd:2 * hd]), hm(ps[..., 2 * hd:]),
                      hm(cache_k.reshape(bs, -1, hd)), hm(cache_v.reshape(bs, -1, hd)), na_bias_table(rpb))
    x3 = outproj_ln("plain", _stream(_token_major(yp), _token_major(ys)), x3, mod3, 2, w_out.astype(BF16), ln_g, ln_b)
    return (x3, pp[..., hd:2 * hd].reshape(bp, lp, NA_HEADS, NA_HD), pp[..., 2 * hd:].reshape(bp, lp, NA_HEADS, NA_HD))


def _rope_rotated_cols(w):
    q = MLA_ROPE // 4
    return jnp.concatenate([-w[..., q:2 * q], w[..., :q], -w[..., 3 * q:], w[..., 2 * q:3 * q]], axis=-1)


def _rope_tables(ts):
    ra = MLA_ROPE // 2
    t = np.arange(ts)
    inv = 1.0 / (ROPE_BASE ** (np.arange(0, ra, 2, dtype=np.float32) / ra))
    ang_r = (t // GRID_W).astype(np.float32)[:, None] * inv[None, :]
    ang_c = (t % GRID_W).astype(np.float32)[:, None] * inv[None, :]
    ang = np.concatenate([ang_r, ang_r, ang_c, ang_c], axis=-1).astype(np.float32)
    return jnp.cos(jnp.asarray(ang)), jnp.sin(jnp.asarray(ang))


def mla_layer(x3, mod3, bp, lp, cache_ckv, cache_kpe, w_in, q_norm, w_qup, kv_norm, w_kvup, w_out, ln_g, ln_b):
    nseg, seg, _ = x3.shape
    bs = nseg - 1
    nq = MLA_Q_LORA + MLA_KV_LORA
    w_ext = jnp.concatenate([w_in, _rope_rotated_cols(w_in[:, nq:])], axis=1)
    p = mod_matmul(x3, mod3, 0, _pad_cols(w_ext, 896).astype(BF16))
    cos_t, sin_t = _rope_tables(seg)
    cos3 = jnp.concatenate([jnp.ones((1, seg, MLA_ROPE), F32), jnp.broadcast_to(cos_t, (bs, seg, MLA_ROPE))], 0)
    sin3 = jnp.concatenate([jnp.zeros((1, seg, MLA_ROPE), F32), jnp.broadcast_to(sin_t, (bs, seg, MLA_ROPE))], 0)
    wq = w_qup.reshape(MLA_Q_LORA, MLA_HEADS, MLA_NOPE + MLA_ROPE)
    wq_rope = wq[:, :, MLA_NOPE:]
    w_q3 = jnp.concatenate([wq[:, :, :MLA_NOPE].reshape(MLA_Q_LORA, -1), wq_rope.reshape(MLA_Q_LORA, -1),
                            _rope_rotated_cols(wq_rope).reshape(MLA_Q_LORA, -1)], axis=1).astype(BF16)
    q_all = mla_q(p, q_norm, w_q3, jnp.tile(cos3, (1, 1, MLA_HEADS)), jnp.tile(sin3, (1, 1, MLA_HEADS)))
    wkv = w_kvup.reshape(MLA_KV_LORA, MLA_HEADS, MLA_NOPE + MLA_VD)
    w_kv2 = jnp.concatenate([wkv[:, :, :MLA_NOPE].reshape(MLA_KV_LORA, -1),
                             wkv[:, :, MLA_NOPE:].reshape(MLA_KV_LORA, -1)], axis=1).astype(BF16)
    ckvn, kpe, kv = mla_kv(p, kv_norm, w_kv2, cos3, sin3)
    kvc = matmul(cache_ckv.reshape(-1, MLA_KV_LORA), w_kv2, 512).reshape(bs, -1, w_kv2.shape[1])
    nn = MLA_HEADS * MLA_NOPE

    def heads(q_rows, kv_rows, kpe_rows):
        b, t, _ = q_rows.shape
        tk = kv_rows.shape[1]
        qh = jnp.concatenate([q_rows[..., :nn].reshape(b, t, MLA_HEADS, MLA_NOPE),
                              q_rows[..., nn:].reshape(b, t, MLA_HEADS, MLA_ROPE)], -1)
        kh = jnp.concatenate([kv_rows[..., :nn].reshape(b, tk, MLA_HEADS, MLA_NOPE),
                              jnp.broadcast_to(kpe_rows[:, :, None, :], (b, tk, MLA_HEADS, MLA_ROPE))], -1)
        vh = kv_rows[..., nn:].reshape(b, tk, MLA_HEADS, MLA_VD)
        tr = lambda a: jnp.transpose(a, (0, 2, 1, 3)).astype(BF16)
        return tr(qh), tr(kh), tr(vh)

    yp = attention(*heads(q_all[0].reshape(bp, lp, -1), kv[0].reshape(bp, lp, -1), kpe[0].reshape(bp, lp, -1)), lp)
    ys = attention(*heads(q_all[1:], jnp.concatenate([kv[1:], kvc], 1), jnp.concatenate([kpe[1:], cache_kpe], 1)), 256)
    x3 = outproj_ln("plain", _stream(_token_major(yp), _token_major(ys)), x3, mod3, 2, w_out.astype(BF16), ln_g, ln_b)
    return x3, ckvn[0].reshape(bp, lp, MLA_KV_LORA), kpe[0].reshape(bp, lp, MLA_ROPE)


def kernel(x_prompt, x_sample, c, c_ctx, state_mlstm_C, state_mlstm_n, state_mlstm_m, state_gla_S, cache_na_k, cache_na_v, cache_mla_ckv, cache_mla_kpe, ada_w, ada_b, ln_mix_g, ln_mix_b, ln_ffn_g, ln_ffn_b, mlstm_w_in, mlstm_b_gate, mlstm_norm_w, mlstm_w_out, gla_w_in, gla_w_gate2, gla_b_gate2, gla_norm_w, gla_w_out, na_w_in, na_rpb, na_w_out, mla_w_in, mla_q_norm, mla_w_qup, mla_kv_norm, mla_w_kvup, mla_w_out, peer_w_q, peer_subkeys, peer_u, peer_v):
    bp, lp, d = x_prompt.shape
    bs, ts, _ = x_sample.shape
    assert bp * lp == ts and bs + 1 <= 8
    x3 = _stream(x_prompt, x_sample)
    cond8 = jnp.zeros((8, d), F32).at[0].set(c_ctx).at[1:1 + bs].set(c)
    mods = adaln_all(cond8, ada_w, ada_b)
    u_all = peer_u.astype(BF16)
    vt_all = jnp.swapaxes(peer_v, 1, 2).astype(BF16)
    outs = {}
    for l in range(DEPTH):
        mod3 = mods[l].reshape(8, 1, ADA_CHUNKS * d)
        kind = l % 4
        if kind == 0:
            x3, outs["C"], outs["n"], outs["m"] = mlstm_layer(
                x3, mod3, bp, lp, state_mlstm_C, state_mlstm_n, state_mlstm_m, mlstm_w_in, mlstm_b_gate,
                mlstm_norm_w, mlstm_w_out, ln_mix_g[l], ln_mix_b[l])
        elif kind == 1:
            x3, outs["S"] = gla_layer(x3, mod3, bp, lp, state_gla_S, gla_w_in, gla_w_gate2, gla_b_gate2,
                                      gla_norm_w, gla_w_out, ln_mix_g[l], ln_mix_b[l])
        elif kind == 2:
            x3, outs["nk"], outs["nv"] = na_layer(x3, mod3, bp, lp, cache_na_k, cache_na_v, na_w_in, na_rpb,
                                                  na_w_out, ln_mix_g[l], ln_mix_b[l])
        else:
            x3, outs["ckv"], outs["kpe"] = mla_layer(x3, mod3, bp, lp, cache_mla_ckv, cache_mla_kpe, mla_w_in,
                                                     mla_q_norm, mla_w_qup, mla_kv_norm, mla_w_kvup, mla_w_out,
                                                     ln_mix_g[l], ln_mix_b[l])
        x3 = peer_layer(x3, mod3, l, peer_w_q[l], peer_subkeys[l], u_all, vt_all, ln_ffn_g[l], ln_ffn_b[l])
    return (x3[0].reshape(bp, lp, d), x3[1:], outs["C"], outs["n"], outs["m"], outs["S"], outs["nk"], outs["nv"],
            outs["ckv"], outs["kpe"])
```

```python
import functools

import numpy as np
import jax
import jax.numpy as jnp
from jax import lax
from jax.experimental import pallas as pl
from jax.experimental.pallas import tpu as pltpu

D_MODEL = 1024
DEPTH = 4
GRID_W = 64
DEEPNORM_ALPHA = (2.0 * DEPTH) ** 0.25
ADA_CHUNKS = 6
NORM_EPS = 1e-5
SEG = 4096
NSEG = 3

M_HEADS, M_DK, M_DV = 4, 128, 256
G_HEADS, G_DK, G_DV = 4, 128, 256
G_GATE_RANK = 16
G_GATE_NORM = 16.0
NA_HEADS, NA_HD, NA_ROWS, NA_COLS = 16, 64, 8, 16
MLA_HEADS, MLA_Q_LORA, MLA_KV_LORA, MLA_NOPE, MLA_ROPE, MLA_VD = 16, 512, 256, 64, 32, 64
ROPE_BASE = 10000.0
PEER_HEADS, PEER_NKEYS, PEER_HALF, PEER_TOPK = 8, 128, 128, 16

V7X_VMEM_LIMIT = 56 * 1024 * 1024
F32 = jnp.float32
BF16 = jnp.bfloat16
NEG_INF = float("-inf")


def _params(sem, vmem=V7X_VMEM_LIMIT):
    return pltpu.CompilerParams(dimension_semantics=sem, vmem_limit_bytes=vmem)


def _dot(a, b, dims=((1,), (0,))):
    return lax.dot_general(a, b, (dims, ((), ())), preferred_element_type=F32)


def _split3(a):
    hi = a.astype(BF16)
    r1 = a - hi.astype(F32)
    mid = r1.astype(BF16)
    lo = (r1 - mid.astype(F32)).astype(BF16)
    return hi, mid, lo


def _dot_exact_lhs(m01, a):
    hi, mid, lo = _split3(a)
    return _dot(m01, hi) + _dot(m01, mid) + _dot(m01, lo)


def _dot_exact_rhs(a, m01):
    hi, mid, lo = _split3(a)
    return _dot(hi, m01) + _dot(mid, m01) + _dot(lo, m01)


def _log_sigmoid(x):
    return jnp.minimum(x, 0.0) - jnp.log(1.0 + jnp.exp(-jnp.abs(x)))


def _sigmoid(x):
    return 1.0 / (1.0 + jnp.exp(-x))


def _adaln_kernel(c_ref, w_ref, b_ref, o_ref):
    cv = c_ref[...]
    a = cv * _sigmoid(cv)
    o_ref[0] = lax.dot_general(a, w_ref[0], (((1,), (0,)), ((), ())), precision=lax.Precision.HIGHEST,
                               preferred_element_type=F32) + b_ref[0]


def adaln_all(cond8, ada_w, ada_b):
    tn = 1024
    n = ada_w.shape[-1]
    return pl.pallas_call(
        _adaln_kernel,
        grid=(DEPTH, n // tn),
        in_specs=[pl.BlockSpec((8, D_MODEL), lambda l, j: (0, 0)),
                  pl.BlockSpec((1, D_MODEL, tn), lambda l, j: (l, 0, j)),
                  pl.BlockSpec((1, 1, tn), lambda l, j: (l, 0, j))],
        out_specs=pl.BlockSpec((1, 8, tn), lambda l, j: (l, 0, j)),
        out_shape=jax.ShapeDtypeStruct((DEPTH, 8, n), F32),
        compiler_params=_params(("arbitrary", "arbitrary")),
        name="adaln",
    )(cond8, ada_w, ada_b.reshape(DEPTH, 1, n))


def _modmm_kernel(x_ref, sh_ref, sc_ref, w_ref, o_ref, xm_ref):
    @pl.when(pl.program_id(2) == 0)
    def _():
        xm_ref[...] = (x_ref[0] * (1.0 + sc_ref[0]) + sh_ref[0]).astype(BF16)

    o_ref[0] = _dot(xm_ref[...], w_ref[...]).astype(o_ref.dtype)


def mod_matmul(x3, mod3, shift_chunk, w_bf16, tm=512, tn=None, out_dtype=F32):
    nseg, seg, d = x3.shape
    n = w_bf16.shape[1]
    tn = n if tn is None else tn
    return pl.pallas_call(
        _modmm_kernel,
        grid=(nseg, seg // tm, n // tn),
        in_specs=[pl.BlockSpec((1, tm, d), lambda s, i, j: (s, i, 0)),
                  pl.BlockSpec((1, 1, d), lambda s, i, j: (s, 0, shift_chunk)),
                  pl.BlockSpec((1, 1, d), lambda s, i, j: (s, 0, shift_chunk + 1)),
                  pl.BlockSpec((d, tn), lambda s, i, j: (0, j))],
        out_specs=pl.BlockSpec((1, tm, tn), lambda s, i, j: (s, i, j)),
        out_shape=jax.ShapeDtypeStruct((nseg, seg, n), out_dtype),
        scratch_shapes=[pltpu.VMEM((tm, d), BF16)],
        compiler_params=_params(("arbitrary", "arbitrary", "arbitrary")),
        name="mod_matmul",
    )(x3, mod3, mod3, w_bf16)


def _layer_norm_rows(y, g, b):
    mu = jnp.mean(y, axis=-1, keepdims=True)
    yc = y - mu
    var = jnp.mean(yc * yc, axis=-1, keepdims=True)
    return yc * lax.rsqrt(var + NORM_EPS) * g + b


def _outproj_kernel(*refs, mode):
    if mode == "plain":
        y_ref, x_ref, gate_ref, w_ref, g_ref, b_ref, o_ref = refs
        yin = y_ref[0].astype(BF16)
    else:
        ya_ref, yb_ref, og_ref, nw_ref, x_ref, gate_ref, w_ref, g_ref, b_ref, o_ref = refs
        hs = ya_ref[0] + yb_ref[0]
        og = og_ref[0]
        parts = []
        for h in range(4):
            seg = hs[:, h * 256:(h + 1) * 256]
            nw = nw_ref[:, h * 256:(h + 1) * 256]
            if mode == "mlstm":
                mu = jnp.mean(seg, axis=-1, keepdims=True)
                sc = seg - mu
                var = jnp.mean(sc * sc, axis=-1, keepdims=True)
                parts.append(sc * lax.rsqrt(var + NORM_EPS) * nw)
            else:
                ms = jnp.mean(seg * seg, axis=-1, keepdims=True)
                parts.append(seg * lax.rsqrt(ms + NORM_EPS) * nw)
        hn = jnp.concatenate(parts, axis=-1)
        act = _sigmoid(og) if mode == "mlstm" else og * _sigmoid(og)
        yin = (act * hn).astype(BF16)
    y = _dot(yin, w_ref[...])
    z = DEEPNORM_ALPHA * x_ref[0] + gate_ref[0] * y
    o_ref[0] = _layer_norm_rows(z, g_ref[...], b_ref[...])


def outproj_ln(mode, ys, x3, mod3, gate_chunk, w_bf16, ln_g, ln_b, norm_w=None, og=None, og_col=0, tm=512):
    nseg, seg, d = x3.shape
    k = w_bf16.shape[0]
    tok = lambda s, i: (s, i, 0)
    if mode == "plain":
        args = [ys]
        specs = [pl.BlockSpec((1, tm, k), tok)]
    else:
        args = [ys[0], ys[1], og, norm_w.reshape(1, k)]
        specs = [pl.BlockSpec((1, tm, k), tok), pl.BlockSpec((1, tm, k), tok),
                 pl.BlockSpec((1, tm, k), lambda s, i: (s, i, og_col)),
                 pl.BlockSpec((1, k), lambda s, i: (0, 0))]
    args += [x3, mod3, w_bf16, ln_g.reshape(1, d), ln_b.reshape(1, d)]
    specs += [pl.BlockSpec((1, tm, d), tok),
              pl.BlockSpec((1, 1, d), lambda s, i: (s, 0, gate_chunk)),
              pl.BlockSpec((k, d), lambda s, i: (0, 0)),
              pl.BlockSpec((1, d), lambda s, i: (0, 0)),
              pl.BlockSpec((1, d), lambda s, i: (0, 0))]
    return pl.pallas_call(
        functools.partial(_outproj_kernel, mode=mode),
        grid=(nseg, seg // tm),
        in_specs=specs,
        out_specs=pl.BlockSpec((1, tm, d), tok),
        out_shape=jax.ShapeDtypeStruct((nseg, seg, d), F32),
        compiler_params=_params(("arbitrary", "arbitrary")),
        name="outproj_ln_" + mode,
    )(*args)


def _tri(n, lower):
    r = lax.broadcasted_iota(jnp.int32, (n, n), 0)
    c = lax.broadcasted_iota(jnp.int32, (n, n), 1)
    return (c <= r) if lower else (c >= r)


def _mlstm_kernel(pf_ref, pb_ref, gf_ref, gb_ref, gtf_ref, gtb_ref, bias_ref, biast_ref,
                  c0_ref, n0_ref, m0_ref, hf_ref, hb_ref, co_ref, no_ref, mo_ref,
                  c_s, n_s, m_s, *, L):
    c = pl.program_id(1)

    @pl.when(c == 0)
    def _():
        c_s[...] = c0_ref[0]
        n_s[...] = n0_ref[0]
        m_s[...] = m0_ref[0]

    for d in range(2):
        p_ref, g_ref, gt_ref, h_ref = ((pf_ref, gf_ref, gtf_ref, hf_ref) if d == 0
                                       else (pb_ref, gb_ref, gtb_ref, hb_ref))
        mask = _tri(L, lower=(d == 0))
        mcol = mask.astype(BF16)
        mrow = _tri(L, lower=(d != 0)).astype(BF16)
        g = g_ref[0] + bias_ref[...]
        gt = gt_ref[0] + biast_ref[...]
        li_c = g[:, d * 8:d * 8 + 4]
        lf_c = _log_sigmoid(g[:, d * 8 + 4:d * 8 + 8])
        li_r = gt[d * 8:d * 8 + 4, :]
        lf_r = _log_sigmoid(gt[d * 8 + 4:d * 8 + 8, :])
        b_c = _dot_exact_lhs(mcol, lf_c)
        b_r = _dot_exact_rhs(lf_r, mrow)
        last = L - 1 if d == 0 else 0
        for h in range(M_HEADS):
            u = d * M_HEADS + h
            q = p_ref[0, :, h * M_DK:(h + 1) * M_DK]
            k = p_ref[0, :, 512 + h * M_DK:512 + (h + 1) * M_DK] * (M_DK ** -0.5)
            v = p_ref[0, :, 1024 + h * M_DV:1024 + (h + 1) * M_DV].astype(BF16)
            qb = q.astype(BF16)
            bc, br = b_c[:, h:h + 1], b_r[h:h + 1, :]
            lic, lir = li_c[:, h:h + 1], li_r[h:h + 1, :]
            m_prev = m_s[u:u + 1, 0:1]
            dmat = jnp.where(mask, bc - br + lir, NEG_INF)
            inter = bc + m_prev
            mt = jnp.maximum(inter, jnp.max(dmat, axis=-1, keepdims=True))
            smat = _dot(qb, k.astype(BF16), ((1,), (1,))) * jnp.exp(dmat - mt)
            ei = jnp.exp(inter - mt)
            cmat = c_s[u]
            num = _dot(smat.astype(BF16), v) + ei * _dot(qb, cmat.astype(BF16))
            nrow = n_s[u:u + 1, :]
            den = jnp.sum(smat, axis=-1, keepdims=True) + ei * jnp.sum(q * nrow, axis=-1, keepdims=True)
            h_ref[0, :, h * M_DV:(h + 1) * M_DV] = num / jnp.maximum(jnp.abs(den), jnp.exp(-mt))
            tot = br[:, last:last + 1]
            g_c = tot - bc + lic
            g_r = tot - br + lir
            m_new = jnp.maximum(tot + m_prev, jnp.max(g_r, axis=-1, keepdims=True))
            kw = k * jnp.exp(g_c - m_new)
            dec = jnp.exp(tot + m_prev - m_new)
            c_s[u] = dec * cmat + _dot(kw.astype(BF16), v, ((0,), (0,)))
            n_s[u:u + 1, :] = dec * nrow + jnp.sum(kw, axis=0, keepdims=True)
            m_s[u:u + 1, :] = jnp.broadcast_to(m_new, (1, 128))

    @pl.when(c == pl.num_programs(1) - 1)
    def _():
        co_ref[0] = c_s[...]
        no_ref[0] = n_s[...]
        mo_ref[0] = m_s[...]


def mlstm_scan(p, b0, nb, t, g, gt, bias, c0, n0, m0, L):
    nc = t // L
    hshape = jax.ShapeDtypeStruct((nb, t, M_HEADS * M_DV), F32)
    fwd = lambda b, c: (b + b0, c, 0)
    bwd = lambda b, c: (b + b0, nc - 1 - c, 0)
    st4 = lambda b, c: (b, 0, 0, 0)
    st3 = lambda b, c: (b, 0, 0)
    return pl.pallas_call(
        functools.partial(_mlstm_kernel, L=L),
        grid=(nb, nc),
        in_specs=[pl.BlockSpec((1, L, 2048), fwd), pl.BlockSpec((1, L, 2048), bwd),
                  pl.BlockSpec((1, L, 16), lambda b, c: (b, c, 0)),
                  pl.BlockSpec((1, L, 16), lambda b, c: (b, nc - 1 - c, 0)),
                  pl.BlockSpec((1, 16, L), lambda b, c: (b, 0, c)),
                  pl.BlockSpec((1, 16, L), lambda b, c: (b, 0, nc - 1 - c)),
                  pl.BlockSpec((1, 16), lambda b, c: (0, 0)),
                  pl.BlockSpec((16, 1), lambda b, c: (0, 0)),
                  pl.BlockSpec((1, 8, M_DK, M_DV), st4),
                  pl.BlockSpec((1, 8, M_DK), st3),
                  pl.BlockSpec((1, 8, M_DK), st3)],
        out_specs=[pl.BlockSpec((1, L, 1024), lambda b, c: (b, c, 0)),
                   pl.BlockSpec((1, L, 1024), lambda b, c: (b, nc - 1 - c, 0)),
                   pl.BlockSpec((1, 8, M_DK, M_DV), st4),
                   pl.BlockSpec((1, 8, M_DK), st3),
                   pl.BlockSpec((1, 8, M_DK), st3)],
        out_shape=[hshape, hshape,
                   jax.ShapeDtypeStruct((nb, 8, M_DK, M_DV), F32),
                   jax.ShapeDtypeStruct((nb, 8, M_DK), F32),
                   jax.ShapeDtypeStruct((nb, 8, M_DK), F32)],
        scratch_shapes=[pltpu.VMEM((8, M_DK, M_DV), F32), pltpu.VMEM((8, M_DK), F32),
                        pltpu.VMEM((8, M_DK), F32)],
        compiler_params=_params(("arbitrary", "arbitrary")),
        name="mlstm_scan",
    )(p, p, g, g, gt, gt, bias.reshape(1, 16), bias.reshape(16, 1), c0, n0, m0)


def _gla_kernel(pf_ref, pb_ref, gf_ref, gb_ref, w2_ref, b2_ref, s0_ref, of_ref, ob_ref, so_ref, s_s, *, L):
    c = pl.program_id(1)

    @pl.when(c == 0)
    def _():
        s_s[...] = s0_ref[0]

    for d in range(2):
        p_ref, g_ref, o_ref = (pf_ref, gf_ref, of_ref) if d == 0 else (pb_ref, gb_ref, ob_ref)
        mask = _tri(L, lower=(d == 0))
        mcol = mask.astype(BF16)
        gr = g_ref[0][:, d * G_GATE_RANK:(d + 1) * G_GATE_RANK]
        pre = lax.dot_general(gr, w2_ref[d], (((1,), (0,)), ((), ())), precision=lax.Precision.HIGHEST,
                              preferred_element_type=F32) + b2_ref[d]
        la = _log_sigmoid(pre) * (1.0 / G_GATE_NORM)
        bc_all = _dot_exact_lhs(mcol, la)
        last = L - 1 if d == 0 else 0
        for h in range(G_HEADS):
            u = d * G_HEADS + h
            q = p_ref[0, :, h * G_DK:(h + 1) * G_DK] * (G_DK ** -0.5)
            k = p_ref[0, :, 512 + h * G_DK:512 + (h + 1) * G_DK]
            v = p_ref[0, :, 1024 + h * G_DV:1024 + (h + 1) * G_DV].astype(BF16)
            bc = bc_all[:, h * G_DK:(h + 1) * G_DK]
            qd = (q * jnp.exp(bc)).astype(BF16)
            kd = (k * jnp.exp(-bc)).astype(BF16)
            a = jnp.where(mask, _dot(qd, kd, ((1,), (1,))), 0.0)
            st = s_s[u]
            o_ref[0, :, h * G_DV:(h + 1) * G_DV] = (_dot(a.astype(BF16), v)
                                                    + _dot(qd, st.astype(BF16), ((1,), (1,))))
            bl = bc[last:last + 1, :]
            kl = (k * jnp.exp(bl - bc)).astype(BF16)
            s_s[u] = st * jnp.exp(bl) + _dot(v, kl, ((0,), (0,)))

    @pl.when(c == pl.num_programs(1) - 1)
    def _():
        so_ref[0] = s_s[...]


def gla_scan(p, b0, nb, t, gr, w2, b2, s0t, L):
    nc = t // L
    oshape = jax.ShapeDtypeStruct((nb, t, G_HEADS * G_DV), F32)
    st4 = lambda b, c: (b, 0, 0, 0)
    return pl.pallas_call(
        functools.partial(_gla_kernel, L=L),
        grid=(nb, nc),
        in_specs=[pl.BlockSpec((1, L, 2048), lambda b, c: (b + b0, c, 0)),
                  pl.BlockSpec((1, L, 2048), lambda b, c: (b + b0, nc - 1 - c, 0)),
                  pl.BlockSpec((1, L, 32), lambda b, c: (b, c, 0)),
                  pl.BlockSpec((1, L, 32), lambda b, c: (b, nc - 1 - c, 0)),
                  pl.BlockSpec((2, G_GATE_RANK, 512), lambda b, c: (0, 0, 0)),
                  pl.BlockSpec((2, 1, 512), lambda b, c: (0, 0, 0)),
                  pl.BlockSpec((1, 8, G_DV, G_DK), st4)],
        out_specs=[pl.BlockSpec((1, L, 1024), lambda b, c: (b, c, 0)),
                   pl.BlockSpec((1, L, 1024), lambda b, c: (b, nc - 1 - c, 0)),
                   pl.BlockSpec((1, 8, G_DV, G_DK), st4)],
        out_shape=[oshape, oshape, jax.ShapeDtypeStruct((nb, 8, G_DV, G_DK), F32)],
        scratch_shapes=[pltpu.VMEM((8, G_DV, G_DK), F32)],
        compiler_params=_params(("arbitrary", "arbitrary")),
        name="gla_scan",
    )(p, p, gr, gr, w2, b2.reshape(2, 1, 512), s0t)


def _attn_kernel(q_ref, k_ref, v_ref, o_ref, *, scale):
    s = _dot(q_ref[0, 0], k_ref[0, 0], ((1,), (1,))) * scale
    m = jnp.max(s, axis=-1, keepdims=True)
    p = jnp.exp(s - m)
    l = jnp.sum(p, axis=-1, keepdims=True)
    o_ref[0, 0] = _dot(p.astype(BF16), v_ref[0, 0]) / l


def attention(q, k, v, tq):
    b, h, lq, dq = q.shape
    lk, dv = k.shape[2], v.shape[3]
    return pl.pallas_call(
        functools.partial(_attn_kernel, scale=dq ** -0.5),
        grid=(b, h, lq // tq),
        in_specs=[pl.BlockSpec((1, 1, tq, dq), lambda b, h, i: (b, h, i, 0)),
                  pl.BlockSpec((1, 1, lk, dq), lambda b, h, i: (b, h, 0, 0)),
                  pl.BlockSpec((1, 1, lk, dv), lambda b, h, i: (b, h, 0, 0))],
        out_specs=pl.BlockSpec((1, 1, tq, dv), lambda b, h, i: (b, h, i, 0)),
        out_shape=jax.ShapeDtypeStruct((b, h, lq, dv), F32),
        compiler_params=_params(("arbitrary", "arbitrary", "arbitrary")),
        name="attention",
    )(q, k, v)


NA_RB = 8


def _na_kernel(q_ref, k_ref, v_ref, kc_ref, vc_ref, bias_ref, o_ref, *, rows):
    j = pl.program_id(2)
    scale = NA_HD ** -0.5
    kc, vc = kc_ref[0, 0], vc_ref[0, 0]
    for a in range(NA_RB):
        r = j * NA_RB + a
        start = jnp.clip(r - NA_ROWS // 2, 0, rows - NA_ROWS)
        dr0 = start - r + (NA_ROWS - 1)
        off = pl.multiple_of(start * GRID_W, GRID_W)
        qa = q_ref[0, 0, a * GRID_W:(a + 1) * GRID_W, :]
        kl = k_ref[0, 0, pl.ds(off, NA_ROWS * GRID_W), :]
        vl = v_ref[0, 0, pl.ds(off, NA_ROWS * GRID_W), :]
        s_loc = _dot(qa, kl, ((1,), (1,))) * scale + bias_ref[0, dr0]
        s_ctx = _dot(qa, kc, ((1,), (1,))) * scale
        m = jnp.maximum(jnp.max(s_loc, axis=-1, keepdims=True), jnp.max(s_ctx, axis=-1, keepdims=True))
        p_loc = jnp.exp(s_loc - m)
        p_ctx = jnp.exp(s_ctx - m)
        l = jnp.sum(p_loc, axis=-1, keepdims=True) + jnp.sum(p_ctx, axis=-1, keepdims=True)
        o = _dot(p_loc.astype(BF16), vl) + _dot(p_ctx.astype(BF16), vc)
        o_ref[0, 0, a * GRID_W:(a + 1) * GRID_W, :] = o / l


def na_bias_table(rpb):
    cq = np.arange(GRID_W)[:, None]
    ck = np.arange(GRID_W)[None, :]
    cs = np.clip(cq - NA_COLS // 2, 0, GRID_W - NA_COLS)
    ok = (ck >= cs) & (ck < cs + NA_COLS)
    dc = np.clip(ck - cq, -(NA_COLS - 1), NA_COLS - 1) + (NA_COLS - 1)
    t = jnp.where(ok[None, None], rpb.astype(F32)[:, :, dc], NEG_INF)
    rows = np.arange(NA_ROWS)[:, None] + np.arange(NA_ROWS)[None, :]
    tf = t[:, rows]
    return jnp.transpose(tf, (0, 1, 3, 2, 4)).reshape(NA_HEADS, NA_ROWS, GRID_W, NA_ROWS * GRID_W)


def na_attention(q, k, v, kc, vc, bias):
    b, h, t, dh = q.shape
    lc = kc.shape[2]
    rows = t // GRID_W
    full = lambda b, h, j: (b, h, 0, 0)
    return pl.pallas_call(
        functools.partial(_na_kernel, rows=rows),
        grid=(b, h, rows // NA_RB),
        in_specs=[pl.BlockSpec((1, 1, NA_RB * GRID_W, dh), lambda b, h, j: (b, h, j, 0)),
                  pl.BlockSpec((1, 1, t, dh), full), pl.BlockSpec((1, 1, t, dh), full),
                  pl.BlockSpec((1, 1, lc, dh), full), pl.BlockSpec((1, 1, lc, dh), full),
                  pl.BlockSpec((1, NA_ROWS, GRID_W, NA_ROWS * GRID_W), lambda b, h, j: (h, 0, 0, 0))],
        out_specs=pl.BlockSpec((1, 1, NA_RB * GRID_W, dh), lambda b, h, j: (b, h, j, 0)),
        out_shape=jax.ShapeDtypeStruct((b, h, t, dh), F32),
        compiler_params=_params(("arbitrary", "arbitrary", "arbitrary")),
        name="na_attention",
    )(q, k, v, kc, vc, bias)


def _rms_rows(x, g):
    return x * lax.rsqrt(jnp.mean(x * x, axis=-1, keepdims=True) + NORM_EPS) * g


def _mla_q_kernel(cq_ref, g_ref, w_ref, cos_ref, sin_ref, o_ref):
    r = _dot(_rms_rows(cq_ref[0], g_ref[...]).astype(BF16), w_ref[...])
    nn = MLA_HEADS * MLA_NOPE
    nr = MLA_HEADS * MLA_ROPE
    o_ref[0, :, :nn] = r[:, :nn]
    o_ref[0, :, nn:] = r[:, nn:nn + nr] * cos_ref[0] + r[:, nn + nr:] * sin_ref[0]


def mla_q(p, q_norm, w_q3, cos_q, sin_q, tm=512):
    nseg, seg, _ = p.shape
    nout = MLA_HEADS * (MLA_NOPE + MLA_ROPE)
    nr = MLA_HEADS * MLA_ROPE
    tok = lambda s, i: (s, i, 0)
    return pl.pallas_call(
        _mla_q_kernel,
        grid=(nseg, seg // tm),
        in_specs=[pl.BlockSpec((1, tm, MLA_Q_LORA), tok),
                  pl.BlockSpec((1, MLA_Q_LORA), lambda s, i: (0, 0)),
                  pl.BlockSpec(w_q3.shape, lambda s, i: (0, 0)),
                  pl.BlockSpec((1, tm, nr), tok), pl.BlockSpec((1, tm, nr), tok)],
        out_specs=pl.BlockSpec((1, tm, nout), tok),
        out_shape=jax.ShapeDtypeStruct((nseg, seg, nout), F32),
        compiler_params=_params(("arbitrary", "arbitrary")),
        name="mla_q",
    )(p, q_norm.reshape(1, -1), w_q3, cos_q, sin_q)


def _mla_kv_kernel(ckv_ref, kpe_ref, g_ref, w_ref, cos_ref, sin_ref, ckvn_ref, kpeo_ref, kv_ref):
    cn = _rms_rows(ckv_ref[0], g_ref[...])
    ckvn_ref[0] = cn
    kv_ref[0] = _dot(cn.astype(BF16), w_ref[...])
    kp = kpe_ref[0]
    kpeo_ref[0] = kp[:, :MLA_ROPE] * cos_ref[0] + kp[:, MLA_ROPE:2 * MLA_ROPE] * sin_ref[0]


def mla_kv(p, kv_norm, w_kv, cos_k, sin_k, tm=512):
    nseg, seg, _ = p.shape
    nkv = w_kv.shape[1]
    tok = lambda s, i: (s, i, 0)
    return pl.pallas_call(
        _mla_kv_kernel,
        grid=(nseg, seg // tm),
        in_specs=[pl.BlockSpec((1, tm, MLA_KV_LORA), lambda s, i: (s, i, MLA_Q_LORA // MLA_KV_LORA)),
                  pl.BlockSpec((1, tm, 128), lambda s, i: (s, i, (MLA_Q_LORA + MLA_KV_LORA) // 128)),
                  pl.BlockSpec((1, MLA_KV_LORA), lambda s, i: (0, 0)),
                  pl.BlockSpec(w_kv.shape, lambda s, i: (0, 0)),
                  pl.BlockSpec((1, tm, MLA_ROPE), tok), pl.BlockSpec((1, tm, MLA_ROPE), tok)],
        out_specs=[pl.BlockSpec((1, tm, MLA_KV_LORA), tok), pl.BlockSpec((1, tm, MLA_ROPE), tok),
                   pl.BlockSpec((1, tm, nkv), tok)],
        out_shape=[jax.ShapeDtypeStruct((nseg, seg, MLA_KV_LORA), F32),
                   jax.ShapeDtypeStruct((nseg, seg, MLA_ROPE), F32),
                   jax.ShapeDtypeStruct((nseg, seg, nkv), F32)],
        compiler_params=_params(("arbitrary", "arbitrary")),
        name="mla_kv",
    )(p, p, kv_norm.reshape(1, -1), w_kv, cos_k, sin_k)


def _mm_kernel(a_ref, w_ref, o_ref):
    o_ref[...] = _dot(a_ref[...].astype(BF16), w_ref[...])


def matmul(a, w_bf16, tm):
    m, k = a.shape
    n = w_bf16.shape[1]
    return pl.pallas_call(
        _mm_kernel,
        grid=(m // tm,),
        in_specs=[pl.BlockSpec((tm, k), lambda i: (i, 0)), pl.BlockSpec((k, n), lambda i: (0, 0))],
        out_specs=pl.BlockSpec((tm, n), lambda i: (i, 0)),
        out_shape=jax.ShapeDtypeStruct((m, n), F32),
        compiler_params=_params(("arbitrary",)),
        name="matmul",
    )(a, w_bf16)


PEER_RT = 128
NOT_TOP = 99.0


def _top16(s, exact):
    key = lax.broadcasted_iota(jnp.int32, s.shape, 0).astype(F32)
    rank = jnp.full(s.shape, NOT_TOP, F32)
    vals = []
    for r in range(PEER_TOPK):
        m = jnp.max(s, axis=0, keepdims=True)
        hit = s == m
        if exact:
            hit = key == jnp.min(jnp.where(hit, key, 1e9), axis=0, keepdims=True)
        rank = jnp.where(hit, float(r), rank)
        s = jnp.where(hit, NEG_INF, s)
        vals.append(m)
    return vals, rank


def _pair_topk(av, bv, exact):
    n = av[0].shape[-1]
    a_lo, a_hi = jnp.concatenate(av[:8], 0), jnp.concatenate(av[8:], 0)
    b_lo, b_hi = jnp.concatenate(bv[:8], 0), jnp.concatenate(bv[8:], 0)
    row = lax.broadcasted_iota(jnp.int32, (8, n), 0).astype(F32)

    no_pos = 1e8

    def rows_b(a, b_blk, boff, nvalid):
        ok = row < nvalid
        return jnp.where(ok, av[a] + b_blk, NEG_INF), jnp.where(ok, a * 16.0 + boff + row, no_pos)

    def rows_a(b, a_blk, aoff, lo, hi):
        ok = (row >= lo) & (row < hi)
        return jnp.where(ok, a_blk + bv[b], NEG_INF), jnp.where(ok, (aoff + row) * 16.0 + b, no_pos)

    groups = [rows_b(0, b_lo, 0, 8), rows_b(0, b_hi, 8, 8), rows_b(1, b_lo, 0, 8), rows_b(2, b_lo, 0, 5),
              rows_b(3, b_lo, 0, 4), rows_a(0, a_lo, 0, 4, 8), rows_a(0, a_hi, 8, 0, 8),
              rows_a(1, a_lo, 0, 4, 8), rows_a(2, a_lo, 0, 4, 5)]
    cands = [g[0] for g in groups]
    poss = [g[1] for g in groups]
    sels = [jnp.zeros((8, n), F32) for _ in groups]
    top = av[0] + bv[0]
    z = jnp.zeros((1, n), F32)
    for _ in range(PEER_TOPK):
        m = functools.reduce(jnp.maximum, cands)
        m = jnp.max(m, axis=0, keepdims=True)
        hits = [c == m for c in cands]
        if exact:
            first = functools.reduce(jnp.minimum, [jnp.where(hh, p, 1e9) for hh, p in zip(hits, poss)])
            first = jnp.min(first, axis=0, keepdims=True)
            hits = [p == first for p in poss]
        cands = [jnp.where(hh, NEG_INF, c) for hh, c in zip(hits, cands)]
        sels = [jnp.where(hh, 1.0, s) for hh, s in zip(hits, sels)]
        z = z + jnp.exp(m - top)
    cnt = lambda x: jnp.sum(x, axis=0, keepdims=True)
    cut_lo = sels[5] + sels[7] + sels[8]
    for a, c in enumerate([cnt(sels[0]) + cnt(sels[1]), cnt(sels[2]), cnt(sels[3]), cnt(sels[4])]):
        cut_lo = cut_lo + jnp.where(row == a, c, 0.0)
    return cut_lo, sels[6], z, cnt(cut_lo) + cnt(sels[6])


def _peer_route_kernel(x_ref, sh_ref, sc_ref, wq_ref, sk_ref, xm_ref, e1_ref, cut_ref, e2_ref, r2_ref, q_s, *, tm):
    xm = (x_ref[0] * (1.0 + sc_ref[0]) + sh_ref[0]).astype(BF16)
    xm_ref[0] = xm
    q = _dot(xm, wq_ref[...])
    for hp in range(2 * PEER_HEADS):
        q_s[hp] = q[:, hp * PEER_HALF:(hp + 1) * PEER_HALF]

    def route(h, tok, exact):
        def scores(hp):
            return lax.dot_general(sk_ref[hp], q_s[hp, tok, :], (((1,), (1,)), ((), ())),
                                   precision=lax.Precision.HIGHEST, preferred_element_type=F32)

        s1, s2 = scores(2 * h), scores(2 * h + 1)
        av, rank1 = _top16(s1, exact)
        bv, rank2 = _top16(s2, exact)
        cut_lo, cut_hi, z, nsel = _pair_topk(av, bv, exact)
        cut = jnp.zeros_like(s1)
        for r in range(PEER_TOPK):
            src = cut_lo if r < 8 else cut_hi
            cut = jnp.where(rank1 == float(r), src[r % 8:r % 8 + 1, :], cut)
        e1_ref[0, h, :, tok] = (jnp.exp(s1 - av[0]) / z).astype(BF16)
        cut_ref[0, h, :, tok] = cut.astype(BF16)
        e2_ref[0, h, :, tok] = jnp.exp(s2 - bv[0]).astype(BF16)
        r2_ref[0, h, :, tok] = rank2.astype(BF16)
        ranked = lambda rk: jnp.sum(jnp.where(rk < PEER_TOPK, 1.0, 0.0), axis=0, keepdims=True)
        return ranked(rank1), ranked(rank2), nsel

    def body(h, carry):
        toks = [pl.ds(t0, PEER_RT) for t0 in range(0, tm, PEER_RT)]
        counts = [route(h, tok, exact=False) for tok in toks]
        for tok, cnts in zip(toks, counts):
            bad = functools.reduce(jnp.maximum, [jnp.abs(cn - PEER_TOPK) for cn in cnts])

            @pl.when(jnp.max(bad) > 0.0)
            def _():
                route(h, tok, exact=True)
        return carry

    lax.fori_loop(0, PEER_HEADS, body, 0)


def peer_route(x3, mod3, shift_chunk, wq_bf16, subkeys, tm=256):
    nseg, seg, d = x3.shape
    tok = lambda s, i: (s, i, 0)
    rshape = jax.ShapeDtypeStruct((nseg, PEER_HEADS, PEER_NKEYS, seg), BF16)
    rspec = pl.BlockSpec((1, PEER_HEADS, PEER_NKEYS, tm), lambda s, i: (s, 0, 0, i))
    return pl.pallas_call(
        functools.partial(_peer_route_kernel, tm=tm),
        grid=(nseg, seg // tm),
        in_specs=[pl.BlockSpec((1, tm, d), tok),
                  pl.BlockSpec((1, 1, d), lambda s, i: (s, 0, shift_chunk)),
                  pl.BlockSpec((1, 1, d), lambda s, i: (s, 0, shift_chunk + 1)),
                  pl.BlockSpec(wq_bf16.shape, lambda s, i: (0, 0)),
                  pl.BlockSpec((2 * PEER_HEADS, PEER_NKEYS, PEER_HALF), lambda s, i: (0, 0, 0))],
        out_specs=[pl.BlockSpec((1, tm, d), tok), rspec, rspec, rspec, rspec],
        out_shape=[jax.ShapeDtypeStruct((nseg, seg, d), BF16)] + [rshape] * 4,
        scratch_shapes=[pltpu.VMEM((2 * PEER_HEADS, tm, PEER_HALF), F32)],
        compiler_params=_params(("arbitrary", "arbitrary")),
        name="peer_route",
    )(x3, mod3, mod3, wq_bf16, subkeys.reshape(2 * PEER_HEADS, PEER_NKEYS, PEER_HALF))


PEER_CE = 1024


def _gelu_tanh(x):
    return 0.5 * x * (1.0 + jnp.tanh(0.7978845608028654 * (x + 0.044715 * x * x * x)))


def _peer_dense_kernel(xm_ref, u_ref, vt_ref, e1_ref, cut_ref, e2_ref, r2_ref, x_ref, gate_ref, g_ref, b_ref,
                       o_ref, acc_s, at_s, w_s, e2_s, r2_s, e1b_s, cutb_s, *, tm):
    e = pl.program_id(2)
    nb = PEER_CE // PEER_NKEYS
    ntt = tm // PEER_RT

    @pl.when(e == 0)
    def _():
        acc_s[...] = jnp.zeros_like(acc_s)
        e2_s[:, :, :tm] = e2_ref[0]
        r2_s[:, :, :tm] = r2_ref[0]

    for h in range(PEER_HEADS):
        for ii in range(nb):
            e1b_s[ii, h, :, :tm] = jnp.broadcast_to(e1_ref[0, h, ii:ii + 1, :], (16, tm))
            cutb_s[ii, h, :, :tm] = jnp.broadcast_to(cut_ref[0, h, ii:ii + 1, :], (16, tm))

    packed = (PEER_NKEYS // 16, 16, PEER_RT)
    ng = 2
    tw = 2 * PEER_RT

    def activations(tp):
        tok = slice(tp * tw, (tp + 1) * tw)
        at_s[:, tok] = _dot(u_ref[0], xm_ref[0, tok, :], ((1,), (1,)))

    def gate_tiles(tt, i0):
        tok = slice(tt * PEER_RT, (tt + 1) * PEER_RT)
        gmats = [jnp.zeros(packed, BF16) for _ in range(ng)]
        for h in range(PEER_HEADS):
            e2 = e2_s[h, :, tok].reshape(packed)
            r2 = r2_s[h, :, tok].reshape(packed)
            for k in range(ng):
                e1 = e1b_s[i0 + k, h, :, tok][None]
                cut = cutb_s[i0 + k, h, :, tok][None]
                gmats[k] = gmats[k] + e1 * jnp.where(r2 < cut, e2, jnp.zeros_like(e2))
        for k in range(ng):
            rows = slice((i0 + k) * PEER_NKEYS, (i0 + k + 1) * PEER_NKEYS)
            act = _gelu_tanh(at_s[rows, tok]).astype(BF16)
            w_s[rows, tok] = gmats[k].reshape(PEER_NKEYS, PEER_RT) * act

    at_s[:, :tm] = _dot(u_ref[0], xm_ref[0], ((1,), (1,)))
    for tt in range(ntt):
        for i0 in range(0, nb, ng):
            gate_tiles(tt, i0)
    acc_s[:, :tm] += _dot(vt_ref[0], w_s[:, :tm])

    @pl.when(e == pl.num_programs(2) - 1)
    def _():
        z = DEEPNORM_ALPHA * x_ref[0] + gate_ref[0] * acc_s[:, :tm].T
        o_ref[0] = _layer_norm_rows(z, g_ref[...], b_ref[...])


def peer_dense(xm, u_all, vt_all, l, e1, cut, e2, r2, x3, mod3, gate_chunk, ln_g, ln_b, tm=512):
    nseg, seg, d = x3.shape
    ne = u_all.shape[1]
    nb = PEER_CE // PEER_NKEYS
    tp = tm + PEER_RT
    tok = lambda s, i, e: (s, i, 0)
    chunk = pl.BlockSpec((1, PEER_HEADS, nb, tm), lambda s, i, e: (s, 0, e, i))
    full = pl.BlockSpec((1, PEER_HEADS, PEER_NKEYS, tm), lambda s, i, e: (s, 0, 0, i))
    return pl.pallas_call(
        functools.partial(_peer_dense_kernel, tm=tm),
        grid=(nseg, seg // tm, ne // PEER_CE),
        in_specs=[pl.BlockSpec((1, tm, d), tok),
                  pl.BlockSpec((1, PEER_CE, d), lambda s, i, e: (l, e, 0)),
                  pl.BlockSpec((1, d, PEER_CE), lambda s, i, e: (l, 0, e)),
                  chunk, chunk, full, full,
                  pl.BlockSpec((1, tm, d), tok),
                  pl.BlockSpec((1, 1, d), lambda s, i, e: (s, 0, gate_chunk)),
                  pl.BlockSpec((1, d), lambda s, i, e: (0, 0)),
                  pl.BlockSpec((1, d), lambda s, i, e: (0, 0))],
        out_specs=pl.BlockSpec((1, tm, d), tok),
        out_shape=jax.ShapeDtypeStruct((nseg, seg, d), F32),
        scratch_shapes=[pltpu.VMEM((d, tp), F32), pltpu.VMEM((PEER_CE, tp), F32), pltpu.VMEM((PEER_CE, tp), BF16),
                        pltpu.VMEM((PEER_HEADS, PEER_NKEYS, tp), BF16), pltpu.VMEM((PEER_HEADS, PEER_NKEYS, tp), BF16),
                        pltpu.VMEM((nb, PEER_HEADS, 16, tp), BF16), pltpu.VMEM((nb, PEER_HEADS, 16, tp), BF16)],
        compiler_params=_params(("arbitrary", "arbitrary", "arbitrary")),
        name="peer_dense",
    )(xm, u_all, vt_all, e1, cut, e2, r2, x3, mod3, ln_g.reshape(1, d), ln_b.reshape(1, d))


def peer_layer(x3, mod3, l, wq, subkeys, u_all, vt_all, ln_g, ln_b):
    xm, e1, cut, e2, r2 = peer_route(x3, mod3, 3, wq.astype(BF16), subkeys)
    return peer_dense(xm, u_all, vt_all, l, e1, cut, e2, r2, x3, mod3, 5, ln_g, ln_b)


def _pad_cols(w, n):
    return jnp.pad(w, ((0, 0), (0, n - w.shape[1])))


def _stream(prompt_part, sample_part):
    return jnp.concatenate([prompt_part.reshape(1, -1, prompt_part.shape[-1]), sample_part], axis=0)


def _head_major(a, heads):
    b, t, _ = a.shape
    return jnp.transpose(a.reshape(b, t, heads, -1), (0, 2, 1, 3))


def _token_major(a):
    b, h, t, dh = a.shape
    return jnp.transpose(a, (0, 2, 1, 3)).reshape(b, t, h * dh)


MLSTM_CHUNK = 128
GLA_CHUNK = 32
NPROJ = 3200


def mlstm_layer(x3, mod3, bp, lp, st_c, st_n, st_m, w_in, b_gate, norm_w, w_out, ln_g, ln_b):
    nseg, seg, _ = x3.shape
    bs = nseg - 1
    p = mod_matmul(x3, mod3, 0, _pad_cols(w_in, NPROJ).astype(BF16))
    graw = p[:, :, 3072:3088]
    gp = graw[0].reshape(bp, lp, 16)
    zc = jnp.zeros((bp, 8, M_DK, M_DV), F32)
    zn = jnp.zeros((bp, 8, M_DK), F32)
    hfp, hbp, c_new, n_new, m_new = mlstm_scan(p.reshape(nseg * bp, lp, NPROJ), 0, bp, lp, gp,
                                               jnp.swapaxes(gp, 1, 2), b_gate, zc, zn, zn, min(MLSTM_CHUNK, lp))
    gs = graw[1:]
    hfs, hbs, _, _, _ = mlstm_scan(p, 1, bs, seg, gs, jnp.swapaxes(gs, 1, 2), b_gate,
                                   st_c.reshape(bs, 8, M_DK, M_DV), st_n.reshape(bs, 8, M_DK),
                                   jnp.broadcast_to(st_m.reshape(bs, 8, 1), (bs, 8, M_DK)), MLSTM_CHUNK)
    x3 = outproj_ln("mlstm", (_stream(hfp, hfs), _stream(hbp, hbs)), x3, mod3, 2, w_out.astype(BF16), ln_g, ln_b,
                    norm_w=norm_w, og=p, og_col=2)
    return (x3, c_new.reshape(bp, 2, M_HEADS, M_DK, M_DV), n_new.reshape(bp, 2, M_HEADS, M_DK),
            m_new[:, :, 0].reshape(bp, 2, M_HEADS))


def gla_layer(x3, mod3, bp, lp, st_s, w_in, w_gate2, b_gate2, norm_w, w_out, ln_g, ln_b):
    nseg, seg, _ = x3.shape
    bs = nseg - 1
    p = mod_matmul(x3, mod3, 0, _pad_cols(w_in, NPROJ).astype(BF16))
    gr = p[:, :, 3072:3104]
    zs = jnp.zeros((bp, 8, G_DV, G_DK), F32)
    ofp, obp, s_new = gla_scan(p.reshape(nseg * bp, lp, NPROJ), 0, bp, lp, gr[0].reshape(bp, lp, 32),
                               w_gate2, b_gate2, zs, GLA_CHUNK)
    s0t = jnp.swapaxes(st_s.reshape(bs, 8, G_DK, G_DV), -1, -2)
    ofs, obs, _ = gla_scan(p, 1, bs, seg, gr[1:], w_gate2, b_gate2, s0t, GLA_CHUNK)
    x3 = outproj_ln("gla", (_stream(ofp, ofs), _stream(obp, obs)), x3, mod3, 2, w_out.astype(BF16), ln_g, ln_b,
                    norm_w=jnp.tile(norm_w, G_HEADS), og=p, og_col=2)
    return x3, jnp.swapaxes(s_new, -1, -2).reshape(bp, 2, G_HEADS, G_DK, G_DV)


def na_layer(x3, mod3, bp, lp, cache_k, cache_v, w_in, rpb, w_out, ln_g, ln_b):
    nseg, seg, _ = x3.shape
    bs = nseg - 1
    hd = NA_HEADS * NA_HD
    p = mod_matmul(x3, mod3, 0, w_in.astype(BF16))
    pp = p[0].reshape(bp, lp, 3 * hd)
    hm = lambda a: _head_major(a, NA_HEADS).astype(BF16)
    yp = attention(hm(pp[..., :hd]), hm(pp[..., hd:2 * hd]), hm(pp[..., 2 * hd:]), lp)
    ps = p[1:]
    ys = na_attention(hm(ps[..., :hd]), hm(ps[..., hd:2 * hd]), hm(ps[..., 2 * hd:]),
                      hm(cache_k.reshape(bs, -1, hd)), hm(cache_v.reshape(bs, -1, hd)), na_bias_table(rpb))
    x3 = outproj_ln("plain", _stream(_token_major(yp), _token_major(ys)), x3, mod3, 2, w_out.astype(BF16), ln_g, ln_b)
    return (x3, pp[..., hd:2 * hd].reshape(bp, lp, NA_HEADS, NA_HD), pp[..., 2 * hd:].reshape(bp, lp, NA_HEADS, NA_HD))


def _rope_rotated_cols(w):
    q = MLA_ROPE // 4
    return jnp.concatenate([-w[..., q:2 * q], w[..., :q], -w[..., 3 * q:], w[..., 2 * q:3 * q]], axis=-1)


def _rope_tables(ts):
    ra = MLA_ROPE // 2
    t = np.arange(ts)
    inv = 1.0 / (ROPE_BASE ** (np.arange(0, ra, 2, dtype=np.float32) / ra))
    ang_r = (t // GRID_W).astype(np.float32)[:, None] * inv[None, :]
    ang_c = (t % GRID_W).astype(np.float32)[:, None] * inv[None, :]
    ang = np.concatenate([ang_r, ang_r, ang_c, ang_c], axis=-1).astype(np.float32)
    return jnp.cos(jnp.asarray(ang)), jnp.sin(jnp.asarray(ang))


def mla_layer(x3, mod3, bp, lp, cache_ckv, cache_kpe, w_in, q_norm, w_qup, kv_norm, w_kvup, w_out, ln_g, ln_b):
    nseg, seg, _ = x3.shape
    bs = nseg - 1
    nq = MLA_Q_LORA + MLA_KV_LORA
    w_ext = jnp.concatenate([w_in, _rope_rotated_cols(w_in[:, nq:])], axis=1)
    p = mod_matmul(x3, mod3, 0, _pad_cols(w_ext, 896).astype(BF16))
    cos_t, sin_t = _rope_tables(seg)
    cos3 = jnp.concatenate([jnp.ones((1, seg, MLA_ROPE), F32), jnp.broadcast_to(cos_t, (bs, seg, MLA_ROPE))], 0)
    sin3 = jnp.concatenate([jnp.zeros((1, seg, MLA_ROPE), F32), jnp.broadcast_to(sin_t, (bs, seg, MLA_ROPE))], 0)
    wq = w_qup.reshape(MLA_Q_LORA, MLA_HEADS, MLA_NOPE + MLA_ROPE)
    wq_rope = wq[:, :, MLA_NOPE:]
    w_q3 = jnp.concatenate([wq[:, :, :MLA_NOPE].reshape(MLA_Q_LORA, -1), wq_rope.reshape(MLA_Q_LORA, -1),
                            _rope_rotated_cols(wq_rope).reshape(MLA_Q_LORA, -1)], axis=1).astype(BF16)
    q_all = mla_q(p, q_norm, w_q3, jnp.tile(cos3, (1, 1, MLA_HEADS)), jnp.tile(sin3, (1, 1, MLA_HEADS)))
    wkv = w_kvup.reshape(MLA_KV_LORA, MLA_HEADS, MLA_NOPE + MLA_VD)
    w_kv2 = jnp.concatenate([wkv[:, :, :MLA_NOPE].reshape(MLA_KV_LORA, -1),
                             wkv[:, :, MLA_NOPE:].reshape(MLA_KV_LORA, -1)], axis=1).astype(BF16)
    ckvn, kpe, kv = mla_kv(p, kv_norm, w_kv2, cos3, sin3)
    kvc = matmul(cache_ckv.reshape(-1, MLA_KV_LORA), w_kv2, 512).reshape(bs, -1, w_kv2.shape[1])
    nn = MLA_HEADS * MLA_NOPE

    def heads(q_rows, kv_rows, kpe_rows):
        b, t, _ = q_rows.shape
        tk = kv_rows.shape[1]
        qh = jnp.concatenate([q_rows[..., :nn].reshape(b, t, MLA_HEADS, MLA_NOPE),
                              q_rows[..., nn:].reshape(b, t, MLA_HEADS, MLA_ROPE)], -1)
        kh = jnp.concatenate([kv_rows[..., :nn].reshape(b, tk, MLA_HEADS, MLA_NOPE),
                              jnp.broadcast_to(kpe_rows[:, :, None, :], (b, tk, MLA_HEADS, MLA_ROPE))], -1)
        vh = kv_rows[..., nn:].reshape(b, tk, MLA_HEADS, MLA_VD)
        tr = lambda a: jnp.transpose(a, (0, 2, 1, 3)).astype(BF16)
        return tr(qh), tr(kh), tr(vh)

    yp = attention(*heads(q_all[0].reshape(bp, lp, -1), kv[0].reshape(bp, lp, -1), kpe[0].reshape(bp, lp, -1)), lp)
    ys = attention(*heads(q_all[1:], jnp.concatenate([kv[1:], kvc], 1), jnp.concatenate([kpe[1:], cache_kpe], 1)), 256)
    x3 = outproj_ln("plain", _stream(_token_major(yp), _token_major(ys)), x3, mod3, 2, w_out.astype(BF16), ln_g, ln_b)
    return x3, ckvn[0].reshape(bp, lp, MLA_KV_LORA), kpe[0].reshape(bp, lp, MLA_ROPE)


def kernel(x_prompt, x_sample, c, c_ctx, state_mlstm_C, state_mlstm_n, state_mlstm_m, state_gla_S, cache_na_k, cache_na_v, cache_mla_ckv, cache_mla_kpe, ada_w, ada_b, ln_mix_g, ln_mix_b, ln_ffn_g, ln_ffn_b, mlstm_w_in, mlstm_b_gate, mlstm_norm_w, mlstm_w_out, gla_w_in, gla_w_gate2, gla_b_gate2, gla_norm_w, gla_w_out, na_w_in, na_rpb, na_w_out, mla_w_in, mla_q_norm, mla_w_qup, mla_kv_norm, mla_w_kvup, mla_w_out, peer_w_q, peer_subkeys, peer_u, peer_v):
    bp, lp, d = x_prompt.shape
    bs, ts, _ = x_sample.shape
    assert bp * lp == ts and bs + 1 <= 8
    x3 = _stream(x_prompt, x_sample)
    cond8 = jnp.zeros((8, d), F32).at[0].set(c_ctx).at[1:1 + bs].set(c)
    mods = adaln_all(cond8, ada_w, ada_b)
    u_all = peer_u.astype(BF16)
    vt_all = jnp.swapaxes(peer_v, 1, 2).astype(BF16)
    outs = {}
    for l in range(DEPTH):
        mod3 = mods[l].reshape(8, 1, ADA_CHUNKS * d)
        kind = l % 4
        if kind == 0:
            x3, outs["C"], outs["n"], outs["m"] = mlstm_layer(
                x3, mod3, bp, lp, state_mlstm_C, state_mlstm_n, state_mlstm_m, mlstm_w_in, mlstm_b_gate,
                mlstm_norm_w, mlstm_w_out, ln_mix_g[l], ln_mix_b[l])
        elif kind == 1:
            x3, outs["S"] = gla_layer(x3, mod3, bp, lp, state_gla_S, gla_w_in, gla_w_gate2, gla_b_gate2,
                                      gla_norm_w, gla_w_out, ln_mix_g[l], ln_mix_b[l])
        elif kind == 2:
            x3, outs["nk"], outs["nv"] = na_layer(x3, mod3, bp, lp, cache_na_k, cache_na_v, na_w_in, na_rpb,
                                                  na_w_out, ln_mix_g[l], ln_mix_b[l])
        else:
            x3, outs["ckv"], outs["kpe"] = mla_layer(x3, mod3, bp, lp, cache_mla_ckv, cache_mla_kpe, mla_w_in,
                                                     mla_q_norm, mla_w_qup, mla_kv_norm, mla_w_kvup, mla_w_out,
                                                     ln_mix_g[l], ln_mix_b[l])
        x3 = peer_layer(x3, mod3, l, peer_w_q[l], peer_subkeys[l], u_all, vt_all, ln_ffn_g[l], ln_ffn_b[l])
    return (x3[0].reshape(bp, lp, d), x3[1:], outs["C"], outs["n"], outs["m"], outs["S"], outs["nk"], outs["nv"],
            outs["ckv"], outs["kpe"])
```

```python
import functools

import numpy as np
import jax
import jax.numpy as jnp
from jax import lax
from jax.experimental import pallas as pl
from jax.experimental.pallas import tpu as pltpu

D_MODEL = 1024
DEPTH = 4
GRID_W = 64
DEEPNORM_ALPHA = (2.0 * DEPTH) ** 0.25
ADA_CHUNKS = 6
NORM_EPS = 1e-5
SEG = 4096
NSEG = 3

M_HEADS, M_DK, M_DV = 4, 128, 256
G_HEADS, G_DK, G_DV = 4, 128, 256
G_GATE_RANK = 16
G_GATE_NORM = 16.0
NA_HEADS, NA_HD, NA_ROWS, NA_COLS = 16, 64, 8, 16
MLA_HEADS, MLA_Q_LORA, MLA_KV_LORA, MLA_NOPE, MLA_ROPE, MLA_VD = 16, 512, 256, 64, 32, 64
ROPE_BASE = 10000.0
PEER_HEADS, PEER_NKEYS, PEER_HALF, PEER_TOPK = 8, 128, 128, 16

V7X_VMEM_LIMIT = 56 * 1024 * 1024
F32 = jnp.float32
BF16 = jnp.bfloat16
NEG_INF = float("-inf")


def _params(sem, vmem=V7X_VMEM_LIMIT):
    return pltpu.CompilerParams(dimension_semantics=sem, vmem_limit_bytes=vmem)


def _dot(a, b, dims=((1,), (0,))):
    return lax.dot_general(a, b, (dims, ((), ())), preferred_element_type=F32)


def _split3(a):
    hi = a.astype(BF16)
    r1 = a - hi.astype(F32)
    mid = r1.astype(BF16)
    lo = (r1 - mid.astype(F32)).astype(BF16)
    return hi, mid, lo


def _dot_exact_lhs(m01, a):
    hi, mid, lo = _split3(a)
    return _dot(m01, hi) + _dot(m01, mid) + _dot(m01, lo)


def _dot_exact_rhs(a, m01):
    hi, mid, lo = _split3(a)
    return _dot(hi, m01) + _dot(mid, m01) + _dot(lo, m01)


def _log_sigmoid(x):
    return jnp.minimum(x, 0.0) - jnp.log(1.0 + jnp.exp(-jnp.abs(x)))


def _sigmoid(x):
    return 1.0 / (1.0 + jnp.exp(-x))


def _adaln_kernel(c_ref, w_ref, b_ref, o_ref):
    cv = c_ref[...]
    a = cv * _sigmoid(cv)
    o_ref[0] = lax.dot_general(a, w_ref[0], (((1,), (0,)), ((), ())), precision=lax.Precision.HIGHEST,
                               preferred_element_type=F32) + b_ref[0]


def adaln_all(cond8, ada_w, ada_b):
    tn = 1024
    n = ada_w.shape[-1]
    return pl.pallas_call(
        _adaln_kernel,
        grid=(DEPTH, n // tn),
        in_specs=[pl.BlockSpec((8, D_MODEL), lambda l, j: (0, 0)),
                  pl.BlockSpec((1, D_MODEL, tn), lambda l, j: (l, 0, j)),
                  pl.BlockSpec((1, 1, tn), lambda l, j: (l, 0, j))],
        out_specs=pl.BlockSpec((1, 8, tn), lambda l, j: (l, 0, j)),
        out_shape=jax.ShapeDtypeStruct((DEPTH, 8, n), F32),
        compiler_params=_params(("arbitrary", "arbitrary")),
        name="adaln",
    )(cond8, ada_w, ada_b.reshape(DEPTH, 1, n))


def _modmm_kernel(x_ref, sh_ref, sc_ref, w_ref, o_ref, xm_ref):
    @pl.when(pl.program_id(2) == 0)
    def _():
        xm_ref[...] = (x_ref[0] * (1.0 + sc_ref[0]) + sh_ref[0]).astype(BF16)

    o_ref[0] = _dot(xm_ref[...], w_ref[...]).astype(o_ref.dtype)


def mod_matmul(x3, mod3, shift_chunk, w_bf16, tm=512, tn=None, out_dtype=F32):
    nseg, seg, d = x3.shape
    n = w_bf16.shape[1]
    tn = n if tn is None else tn
    return pl.pallas_call(
        _modmm_kernel,
        grid=(nseg, seg // tm, n // tn),
        in_specs=[pl.BlockSpec((1, tm, d), lambda s, i, j: (s, i, 0)),
                  pl.BlockSpec((1, 1, d), lambda s, i, j: (s, 0, shift_chunk)),
                  pl.BlockSpec((1, 1, d), lambda s, i, j: (s, 0, shift_chunk + 1)),
                  pl.BlockSpec((d, tn), lambda s, i, j: (0, j))],
        out_specs=pl.BlockSpec((1, tm, tn), lambda s, i, j: (s, i, j)),
        out_shape=jax.ShapeDtypeStruct((nseg, seg, n), out_dtype),
        scratch_shapes=[pltpu.VMEM((tm, d), BF16)],
        compiler_params=_params(("arbitrary", "arbitrary", "arbitrary")),
        name="mod_matmul",
    )(x3, mod3, mod3, w_bf16)


def _layer_norm_rows(y, g, b):
    mu = jnp.mean(y, axis=-1, keepdims=True)
    yc = y - mu
    var = jnp.mean(yc * yc, axis=-1, keepdims=True)
    return yc * lax.rsqrt(var + NORM_EPS) * g + b


def _outproj_kernel(*refs, mode):
    if mode == "plain":
        y_ref, x_ref, gate_ref, w_ref, g_ref, b_ref, o_ref = refs
        yin = y_ref[0].astype(BF16)
    else:
        ya_ref, yb_ref, og_ref, nw_ref, x_ref, gate_ref, w_ref, g_ref, b_ref, o_ref = refs
        hs = ya_ref[0] + yb_ref[0]
        og = og_ref[0]
        parts = []
        for h in range(4):
            seg = hs[:, h * 256:(h + 1) * 256]
            nw = nw_ref[:, h * 256:(h + 1) * 256]
            if mode == "mlstm":
                mu = jnp.mean(seg, axis=-1, keepdims=True)
                sc = seg - mu
                var = jnp.mean(sc * sc, axis=-1, keepdims=True)
                parts.append(sc * lax.rsqrt(var + NORM_EPS) * nw)
            else:
                ms = jnp.mean(seg * seg, axis=-1, keepdims=True)
                parts.append(seg * lax.rsqrt(ms + NORM_EPS) * nw)
        hn = jnp.concatenate(parts, axis=-1)
        act = _sigmoid(og) if mode == "mlstm" else og * _sigmoid(og)
        yin = (act * hn).astype(BF16)
    y = _dot(yin, w_ref[...])
    z = DEEPNORM_ALPHA * x_ref[0] + gate_ref[0] * y
    o_ref[0] = _layer_norm_rows(z, g_ref[...], b_ref[...])


def outproj_ln(mode, ys, x3, mod3, gate_chunk, w_bf16, ln_g, ln_b, norm_w=None, og=None, og_col=0, tm=512):
    nseg, seg, d = x3.shape
    k = w_bf16.shape[0]
    tok = lambda s, i: (s, i, 0)
    if mode == "plain":
        args = [ys]
        specs = [pl.BlockSpec((1, tm, k), tok)]
    else:
        args = [ys[0], ys[1], og, norm_w.reshape(1, k)]
        specs = [pl.BlockSpec((1, tm, k), tok), pl.BlockSpec((1, tm, k), tok),
                 pl.BlockSpec((1, tm, k), lambda s, i: (s, i, og_col)),
                 pl.BlockSpec((1, k), lambda s, i: (0, 0))]
    args += [x3, mod3, w_bf16, ln_g.reshape(1, d), ln_b.reshape(1, d)]
    specs += [pl.BlockSpec((1, tm, d), tok),
              pl.BlockSpec((1, 1, d), lambda s, i: (s, 0, gate_chunk)),
              pl.BlockSpec((k, d), lambda s, i: (0, 0)),
              pl.BlockSpec((1, d), lambda s, i: (0, 0)),
              pl.BlockSpec((1, d), lambda s, i: (0, 0))]
    return pl.pallas_call(
        functools.partial(_outproj_kernel, mode=mode),
        grid=(nseg, seg // tm),
        in_specs=specs,
        out_specs=pl.BlockSpec((1, tm, d), tok),
        out_shape=jax.ShapeDtypeStruct((nseg, seg, d), F32),
        compiler_params=_params(("arbitrary", "arbitrary")),
        name="outproj_ln_" + mode,
    )(*args)


def _tri(n, lower):
    r = lax.broadcasted_iota(jnp.int32, (n, n), 0)
    c = lax.broadcasted_iota(jnp.int32, (n, n), 1)
    return (c <= r) if lower else (c >= r)


def _mlstm_kernel(pf_ref, pb_ref, gf_ref, gb_ref, gtf_ref, gtb_ref, bias_ref, biast_ref,
                  c0_ref, n0_ref, m0_ref, hf_ref, hb_ref, co_ref, no_ref, mo_ref,
                  c_s, n_s, m_s, *, L):
    c = pl.program_id(1)

    @pl.when(c == 0)
    def _():
        c_s[...] = c0_ref[0]
        n_s[...] = n0_ref[0]
        m_s[...] = m0_ref[0]

    for d in range(2):
        p_ref, g_ref, gt_ref, h_ref = ((pf_ref, gf_ref, gtf_ref, hf_ref) if d == 0
                                       else (pb_ref, gb_ref, gtb_ref, hb_ref))
        mask = _tri(L, lower=(d == 0))
        mcol = mask.astype(BF16)
        mrow = _tri(L, lower=(d != 0)).astype(BF16)
        g = g_ref[0] + bias_ref[...]
        gt = gt_ref[0] + biast_ref[...]
        li_c = g[:, d * 8:d * 8 + 4]
        lf_c = _log_sigmoid(g[:, d * 8 + 4:d * 8 + 8])
        li_r = gt[d * 8:d * 8 + 4, :]
        lf_r = _log_sigmoid(gt[d * 8 + 4:d * 8 + 8, :])
        b_c = _dot_exact_lhs(mcol, lf_c)
        b_r = _dot_exact_rhs(lf_r, mrow)
        last = L - 1 if d == 0 else 0
        for h in range(M_HEADS):
            u = d * M_HEADS + h
            q = p_ref[0, :, h * M_DK:(h + 1) * M_DK]
            k = p_ref[0, :, 512 + h * M_DK:512 + (h + 1) * M_DK] * (M_DK ** -0.5)
            v = p_ref[0, :, 1024 + h * M_DV:1024 + (h + 1) * M_DV].astype(BF16)
            qb = q.astype(BF16)
            bc, br = b_c[:, h:h + 1], b_r[h:h + 1, :]
            lic, lir = li_c[:, h:h + 1], li_r[h:h + 1, :]
            m_prev = m_s[u:u + 1, 0:1]
            dmat = jnp.where(mask, bc - br + lir, NEG_INF)
            inter = bc + m_prev
            mt = jnp.maximum(inter, jnp.max(dmat, axis=-1, keepdims=True))
            smat = _dot(qb, k.astype(BF16), ((1,), (1,))) * jnp.exp(dmat - mt)
            ei = jnp.exp(inter - mt)
            cmat = c_s[u]
            num = _dot(smat.astype(BF16), v) + ei * _dot(qb, cmat.astype(BF16))
            nrow = n_s[u:u + 1, :]
            den = jnp.sum(smat, axis=-1, keepdims=True) + ei * jnp.sum(q * nrow, axis=-1, keepdims=True)
            h_ref[0, :, h * M_DV:(h + 1) * M_DV] = num / jnp.maximum(jnp.abs(den), jnp.exp(-mt))
            tot = br[:, last:last + 1]
            g_c = tot - bc + lic
            g_r = tot - br + lir
            m_new = jnp.maximum(tot + m_prev, jnp.max(g_r, axis=-1, keepdims=True))
            kw = k * jnp.exp(g_c - m_new)
            dec = jnp.exp(tot + m_prev - m_new)
            c_s[u] = dec * cmat + _dot(kw.astype(BF16), v, ((0,), (0,)))
            n_s[u:u + 1, :] = dec * nrow + jnp.sum(kw, axis=0, keepdims=True)
            m_s[u:u + 1, :] = jnp.broadcast_to(m_new, (1, 128))

    @pl.when(c == pl.num_programs(1) - 1)
    def _():
        co_ref[0] = c_s[...]
        no_ref[0] = n_s[...]
        mo_ref[0] = m_s[...]


def mlstm_scan(p, b0, nb, t, g, gt, bias, c0, n0, m0, L):
    nc = t // L
    hshape = jax.ShapeDtypeStruct((nb, t, M_HEADS * M_DV), F32)
    fwd = lambda b, c: (b + b0, c, 0)
    bwd = lambda b, c: (b + b0, nc - 1 - c, 0)
    st4 = lambda b, c: (b, 0, 0, 0)
    st3 = lambda b, c: (b, 0, 0)
    return pl.pallas_call(
        functools.partial(_mlstm_kernel, L=L),
        grid=(nb, nc),
        in_specs=[pl.BlockSpec((1, L, 2048), fwd), pl.BlockSpec((1, L, 2048), bwd),
                  pl.BlockSpec((1, L, 16), lambda b, c: (b, c, 0)),
                  pl.BlockSpec((1, L, 16), lambda b, c: (b, nc - 1 - c, 0)),
                  pl.BlockSpec((1, 16, L), lambda b, c: (b, 0, c)),
                  pl.BlockSpec((1, 16, L), lambda b, c: (b, 0, nc - 1 - c)),
                  pl.BlockSpec((1, 16), lambda b, c: (0, 0)),
                  pl.BlockSpec((16, 1), lambda b, c: (0, 0)),
                  pl.BlockSpec((1, 8, M_DK, M_DV), st4),
                  pl.BlockSpec((1, 8, M_DK), st3),
                  pl.BlockSpec((1, 8, M_DK), st3)],
        out_specs=[pl.BlockSpec((1, L, 1024), lambda b, c: (b, c, 0)),
                   pl.BlockSpec((1, L, 1024), lambda b, c: (b, nc - 1 - c, 0)),
                   pl.BlockSpec((1, 8, M_DK, M_DV), st4),
                   pl.BlockSpec((1, 8, M_DK), st3),
                   pl.BlockSpec((1, 8, M_DK), st3)],
        out_shape=[hshape, hshape,
                   jax.ShapeDtypeStruct((nb, 8, M_DK, M_DV), F32),
                   jax.ShapeDtypeStruct((nb, 8, M_DK), F32),
                   jax.ShapeDtypeStruct((nb, 8, M_DK), F32)],
        scratch_shapes=[pltpu.VMEM((8, M_DK, M_DV), F32), pltpu.VMEM((8, M_DK), F32),
                        pltpu.VMEM((8, M_DK), F32)],
        compiler_params=_params(("arbitrary", "arbitrary")),
        name="mlstm_scan",
    )(p, p, g, g, gt, gt, bias.reshape(1, 16), bias.reshape(16, 1), c0, n0, m0)


def _gla_kernel(pf_ref, pb_ref, gf_ref, gb_ref, w2_ref, b2_ref, s0_ref, of_ref, ob_ref, so_ref, s_s, *, L):
    c = pl.program_id(1)

    @pl.when(c == 0)
    def _():
        s_s[...] = s0_ref[0]

    for d in range(2):
        p_ref, g_ref, o_ref = (pf_ref, gf_ref, of_ref) if d == 0 else (pb_ref, gb_ref, ob_ref)
        mask = _tri(L, lower=(d == 0))
        mcol = mask.astype(BF16)
        gr = g_ref[0][:, d * G_GATE_RANK:(d + 1) * G_GATE_RANK]
        pre = lax.dot_general(gr, w2_ref[d], (((1,), (0,)), ((), ())), precision=lax.Precision.HIGHEST,
                              preferred_element_type=F32) + b2_ref[d]
        la = _log_sigmoid(pre) * (1.0 / G_GATE_NORM)
        bc_all = _dot_exact_lhs(mcol, la)
        last = L - 1 if d == 0 else 0
        for h in range(G_HEADS):
            u = d * G_HEADS + h
            q = p_ref[0, :, h * G_DK:(h + 1) * G_DK] * (G_DK ** -0.5)
            k = p_ref[0, :, 512 + h * G_DK:512 + (h + 1) * G_DK]
            v = p_ref[0, :, 1024 + h * G_DV:1024 + (h + 1) * G_DV].astype(BF16)
            bc = bc_all[:, h * G_DK:(h + 1) * G_DK]
            qd = (q * jnp.exp(bc)).astype(BF16)
            kd = (k * jnp.exp(-bc)).astype(BF16)
            a = jnp.where(mask, _dot(qd, kd, ((1,), (1,))), 0.0)
            st = s_s[u]
            o_ref[0, :, h * G_DV:(h + 1) * G_DV] = (_dot(a.astype(BF16), v)
                                                    + _dot(qd, st.astype(BF16), ((1,), (1,))))
            bl = bc[last:last + 1, :]
            kl = (k * jnp.exp(bl - bc)).astype(BF16)
            s_s[u] = st * jnp.exp(bl) + _dot(v, kl, ((0,), (0,)))

    @pl.when(c == pl.num_programs(1) - 1)
    def _():
        so_ref[0] = s_s[...]


def gla_scan(p, b0, nb, t, gr, w2, b2, s0t, L):
    nc = t // L
    oshape = jax.ShapeDtypeStruct((nb, t, G_HEADS * G_DV), F32)
    st4 = lambda b, c: (b, 0, 0, 0)
    return pl.pallas_call(
        functools.partial(_gla_kernel, L=L),
        grid=(nb, nc),
        in_specs=[pl.BlockSpec((1, L, 2048), lambda b, c: (b + b0, c, 0)),
                  pl.BlockSpec((1, L, 2048), lambda b, c: (b + b0, nc - 1 - c, 0)),
                  pl.BlockSpec((1, L, 32), lambda b, c: (b, c, 0)),
                  pl.BlockSpec((1, L, 32), lambda b, c: (b, nc - 1 - c, 0)),
                  pl.BlockSpec((2, G_GATE_RANK, 512), lambda b, c: (0, 0, 0)),
                  pl.BlockSpec((2, 1, 512), lambda b, c: (0, 0, 0)),
                  pl.BlockSpec((1, 8, G_DV, G_DK), st4)],
        out_specs=[pl.BlockSpec((1, L, 1024), lambda b, c: (b, c, 0)),
                   pl.BlockSpec((1, L, 1024), lambda b, c: (b, nc - 1 - c, 0)),
                   pl.BlockSpec((1, 8, G_DV, G_DK), st4)],
        out_shape=[oshape, oshape, jax.ShapeDtypeStruct((nb, 8, G_DV, G_DK), F32)],
        scratch_shapes=[pltpu.VMEM((8, G_DV, G_DK), F32)],
        compiler_params=_params(("arbitrary", "arbitrary")),
        name="gla_scan",
    )(p, p, gr, gr, w2, b2.reshape(2, 1, 512), s0t)


def _attn_kernel(q_ref, k_ref, v_ref, o_ref, *, scale):
    s = _dot(q_ref[0, 0], k_ref[0, 0], ((1,), (1,))) * scale
    m = jnp.max(s, axis=-1, keepdims=True)
    p = jnp.exp(s - m)
    l = jnp.sum(p, axis=-1, keepdims=True)
    o_ref[0, 0] = _dot(p.astype(BF16), v_ref[0, 0]) / l


def attention(q, k, v, tq):
    b, h, lq, dq = q.shape
    lk, dv = k.shape[2], v.shape[3]
    return pl.pallas_call(
        functools.partial(_attn_kernel, scale=dq ** -0.5),
        grid=(b, h, lq // tq),
        in_specs=[pl.BlockSpec((1, 1, tq, dq), lambda b, h, i: (b, h, i, 0)),
                  pl.BlockSpec((1, 1, lk, dq), lambda b, h, i: (b, h, 0, 0)),
                  pl.BlockSpec((1, 1, lk, dv), lambda b, h, i: (b, h, 0, 0))],
        out_specs=pl.BlockSpec((1, 1, tq, dv), lambda b, h, i: (b, h, i, 0)),
        out_shape=jax.ShapeDtypeStruct((b, h, lq, dv), F32),
        compiler_params=_params(("arbitrary", "arbitrary", "arbitrary")),
        name="attention",
    )(q, k, v)


NA_RB = 8


def _na_kernel(q_ref, k_ref, v_ref, kc_ref, vc_ref, bias_ref, o_ref, *, rows):
    j = pl.program_id(2)
    scale = NA_HD ** -0.5
    kc, vc = kc_ref[0, 0], vc_ref[0, 0]
    for a in range(NA_RB):
        r = j * NA_RB + a
        start = jnp.clip(r - NA_ROWS // 2, 0, rows - NA_ROWS)
        dr0 = start - r + (NA_ROWS - 1)
        off = pl.multiple_of(start * GRID_W, GRID_W)
        qa = q_ref[0, 0, a * GRID_W:(a + 1) * GRID_W, :]
        kl = k_ref[0, 0, pl.ds(off, NA_ROWS * GRID_W), :]
        vl = v_ref[0, 0, pl.ds(off, NA_ROWS * GRID_W), :]
        s_loc = _dot(qa, kl, ((1,), (1,))) * scale + bias_ref[0, dr0]
        s_ctx = _dot(qa, kc, ((1,), (1,))) * scale
        m = jnp.maximum(jnp.max(s_loc, axis=-1, keepdims=True), jnp.max(s_ctx, axis=-1, keepdims=True))
        p_loc = jnp.exp(s_loc - m)
        p_ctx = jnp.exp(s_ctx - m)
        l = jnp.sum(p_loc, axis=-1, keepdims=True) + jnp.sum(p_ctx, axis=-1, keepdims=True)
        o = _dot(p_loc.astype(BF16), vl) + _dot(p_ctx.astype(BF16), vc)
        o_ref[0, 0, a * GRID_W:(a + 1) * GRID_W, :] = o / l


def na_bias_table(rpb):
    cq = np.arange(GRID_W)[:, None]
    ck = np.arange(GRID_W)[None, :]
    cs = np.clip(cq - NA_COLS // 2, 0, GRID_W - NA_COLS)
    ok = (ck >= cs) & (ck < cs + NA_COLS)
    dc = np.clip(ck - cq, -(NA_COLS - 1), NA_COLS - 1) + (NA_COLS - 1)
    t = jnp.where(ok[None, None], rpb.astype(F32)[:, :, dc], NEG_INF)
    rows = np.arange(NA_ROWS)[:, None] + np.arange(NA_ROWS)[None, :]
    tf = t[:, rows]
    return jnp.transpose(tf, (0, 1, 3, 2, 4)).reshape(NA_HEADS, NA_ROWS, GRID_W, NA_ROWS * GRID_W)


def na_attention(q, k, v, kc, vc, bias):
    b, h, t, dh = q.shape
    lc = kc.shape[2]
    rows = t // GRID_W
    full = lambda b, h, j: (b, h, 0, 0)
    return pl.pallas_call(
        functools.partial(_na_kernel, rows=rows),
        grid=(b, h, rows // NA_RB),
        in_specs=[pl.BlockSpec((1, 1, NA_RB * GRID_W, dh), lambda b, h, j: (b, h, j, 0)),
                  pl.BlockSpec((1, 1, t, dh), full), pl.BlockSpec((1, 1, t, dh), full),
                  pl.BlockSpec((1, 1, lc, dh), full), pl.BlockSpec((1, 1, lc, dh), full),
                  pl.BlockSpec((1, NA_ROWS, GRID_W, NA_ROWS * GRID_W), lambda b, h, j: (h, 0, 0, 0))],
        out_specs=pl.BlockSpec((1, 1, NA_RB * GRID_W, dh), lambda b, h, j: (b, h, j, 0)),
        out_shape=jax.ShapeDtypeStruct((b, h, t, dh), F32),
        compiler_params=_params(("arbitrary", "arbitrary", "arbitrary")),
        name="na_attention",
    )(q, k, v, kc, vc, bias)


def _rms_rows(x, g):
    return x * lax.rsqrt(jnp.mean(x * x, axis=-1, keepdims=True) + NORM_EPS) * g


def _mla_q_kernel(cq_ref, g_ref, w_ref, cos_ref, sin_ref, o_ref):
    r = _dot(_rms_rows(cq_ref[0], g_ref[...]).astype(BF16), w_ref[...])
    nn = MLA_HEADS * MLA_NOPE
    nr = MLA_HEADS * MLA_ROPE
    o_ref[0, :, :nn] = r[:, :nn]
    o_ref[0, :, nn:] = r[:, nn:nn + nr] * cos_ref[0] + r[:, nn + nr:] * sin_ref[0]


def mla_q(p, q_norm, w_q3, cos_q, sin_q, tm=512):
    nseg, seg, _ = p.shape
    nout = MLA_HEADS * (MLA_NOPE + MLA_ROPE)
    nr = MLA_HEADS * MLA_ROPE
    tok = lambda s, i: (s, i, 0)
    return pl.pallas_call(
        _mla_q_kernel,
        grid=(nseg, seg // tm),
        in_specs=[pl.BlockSpec((1, tm, MLA_Q_LORA), tok),
                  pl.BlockSpec((1, MLA_Q_LORA), lambda s, i: (0, 0)),
                  pl.BlockSpec(w_q3.shape, lambda s, i: (0, 0)),
                  pl.BlockSpec((1, tm, nr), tok), pl.BlockSpec((1, tm, nr), tok)],
        out_specs=pl.BlockSpec((1, tm, nout), tok),
        out_shape=jax.ShapeDtypeStruct((nseg, seg, nout), F32),
        compiler_params=_params(("arbitrary", "arbitrary")),
        name="mla_q",
    )(p, q_norm.reshape(1, -1), w_q3, cos_q, sin_q)


def _mla_kv_kernel(ckv_ref, kpe_ref, g_ref, w_ref, cos_ref, sin_ref, ckvn_ref, kpeo_ref, kv_ref):
    cn = _rms_rows(ckv_ref[0], g_ref[...])
    ckvn_ref[0] = cn
    kv_ref[0] = _dot(cn.astype(BF16), w_ref[...])
    kp = kpe_ref[0]
    kpeo_ref[0] = kp[:, :MLA_ROPE] * cos_ref[0] + kp[:, MLA_ROPE:2 * MLA_ROPE] * sin_ref[0]


def mla_kv(p, kv_norm, w_kv, cos_k, sin_k, tm=512):
    nseg, seg, _ = p.shape
    nkv = w_kv.shape[1]
    tok = lambda s, i: (s, i, 0)
    return pl.pallas_call(
        _mla_kv_kernel,
        grid=(nseg, seg // tm),
        in_specs=[pl.BlockSpec((1, tm, MLA_KV_LORA), lambda s, i: (s, i, MLA_Q_LORA // MLA_KV_LORA)),
                  pl.BlockSpec((1, tm, 128), lambda s, i: (s, i, (MLA_Q_LORA + MLA_KV_LORA) // 128)),
                  pl.BlockSpec((1, MLA_KV_LORA), lambda s, i: (0, 0)),
                  pl.BlockSpec(w_kv.shape, lambda s, i: (0, 0)),
                  pl.BlockSpec((1, tm, MLA_ROPE), tok), pl.BlockSpec((1, tm, MLA_ROPE), tok)],
        out_specs=[pl.BlockSpec((1, tm, MLA_KV_LORA), tok), pl.BlockSpec((1, tm, MLA_ROPE), tok),
                   pl.BlockSpec((1, tm, nkv), tok)],
        out_shape=[jax.ShapeDtypeStruct((nseg, seg, MLA_KV_LORA), F32),
                   jax.ShapeDtypeStruct((nseg, seg, MLA_ROPE), F32),
                   jax.ShapeDtypeStruct((nseg, seg, nkv), F32)],
        compiler_params=_params(("arbitrary", "arbitrary")),
        name="mla_kv",
    )(p, p, kv_norm.reshape(1, -1), w_kv, cos_k, sin_k)


def _mm_kernel(a_ref, w_ref, o_ref):
    o_ref[...] = _dot(a_ref[...].astype(BF16), w_ref[...])


def matmul(a, w_bf16, tm):
    m, k = a.shape
    n = w_bf16.shape[1]
    return pl.pallas_call(
        _mm_kernel,
        grid=(m // tm,),
        in_specs=[pl.BlockSpec((tm, k), lambda i: (i, 0)), pl.BlockSpec((k, n), lambda i: (0, 0))],
        out_specs=pl.BlockSpec((tm, n), lambda i: (i, 0)),
        out_shape=jax.ShapeDtypeStruct((m, n), F32),
        compiler_params=_params(("arbitrary",)),
        name="matmul",
    )(a, w_bf16)


PEER_RT = 128
NOT_TOP = 99.0


def _top16(s, exact):
    key = lax.broadcasted_iota(jnp.int32, s.shape, 0).astype(F32)
    rank = jnp.full(s.shape, NOT_TOP, F32)
    vals = []
    for r in range(PEER_TOPK):
        m = jnp.max(s, axis=0, keepdims=True)
        hit = s == m
        if exact:
            hit = key == jnp.min(jnp.where(hit, key, 1e9), axis=0, keepdims=True)
        rank = jnp.where(hit, float(r), rank)
        s = jnp.where(hit, NEG_INF, s)
        vals.append(m)
    return vals, rank


def _pair_topk(av, bv, exact):
    n = av[0].shape[-1]
    a_lo, a_hi = jnp.concatenate(av[:8], 0), jnp.concatenate(av[8:], 0)
    b_lo, b_hi = jnp.concatenate(bv[:8], 0), jnp.concatenate(bv[8:], 0)
    row = lax.broadcasted_iota(jnp.int32, (8, n), 0).astype(F32)

    no_pos = 1e8

    def rows_b(a, b_blk, boff, nvalid):
        ok = row < nvalid
        return jnp.where(ok, av[a] + b_blk, NEG_INF), jnp.where(ok, a * 16.0 + boff + row, no_pos)

    def rows_a(b, a_blk, aoff, lo, hi):
        ok = (row >= lo) & (row < hi)
        return jnp.where(ok, a_blk + bv[b], NEG_INF), jnp.where(ok, (aoff + row) * 16.0 + b, no_pos)

    groups = [rows_b(0, b_lo, 0, 8), rows_b(0, b_hi, 8, 8), rows_b(1, b_lo, 0, 8), rows_b(2, b_lo, 0, 5),
              rows_b(3, b_lo, 0, 4), rows_a(0, a_lo, 0, 4, 8), rows_a(0, a_hi, 8, 0, 8),
              rows_a(1, a_lo, 0, 4, 8), rows_a(2, a_lo, 0, 4, 5)]
    cands = [g[0] for g in groups]
    poss = [g[1] for g in groups]
    sels = [jnp.zeros((8, n), F32) for _ in groups]
    top = av[0] + bv[0]
    z = jnp.zeros((1, n), F32)
    for _ in range(PEER_TOPK):
        m = functools.reduce(jnp.maximum, cands)
        m = jnp.max(m, axis=0, keepdims=True)
        hits = [c == m for c in cands]
        if exact:
            first = functools.reduce(jnp.minimum, [jnp.where(hh, p, 1e9) for hh, p in zip(hits, poss)])
            first = jnp.min(first, axis=0, keepdims=True)
            hits = [p == first for p in poss]
        cands = [jnp.where(hh, NEG_INF, c) for hh, c in zip(hits, cands)]
        sels = [jnp.where(hh, 1.0, s) for hh, s in zip(hits, sels)]
        z = z + jnp.exp(m - top)
    cnt = lambda x: jnp.sum(x, axis=0, keepdims=True)
    cut_lo = sels[5] + sels[7] + sels[8]
    for a, c in enumerate([cnt(sels[0]) + cnt(sels[1]), cnt(sels[2]), cnt(sels[3]), cnt(sels[4])]):
        cut_lo = cut_lo + jnp.where(row == a, c, 0.0)
    return cut_lo, sels[6], z, cnt(cut_lo) + cnt(sels[6])


def _peer_route_kernel(x_ref, sh_ref, sc_ref, wq_ref, sk_ref, xm_ref, e1_ref, cut_ref, e2_ref, r2_ref, q_s, *, tm):
    xm = (x_ref[0] * (1.0 + sc_ref[0]) + sh_ref[0]).astype(BF16)
    xm_ref[0] = xm
    q = _dot(xm, wq_ref[...])
    for hp in range(2 * PEER_HEADS):
        q_s[hp] = q[:, hp * PEER_HALF:(hp + 1) * PEER_HALF]

    def route(h, tok, exact):
        def scores(hp):
            return lax.dot_general(sk_ref[hp], q_s[hp, tok, :], (((1,), (1,)), ((), ())),
                                   precision=lax.Precision.HIGHEST, preferred_element_type=F32)

        s1, s2 = scores(2 * h), scores(2 * h + 1)
        av, rank1 = _top16(s1, exact)
        bv, rank2 = _top16(s2, exact)
        cut_lo, cut_hi, z, nsel = _pair_topk(av, bv, exact)
        cut = jnp.zeros_like(s1)
        for r in range(PEER_TOPK):
            src = cut_lo if r < 8 else cut_hi
            cut = jnp.where(rank1 == float(r), src[r % 8:r % 8 + 1, :], cut)
        e1_ref[0, h, :, tok] = (jnp.exp(s1 - av[0]) / z).astype(BF16)
        cut_ref[0, h, :, tok] = cut.astype(BF16)
        e2_ref[0, h, :, tok] = jnp.exp(s2 - bv[0]).astype(BF16)
        r2_ref[0, h, :, tok] = rank2.astype(BF16)
        ranked = lambda rk: jnp.sum(jnp.where(rk < PEER_TOPK, 1.0, 0.0), axis=0, keepdims=True)
        return ranked(rank1), ranked(rank2), nsel

    def body(h, carry):
        toks = [pl.ds(t0, PEER_RT) for t0 in range(0, tm, PEER_RT)]
        counts = [route(h, tok, exact=False) for tok in toks]
        for tok, cnts in zip(toks, counts):
            bad = functools.reduce(jnp.maximum, [jnp.abs(cn - PEER_TOPK) for cn in cnts])

            @pl.when(jnp.max(bad) > 0.0)
            def _():
                route(h, tok, exact=True)
        return carry

    lax.fori_loop(0, PEER_HEADS, body, 0)


def peer_route(x3, mod3, shift_chunk, wq_bf16, subkeys, tm=256):
    nseg, seg, d = x3.shape
    tok = lambda s, i: (s, i, 0)
    rshape = jax.ShapeDtypeStruct((nseg, PEER_HEADS, PEER_NKEYS, seg), BF16)
    rspec = pl.BlockSpec((1, PEER_HEADS, PEER_NKEYS, tm), lambda s, i: (s, 0, 0, i))
    return pl.pallas_call(
        functools.partial(_peer_route_kernel, tm=tm),
        grid=(nseg, seg // tm),
        in_specs=[pl.BlockSpec((1, tm, d), tok),
                  pl.BlockSpec((1, 1, d), lambda s, i: (s, 0, shift_chunk)),
                  pl.BlockSpec((1, 1, d), lambda s, i: (s, 0, shift_chunk + 1)),
                  pl.BlockSpec(wq_bf16.shape, lambda s, i: (0, 0)),
                  pl.BlockSpec((2 * PEER_HEADS, PEER_NKEYS, PEER_HALF), lambda s, i: (0, 0, 0))],
        out_specs=[pl.BlockSpec((1, tm, d), tok), rspec, rspec, rspec, rspec],
        out_shape=[jax.ShapeDtypeStruct((nseg, seg, d), BF16)] + [rshape] * 4,
        scratch_shapes=[pltpu.VMEM((2 * PEER_HEADS, tm, PEER_HALF), F32)],
        compiler_params=_params(("arbitrary", "arbitrary")),
        name="peer_route",
    )(x3, mod3, mod3, wq_bf16, subkeys.reshape(2 * PEER_HEADS, PEER_NKEYS, PEER_HALF))


PEER_CE = 1024


def _gelu_tanh(x):
    return 0.5 * x * (1.0 + jnp.tanh(0.7978845608028654 * (x + 0.044715 * x * x * x)))


def _peer_dense_kernel(xm_ref, u_ref, vt_ref, e1_ref, cut_ref, e2_ref, r2_ref, x_ref, gate_ref, g_ref, b_ref,
                       o_ref, acc_s, at_s, w_s, e2_s, r2_s, *, tm):
    e = pl.program_id(2)
    nb = PEER_CE // PEER_NKEYS
    ntt = tm // PEER_RT

    @pl.when(e == 0)
    def _():
        acc_s[...] = jnp.zeros_like(acc_s)
        e2_s[:, :, :tm] = e2_ref[0]
        r2_s[:, :, :tm] = r2_ref[0]

    packed = (PEER_NKEYS // 16, 16, PEER_RT)
    ng = 2

    def gate_tiles(tt, i0):
        tok = slice(tt * PEER_RT, (tt + 1) * PEER_RT)
        gmats = [jnp.zeros(packed, BF16) for _ in range(ng)]
        for h in range(PEER_HEADS):
            e2 = e2_s[h, :, tok].reshape(packed)
            r2 = r2_s[h, :, tok].reshape(packed)
            for k in range(ng):
                i = i0 + k
                e1 = jnp.broadcast_to(e1_ref[0, h, i:i + 1, tok], (16, PEER_RT))[None]
                cut = jnp.broadcast_to(cut_ref[0, h, i:i + 1, tok], (16, PEER_RT))[None]
                gmats[k] = gmats[k] + e1 * jnp.where(r2 < cut, e2, jnp.zeros_like(e2))
        for k in range(ng):
            rows = slice((i0 + k) * PEER_NKEYS, (i0 + k + 1) * PEER_NKEYS)
            act = _gelu_tanh(at_s[rows, tok]).astype(BF16)
            w_s[rows, tok] = gmats[k].reshape(PEER_NKEYS, PEER_RT) * act

    at_s[:, :tm] = _dot(u_ref[0], xm_ref[0], ((1,), (1,)))
    for tt in range(ntt):
        for i0 in range(0, nb, ng):
            gate_tiles(tt, i0)
    acc_s[:, :tm] += _dot(vt_ref[0], w_s[:, :tm])

    @pl.when(e == pl.num_programs(2) - 1)
    def _():
        z = DEEPNORM_ALPHA * x_ref[0] + gate_ref[0] * acc_s[:, :tm].T
        o_ref[0] = _layer_norm_rows(z, g_ref[...], b_ref[...])


def peer_dense(xm, u_all, vt_all, l, e1, cut, e2, r2, x3, mod3, gate_chunk, ln_g, ln_b, tm=512):
    nseg, seg, d = x3.shape
    ne = u_all.shape[1]
    nb = PEER_CE // PEER_NKEYS
    tp = tm + PEER_RT
    tok = lambda s, i, e: (s, i, 0)
    chunk = pl.BlockSpec((1, PEER_HEADS, nb, tm), lambda s, i, e: (s, 0, e, i))
    full = pl.BlockSpec((1, PEER_HEADS, PEER_NKEYS, tm), lambda s, i, e: (s, 0, 0, i))
    return pl.pallas_call(
        functools.partial(_peer_dense_kernel, tm=tm),
        grid=(nseg, seg // tm, ne // PEER_CE),
        in_specs=[pl.BlockSpec((1, tm, d), tok),
                  pl.BlockSpec((1, PEER_CE, d), lambda s, i, e: (l, e, 0)),
                  pl.BlockSpec((1, d, PEER_CE), lambda s, i, e: (l, 0, e)),
                  chunk, chunk, full, full,
                  pl.BlockSpec((1, tm, d), tok),
                  pl.BlockSpec((1, 1, d), lambda s, i, e: (s, 0, gate_chunk)),
                  pl.BlockSpec((1, d), lambda s, i, e: (0, 0)),
                  pl.BlockSpec((1, d), lambda s, i, e: (0, 0))],
        out_specs=pl.BlockSpec((1, tm, d), tok),
        out_shape=jax.ShapeDtypeStruct((nseg, seg, d), F32),
        scratch_shapes=[pltpu.VMEM((d, tp), F32), pltpu.VMEM((PEER_CE, tp), F32), pltpu.VMEM((PEER_CE, tp), BF16),
                        pltpu.VMEM((PEER_HEADS, PEER_NKEYS, tp), BF16), pltpu.VMEM((PEER_HEADS, PEER_NKEYS, tp), BF16)],
        compiler_params=_params(("arbitrary", "arbitrary", "arbitrary")),
        name="peer_dense",
    )(xm, u_all, vt_all, e1, cut, e2, r2, x3, mod3, ln_g.reshape(1, d), ln_b.reshape(1, d))


def peer_layer(x3, mod3, l, wq, subkeys, u_all, vt_all, ln_g, ln_b):
    xm, e1, cut, e2, r2 = peer_route(x3, mod3, 3, wq.astype(BF16), subkeys)
    return peer_dense(xm, u_all, vt_all, l, e1, cut, e2, r2, x3, mod3, 5, ln_g, ln_b)


def _pad_cols(w, n):
    return jnp.pad(w, ((0, 0), (0, n - w.shape[1])))


def _stream(prompt_part, sample_part):
    return jnp.concatenate([prompt_part.reshape(1, -1, prompt_part.shape[-1]), sample_part], axis=0)


def _head_major(a, heads):
    b, t, _ = a.shape
    return jnp.transpose(a.reshape(b, t, heads, -1), (0, 2, 1, 3))


def _token_major(a):
    b, h, t, dh = a.shape
    return jnp.transpose(a, (0, 2, 1, 3)).reshape(b, t, h * dh)


MLSTM_CHUNK = 128
GLA_CHUNK = 32
NPROJ = 3200


def mlstm_layer(x3, mod3, bp, lp, st_c, st_n, st_m, w_in, b_gate, norm_w, w_out, ln_g, ln_b):
    nseg, seg, _ = x3.shape
    bs = nseg - 1
    p = mod_matmul(x3, mod3, 0, _pad_cols(w_in, NPROJ).astype(BF16))
    graw = p[:, :, 3072:3088]
    gp = graw[0].reshape(bp, lp, 16)
    zc = jnp.zeros((bp, 8, M_DK, M_DV), F32)
    zn = jnp.zeros((bp, 8, M_DK), F32)
    hfp, hbp, c_new, n_new, m_new = mlstm_scan(p.reshape(nseg * bp, lp, NPROJ), 0, bp, lp, gp,
                                               jnp.swapaxes(gp, 1, 2), b_gate, zc, zn, zn, min(MLSTM_CHUNK, lp))
    gs = graw[1:]
    hfs, hbs, _, _, _ = mlstm_scan(p, 1, bs, seg, gs, jnp.swapaxes(gs, 1, 2), b_gate,
                                   st_c.reshape(bs, 8, M_DK, M_DV), st_n.reshape(bs, 8, M_DK),
                                   jnp.broadcast_to(st_m.reshape(bs, 8, 1), (bs, 8, M_DK)), MLSTM_CHUNK)
    x3 = outproj_ln("mlstm", (_stream(hfp, hfs), _stream(hbp, hbs)), x3, mod3, 2, w_out.astype(BF16), ln_g, ln_b,
                    norm_w=norm_w, og=p, og_col=2)
    return (x3, c_new.reshape(bp, 2, M_HEADS, M_DK, M_DV), n_new.reshape(bp, 2, M_HEADS, M_DK),
            m_new[:, :, 0].reshape(bp, 2, M_HEADS))


def gla_layer(x3, mod3, bp, lp, st_s, w_in, w_gate2, b_gate2, norm_w, w_out, ln_g, ln_b):
    nseg, seg, _ = x3.shape
    bs = nseg - 1
    p = mod_matmul(x3, mod3, 0, _pad_cols(w_in, NPROJ).astype(BF16))
    gr = p[:, :, 3072:3104]
    zs = jnp.zeros((bp, 8, G_DV, G_DK), F32)
    ofp, obp, s_new = gla_scan(p.reshape(nseg * bp, lp, NPROJ), 0, bp, lp, gr[0].reshape(bp, lp, 32),
                               w_gate2, b_gate2, zs, GLA_CHUNK)
    s0t = jnp.swapaxes(st_s.reshape(bs, 8, G_DK, G_DV), -1, -2)
    ofs, obs, _ = gla_scan(p, 1, bs, seg, gr[1:], w_gate2, b_gate2, s0t, GLA_CHUNK)
    x3 = outproj_ln("gla", (_stream(ofp, ofs), _stream(obp, obs)), x3, mod3, 2, w_out.astype(BF16), ln_g, ln_b,
                    norm_w=jnp.tile(norm_w, G_HEADS), og=p, og_col=2)
    return x3, jnp.swapaxes(s_new, -1, -2).reshape(bp, 2, G_HEADS, G_DK, G_DV)


def na_layer(x3, mod3, bp, lp, cache_k, cache_v, w_in, rpb, w_out, ln_g, ln_b):
    nseg, seg, _ = x3.shape
    bs = nseg - 1
    hd = NA_HEADS * NA_HD
    p = mod_matmul(x3, mod3, 0, w_in.astype(BF16))
    pp = p[0].reshape(bp, lp, 3 * hd)
    hm = lambda a: _head_major(a, NA_HEADS).astype(BF16)
    yp = attention(hm(pp[..., :hd]), hm(pp[..., hd:2 * hd]), hm(pp[..., 2 * hd:]), lp)
    ps = p[1:]
    ys = na_attention(hm(ps[..., :hd]), hm(ps[..., hd:2 * hd]), hm(ps[..., 2 * hd:]),
                      hm(cache_k.reshape(bs, -1, hd)), hm(cache_v.reshape(bs, -1, hd)), na_bias_table(rpb))
    x3 = outproj_ln("plain", _stream(_token_major(yp), _token_major(ys)), x3, mod3, 2, w_out.astype(BF16), ln_g, ln_b)
    return (x3, pp[..., hd:2 * hd].reshape(bp, lp, NA_HEADS, NA_HD), pp[..., 2 * hd:].reshape(bp, lp, NA_HEADS, NA_HD))


def _rope_rotated_cols(w):
    q = MLA_ROPE // 4
    return jnp.concatenate([-w[..., q:2 * q], w[..., :q], -w[..., 3 * q:], w[..., 2 * q:3 * q]], axis=-1)


def _rope_tables(ts):
    ra = MLA_ROPE // 2
    t = np.arange(ts)
    inv = 1.0 / (ROPE_BASE ** (np.arange(0, ra, 2, dtype=np.float32) / ra))
    ang_r = (t // GRID_W).astype(np.float32)[:, None] * inv[None, :]
    ang_c = (t % GRID_W).astype(np.float32)[:, None] * inv[None, :]
    ang = np.concatenate([ang_r, ang_r, ang_c, ang_c], axis=-1).astype(np.float32)
    return jnp.cos(jnp.asarray(ang)), jnp.sin(jnp.asarray(ang))


def mla_layer(x3, mod3, bp, lp, cache_ckv, cache_kpe, w_in, q_norm, w_qup, kv_norm, w_kvup, w_out, ln_g, ln_b):
    nseg, seg, _ = x3.shape
    bs = nseg - 1
    nq = MLA_Q_LORA + MLA_KV_LORA
    w_ext = jnp.concatenate([w_in, _rope_rotated_cols(w_in[:, nq:])], axis=1)
    p = mod_matmul(x3, mod3, 0, _pad_cols(w_ext, 896).astype(BF16))
    cos_t, sin_t = _rope_tables(seg)
    cos3 = jnp.concatenate([jnp.ones((1, seg, MLA_ROPE), F32), jnp.broadcast_to(cos_t, (bs, seg, MLA_ROPE))], 0)
    sin3 = jnp.concatenate([jnp.zeros((1, seg, MLA_ROPE), F32), jnp.broadcast_to(sin_t, (bs, seg, MLA_ROPE))], 0)
    wq = w_qup.reshape(MLA_Q_LORA, MLA_HEADS, MLA_NOPE + MLA_ROPE)
    wq_rope = wq[:, :, MLA_NOPE:]
    w_q3 = jnp.concatenate([wq[:, :, :MLA_NOPE].reshape(MLA_Q_LORA, -1), wq_rope.reshape(MLA_Q_LORA, -1),
                            _rope_rotated_cols(wq_rope).reshape(MLA_Q_LORA, -1)], axis=1).astype(BF16)
    q_all = mla_q(p, q_norm, w_q3, jnp.tile(cos3, (1, 1, MLA_HEADS)), jnp.tile(sin3, (1, 1, MLA_HEADS)))
    wkv = w_kvup.reshape(MLA_KV_LORA, MLA_HEADS, MLA_NOPE + MLA_VD)
    w_kv2 = jnp.concatenate([wkv[:, :, :MLA_NOPE].reshape(MLA_KV_LORA, -1),
                             wkv[:, :, MLA_NOPE:].reshape(MLA_KV_LORA, -1)], axis=1).astype(BF16)
    ckvn, kpe, kv = mla_kv(p, kv_norm, w_kv2, cos3, sin3)
    kvc = matmul(cache_ckv.reshape(-1, MLA_KV_LORA), w_kv2, 512).reshape(bs, -1, w_kv2.shape[1])
    nn = MLA_HEADS * MLA_NOPE

    def heads(q_rows, kv_rows, kpe_rows):
        b, t, _ = q_rows.shape
        tk = kv_rows.shape[1]
        qh = jnp.concatenate([q_rows[..., :nn].reshape(b, t, MLA_HEADS, MLA_NOPE),
                              q_rows[..., nn:].reshape(b, t, MLA_HEADS, MLA_ROPE)], -1)
        kh = jnp.concatenate([kv_rows[..., :nn].reshape(b, tk, MLA_HEADS, MLA_NOPE),
                              jnp.broadcast_to(kpe_rows[:, :, None, :], (b, tk, MLA_HEADS, MLA_ROPE))], -1)
        vh = kv_rows[..., nn:].reshape(b, tk, MLA_HEADS, MLA_VD)
        tr = lambda a: jnp.transpose(a, (0, 2, 1, 3)).astype(BF16)
        return tr(qh), tr(kh), tr(vh)

    yp = attention(*heads(q_all[0].reshape(bp, lp, -1), kv[0].reshape(bp, lp, -1), kpe[0].reshape(bp, lp, -1)), lp)
    ys = attention(*heads(q_all[1:], jnp.concatenate([kv[1:], kvc], 1), jnp.concatenate([kpe[1:], cache_kpe], 1)), 256)
    x3 = outproj_ln("plain", _stream(_token_major(yp), _token_major(ys)), x3, mod3, 2, w_out.astype(BF16), ln_g, ln_b)
    return x3, ckvn[0].reshape(bp, lp, MLA_KV_LORA), kpe[0].reshape(bp, lp, MLA_ROPE)


def kernel(x_prompt, x_sample, c, c_ctx, state_mlstm_C, state_mlstm_n, state_mlstm_m, state_gla_S, cache_na_k, cache_na_v, cache_mla_ckv, cache_mla_kpe, ada_w, ada_b, ln_mix_g, ln_mix_b, ln_ffn_g, ln_ffn_b, mlstm_w_in, mlstm_b_gate, mlstm_norm_w, mlstm_w_out, gla_w_in, gla_w_gate2, gla_b_gate2, gla_norm_w, gla_w_out, na_w_in, na_rpb, na_w_out, mla_w_in, mla_q_norm, mla_w_qup, mla_kv_norm, mla_w_kvup, mla_w_out, peer_w_q, peer_subkeys, peer_u, peer_v):
    bp, lp, d = x_prompt.shape
    bs, ts, _ = x_sample.shape
    assert bp * lp == ts and bs + 1 <= 8
    x3 = _stream(x_prompt, x_sample)
    cond8 = jnp.zeros((8, d), F32).at[0].set(c_ctx).at[1:1 + bs].set(c)
    mods = adaln_all(cond8, ada_w, ada_b)
    u_all = peer_u.astype(BF16)
    vt_all = jnp.swapaxes(peer_v, 1, 2).astype(BF16)
    outs = {}
    for l in range(DEPTH):
        mod3 = mods[l].reshape(8, 1, ADA_CHUNKS * d)
        kind = l % 4
        if kind == 0:
            x3, outs["C"], outs["n"], outs["m"] = mlstm_layer(
                x3, mod3, bp, lp, state_mlstm_C, state_mlstm_n, state_mlstm_m, mlstm_w_in, mlstm_b_gate,
                mlstm_norm_w, mlstm_w_out, ln_mix_g[l], ln_mix_b[l])
        elif kind == 1:
            x3, outs["S"] = gla_layer(x3, mod3, bp, lp, state_gla_S, gla_w_in, gla_w_gate2, gla_b_gate2,
                                      gla_norm_w, gla_w_out, ln_mix_g[l], ln_mix_b[l])
        elif kind == 2:
            x3, outs["nk"], outs["nv"] = na_layer(x3, mod3, bp, lp, cache_na_k, cache_na_v, na_w_in, na_rpb,
                                                  na_w_out, ln_mix_g[l], ln_mix_b[l])
        else:
            x3, outs["ckv"], outs["kpe"] = mla_layer(x3, mod3, bp, lp, cache_mla_ckv, cache_mla_kpe, mla_w_in,
                                                     mla_q_norm, mla_w_qup, mla_kv_norm, mla_w_kvup, mla_w_out,
                                                     ln_mix_g[l], ln_mix_b[l])
        x3 = peer_layer(x3, mod3, l, peer_w_q[l], peer_subkeys[l], u_all, vt_all, ln_ffn_g[l], ln_ffn_b[l])
    return (x3[0].reshape(bp, lp, d), x3[1:], outs["C"], outs["n"], outs["m"], outs["S"], outs["nk"], outs["nv"],
            outs["ckv"], outs["kpe"])
```

```python
import functools

import numpy as np
import jax
import jax.numpy as jnp
from jax import lax
from jax.experimental import pallas as pl
from jax.experimental.pallas import tpu as pltpu

D_MODEL = 1024
DEPTH = 4
GRID_W = 64
DEEPNORM_ALPHA = (2.0 * DEPTH) ** 0.25
ADA_CHUNKS = 6
NORM_EPS = 1e-5
SEG = 4096
NSEG = 3

M_HEADS, M_DK, M_DV = 4, 128, 256
G_HEADS, G_DK, G_DV = 4, 128, 256
G_GATE_RANK = 16
G_GATE_NORM = 16.0
NA_HEADS, NA_HD, NA_ROWS, NA_COLS = 16, 64, 8, 16
MLA_HEADS, MLA_Q_LORA, MLA_KV_LORA, MLA_NOPE, MLA_ROPE, MLA_VD = 16, 512, 256, 64, 32, 64
ROPE_BASE = 10000.0
PEER_HEADS, PEER_NKEYS, PEER_HALF, PEER_TOPK = 8, 128, 128, 16

V7X_VMEM_LIMIT = 56 * 1024 * 1024
F32 = jnp.float32
BF16 = jnp.bfloat16
NEG_INF = float("-inf")


def _params(sem, vmem=V7X_VMEM_LIMIT):
    return pltpu.CompilerParams(dimension_semantics=sem, vmem_limit_bytes=vmem)


def _dot(a, b, dims=((1,), (0,))):
    return lax.dot_general(a, b, (dims, ((), ())), preferred_element_type=F32)


def _split3(a):
    hi = a.astype(BF16)
    r1 = a - hi.astype(F32)
    mid = r1.astype(BF16)
    lo = (r1 - mid.astype(F32)).astype(BF16)
    return hi, mid, lo


def _dot_exact_lhs(m01, a):
    hi, mid, lo = _split3(a)
    return _dot(m01, hi) + _dot(m01, mid) + _dot(m01, lo)


def _dot_exact_rhs(a, m01):
    hi, mid, lo = _split3(a)
    return _dot(hi, m01) + _dot(mid, m01) + _dot(lo, m01)


def _log_sigmoid(x):
    return jnp.minimum(x, 0.0) - jnp.log(1.0 + jnp.exp(-jnp.abs(x)))


def _sigmoid(x):
    return 1.0 / (1.0 + jnp.exp(-x))


def _adaln_kernel(c_ref, w_ref, b_ref, o_ref):
    cv = c_ref[...]
    a = cv * _sigmoid(cv)
    o_ref[0] = lax.dot_general(a, w_ref[0], (((1,), (0,)), ((), ())), precision=lax.Precision.HIGHEST,
                               preferred_element_type=F32) + b_ref[0]


def adaln_all(cond8, ada_w, ada_b):
    tn = 1024
    n = ada_w.shape[-1]
    return pl.pallas_call(
        _adaln_kernel,
        grid=(DEPTH, n // tn),
        in_specs=[pl.BlockSpec((8, D_MODEL), lambda l, j: (0, 0)),
                  pl.BlockSpec((1, D_MODEL, tn), lambda l, j: (l, 0, j)),
                  pl.BlockSpec((1, 1, tn), lambda l, j: (l, 0, j))],
        out_specs=pl.BlockSpec((1, 8, tn), lambda l, j: (l, 0, j)),
        out_shape=jax.ShapeDtypeStruct((DEPTH, 8, n), F32),
        compiler_params=_params(("arbitrary", "arbitrary")),
        name="adaln",
    )(cond8, ada_w, ada_b.reshape(DEPTH, 1, n))


def _modmm_kernel(x_ref, sh_ref, sc_ref, w_ref, o_ref, xm_ref):
    @pl.when(pl.program_id(2) == 0)
    def _():
        xm_ref[...] = (x_ref[0] * (1.0 + sc_ref[0]) + sh_ref[0]).astype(BF16)

    o_ref[0] = _dot(xm_ref[...], w_ref[...]).astype(o_ref.dtype)


def mod_matmul(x3, mod3, shift_chunk, w_bf16, tm=512, tn=None, out_dtype=F32):
    nseg, seg, d = x3.shape
    n = w_bf16.shape[1]
    tn = n if tn is None else tn
    return pl.pallas_call(
        _modmm_kernel,
        grid=(nseg, seg // tm, n // tn),
        in_specs=[pl.BlockSpec((1, tm, d), lambda s, i, j: (s, i, 0)),
                  pl.BlockSpec((1, 1, d), lambda s, i, j: (s, 0, shift_chunk)),
                  pl.BlockSpec((1, 1, d), lambda s, i, j: (s, 0, shift_chunk + 1)),
                  pl.BlockSpec((d, tn), lambda s, i, j: (0, j))],
        out_specs=pl.BlockSpec((1, tm, tn), lambda s, i, j: (s, i, j)),
        out_shape=jax.ShapeDtypeStruct((nseg, seg, n), out_dtype),
        scratch_shapes=[pltpu.VMEM((tm, d), BF16)],
        compiler_params=_params(("arbitrary", "arbitrary", "arbitrary")),
        name="mod_matmul",
    )(x3, mod3, mod3, w_bf16)


def _layer_norm_rows(y, g, b):
    mu = jnp.mean(y, axis=-1, keepdims=True)
    yc = y - mu
    var = jnp.mean(yc * yc, axis=-1, keepdims=True)
    return yc * lax.rsqrt(var + NORM_EPS) * g + b


def _outproj_kernel(*refs, mode):
    if mode == "plain":
        y_ref, x_ref, gate_ref, w_ref, g_ref, b_ref, o_ref = refs
        yin = y_ref[0].astype(BF16)
    else:
        ya_ref, yb_ref, og_ref, nw_ref, x_ref, gate_ref, w_ref, g_ref, b_ref, o_ref = refs
        hs = ya_ref[0] + yb_ref[0]
        og = og_ref[0]
        parts = []
        for h in range(4):
            seg = hs[:, h * 256:(h + 1) * 256]
            nw = nw_ref[:, h * 256:(h + 1) * 256]
            if mode == "mlstm":
                mu = jnp.mean(seg, axis=-1, keepdims=True)
                sc = seg - mu
                var = jnp.mean(sc * sc, axis=-1, keepdims=True)
                parts.append(sc * lax.rsqrt(var + NORM_EPS) * nw)
            else:
                ms = jnp.mean(seg * seg, axis=-1, keepdims=True)
                parts.append(seg * lax.rsqrt(ms + NORM_EPS) * nw)
        hn = jnp.concatenate(parts, axis=-1)
        act = _sigmoid(og) if mode == "mlstm" else og * _sigmoid(og)
        yin = (act * hn).astype(BF16)
    y = _dot(yin, w_ref[...])
    z = DEEPNORM_ALPHA * x_ref[0] + gate_ref[0] * y
    o_ref[0] = _layer_norm_rows(z, g_ref[...], b_ref[...])


def outproj_ln(mode, ys, x3, mod3, gate_chunk, w_bf16, ln_g, ln_b, norm_w=None, og=None, og_col=0, tm=512):
    nseg, seg, d = x3.shape
    k = w_bf16.shape[0]
    tok = lambda s, i: (s, i, 0)
    if mode == "plain":
        args = [ys]
        specs = [pl.BlockSpec((1, tm, k), tok)]
    else:
        args = [ys[0], ys[1], og, norm_w.reshape(1, k)]
        specs = [pl.BlockSpec((1, tm, k), tok), pl.BlockSpec((1, tm, k), tok),
                 pl.BlockSpec((1, tm, k), lambda s, i: (s, i, og_col)),
                 pl.BlockSpec((1, k), lambda s, i: (0, 0))]
    args += [x3, mod3, w_bf16, ln_g.reshape(1, d), ln_b.reshape(1, d)]
    specs += [pl.BlockSpec((1, tm, d), tok),
              pl.BlockSpec((1, 1, d), lambda s, i: (s, 0, gate_chunk)),
              pl.BlockSpec((k, d), lambda s, i: (0, 0)),
              pl.BlockSpec((1, d), lambda s, i: (0, 0)),
              pl.BlockSpec((1, d), lambda s, i: (0, 0))]
    return pl.pallas_call(
        functools.partial(_outproj_kernel, mode=mode),
        grid=(nseg, seg // tm),
        in_specs=specs,
        out_specs=pl.BlockSpec((1, tm, d), tok),
        out_shape=jax.ShapeDtypeStruct((nseg, seg, d), F32),
        compiler_params=_params(("arbitrary", "arbitrary")),
        name="outproj_ln_" + mode,
    )(*args)


def _tri(n, lower):
    r = lax.broadcasted_iota(jnp.int32, (n, n), 0)
    c = lax.broadcasted_iota(jnp.int32, (n, n), 1)
    return (c <= r) if lower else (c >= r)


def _mlstm_kernel(pf_ref, pb_ref, gf_ref, gb_ref, gtf_ref, gtb_ref, bias_ref, biast_ref,
                  c0_ref, n0_ref, m0_ref, hf_ref, hb_ref, co_ref, no_ref, mo_ref,
                  c_s, n_s, m_s, *, L):
    c = pl.program_id(1)

    @pl.when(c == 0)
    def _():
        c_s[...] = c0_ref[0]
        n_s[...] = n0_ref[0]
        m_s[...] = m0_ref[0]

    for d in range(2):
        p_ref, g_ref, gt_ref, h_ref = ((pf_ref, gf_ref, gtf_ref, hf_ref) if d == 0
                                       else (pb_ref, gb_ref, gtb_ref, hb_ref))
        mask = _tri(L, lower=(d == 0))
        mcol = mask.astype(BF16)
        mrow = _tri(L, lower=(d != 0)).astype(BF16)
        g = g_ref[0] + bias_ref[...]
        gt = gt_ref[0] + biast_ref[...]
        li_c = g[:, d * 8:d * 8 + 4]
        lf_c = _log_sigmoid(g[:, d * 8 + 4:d * 8 + 8])
        li_r = gt[d * 8:d * 8 + 4, :]
        lf_r = _log_sigmoid(gt[d * 8 + 4:d * 8 + 8, :])
        b_c = _dot_exact_lhs(mcol, lf_c)
        b_r = _dot_exact_rhs(lf_r, mrow)
        last = L - 1 if d == 0 else 0
        for h in range(M_HEADS):
            u = d * M_HEADS + h
            q = p_ref[0, :, h * M_DK:(h + 1) * M_DK]
            k = p_ref[0, :, 512 + h * M_DK:512 + (h + 1) * M_DK] * (M_DK ** -0.5)
            v = p_ref[0, :, 1024 + h * M_DV:1024 + (h + 1) * M_DV].astype(BF16)
            qb = q.astype(BF16)
            bc, br = b_c[:, h:h + 1], b_r[h:h + 1, :]
            lic, lir = li_c[:, h:h + 1], li_r[h:h + 1, :]
            m_prev = m_s[u:u + 1, 0:1]
            dmat = jnp.where(mask, bc - br + lir, NEG_INF)
            inter = bc + m_prev
            mt = jnp.maximum(inter, jnp.max(dmat, axis=-1, keepdims=True))
            smat = _dot(qb, k.astype(BF16), ((1,), (1,))) * jnp.exp(dmat - mt)
            ei = jnp.exp(inter - mt)
            cmat = c_s[u]
            num = _dot(smat.astype(BF16), v) + ei * _dot(qb, cmat.astype(BF16))
            nrow = n_s[u:u + 1, :]
            den = jnp.sum(smat, axis=-1, keepdims=True) + ei * jnp.sum(q * nrow, axis=-1, keepdims=True)
            h_ref[0, :, h * M_DV:(h + 1) * M_DV] = num / jnp.maximum(jnp.abs(den), jnp.exp(-mt))
            tot = br[:, last:last + 1]
            g_c = tot - bc + lic
            g_r = tot - br + lir
            m_new = jnp.maximum(tot + m_prev, jnp.max(g_r, axis=-1, keepdims=True))
            kw = k * jnp.exp(g_c - m_new)
            dec = jnp.exp(tot + m_prev - m_new)
            c_s[u] = dec * cmat + _dot(kw.astype(BF16), v, ((0,), (0,)))
            n_s[u:u + 1, :] = dec * nrow + jnp.sum(kw, axis=0, keepdims=True)
            m_s[u:u + 1, :] = jnp.broadcast_to(m_new, (1, 128))

    @pl.when(c == pl.num_programs(1) - 1)
    def _():
        co_ref[0] = c_s[...]
        no_ref[0] = n_s[...]
        mo_ref[0] = m_s[...]


def mlstm_scan(p, b0, nb, t, g, gt, bias, c0, n0, m0, L):
    nc = t // L
    hshape = jax.ShapeDtypeStruct((nb, t, M_HEADS * M_DV), F32)
    fwd = lambda b, c: (b + b0, c, 0)
    bwd = lambda b, c: (b + b0, nc - 1 - c, 0)
    st4 = lambda b, c: (b, 0, 0, 0)
    st3 = lambda b, c: (b, 0, 0)
    return pl.pallas_call(
        functools.partial(_mlstm_kernel, L=L),
        grid=(nb, nc),
        in_specs=[pl.BlockSpec((1, L, 2048), fwd), pl.BlockSpec((1, L, 2048), bwd),
                  pl.BlockSpec((1, L, 16), lambda b, c: (b, c, 0)),
                  pl.BlockSpec((1, L, 16), lambda b, c: (b, nc - 1 - c, 0)),
                  pl.BlockSpec((1, 16, L), lambda b, c: (b, 0, c)),
                  pl.BlockSpec((1, 16, L), lambda b, c: (b, 0, nc - 1 - c)),
                  pl.BlockSpec((1, 16), lambda b, c: (0, 0)),
                  pl.BlockSpec((16, 1), lambda b, c: (0, 0)),
                  pl.BlockSpec((1, 8, M_DK, M_DV), st4),
                  pl.BlockSpec((1, 8, M_DK), st3),
                  pl.BlockSpec((1, 8, M_DK), st3)],
        out_specs=[pl.BlockSpec((1, L, 1024), lambda b, c: (b, c, 0)),
                   pl.BlockSpec((1, L, 1024), lambda b, c: (b, nc - 1 - c, 0)),
                   pl.BlockSpec((1, 8, M_DK, M_DV), st4),
                   pl.BlockSpec((1, 8, M_DK), st3),
                   pl.BlockSpec((1, 8, M_DK), st3)],
        out_shape=[hshape, hshape,
                   jax.ShapeDtypeStruct((nb, 8, M_DK, M_DV), F32),
                   jax.ShapeDtypeStruct((nb, 8, M_DK), F32),
                   jax.ShapeDtypeStruct((nb, 8, M_DK), F32)],
        scratch_shapes=[pltpu.VMEM((8, M_DK, M_DV), F32), pltpu.VMEM((8, M_DK), F32),
                        pltpu.VMEM((8, M_DK), F32)],
        compiler_params=_params(("arbitrary", "arbitrary")),
        name="mlstm_scan",
    )(p, p, g, g, gt, gt, bias.reshape(1, 16), bias.reshape(16, 1), c0, n0, m0)


def _gla_kernel(pf_ref, pb_ref, gf_ref, gb_ref, w2_ref, b2_ref, s0_ref, of_ref, ob_ref, so_ref, s_s, *, L):
    c = pl.program_id(1)

    @pl.when(c == 0)
    def _():
        s_s[...] = s0_ref[0]

    for d in range(2):
        p_ref, g_ref, o_ref = (pf_ref, gf_ref, of_ref) if d == 0 else (pb_ref, gb_ref, ob_ref)
        mask = _tri(L, lower=(d == 0))
        mcol = mask.astype(BF16)
        gr = g_ref[0][:, d * G_GATE_RANK:(d + 1) * G_GATE_RANK]
        pre = lax.dot_general(gr, w2_ref[d], (((1,), (0,)), ((), ())), precision=lax.Precision.HIGHEST,
                              preferred_element_type=F32) + b2_ref[d]
        la = _log_sigmoid(pre) * (1.0 / G_GATE_NORM)
        bc_all = _dot_exact_lhs(mcol, la)
        last = L - 1 if d == 0 else 0
        for h in range(G_HEADS):
            u = d * G_HEADS + h
            q = p_ref[0, :, h * G_DK:(h + 1) * G_DK] * (G_DK ** -0.5)
            k = p_ref[0, :, 512 + h * G_DK:512 + (h + 1) * G_DK]
            v = p_ref[0, :, 1024 + h * G_DV:1024 + (h + 1) * G_DV].astype(BF16)
            bc = bc_all[:, h * G_DK:(h + 1) * G_DK]
            qd = (q * jnp.exp(bc)).astype(BF16)
            kd = (k * jnp.exp(-bc)).astype(BF16)
            a = jnp.where(mask, _dot(qd, kd, ((1,), (1,))), 0.0)
            st = s_s[u]
            o_ref[0, :, h * G_DV:(h + 1) * G_DV] = (_dot(a.astype(BF16), v)
                                                    + _dot(qd, st.astype(BF16), ((1,), (1,))))
            bl = bc[last:last + 1, :]
            kl = (k * jnp.exp(bl - bc)).astype(BF16)
            s_s[u] = st * jnp.exp(bl) + _dot(v, kl, ((0,), (0,)))

    @pl.when(c == pl.num_programs(1) - 1)
    def _():
        so_ref[0] = s_s[...]


def gla_scan(p, b0, nb, t, gr, w2, b2, s0t, L):
    nc = t // L
    oshape = jax.ShapeDtypeStruct((nb, t, G_HEADS * G_DV), F32)
    st4 = lambda b, c: (b, 0, 0, 0)
    return pl.pallas_call(
        functools.partial(_gla_kernel, L=L),
        grid=(nb, nc),
        in_specs=[pl.BlockSpec((1, L, 2048), lambda b, c: (b + b0, c, 0)),
                  pl.BlockSpec((1, L, 2048), lambda b, c: (b + b0, nc - 1 - c, 0)),
                  pl.BlockSpec((1, L, 32), lambda b, c: (b, c, 0)),
                  pl.BlockSpec((1, L, 32), lambda b, c: (b, nc - 1 - c, 0)),
                  pl.BlockSpec((2, G_GATE_RANK, 512), lambda b, c: (0, 0, 0)),
                  pl.BlockSpec((2, 1, 512), lambda b, c: (0, 0, 0)),
                  pl.BlockSpec((1, 8, G_DV, G_DK), st4)],
        out_specs=[pl.BlockSpec((1, L, 1024), lambda b, c: (b, c, 0)),
                   pl.BlockSpec((1, L, 1024), lambda b, c: (b, nc - 1 - c, 0)),
                   pl.BlockSpec((1, 8, G_DV, G_DK), st4)],
        out_shape=[oshape, oshape, jax.ShapeDtypeStruct((nb, 8, G_DV, G_DK), F32)],
        scratch_shapes=[pltpu.VMEM((8, G_DV, G_DK), F32)],
        compiler_params=_params(("arbitrary", "arbitrary")),
        name="gla_scan",
    )(p, p, gr, gr, w2, b2.reshape(2, 1, 512), s0t)


def _attn_kernel(q_ref, k_ref, v_ref, o_ref, *, scale):
    s = _dot(q_ref[0, 0], k_ref[0, 0], ((1,), (1,))) * scale
    m = jnp.max(s, axis=-1, keepdims=True)
    p = jnp.exp(s - m)
    l = jnp.sum(p, axis=-1, keepdims=True)
    o_ref[0, 0] = _dot(p.astype(BF16), v_ref[0, 0]) / l


def attention(q, k, v, tq):
    b, h, lq, dq = q.shape
    lk, dv = k.shape[2], v.shape[3]
    return pl.pallas_call(
        functools.partial(_attn_kernel, scale=dq ** -0.5),
        grid=(b, h, lq // tq),
        in_specs=[pl.BlockSpec((1, 1, tq, dq), lambda b, h, i: (b, h, i, 0)),
                  pl.BlockSpec((1, 1, lk, dq), lambda b, h, i: (b, h, 0, 0)),
                  pl.BlockSpec((1, 1, lk, dv), lambda b, h, i: (b, h, 0, 0))],
        out_specs=pl.BlockSpec((1, 1, tq, dv), lambda b, h, i: (b, h, i, 0)),
        out_shape=jax.ShapeDtypeStruct((b, h, lq, dv), F32),
        compiler_params=_params(("arbitrary", "arbitrary", "arbitrary")),
        name="attention",
    )(q, k, v)


NA_RB = 8


def _na_kernel(q_ref, k_ref, v_ref, kc_ref, vc_ref, bias_ref, o_ref, *, rows):
    j = pl.program_id(2)
    scale = NA_HD ** -0.5
    kc, vc = kc_ref[0, 0], vc_ref[0, 0]
    for a in range(NA_RB):
        r = j * NA_RB + a
        start = jnp.clip(r - NA_ROWS // 2, 0, rows - NA_ROWS)
        dr0 = start - r + (NA_ROWS - 1)
        off = pl.multiple_of(start * GRID_W, GRID_W)
        qa = q_ref[0, 0, a * GRID_W:(a + 1) * GRID_W, :]
        kl = k_ref[0, 0, pl.ds(off, NA_ROWS * GRID_W), :]
        vl = v_ref[0, 0, pl.ds(off, NA_ROWS * GRID_W), :]
        s_loc = _dot(qa, kl, ((1,), (1,))) * scale + bias_ref[0, dr0]
        s_ctx = _dot(qa, kc, ((1,), (1,))) * scale
        m = jnp.maximum(jnp.max(s_loc, axis=-1, keepdims=True), jnp.max(s_ctx, axis=-1, keepdims=True))
        p_loc = jnp.exp(s_loc - m)
        p_ctx = jnp.exp(s_ctx - m)
        l = jnp.sum(p_loc, axis=-1, keepdims=True) + jnp.sum(p_ctx, axis=-1, keepdims=True)
        o = _dot(p_loc.astype(BF16), vl) + _dot(p_ctx.astype(BF16), vc)
        o_ref[0, 0, a * GRID_W:(a + 1) * GRID_W, :] = o / l


def na_bias_table(rpb):
    cq = np.arange(GRID_W)[:, None]
    ck = np.arange(GRID_W)[None, :]
    cs = np.clip(cq - NA_COLS // 2, 0, GRID_W - NA_COLS)
    ok = (ck >= cs) & (ck < cs + NA_COLS)
    dc = np.clip(ck - cq, -(NA_COLS - 1), NA_COLS - 1) + (NA_COLS - 1)
    t = jnp.where(ok[None, None], rpb.astype(F32)[:, :, dc], NEG_INF)
    rows = np.arange(NA_ROWS)[:, None] + np.arange(NA_ROWS)[None, :]
    tf = t[:, rows]
    return jnp.transpose(tf, (0, 1, 3, 2, 4)).reshape(NA_HEADS, NA_ROWS, GRID_W, NA_ROWS * GRID_W)


def na_attention(q, k, v, kc, vc, bias):
    b, h, t, dh = q.shape
    lc = kc.shape[2]
    rows = t // GRID_W
    full = lambda b, h, j: (b, h, 0, 0)
    return pl.pallas_call(
        functools.partial(_na_kernel, rows=rows),
        grid=(b, h, rows // NA_RB),
        in_specs=[pl.BlockSpec((1, 1, NA_RB * GRID_W, dh), lambda b, h, j: (b, h, j, 0)),
                  pl.BlockSpec((1, 1, t, dh), full), pl.BlockSpec((1, 1, t, dh), full),
                  pl.BlockSpec((1, 1, lc, dh), full), pl.BlockSpec((1, 1, lc, dh), full),
                  pl.BlockSpec((1, NA_ROWS, GRID_W, NA_ROWS * GRID_W), lambda b, h, j: (h, 0, 0, 0))],
        out_specs=pl.BlockSpec((1, 1, NA_RB * GRID_W, dh), lambda b, h, j: (b, h, j, 0)),
        out_shape=jax.ShapeDtypeStruct((b, h, t, dh), F32),
        compiler_params=_params(("arbitrary", "arbitrary", "arbitrary")),
        name="na_attention",
    )(q, k, v, kc, vc, bias)


def _rms_rows(x, g):
    return x * lax.rsqrt(jnp.mean(x * x, axis=-1, keepdims=True) + NORM_EPS) * g


def _mla_q_kernel(cq_ref, g_ref, w_ref, cos_ref, sin_ref, o_ref):
    r = _dot(_rms_rows(cq_ref[0], g_ref[...]).astype(BF16), w_ref[...])
    nn = MLA_HEADS * MLA_NOPE
    nr = MLA_HEADS * MLA_ROPE
    o_ref[0, :, :nn] = r[:, :nn]
    o_ref[0, :, nn:] = r[:, nn:nn + nr] * cos_ref[0] + r[:, nn + nr:] * sin_ref[0]


def mla_q(p, q_norm, w_q3, cos_q, sin_q, tm=512):
    nseg, seg, _ = p.shape
    nout = MLA_HEADS * (MLA_NOPE + MLA_ROPE)
    nr = MLA_HEADS * MLA_ROPE
    tok = lambda s, i: (s, i, 0)
    return pl.pallas_call(
        _mla_q_kernel,
        grid=(nseg, seg // tm),
        in_specs=[pl.BlockSpec((1, tm, MLA_Q_LORA), tok),
                  pl.BlockSpec((1, MLA_Q_LORA), lambda s, i: (0, 0)),
                  pl.BlockSpec(w_q3.shape, lambda s, i: (0, 0)),
                  pl.BlockSpec((1, tm, nr), tok), pl.BlockSpec((1, tm, nr), tok)],
        out_specs=pl.BlockSpec((1, tm, nout), tok),
        out_shape=jax.ShapeDtypeStruct((nseg, seg, nout), F32),
        compiler_params=_params(("arbitrary", "arbitrary")),
        name="mla_q",
    )(p, q_norm.reshape(1, -1), w_q3, cos_q, sin_q)


def _mla_kv_kernel(ckv_ref, kpe_ref, g_ref, w_ref, cos_ref, sin_ref, ckvn_ref, kpeo_ref, kv_ref):
    cn = _rms_rows(ckv_ref[0], g_ref[...])
    ckvn_ref[0] = cn
    kv_ref[0] = _dot(cn.astype(BF16), w_ref[...])
    kp = kpe_ref[0]
    kpeo_ref[0] = kp[:, :MLA_ROPE] * cos_ref[0] + kp[:, MLA_ROPE:2 * MLA_ROPE] * sin_ref[0]


def mla_kv(p, kv_norm, w_kv, cos_k, sin_k, tm=512):
    nseg, seg, _ = p.shape
    nkv = w_kv.shape[1]
    tok = lambda s, i: (s, i, 0)
    return pl.pallas_call(
        _mla_kv_kernel,
        grid=(nseg, seg // tm),
        in_specs=[pl.BlockSpec((1, tm, MLA_KV_LORA), lambda s, i: (s, i, MLA_Q_LORA // MLA_KV_LORA)),
                  pl.BlockSpec((1, tm, 128), lambda s, i: (s, i, (MLA_Q_LORA + MLA_KV_LORA) // 128)),
                  pl.BlockSpec((1, MLA_KV_LORA), lambda s, i: (0, 0)),
                  pl.BlockSpec(w_kv.shape, lambda s, i: (0, 0)),
                  pl.BlockSpec((1, tm, MLA_ROPE), tok), pl.BlockSpec((1, tm, MLA_ROPE), tok)],
        out_specs=[pl.BlockSpec((1, tm, MLA_KV_LORA), tok), pl.BlockSpec((1, tm, MLA_ROPE), tok),
                   pl.BlockSpec((1, tm, nkv), tok)],
        out_shape=[jax.ShapeDtypeStruct((nseg, seg, MLA_KV_LORA), F32),
                   jax.ShapeDtypeStruct((nseg, seg, MLA_ROPE), F32),
                   jax.ShapeDtypeStruct((nseg, seg, nkv), F32)],
        compiler_params=_params(("arbitrary", "arbitrary")),
        name="mla_kv",
    )(p, p, kv_norm.reshape(1, -1), w_kv, cos_k, sin_k)


def _mm_kernel(a_ref, w_ref, o_ref):
    o_ref[...] = _dot(a_ref[...].astype(BF16), w_ref[...])


def matmul(a, w_bf16, tm):
    m, k = a.shape
    n = w_bf16.shape[1]
    return pl.pallas_call(
        _mm_kernel,
        grid=(m // tm,),
        in_specs=[pl.BlockSpec((tm, k), lambda i: (i, 0)), pl.BlockSpec((k, n), lambda i: (0, 0))],
        out_specs=pl.BlockSpec((tm, n), lambda i: (i, 0)),
        out_shape=jax.ShapeDtypeStruct((m, n), F32),
        compiler_params=_params(("arbitrary",)),
        name="matmul",
    )(a, w_bf16)


PEER_RT = 128
NOT_TOP = 99.0


def _top16(s, exact):
    key = lax.broadcasted_iota(jnp.int32, s.shape, 0).astype(F32)
    rank = jnp.full(s.shape, NOT_TOP, F32)
    vals = []
    for r in range(PEER_TOPK):
        m = jnp.max(s, axis=0, keepdims=True)
        hit = s == m
        if exact:
            hit = key == jnp.min(jnp.where(hit, key, 1e9), axis=0, keepdims=True)
        rank = jnp.where(hit, float(r), rank)
        s = jnp.where(hit, NEG_INF, s)
        vals.append(m)
    return vals, rank


def _pair_topk(av, bv, exact):
    n = av[0].shape[-1]
    a_lo, a_hi = jnp.concatenate(av[:8], 0), jnp.concatenate(av[8:], 0)
    b_lo, b_hi = jnp.concatenate(bv[:8], 0), jnp.concatenate(bv[8:], 0)
    row = lax.broadcasted_iota(jnp.int32, (8, n), 0).astype(F32)

    no_pos = 1e8

    def rows_b(a, b_blk, boff, nvalid):
        ok = row < nvalid
        return jnp.where(ok, av[a] + b_blk, NEG_INF), jnp.where(ok, a * 16.0 + boff + row, no_pos)

    def rows_a(b, a_blk, aoff, lo, hi):
        ok = (row >= lo) & (row < hi)
        return jnp.where(ok, a_blk + bv[b], NEG_INF), jnp.where(ok, (aoff + row) * 16.0 + b, no_pos)

    groups = [rows_b(0, b_lo, 0, 8), rows_b(0, b_hi, 8, 8), rows_b(1, b_lo, 0, 8), rows_b(2, b_lo, 0, 5),
              rows_b(3, b_lo, 0, 4), rows_a(0, a_lo, 0, 4, 8), rows_a(0, a_hi, 8, 0, 8),
              rows_a(1, a_lo, 0, 4, 8), rows_a(2, a_lo, 0, 4, 5)]
    cands = [g[0] for g in groups]
    poss = [g[1] for g in groups]
    sels = [jnp.zeros((8, n), F32) for _ in groups]
    top = av[0] + bv[0]
    z = jnp.zeros((1, n), F32)
    for _ in range(PEER_TOPK):
        m = functools.reduce(jnp.maximum, cands)
        m = jnp.max(m, axis=0, keepdims=True)
        hits = [c == m for c in cands]
        if exact:
            first = functools.reduce(jnp.minimum, [jnp.where(hh, p, 1e9) for hh, p in zip(hits, poss)])
            first = jnp.min(first, axis=0, keepdims=True)
            hits = [p == first for p in poss]
        cands = [jnp.where(hh, NEG_INF, c) for hh, c in zip(hits, cands)]
        sels = [jnp.where(hh, 1.0, s) for hh, s in zip(hits, sels)]
        z = z + jnp.exp(m - top)
    cnt = lambda x: jnp.sum(x, axis=0, keepdims=True)
    cut_lo = sels[5] + sels[7] + sels[8]
    for a, c in enumerate([cnt(sels[0]) + cnt(sels[1]), cnt(sels[2]), cnt(sels[3]), cnt(sels[4])]):
        cut_lo = cut_lo + jnp.where(row == a, c, 0.0)
    return cut_lo, sels[6], z, cnt(cut_lo) + cnt(sels[6])


def _peer_route_kernel(x_ref, sh_ref, sc_ref, wq_ref, sk_ref, xm_ref, e1_ref, cut_ref, e2_ref, r2_ref, q_s, *, tm):
    xm = (x_ref[0] * (1.0 + sc_ref[0]) + sh_ref[0]).astype(BF16)
    xm_ref[0] = xm
    q = _dot(xm, wq_ref[...])
    for hp in range(2 * PEER_HEADS):
        q_s[hp] = q[:, hp * PEER_HALF:(hp + 1) * PEER_HALF]

    def route(h, tok, exact):
        def scores(hp):
            return lax.dot_general(sk_ref[hp], q_s[hp, tok, :], (((1,), (1,)), ((), ())),
                                   precision=lax.Precision.HIGHEST, preferred_element_type=F32)

        s1, s2 = scores(2 * h), scores(2 * h + 1)
        av, rank1 = _top16(s1, exact)
        bv, rank2 = _top16(s2, exact)
        cut_lo, cut_hi, z, nsel = _pair_topk(av, bv, exact)
        cut = jnp.zeros_like(s1)
        for r in range(PEER_TOPK):
            src = cut_lo if r < 8 else cut_hi
            cut = jnp.where(rank1 == float(r), src[r % 8:r % 8 + 1, :], cut)
        e1_ref[0, h, :, tok] = (jnp.exp(s1 - av[0]) / z).astype(BF16)
        cut_ref[0, h, :, tok] = cut.astype(BF16)
        e2_ref[0, h, :, tok] = jnp.exp(s2 - bv[0]).astype(BF16)
        r2_ref[0, h, :, tok] = rank2.astype(BF16)
        ranked = lambda rk: jnp.sum(jnp.where(rk < PEER_TOPK, 1.0, 0.0), axis=0, keepdims=True)
        return ranked(rank1), ranked(rank2), nsel

    def body(h, carry):
        toks = [pl.ds(t0, PEER_RT) for t0 in range(0, tm, PEER_RT)]
        counts = [route(h, tok, exact=False) for tok in toks]
        for tok, cnts in zip(toks, counts):
            bad = functools.reduce(jnp.maximum, [jnp.abs(cn - PEER_TOPK) for cn in cnts])

            @pl.when(jnp.max(bad) > 0.0)
            def _():
                route(h, tok, exact=True)
        return carry

    lax.fori_loop(0, PEER_HEADS, body, 0)


def peer_route(x3, mod3, shift_chunk, wq_bf16, subkeys, tm=256):
    nseg, seg, d = x3.shape
    tok = lambda s, i: (s, i, 0)
    rshape = jax.ShapeDtypeStruct((nseg, PEER_HEADS, PEER_NKEYS, seg), BF16)
    rspec = pl.BlockSpec((1, PEER_HEADS, PEER_NKEYS, tm), lambda s, i: (s, 0, 0, i))
    return pl.pallas_call(
        functools.partial(_peer_route_kernel, tm=tm),
        grid=(nseg, seg // tm),
        in_specs=[pl.BlockSpec((1, tm, d), tok),
                  pl.BlockSpec((1, 1, d), lambda s, i: (s, 0, shift_chunk)),
                  pl.BlockSpec((1, 1, d), lambda s, i: (s, 0, shift_chunk + 1)),
                  pl.BlockSpec(wq_bf16.shape, lambda s, i: (0, 0)),
                  pl.BlockSpec((2 * PEER_HEADS, PEER_NKEYS, PEER_HALF), lambda s, i: (0, 0, 0))],
        out_specs=[pl.BlockSpec((1, tm, d), tok), rspec, rspec, rspec, rspec],
        out_shape=[jax.ShapeDtypeStruct((nseg, seg, d), BF16)] + [rshape] * 4,
        scratch_shapes=[pltpu.VMEM((2 * PEER_HEADS, tm, PEER_HALF), F32)],
        compiler_params=_params(("arbitrary", "arbitrary")),
        name="peer_route",
    )(x3, mod3, mod3, wq_bf16, subkeys.reshape(2 * PEER_HEADS, PEER_NKEYS, PEER_HALF))


PEER_CE = 1024


def _gelu_tanh(x):
    return 0.5 * x * (1.0 + jnp.tanh(0.7978845608028654 * (x + 0.044715 * x * x * x)))


def _peer_dense_kernel(xm_ref, u_ref, vt_ref, e1_ref, cut_ref, e2_ref, r2_ref, x_ref, gate_ref, g_ref, b_ref,
                       o_ref, acc_s, at_s, w_s, e2_s, r2_s, *, tm):
    e = pl.program_id(2)
    nb = PEER_CE // PEER_NKEYS
    ntt = tm // PEER_RT

    @pl.when(e == 0)
    def _():
        acc_s[...] = jnp.zeros_like(acc_s)
        e2_s[:, :, :tm] = e2_ref[0]
        r2_s[:, :, :tm] = r2_ref[0]

    packed = (PEER_NKEYS // 16, 16, PEER_RT)
    ng = 2

    def gate_tiles(tt, i0):
        tok = slice(tt * PEER_RT, (tt + 1) * PEER_RT)
        gmats = [jnp.zeros(packed, BF16) for _ in range(ng)]
        for h in range(PEER_HEADS):
            e2 = e2_s[h, :, tok].reshape(packed)
            r2 = r2_s[h, :, tok].reshape(packed)
            for k in range(ng):
                i = i0 + k
                e1 = jnp.broadcast_to(e1_ref[0, h, i:i + 1, tok], (16, PEER_RT))[None]
                cut = jnp.broadcast_to(cut_ref[0, h, i:i + 1, tok], (16, PEER_RT))[None]
                gmats[k] = gmats[k] + e1 * jnp.where(r2 < cut, e2, jnp.zeros_like(e2))
        for k in range(ng):
            rows = slice((i0 + k) * PEER_NKEYS, (i0 + k + 1) * PEER_NKEYS)
            act = _gelu_tanh(at_s[rows, tok]).astype(BF16)
            w_s[rows, tok] = gmats[k].reshape(PEER_NKEYS, PEER_RT) * act

    at_s[:, :tm] = _dot(u_ref[0], xm_ref[0], ((1,), (1,)))
    for tt in range(ntt):
        for i0 in range(0, nb, ng):
            gate_tiles(tt, i0)
    acc_s[:, :tm] += _dot(vt_ref[0, 0], w_s[:, :tm])

    @pl.when(e == pl.num_programs(2) - 1)
    def _():
        z = DEEPNORM_ALPHA * x_ref[0] + gate_ref[0] * acc_s[:, :tm].T
        o_ref[0] = _layer_norm_rows(z, g_ref[...], b_ref[...])


def peer_dense(xm, u_all, vt_all, l, e1, cut, e2, r2, x3, mod3, gate_chunk, ln_g, ln_b, tm=512):
    nseg, seg, d = x3.shape
    ne = u_all.shape[1]
    nb = PEER_CE // PEER_NKEYS
    tp = tm + PEER_RT
    tok = lambda s, i, e: (s, i, 0)
    chunk = pl.BlockSpec((1, PEER_HEADS, nb, tm), lambda s, i, e: (s, 0, e, i))
    full = pl.BlockSpec((1, PEER_HEADS, PEER_NKEYS, tm), lambda s, i, e: (s, 0, 0, i))
    return pl.pallas_call(
        functools.partial(_peer_dense_kernel, tm=tm),
        grid=(nseg, seg // tm, ne // PEER_CE),
        in_specs=[pl.BlockSpec((1, tm, d), tok),
                  pl.BlockSpec((1, PEER_CE, d), lambda s, i, e: (l, e, 0)),
                  pl.BlockSpec((1, 1, d, PEER_CE), lambda s, i, e: (l, e, 0, 0)),
                  chunk, chunk, full, full,
                  pl.BlockSpec((1, tm, d), tok),
                  pl.BlockSpec((1, 1, d), lambda s, i, e: (s, 0, gate_chunk)),
                  pl.BlockSpec((1, d), lambda s, i, e: (0, 0)),
                  pl.BlockSpec((1, d), lambda s, i, e: (0, 0))],
        out_specs=pl.BlockSpec((1, tm, d), tok),
        out_shape=jax.ShapeDtypeStruct((nseg, seg, d), F32),
        scratch_shapes=[pltpu.VMEM((d, tp), F32), pltpu.VMEM((PEER_CE, tp), F32), pltpu.VMEM((PEER_CE, tp), BF16),
                        pltpu.VMEM((PEER_HEADS, PEER_NKEYS, tp), BF16), pltpu.VMEM((PEER_HEADS, PEER_NKEYS, tp), BF16)],
        compiler_params=_params(("arbitrary", "arbitrary", "arbitrary")),
        name="peer_dense",
    )(xm, u_all, vt_all, e1, cut, e2, r2, x3, mod3, ln_g.reshape(1, d), ln_b.reshape(1, d))


def peer_layer(x3, mod3, l, wq, subkeys, u_all, vt_all, ln_g, ln_b):
    xm, e1, cut, e2, r2 = peer_route(x3, mod3, 3, wq.astype(BF16), subkeys)
    return peer_dense(xm, u_all, vt_all, l, e1, cut, e2, r2, x3, mod3, 5, ln_g, ln_b)


def _pad_cols(w, n):
    return jnp.pad(w, ((0, 0), (0, n - w.shape[1])))


def _stream(prompt_part, sample_part):
    return jnp.concatenate([prompt_part.reshape(1, -1, prompt_part.shape[-1]), sample_part], axis=0)


def _head_major(a, heads):
    b, t, _ = a.shape
    return jnp.transpose(a.reshape(b, t, heads, -1), (0, 2, 1, 3))


def _token_major(a):
    b, h, t, dh = a.shape
    return jnp.transpose(a, (0, 2, 1, 3)).reshape(b, t, h * dh)


MLSTM_CHUNK = 128
GLA_CHUNK = 32
NPROJ = 3200


def mlstm_layer(x3, mod3, bp, lp, st_c, st_n, st_m, w_in, b_gate, norm_w, w_out, ln_g, ln_b):
    nseg, seg, _ = x3.shape
    bs = nseg - 1
    p = mod_matmul(x3, mod3, 0, _pad_cols(w_in, NPROJ).astype(BF16))
    graw = p[:, :, 3072:3088]
    gp = graw[0].reshape(bp, lp, 16)
    zc = jnp.zeros((bp, 8, M_DK, M_DV), F32)
    zn = jnp.zeros((bp, 8, M_DK), F32)
    hfp, hbp, c_new, n_new, m_new = mlstm_scan(p.reshape(nseg * bp, lp, NPROJ), 0, bp, lp, gp,
                                               jnp.swapaxes(gp, 1, 2), b_gate, zc, zn, zn, min(MLSTM_CHUNK, lp))
    gs = graw[1:]
    hfs, hbs, _, _, _ = mlstm_scan(p, 1, bs, seg, gs, jnp.swapaxes(gs, 1, 2), b_gate,
                                   st_c.reshape(bs, 8, M_DK, M_DV), st_n.reshape(bs, 8, M_DK),
                                   jnp.broadcast_to(st_m.reshape(bs, 8, 1), (bs, 8, M_DK)), MLSTM_CHUNK)
    x3 = outproj_ln("mlstm", (_stream(hfp, hfs), _stream(hbp, hbs)), x3, mod3, 2, w_out.astype(BF16), ln_g, ln_b,
                    norm_w=norm_w, og=p, og_col=2)
    return (x3, c_new.reshape(bp, 2, M_HEADS, M_DK, M_DV), n_new.reshape(bp, 2, M_HEADS, M_DK),
            m_new[:, :, 0].reshape(bp, 2, M_HEADS))


def gla_layer(x3, mod3, bp, lp, st_s, w_in, w_gate2, b_gate2, norm_w, w_out, ln_g, ln_b):
    nseg, seg, _ = x3.shape
    bs = nseg - 1
    p = mod_matmul(x3, mod3, 0, _pad_cols(w_in, NPROJ).astype(BF16))
    gr = p[:, :, 3072:3104]
    zs = jnp.zeros((bp, 8, G_DV, G_DK), F32)
    ofp, obp, s_new = gla_scan(p.reshape(nseg * bp, lp, NPROJ), 0, bp, lp, gr[0].reshape(bp, lp, 32),
                               w_gate2, b_gate2, zs, GLA_CHUNK)
    s0t = jnp.swapaxes(st_s.reshape(bs, 8, G_DK, G_DV), -1, -2)
    ofs, obs, _ = gla_scan(p, 1, bs, seg, gr[1:], w_gate2, b_gate2, s0t, GLA_CHUNK)
    x3 = outproj_ln("gla", (_stream(ofp, ofs), _stream(obp, obs)), x3, mod3, 2, w_out.astype(BF16), ln_g, ln_b,
                    norm_w=jnp.tile(norm_w, G_HEADS), og=p, og_col=2)
    return x3, jnp.swapaxes(s_new, -1, -2).reshape(bp, 2, G_HEADS, G_DK, G_DV)


def na_layer(x3, mod3, bp, lp, cache_k, cache_v, w_in, rpb, w_out, ln_g, ln_b):
    nseg, seg, _ = x3.shape
    bs = nseg - 1
    hd = NA_HEADS * NA_HD
    p = mod_matmul(x3, mod3, 0, w_in.astype(BF16))
    pp = p[0].reshape(bp, lp, 3 * hd)
    hm = lambda a: _head_major(a, NA_HEADS).astype(BF16)
    yp = attention(hm(pp[..., :hd]), hm(pp[..., hd:2 * hd]), hm(pp[..., 2 * hd:]), lp)
    ps = p[1:]
    ys = na_attention(hm(ps[..., :hd]), hm(ps[..., hd:2 * hd]), hm(ps[..., 2 * hd:]),
                      hm(cache_k.reshape(bs, -1, hd)), hm(cache_v.reshape(bs, -1, hd)), na_bias_table(rpb))
    x3 = outproj_ln("plain", _stream(_token_major(yp), _token_major(ys)), x3, mod3, 2, w_out.astype(BF16), ln_g, ln_b)
    return (x3, pp[..., hd:2 * hd].reshape(bp, lp, NA_HEADS, NA_HD), pp[..., 2 * hd:].reshape(bp, lp, NA_HEADS, NA_HD))


def _rope_rotated_cols(w):
    q = MLA_ROPE // 4
    return jnp.concatenate([-w[..., q:2 * q], w[..., :q], -w[..., 3 * q:], w[..., 2 * q:3 * q]], axis=-1)


def _rope_tables(ts):
    ra = MLA_ROPE // 2
    t = np.arange(ts)
    inv = 1.0 / (ROPE_BASE ** (np.arange(0, ra, 2, dtype=np.float32) / ra))
    ang_r = (t // GRID_W).astype(np.float32)[:, None] * inv[None, :]
    ang_c = (t % GRID_W).astype(np.float32)[:, None] * inv[None, :]
    ang = np.concatenate([ang_r, ang_r, ang_c, ang_c], axis=-1).astype(np.float32)
    return jnp.cos(jnp.asarray(ang)), jnp.sin(jnp.asarray(ang))


def mla_layer(x3, mod3, bp, lp, cache_ckv, cache_kpe, w_in, q_norm, w_qup, kv_norm, w_kvup, w_out, ln_g, ln_b):
    nseg, seg, _ = x3.shape
    bs = nseg - 1
    nq = MLA_Q_LORA + MLA_KV_LORA
    w_ext = jnp.concatenate([w_in, _rope_rotated_cols(w_in[:, nq:])], axis=1)
    p = mod_matmul(x3, mod3, 0, _pad_cols(w_ext, 896).astype(BF16))
    cos_t, sin_t = _rope_tables(seg)
    cos3 = jnp.concatenate([jnp.ones((1, seg, MLA_ROPE), F32), jnp.broadcast_to(cos_t, (bs, seg, MLA_ROPE))], 0)
    sin3 = jnp.concatenate([jnp.zeros((1, seg, MLA_ROPE), F32), jnp.broadcast_to(sin_t, (bs, seg, MLA_ROPE))], 0)
    wq = w_qup.reshape(MLA_Q_LORA, MLA_HEADS, MLA_NOPE + MLA_ROPE)
    wq_rope = wq[:, :, MLA_NOPE:]
    w_q3 = jnp.concatenate([wq[:, :, :MLA_NOPE].reshape(MLA_Q_LORA, -1), wq_rope.reshape(MLA_Q_LORA, -1),
                            _rope_rotated_cols(wq_rope).reshape(MLA_Q_LORA, -1)], axis=1).astype(BF16)
    q_all = mla_q(p, q_norm, w_q3, jnp.tile(cos3, (1, 1, MLA_HEADS)), jnp.tile(sin3, (1, 1, MLA_HEADS)))
    wkv = w_kvup.reshape(MLA_KV_LORA, MLA_HEADS, MLA_NOPE + MLA_VD)
    w_kv2 = jnp.concatenate([wkv[:, :, :MLA_NOPE].reshape(MLA_KV_LORA, -1),
                             wkv[:, :, MLA_NOPE:].reshape(MLA_KV_LORA, -1)], axis=1).astype(BF16)
    ckvn, kpe, kv = mla_kv(p, kv_norm, w_kv2, cos3, sin3)
    kvc = matmul(cache_ckv.reshape(-1, MLA_KV_LORA), w_kv2, 512).reshape(bs, -1, w_kv2.shape[1])
    nn = MLA_HEADS * MLA_NOPE

    def heads(q_rows, kv_rows, kpe_rows):
        b, t, _ = q_rows.shape
        tk = kv_rows.shape[1]
        qh = jnp.concatenate([q_rows[..., :nn].reshape(b, t, MLA_HEADS, MLA_NOPE),
                              q_rows[..., nn:].reshape(b, t, MLA_HEADS, MLA_ROPE)], -1)
        kh = jnp.concatenate([kv_rows[..., :nn].reshape(b, tk, MLA_HEADS, MLA_NOPE),
                              jnp.broadcast_to(kpe_rows[:, :, None, :], (b, tk, MLA_HEADS, MLA_ROPE))], -1)
        vh = kv_rows[..., nn:].reshape(b, tk, MLA_HEADS, MLA_VD)
        tr = lambda a: jnp.transpose(a, (0, 2, 1, 3)).astype(BF16)
        return tr(qh), tr(kh), tr(vh)

    yp = attention(*heads(q_all[0].reshape(bp, lp, -1), kv[0].reshape(bp, lp, -1), kpe[0].reshape(bp, lp, -1)), lp)
    ys = attention(*heads(q_all[1:], jnp.concatenate([kv[1:], kvc], 1), jnp.concatenate([kpe[1:], cache_kpe], 1)), 256)
    x3 = outproj_ln("plain", _stream(_token_major(yp), _token_major(ys)), x3, mod3, 2, w_out.astype(BF16), ln_g, ln_b)
    return x3, ckvn[0].reshape(bp, lp, MLA_KV_LORA), kpe[0].reshape(bp, lp, MLA_ROPE)


def kernel(x_prompt, x_sample, c, c_ctx, state_mlstm_C, state_mlstm_n, state_mlstm_m, state_gla_S, cache_na_k, cache_na_v, cache_mla_ckv, cache_mla_kpe, ada_w, ada_b, ln_mix_g, ln_mix_b, ln_ffn_g, ln_ffn_b, mlstm_w_in, mlstm_b_gate, mlstm_norm_w, mlstm_w_out, gla_w_in, gla_w_gate2, gla_b_gate2, gla_norm_w, gla_w_out, na_w_in, na_rpb, na_w_out, mla_w_in, mla_q_norm, mla_w_qup, mla_kv_norm, mla_w_kvup, mla_w_out, peer_w_q, peer_subkeys, peer_u, peer_v):
    bp, lp, d = x_prompt.shape
    bs, ts, _ = x_sample.shape
    assert bp * lp == ts and bs + 1 <= 8
    x3 = _stream(x_prompt, x_sample)
    cond8 = jnp.zeros((8, d), F32).at[0].set(c_ctx).at[1:1 + bs].set(c)
    mods = adaln_all(cond8, ada_w, ada_b)
    u_all = peer_u.astype(BF16)
    vt_all = jnp.swapaxes(peer_v.reshape(DEPTH, -1, PEER_CE, d), 2, 3).astype(BF16)
    outs = {}
    for l in range(DEPTH):
        mod3 = mods[l].reshape(8, 1, ADA_CHUNKS * d)
        kind = l % 4
        if kind == 0:
            x3, outs["C"], outs["n"], outs["m"] = mlstm_layer(
                x3, mod3, bp, lp, state_mlstm_C, state_mlstm_n, state_mlstm_m, mlstm_w_in, mlstm_b_gate,
                mlstm_norm_w, mlstm_w_out, ln_mix_g[l], ln_mix_b[l])
        elif kind == 1:
            x3, outs["S"] = gla_layer(x3, mod3, bp, lp, state_gla_S, gla_w_in, gla_w_gate2, gla_b_gate2,
                                      gla_norm_w, gla_w_out, ln_mix_g[l], ln_mix_b[l])
        elif kind == 2:
            x3, outs["nk"], outs["nv"] = na_layer(x3, mod3, bp, lp, cache_na_k, cache_na_v, na_w_in, na_rpb,
                                                  na_w_out, ln_mix_g[l], ln_mix_b[l])
        else:
            x3, outs["ckv"], outs["kpe"] = mla_layer(x3, mod3, bp, lp, cache_mla_ckv, cache_mla_kpe, mla_w_in,
                                                     mla_q_norm, mla_w_qup, mla_kv_norm, mla_w_kvup, mla_w_out,
                                                     ln_mix_g[l], ln_mix_b[l])
        x3 = peer_layer(x3, mod3, l, peer_w_q[l], peer_subkeys[l], u_all, vt_all, ln_ffn_g[l], ln_ffn_b[l])
    return (x3[0].reshape(bp, lp, d), x3[1:], outs["C"], outs["n"], outs["m"], outs["S"], outs["nk"], outs["nv"],
            outs["ckv"], outs["kpe"])
```

```python
import functools

import numpy as np
import jax
import jax.numpy as jnp
from jax import lax
from jax.experimental import pallas as pl
from jax.experimental.pallas import tpu as pltpu

D_MODEL = 1024
DEPTH = 4
GRID_W = 64
DEEPNORM_ALPHA = (2.0 * DEPTH) ** 0.25
ADA_CHUNKS = 6
NORM_EPS = 1e-5
SEG = 4096
NSEG = 3

M_HEADS, M_DK, M_DV = 4, 128, 256
G_HEADS, G_DK, G_DV = 4, 128, 256
G_GATE_RANK = 16
G_GATE_NORM = 16.0
NA_HEADS, NA_HD, NA_ROWS, NA_COLS = 16, 64, 8, 16
MLA_HEADS, MLA_Q_LORA, MLA_KV_LORA, MLA_NOPE, MLA_ROPE, MLA_VD = 16, 512, 256, 64, 32, 64
ROPE_BASE = 10000.0
PEER_HEADS, PEER_NKEYS, PEER_HALF, PEER_TOPK = 8, 128, 128, 16

V7X_VMEM_LIMIT = 56 * 1024 * 1024
F32 = jnp.float32
BF16 = jnp.bfloat16
NEG_INF = float("-inf")


def _params(sem, vmem=V7X_VMEM_LIMIT):
    return pltpu.CompilerParams(dimension_semantics=sem, vmem_limit_bytes=vmem)


def _dot(a, b, dims=((1,), (0,))):
    return lax.dot_general(a, b, (dims, ((), ())), preferred_element_type=F32)


def _split3(a):
    hi = a.astype(BF16)
    r1 = a - hi.astype(F32)
    mid = r1.astype(BF16)
    lo = (r1 - mid.astype(F32)).astype(BF16)
    return hi, mid, lo


def _dot_exact_lhs(m01, a):
    hi, mid, lo = _split3(a)
    return _dot(m01, hi) + _dot(m01, mid) + _dot(m01, lo)


def _dot_exact_rhs(a, m01):
    hi, mid, lo = _split3(a)
    return _dot(hi, m01) + _dot(mid, m01) + _dot(lo, m01)


def _log_sigmoid(x):
    return jnp.minimum(x, 0.0) - jnp.log(1.0 + jnp.exp(-jnp.abs(x)))


def _sigmoid(x):
    return 1.0 / (1.0 + jnp.exp(-x))


def _adaln_kernel(c_ref, w_ref, b_ref, o_ref):
    cv = c_ref[...]
    a = cv * _sigmoid(cv)
    o_ref[0] = lax.dot_general(a, w_ref[0], (((1,), (0,)), ((), ())), precision=lax.Precision.HIGHEST,
                               preferred_element_type=F32) + b_ref[0]


def adaln_all(cond8, ada_w, ada_b):
    tn = 1024
    n = ada_w.shape[-1]
    return pl.pallas_call(
        _adaln_kernel,
        grid=(DEPTH, n // tn),
        in_specs=[pl.BlockSpec((8, D_MODEL), lambda l, j: (0, 0)),
                  pl.BlockSpec((1, D_MODEL, tn), lambda l, j: (l, 0, j)),
                  pl.BlockSpec((1, 1, tn), lambda l, j: (l, 0, j))],
        out_specs=pl.BlockSpec((1, 8, tn), lambda l, j: (l, 0, j)),
        out_shape=jax.ShapeDtypeStruct((DEPTH, 8, n), F32),
        compiler_params=_params(("arbitrary", "arbitrary")),
        name="adaln",
    )(cond8, ada_w, ada_b.reshape(DEPTH, 1, n))


def _modmm_kernel(x_ref, sh_ref, sc_ref, w_ref, o_ref, xm_ref):
    @pl.when(pl.program_id(2) == 0)
    def _():
        xm_ref[...] = (x_ref[0] * (1.0 + sc_ref[0]) + sh_ref[0]).astype(BF16)

    o_ref[0] = _dot(xm_ref[...], w_ref[...]).astype(o_ref.dtype)


def mod_matmul(x3, mod3, shift_chunk, w_bf16, tm=512, tn=None, out_dtype=F32):
    nseg, seg, d = x3.shape
    n = w_bf16.shape[1]
    tn = n if tn is None else tn
    return pl.pallas_call(
        _modmm_kernel,
        grid=(nseg, seg // tm, n // tn),
        in_specs=[pl.BlockSpec((1, tm, d), lambda s, i, j: (s, i, 0)),
                  pl.BlockSpec((1, 1, d), lambda s, i, j: (s, 0, shift_chunk)),
                  pl.BlockSpec((1, 1, d), lambda s, i, j: (s, 0, shift_chunk + 1)),
                  pl.BlockSpec((d, tn), lambda s, i, j: (0, j))],
        out_specs=pl.BlockSpec((1, tm, tn), lambda s, i, j: (s, i, j)),
        out_shape=jax.ShapeDtypeStruct((nseg, seg, n), out_dtype),
        scratch_shapes=[pltpu.VMEM((tm, d), BF16)],
        compiler_params=_params(("arbitrary", "arbitrary", "arbitrary")),
        name="mod_matmul",
    )(x3, mod3, mod3, w_bf16)


def _layer_norm_rows(y, g, b):
    mu = jnp.mean(y, axis=-1, keepdims=True)
    yc = y - mu
    var = jnp.mean(yc * yc, axis=-1, keepdims=True)
    return yc * lax.rsqrt(var + NORM_EPS) * g + b


def _outproj_kernel(*refs, mode):
    if mode == "plain":
        y_ref, x_ref, gate_ref, w_ref, g_ref, b_ref, o_ref = refs
        yin = y_ref[0].astype(BF16)
    else:
        ya_ref, yb_ref, og_ref, nw_ref, x_ref, gate_ref, w_ref, g_ref, b_ref, o_ref = refs
        hs = ya_ref[0] + yb_ref[0]
        og = og_ref[0]
        parts = []
        for h in range(4):
            seg = hs[:, h * 256:(h + 1) * 256]
            nw = nw_ref[:, h * 256:(h + 1) * 256]
            if mode == "mlstm":
                mu = jnp.mean(seg, axis=-1, keepdims=True)
                sc = seg - mu
                var = jnp.mean(sc * sc, axis=-1, keepdims=True)
                parts.append(sc * lax.rsqrt(var + NORM_EPS) * nw)
            else:
                ms = jnp.mean(seg * seg, axis=-1, keepdims=True)
                parts.append(seg * lax.rsqrt(ms + NORM_EPS) * nw)
        hn = jnp.concatenate(parts, axis=-1)
        act = _sigmoid(og) if mode == "mlstm" else og * _sigmoid(og)
        yin = (act * hn).astype(BF16)
    y = _dot(yin, w_ref[...])
    z = DEEPNORM_ALPHA * x_ref[0] + gate_ref[0] * y
    o_ref[0] = _layer_norm_rows(z, g_ref[...], b_ref[...])


def outproj_ln(mode, ys, x3, mod3, gate_chunk, w_bf16, ln_g, ln_b, norm_w=None, og=None, og_col=0, tm=512):
    nseg, seg, d = x3.shape
    k = w_bf16.shape[0]
    tok = lambda s, i: (s, i, 0)
    if mode == "plain":
        args = [ys]
        specs = [pl.BlockSpec((1, tm, k), tok)]
    else:
        args = [ys[0], ys[1], og, norm_w.reshape(1, k)]
        specs = [pl.BlockSpec((1, tm, k), tok), pl.BlockSpec((1, tm, k), tok),
                 pl.BlockSpec((1, tm, k), lambda s, i: (s, i, og_col)),
                 pl.BlockSpec((1, k), lambda s, i: (0, 0))]
    args += [x3, mod3, w_bf16, ln_g.reshape(1, d), ln_b.reshape(1, d)]
    specs += [pl.BlockSpec((1, tm, d), tok),
              pl.BlockSpec((1, 1, d), lambda s, i: (s, 0, gate_chunk)),
              pl.BlockSpec((k, d), lambda s, i: (0, 0)),
              pl.BlockSpec((1, d), lambda s, i: (0, 0)),
              pl.BlockSpec((1, d), lambda s, i: (0, 0))]
    return pl.pallas_call(
        functools.partial(_outproj_kernel, mode=mode),
        grid=(nseg, seg // tm),
        in_specs=specs,
        out_specs=pl.BlockSpec((1, tm, d), tok),
        out_shape=jax.ShapeDtypeStruct((nseg, seg, d), F32),
        compiler_params=_params(("arbitrary", "arbitrary")),
        name="outproj_ln_" + mode,
    )(*args)


def _tri(n, lower):
    r = lax.broadcasted_iota(jnp.int32, (n, n), 0)
    c = lax.broadcasted_iota(jnp.int32, (n, n), 1)
    return (c <= r) if lower else (c >= r)


def _mlstm_kernel(pf_ref, pb_ref, gf_ref, gb_ref, gtf_ref, gtb_ref, bias_ref, biast_ref,
                  c0_ref, n0_ref, m0_ref, hf_ref, hb_ref, co_ref, no_ref, mo_ref,
                  c_s, n_s, m_s, *, L):
    c = pl.program_id(1)

    @pl.when(c == 0)
    def _():
        c_s[...] = c0_ref[0]
        n_s[...] = n0_ref[0]
        m_s[...] = m0_ref[0]

    for d in range(2):
        p_ref, g_ref, gt_ref, h_ref = ((pf_ref, gf_ref, gtf_ref, hf_ref) if d == 0
                                       else (pb_ref, gb_ref, gtb_ref, hb_ref))
        mask = _tri(L, lower=(d == 0))
        mcol = mask.astype(BF16)
        mrow = _tri(L, lower=(d != 0)).astype(BF16)
        g = g_ref[0] + bias_ref[...]
        gt = gt_ref[0] + biast_ref[...]
        li_c = g[:, d * 8:d * 8 + 4]
        lf_c = _log_sigmoid(g[:, d * 8 + 4:d * 8 + 8])
        li_r = gt[d * 8:d * 8 + 4, :]
        lf_r = _log_sigmoid(gt[d * 8 + 4:d * 8 + 8, :])
        b_c = _dot_exact_lhs(mcol, lf_c)
        b_r = _dot_exact_rhs(lf_r, mrow)
        last = L - 1 if d == 0 else 0
        for h in range(M_HEADS):
            u = d * M_HEADS + h
            q = p_ref[0, :, h * M_DK:(h + 1) * M_DK]
            k = p_ref[0, :, 512 + h * M_DK:512 + (h + 1) * M_DK] * (M_DK ** -0.5)
            v = p_ref[0, :, 1024 + h * M_DV:1024 + (h + 1) * M_DV].astype(BF16)
            qb = q.astype(BF16)
            bc, br = b_c[:, h:h + 1], b_r[h:h + 1, :]
            lic, lir = li_c[:, h:h + 1], li_r[h:h + 1, :]
            m_prev = m_s[u:u + 1, 0:1]
            dmat = jnp.where(mask, bc - br + lir, NEG_INF)
            inter = bc + m_prev
            mt = jnp.maximum(inter, jnp.max(dmat, axis=-1, keepdims=True))
            smat = _dot(qb, k.astype(BF16), ((1,), (1,))) * jnp.exp(dmat - mt)
            ei = jnp.exp(inter - mt)
            cmat = c_s[u]
            num = _dot(smat.astype(BF16), v) + ei * _dot(qb, cmat.astype(BF16))
            nrow = n_s[u:u + 1, :]
            den = jnp.sum(smat, axis=-1, keepdims=True) + ei * jnp.sum(q * nrow, axis=-1, keepdims=True)
            h_ref[0, :, h * M_DV:(h + 1) * M_DV] = num / jnp.maximum(jnp.abs(den), jnp.exp(-mt))
            tot = br[:, last:last + 1]
            g_c = tot - bc + lic
            g_r = tot - br + lir
            m_new = jnp.maximum(tot + m_prev, jnp.max(g_r, axis=-1, keepdims=True))
            kw = k * jnp.exp(g_c - m_new)
            dec = jnp.exp(tot + m_prev - m_new)
            c_s[u] = dec * cmat + _dot(kw.astype(BF16), v, ((0,), (0,)))
            n_s[u:u + 1, :] = dec * nrow + jnp.sum(kw, axis=0, keepdims=True)
            m_s[u:u + 1, :] = jnp.broadcast_to(m_new, (1, 128))

    @pl.when(c == pl.num_programs(1) - 1)
    def _():
        co_ref[0] = c_s[...]
        no_ref[0] = n_s[...]
        mo_ref[0] = m_s[...]


def mlstm_scan(p, b0, nb, t, g, gt, bias, c0, n0, m0, L):
    nc = t // L
    hshape = jax.ShapeDtypeStruct((nb, t, M_HEADS * M_DV), F32)
    fwd = lambda b, c: (b + b0, c, 0)
    bwd = lambda b, c: (b + b0, nc - 1 - c, 0)
    st4 = lambda b, c: (b, 0, 0, 0)
    st3 = lambda b, c: (b, 0, 0)
    return pl.pallas_call(
        functools.partial(_mlstm_kernel, L=L),
        grid=(nb, nc),
        in_specs=[pl.BlockSpec((1, L, 2048), fwd), pl.BlockSpec((1, L, 2048), bwd),
                  pl.BlockSpec((1, L, 16), lambda b, c: (b, c, 0)),
                  pl.BlockSpec((1, L, 16), lambda b, c: (b, nc - 1 - c, 0)),
                  pl.BlockSpec((1, 16, L), lambda b, c: (b, 0, c)),
                  pl.BlockSpec((1, 16, L), lambda b, c: (b, 0, nc - 1 - c)),
                  pl.BlockSpec((1, 16), lambda b, c: (0, 0)),
                  pl.BlockSpec((16, 1), lambda b, c: (0, 0)),
                  pl.BlockSpec((1, 8, M_DK, M_DV), st4),
                  pl.BlockSpec((1, 8, M_DK), st3),
                  pl.BlockSpec((1, 8, M_DK), st3)],
        out_specs=[pl.BlockSpec((1, L, 1024), lambda b, c: (b, c, 0)),
                   pl.BlockSpec((1, L, 1024), lambda b, c: (b, nc - 1 - c, 0)),
                   pl.BlockSpec((1, 8, M_DK, M_DV), st4),
                   pl.BlockSpec((1, 8, M_DK), st3),
                   pl.BlockSpec((1, 8, M_DK), st3)],
        out_shape=[hshape, hshape,
                   jax.ShapeDtypeStruct((nb, 8, M_DK, M_DV), F32),
                   jax.ShapeDtypeStruct((nb, 8, M_DK), F32),
                   jax.ShapeDtypeStruct((nb, 8, M_DK), F32)],
        scratch_shapes=[pltpu.VMEM((8, M_DK, M_DV), F32), pltpu.VMEM((8, M_DK), F32),
                        pltpu.VMEM((8, M_DK), F32)],
        compiler_params=_params(("arbitrary", "arbitrary")),
        name="mlstm_scan",
    )(p, p, g, g, gt, gt, bias.reshape(1, 16), bias.reshape(16, 1), c0, n0, m0)


def _gla_kernel(pf_ref, pb_ref, gf_ref, gb_ref, w2_ref, b2_ref, s0_ref, of_ref, ob_ref, so_ref, s_s, *, L):
    c = pl.program_id(1)

    @pl.when(c == 0)
    def _():
        s_s[...] = s0_ref[0]

    for d in range(2):
        p_ref, g_ref, o_ref = (pf_ref, gf_ref, of_ref) if d == 0 else (pb_ref, gb_ref, ob_ref)
        mask = _tri(L, lower=(d == 0))
        mcol = mask.astype(BF16)
        gr = g_ref[0][:, d * G_GATE_RANK:(d + 1) * G_GATE_RANK]
        pre = lax.dot_general(gr, w2_ref[d], (((1,), (0,)), ((), ())), precision=lax.Precision.HIGHEST,
                              preferred_element_type=F32) + b2_ref[d]
        la = _log_sigmoid(pre) * (1.0 / G_GATE_NORM)
        bc_all = _dot_exact_lhs(mcol, la)
        last = L - 1 if d == 0 else 0
        for h in range(G_HEADS):
            u = d * G_HEADS + h
            q = p_ref[0, :, h * G_DK:(h + 1) * G_DK] * (G_DK ** -0.5)
            k = p_ref[0, :, 512 + h * G_DK:512 + (h + 1) * G_DK]
            v = p_ref[0, :, 1024 + h * G_DV:1024 + (h + 1) * G_DV].astype(BF16)
            bc = bc_all[:, h * G_DK:(h + 1) * G_DK]
            qd = (q * jnp.exp(bc)).astype(BF16)
            kd = (k * jnp.exp(-bc)).astype(BF16)
            a = jnp.where(mask, _dot(qd, kd, ((1,), (1,))), 0.0)
            st = s_s[u]
            o_ref[0, :, h * G_DV:(h + 1) * G_DV] = (_dot(a.astype(BF16), v)
                                                    + _dot(qd, st.astype(BF16), ((1,), (1,))))
            bl = bc[last:last + 1, :]
            kl = (k * jnp.exp(bl - bc)).astype(BF16)
            s_s[u] = st * jnp.exp(bl) + _dot(v, kl, ((0,), (0,)))

    @pl.when(c == pl.num_programs(1) - 1)
    def _():
        so_ref[0] = s_s[...]


def gla_scan(p, b0, nb, t, gr, w2, b2, s0t, L):
    nc = t // L
    oshape = jax.ShapeDtypeStruct((nb, t, G_HEADS * G_DV), F32)
    st4 = lambda b, c: (b, 0, 0, 0)
    return pl.pallas_call(
        functools.partial(_gla_kernel, L=L),
        grid=(nb, nc),
        in_specs=[pl.BlockSpec((1, L, 2048), lambda b, c: (b + b0, c, 0)),
                  pl.BlockSpec((1, L, 2048), lambda b, c: (b + b0, nc - 1 - c, 0)),
                  pl.BlockSpec((1, L, 32), lambda b, c: (b, c, 0)),
                  pl.BlockSpec((1, L, 32), lambda b, c: (b, nc - 1 - c, 0)),
                  pl.BlockSpec((2, G_GATE_RANK, 512), lambda b, c: (0, 0, 0)),
                  pl.BlockSpec((2, 1, 512), lambda b, c: (0, 0, 0)),
                  pl.BlockSpec((1, 8, G_DV, G_DK), st4)],
        out_specs=[pl.BlockSpec((1, L, 1024), lambda b, c: (b, c, 0)),
                   pl.BlockSpec((1, L, 1024), lambda b, c: (b, nc - 1 - c, 0)),
                   pl.BlockSpec((1, 8, G_DV, G_DK), st4)],
        out_shape=[oshape, oshape, jax.ShapeDtypeStruct((nb, 8, G_DV, G_DK), F32)],
        scratch_shapes=[pltpu.VMEM((8, G_DV, G_DK), F32)],
        compiler_params=_params(("arbitrary", "arbitrary")),
        name="gla_scan",
    )(p, p, gr, gr, w2, b2.reshape(2, 1, 512), s0t)


def _attn_kernel(q_ref, k_ref, v_ref, o_ref, *, scale):
    s = _dot(q_ref[0, 0], k_ref[0, 0], ((1,), (1,))) * scale
    m = jnp.max(s, axis=-1, keepdims=True)
    p = jnp.exp(s - m)
    l = jnp.sum(p, axis=-1, keepdims=True)
    o_ref[0, 0] = _dot(p.astype(BF16), v_ref[0, 0]) / l


def attention(q, k, v, tq):
    b, h, lq, dq = q.shape
    lk, dv = k.shape[2], v.shape[3]
    return pl.pallas_call(
        functools.partial(_attn_kernel, scale=dq ** -0.5),
        grid=(b, h, lq // tq),
        in_specs=[pl.BlockSpec((1, 1, tq, dq), lambda b, h, i: (b, h, i, 0)),
                  pl.BlockSpec((1, 1, lk, dq), lambda b, h, i: (b, h, 0, 0)),
                  pl.BlockSpec((1, 1, lk, dv), lambda b, h, i: (b, h, 0, 0))],
        out_specs=pl.BlockSpec((1, 1, tq, dv), lambda b, h, i: (b, h, i, 0)),
        out_shape=jax.ShapeDtypeStruct((b, h, lq, dv), F32),
        compiler_params=_params(("arbitrary", "arbitrary", "arbitrary")),
        name="attention",
    )(q, k, v)


NA_RB = 8


def _na_kernel(q_ref, k_ref, v_ref, kc_ref, vc_ref, bias_ref, o_ref, *, rows):
    j = pl.program_id(2)
    scale = NA_HD ** -0.5
    kc, vc = kc_ref[0, 0], vc_ref[0, 0]
    for a in range(NA_RB):
        r = j * NA_RB + a
        start = jnp.clip(r - NA_ROWS // 2, 0, rows - NA_ROWS)
        dr0 = start - r + (NA_ROWS - 1)
        off = pl.multiple_of(start * GRID_W, GRID_W)
        qa = q_ref[0, 0, a * GRID_W:(a + 1) * GRID_W, :]
        kl = k_ref[0, 0, pl.ds(off, NA_ROWS * GRID_W), :]
        vl = v_ref[0, 0, pl.ds(off, NA_ROWS * GRID_W), :]
        s_loc = _dot(qa, kl, ((1,), (1,))) * scale + bias_ref[0, dr0]
        s_ctx = _dot(qa, kc, ((1,), (1,))) * scale
        m = jnp.maximum(jnp.max(s_loc, axis=-1, keepdims=True), jnp.max(s_ctx, axis=-1, keepdims=True))
        p_loc = jnp.exp(s_loc - m)
        p_ctx = jnp.exp(s_ctx - m)
        l = jnp.sum(p_loc, axis=-1, keepdims=True) + jnp.sum(p_ctx, axis=-1, keepdims=True)
        o = _dot(p_loc.astype(BF16), vl) + _dot(p_ctx.astype(BF16), vc)
        o_ref[0, 0, a * GRID_W:(a + 1) * GRID_W, :] = o / l


def na_bias_table(rpb):
    cq = np.arange(GRID_W)[:, None]
    ck = np.arange(GRID_W)[None, :]
    cs = np.clip(cq - NA_COLS // 2, 0, GRID_W - NA_COLS)
    ok = (ck >= cs) & (ck < cs + NA_COLS)
    dc = np.clip(ck - cq, -(NA_COLS - 1), NA_COLS - 1) + (NA_COLS - 1)
    t = jnp.where(ok[None, None], rpb.astype(F32)[:, :, dc], NEG_INF)
    rows = np.arange(NA_ROWS)[:, None] + np.arange(NA_ROWS)[None, :]
    tf = t[:, rows]
    return jnp.transpose(tf, (0, 1, 3, 2, 4)).reshape(NA_HEADS, NA_ROWS, GRID_W, NA_ROWS * GRID_W)


def na_attention(q, k, v, kc, vc, bias):
    b, h, t, dh = q.shape
    lc = kc.shape[2]
    rows = t // GRID_W
    full = lambda b, h, j: (b, h, 0, 0)
    return pl.pallas_call(
        functools.partial(_na_kernel, rows=rows),
        grid=(b, h, rows // NA_RB),
        in_specs=[pl.BlockSpec((1, 1, NA_RB * GRID_W, dh), lambda b, h, j: (b, h, j, 0)),
                  pl.BlockSpec((1, 1, t, dh), full), pl.BlockSpec((1, 1, t, dh), full),
                  pl.BlockSpec((1, 1, lc, dh), full), pl.BlockSpec((1, 1, lc, dh), full),
                  pl.BlockSpec((1, NA_ROWS, GRID_W, NA_ROWS * GRID_W), lambda b, h, j: (h, 0, 0, 0))],
        out_specs=pl.BlockSpec((1, 1, NA_RB * GRID_W, dh), lambda b, h, j: (b, h, j, 0)),
        out_shape=jax.ShapeDtypeStruct((b, h, t, dh), F32),
        compiler_params=_params(("arbitrary", "arbitrary", "arbitrary")),
        name="na_attention",
    )(q, k, v, kc, vc, bias)


def _rms_rows(x, g):
    return x * lax.rsqrt(jnp.mean(x * x, axis=-1, keepdims=True) + NORM_EPS) * g


def _mla_q_kernel(cq_ref, g_ref, w_ref, cos_ref, sin_ref, o_ref):
    r = _dot(_rms_rows(cq_ref[0], g_ref[...]).astype(BF16), w_ref[...])
    nn = MLA_HEADS * MLA_NOPE
    nr = MLA_HEADS * MLA_ROPE
    o_ref[0, :, :nn] = r[:, :nn]
    o_ref[0, :, nn:] = r[:, nn:nn + nr] * cos_ref[0] + r[:, nn + nr:] * sin_ref[0]


def mla_q(p, q_norm, w_q3, cos_q, sin_q, tm=512):
    nseg, seg, _ = p.shape
    nout = MLA_HEADS * (MLA_NOPE + MLA_ROPE)
    nr = MLA_HEADS * MLA_ROPE
    tok = lambda s, i: (s, i, 0)
    return pl.pallas_call(
        _mla_q_kernel,
        grid=(nseg, seg // tm),
        in_specs=[pl.BlockSpec((1, tm, MLA_Q_LORA), tok),
                  pl.BlockSpec((1, MLA_Q_LORA), lambda s, i: (0, 0)),
                  pl.BlockSpec(w_q3.shape, lambda s, i: (0, 0)),
                  pl.BlockSpec((1, tm, nr), tok), pl.BlockSpec((1, tm, nr), tok)],
        out_specs=pl.BlockSpec((1, tm, nout), tok),
        out_shape=jax.ShapeDtypeStruct((nseg, seg, nout), F32),
        compiler_params=_params(("arbitrary", "arbitrary")),
        name="mla_q",
    )(p, q_norm.reshape(1, -1), w_q3, cos_q, sin_q)


def _mla_kv_kernel(ckv_ref, kpe_ref, g_ref, w_ref, cos_ref, sin_ref, ckvn_ref, kpeo_ref, kv_ref):
    cn = _rms_rows(ckv_ref[0], g_ref[...])
    ckvn_ref[0] = cn
    kv_ref[0] = _dot(cn.astype(BF16), w_ref[...])
    kp = kpe_ref[0]
    kpeo_ref[0] = kp[:, :MLA_ROPE] * cos_ref[0] + kp[:, MLA_ROPE:2 * MLA_ROPE] * sin_ref[0]


def mla_kv(p, kv_norm, w_kv, cos_k, sin_k, tm=512):
    nseg, seg, _ = p.shape
    nkv = w_kv.shape[1]
    tok = lambda s, i: (s, i, 0)
    return pl.pallas_call(
        _mla_kv_kernel,
        grid=(nseg, seg // tm),
        in_specs=[pl.BlockSpec((1, tm, MLA_KV_LORA), lambda s, i: (s, i, MLA_Q_LORA // MLA_KV_LORA)),
                  pl.BlockSpec((1, tm, 128), lambda s, i: (s, i, (MLA_Q_LORA + MLA_KV_LORA) // 128)),
                  pl.BlockSpec((1, MLA_KV_LORA), lambda s, i: (0, 0)),
                  pl.BlockSpec(w_kv.shape, lambda s, i: (0, 0)),
                  pl.BlockSpec((1, tm, MLA_ROPE), tok), pl.BlockSpec((1, tm, MLA_ROPE), tok)],
        out_specs=[pl.BlockSpec((1, tm, MLA_KV_LORA), tok), pl.BlockSpec((1, tm, MLA_ROPE), tok),
                   pl.BlockSpec((1, tm, nkv), tok)],
        out_shape=[jax.ShapeDtypeStruct((nseg, seg, MLA_KV_LORA), F32),
                   jax.ShapeDtypeStruct((nseg, seg, MLA_ROPE), F32),
                   jax.ShapeDtypeStruct((nseg, seg, nkv), F32)],
        compiler_params=_params(("arbitrary", "arbitrary")),
        name="mla_kv",
    )(p, p, kv_norm.reshape(1, -1), w_kv, cos_k, sin_k)


def _mm_kernel(a_ref, w_ref, o_ref):
    o_ref[...] = _dot(a_ref[...].astype(BF16), w_ref[...])


def matmul(a, w_bf16, tm):
    m, k = a.shape
    n = w_bf16.shape[1]
    return pl.pallas_call(
        _mm_kernel,
        grid=(m // tm,),
        in_specs=[pl.BlockSpec((tm, k), lambda i: (i, 0)), pl.BlockSpec((k, n), lambda i: (0, 0))],
        out_specs=pl.BlockSpec((tm, n), lambda i: (i, 0)),
        out_shape=jax.ShapeDtypeStruct((m, n), F32),
        compiler_params=_params(("arbitrary",)),
        name="matmul",
    )(a, w_bf16)


PEER_RT = 128
NOT_TOP = 99.0


def _top16(s, exact):
    key = lax.broadcasted_iota(jnp.int32, s.shape, 0).astype(F32)
    rank = jnp.full(s.shape, NOT_TOP, F32)
    vals = []
    for r in range(PEER_TOPK):
        m = jnp.max(s, axis=0, keepdims=True)
        hit = s == m
        if exact:
            hit = key == jnp.min(jnp.where(hit, key, 1e9), axis=0, keepdims=True)
        rank = jnp.where(hit, float(r), rank)
        s = jnp.where(hit, NEG_INF, s)
        vals.append(m)
    return vals, rank


def _pair_topk(av, bv, exact):
    n = av[0].shape[-1]
    a_lo, a_hi = jnp.concatenate(av[:8], 0), jnp.concatenate(av[8:], 0)
    b_lo, b_hi = jnp.concatenate(bv[:8], 0), jnp.concatenate(bv[8:], 0)
    row = lax.broadcasted_iota(jnp.int32, (8, n), 0).astype(F32)

    no_pos = 1e8

    def rows_b(a, b_blk, boff, nvalid):
        ok = row < nvalid
        return jnp.where(ok, av[a] + b_blk, NEG_INF), jnp.where(ok, a * 16.0 + boff + row, no_pos)

    def rows_a(b, a_blk, aoff, lo, hi):
        ok = (row >= lo) & (row < hi)
        return jnp.where(ok, a_blk + bv[b], NEG_INF), jnp.where(ok, (aoff + row) * 16.0 + b, no_pos)

    groups = [rows_b(0, b_lo, 0, 8), rows_b(0, b_hi, 8, 8), rows_b(1, b_lo, 0, 8), rows_b(2, b_lo, 0, 5),
              rows_b(3, b_lo, 0, 4), rows_a(0, a_lo, 0, 4, 8), rows_a(0, a_hi, 8, 0, 8),
              rows_a(1, a_lo, 0, 4, 8), rows_a(2, a_lo, 0, 4, 5)]
    cands = [g[0] for g in groups]
    poss = [g[1] for g in groups]
    sels = [jnp.zeros((8, n), F32) for _ in groups]
    top = av[0] + bv[0]
    z = jnp.zeros((1, n), F32)
    for _ in range(PEER_TOPK):
        m = functools.reduce(jnp.maximum, cands)
        m = jnp.max(m, axis=0, keepdims=True)
        hits = [c == m for c in cands]
        if exact:
            first = functools.reduce(jnp.minimum, [jnp.where(hh, p, 1e9) for hh, p in zip(hits, poss)])
            first = jnp.min(first, axis=0, keepdims=True)
            hits = [p == first for p in poss]
        cands = [jnp.where(hh, NEG_INF, c) for hh, c in zip(hits, cands)]
        sels = [jnp.where(hh, 1.0, s) for hh, s in zip(hits, sels)]
        z = z + jnp.exp(m - top)
    cnt = lambda x: jnp.sum(x, axis=0, keepdims=True)
    cut_lo = sels[5] + sels[7] + sels[8]
    for a, c in enumerate([cnt(sels[0]) + cnt(sels[1]), cnt(sels[2]), cnt(sels[3]), cnt(sels[4])]):
        cut_lo = cut_lo + jnp.where(row == a, c, 0.0)
    return cut_lo, sels[6], z, cnt(cut_lo) + cnt(sels[6])


def _peer_route_kernel(x_ref, sh_ref, sc_ref, wq_ref, sk_ref, xm_ref, e1_ref, cut_ref, e2_ref, r2_ref, q_s, *, tm):
    xm = (x_ref[0] * (1.0 + sc_ref[0]) + sh_ref[0]).astype(BF16)
    xm_ref[0] = xm
    q = _dot(xm, wq_ref[...])
    for hp in range(2 * PEER_HEADS):
        q_s[hp] = q[:, hp * PEER_HALF:(hp + 1) * PEER_HALF]

    def route(h, tok, exact):
        def scores(hp):
            return lax.dot_general(sk_ref[hp], q_s[hp, tok, :], (((1,), (1,)), ((), ())),
                                   precision=lax.Precision.HIGHEST, preferred_element_type=F32)

        s1, s2 = scores(2 * h), scores(2 * h + 1)
        av, rank1 = _top16(s1, exact)
        bv, rank2 = _top16(s2, exact)
        cut_lo, cut_hi, z, nsel = _pair_topk(av, bv, exact)
        cut = jnp.zeros_like(s1)
        for r in range(PEER_TOPK):
            src = cut_lo if r < 8 else cut_hi
            cut = jnp.where(rank1 == float(r), src[r % 8:r % 8 + 1, :], cut)
        e1_ref[0, h, :, tok] = (jnp.exp(s1 - av[0]) / z).astype(BF16)
        cut_ref[0, h, :, tok] = cut.astype(BF16)
        e2_ref[0, h, :, tok] = jnp.exp(s2 - bv[0]).astype(BF16)
        r2_ref[0, h, :, tok] = rank2.astype(BF16)
        ranked = lambda rk: jnp.sum(jnp.where(rk < PEER_TOPK, 1.0, 0.0), axis=0, keepdims=True)
        return ranked(rank1), ranked(rank2), nsel

    def body(h, carry):
        toks = [pl.ds(t0, PEER_RT) for t0 in range(0, tm, PEER_RT)]
        counts = [route(h, tok, exact=False) for tok in toks]
        for tok, cnts in zip(toks, counts):
            bad = functools.reduce(jnp.maximum, [jnp.abs(cn - PEER_TOPK) for cn in cnts])

            @pl.when(jnp.max(bad) > 0.0)
            def _():
                route(h, tok, exact=True)
        return carry

    lax.fori_loop(0, PEER_HEADS, body, 0)


def peer_route(x3, mod3, shift_chunk, wq_bf16, subkeys, tm=256):
    nseg, seg, d = x3.shape
    tok = lambda s, i: (s, i, 0)
    rshape = jax.ShapeDtypeStruct((nseg, PEER_HEADS, PEER_NKEYS, seg), BF16)
    rspec = pl.BlockSpec((1, PEER_HEADS, PEER_NKEYS, tm), lambda s, i: (s, 0, 0, i))
    return pl.pallas_call(
        functools.partial(_peer_route_kernel, tm=tm),
        grid=(nseg, seg // tm),
        in_specs=[pl.BlockSpec((1, tm, d), tok),
                  pl.BlockSpec((1, 1, d), lambda s, i: (s, 0, shift_chunk)),
                  pl.BlockSpec((1, 1, d), lambda s, i: (s, 0, shift_chunk + 1)),
                  pl.BlockSpec(wq_bf16.shape, lambda s, i: (0, 0)),
                  pl.BlockSpec((2 * PEER_HEADS, PEER_NKEYS, PEER_HALF), lambda s, i: (0, 0, 0))],
        out_specs=[pl.BlockSpec((1, tm, d), tok), rspec, rspec, rspec, rspec],
        out_shape=[jax.ShapeDtypeStruct((nseg, seg, d), BF16)] + [rshape] * 4,
        scratch_shapes=[pltpu.VMEM((2 * PEER_HEADS, tm, PEER_HALF), F32)],
        compiler_params=_params(("arbitrary", "arbitrary")),
        name="peer_route",
    )(x3, mod3, mod3, wq_bf16, subkeys.reshape(2 * PEER_HEADS, PEER_NKEYS, PEER_HALF))


PEER_CE = 1024


def _gelu_tanh(x):
    return 0.5 * x * (1.0 + jnp.tanh(0.7978845608028654 * (x + 0.044715 * x * x * x)))


def _peer_dense_kernel(xm_ref, u_ref, vt_ref, e1_ref, cut_ref, e2_ref, r2_ref, x_ref, gate_ref, g_ref, b_ref,
                       o_ref, acc_s, at_s, w_s, e2_s, r2_s, *, tm):
    e = pl.program_id(2)
    nb = PEER_CE // PEER_NKEYS
    ntt = tm // PEER_RT

    @pl.when(e == 0)
    def _():
        acc_s[...] = jnp.zeros_like(acc_s)
        e2_s[:, :, :tm] = e2_ref[0]
        r2_s[:, :, :tm] = r2_ref[0]

    packed = (PEER_NKEYS // 16, 16, PEER_RT)
    ng = 2

    def gate_tiles(tt, i0):
        tok = slice(tt * PEER_RT, (tt + 1) * PEER_RT)
        gmats = [jnp.zeros(packed, BF16) for _ in range(ng)]
        for h in range(PEER_HEADS):
            e2 = e2_s[h, :, tok].reshape(packed)
            r2 = r2_s[h, :, tok].reshape(packed)
            for k in range(ng):
                i = i0 + k
                e1 = jnp.broadcast_to(e1_ref[0, h, i:i + 1, tok], (16, PEER_RT))[None]
                cut = jnp.broadcast_to(cut_ref[0, h, i:i + 1, tok], (16, PEER_RT))[None]
                gmats[k] = gmats[k] + e1 * jnp.where(r2 < cut, e2, jnp.zeros_like(e2))
        for k in range(ng):
            rows = slice((i0 + k) * PEER_NKEYS, (i0 + k + 1) * PEER_NKEYS)
            act = _gelu_tanh(at_s[rows, tok]).astype(BF16)
            w_s[rows, tok] = gmats[k].reshape(PEER_NKEYS, PEER_RT) * act

    at_s[:, :tm] = _dot(u_ref[0], xm_ref[0], ((1,), (1,)))
    for tt in range(ntt):
        for i0 in range(0, nb, ng):
            gate_tiles(tt, i0)
    acc_s[:, :tm] += _dot(vt_ref[0, 0], w_s[:, :tm])

    @pl.when(e == pl.num_programs(2) - 1)
    def _():
        z = DEEPNORM_ALPHA * x_ref[0] + gate_ref[0] * acc_s[:, :tm].T
        o_ref[0] = _layer_norm_rows(z, g_ref[...], b_ref[...])


def peer_dense(xm, u_all, vt_all, l, e1, cut, e2, r2, x3, mod3, gate_chunk, ln_g, ln_b, tm=1024):
    nseg, seg, d = x3.shape
    ne = u_all.shape[1]
    nb = PEER_CE // PEER_NKEYS
    tp = tm + PEER_RT
    tok = lambda s, i, e: (s, i, 0)
    chunk = pl.BlockSpec((1, PEER_HEADS, nb, tm), lambda s, i, e: (s, 0, e, i))
    full = pl.BlockSpec((1, PEER_HEADS, PEER_NKEYS, tm), lambda s, i, e: (s, 0, 0, i))
    return pl.pallas_call(
        functools.partial(_peer_dense_kernel, tm=tm),
        grid=(nseg, seg // tm, ne // PEER_CE),
        in_specs=[pl.BlockSpec((1, tm, d), tok),
                  pl.BlockSpec((1, PEER_CE, d), lambda s, i, e: (l, e, 0)),
                  pl.BlockSpec((1, 1, d, PEER_CE), lambda s, i, e: (l, e, 0, 0)),
                  chunk, chunk, full, full,
                  pl.BlockSpec((1, tm, d), tok),
                  pl.BlockSpec((1, 1, d), lambda s, i, e: (s, 0, gate_chunk)),
                  pl.BlockSpec((1, d), lambda s, i, e: (0, 0)),
                  pl.BlockSpec((1, d), lambda s, i, e: (0, 0))],
        out_specs=pl.BlockSpec((1, tm, d), tok),
        out_shape=jax.ShapeDtypeStruct((nseg, seg, d), F32),
        scratch_shapes=[pltpu.VMEM((d, tp), F32), pltpu.VMEM((PEER_CE, tp), F32), pltpu.VMEM((PEER_CE, tp), BF16),
                        pltpu.VMEM((PEER_HEADS, PEER_NKEYS, tp), BF16), pltpu.VMEM((PEER_HEADS, PEER_NKEYS, tp), BF16)],
        compiler_params=_params(("arbitrary", "arbitrary", "arbitrary")),
        name="peer_dense",
    )(xm, u_all, vt_all, e1, cut, e2, r2, x3, mod3, ln_g.reshape(1, d), ln_b.reshape(1, d))


def peer_layer(x3, mod3, l, wq, subkeys, u_all, vt_all, ln_g, ln_b):
    xm, e1, cut, e2, r2 = peer_route(x3, mod3, 3, wq.astype(BF16), subkeys)
    return peer_dense(xm, u_all, vt_all, l, e1, cut, e2, r2, x3, mod3, 5, ln_g, ln_b)


def _pad_cols(w, n):
    return jnp.pad(w, ((0, 0), (0, n - w.shape[1])))


def _stream(prompt_part, sample_part):
    return jnp.concatenate([prompt_part.reshape(1, -1, prompt_part.shape[-1]), sample_part], axis=0)


def _head_major(a, heads):
    b, t, _ = a.shape
    return jnp.transpose(a.reshape(b, t, heads, -1), (0, 2, 1, 3))


def _token_major(a):
    b, h, t, dh = a.shape
    return jnp.transpose(a, (0, 2, 1, 3)).reshape(b, t, h * dh)


MLSTM_CHUNK = 128
GLA_CHUNK = 32
NPROJ = 3200


def mlstm_layer(x3, mod3, bp, lp, st_c, st_n, st_m, w_in, b_gate, norm_w, w_out, ln_g, ln_b):
    nseg, seg, _ = x3.shape
    bs = nseg - 1
    p = mod_matmul(x3, mod3, 0, _pad_cols(w_in, NPROJ).astype(BF16))
    graw = p[:, :, 3072:3088]
    gp = graw[0].reshape(bp, lp, 16)
    zc = jnp.zeros((bp, 8, M_DK, M_DV), F32)
    zn = jnp.zeros((bp, 8, M_DK), F32)
    hfp, hbp, c_new, n_new, m_new = mlstm_scan(p.reshape(nseg * bp, lp, NPROJ), 0, bp, lp, gp,
                                               jnp.swapaxes(gp, 1, 2), b_gate, zc, zn, zn, min(MLSTM_CHUNK, lp))
    gs = graw[1:]
    hfs, hbs, _, _, _ = mlstm_scan(p, 1, bs, seg, gs, jnp.swapaxes(gs, 1, 2), b_gate,
                                   st_c.reshape(bs, 8, M_DK, M_DV), st_n.reshape(bs, 8, M_DK),
                                   jnp.broadcast_to(st_m.reshape(bs, 8, 1), (bs, 8, M_DK)), MLSTM_CHUNK)
    x3 = outproj_ln("mlstm", (_stream(hfp, hfs), _stream(hbp, hbs)), x3, mod3, 2, w_out.astype(BF16), ln_g, ln_b,
                    norm_w=norm_w, og=p, og_col=2)
    return (x3, c_new.reshape(bp, 2, M_HEADS, M_DK, M_DV), n_new.reshape(bp, 2, M_HEADS, M_DK),
            m_new[:, :, 0].reshape(bp, 2, M_HEADS))


def gla_layer(x3, mod3, bp, lp, st_s, w_in, w_gate2, b_gate2, norm_w, w_out, ln_g, ln_b):
    nseg, seg, _ = x3.shape
    bs = nseg - 1
    p = mod_matmul(x3, mod3, 0, _pad_cols(w_in, NPROJ).astype(BF16))
    gr = p[:, :, 3072:3104]
    zs = jnp.zeros((bp, 8, G_DV, G_DK), F32)
    ofp, obp, s_new = gla_scan(p.reshape(nseg * bp, lp, NPROJ), 0, bp, lp, gr[0].reshape(bp, lp, 32),
                               w_gate2, b_gate2, zs, GLA_CHUNK)
    s0t = jnp.swapaxes(st_s.reshape(bs, 8, G_DK, G_DV), -1, -2)
    ofs, obs, _ = gla_scan(p, 1, bs, seg, gr[1:], w_gate2, b_gate2, s0t, GLA_CHUNK)
    x3 = outproj_ln("gla", (_stream(ofp, ofs), _stream(obp, obs)), x3, mod3, 2, w_out.astype(BF16), ln_g, ln_b,
                    norm_w=jnp.tile(norm_w, G_HEADS), og=p, og_col=2)
    return x3, jnp.swapaxes(s_new, -1, -2).reshape(bp, 2, G_HEADS, G_DK, G_DV)


def na_layer(x3, mod3, bp, lp, cache_k, cache_v, w_in, rpb, w_out, ln_g, ln_b):
    nseg, seg, _ = x3.shape
    bs = nseg - 1
    hd = NA_HEADS * NA_HD
    p = mod_matmul(x3, mod3, 0, w_in.astype(BF16))
    pp = p[0].reshape(bp, lp, 3 * hd)
    hm = lambda a: _head_major(a, NA_HEADS).astype(BF16)
    yp = attention(hm(pp[..., :hd]), hm(pp[..., hd:2 * hd]), hm(pp[..., 2 * hd:]), lp)
    ps = p[1:]
    ys = na_attention(hm(ps[..., :hd]), hm(ps[..., hd:2 * hd]), hm(ps[..., 2 * hd:]),
                      hm(cache_k.reshape(bs, -1, hd)), hm(cache_v.reshape(bs, -1, hd)), na_bias_table(rpb))
    x3 = outproj_ln("plain", _stream(_token_major(yp), _token_major(ys)), x3, mod3, 2, w_out.astype(BF16), ln_g, ln_b)
    return (x3, pp[..., hd:2 * hd].reshape(bp, lp, NA_HEADS, NA_HD), pp[..., 2 * hd:].reshape(bp, lp, NA_HEADS, NA_HD))


def _rope_rotated_cols(w):
    q = MLA_ROPE // 4
    return jnp.concatenate([-w[..., q:2 * q], w[..., :q], -w[..., 3 * q:], w[..., 2 * q:3 * q]], axis=-1)


def _rope_tables(ts):
    ra = MLA_ROPE // 2
    t = np.arange(ts)
    inv = 1.0 / (ROPE_BASE ** (np.arange(0, ra, 2, dtype=np.float32) / ra))
    ang_r = (t // GRID_W).astype(np.float32)[:, None] * inv[None, :]
    ang_c = (t % GRID_W).astype(np.float32)[:, None] * inv[None, :]
    ang = np.concatenate([ang_r, ang_r, ang_c, ang_c], axis=-1).astype(np.float32)
    return jnp.cos(jnp.asarray(ang)), jnp.sin(jnp.asarray(ang))


def mla_layer(x3, mod3, bp, lp, cache_ckv, cache_kpe, w_in, q_norm, w_qup, kv_norm, w_kvup, w_out, ln_g, ln_b):
    nseg, seg, _ = x3.shape
    bs = nseg - 1
    nq = MLA_Q_LORA + MLA_KV_LORA
    w_ext = jnp.concatenate([w_in, _rope_rotated_cols(w_in[:, nq:])], axis=1)
    p = mod_matmul(x3, mod3, 0, _pad_cols(w_ext, 896).astype(BF16))
    cos_t, sin_t = _rope_tables(seg)
    cos3 = jnp.concatenate([jnp.ones((1, seg, MLA_ROPE), F32), jnp.broadcast_to(cos_t, (bs, seg, MLA_ROPE))], 0)
    sin3 = jnp.concatenate([jnp.zeros((1, seg, MLA_ROPE), F32), jnp.broadcast_to(sin_t, (bs, seg, MLA_ROPE))], 0)
    wq = w_qup.reshape(MLA_Q_LORA, MLA_HEADS, MLA_NOPE + MLA_ROPE)
    wq_rope = wq[:, :, MLA_NOPE:]
    w_q3 = jnp.concatenate([wq[:, :, :MLA_NOPE].reshape(MLA_Q_LORA, -1), wq_rope.reshape(MLA_Q_LORA, -1),
                            _rope_rotated_cols(wq_rope).reshape(MLA_Q_LORA, -1)], axis=1).astype(BF16)
    q_all = mla_q(p, q_norm, w_q3, jnp.tile(cos3, (1, 1, MLA_HEADS)), jnp.tile(sin3, (1, 1, MLA_HEADS)))
    wkv = w_kvup.reshape(MLA_KV_LORA, MLA_HEADS, MLA_NOPE + MLA_VD)
    w_kv2 = jnp.concatenate([wkv[:, :, :MLA_NOPE].reshape(MLA_KV_LORA, -1),
                             wkv[:, :, MLA_NOPE:].reshape(MLA_KV_LORA, -1)], axis=1).astype(BF16)
    ckvn, kpe, kv = mla_kv(p, kv_norm, w_kv2, cos3, sin3)
    kvc = matmul(cache_ckv.reshape(-1, MLA_KV_LORA), w_kv2, 512).reshape(bs, -1, w_kv2.shape[1])
    nn = MLA_HEADS * MLA_NOPE

    def heads(q_rows, kv_rows, kpe_rows):
        b, t, _ = q_rows.shape
        tk = kv_rows.shape[1]
        qh = jnp.concatenate([q_rows[..., :nn].reshape(b, t, MLA_HEADS, MLA_NOPE),
                              q_rows[..., nn:].reshape(b, t, MLA_HEADS, MLA_ROPE)], -1)
        kh = jnp.concatenate([kv_rows[..., :nn].reshape(b, tk, MLA_HEADS, MLA_NOPE),
                              jnp.broadcast_to(kpe_rows[:, :, None, :], (b, tk, MLA_HEADS, MLA_ROPE))], -1)
        vh = kv_rows[..., nn:].reshape(b, tk, MLA_HEADS, MLA_VD)
        tr = lambda a: jnp.transpose(a, (0, 2, 1, 3)).astype(BF16)
        return tr(qh), tr(kh), tr(vh)

    yp = attention(*heads(q_all[0].reshape(bp, lp, -1), kv[0].reshape(bp, lp, -1), kpe[0].reshape(bp, lp, -1)), lp)
    ys = attention(*heads(q_all[1:], jnp.concatenate([kv[1:], kvc], 1), jnp.concatenate([kpe[1:], cache_kpe], 1)), 256)
    x3 = outproj_ln("plain", _stream(_token_major(yp), _token_major(ys)), x3, mod3, 2, w_out.astype(BF16), ln_g, ln_b)
    return x3, ckvn[0].reshape(bp, lp, MLA_KV_LORA), kpe[0].reshape(bp, lp, MLA_ROPE)


def kernel(x_prompt, x_sample, c, c_ctx, state_mlstm_C, state_mlstm_n, state_mlstm_m, state_gla_S, cache_na_k, cache_na_v, cache_mla_ckv, cache_mla_kpe, ada_w, ada_b, ln_mix_g, ln_mix_b, ln_ffn_g, ln_ffn_b, mlstm_w_in, mlstm_b_gate, mlstm_norm_w, mlstm_w_out, gla_w_in, gla_w_gate2, gla_b_gate2, gla_norm_w, gla_w_out, na_w_in, na_rpb, na_w_out, mla_w_in, mla_q_norm, mla_w_qup, mla_kv_norm, mla_w_kvup, mla_w_out, peer_w_q, peer_subkeys, peer_u, peer_v):
    bp, lp, d = x_prompt.shape
    bs, ts, _ = x_sample.shape
    assert bp * lp == ts and bs + 1 <= 8
    x3 = _stream(x_prompt, x_sample)
    cond8 = jnp.zeros((8, d), F32).at[0].set(c_ctx).at[1:1 + bs].set(c)
    mods = adaln_all(cond8, ada_w, ada_b)
    u_all = peer_u.astype(BF16)
    vt_all = jnp.swapaxes(peer_v.reshape(DEPTH, -1, PEER_CE, d), 2, 3).astype(BF16)
    outs = {}
    for l in range(DEPTH):
        mod3 = mods[l].reshape(8, 1, ADA_CHUNKS * d)
        kind = l % 4
        if kind == 0:
            x3, outs["C"], outs["n"], outs["m"] = mlstm_layer(
                x3, mod3, bp, lp, state_mlstm_C, state_mlstm_n, state_mlstm_m, mlstm_w_in, mlstm_b_gate,
                mlstm_norm_w, mlstm_w_out, ln_mix_g[l], ln_mix_b[l])
        elif kind == 1:
            x3, outs["S"] = gla_layer(x3, mod3, bp, lp, state_gla_S, gla_w_in, gla_w_gate2, gla_b_gate2,
                                      gla_norm_w, gla_w_out, ln_mix_g[l], ln_mix_b[l])
        elif kind == 2:
            x3, outs["nk"], outs["nv"] = na_layer(x3, mod3, bp, lp, cache_na_k, cache_na_v, na_w_in, na_rpb,
                                                  na_w_out, ln_mix_g[l], ln_mix_b[l])
        else:
            x3, outs["ckv"], outs["kpe"] = mla_layer(x3, mod3, bp, lp, cache_mla_ckv, cache_mla_kpe, mla_w_in,
                                                     mla_q_norm, mla_w_qup, mla_kv_norm, mla_w_kvup, mla_w_out,
                                                     ln_mix_g[l], ln_mix_b[l])
        x3 = peer_layer(x3, mod3, l, peer_w_q[l], peer_subkeys[l], u_all, vt_all, ln_ffn_g[l], ln_ffn_b[l])
    return (x3[0].reshape(bp, lp, d), x3[1:], outs["C"], outs["n"], outs["m"], outs["S"], outs["nk"], outs["nv"],
            outs["ckv"], outs["kpe"])
```

```python
import functools

import numpy as np
import jax
import jax.numpy as jnp
from jax import lax
from jax.experimental import pallas as pl
from jax.experimental.pallas import tpu as pltpu

D_MODEL = 1024
DEPTH = 4
GRID_W = 64
DEEPNORM_ALPHA = (2.0 * DEPTH) ** 0.25
ADA_CHUNKS = 6
NORM_EPS = 1e-5
SEG = 4096
NSEG = 3

M_HEADS, M_DK, M_DV = 4, 128, 256
G_HEADS, G_DK, G_DV = 4, 128, 256
G_GATE_RANK = 16
G_GATE_NORM = 16.0
NA_HEADS, NA_HD, NA_ROWS, NA_COLS = 16, 64, 8, 16
MLA_HEADS, MLA_Q_LORA, MLA_KV_LORA, MLA_NOPE, MLA_ROPE, MLA_VD = 16, 512, 256, 64, 32, 64
ROPE_BASE = 10000.0
PEER_HEADS, PEER_NKEYS, PEER_HALF, PEER_TOPK = 8, 128, 128, 16

V7X_VMEM_LIMIT = 56 * 1024 * 1024
F32 = jnp.float32
BF16 = jnp.bfloat16
NEG_INF = float("-inf")


def _params(sem, vmem=V7X_VMEM_LIMIT):
    return pltpu.CompilerParams(dimension_semantics=sem, vmem_limit_bytes=vmem)


def _dot(a, b, dims=((1,), (0,))):
    return lax.dot_general(a, b, (dims, ((), ())), preferred_element_type=F32)


def _split3(a):
    hi = a.astype(BF16)
    r1 = a - hi.astype(F32)
    mid = r1.astype(BF16)
    lo = (r1 - mid.astype(F32)).astype(BF16)
    return hi, mid, lo


def _dot_exact_lhs(m01, a):
    hi, mid, lo = _split3(a)
    return _dot(m01, hi) + _dot(m01, mid) + _dot(m01, lo)


def _dot_exact_rhs(a, m01):
    hi, mid, lo = _split3(a)
    return _dot(hi, m01) + _dot(mid, m01) + _dot(lo, m01)


def _log_sigmoid(x):
    return jnp.minimum(x, 0.0) - jnp.log(1.0 + jnp.exp(-jnp.abs(x)))


def _sigmoid(x):
    return 1.0 / (1.0 + jnp.exp(-x))


def _adaln_kernel(c_ref, w_ref, b_ref, o_ref):
    cv = c_ref[...]
    a = cv * _sigmoid(cv)
    o_ref[0] = lax.dot_general(a, w_ref[0], (((1,), (0,)), ((), ())), precision=lax.Precision.HIGHEST,
                               preferred_element_type=F32) + b_ref[0]


def adaln_all(cond8, ada_w, ada_b):
    tn = 1024
    n = ada_w.shape[-1]
    return pl.pallas_call(
        _adaln_kernel,
        grid=(DEPTH, n // tn),
        in_specs=[pl.BlockSpec((8, D_MODEL), lambda l, j: (0, 0)),
                  pl.BlockSpec((1, D_MODEL, tn), lambda l, j: (l, 0, j)),
                  pl.BlockSpec((1, 1, tn), lambda l, j: (l, 0, j))],
        out_specs=pl.BlockSpec((1, 8, tn), lambda l, j: (l, 0, j)),
        out_shape=jax.ShapeDtypeStruct((DEPTH, 8, n), F32),
        compiler_params=_params(("arbitrary", "arbitrary")),
        name="adaln",
    )(cond8, ada_w, ada_b.reshape(DEPTH, 1, n))


def _modmm_kernel(x_ref, sh_ref, sc_ref, w_ref, o_ref, xm_ref):
    @pl.when(pl.program_id(2) == 0)
    def _():
        xm_ref[...] = (x_ref[0] * (1.0 + sc_ref[0]) + sh_ref[0]).astype(BF16)

    o_ref[0] = _dot(xm_ref[...], w_ref[...]).astype(o_ref.dtype)


def mod_matmul(x3, mod3, shift_chunk, w_bf16, tm=512, tn=None, out_dtype=F32):
    nseg, seg, d = x3.shape
    n = w_bf16.shape[1]
    tn = n if tn is None else tn
    return pl.pallas_call(
        _modmm_kernel,
        grid=(nseg, seg // tm, n // tn),
        in_specs=[pl.BlockSpec((1, tm, d), lambda s, i, j: (s, i, 0)),
                  pl.BlockSpec((1, 1, d), lambda s, i, j: (s, 0, shift_chunk)),
                  pl.BlockSpec((1, 1, d), lambda s, i, j: (s, 0, shift_chunk + 1)),
                  pl.BlockSpec((d, tn), lambda s, i, j: (0, j))],
        out_specs=pl.BlockSpec((1, tm, tn), lambda s, i, j: (s, i, j)),
        out_shape=jax.ShapeDtypeStruct((nseg, seg, n), out_dtype),
        scratch_shapes=[pltpu.VMEM((tm, d), BF16)],
        compiler_params=_params(("arbitrary", "arbitrary", "arbitrary")),
        name="mod_matmul",
    )(x3, mod3, mod3, w_bf16)


def _layer_norm_rows(y, g, b):
    mu = jnp.mean(y, axis=-1, keepdims=True)
    yc = y - mu
    var = jnp.mean(yc * yc, axis=-1, keepdims=True)
    return yc * lax.rsqrt(var + NORM_EPS) * g + b


def _outproj_kernel(*refs, mode):
    if mode == "plain":
        y_ref, x_ref, gate_ref, w_ref, g_ref, b_ref, o_ref = refs
        yin = y_ref[0].astype(BF16)
    else:
        ya_ref, yb_ref, og_ref, nw_ref, x_ref, gate_ref, w_ref, g_ref, b_ref, o_ref = refs
        hs = ya_ref[0] + yb_ref[0]
        og = og_ref[0]
        parts = []
        for h in range(4):
            seg = hs[:, h * 256:(h + 1) * 256]
            nw = nw_ref[:, h * 256:(h + 1) * 256]
            if mode == "mlstm":
                mu = jnp.mean(seg, axis=-1, keepdims=True)
                sc = seg - mu
                var = jnp.mean(sc * sc, axis=-1, keepdims=True)
                parts.append(sc * lax.rsqrt(var + NORM_EPS) * nw)
            else:
                ms = jnp.mean(seg * seg, axis=-1, keepdims=True)
                parts.append(seg * lax.rsqrt(ms + NORM_EPS) * nw)
        hn = jnp.concatenate(parts, axis=-1)
        act = _sigmoid(og) if mode == "mlstm" else og * _sigmoid(og)
        yin = (act * hn).astype(BF16)
    y = _dot(yin, w_ref[...])
    z = DEEPNORM_ALPHA * x_ref[0] + gate_ref[0] * y
    o_ref[0] = _layer_norm_rows(z, g_ref[...], b_ref[...])


def outproj_ln(mode, ys, x3, mod3, gate_chunk, w_bf16, ln_g, ln_b, norm_w=None, og=None, og_col=0, tm=512):
    nseg, seg, d = x3.shape
    k = w_bf16.shape[0]
    tok = lambda s, i: (s, i, 0)
    if mode == "plain":
        args = [ys]
        specs = [pl.BlockSpec((1, tm, k), tok)]
    else:
        args = [ys[0], ys[1], og, norm_w.reshape(1, k)]
        specs = [pl.BlockSpec((1, tm, k), tok), pl.BlockSpec((1, tm, k), tok),
                 pl.BlockSpec((1, tm, k), lambda s, i: (s, i, og_col)),
                 pl.BlockSpec((1, k), lambda s, i: (0, 0))]
    args += [x3, mod3, w_bf16, ln_g.reshape(1, d), ln_b.reshape(1, d)]
    specs += [pl.BlockSpec((1, tm, d), tok),
              pl.BlockSpec((1, 1, d), lambda s, i: (s, 0, gate_chunk)),
              pl.BlockSpec((k, d), lambda s, i: (0, 0)),
              pl.BlockSpec((1, d), lambda s, i: (0, 0)),
              pl.BlockSpec((1, d), lambda s, i: (0, 0))]
    return pl.pallas_call(
        functools.partial(_outproj_kernel, mode=mode),
        grid=(nseg, seg // tm),
        in_specs=specs,
        out_specs=pl.BlockSpec((1, tm, d), tok),
        out_shape=jax.ShapeDtypeStruct((nseg, seg, d), F32),
        compiler_params=_params(("arbitrary", "arbitrary")),
        name="outproj_ln_" + mode,
    )(*args)


def _tri(n, lower):
    r = lax.broadcasted_iota(jnp.int32, (n, n), 0)
    c = lax.broadcasted_iota(jnp.int32, (n, n), 1)
    return (c <= r) if lower else (c >= r)


def _mlstm_kernel(pf_ref, pb_ref, gf_ref, gb_ref, gtf_ref, gtb_ref, bias_ref, biast_ref,
                  c0_ref, n0_ref, m0_ref, hf_ref, hb_ref, co_ref, no_ref, mo_ref,
                  c_s, n_s, m_s, *, L):
    c = pl.program_id(1)

    @pl.when(c == 0)
    def _():
        c_s[...] = c0_ref[0]
        n_s[...] = n0_ref[0]
        m_s[...] = m0_ref[0]

    for d in range(2):
        p_ref, g_ref, gt_ref, h_ref = ((pf_ref, gf_ref, gtf_ref, hf_ref) if d == 0
                                       else (pb_ref, gb_ref, gtb_ref, hb_ref))
        mask = _tri(L, lower=(d == 0))
        mcol = mask.astype(BF16)
        mrow = _tri(L, lower=(d != 0)).astype(BF16)
        g = g_ref[0] + bias_ref[...]
        gt = gt_ref[0] + biast_ref[...]
        li_c = g[:, d * 8:d * 8 + 4]
        lf_c = _log_sigmoid(g[:, d * 8 + 4:d * 8 + 8])
        li_r = gt[d * 8:d * 8 + 4, :]
        lf_r = _log_sigmoid(gt[d * 8 + 4:d * 8 + 8, :])
        b_c = _dot_exact_lhs(mcol, lf_c)
        b_r = _dot_exact_rhs(lf_r, mrow)
        last = L - 1 if d == 0 else 0
        for h in range(M_HEADS):
            u = d * M_HEADS + h
            q = p_ref[0, :, h * M_DK:(h + 1) * M_DK]
            k = p_ref[0, :, 512 + h * M_DK:512 + (h + 1) * M_DK] * (M_DK ** -0.5)
            v = p_ref[0, :, 1024 + h * M_DV:1024 + (h + 1) * M_DV].astype(BF16)
            qb = q.astype(BF16)
            bc, br = b_c[:, h:h + 1], b_r[h:h + 1, :]
            lic, lir = li_c[:, h:h + 1], li_r[h:h + 1, :]
            m_prev = m_s[u:u + 1, 0:1]
            dmat = jnp.where(mask, bc - br + lir, NEG_INF)
            inter = bc + m_prev
            mt = jnp.maximum(inter, jnp.max(dmat, axis=-1, keepdims=True))
            smat = _dot(qb, k.astype(BF16), ((1,), (1,))) * jnp.exp(dmat - mt)
            ei = jnp.exp(inter - mt)
            cmat = c_s[u]
            num = _dot(smat.astype(BF16), v) + ei * _dot(qb, cmat.astype(BF16))
            nrow = n_s[u:u + 1, :]
            den = jnp.sum(smat, axis=-1, keepdims=True) + ei * jnp.sum(q * nrow, axis=-1, keepdims=True)
            h_ref[0, :, h * M_DV:(h + 1) * M_DV] = num / jnp.maximum(jnp.abs(den), jnp.exp(-mt))
            tot = br[:, last:last + 1]
            g_c = tot - bc + lic
            g_r = tot - br + lir
            m_new = jnp.maximum(tot + m_prev, jnp.max(g_r, axis=-1, keepdims=True))
            kw = k * jnp.exp(g_c - m_new)
            dec = jnp.exp(tot + m_prev - m_new)
            c_s[u] = dec * cmat + _dot(kw.astype(BF16), v, ((0,), (0,)))
            n_s[u:u + 1, :] = dec * nrow + jnp.sum(kw, axis=0, keepdims=True)
            m_s[u:u + 1, :] = jnp.broadcast_to(m_new, (1, 128))

    @pl.when(c == pl.num_programs(1) - 1)
    def _():
        co_ref[0] = c_s[...]
        no_ref[0] = n_s[...]
        mo_ref[0] = m_s[...]


def _skip_refs(body, start, n):
    def kernel(*refs):
        return body(*refs[:start], *refs[start + n:])
    return kernel


def _stream_out_args(out_rows, t, width, prev):
    shape = jax.ShapeDtypeStruct((out_rows, t, width), F32)
    if prev is None:
        return shape, [], []
    return shape, [a.reshape(out_rows, t, width) for a in prev], [pl.BlockSpec(memory_space=pl.ANY)] * len(prev)


def mlstm_scan(p, b0, nb, t, g, gt, bias, c0, n0, m0, L, out_rows, ob0, prev=None):
    nc = t // L
    hshape, prev_args, prev_specs = _stream_out_args(out_rows, t, M_HEADS * M_DV, prev)
    n_in = 11
    fwd = lambda b, c: (b + b0, c, 0)
    bwd = lambda b, c: (b + b0, nc - 1 - c, 0)
    st4 = lambda b, c: (b, 0, 0, 0)
    st3 = lambda b, c: (b, 0, 0)
    return pl.pallas_call(
        _skip_refs(functools.partial(_mlstm_kernel, L=L), n_in, len(prev_args)),
        grid=(nb, nc),
        input_output_aliases={n_in + i: i for i in range(len(prev_args))},
        in_specs=[pl.BlockSpec((1, L, 2048), fwd), pl.BlockSpec((1, L, 2048), bwd),
                  pl.BlockSpec((1, L, 16), lambda b, c: (b, c, 0)),
                  pl.BlockSpec((1, L, 16), lambda b, c: (b, nc - 1 - c, 0)),
                  pl.BlockSpec((1, 16, L), lambda b, c: (b, 0, c)),
                  pl.BlockSpec((1, 16, L), lambda b, c: (b, 0, nc - 1 - c)),
                  pl.BlockSpec((1, 16), lambda b, c: (0, 0)),
                  pl.BlockSpec((16, 1), lambda b, c: (0, 0)),
                  pl.BlockSpec((1, 8, M_DK, M_DV), st4),
                  pl.BlockSpec((1, 8, M_DK), st3),
                  pl.BlockSpec((1, 8, M_DK), st3)] + prev_specs,
        out_specs=[pl.BlockSpec((1, L, 1024), lambda b, c: (b + ob0, c, 0)),
                   pl.BlockSpec((1, L, 1024), lambda b, c: (b + ob0, nc - 1 - c, 0)),
                   pl.BlockSpec((1, 8, M_DK, M_DV), st4),
                   pl.BlockSpec((1, 8, M_DK), st3),
                   pl.BlockSpec((1, 8, M_DK), st3)],
        out_shape=[hshape, hshape,
                   jax.ShapeDtypeStruct((nb, 8, M_DK, M_DV), F32),
                   jax.ShapeDtypeStruct((nb, 8, M_DK), F32),
                   jax.ShapeDtypeStruct((nb, 8, M_DK), F32)],
        scratch_shapes=[pltpu.VMEM((8, M_DK, M_DV), F32), pltpu.VMEM((8, M_DK), F32),
                        pltpu.VMEM((8, M_DK), F32)],
        compiler_params=_params(("arbitrary", "arbitrary")),
        name="mlstm_scan",
    )(p, p, g, g, gt, gt, bias.reshape(1, 16), bias.reshape(16, 1), c0, n0, m0, *prev_args)


def _gla_kernel(pf_ref, pb_ref, gf_ref, gb_ref, w2_ref, b2_ref, s0_ref, of_ref, ob_ref, so_ref, s_s, *, L):
    c = pl.program_id(1)

    @pl.when(c == 0)
    def _():
        s_s[...] = s0_ref[0]

    for d in range(2):
        p_ref, g_ref, o_ref = (pf_ref, gf_ref, of_ref) if d == 0 else (pb_ref, gb_ref, ob_ref)
        mask = _tri(L, lower=(d == 0))
        mcol = mask.astype(BF16)
        gr = g_ref[0][:, d * G_GATE_RANK:(d + 1) * G_GATE_RANK]
        pre = lax.dot_general(gr, w2_ref[d], (((1,), (0,)), ((), ())), precision=lax.Precision.HIGHEST,
                              preferred_element_type=F32) + b2_ref[d]
        la = _log_sigmoid(pre) * (1.0 / G_GATE_NORM)
        bc_all = _dot_exact_lhs(mcol, la)
        last = L - 1 if d == 0 else 0
        for h in range(G_HEADS):
            u = d * G_HEADS + h
            q = p_ref[0, :, h * G_DK:(h + 1) * G_DK] * (G_DK ** -0.5)
            k = p_ref[0, :, 512 + h * G_DK:512 + (h + 1) * G_DK]
            v = p_ref[0, :, 1024 + h * G_DV:1024 + (h + 1) * G_DV].astype(BF16)
            bc = bc_all[:, h * G_DK:(h + 1) * G_DK]
            qd = (q * jnp.exp(bc)).astype(BF16)
            kd = (k * jnp.exp(-bc)).astype(BF16)
            a = jnp.where(mask, _dot(qd, kd, ((1,), (1,))), 0.0)
            st = s_s[u]
            o_ref[0, :, h * G_DV:(h + 1) * G_DV] = (_dot(a.astype(BF16), v)
                                                    + _dot(qd, st.astype(BF16), ((1,), (1,))))
            bl = bc[last:last + 1, :]
            kl = (k * jnp.exp(bl - bc)).astype(BF16)
            s_s[u] = st * jnp.exp(bl) + _dot(v, kl, ((0,), (0,)))

    @pl.when(c == pl.num_programs(1) - 1)
    def _():
        so_ref[0] = s_s[...]


def gla_scan(p, b0, nb, t, gr, w2, b2, s0t, L, out_rows, ob0, prev=None):
    nc = t // L
    oshape, prev_args, prev_specs = _stream_out_args(out_rows, t, G_HEADS * G_DV, prev)
    n_in = 7
    st4 = lambda b, c: (b, 0, 0, 0)
    return pl.pallas_call(
        _skip_refs(functools.partial(_gla_kernel, L=L), n_in, len(prev_args)),
        grid=(nb, nc),
        input_output_aliases={n_in + i: i for i in range(len(prev_args))},
        in_specs=[pl.BlockSpec((1, L, 2048), lambda b, c: (b + b0, c, 0)),
                  pl.BlockSpec((1, L, 2048), lambda b, c: (b + b0, nc - 1 - c, 0)),
                  pl.BlockSpec((1, L, 32), lambda b, c: (b, c, 0)),
                  pl.BlockSpec((1, L, 32), lambda b, c: (b, nc - 1 - c, 0)),
                  pl.BlockSpec((2, G_GATE_RANK, 512), lambda b, c: (0, 0, 0)),
                  pl.BlockSpec((2, 1, 512), lambda b, c: (0, 0, 0)),
                  pl.BlockSpec((1, 8, G_DV, G_DK), st4)] + prev_specs,
        out_specs=[pl.BlockSpec((1, L, 1024), lambda b, c: (b + ob0, c, 0)),
                   pl.BlockSpec((1, L, 1024), lambda b, c: (b + ob0, nc - 1 - c, 0)),
                   pl.BlockSpec((1, 8, G_DV, G_DK), st4)],
        out_shape=[oshape, oshape, jax.ShapeDtypeStruct((nb, 8, G_DV, G_DK), F32)],
        scratch_shapes=[pltpu.VMEM((8, G_DV, G_DK), F32)],
        compiler_params=_params(("arbitrary", "arbitrary")),
        name="gla_scan",
    )(p, p, gr, gr, w2, b2.reshape(2, 1, 512), s0t, *prev_args)


def _attn_kernel(q_ref, k_ref, v_ref, o_ref, *, scale):
    s = _dot(q_ref[0, 0], k_ref[0, 0], ((1,), (1,))) * scale
    m = jnp.max(s, axis=-1, keepdims=True)
    p = jnp.exp(s - m)
    l = jnp.sum(p, axis=-1, keepdims=True)
    o_ref[0, 0] = _dot(p.astype(BF16), v_ref[0, 0]) / l


def attention(q, k, v, tq):
    b, h, lq, dq = q.shape
    lk, dv = k.shape[2], v.shape[3]
    return pl.pallas_call(
        functools.partial(_attn_kernel, scale=dq ** -0.5),
        grid=(b, h, lq // tq),
        in_specs=[pl.BlockSpec((1, 1, tq, dq), lambda b, h, i: (b, h, i, 0)),
                  pl.BlockSpec((1, 1, lk, dq), lambda b, h, i: (b, h, 0, 0)),
                  pl.BlockSpec((1, 1, lk, dv), lambda b, h, i: (b, h, 0, 0))],
        out_specs=pl.BlockSpec((1, 1, tq, dv), lambda b, h, i: (b, h, i, 0)),
        out_shape=jax.ShapeDtypeStruct((b, h, lq, dv), F32),
        compiler_params=_params(("arbitrary", "arbitrary", "arbitrary")),
        name="attention",
    )(q, k, v)


NA_RB = 8


def _na_kernel(q_ref, k_ref, v_ref, kc_ref, vc_ref, bias_ref, o_ref, *, rows):
    j = pl.program_id(2)
    scale = NA_HD ** -0.5
    kc, vc = kc_ref[0, 0], vc_ref[0, 0]
    for a in range(NA_RB):
        r = j * NA_RB + a
        start = jnp.clip(r - NA_ROWS // 2, 0, rows - NA_ROWS)
        dr0 = start - r + (NA_ROWS - 1)
        off = pl.multiple_of(start * GRID_W, GRID_W)
        qa = q_ref[0, 0, a * GRID_W:(a + 1) * GRID_W, :]
        kl = k_ref[0, 0, pl.ds(off, NA_ROWS * GRID_W), :]
        vl = v_ref[0, 0, pl.ds(off, NA_ROWS * GRID_W), :]
        s_loc = _dot(qa, kl, ((1,), (1,))) * scale + bias_ref[0, dr0]
        s_ctx = _dot(qa, kc, ((1,), (1,))) * scale
        m = jnp.maximum(jnp.max(s_loc, axis=-1, keepdims=True), jnp.max(s_ctx, axis=-1, keepdims=True))
        p_loc = jnp.exp(s_loc - m)
        p_ctx = jnp.exp(s_ctx - m)
        l = jnp.sum(p_loc, axis=-1, keepdims=True) + jnp.sum(p_ctx, axis=-1, keepdims=True)
        o = _dot(p_loc.astype(BF16), vl) + _dot(p_ctx.astype(BF16), vc)
        o_ref[0, 0, a * GRID_W:(a + 1) * GRID_W, :] = o / l


def na_bias_table(rpb):
    cq = np.arange(GRID_W)[:, None]
    ck = np.arange(GRID_W)[None, :]
    cs = np.clip(cq - NA_COLS // 2, 0, GRID_W - NA_COLS)
    ok = (ck >= cs) & (ck < cs + NA_COLS)
    dc = np.clip(ck - cq, -(NA_COLS - 1), NA_COLS - 1) + (NA_COLS - 1)
    t = jnp.where(ok[None, None], rpb.astype(F32)[:, :, dc], NEG_INF)
    rows = np.arange(NA_ROWS)[:, None] + np.arange(NA_ROWS)[None, :]
    tf = t[:, rows]
    return jnp.transpose(tf, (0, 1, 3, 2, 4)).reshape(NA_HEADS, NA_ROWS, GRID_W, NA_ROWS * GRID_W)


def na_attention(q, k, v, kc, vc, bias):
    b, h, t, dh = q.shape
    lc = kc.shape[2]
    rows = t // GRID_W
    full = lambda b, h, j: (b, h, 0, 0)
    return pl.pallas_call(
        functools.partial(_na_kernel, rows=rows),
        grid=(b, h, rows // NA_RB),
        in_specs=[pl.BlockSpec((1, 1, NA_RB * GRID_W, dh), lambda b, h, j: (b, h, j, 0)),
                  pl.BlockSpec((1, 1, t, dh), full), pl.BlockSpec((1, 1, t, dh), full),
                  pl.BlockSpec((1, 1, lc, dh), full), pl.BlockSpec((1, 1, lc, dh), full),
                  pl.BlockSpec((1, NA_ROWS, GRID_W, NA_ROWS * GRID_W), lambda b, h, j: (h, 0, 0, 0))],
        out_specs=pl.BlockSpec((1, 1, NA_RB * GRID_W, dh), lambda b, h, j: (b, h, j, 0)),
        out_shape=jax.ShapeDtypeStruct((b, h, t, dh), F32),
        compiler_params=_params(("arbitrary", "arbitrary", "arbitrary")),
        name="na_attention",
    )(q, k, v, kc, vc, bias)


def _rms_rows(x, g):
    return x * lax.rsqrt(jnp.mean(x * x, axis=-1, keepdims=True) + NORM_EPS) * g


def _mla_q_kernel(cq_ref, g_ref, w_ref, cos_ref, sin_ref, o_ref):
    r = _dot(_rms_rows(cq_ref[0], g_ref[...]).astype(BF16), w_ref[...])
    nn = MLA_HEADS * MLA_NOPE
    nr = MLA_HEADS * MLA_ROPE
    o_ref[0, :, :nn] = r[:, :nn]
    o_ref[0, :, nn:] = r[:, nn:nn + nr] * cos_ref[0] + r[:, nn + nr:] * sin_ref[0]


def mla_q(p, q_norm, w_q3, cos_q, sin_q, tm=512):
    nseg, seg, _ = p.shape
    nout = MLA_HEADS * (MLA_NOPE + MLA_ROPE)
    nr = MLA_HEADS * MLA_ROPE
    tok = lambda s, i: (s, i, 0)
    return pl.pallas_call(
        _mla_q_kernel,
        grid=(nseg, seg // tm),
        in_specs=[pl.BlockSpec((1, tm, MLA_Q_LORA), tok),
                  pl.BlockSpec((1, MLA_Q_LORA), lambda s, i: (0, 0)),
                  pl.BlockSpec(w_q3.shape, lambda s, i: (0, 0)),
                  pl.BlockSpec((1, tm, nr), tok), pl.BlockSpec((1, tm, nr), tok)],
        out_specs=pl.BlockSpec((1, tm, nout), tok),
        out_shape=jax.ShapeDtypeStruct((nseg, seg, nout), F32),
        compiler_params=_params(("arbitrary", "arbitrary")),
        name="mla_q",
    )(p, q_norm.reshape(1, -1), w_q3, cos_q, sin_q)


def _mla_kv_kernel(ckv_ref, kpe_ref, g_ref, w_ref, cos_ref, sin_ref, ckvn_ref, kpeo_ref, kv_ref):
    cn = _rms_rows(ckv_ref[0], g_ref[...])
    ckvn_ref[0] = cn
    kv_ref[0] = _dot(cn.astype(BF16), w_ref[...])
    kp = kpe_ref[0]
    kpeo_ref[0] = kp[:, :MLA_ROPE] * cos_ref[0] + kp[:, MLA_ROPE:2 * MLA_ROPE] * sin_ref[0]


def mla_kv(p, kv_norm, w_kv, cos_k, sin_k, tm=512):
    nseg, seg, _ = p.shape
    nkv = w_kv.shape[1]
    tok = lambda s, i: (s, i, 0)
    return pl.pallas_call(
        _mla_kv_kernel,
        grid=(nseg, seg // tm),
        in_specs=[pl.BlockSpec((1, tm, MLA_KV_LORA), lambda s, i: (s, i, MLA_Q_LORA // MLA_KV_LORA)),
                  pl.BlockSpec((1, tm, 128), lambda s, i: (s, i, (MLA_Q_LORA + MLA_KV_LORA) // 128)),
                  pl.BlockSpec((1, MLA_KV_LORA), lambda s, i: (0, 0)),
                  pl.BlockSpec(w_kv.shape, lambda s, i: (0, 0)),
                  pl.BlockSpec((1, tm, MLA_ROPE), tok), pl.BlockSpec((1, tm, MLA_ROPE), tok)],
        out_specs=[pl.BlockSpec((1, tm, MLA_KV_LORA), tok), pl.BlockSpec((1, tm, MLA_ROPE), tok),
                   pl.BlockSpec((1, tm, nkv), tok)],
        out_shape=[jax.ShapeDtypeStruct((nseg, seg, MLA_KV_LORA), F32),
                   jax.ShapeDtypeStruct((nseg, seg, MLA_ROPE), F32),
                   jax.ShapeDtypeStruct((nseg, seg, nkv), F32)],
        compiler_params=_params(("arbitrary", "arbitrary")),
        name="mla_kv",
    )(p, p, kv_norm.reshape(1, -1), w_kv, cos_k, sin_k)


def _mm_kernel(a_ref, w_ref, o_ref):
    o_ref[...] = _dot(a_ref[...].astype(BF16), w_ref[...])


def matmul(a, w_bf16, tm):
    m, k = a.shape
    n = w_bf16.shape[1]
    return pl.pallas_call(
        _mm_kernel,
        grid=(m // tm,),
        in_specs=[pl.BlockSpec((tm, k), lambda i: (i, 0)), pl.BlockSpec((k, n), lambda i: (0, 0))],
        out_specs=pl.BlockSpec((tm, n), lambda i: (i, 0)),
        out_shape=jax.ShapeDtypeStruct((m, n), F32),
        compiler_params=_params(("arbitrary",)),
        name="matmul",
    )(a, w_bf16)


PEER_RT = 128
NOT_TOP = 99.0


def _top16(s, exact):
    key = lax.broadcasted_iota(jnp.int32, s.shape, 0).astype(F32)
    rank = jnp.full(s.shape, NOT_TOP, F32)
    vals = []
    for r in range(PEER_TOPK):
        m = jnp.max(s, axis=0, keepdims=True)
        hit = s == m
        if exact:
            hit = key == jnp.min(jnp.where(hit, key, 1e9), axis=0, keepdims=True)
        rank = jnp.where(hit, float(r), rank)
        s = jnp.where(hit, NEG_INF, s)
        vals.append(m)
    return vals, rank


def _pair_topk(av, bv, exact):
    n = av[0].shape[-1]
    a_lo, a_hi = jnp.concatenate(av[:8], 0), jnp.concatenate(av[8:], 0)
    b_lo, b_hi = jnp.concatenate(bv[:8], 0), jnp.concatenate(bv[8:], 0)
    row = lax.broadcasted_iota(jnp.int32, (8, n), 0).astype(F32)

    no_pos = 1e8

    def rows_b(a, b_blk, boff, nvalid):
        ok = row < nvalid
        return jnp.where(ok, av[a] + b_blk, NEG_INF), jnp.where(ok, a * 16.0 + boff + row, no_pos)

    def rows_a(b, a_blk, aoff, lo, hi):
        ok = (row >= lo) & (row < hi)
        return jnp.where(ok, a_blk + bv[b], NEG_INF), jnp.where(ok, (aoff + row) * 16.0 + b, no_pos)

    groups = [rows_b(0, b_lo, 0, 8), rows_b(0, b_hi, 8, 8), rows_b(1, b_lo, 0, 8), rows_b(2, b_lo, 0, 5),
              rows_b(3, b_lo, 0, 4), rows_a(0, a_lo, 0, 4, 8), rows_a(0, a_hi, 8, 0, 8),
              rows_a(1, a_lo, 0, 4, 8), rows_a(2, a_lo, 0, 4, 5)]
    cands = [g[0] for g in groups]
    poss = [g[1] for g in groups]
    sels = [jnp.zeros((8, n), F32) for _ in groups]
    top = av[0] + bv[0]
    z = jnp.zeros((1, n), F32)
    for _ in range(PEER_TOPK):
        m = functools.reduce(jnp.maximum, cands)
        m = jnp.max(m, axis=0, keepdims=True)
        hits = [c == m for c in cands]
        if exact:
            first = functools.reduce(jnp.minimum, [jnp.where(hh, p, 1e9) for hh, p in zip(hits, poss)])
            first = jnp.min(first, axis=0, keepdims=True)
            hits = [p == first for p in poss]
        cands = [jnp.where(hh, NEG_INF, c) for hh, c in zip(hits, cands)]
        sels = [jnp.where(hh, 1.0, s) for hh, s in zip(hits, sels)]
        z = z + jnp.exp(m - top)
    cnt = lambda x: jnp.sum(x, axis=0, keepdims=True)
    cut_lo = sels[5] + sels[7] + sels[8]
    for a, c in enumerate([cnt(sels[0]) + cnt(sels[1]), cnt(sels[2]), cnt(sels[3]), cnt(sels[4])]):
        cut_lo = cut_lo + jnp.where(row == a, c, 0.0)
    return cut_lo, sels[6], z, cnt(cut_lo) + cnt(sels[6])


def _peer_route_kernel(x_ref, sh_ref, sc_ref, wq_ref, sk_ref, xm_ref, e1_ref, cut_ref, e2_ref, r2_ref, q_s, *, tm):
    xm = (x_ref[0] * (1.0 + sc_ref[0]) + sh_ref[0]).astype(BF16)
    xm_ref[0] = xm
    q = _dot(xm, wq_ref[...])
    for hp in range(2 * PEER_HEADS):
        q_s[hp] = q[:, hp * PEER_HALF:(hp + 1) * PEER_HALF]

    def route(h, tok, exact):
        def scores(hp):
            return lax.dot_general(sk_ref[hp], q_s[hp, tok, :], (((1,), (1,)), ((), ())),
                                   precision=lax.Precision.HIGHEST, preferred_element_type=F32)

        s1, s2 = scores(2 * h), scores(2 * h + 1)
        av, rank1 = _top16(s1, exact)
        bv, rank2 = _top16(s2, exact)
        cut_lo, cut_hi, z, nsel = _pair_topk(av, bv, exact)
        cut = jnp.zeros_like(s1)
        for r in range(PEER_TOPK):
            src = cut_lo if r < 8 else cut_hi
            cut = jnp.where(rank1 == float(r), src[r % 8:r % 8 + 1, :], cut)
        e1_ref[0, h, :, tok] = (jnp.exp(s1 - av[0]) / z).astype(BF16)
        cut_ref[0, h, :, tok] = cut.astype(BF16)
        e2_ref[0, h, :, tok] = jnp.exp(s2 - bv[0]).astype(BF16)
        r2_ref[0, h, :, tok] = rank2.astype(BF16)
        ranked = lambda rk: jnp.sum(jnp.where(rk < PEER_TOPK, 1.0, 0.0), axis=0, keepdims=True)
        return ranked(rank1), ranked(rank2), nsel

    def body(h, carry):
        toks = [pl.ds(t0, PEER_RT) for t0 in range(0, tm, PEER_RT)]
        counts = [route(h, tok, exact=False) for tok in toks]
        for tok, cnts in zip(toks, counts):
            bad = functools.reduce(jnp.maximum, [jnp.abs(cn - PEER_TOPK) for cn in cnts])

            @pl.when(jnp.max(bad) > 0.0)
            def _():
                route(h, tok, exact=True)
        return carry

    lax.fori_loop(0, PEER_HEADS, body, 0)


def peer_route(x3, mod3, shift_chunk, wq_bf16, subkeys, tm=512):
    nseg, seg, d = x3.shape
    tok = lambda s, i: (s, i, 0)
    rshape = jax.ShapeDtypeStruct((nseg, PEER_HEADS, PEER_NKEYS, seg), BF16)
    rspec = pl.BlockSpec((1, PEER_HEADS, PEER_NKEYS, tm), lambda s, i: (s, 0, 0, i))
    return pl.pallas_call(
        functools.partial(_peer_route_kernel, tm=tm),
        grid=(nseg, seg // tm),
        in_specs=[pl.BlockSpec((1, tm, d), tok),
                  pl.BlockSpec((1, 1, d), lambda s, i: (s, 0, shift_chunk)),
                  pl.BlockSpec((1, 1, d), lambda s, i: (s, 0, shift_chunk + 1)),
                  pl.BlockSpec(wq_bf16.shape, lambda s, i: (0, 0)),
                  pl.BlockSpec((2 * PEER_HEADS, PEER_NKEYS, PEER_HALF), lambda s, i: (0, 0, 0))],
        out_specs=[pl.BlockSpec((1, tm, d), tok), rspec, rspec, rspec, rspec],
        out_shape=[jax.ShapeDtypeStruct((nseg, seg, d), BF16)] + [rshape] * 4,
        scratch_shapes=[pltpu.VMEM((2 * PEER_HEADS, tm, PEER_HALF), F32)],
        compiler_params=_params(("arbitrary", "arbitrary")),
        name="peer_route",
    )(x3, mod3, mod3, wq_bf16, subkeys.reshape(2 * PEER_HEADS, PEER_NKEYS, PEER_HALF))


PEER_CE = 1024


def _gelu_tanh(x):
    return 0.5 * x * (1.0 + jnp.tanh(0.7978845608028654 * (x + 0.044715 * x * x * x)))


def _peer_dense_kernel(xm_ref, u_ref, vt_ref, e1_ref, cut_ref, e2_ref, r2_ref, x_ref, gate_ref, g_ref, b_ref,
                       o_ref, acc_s, at_s, w_s, e2_s, r2_s, *, tm):
    e = pl.program_id(2)
    nb = PEER_CE // PEER_NKEYS
    ntt = tm // PEER_RT

    @pl.when(e == 0)
    def _():
        acc_s[...] = jnp.zeros_like(acc_s)
        e2_s[:, :, :tm] = e2_ref[0]
        r2_s[:, :, :tm] = r2_ref[0]

    packed = (PEER_NKEYS // 16, 16, PEER_RT)
    ng = 2

    def gate_tiles(tt, i0):
        tok = slice(tt * PEER_RT, (tt + 1) * PEER_RT)
        gmats = [jnp.zeros(packed, BF16) for _ in range(ng)]
        for h in range(PEER_HEADS):
            e2 = e2_s[h, :, tok].reshape(packed)
            r2 = r2_s[h, :, tok].reshape(packed)
            for k in range(ng):
                i = i0 + k
                e1 = jnp.broadcast_to(e1_ref[0, h, i:i + 1, tok], (16, PEER_RT))[None]
                cut = jnp.broadcast_to(cut_ref[0, h, i:i + 1, tok], (16, PEER_RT))[None]
                gmats[k] = gmats[k] + e1 * jnp.where(r2 < cut, e2, jnp.zeros_like(e2))
        for k in range(ng):
            rows = slice((i0 + k) * PEER_NKEYS, (i0 + k + 1) * PEER_NKEYS)
            act = _gelu_tanh(at_s[rows, tok]).astype(BF16)
            w_s[rows, tok] = gmats[k].reshape(PEER_NKEYS, PEER_RT) * act

    at_s[:, :tm] = _dot(u_ref[0], xm_ref[0], ((1,), (1,)))
    for tt in range(ntt):
        for i0 in range(0, nb, ng):
            gate_tiles(tt, i0)
    acc_s[:, :tm] += _dot(vt_ref[0, 0], w_s[:, :tm])

    @pl.when(e == pl.num_programs(2) - 1)
    def _():
        z = DEEPNORM_ALPHA * x_ref[0] + gate_ref[0] * acc_s[:, :tm].T
        o_ref[0] = _layer_norm_rows(z, g_ref[...], b_ref[...])


def peer_dense(xm, u_all, vt_all, l, e1, cut, e2, r2, x3, mod3, gate_chunk, ln_g, ln_b, tm=1024):
    nseg, seg, d = x3.shape
    ne = u_all.shape[1]
    nb = PEER_CE // PEER_NKEYS
    tp = tm + PEER_RT
    tok = lambda s, i, e: (s, i, 0)
    chunk = pl.BlockSpec((1, PEER_HEADS, nb, tm), lambda s, i, e: (s, 0, e, i))
    full = pl.BlockSpec((1, PEER_HEADS, PEER_NKEYS, tm), lambda s, i, e: (s, 0, 0, i))
    return pl.pallas_call(
        functools.partial(_peer_dense_kernel, tm=tm),
        grid=(nseg, seg // tm, ne // PEER_CE),
        in_specs=[pl.BlockSpec((1, tm, d), tok),
                  pl.BlockSpec((1, PEER_CE, d), lambda s, i, e: (l, e, 0)),
                  pl.BlockSpec((1, 1, d, PEER_CE), lambda s, i, e: (l, e, 0, 0)),
                  chunk, chunk, full, full,
                  pl.BlockSpec((1, tm, d), tok),
                  pl.BlockSpec((1, 1, d), lambda s, i, e: (s, 0, gate_chunk)),
                  pl.BlockSpec((1, d), lambda s, i, e: (0, 0)),
                  pl.BlockSpec((1, d), lambda s, i, e: (0, 0))],
        out_specs=pl.BlockSpec((1, tm, d), tok),
        out_shape=jax.ShapeDtypeStruct((nseg, seg, d), F32),
        scratch_shapes=[pltpu.VMEM((d, tp), F32), pltpu.VMEM((PEER_CE, tp), F32), pltpu.VMEM((PEER_CE, tp), BF16),
                        pltpu.VMEM((PEER_HEADS, PEER_NKEYS, tp), BF16), pltpu.VMEM((PEER_HEADS, PEER_NKEYS, tp), BF16)],
        compiler_params=_params(("arbitrary", "arbitrary", "arbitrary")),
        name="peer_dense",
    )(xm, u_all, vt_all, e1, cut, e2, r2, x3, mod3, ln_g.reshape(1, d), ln_b.reshape(1, d))


def peer_layer(x3, mod3, l, wq, subkeys, u_all, vt_all, ln_g, ln_b):
    xm, e1, cut, e2, r2 = peer_route(x3, mod3, 3, wq.astype(BF16), subkeys)
    return peer_dense(xm, u_all, vt_all, l, e1, cut, e2, r2, x3, mod3, 5, ln_g, ln_b)


def _pad_cols(w, n):
    return jnp.pad(w, ((0, 0), (0, n - w.shape[1])))


def _stream(prompt_part, sample_part):
    return jnp.concatenate([prompt_part.reshape(1, -1, prompt_part.shape[-1]), sample_part], axis=0)


def _head_major(a, heads):
    b, t, _ = a.shape
    return jnp.transpose(a.reshape(b, t, heads, -1), (0, 2, 1, 3))


def _token_major(a):
    b, h, t, dh = a.shape
    return jnp.transpose(a, (0, 2, 1, 3)).reshape(b, t, h * dh)


MLSTM_CHUNK = 128
GLA_CHUNK = 64
NPROJ = 3200


def mlstm_layer(x3, mod3, bp, lp, st_c, st_n, st_m, w_in, b_gate, norm_w, w_out, ln_g, ln_b):
    nseg, seg, _ = x3.shape
    bs = nseg - 1
    p = mod_matmul(x3, mod3, 0, _pad_cols(w_in, NPROJ).astype(BF16))
    graw = p[:, :, 3072:3088]
    gp = graw[0].reshape(bp, lp, 16)
    zc = jnp.zeros((bp, 8, M_DK, M_DV), F32)
    zn = jnp.zeros((bp, 8, M_DK), F32)
    hfp, hbp, c_new, n_new, m_new = mlstm_scan(p.reshape(nseg * bp, lp, NPROJ), 0, bp, lp, gp,
                                               jnp.swapaxes(gp, 1, 2), b_gate, zc, zn, zn, min(MLSTM_CHUNK, lp),
                                               nseg * bp, 0)
    gs = graw[1:]
    hf, hb, _, _, _ = mlstm_scan(p, 1, bs, seg, gs, jnp.swapaxes(gs, 1, 2), b_gate,
                                 st_c.reshape(bs, 8, M_DK, M_DV), st_n.reshape(bs, 8, M_DK),
                                 jnp.broadcast_to(st_m.reshape(bs, 8, 1), (bs, 8, M_DK)), MLSTM_CHUNK,
                                 nseg, 1, prev=(hfp, hbp))
    x3 = outproj_ln("mlstm", (hf, hb), x3, mod3, 2, w_out.astype(BF16), ln_g, ln_b,
                    norm_w=norm_w, og=p, og_col=2)
    return (x3, c_new.reshape(bp, 2, M_HEADS, M_DK, M_DV), n_new.reshape(bp, 2, M_HEADS, M_DK),
            m_new[:, :, 0].reshape(bp, 2, M_HEADS))


def gla_layer(x3, mod3, bp, lp, st_s, w_in, w_gate2, b_gate2, norm_w, w_out, ln_g, ln_b):
    nseg, seg, _ = x3.shape
    bs = nseg - 1
    p = mod_matmul(x3, mod3, 0, _pad_cols(w_in, NPROJ).astype(BF16))
    gr = p[:, :, 3072:3104]
    zs = jnp.zeros((bp, 8, G_DV, G_DK), F32)
    ofp, obp, s_new = gla_scan(p.reshape(nseg * bp, lp, NPROJ), 0, bp, lp, gr[0].reshape(bp, lp, 32),
                               w_gate2, b_gate2, zs, GLA_CHUNK, nseg * bp, 0)
    s0t = jnp.swapaxes(st_s.reshape(bs, 8, G_DK, G_DV), -1, -2)
    of, ob, _ = gla_scan(p, 1, bs, seg, gr[1:], w_gate2, b_gate2, s0t, GLA_CHUNK, nseg, 1, prev=(ofp, obp))
    x3 = outproj_ln("gla", (of, ob), x3, mod3, 2, w_out.astype(BF16), ln_g, ln_b,
                    norm_w=jnp.tile(norm_w, G_HEADS), og=p, og_col=2)
    return x3, jnp.swapaxes(s_new, -1, -2).reshape(bp, 2, G_HEADS, G_DK, G_DV)


def na_layer(x3, mod3, bp, lp, cache_k, cache_v, w_in, rpb, w_out, ln_g, ln_b):
    nseg, seg, _ = x3.shape
    bs = nseg - 1
    hd = NA_HEADS * NA_HD
    p = mod_matmul(x3, mod3, 0, w_in.astype(BF16))
    pp = p[0].reshape(bp, lp, 3 * hd)
    hm = lambda a: _head_major(a, NA_HEADS).astype(BF16)
    yp = attention(hm(pp[..., :hd]), hm(pp[..., hd:2 * hd]), hm(pp[..., 2 * hd:]), lp)
    ps = p[1:]
    ys = na_attention(hm(ps[..., :hd]), hm(ps[..., hd:2 * hd]), hm(ps[..., 2 * hd:]),
                      hm(cache_k.reshape(bs, -1, hd)), hm(cache_v.reshape(bs, -1, hd)), na_bias_table(rpb))
    x3 = outproj_ln("plain", _stream(_token_major(yp), _token_major(ys)), x3, mod3, 2, w_out.astype(BF16), ln_g, ln_b)
    return (x3, pp[..., hd:2 * hd].reshape(bp, lp, NA_HEADS, NA_HD), pp[..., 2 * hd:].reshape(bp, lp, NA_HEADS, NA_HD))


def _rope_rotated_cols(w):
    q = MLA_ROPE // 4
    return jnp.concatenate([-w[..., q:2 * q], w[..., :q], -w[..., 3 * q:], w[..., 2 * q:3 * q]], axis=-1)


def _rope_tables(ts):
    ra = MLA_ROPE // 2
    t = np.arange(ts)
    inv = 1.0 / (ROPE_BASE ** (np.arange(0, ra, 2, dtype=np.float32) / ra))
    ang_r = (t // GRID_W).astype(np.float32)[:, None] * inv[None, :]
    ang_c = (t % GRID_W).astype(np.float32)[:, None] * inv[None, :]
    ang = np.concatenate([ang_r, ang_r, ang_c, ang_c], axis=-1).astype(np.float32)
    return jnp.cos(jnp.asarray(ang)), jnp.sin(jnp.asarray(ang))


def mla_layer(x3, mod3, bp, lp, cache_ckv, cache_kpe, w_in, q_norm, w_qup, kv_norm, w_kvup, w_out, ln_g, ln_b):
    nseg, seg, _ = x3.shape
    bs = nseg - 1
    nq = MLA_Q_LORA + MLA_KV_LORA
    w_ext = jnp.concatenate([w_in, _rope_rotated_cols(w_in[:, nq:])], axis=1)
    p = mod_matmul(x3, mod3, 0, _pad_cols(w_ext, 896).astype(BF16))
    cos_t, sin_t = _rope_tables(seg)
    cos3 = jnp.concatenate([jnp.ones((1, seg, MLA_ROPE), F32), jnp.broadcast_to(cos_t, (bs, seg, MLA_ROPE))], 0)
    sin3 = jnp.concatenate([jnp.zeros((1, seg, MLA_ROPE), F32), jnp.broadcast_to(sin_t, (bs, seg, MLA_ROPE))], 0)
    wq = w_qup.reshape(MLA_Q_LORA, MLA_HEADS, MLA_NOPE + MLA_ROPE)
    wq_rope = wq[:, :, MLA_NOPE:]
    w_q3 = jnp.concatenate([wq[:, :, :MLA_NOPE].reshape(MLA_Q_LORA, -1), wq_rope.reshape(MLA_Q_LORA, -1),
                            _rope_rotated_cols(wq_rope).reshape(MLA_Q_LORA, -1)], axis=1).astype(BF16)
    q_all = mla_q(p, q_norm, w_q3, jnp.tile(cos3, (1, 1, MLA_HEADS)), jnp.tile(sin3, (1, 1, MLA_HEADS)))
    wkv = w_kvup.reshape(MLA_KV_LORA, MLA_HEADS, MLA_NOPE + MLA_VD)
    w_kv2 = jnp.concatenate([wkv[:, :, :MLA_NOPE].reshape(MLA_KV_LORA, -1),
                             wkv[:, :, MLA_NOPE:].reshape(MLA_KV_LORA, -1)], axis=1).astype(BF16)
    ckvn, kpe, kv = mla_kv(p, kv_norm, w_kv2, cos3, sin3)
    kvc = matmul(cache_ckv.reshape(-1, MLA_KV_LORA), w_kv2, 512).reshape(bs, -1, w_kv2.shape[1])
    nn = MLA_HEADS * MLA_NOPE

    def heads(q_rows, kv_rows, kpe_rows):
        b, t, _ = q_rows.shape
        tk = kv_rows.shape[1]
        qh = jnp.concatenate([q_rows[..., :nn].reshape(b, t, MLA_HEADS, MLA_NOPE),
                              q_rows[..., nn:].reshape(b, t, MLA_HEADS, MLA_ROPE)], -1)
        kh = jnp.concatenate([kv_rows[..., :nn].reshape(b, tk, MLA_HEADS, MLA_NOPE),
                              jnp.broadcast_to(kpe_rows[:, :, None, :], (b, tk, MLA_HEADS, MLA_ROPE))], -1)
        vh = kv_rows[..., nn:].reshape(b, tk, MLA_HEADS, MLA_VD)
        tr = lambda a: jnp.transpose(a, (0, 2, 1, 3)).astype(BF16)
        return tr(qh), tr(kh), tr(vh)

    yp = attention(*heads(q_all[0].reshape(bp, lp, -1), kv[0].reshape(bp, lp, -1), kpe[0].reshape(bp, lp, -1)), lp)
    ys = attention(*heads(q_all[1:], jnp.concatenate([kv[1:], kvc], 1), jnp.concatenate([kpe[1:], cache_kpe], 1)), 256)
    x3 = outproj_ln("plain", _stream(_token_major(yp), _token_major(ys)), x3, mod3, 2, w_out.astype(BF16), ln_g, ln_b)
    return x3, ckvn[0].reshape(bp, lp, MLA_KV_LORA), kpe[0].reshape(bp, lp, MLA_ROPE)


def kernel(x_prompt, x_sample, c, c_ctx, state_mlstm_C, state_mlstm_n, state_mlstm_m, state_gla_S, cache_na_k, cache_na_v, cache_mla_ckv, cache_mla_kpe, ada_w, ada_b, ln_mix_g, ln_mix_b, ln_ffn_g, ln_ffn_b, mlstm_w_in, mlstm_b_gate, mlstm_norm_w, mlstm_w_out, gla_w_in, gla_w_gate2, gla_b_gate2, gla_norm_w, gla_w_out, na_w_in, na_rpb, na_w_out, mla_w_in, mla_q_norm, mla_w_qup, mla_kv_norm, mla_w_kvup, mla_w_out, peer_w_q, peer_subkeys, peer_u, peer_v):
    bp, lp, d = x_prompt.shape
    bs, ts, _ = x_sample.shape
    assert bp * lp == ts and bs + 1 <= 8
    x3 = _stream(x_prompt, x_sample)
    cond8 = jnp.zeros((8, d), F32).at[0].set(c_ctx).at[1:1 + bs].set(c)
    mods = adaln_all(cond8, ada_w, ada_b)
    u_all = peer_u.astype(BF16)
    vt_all = jnp.swapaxes(peer_v.reshape(DEPTH, -1, PEER_CE, d), 2, 3).astype(BF16)
    outs = {}
    for l in range(DEPTH):
        mod3 = mods[l].reshape(8, 1, ADA_CHUNKS * d)
        kind = l % 4
        if kind == 0:
            x3, outs["C"], outs["n"], outs["m"] = mlstm_layer(
                x3, mod3, bp, lp, state_mlstm_C, state_mlstm_n, state_mlstm_m, mlstm_w_in, mlstm_b_gate,
                mlstm_norm_w, mlstm_w_out, ln_mix_g[l], ln_mix_b[l])
        elif kind == 1:
            x3, outs["S"] = gla_layer(x3, mod3, bp, lp, state_gla_S, gla_w_in, gla_w_gate2, gla_b_gate2,
                                      gla_norm_w, gla_w_out, ln_mix_g[l], ln_mix_b[l])
        elif kind == 2:
            x3, outs["nk"], outs["nv"] = na_layer(x3, mod3, bp, lp, cache_na_k, cache_na_v, na_w_in, na_rpb,
                                                  na_w_out, ln_mix_g[l], ln_mix_b[l])
        else:
            x3, outs["ckv"], outs["kpe"] = mla_layer(x3, mod3, bp, lp, cache_mla_ckv, cache_mla_kpe, mla_w_in,
                                                     mla_q_norm, mla_w_qup, mla_kv_norm, mla_w_kvup, mla_w_out,
                                                     ln_mix_g[l], ln_mix_b[l])
        x3 = peer_layer(x3, mod3, l, peer_w_q[l], peer_subkeys[l], u_all, vt_all, ln_ffn_g[l], ln_ffn_b[l])
    return (x3[0].reshape(bp, lp, d), x3[1:], outs["C"], outs["n"], outs["m"], outs["S"], outs["nk"], outs["nv"],
            outs["ckv"], outs["kpe"])
```

```python
import functools

import numpy as np
import jax
import jax.numpy as jnp
from jax import lax
from jax.experimental import pallas as pl
from jax.experimental.pallas import tpu as pltpu

D_MODEL = 1024
DEPTH = 4
GRID_W = 64
DEEPNORM_ALPHA = (2.0 * DEPTH) ** 0.25
ADA_CHUNKS = 6
NORM_EPS = 1e-5
SEG = 4096
NSEG = 3

M_HEADS, M_DK, M_DV = 4, 128, 256
G_HEADS, G_DK, G_DV = 4, 128, 256
G_GATE_RANK = 16
G_GATE_NORM = 16.0
NA_HEADS, NA_HD, NA_ROWS, NA_COLS = 16, 64, 8, 16
MLA_HEADS, MLA_Q_LORA, MLA_KV_LORA, MLA_NOPE, MLA_ROPE, MLA_VD = 16, 512, 256, 64, 32, 64
ROPE_BASE = 10000.0
PEER_HEADS, PEER_NKEYS, PEER_HALF, PEER_TOPK = 8, 128, 128, 16

V7X_VMEM_LIMIT = 56 * 1024 * 1024
F32 = jnp.float32
BF16 = jnp.bfloat16
NEG_INF = float("-inf")


def _params(sem, vmem=V7X_VMEM_LIMIT):
    return pltpu.CompilerParams(dimension_semantics=sem, vmem_limit_bytes=vmem)


def _dot(a, b, dims=((1,), (0,))):
    return lax.dot_general(a, b, (dims, ((), ())), preferred_element_type=F32)


def _split3(a):
    hi = a.astype(BF16)
    r1 = a - hi.astype(F32)
    mid = r1.astype(BF16)
    lo = (r1 - mid.astype(F32)).astype(BF16)
    return hi, mid, lo


def _dot_exact_lhs(m01, a):
    hi, mid, lo = _split3(a)
    return _dot(m01, hi) + _dot(m01, mid) + _dot(m01, lo)


def _dot_exact_rhs(a, m01):
    hi, mid, lo = _split3(a)
    return _dot(hi, m01) + _dot(mid, m01) + _dot(lo, m01)


def _log_sigmoid(x):
    return jnp.minimum(x, 0.0) - jnp.log(1.0 + jnp.exp(-jnp.abs(x)))


def _sigmoid(x):
    return 1.0 / (1.0 + jnp.exp(-x))


def _adaln_kernel(c_ref, w_ref, b_ref, o_ref):
    cv = c_ref[...]
    a = cv * _sigmoid(cv)
    o_ref[0] = lax.dot_general(a, w_ref[0], (((1,), (0,)), ((), ())), precision=lax.Precision.HIGHEST,
                               preferred_element_type=F32) + b_ref[0]


def adaln_all(cond8, ada_w, ada_b):
    tn = 1024
    n = ada_w.shape[-1]
    return pl.pallas_call(
        _adaln_kernel,
        grid=(DEPTH, n // tn),
        in_specs=[pl.BlockSpec((8, D_MODEL), lambda l, j: (0, 0)),
                  pl.BlockSpec((1, D_MODEL, tn), lambda l, j: (l, 0, j)),
                  pl.BlockSpec((1, 1, tn), lambda l, j: (l, 0, j))],
        out_specs=pl.BlockSpec((1, 8, tn), lambda l, j: (l, 0, j)),
        out_shape=jax.ShapeDtypeStruct((DEPTH, 8, n), F32),
        compiler_params=_params(("arbitrary", "arbitrary")),
        name="adaln",
    )(cond8, ada_w, ada_b.reshape(DEPTH, 1, n))


def _modmm_kernel(x_ref, sh_ref, sc_ref, w_ref, o_ref, xm_ref):
    @pl.when(pl.program_id(2) == 0)
    def _():
        xm_ref[...] = (x_ref[0] * (1.0 + sc_ref[0]) + sh_ref[0]).astype(BF16)

    o_ref[0] = _dot(xm_ref[...], w_ref[...]).astype(o_ref.dtype)


def mod_matmul(x3, mod3, shift_chunk, w_bf16, tm=512, tn=None, out_dtype=F32):
    nseg, seg, d = x3.shape
    n = w_bf16.shape[1]
    tn = n if tn is None else tn
    return pl.pallas_call(
        _modmm_kernel,
        grid=(nseg, seg // tm, n // tn),
        in_specs=[pl.BlockSpec((1, tm, d), lambda s, i, j: (s, i, 0)),
                  pl.BlockSpec((1, 1, d), lambda s, i, j: (s, 0, shift_chunk)),
                  pl.BlockSpec((1, 1, d), lambda s, i, j: (s, 0, shift_chunk + 1)),
                  pl.BlockSpec((d, tn), lambda s, i, j: (0, j))],
        out_specs=pl.BlockSpec((1, tm, tn), lambda s, i, j: (s, i, j)),
        out_shape=jax.ShapeDtypeStruct((nseg, seg, n), out_dtype),
        scratch_shapes=[pltpu.VMEM((tm, d), BF16)],
        compiler_params=_params(("arbitrary", "arbitrary", "arbitrary")),
        name="mod_matmul",
    )(x3, mod3, mod3, w_bf16)


def _layer_norm_rows(y, g, b):
    mu = jnp.mean(y, axis=-1, keepdims=True)
    yc = y - mu
    var = jnp.mean(yc * yc, axis=-1, keepdims=True)
    return yc * lax.rsqrt(var + NORM_EPS) * g + b


def _outproj_kernel(*refs, mode):
    if mode == "plain":
        y_ref, x_ref, gate_ref, w_ref, g_ref, b_ref, o_ref = refs
        yin = y_ref[0].astype(BF16)
    else:
        ya_ref, yb_ref, og_ref, nw_ref, x_ref, gate_ref, w_ref, g_ref, b_ref, o_ref = refs
        hs = ya_ref[0] + yb_ref[0]
        og = og_ref[0]
        parts = []
        for h in range(4):
            seg = hs[:, h * 256:(h + 1) * 256]
            nw = nw_ref[:, h * 256:(h + 1) * 256]
            if mode == "mlstm":
                mu = jnp.mean(seg, axis=-1, keepdims=True)
                sc = seg - mu
                var = jnp.mean(sc * sc, axis=-1, keepdims=True)
                parts.append(sc * lax.rsqrt(var + NORM_EPS) * nw)
            else:
                ms = jnp.mean(seg * seg, axis=-1, keepdims=True)
                parts.append(seg * lax.rsqrt(ms + NORM_EPS) * nw)
        hn = jnp.concatenate(parts, axis=-1)
        act = _sigmoid(og) if mode == "mlstm" else og * _sigmoid(og)
        yin = (act * hn).astype(BF16)
    y = _dot(yin, w_ref[...])
    z = DEEPNORM_ALPHA * x_ref[0] + gate_ref[0] * y
    o_ref[0] = _layer_norm_rows(z, g_ref[...], b_ref[...])


def outproj_ln(mode, ys, x3, mod3, gate_chunk, w_bf16, ln_g, ln_b, norm_w=None, og=None, og_col=0, tm=512):
    nseg, seg, d = x3.shape
    k = w_bf16.shape[0]
    tok = lambda s, i: (s, i, 0)
    if mode == "plain":
        args = [ys]
        specs = [pl.BlockSpec((1, tm, k), tok)]
    else:
        args = [ys[0], ys[1], og, norm_w.reshape(1, k)]
        specs = [pl.BlockSpec((1, tm, k), tok), pl.BlockSpec((1, tm, k), tok),
                 pl.BlockSpec((1, tm, k), lambda s, i: (s, i, og_col)),
                 pl.BlockSpec((1, k), lambda s, i: (0, 0))]
    args += [x3, mod3, w_bf16, ln_g.reshape(1, d), ln_b.reshape(1, d)]
    specs += [pl.BlockSpec((1, tm, d), tok),
              pl.BlockSpec((1, 1, d), lambda s, i: (s, 0, gate_chunk)),
              pl.BlockSpec((k, d), lambda s, i: (0, 0)),
              pl.BlockSpec((1, d), lambda s, i: (0, 0)),
              pl.BlockSpec((1, d), lambda s, i: (0, 0))]
    return pl.pallas_call(
        functools.partial(_outproj_kernel, mode=mode),
        grid=(nseg, seg // tm),
        in_specs=specs,
        out_specs=pl.BlockSpec((1, tm, d), tok),
        out_shape=jax.ShapeDtypeStruct((nseg, seg, d), F32),
        compiler_params=_params(("arbitrary", "arbitrary")),
        name="outproj_ln_" + mode,
    )(*args)


def _tri(n, lower):
    r = lax.broadcasted_iota(jnp.int32, (n, n), 0)
    c = lax.broadcasted_iota(jnp.int32, (n, n), 1)
    return (c <= r) if lower else (c >= r)


def _mlstm_kernel(pf_ref, pb_ref, gf_ref, gb_ref, gtf_ref, gtb_ref, bias_ref, biast_ref,
                  c0_ref, n0_ref, m0_ref, hf_ref, hb_ref, co_ref, no_ref, mo_ref,
                  c_s, n_s, m_s, *, L):
    c = pl.program_id(1)

    @pl.when(c == 0)
    def _():
        c_s[...] = c0_ref[0]
        n_s[...] = n0_ref[0]
        m_s[...] = m0_ref[0]

    for d in range(2):
        p_ref, g_ref, gt_ref, h_ref = ((pf_ref, gf_ref, gtf_ref, hf_ref) if d == 0
                                       else (pb_ref, gb_ref, gtb_ref, hb_ref))
        mask = _tri(L, lower=(d == 0))
        mcol = mask.astype(BF16)
        mrow = _tri(L, lower=(d != 0)).astype(BF16)
        g = g_ref[0] + bias_ref[...]
        gt = gt_ref[0] + biast_ref[...]
        li_c = g[:, d * 8:d * 8 + 4]
        lf_c = _log_sigmoid(g[:, d * 8 + 4:d * 8 + 8])
        li_r = gt[d * 8:d * 8 + 4, :]
        lf_r = _log_sigmoid(gt[d * 8 + 4:d * 8 + 8, :])
        b_c = _dot_exact_lhs(mcol, lf_c)
        b_r = _dot_exact_rhs(lf_r, mrow)
        last = L - 1 if d == 0 else 0
        for h in range(M_HEADS):
            u = d * M_HEADS + h
            q = p_ref[0, :, h * M_DK:(h + 1) * M_DK]
            k = p_ref[0, :, 512 + h * M_DK:512 + (h + 1) * M_DK] * (M_DK ** -0.5)
            v = p_ref[0, :, 1024 + h * M_DV:1024 + (h + 1) * M_DV].astype(BF16)
            qb = q.astype(BF16)
            bc, br = b_c[:, h:h + 1], b_r[h:h + 1, :]
            lic, lir = li_c[:, h:h + 1], li_r[h:h + 1, :]
            m_prev = m_s[u:u + 1, 0:1]
            dmat = jnp.where(mask, bc - br + lir, NEG_INF)
            inter = bc + m_prev
            mt = jnp.maximum(inter, jnp.max(dmat, axis=-1, keepdims=True))
            smat = _dot(qb, k.astype(BF16), ((1,), (1,))) * jnp.exp(dmat - mt)
            ei = jnp.exp(inter - mt)
            cmat = c_s[u]
            num = _dot(smat.astype(BF16), v) + ei * _dot(qb, cmat.astype(BF16))
            nrow = n_s[u:u + 1, :]
            den = jnp.sum(smat, axis=-1, keepdims=True) + ei * jnp.sum(q * nrow, axis=-1, keepdims=True)
            h_ref[0, :, h * M_DV:(h + 1) * M_DV] = num / jnp.maximum(jnp.abs(den), jnp.exp(-mt))
            tot = br[:, last:last + 1]
            g_c = tot - bc + lic
            g_r = tot - br + lir
            m_new = jnp.maximum(tot + m_prev, jnp.max(g_r, axis=-1, keepdims=True))
            kw = k * jnp.exp(g_c - m_new)
            dec = jnp.exp(tot + m_prev - m_new)
            c_s[u] = dec * cmat + _dot(kw.astype(BF16), v, ((0,), (0,)))
            n_s[u:u + 1, :] = dec * nrow + jnp.sum(kw, axis=0, keepdims=True)
            m_s[u:u + 1, :] = jnp.broadcast_to(m_new, (1, 128))

    @pl.when(c == pl.num_programs(1) - 1)
    def _():
        co_ref[0] = c_s[...]
        no_ref[0] = n_s[...]
        mo_ref[0] = m_s[...]


def _skip_refs(body, start, n):
    def kernel(*refs):
        return body(*refs[:start], *refs[start + n:])
    return kernel


def _stream_out_args(out_rows, t, width, prev):
    shape = jax.ShapeDtypeStruct((out_rows, t, width), F32)
    if prev is None:
        return shape, [], []
    return shape, [a.reshape(out_rows, t, width) for a in prev], [pl.BlockSpec(memory_space=pl.ANY)] * len(prev)


def mlstm_scan(p, b0, nb, t, g, gt, bias, c0, n0, m0, L, out_rows, ob0, prev=None):
    nc = t // L
    hshape, prev_args, prev_specs = _stream_out_args(out_rows, t, M_HEADS * M_DV, prev)
    n_in = 11
    fwd = lambda b, c: (b + b0, c, 0)
    bwd = lambda b, c: (b + b0, nc - 1 - c, 0)
    st4 = lambda b, c: (b, 0, 0, 0)
    st3 = lambda b, c: (b, 0, 0)
    return pl.pallas_call(
        _skip_refs(functools.partial(_mlstm_kernel, L=L), n_in, len(prev_args)),
        grid=(nb, nc),
        input_output_aliases={n_in + i: i for i in range(len(prev_args))},
        in_specs=[pl.BlockSpec((1, L, 2048), fwd), pl.BlockSpec((1, L, 2048), bwd),
                  pl.BlockSpec((1, L, 16), lambda b, c: (b, c, 0)),
                  pl.BlockSpec((1, L, 16), lambda b, c: (b, nc - 1 - c, 0)),
                  pl.BlockSpec((1, 16, L), lambda b, c: (b, 0, c)),
                  pl.BlockSpec((1, 16, L), lambda b, c: (b, 0, nc - 1 - c)),
                  pl.BlockSpec((1, 16), lambda b, c: (0, 0)),
                  pl.BlockSpec((16, 1), lambda b, c: (0, 0)),
                  pl.BlockSpec((1, 8, M_DK, M_DV), st4),
                  pl.BlockSpec((1, 8, M_DK), st3),
                  pl.BlockSpec((1, 8, M_DK), st3)] + prev_specs,
        out_specs=[pl.BlockSpec((1, L, 1024), lambda b, c: (b + ob0, c, 0)),
                   pl.BlockSpec((1, L, 1024), lambda b, c: (b + ob0, nc - 1 - c, 0)),
                   pl.BlockSpec((1, 8, M_DK, M_DV), st4),
                   pl.BlockSpec((1, 8, M_DK), st3),
                   pl.BlockSpec((1, 8, M_DK), st3)],
        out_shape=[hshape, hshape,
                   jax.ShapeDtypeStruct((nb, 8, M_DK, M_DV), F32),
                   jax.ShapeDtypeStruct((nb, 8, M_DK), F32),
                   jax.ShapeDtypeStruct((nb, 8, M_DK), F32)],
        scratch_shapes=[pltpu.VMEM((8, M_DK, M_DV), F32), pltpu.VMEM((8, M_DK), F32),
                        pltpu.VMEM((8, M_DK), F32)],
        compiler_params=_params(("arbitrary", "arbitrary")),
        name="mlstm_scan",
    )(p, p, g, g, gt, gt, bias.reshape(1, 16), bias.reshape(16, 1), c0, n0, m0, *prev_args)


def _gla_kernel(pf_ref, pb_ref, gf_ref, gb_ref, w2_ref, b2_ref, s0_ref, of_ref, ob_ref, so_ref, s_s, *, L):
    c = pl.program_id(1)

    @pl.when(c == 0)
    def _():
        s_s[...] = s0_ref[0]

    for d in range(2):
        p_ref, g_ref, o_ref = (pf_ref, gf_ref, of_ref) if d == 0 else (pb_ref, gb_ref, ob_ref)
        mask = _tri(L, lower=(d == 0))
        mcol = mask.astype(BF16)
        gr = g_ref[0][:, d * G_GATE_RANK:(d + 1) * G_GATE_RANK]
        pre = lax.dot_general(gr, w2_ref[d], (((1,), (0,)), ((), ())), precision=lax.Precision.HIGHEST,
                              preferred_element_type=F32) + b2_ref[d]
        la = _log_sigmoid(pre) * (1.0 / G_GATE_NORM)
        bc_all = _dot_exact_lhs(mcol, la)
        last = L - 1 if d == 0 else 0
        for h in range(G_HEADS):
            u = d * G_HEADS + h
            q = p_ref[0, :, h * G_DK:(h + 1) * G_DK] * (G_DK ** -0.5)
            k = p_ref[0, :, 512 + h * G_DK:512 + (h + 1) * G_DK]
            v = p_ref[0, :, 1024 + h * G_DV:1024 + (h + 1) * G_DV].astype(BF16)
            bc = bc_all[:, h * G_DK:(h + 1) * G_DK]
            qd = (q * jnp.exp(bc)).astype(BF16)
            kd = (k * jnp.exp(-bc)).astype(BF16)
            a = jnp.where(mask, _dot(qd, kd, ((1,), (1,))), 0.0)
            st = s_s[u]
            o_ref[0, :, h * G_DV:(h + 1) * G_DV] = (_dot(a.astype(BF16), v)
                                                    + _dot(qd, st.astype(BF16), ((1,), (1,))))
            bl = bc[last:last + 1, :]
            kl = (k * jnp.exp(bl - bc)).astype(BF16)
            s_s[u] = st * jnp.exp(bl) + _dot(v, kl, ((0,), (0,)))

    @pl.when(c == pl.num_programs(1) - 1)
    def _():
        so_ref[0] = s_s[...]


def gla_scan(p, b0, nb, t, gr, w2, b2, s0t, L, out_rows, ob0, prev=None):
    nc = t // L
    oshape, prev_args, prev_specs = _stream_out_args(out_rows, t, G_HEADS * G_DV, prev)
    n_in = 7
    st4 = lambda b, c: (b, 0, 0, 0)
    return pl.pallas_call(
        _skip_refs(functools.partial(_gla_kernel, L=L), n_in, len(prev_args)),
        grid=(nb, nc),
        input_output_aliases={n_in + i: i for i in range(len(prev_args))},
        in_specs=[pl.BlockSpec((1, L, 2048), lambda b, c: (b + b0, c, 0)),
                  pl.BlockSpec((1, L, 2048), lambda b, c: (b + b0, nc - 1 - c, 0)),
                  pl.BlockSpec((1, L, 32), lambda b, c: (b, c, 0)),
                  pl.BlockSpec((1, L, 32), lambda b, c: (b, nc - 1 - c, 0)),
                  pl.BlockSpec((2, G_GATE_RANK, 512), lambda b, c: (0, 0, 0)),
                  pl.BlockSpec((2, 1, 512), lambda b, c: (0, 0, 0)),
                  pl.BlockSpec((1, 8, G_DV, G_DK), st4)] + prev_specs,
        out_specs=[pl.BlockSpec((1, L, 1024), lambda b, c: (b + ob0, c, 0)),
                   pl.BlockSpec((1, L, 1024), lambda b, c: (b + ob0, nc - 1 - c, 0)),
                   pl.BlockSpec((1, 8, G_DV, G_DK), st4)],
        out_shape=[oshape, oshape, jax.ShapeDtypeStruct((nb, 8, G_DV, G_DK), F32)],
        scratch_shapes=[pltpu.VMEM((8, G_DV, G_DK), F32)],
        compiler_params=_params(("arbitrary", "arbitrary")),
        name="gla_scan",
    )(p, p, gr, gr, w2, b2.reshape(2, 1, 512), s0t, *prev_args)


ATTN_HEADS_PER_STEP = 2


def _attn_kernel(q_ref, k_ref, v_ref, o_ref, *, scale):
    for j in range(ATTN_HEADS_PER_STEP):
        s = _dot(q_ref[0, j], k_ref[0, j], ((1,), (1,))) * scale
        m = jnp.max(s, axis=-1, keepdims=True)
        p = jnp.exp(s - m)
        l = jnp.sum(p, axis=-1, keepdims=True)
        o_ref[0, j] = _dot(p.astype(BF16), v_ref[0, j]) / l


def attention(q, k, v, tq):
    b, h, lq, dq = q.shape
    lk, dv = k.shape[2], v.shape[3]
    hb = ATTN_HEADS_PER_STEP
    return pl.pallas_call(
        functools.partial(_attn_kernel, scale=dq ** -0.5),
        grid=(b, h // hb, lq // tq),
        in_specs=[pl.BlockSpec((1, hb, tq, dq), lambda b, h, i: (b, h, i, 0)),
                  pl.BlockSpec((1, hb, lk, dq), lambda b, h, i: (b, h, 0, 0)),
                  pl.BlockSpec((1, hb, lk, dv), lambda b, h, i: (b, h, 0, 0))],
        out_specs=pl.BlockSpec((1, hb, tq, dv), lambda b, h, i: (b, h, i, 0)),
        out_shape=jax.ShapeDtypeStruct((b, h, lq, dv), F32),
        compiler_params=_params(("arbitrary", "arbitrary", "arbitrary")),
        name="attention",
    )(q, k, v)


NA_RB = 8


def _na_kernel(q_ref, k_ref, v_ref, kc_ref, vc_ref, bias_ref, o_ref, *, rows):
    j = pl.program_id(2)
    scale = NA_HD ** -0.5
    kc, vc = kc_ref[0, 0], vc_ref[0, 0]
    for a in range(NA_RB):
        r = j * NA_RB + a
        start = jnp.clip(r - NA_ROWS // 2, 0, rows - NA_ROWS)
        dr0 = start - r + (NA_ROWS - 1)
        off = pl.multiple_of(start * GRID_W, GRID_W)
        qa = q_ref[0, 0, a * GRID_W:(a + 1) * GRID_W, :]
        kl = k_ref[0, 0, pl.ds(off, NA_ROWS * GRID_W), :]
        vl = v_ref[0, 0, pl.ds(off, NA_ROWS * GRID_W), :]
        s_loc = _dot(qa, kl, ((1,), (1,))) * scale + bias_ref[0, dr0]
        s_ctx = _dot(qa, kc, ((1,), (1,))) * scale
        m = jnp.maximum(jnp.max(s_loc, axis=-1, keepdims=True), jnp.max(s_ctx, axis=-1, keepdims=True))
        p_loc = jnp.exp(s_loc - m)
        p_ctx = jnp.exp(s_ctx - m)
        l = jnp.sum(p_loc, axis=-1, keepdims=True) + jnp.sum(p_ctx, axis=-1, keepdims=True)
        o = _dot(p_loc.astype(BF16), vl) + _dot(p_ctx.astype(BF16), vc)
        o_ref[0, 0, a * GRID_W:(a + 1) * GRID_W, :] = o / l


def na_bias_table(rpb):
    cq = np.arange(GRID_W)[:, None]
    ck = np.arange(GRID_W)[None, :]
    cs = np.clip(cq - NA_COLS // 2, 0, GRID_W - NA_COLS)
    ok = (ck >= cs) & (ck < cs + NA_COLS)
    dc = np.clip(ck - cq, -(NA_COLS - 1), NA_COLS - 1) + (NA_COLS - 1)
    t = jnp.where(ok[None, None], rpb.astype(F32)[:, :, dc], NEG_INF)
    rows = np.arange(NA_ROWS)[:, None] + np.arange(NA_ROWS)[None, :]
    tf = t[:, rows]
    return jnp.transpose(tf, (0, 1, 3, 2, 4)).reshape(NA_HEADS, NA_ROWS, GRID_W, NA_ROWS * GRID_W)


def na_attention(q, k, v, kc, vc, bias):
    b, h, t, dh = q.shape
    lc = kc.shape[2]
    rows = t // GRID_W
    full = lambda b, h, j: (b, h, 0, 0)
    return pl.pallas_call(
        functools.partial(_na_kernel, rows=rows),
        grid=(b, h, rows // NA_RB),
        in_specs=[pl.BlockSpec((1, 1, NA_RB * GRID_W, dh), lambda b, h, j: (b, h, j, 0)),
                  pl.BlockSpec((1, 1, t, dh), full), pl.BlockSpec((1, 1, t, dh), full),
                  pl.BlockSpec((1, 1, lc, dh), full), pl.BlockSpec((1, 1, lc, dh), full),
                  pl.BlockSpec((1, NA_ROWS, GRID_W, NA_ROWS * GRID_W), lambda b, h, j: (h, 0, 0, 0))],
        out_specs=pl.BlockSpec((1, 1, NA_RB * GRID_W, dh), lambda b, h, j: (b, h, j, 0)),
        out_shape=jax.ShapeDtypeStruct((b, h, t, dh), F32),
        compiler_params=_params(("arbitrary", "arbitrary", "arbitrary")),
        name="na_attention",
    )(q, k, v, kc, vc, bias)


def _rms_rows(x, g):
    return x * lax.rsqrt(jnp.mean(x * x, axis=-1, keepdims=True) + NORM_EPS) * g


def _mla_q_kernel(cq_ref, g_ref, w_ref, cos_ref, sin_ref, o_ref):
    r = _dot(_rms_rows(cq_ref[0], g_ref[...]).astype(BF16), w_ref[...])
    nn = MLA_HEADS * MLA_NOPE
    nr = MLA_HEADS * MLA_ROPE
    o_ref[0, :, :nn] = r[:, :nn]
    o_ref[0, :, nn:] = r[:, nn:nn + nr] * cos_ref[0] + r[:, nn + nr:] * sin_ref[0]


def mla_q(p, q_norm, w_q3, cos_q, sin_q, tm=512):
    nseg, seg, _ = p.shape
    nout = MLA_HEADS * (MLA_NOPE + MLA_ROPE)
    nr = MLA_HEADS * MLA_ROPE
    tok = lambda s, i: (s, i, 0)
    return pl.pallas_call(
        _mla_q_kernel,
        grid=(nseg, seg // tm),
        in_specs=[pl.BlockSpec((1, tm, MLA_Q_LORA), tok),
                  pl.BlockSpec((1, MLA_Q_LORA), lambda s, i: (0, 0)),
                  pl.BlockSpec(w_q3.shape, lambda s, i: (0, 0)),
                  pl.BlockSpec((1, tm, nr), tok), pl.BlockSpec((1, tm, nr), tok)],
        out_specs=pl.BlockSpec((1, tm, nout), tok),
        out_shape=jax.ShapeDtypeStruct((nseg, seg, nout), F32),
        compiler_params=_params(("arbitrary", "arbitrary")),
        name="mla_q",
    )(p, q_norm.reshape(1, -1), w_q3, cos_q, sin_q)


def _mla_kv_kernel(ckv_ref, kpe_ref, g_ref, w_ref, cos_ref, sin_ref, ckvn_ref, kpeo_ref, kv_ref):
    cn = _rms_rows(ckv_ref[0], g_ref[...])
    ckvn_ref[0] = cn
    kv_ref[0] = _dot(cn.astype(BF16), w_ref[...])
    kp = kpe_ref[0]
    kpeo_ref[0] = kp[:, :MLA_ROPE] * cos_ref[0] + kp[:, MLA_ROPE:2 * MLA_ROPE] * sin_ref[0]


def mla_kv(p, kv_norm, w_kv, cos_k, sin_k, tm=512):
    nseg, seg, _ = p.shape
    nkv = w_kv.shape[1]
    tok = lambda s, i: (s, i, 0)
    return pl.pallas_call(
        _mla_kv_kernel,
        grid=(nseg, seg // tm),
        in_specs=[pl.BlockSpec((1, tm, MLA_KV_LORA), lambda s, i: (s, i, MLA_Q_LORA // MLA_KV_LORA)),
                  pl.BlockSpec((1, tm, 128), lambda s, i: (s, i, (MLA_Q_LORA + MLA_KV_LORA) // 128)),
                  pl.BlockSpec((1, MLA_KV_LORA), lambda s, i: (0, 0)),
                  pl.BlockSpec(w_kv.shape, lambda s, i: (0, 0)),
                  pl.BlockSpec((1, tm, MLA_ROPE), tok), pl.BlockSpec((1, tm, MLA_ROPE), tok)],
        out_specs=[pl.BlockSpec((1, tm, MLA_KV_LORA), tok), pl.BlockSpec((1, tm, MLA_ROPE), tok),
                   pl.BlockSpec((1, tm, nkv), tok)],
        out_shape=[jax.ShapeDtypeStruct((nseg, seg, MLA_KV_LORA), F32),
                   jax.ShapeDtypeStruct((nseg, seg, MLA_ROPE), F32),
                   jax.ShapeDtypeStruct((nseg, seg, nkv), F32)],
        compiler_params=_params(("arbitrary", "arbitrary")),
        name="mla_kv",
    )(p, p, kv_norm.reshape(1, -1), w_kv, cos_k, sin_k)


def _mm_kernel(a_ref, w_ref, o_ref):
    o_ref[...] = _dot(a_ref[...].astype(BF16), w_ref[...])


def matmul(a, w_bf16, tm):
    m, k = a.shape
    n = w_bf16.shape[1]
    return pl.pallas_call(
        _mm_kernel,
        grid=(m // tm,),
        in_specs=[pl.BlockSpec((tm, k), lambda i: (i, 0)), pl.BlockSpec((k, n), lambda i: (0, 0))],
        out_specs=pl.BlockSpec((tm, n), lambda i: (i, 0)),
        out_shape=jax.ShapeDtypeStruct((m, n), F32),
        compiler_params=_params(("arbitrary",)),
        name="matmul",
    )(a, w_bf16)


PEER_RT = 128
NOT_TOP = 99.0
RANK_CODE = 2.0 ** 100


def _top16(s, exact):
    vals = []
    if exact:
        key = lax.broadcasted_iota(jnp.int32, s.shape, 0).astype(F32)
        rank = jnp.full(s.shape, NOT_TOP, F32)
        for r in range(PEER_TOPK):
            m = jnp.max(s, axis=0, keepdims=True)
            hit = key == jnp.min(jnp.where(s == m, key, 1e9), axis=0, keepdims=True)
            rank = jnp.where(hit, float(r), rank)
            s = jnp.where(hit, NEG_INF, s)
            vals.append(m)
        return vals, rank
    for r in range(PEER_TOPK):
        m = jnp.max(s, axis=0, keepdims=True)
        s = jnp.where(s == m, -RANK_CODE * (1.0 + r / 32.0), s)
        vals.append(m)
    return vals, jnp.where(s <= -0.5 * RANK_CODE, s * (-32.0 / RANK_CODE) - 32.0, NOT_TOP)


def _pair_topk(av, bv, exact):
    n = av[0].shape[-1]
    a_lo, a_hi = jnp.concatenate(av[:8], 0), jnp.concatenate(av[8:], 0)
    b_lo, b_hi = jnp.concatenate(bv[:8], 0), jnp.concatenate(bv[8:], 0)
    row = lax.broadcasted_iota(jnp.int32, (8, n), 0).astype(F32)

    no_pos = 1e8

    def rows_b(a, b_blk, boff, nvalid):
        ok = row < nvalid
        return jnp.where(ok, av[a] + b_blk, NEG_INF), jnp.where(ok, a * 16.0 + boff + row, no_pos)

    def rows_a(b, a_blk, aoff, lo, hi):
        ok = (row >= lo) & (row < hi)
        return jnp.where(ok, a_blk + bv[b], NEG_INF), jnp.where(ok, (aoff + row) * 16.0 + b, no_pos)

    groups = [rows_b(0, b_lo, 0, 8), rows_b(0, b_hi, 8, 8), rows_b(1, b_lo, 0, 8), rows_b(2, b_lo, 0, 5),
              rows_b(3, b_lo, 0, 4), rows_a(0, a_lo, 0, 4, 8), rows_a(0, a_hi, 8, 0, 8),
              rows_a(1, a_lo, 0, 4, 8), rows_a(2, a_lo, 0, 4, 5)]
    cands = [g[0] for g in groups]
    poss = [g[1] for g in groups]
    sels = [jnp.zeros((8, n), F32) for _ in groups]
    top = av[0] + bv[0]
    z = jnp.zeros((1, n), F32)
    for _ in range(PEER_TOPK):
        m = functools.reduce(jnp.maximum, cands)
        m = jnp.max(m, axis=0, keepdims=True)
        if exact:
            first = functools.reduce(jnp.minimum, [jnp.where(c == m, p, 1e9) for c, p in zip(cands, poss)])
            first = jnp.min(first, axis=0, keepdims=True)
            hits = [p == first for p in poss]
            cands = [jnp.where(hh, NEG_INF, c) for hh, c in zip(hits, cands)]
            sels = [jnp.where(hh, 1.0, s) for hh, s in zip(hits, sels)]
        else:
            cands = [jnp.where(c == m, -RANK_CODE, c) for c in cands]
        z = z + jnp.exp(m - top)
    if not exact:
        sels = [jnp.where(c == -RANK_CODE, 1.0, 0.0) for c in cands]
    cnt = lambda x: jnp.sum(x, axis=0, keepdims=True)
    cut_lo = sels[5] + sels[7] + sels[8]
    for a, c in enumerate([cnt(sels[0]) + cnt(sels[1]), cnt(sels[2]), cnt(sels[3]), cnt(sels[4])]):
        cut_lo = cut_lo + jnp.where(row == a, c, 0.0)
    return cut_lo, sels[6], z, cnt(cut_lo) + cnt(sels[6])


def _peer_route_kernel(x_ref, sh_ref, sc_ref, wq_ref, sk_ref, xm_ref, e1_ref, cut_ref, e2_ref, r2_ref, q_s, *, tm):
    xm = (x_ref[0] * (1.0 + sc_ref[0]) + sh_ref[0]).astype(BF16)
    xm_ref[0] = xm
    q = _dot(xm, wq_ref[...])
    for hp in range(2 * PEER_HEADS):
        q_s[hp] = q[:, hp * PEER_HALF:(hp + 1) * PEER_HALF]

    def route(h, tok, exact):
        def scores(hp):
            return lax.dot_general(sk_ref[hp], q_s[hp, tok, :], (((1,), (1,)), ((), ())),
                                   precision=lax.Precision.HIGHEST, preferred_element_type=F32)

        s1, s2 = scores(2 * h), scores(2 * h + 1)
        av, rank1 = _top16(s1, exact)
        bv, rank2 = _top16(s2, exact)
        cut_lo, cut_hi, z, nsel = _pair_topk(av, bv, exact)
        cut = jnp.zeros_like(s1)
        for r in range(PEER_TOPK):
            src = cut_lo if r < 8 else cut_hi
            cut = jnp.where(rank1 == float(r), src[r % 8:r % 8 + 1, :], cut)
        e1_ref[0, h, :, tok] = (jnp.exp(s1 - av[0]) / z).astype(BF16)
        cut_ref[0, h, :, tok] = cut.astype(BF16)
        e2_ref[0, h, :, tok] = jnp.exp(s2 - bv[0]).astype(BF16)
        r2_ref[0, h, :, tok] = rank2.astype(BF16)
        ranked = lambda rk: jnp.sum(jnp.where(rk < PEER_TOPK, 1.0, 0.0), axis=0, keepdims=True)
        return ranked(rank1), ranked(rank2), nsel

    def body(h, carry):
        toks = [pl.ds(t0, PEER_RT) for t0 in range(0, tm, PEER_RT)]
        counts = [route(h, tok, exact=False) for tok in toks]
        for tok, cnts in zip(toks, counts):
            bad = functools.reduce(jnp.maximum, [jnp.abs(cn - PEER_TOPK) for cn in cnts])

            @pl.when(jnp.max(bad) > 0.0)
            def _():
                route(h, tok, exact=True)
        return carry

    lax.fori_loop(0, PEER_HEADS, body, 0)


def peer_route(x3, mod3, shift_chunk, wq_bf16, subkeys, tm=512):
    nseg, seg, d = x3.shape
    tok = lambda s, i: (s, i, 0)
    rshape = jax.ShapeDtypeStruct((nseg, PEER_HEADS, PEER_NKEYS, seg), BF16)
    rspec = pl.BlockSpec((1, PEER_HEADS, PEER_NKEYS, tm), lambda s, i: (s, 0, 0, i))
    return pl.pallas_call(
        functools.partial(_peer_route_kernel, tm=tm),
        grid=(nseg, seg // tm),
        in_specs=[pl.BlockSpec((1, tm, d), tok),
                  pl.BlockSpec((1, 1, d), lambda s, i: (s, 0, shift_chunk)),
                  pl.BlockSpec((1, 1, d), lambda s, i: (s, 0, shift_chunk + 1)),
                  pl.BlockSpec(wq_bf16.shape, lambda s, i: (0, 0)),
                  pl.BlockSpec((2 * PEER_HEADS, PEER_NKEYS, PEER_HALF), lambda s, i: (0, 0, 0))],
        out_specs=[pl.BlockSpec((1, tm, d), tok), rspec, rspec, rspec, rspec],
        out_shape=[jax.ShapeDtypeStruct((nseg, seg, d), BF16)] + [rshape] * 4,
        scratch_shapes=[pltpu.VMEM((2 * PEER_HEADS, tm, PEER_HALF), F32)],
        compiler_params=_params(("arbitrary", "arbitrary")),
        name="peer_route",
    )(x3, mod3, mod3, wq_bf16, subkeys.reshape(2 * PEER_HEADS, PEER_NKEYS, PEER_HALF))


PEER_CE = 1024


def _gelu_tanh(x):
    return 0.5 * x * (1.0 + jnp.tanh(0.7978845608028654 * (x + 0.044715 * x * x * x)))


def _peer_dense_kernel(xm_ref, u_ref, vt_ref, e1_ref, cut_ref, e2_ref, r2_ref, x_ref, gate_ref, g_ref, b_ref,
                       o_ref, acc_s, at_s, w_s, e2_s, r2_s, *, tm):
    e = pl.program_id(2)
    nb = PEER_CE // PEER_NKEYS
    ntt = tm // PEER_RT

    @pl.when(e == 0)
    def _():
        acc_s[...] = jnp.zeros_like(acc_s)
        e2_s[:, :, :tm] = e2_ref[0]
        r2_s[:, :, :tm] = r2_ref[0]

    packed = (PEER_NKEYS // 16, 16, PEER_RT)
    ng = 2

    def gate_tiles(tt, i0):
        tok = slice(tt * PEER_RT, (tt + 1) * PEER_RT)
        gmats = [jnp.zeros(packed, BF16) for _ in range(ng)]
        for h in range(PEER_HEADS):
            e2 = e2_s[h, :, tok].reshape(packed)
            r2 = r2_s[h, :, tok].reshape(packed)
            for k in range(ng):
                i = i0 + k
                e1 = jnp.broadcast_to(e1_ref[0, h, i:i + 1, tok], (16, PEER_RT))[None]
                cut = jnp.broadcast_to(cut_ref[0, h, i:i + 1, tok], (16, PEER_RT))[None]
                gmats[k] = gmats[k] + e1 * jnp.where(r2 < cut, e2, jnp.zeros_like(e2))
        for k in range(ng):
            rows = slice((i0 + k) * PEER_NKEYS, (i0 + k + 1) * PEER_NKEYS)
            act = _gelu_tanh(at_s[rows, tok].astype(BF16))
            w_s[rows, tok] = gmats[k].reshape(PEER_NKEYS, PEER_RT) * act

    at_s[:, :tm] = _dot(u_ref[0], xm_ref[0], ((1,), (1,)))
    for tt in range(ntt):
        for i0 in range(0, nb, ng):
            gate_tiles(tt, i0)
    acc_s[:, :tm] += _dot(vt_ref[0, 0], w_s[:, :tm])

    @pl.when(e == pl.num_programs(2) - 1)
    def _():
        z = DEEPNORM_ALPHA * x_ref[0] + gate_ref[0] * acc_s[:, :tm].T
        o_ref[0] = _layer_norm_rows(z, g_ref[...], b_ref[...])


def peer_dense(xm, u_all, vt_all, l, e1, cut, e2, r2, x3, mod3, gate_chunk, ln_g, ln_b, tm=1024):
    nseg, seg, d = x3.shape
    ne = u_all.shape[1]
    nb = PEER_CE // PEER_NKEYS
    tp = tm + PEER_RT
    tok = lambda s, i, e: (s, i, 0)
    chunk = pl.BlockSpec((1, PEER_HEADS, nb, tm), lambda s, i, e: (s, 0, e, i))
    full = pl.BlockSpec((1, PEER_HEADS, PEER_NKEYS, tm), lambda s, i, e: (s, 0, 0, i))
    return pl.pallas_call(
        functools.partial(_peer_dense_kernel, tm=tm),
        grid=(nseg, seg // tm, ne // PEER_CE),
        in_specs=[pl.BlockSpec((1, tm, d), tok),
                  pl.BlockSpec((1, PEER_CE, d), lambda s, i, e: (l, e, 0)),
                  pl.BlockSpec((1, 1, d, PEER_CE), lambda s, i, e: (l, e, 0, 0)),
                  chunk, chunk, full, full,
                  pl.BlockSpec((1, tm, d), tok),
                  pl.BlockSpec((1, 1, d), lambda s, i, e: (s, 0, gate_chunk)),
                  pl.BlockSpec((1, d), lambda s, i, e: (0, 0)),
                  pl.BlockSpec((1, d), lambda s, i, e: (0, 0))],
        out_specs=pl.BlockSpec((1, tm, d), tok),
        out_shape=jax.ShapeDtypeStruct((nseg, seg, d), F32),
        scratch_shapes=[pltpu.VMEM((d, tp), F32), pltpu.VMEM((PEER_CE, tp), F32), pltpu.VMEM((PEER_CE, tp), BF16),
                        pltpu.VMEM((PEER_HEADS, PEER_NKEYS, tp), BF16), pltpu.VMEM((PEER_HEADS, PEER_NKEYS, tp), BF16)],
        compiler_params=_params(("arbitrary", "arbitrary", "arbitrary")),
        name="peer_dense",
    )(xm, u_all, vt_all, e1, cut, e2, r2, x3, mod3, ln_g.reshape(1, d), ln_b.reshape(1, d))


def peer_layer(x3, mod3, l, wq, subkeys, u_all, vt_all, ln_g, ln_b):
    xm, e1, cut, e2, r2 = peer_route(x3, mod3, 3, wq.astype(BF16), subkeys)
    return peer_dense(xm, u_all, vt_all, l, e1, cut, e2, r2, x3, mod3, 5, ln_g, ln_b)


def _pad_cols(w, n):
    return jnp.pad(w, ((0, 0), (0, n - w.shape[1])))


def _stream(prompt_part, sample_part):
    return jnp.concatenate([prompt_part.reshape(1, -1, prompt_part.shape[-1]), sample_part], axis=0)


def _head_major(a, heads):
    b, t, _ = a.shape
    return jnp.transpose(a.reshape(b, t, heads, -1), (0, 2, 1, 3))


def _token_major(a):
    b, h, t, dh = a.shape
    return jnp.transpose(a, (0, 2, 1, 3)).reshape(b, t, h * dh)


MLSTM_CHUNK = 256
GLA_CHUNK = 64
NPROJ = 3200


def mlstm_layer(x3, mod3, bp, lp, st_c, st_n, st_m, w_in, b_gate, norm_w, w_out, ln_g, ln_b):
    nseg, seg, _ = x3.shape
    bs = nseg - 1
    p = mod_matmul(x3, mod3, 0, _pad_cols(w_in, NPROJ).astype(BF16))
    graw = p[:, :, 3072:3088]
    gp = graw[0].reshape(bp, lp, 16)
    zc = jnp.zeros((bp, 8, M_DK, M_DV), F32)
    zn = jnp.zeros((bp, 8, M_DK), F32)
    hfp, hbp, c_new, n_new, m_new = mlstm_scan(p.reshape(nseg * bp, lp, NPROJ), 0, bp, lp, gp,
                                               jnp.swapaxes(gp, 1, 2), b_gate, zc, zn, zn, min(MLSTM_CHUNK, lp),
                                               nseg * bp, 0)
    gs = graw[1:]
    hf, hb, _, _, _ = mlstm_scan(p, 1, bs, seg, gs, jnp.swapaxes(gs, 1, 2), b_gate,
                                 st_c.reshape(bs, 8, M_DK, M_DV), st_n.reshape(bs, 8, M_DK),
                                 jnp.broadcast_to(st_m.reshape(bs, 8, 1), (bs, 8, M_DK)), MLSTM_CHUNK,
                                 nseg, 1, prev=(hfp, hbp))
    x3 = outproj_ln("mlstm", (hf, hb), x3, mod3, 2, w_out.astype(BF16), ln_g, ln_b,
                    norm_w=norm_w, og=p, og_col=2)
    return (x3, c_new.reshape(bp, 2, M_HEADS, M_DK, M_DV), n_new.reshape(bp, 2, M_HEADS, M_DK),
            m_new[:, :, 0].reshape(bp, 2, M_HEADS))


def gla_layer(x3, mod3, bp, lp, st_s, w_in, w_gate2, b_gate2, norm_w, w_out, ln_g, ln_b):
    nseg, seg, _ = x3.shape
    bs = nseg - 1
    p = mod_matmul(x3, mod3, 0, _pad_cols(w_in, NPROJ).astype(BF16))
    gr = p[:, :, 3072:3104]
    zs = jnp.zeros((bp, 8, G_DV, G_DK), F32)
    ofp, obp, s_new = gla_scan(p.reshape(nseg * bp, lp, NPROJ), 0, bp, lp, gr[0].reshape(bp, lp, 32),
                               w_gate2, b_gate2, zs, GLA_CHUNK, nseg * bp, 0)
    s0t = jnp.swapaxes(st_s.reshape(bs, 8, G_DK, G_DV), -1, -2)
    of, ob, _ = gla_scan(p, 1, bs, seg, gr[1:], w_gate2, b_gate2, s0t, GLA_CHUNK, nseg, 1, prev=(ofp, obp))
    x3 = outproj_ln("gla", (of, ob), x3, mod3, 2, w_out.astype(BF16), ln_g, ln_b,
                    norm_w=jnp.tile(norm_w, G_HEADS), og=p, og_col=2)
    return x3, jnp.swapaxes(s_new, -1, -2).reshape(bp, 2, G_HEADS, G_DK, G_DV)


def na_layer(x3, mod3, bp, lp, cache_k, cache_v, w_in, rpb, w_out, ln_g, ln_b):
    nseg, seg, _ = x3.shape
    bs = nseg - 1
    hd = NA_HEADS * NA_HD
    p = mod_matmul(x3, mod3, 0, w_in.astype(BF16))
    pp = p[0].reshape(bp, lp, 3 * hd)
    hm = lambda a: _head_major(a, NA_HEADS).astype(BF16)
    yp = attention(hm(pp[..., :hd]), hm(pp[..., hd:2 * hd]), hm(pp[..., 2 * hd:]), lp)
    ps = p[1:]
    ys = na_attention(hm(ps[..., :hd]), hm(ps[..., hd:2 * hd]), hm(ps[..., 2 * hd:]),
                      hm(cache_k.reshape(bs, -1, hd)), hm(cache_v.reshape(bs, -1, hd)), na_bias_table(rpb))
    x3 = outproj_ln("plain", _stream(_token_major(yp), _token_major(ys)), x3, mod3, 2, w_out.astype(BF16), ln_g, ln_b)
    return (x3, pp[..., hd:2 * hd].reshape(bp, lp, NA_HEADS, NA_HD), pp[..., 2 * hd:].reshape(bp, lp, NA_HEADS, NA_HD))


def _rope_rotated_cols(w):
    q = MLA_ROPE // 4
    return jnp.concatenate([-w[..., q:2 * q], w[..., :q], -w[..., 3 * q:], w[..., 2 * q:3 * q]], axis=-1)


def _rope_tables(ts):
    ra = MLA_ROPE // 2
    t = np.arange(ts)
    inv = 1.0 / (ROPE_BASE ** (np.arange(0, ra, 2, dtype=np.float32) / ra))
    ang_r = (t // GRID_W).astype(np.float32)[:, None] * inv[None, :]
    ang_c = (t % GRID_W).astype(np.float32)[:, None] * inv[None, :]
    ang = np.concatenate([ang_r, ang_r, ang_c, ang_c], axis=-1).astype(np.float32)
    return jnp.cos(jnp.asarray(ang)), jnp.sin(jnp.asarray(ang))


def mla_layer(x3, mod3, bp, lp, cache_ckv, cache_kpe, w_in, q_norm, w_qup, kv_norm, w_kvup, w_out, ln_g, ln_b):
    nseg, seg, _ = x3.shape
    bs = nseg - 1
    nq = MLA_Q_LORA + MLA_KV_LORA
    w_ext = jnp.concatenate([w_in, _rope_rotated_cols(w_in[:, nq:])], axis=1)
    p = mod_matmul(x3, mod3, 0, _pad_cols(w_ext, 896).astype(BF16))
    cos_t, sin_t = _rope_tables(seg)
    cos3 = jnp.concatenate([jnp.ones((1, seg, MLA_ROPE), F32), jnp.broadcast_to(cos_t, (bs, seg, MLA_ROPE))], 0)
    sin3 = jnp.concatenate([jnp.zeros((1, seg, MLA_ROPE), F32), jnp.broadcast_to(sin_t, (bs, seg, MLA_ROPE))], 0)
    wq = w_qup.reshape(MLA_Q_LORA, MLA_HEADS, MLA_NOPE + MLA_ROPE)
    wq_rope = wq[:, :, MLA_NOPE:]
    w_q3 = jnp.concatenate([wq[:, :, :MLA_NOPE].reshape(MLA_Q_LORA, -1), wq_rope.reshape(MLA_Q_LORA, -1),
                            _rope_rotated_cols(wq_rope).reshape(MLA_Q_LORA, -1)], axis=1).astype(BF16)
    q_all = mla_q(p, q_norm, w_q3, jnp.tile(cos3, (1, 1, MLA_HEADS)), jnp.tile(sin3, (1, 1, MLA_HEADS)))
    wkv = w_kvup.reshape(MLA_KV_LORA, MLA_HEADS, MLA_NOPE + MLA_VD)
    w_kv2 = jnp.concatenate([wkv[:, :, :MLA_NOPE].reshape(MLA_KV_LORA, -1),
                             wkv[:, :, MLA_NOPE:].reshape(MLA_KV_LORA, -1)], axis=1).astype(BF16)
    ckvn, kpe, kv = mla_kv(p, kv_norm, w_kv2, cos3, sin3)
    kvc = matmul(cache_ckv.reshape(-1, MLA_KV_LORA), w_kv2, 512).reshape(bs, -1, w_kv2.shape[1])
    nn = MLA_HEADS * MLA_NOPE

    def heads(q_rows, kv_rows, kpe_rows):
        b, t, _ = q_rows.shape
        tk = kv_rows.shape[1]
        qh = jnp.concatenate([q_rows[..., :nn].reshape(b, t, MLA_HEADS, MLA_NOPE),
                              q_rows[..., nn:].reshape(b, t, MLA_HEADS, MLA_ROPE)], -1)
        kh = jnp.concatenate([kv_rows[..., :nn].reshape(b, tk, MLA_HEADS, MLA_NOPE),
                              jnp.broadcast_to(kpe_rows[:, :, None, :], (b, tk, MLA_HEADS, MLA_ROPE))], -1)
        vh = kv_rows[..., nn:].reshape(b, tk, MLA_HEADS, MLA_VD)
        tr = lambda a: jnp.transpose(a, (0, 2, 1, 3)).astype(BF16)
        return tr(qh), tr(kh), tr(vh)

    yp = attention(*heads(q_all[0].reshape(bp, lp, -1), kv[0].reshape(bp, lp, -1), kpe[0].reshape(bp, lp, -1)), lp)
    ys = attention(*heads(q_all[1:], jnp.concatenate([kv[1:], kvc], 1), jnp.concatenate([kpe[1:], cache_kpe], 1)), 256)
    x3 = outproj_ln("plain", _stream(_token_major(yp), _token_major(ys)), x3, mod3, 2, w_out.astype(BF16), ln_g, ln_b)
    return x3, ckvn[0].reshape(bp, lp, MLA_KV_LORA), kpe[0].reshape(bp, lp, MLA_ROPE)


def kernel(x_prompt, x_sample, c, c_ctx, state_mlstm_C, state_mlstm_n, state_mlstm_m, state_gla_S, cache_na_k, cache_na_v, cache_mla_ckv, cache_mla_kpe, ada_w, ada_b, ln_mix_g, ln_mix_b, ln_ffn_g, ln_ffn_b, mlstm_w_in, mlstm_b_gate, mlstm_norm_w, mlstm_w_out, gla_w_in, gla_w_gate2, gla_b_gate2, gla_norm_w, gla_w_out, na_w_in, na_rpb, na_w_out, mla_w_in, mla_q_norm, mla_w_qup, mla_kv_norm, mla_w_kvup, mla_w_out, peer_w_q, peer_subkeys, peer_u, peer_v):
    bp, lp, d = x_prompt.shape
    bs, ts, _ = x_sample.shape
    assert bp * lp == ts and bs + 1 <= 8
    x3 = _stream(x_prompt, x_sample)
    cond8 = jnp.zeros((8, d), F32).at[0].set(c_ctx).at[1:1 + bs].set(c)
    mods = adaln_all(cond8, ada_w, ada_b)
    u_all = peer_u.astype(BF16)
    vt_all = jnp.swapaxes(peer_v.reshape(DEPTH, -1, PEER_CE, d), 2, 3).astype(BF16)
    outs = {}
    for l in range(DEPTH):
        mod3 = mods[l].reshape(8, 1, ADA_CHUNKS * d)
        kind = l % 4
        if kind == 0:
            x3, outs["C"], outs["n"], outs["m"] = mlstm_layer(
                x3, mod3, bp, lp, state_mlstm_C, state_mlstm_n, state_mlstm_m, mlstm_w_in, mlstm_b_gate,
                mlstm_norm_w, mlstm_w_out, ln_mix_g[l], ln_mix_b[l])
        elif kind == 1:
            x3, outs["S"] = gla_layer(x3, mod3, bp, lp, state_gla_S, gla_w_in, gla_w_gate2, gla_b_gate2,
                                      gla_norm_w, gla_w_out, ln_mix_g[l], ln_mix_b[l])
        elif kind == 2:
            x3, outs["nk"], outs["nv"] = na_layer(x3, mod3, bp, lp, cache_na_k, cache_na_v, na_w_in, na_rpb,
                                                  na_w_out, ln_mix_g[l], ln_mix_b[l])
        else:
            x3, outs["ckv"], outs["kpe"] = mla_layer(x3, mod3, bp, lp, cache_mla_ckv, cache_mla_kpe, mla_w_in,
                                                     mla_q_norm, mla_w_qup, mla_kv_norm, mla_w_kvup, mla_w_out,
                                                     ln_mix_g[l], ln_mix_b[l])
        x3 = peer_layer(x3, mod3, l, peer_w_q[l], peer_subkeys[l], u_all, vt_all, ln_ffn_g[l], ln_ffn_b[l])
    return (x3[0].reshape(bp, lp, d), x3[1:], outs["C"], outs["n"], outs["m"], outs["S"], outs["nk"], outs["nv"],
            outs["ckv"], outs["kpe"])
```

```python
import functools

import numpy as np
import jax
import jax.numpy as jnp
from jax import lax
from jax.experimental import pallas as pl
from jax.experimental.pallas import tpu as pltpu

D_MODEL = 1024
DEPTH = 4
GRID_W = 64
DEEPNORM_ALPHA = (2.0 * DEPTH) ** 0.25
ADA_CHUNKS = 6
NORM_EPS = 1e-5
SEG = 4096
NSEG = 3

M_HEADS, M_DK, M_DV = 4, 128, 256
G_HEADS, G_DK, G_DV = 4, 128, 256
G_GATE_RANK = 16
G_GATE_NORM = 16.0
NA_HEADS, NA_HD, NA_ROWS, NA_COLS = 16, 64, 8, 16
MLA_HEADS, MLA_Q_LORA, MLA_KV_LORA, MLA_NOPE, MLA_ROPE, MLA_VD = 16, 512, 256, 64, 32, 64
ROPE_BASE = 10000.0
PEER_HEADS, PEER_NKEYS, PEER_HALF, PEER_TOPK = 8, 128, 128, 16

V7X_VMEM_LIMIT = 56 * 1024 * 1024
F32 = jnp.float32
BF16 = jnp.bfloat16
NEG_INF = float("-inf")


def _params(sem, vmem=V7X_VMEM_LIMIT):
    return pltpu.CompilerParams(dimension_semantics=sem, vmem_limit_bytes=vmem)


def _dot(a, b, dims=((1,), (0,))):
    return lax.dot_general(a, b, (dims, ((), ())), preferred_element_type=F32)


def _split3(a):
    hi = a.astype(BF16)
    r1 = a - hi.astype(F32)
    mid = r1.astype(BF16)
    lo = (r1 - mid.astype(F32)).astype(BF16)
    return hi, mid, lo


def _dot_exact_lhs(m01, a):
    hi, mid, lo = _split3(a)
    return _dot(m01, hi) + _dot(m01, mid) + _dot(m01, lo)


def _dot_exact_rhs(a, m01):
    hi, mid, lo = _split3(a)
    return _dot(hi, m01) + _dot(mid, m01) + _dot(lo, m01)


def _log_sigmoid(x):
    return jnp.minimum(x, 0.0) - jnp.log(1.0 + jnp.exp(-jnp.abs(x)))


def _sigmoid(x):
    return 1.0 / (1.0 + jnp.exp(-x))


def _adaln_kernel(c_ref, w_ref, b_ref, o_ref):
    cv = c_ref[...]
    a = cv * _sigmoid(cv)
    o_ref[0] = lax.dot_general(a, w_ref[0], (((1,), (0,)), ((), ())), precision=lax.Precision.HIGHEST,
                               preferred_element_type=F32) + b_ref[0]


def adaln_all(cond8, ada_w, ada_b):
    tn = 1024
    n = ada_w.shape[-1]
    return pl.pallas_call(
        _adaln_kernel,
        grid=(DEPTH, n // tn),
        in_specs=[pl.BlockSpec((8, D_MODEL), lambda l, j: (0, 0)),
                  pl.BlockSpec((1, D_MODEL, tn), lambda l, j: (l, 0, j)),
                  pl.BlockSpec((1, 1, tn), lambda l, j: (l, 0, j))],
        out_specs=pl.BlockSpec((1, 8, tn), lambda l, j: (l, 0, j)),
        out_shape=jax.ShapeDtypeStruct((DEPTH, 8, n), F32),
        compiler_params=_params(("arbitrary", "arbitrary")),
        name="adaln",
    )(cond8, ada_w, ada_b.reshape(DEPTH, 1, n))


def _modmm_kernel(x_ref, sh_ref, sc_ref, w_ref, o_ref, xm_ref):
    @pl.when(pl.program_id(2) == 0)
    def _():
        xm_ref[...] = (x_ref[0] * (1.0 + sc_ref[0]) + sh_ref[0]).astype(BF16)

    o_ref[0] = _dot(xm_ref[...], w_ref[...]).astype(o_ref.dtype)


def mod_matmul(x3, mod3, shift_chunk, w_bf16, tm=512, tn=None, out_dtype=F32):
    nseg, seg, d = x3.shape
    n = w_bf16.shape[1]
    tn = n if tn is None else tn
    return pl.pallas_call(
        _modmm_kernel,
        grid=(nseg, seg // tm, n // tn),
        in_specs=[pl.BlockSpec((1, tm, d), lambda s, i, j: (s, i, 0)),
                  pl.BlockSpec((1, 1, d), lambda s, i, j: (s, 0, shift_chunk)),
                  pl.BlockSpec((1, 1, d), lambda s, i, j: (s, 0, shift_chunk + 1)),
                  pl.BlockSpec((d, tn), lambda s, i, j: (0, j))],
        out_specs=pl.BlockSpec((1, tm, tn), lambda s, i, j: (s, i, j)),
        out_shape=jax.ShapeDtypeStruct((nseg, seg, n), out_dtype),
        scratch_shapes=[pltpu.VMEM((tm, d), BF16)],
        compiler_params=_params(("arbitrary", "arbitrary", "arbitrary")),
        name="mod_matmul",
    )(x3, mod3, mod3, w_bf16)


def _layer_norm_rows(y, g, b):
    mu = jnp.mean(y, axis=-1, keepdims=True)
    yc = y - mu
    var = jnp.mean(yc * yc, axis=-1, keepdims=True)
    return yc * lax.rsqrt(var + NORM_EPS) * g + b


def _outproj_kernel(*refs, mode):
    if mode == "plain":
        y_ref, x_ref, gate_ref, w_ref, g_ref, b_ref, o_ref = refs
        yin = y_ref[0].astype(BF16)
    else:
        ya_ref, yb_ref, og_ref, nw_ref, x_ref, gate_ref, w_ref, g_ref, b_ref, o_ref = refs
        hs = ya_ref[0] + yb_ref[0]
        og = og_ref[0]
        parts = []
        for h in range(4):
            seg = hs[:, h * 256:(h + 1) * 256]
            nw = nw_ref[:, h * 256:(h + 1) * 256]
            if mode == "mlstm":
                mu = jnp.mean(seg, axis=-1, keepdims=True)
                sc = seg - mu
                var = jnp.mean(sc * sc, axis=-1, keepdims=True)
                parts.append(sc * lax.rsqrt(var + NORM_EPS) * nw)
            else:
                ms = jnp.mean(seg * seg, axis=-1, keepdims=True)
                parts.append(seg * lax.rsqrt(ms + NORM_EPS) * nw)
        hn = jnp.concatenate(parts, axis=-1)
        act = _sigmoid(og) if mode == "mlstm" else og * _sigmoid(og)
        yin = (act * hn).astype(BF16)
    y = _dot(yin, w_ref[...])
    z = DEEPNORM_ALPHA * x_ref[0] + gate_ref[0] * y
    o_ref[0] = _layer_norm_rows(z, g_ref[...], b_ref[...])


def outproj_ln(mode, ys, x3, mod3, gate_chunk, w_bf16, ln_g, ln_b, norm_w=None, og=None, og_col=0, tm=512):
    nseg, seg, d = x3.shape
    k = w_bf16.shape[0]
    tok = lambda s, i: (s, i, 0)
    if mode == "plain":
        args = [ys]
        specs = [pl.BlockSpec((1, tm, k), tok)]
    else:
        args = [ys[0], ys[1], og, norm_w.reshape(1, k)]
        specs = [pl.BlockSpec((1, tm, k), tok), pl.BlockSpec((1, tm, k), tok),
                 pl.BlockSpec((1, tm, k), lambda s, i: (s, i, og_col)),
                 pl.BlockSpec((1, k), lambda s, i: (0, 0))]
    args += [x3, mod3, w_bf16, ln_g.reshape(1, d), ln_b.reshape(1, d)]
    specs += [pl.BlockSpec((1, tm, d), tok),
              pl.BlockSpec((1, 1, d), lambda s, i: (s, 0, gate_chunk)),
              pl.BlockSpec((k, d), lambda s, i: (0, 0)),
              pl.BlockSpec((1, d), lambda s, i: (0, 0)),
              pl.BlockSpec((1, d), lambda s, i: (0, 0))]
    return pl.pallas_call(
        functools.partial(_outproj_kernel, mode=mode),
        grid=(nseg, seg // tm),
        in_specs=specs,
        out_specs=pl.BlockSpec((1, tm, d), tok),
        out_shape=jax.ShapeDtypeStruct((nseg, seg, d), F32),
        compiler_params=_params(("arbitrary", "arbitrary")),
        name="outproj_ln_" + mode,
    )(*args)


def _tri(n, lower):
    r = lax.broadcasted_iota(jnp.int32, (n, n), 0)
    c = lax.broadcasted_iota(jnp.int32, (n, n), 1)
    return (c <= r) if lower else (c >= r)


def _mlstm_kernel(pf_ref, pb_ref, gf_ref, gb_ref, gtf_ref, gtb_ref, bias_ref, biast_ref,
                  c0_ref, n0_ref, m0_ref, hf_ref, hb_ref, co_ref, no_ref, mo_ref,
                  c_s, n_s, m_s, *, L):
    c = pl.program_id(1)

    @pl.when(c == 0)
    def _():
        c_s[...] = c0_ref[0]
        n_s[...] = n0_ref[0]
        m_s[...] = m0_ref[0]

    for d in range(2):
        p_ref, g_ref, gt_ref, h_ref = ((pf_ref, gf_ref, gtf_ref, hf_ref) if d == 0
                                       else (pb_ref, gb_ref, gtb_ref, hb_ref))
        mask = _tri(L, lower=(d == 0))
        mcol = mask.astype(BF16)
        mrow = _tri(L, lower=(d != 0)).astype(BF16)
        g = g_ref[0] + bias_ref[...]
        gt = gt_ref[0] + biast_ref[...]
        li_c = g[:, d * 8:d * 8 + 4]
        lf_c = _log_sigmoid(g[:, d * 8 + 4:d * 8 + 8])
        li_r = gt[d * 8:d * 8 + 4, :]
        lf_r = _log_sigmoid(gt[d * 8 + 4:d * 8 + 8, :])
        b_c = _dot_exact_lhs(mcol, lf_c)
        b_r = _dot_exact_rhs(lf_r, mrow)
        last = L - 1 if d == 0 else 0
        for h in range(M_HEADS):
            u = d * M_HEADS + h
            q = p_ref[0, :, h * M_DK:(h + 1) * M_DK]
            k = p_ref[0, :, 512 + h * M_DK:512 + (h + 1) * M_DK] * (M_DK ** -0.5)
            v = p_ref[0, :, 1024 + h * M_DV:1024 + (h + 1) * M_DV].astype(BF16)
            qb = q.astype(BF16)
            bc, br = b_c[:, h:h + 1], b_r[h:h + 1, :]
            lic, lir = li_c[:, h:h + 1], li_r[h:h + 1, :]
            m_prev = m_s[u:u + 1, 0:1]
            dmat = jnp.where(mask, bc - br + lir, NEG_INF)
            inter = bc + m_prev
            mt = jnp.maximum(inter, jnp.max(dmat, axis=-1, keepdims=True))
            smat = _dot(qb, k.astype(BF16), ((1,), (1,))) * jnp.exp(dmat - mt)
            ei = jnp.exp(inter - mt)
            cmat = c_s[u]
            num = _dot(smat.astype(BF16), v) + ei * _dot(qb, cmat.astype(BF16))
            nrow = n_s[u:u + 1, :]
            den = jnp.sum(smat, axis=-1, keepdims=True) + ei * jnp.sum(q * nrow, axis=-1, keepdims=True)
            h_ref[0, :, h * M_DV:(h + 1) * M_DV] = num / jnp.maximum(jnp.abs(den), jnp.exp(-mt))
            tot = br[:, last:last + 1]
            g_c = tot - bc + lic
            g_r = tot - br + lir
            m_new = jnp.maximum(tot + m_prev, jnp.max(g_r, axis=-1, keepdims=True))
            kw = k * jnp.exp(g_c - m_new)
            dec = jnp.exp(tot + m_prev - m_new)
            c_s[u] = dec * cmat + _dot(kw.astype(BF16), v, ((0,), (0,)))
            n_s[u:u + 1, :] = dec * nrow + jnp.sum(kw, axis=0, keepdims=True)
            m_s[u:u + 1, :] = jnp.broadcast_to(m_new, (1, 128))

    @pl.when(c == pl.num_programs(1) - 1)
    def _():
        co_ref[0] = c_s[...]
        no_ref[0] = n_s[...]
        mo_ref[0] = m_s[...]


def _skip_refs(body, start, n):
    def kernel(*refs):
        return body(*refs[:start], *refs[start + n:])
    return kernel


def _stream_out_args(out_rows, t, width, prev):
    shape = jax.ShapeDtypeStruct((out_rows, t, width), F32)
    if prev is None:
        return shape, [], []
    return shape, [a.reshape(out_rows, t, width) for a in prev], [pl.BlockSpec(memory_space=pl.ANY)] * len(prev)


def mlstm_scan(p, b0, nb, t, g, gt, bias, c0, n0, m0, L, out_rows, ob0, prev=None):
    nc = t // L
    hshape, prev_args, prev_specs = _stream_out_args(out_rows, t, M_HEADS * M_DV, prev)
    n_in = 11
    fwd = lambda b, c: (b + b0, c, 0)
    bwd = lambda b, c: (b + b0, nc - 1 - c, 0)
    st4 = lambda b, c: (b, 0, 0, 0)
    st3 = lambda b, c: (b, 0, 0)
    return pl.pallas_call(
        _skip_refs(functools.partial(_mlstm_kernel, L=L), n_in, len(prev_args)),
        grid=(nb, nc),
        input_output_aliases={n_in + i: i for i in range(len(prev_args))},
        in_specs=[pl.BlockSpec((1, L, 2048), fwd), pl.BlockSpec((1, L, 2048), bwd),
                  pl.BlockSpec((1, L, 16), lambda b, c: (b, c, 0)),
                  pl.BlockSpec((1, L, 16), lambda b, c: (b, nc - 1 - c, 0)),
                  pl.BlockSpec((1, 16, L), lambda b, c: (b, 0, c)),
                  pl.BlockSpec((1, 16, L), lambda b, c: (b, 0, nc - 1 - c)),
                  pl.BlockSpec((1, 16), lambda b, c: (0, 0)),
                  pl.BlockSpec((16, 1), lambda b, c: (0, 0)),
                  pl.BlockSpec((1, 8, M_DK, M_DV), st4),
                  pl.BlockSpec((1, 8, M_DK), st3),
                  pl.BlockSpec((1, 8, M_DK), st3)] + prev_specs,
        out_specs=[pl.BlockSpec((1, L, 1024), lambda b, c: (b + ob0, c, 0)),
                   pl.BlockSpec((1, L, 1024), lambda b, c: (b + ob0, nc - 1 - c, 0)),
                   pl.BlockSpec((1, 8, M_DK, M_DV), st4),
                   pl.BlockSpec((1, 8, M_DK), st3),
                   pl.BlockSpec((1, 8, M_DK), st3)],
        out_shape=[hshape, hshape,
                   jax.ShapeDtypeStruct((nb, 8, M_DK, M_DV), F32),
                   jax.ShapeDtypeStruct((nb, 8, M_DK), F32),
                   jax.ShapeDtypeStruct((nb, 8, M_DK), F32)],
        scratch_shapes=[pltpu.VMEM((8, M_DK, M_DV), F32), pltpu.VMEM((8, M_DK), F32),
                        pltpu.VMEM((8, M_DK), F32)],
        compiler_params=_params(("arbitrary", "arbitrary")),
        name="mlstm_scan",
    )(p, p, g, g, gt, gt, bias.reshape(1, 16), bias.reshape(16, 1), c0, n0, m0, *prev_args)


def _gla_kernel(pf_ref, pb_ref, gf_ref, gb_ref, w2_ref, b2_ref, s0_ref, of_ref, ob_ref, so_ref, s_s, *, L):
    c = pl.program_id(1)

    @pl.when(c == 0)
    def _():
        s_s[...] = s0_ref[0]

    for d in range(2):
        p_ref, g_ref, o_ref = (pf_ref, gf_ref, of_ref) if d == 0 else (pb_ref, gb_ref, ob_ref)
        mask = _tri(L, lower=(d == 0))
        mcol = mask.astype(BF16)
        gr = g_ref[0][:, d * G_GATE_RANK:(d + 1) * G_GATE_RANK]
        pre = lax.dot_general(gr, w2_ref[d], (((1,), (0,)), ((), ())), precision=lax.Precision.HIGHEST,
                              preferred_element_type=F32) + b2_ref[d]
        la = _log_sigmoid(pre) * (1.0 / G_GATE_NORM)
        bc_all = _dot_exact_lhs(mcol, la)
        last = L - 1 if d == 0 else 0
        for h in range(G_HEADS):
            u = d * G_HEADS + h
            q = p_ref[0, :, h * G_DK:(h + 1) * G_DK] * (G_DK ** -0.5)
            k = p_ref[0, :, 512 + h * G_DK:512 + (h + 1) * G_DK]
            v = p_ref[0, :, 1024 + h * G_DV:1024 + (h + 1) * G_DV].astype(BF16)
            bc = bc_all[:, h * G_DK:(h + 1) * G_DK]
            qd = (q * jnp.exp(bc)).astype(BF16)
            kd = (k * jnp.exp(-bc)).astype(BF16)
            a = jnp.where(mask, _dot(qd, kd, ((1,), (1,))), 0.0)
            st = s_s[u]
            o_ref[0, :, h * G_DV:(h + 1) * G_DV] = (_dot(a.astype(BF16), v)
                                                    + _dot(qd, st.astype(BF16), ((1,), (1,))))
            bl = bc[last:last + 1, :]
            kl = (k * jnp.exp(bl - bc)).astype(BF16)
            s_s[u] = st * jnp.exp(bl) + _dot(v, kl, ((0,), (0,)))

    @pl.when(c == pl.num_programs(1) - 1)
    def _():
        so_ref[0] = s_s[...]


def gla_scan(p, b0, nb, t, gr, w2, b2, s0t, L, out_rows, ob0, prev=None):
    nc = t // L
    oshape, prev_args, prev_specs = _stream_out_args(out_rows, t, G_HEADS * G_DV, prev)
    n_in = 7
    st4 = lambda b, c: (b, 0, 0, 0)
    return pl.pallas_call(
        _skip_refs(functools.partial(_gla_kernel, L=L), n_in, len(prev_args)),
        grid=(nb, nc),
        input_output_aliases={n_in + i: i for i in range(len(prev_args))},
        in_specs=[pl.BlockSpec((1, L, 2048), lambda b, c: (b + b0, c, 0)),
                  pl.BlockSpec((1, L, 2048), lambda b, c: (b + b0, nc - 1 - c, 0)),
                  pl.BlockSpec((1, L, 32), lambda b, c: (b, c, 0)),
                  pl.BlockSpec((1, L, 32), lambda b, c: (b, nc - 1 - c, 0)),
                  pl.BlockSpec((2, G_GATE_RANK, 512), lambda b, c: (0, 0, 0)),
                  pl.BlockSpec((2, 1, 512), lambda b, c: (0, 0, 0)),
                  pl.BlockSpec((1, 8, G_DV, G_DK), st4)] + prev_specs,
        out_specs=[pl.BlockSpec((1, L, 1024), lambda b, c: (b + ob0, c, 0)),
                   pl.BlockSpec((1, L, 1024), lambda b, c: (b + ob0, nc - 1 - c, 0)),
                   pl.BlockSpec((1, 8, G_DV, G_DK), st4)],
        out_shape=[oshape, oshape, jax.ShapeDtypeStruct((nb, 8, G_DV, G_DK), F32)],
        scratch_shapes=[pltpu.VMEM((8, G_DV, G_DK), F32)],
        compiler_params=_params(("arbitrary", "arbitrary")),
        name="gla_scan",
    )(p, p, gr, gr, w2, b2.reshape(2, 1, 512), s0t, *prev_args)


ATTN_HEADS_PER_STEP = 2


def _attn_kernel(q_ref, k_ref, v_ref, o_ref, *, scale):
    for j in range(ATTN_HEADS_PER_STEP):
        s = _dot(q_ref[0, j], k_ref[0, j], ((1,), (1,))) * scale
        m = jnp.max(s, axis=-1, keepdims=True)
        p = jnp.exp(s - m)
        l = jnp.sum(p, axis=-1, keepdims=True)
        o_ref[0, j] = _dot(p.astype(BF16), v_ref[0, j]) / l


def attention(q, k, v, tq):
    b, h, lq, dq = q.shape
    lk, dv = k.shape[2], v.shape[3]
    hb = ATTN_HEADS_PER_STEP
    return pl.pallas_call(
        functools.partial(_attn_kernel, scale=dq ** -0.5),
        grid=(b, h // hb, lq // tq),
        in_specs=[pl.BlockSpec((1, hb, tq, dq), lambda b, h, i: (b, h, i, 0)),
                  pl.BlockSpec((1, hb, lk, dq), lambda b, h, i: (b, h, 0, 0)),
                  pl.BlockSpec((1, hb, lk, dv), lambda b, h, i: (b, h, 0, 0))],
        out_specs=pl.BlockSpec((1, hb, tq, dv), lambda b, h, i: (b, h, i, 0)),
        out_shape=jax.ShapeDtypeStruct((b, h, lq, dv), F32),
        compiler_params=_params(("arbitrary", "arbitrary", "arbitrary")),
        name="attention",
    )(q, k, v)


NA_RB = 8


def _na_kernel(q_ref, k_ref, v_ref, kc_ref, vc_ref, bias_ref, o_ref, *, rows):
    j = pl.program_id(2)
    scale = NA_HD ** -0.5
    kc, vc = kc_ref[0, 0], vc_ref[0, 0]
    for a in range(NA_RB):
        r = j * NA_RB + a
        start = jnp.clip(r - NA_ROWS // 2, 0, rows - NA_ROWS)
        dr0 = start - r + (NA_ROWS - 1)
        off = pl.multiple_of(start * GRID_W, GRID_W)
        qa = q_ref[0, 0, a * GRID_W:(a + 1) * GRID_W, :]
        kl = k_ref[0, 0, pl.ds(off, NA_ROWS * GRID_W), :]
        vl = v_ref[0, 0, pl.ds(off, NA_ROWS * GRID_W), :]
        s_loc = _dot(qa, kl, ((1,), (1,))) * scale + bias_ref[0, dr0]
        s_ctx = _dot(qa, kc, ((1,), (1,))) * scale
        m = jnp.maximum(jnp.max(s_loc, axis=-1, keepdims=True), jnp.max(s_ctx, axis=-1, keepdims=True))
        p_loc = jnp.exp(s_loc - m)
        p_ctx = jnp.exp(s_ctx - m)
        l = jnp.sum(p_loc, axis=-1, keepdims=True) + jnp.sum(p_ctx, axis=-1, keepdims=True)
        o = _dot(p_loc.astype(BF16), vl) + _dot(p_ctx.astype(BF16), vc)
        o_ref[0, 0, a * GRID_W:(a + 1) * GRID_W, :] = o / l


def na_bias_table(rpb):
    cq = np.arange(GRID_W)[:, None]
    ck = np.arange(GRID_W)[None, :]
    cs = np.clip(cq - NA_COLS // 2, 0, GRID_W - NA_COLS)
    ok = (ck >= cs) & (ck < cs + NA_COLS)
    dc = np.clip(ck - cq, -(NA_COLS - 1), NA_COLS - 1) + (NA_COLS - 1)
    t = jnp.where(ok[None, None], rpb.astype(F32)[:, :, dc], NEG_INF)
    rows = np.arange(NA_ROWS)[:, None] + np.arange(NA_ROWS)[None, :]
    tf = t[:, rows]
    return jnp.transpose(tf, (0, 1, 3, 2, 4)).reshape(NA_HEADS, NA_ROWS, GRID_W, NA_ROWS * GRID_W)


def na_attention(q, k, v, kc, vc, bias):
    b, h, t, dh = q.shape
    lc = kc.shape[2]
    rows = t // GRID_W
    full = lambda b, h, j: (b, h, 0, 0)
    return pl.pallas_call(
        functools.partial(_na_kernel, rows=rows),
        grid=(b, h, rows // NA_RB),
        in_specs=[pl.BlockSpec((1, 1, NA_RB * GRID_W, dh), lambda b, h, j: (b, h, j, 0)),
                  pl.BlockSpec((1, 1, t, dh), full), pl.BlockSpec((1, 1, t, dh), full),
                  pl.BlockSpec((1, 1, lc, dh), full), pl.BlockSpec((1, 1, lc, dh), full),
                  pl.BlockSpec((1, NA_ROWS, GRID_W, NA_ROWS * GRID_W), lambda b, h, j: (h, 0, 0, 0))],
        out_specs=pl.BlockSpec((1, 1, NA_RB * GRID_W, dh), lambda b, h, j: (b, h, j, 0)),
        out_shape=jax.ShapeDtypeStruct((b, h, t, dh), F32),
        compiler_params=_params(("arbitrary", "arbitrary", "arbitrary")),
        name="na_attention",
    )(q, k, v, kc, vc, bias)


def _rms_rows(x, g):
    return x * lax.rsqrt(jnp.mean(x * x, axis=-1, keepdims=True) + NORM_EPS) * g


def _mla_q_kernel(cq_ref, g_ref, w_ref, cos_ref, sin_ref, o_ref):
    r = _dot(_rms_rows(cq_ref[0], g_ref[...]).astype(BF16), w_ref[...])
    nn = MLA_HEADS * MLA_NOPE
    nr = MLA_HEADS * MLA_ROPE
    o_ref[0, :, :nn] = r[:, :nn]
    o_ref[0, :, nn:] = r[:, nn:nn + nr] * cos_ref[0] + r[:, nn + nr:] * sin_ref[0]


def mla_q(p, q_norm, w_q3, cos_q, sin_q, tm=512):
    nseg, seg, _ = p.shape
    nout = MLA_HEADS * (MLA_NOPE + MLA_ROPE)
    nr = MLA_HEADS * MLA_ROPE
    tok = lambda s, i: (s, i, 0)
    return pl.pallas_call(
        _mla_q_kernel,
        grid=(nseg, seg // tm),
        in_specs=[pl.BlockSpec((1, tm, MLA_Q_LORA), tok),
                  pl.BlockSpec((1, MLA_Q_LORA), lambda s, i: (0, 0)),
                  pl.BlockSpec(w_q3.shape, lambda s, i: (0, 0)),
                  pl.BlockSpec((1, tm, nr), tok), pl.BlockSpec((1, tm, nr), tok)],
        out_specs=pl.BlockSpec((1, tm, nout), tok),
        out_shape=jax.ShapeDtypeStruct((nseg, seg, nout), F32),
        compiler_params=_params(("arbitrary", "arbitrary")),
        name="mla_q",
    )(p, q_norm.reshape(1, -1), w_q3, cos_q, sin_q)


def _mla_kv_kernel(ckv_ref, kpe_ref, g_ref, w_ref, cos_ref, sin_ref, ckvn_ref, kpeo_ref, kv_ref):
    cn = _rms_rows(ckv_ref[0], g_ref[...])
    ckvn_ref[0] = cn
    kv_ref[0] = _dot(cn.astype(BF16), w_ref[...])
    kp = kpe_ref[0]
    kpeo_ref[0] = kp[:, :MLA_ROPE] * cos_ref[0] + kp[:, MLA_ROPE:2 * MLA_ROPE] * sin_ref[0]


def mla_kv(p, kv_norm, w_kv, cos_k, sin_k, tm=512):
    nseg, seg, _ = p.shape
    nkv = w_kv.shape[1]
    tok = lambda s, i: (s, i, 0)
    return pl.pallas_call(
        _mla_kv_kernel,
        grid=(nseg, seg // tm),
        in_specs=[pl.BlockSpec((1, tm, MLA_KV_LORA), lambda s, i: (s, i, MLA_Q_LORA // MLA_KV_LORA)),
                  pl.BlockSpec((1, tm, 128), lambda s, i: (s, i, (MLA_Q_LORA + MLA_KV_LORA) // 128)),
                  pl.BlockSpec((1, MLA_KV_LORA), lambda s, i: (0, 0)),
                  pl.BlockSpec(w_kv.shape, lambda s, i: (0, 0)),
                  pl.BlockSpec((1, tm, MLA_ROPE), tok), pl.BlockSpec((1, tm, MLA_ROPE), tok)],
        out_specs=[pl.BlockSpec((1, tm, MLA_KV_LORA), tok), pl.BlockSpec((1, tm, MLA_ROPE), tok),
                   pl.BlockSpec((1, tm, nkv), tok)],
        out_shape=[jax.ShapeDtypeStruct((nseg, seg, MLA_KV_LORA), F32),
                   jax.ShapeDtypeStruct((nseg, seg, MLA_ROPE), F32),
                   jax.ShapeDtypeStruct((nseg, seg, nkv), F32)],
        compiler_params=_params(("arbitrary", "arbitrary")),
        name="mla_kv",
    )(p, p, kv_norm.reshape(1, -1), w_kv, cos_k, sin_k)


def _mm_kernel(a_ref, w_ref, o_ref):
    o_ref[...] = _dot(a_ref[...].astype(BF16), w_ref[...])


def matmul(a, w_bf16, tm):
    m, k = a.shape
    n = w_bf16.shape[1]
    return pl.pallas_call(
        _mm_kernel,
        grid=(m // tm,),
        in_specs=[pl.BlockSpec((tm, k), lambda i: (i, 0)), pl.BlockSpec((k, n), lambda i: (0, 0))],
        out_specs=pl.BlockSpec((tm, n), lambda i: (i, 0)),
        out_shape=jax.ShapeDtypeStruct((m, n), F32),
        compiler_params=_params(("arbitrary",)),
        name="matmul",
    )(a, w_bf16)


PEER_RT = 128
NOT_TOP = 99.0
RANK_CODE = 2.0 ** 100


def _top16(s, exact):
    vals = []
    if exact:
        key = lax.broadcasted_iota(jnp.int32, s.shape, 0).astype(F32)
        rank = jnp.full(s.shape, NOT_TOP, F32)
        for r in range(PEER_TOPK):
            m = jnp.max(s, axis=0, keepdims=True)
            hit = key == jnp.min(jnp.where(s == m, key, 1e9), axis=0, keepdims=True)
            rank = jnp.where(hit, float(r), rank)
            s = jnp.where(hit, NEG_INF, s)
            vals.append(m)
        return vals, rank
    for r in range(PEER_TOPK):
        m = jnp.max(s, axis=0, keepdims=True)
        s = jnp.where(s == m, -RANK_CODE * (1.0 + r / 32.0), s)
        vals.append(m)
    return vals, jnp.where(s <= -0.5 * RANK_CODE, s * (-32.0 / RANK_CODE) - 32.0, NOT_TOP)


def _pair_topk(av, bv, exact):
    n = av[0].shape[-1]
    a_lo, a_hi = jnp.concatenate(av[:8], 0), jnp.concatenate(av[8:], 0)
    b_lo, b_hi = jnp.concatenate(bv[:8], 0), jnp.concatenate(bv[8:], 0)
    row = lax.broadcasted_iota(jnp.int32, (8, n), 0).astype(F32)

    no_pos = 1e8

    def rows_b(a, b_blk, boff, nvalid):
        ok = row < nvalid
        return jnp.where(ok, av[a] + b_blk, NEG_INF), jnp.where(ok, a * 16.0 + boff + row, no_pos)

    def rows_a(b, a_blk, aoff, lo, hi):
        ok = (row >= lo) & (row < hi)
        return jnp.where(ok, a_blk + bv[b], NEG_INF), jnp.where(ok, (aoff + row) * 16.0 + b, no_pos)

    groups = [rows_b(0, b_lo, 0, 8), rows_b(0, b_hi, 8, 8), rows_b(1, b_lo, 0, 8), rows_b(2, b_lo, 0, 5),
              rows_b(3, b_lo, 0, 4), rows_a(0, a_lo, 0, 4, 8), rows_a(0, a_hi, 8, 0, 8),
              rows_a(1, a_lo, 0, 4, 8), rows_a(2, a_lo, 0, 4, 5)]
    cands = [g[0] for g in groups]
    poss = [g[1] for g in groups]
    sels = [jnp.zeros((8, n), F32) for _ in groups]
    top = av[0] + bv[0]
    z = jnp.zeros((1, n), F32)
    for _ in range(PEER_TOPK):
        m = functools.reduce(jnp.maximum, cands)
        m = jnp.max(m, axis=0, keepdims=True)
        if exact:
            first = functools.reduce(jnp.minimum, [jnp.where(c == m, p, 1e9) for c, p in zip(cands, poss)])
            first = jnp.min(first, axis=0, keepdims=True)
            hits = [p == first for p in poss]
            cands = [jnp.where(hh, NEG_INF, c) for hh, c in zip(hits, cands)]
            sels = [jnp.where(hh, 1.0, s) for hh, s in zip(hits, sels)]
        else:
            cands = [jnp.where(c == m, -RANK_CODE, c) for c in cands]
        z = z + jnp.exp(m - top)
    if not exact:
        sels = [jnp.where(c == -RANK_CODE, 1.0, 0.0) for c in cands]
    cnt = lambda x: jnp.sum(x, axis=0, keepdims=True)
    cut_lo = sels[5] + sels[7] + sels[8]
    for a, c in enumerate([cnt(sels[0]) + cnt(sels[1]), cnt(sels[2]), cnt(sels[3]), cnt(sels[4])]):
        cut_lo = cut_lo + jnp.where(row == a, c, 0.0)
    return cut_lo, sels[6], z, cnt(cut_lo) + cnt(sels[6])


def _peer_route_kernel(x_ref, sh_ref, sc_ref, wq_ref, sk_ref, xm_ref, e1_ref, cut_ref, e2_ref, r2_ref, q_s, *, tm):
    xm = (x_ref[0] * (1.0 + sc_ref[0]) + sh_ref[0]).astype(BF16)
    xm_ref[0] = xm
    q = _dot(xm, wq_ref[...])
    for hp in range(2 * PEER_HEADS):
        q_s[hp] = q[:, hp * PEER_HALF:(hp + 1) * PEER_HALF]

    def route(h, tok, exact):
        def scores(hp):
            return lax.dot_general(sk_ref[hp], q_s[hp, tok, :], (((1,), (1,)), ((), ())),
                                   precision=lax.Precision.HIGHEST, preferred_element_type=F32)

        s1, s2 = scores(2 * h), scores(2 * h + 1)
        av, rank1 = _top16(s1, exact)
        bv, rank2 = _top16(s2, exact)
        cut_lo, cut_hi, z, nsel = _pair_topk(av, bv, exact)
        cut = jnp.zeros_like(s1)
        for r in range(PEER_TOPK):
            src = cut_lo if r < 8 else cut_hi
            cut = jnp.where(rank1 == float(r), src[r % 8:r % 8 + 1, :], cut)
        e1_ref[0, h, :, tok] = (jnp.exp(s1 - av[0]) / z).astype(BF16)
        cut_ref[0, h, :, tok] = cut.astype(BF16)
        e2_ref[0, h, :, tok] = jnp.exp(s2 - bv[0]).astype(BF16)
        r2_ref[0, h, :, tok] = rank2.astype(BF16)
        ranked = lambda rk: jnp.sum(jnp.where(rk < PEER_TOPK, 1.0, 0.0), axis=0, keepdims=True)
        return ranked(rank1), ranked(rank2), nsel

    def body(h, carry):
        toks = [pl.ds(t0, PEER_RT) for t0 in range(0, tm, PEER_RT)]
        counts = [route(h, tok, exact=False) for tok in toks]
        for tok, cnts in zip(toks, counts):
            bad = functools.reduce(jnp.maximum, [jnp.abs(cn - PEER_TOPK) for cn in cnts])

            @pl.when(jnp.max(bad) > 0.0)
            def _():
                route(h, tok, exact=True)
        return carry

    lax.fori_loop(0, PEER_HEADS, body, 0)


def peer_route(x3, mod3, shift_chunk, wq_bf16, subkeys, tm=512):
    nseg, seg, d = x3.shape
    tok = lambda s, i: (s, i, 0)
    rshape = jax.ShapeDtypeStruct((nseg, PEER_HEADS, PEER_NKEYS, seg), BF16)
    rspec = pl.BlockSpec((1, PEER_HEADS, PEER_NKEYS, tm), lambda s, i: (s, 0, 0, i))
    return pl.pallas_call(
        functools.partial(_peer_route_kernel, tm=tm),
        grid=(nseg, seg // tm),
        in_specs=[pl.BlockSpec((1, tm, d), tok),
                  pl.BlockSpec((1, 1, d), lambda s, i: (s, 0, shift_chunk)),
                  pl.BlockSpec((1, 1, d), lambda s, i: (s, 0, shift_chunk + 1)),
                  pl.BlockSpec(wq_bf16.shape, lambda s, i: (0, 0)),
                  pl.BlockSpec((2 * PEER_HEADS, PEER_NKEYS, PEER_HALF), lambda s, i: (0, 0, 0))],
        out_specs=[pl.BlockSpec((1, tm, d), tok), rspec, rspec, rspec, rspec],
        out_shape=[jax.ShapeDtypeStruct((nseg, seg, d), BF16)] + [rshape] * 4,
        scratch_shapes=[pltpu.VMEM((2 * PEER_HEADS, tm, PEER_HALF), F32)],
        compiler_params=_params(("arbitrary", "arbitrary")),
        name="peer_route",
    )(x3, mod3, mod3, wq_bf16, subkeys.reshape(2 * PEER_HEADS, PEER_NKEYS, PEER_HALF))


PEER_CE = 1024


def _gelu_tanh(x):
    return 0.5 * x * (1.0 + jnp.tanh(0.7978845608028654 * (x + 0.044715 * x * x * x)))


def _peer_dense_kernel(xm_ref, u_ref, vt_ref, e1_ref, cut_ref, e2_ref, r2_ref, x_ref, gate_ref, g_ref, b_ref,
                       o_ref, acc_s, at_a, at_b, w_a, w_b, e2_s, r2_s, *, tm):
    e = pl.program_id(2)
    nb = PEER_CE // PEER_NKEYS
    tw = 2 * PEER_RT
    nsl = tm // tw

    @pl.when(e == 0)
    def _():
        acc_s[...] = jnp.zeros_like(acc_s)
        e2_s[:, :, :tm] = e2_ref[0]
        r2_s[:, :, :tm] = r2_ref[0]
        w_a[...] = jnp.zeros_like(w_a)
        w_b[...] = jnp.zeros_like(w_b)

    packed = (PEER_NKEYS // 16, 16, PEER_RT)
    ng = 2

    def tok_slice(sl):
        return pl.ds(pl.multiple_of(sl * tw, tw), tw)

    def activations(sl, at_ref):
        at_ref[...] = _dot(u_ref[0], xm_ref[0, tok_slice(sl), :], ((1,), (1,)))

    def project(sl, w_ref, scale):
        upd = _dot(vt_ref[0, 0], w_ref[...])
        acc_s[:, tok_slice(sl)] += upd if scale is None else upd * scale

    def gate_tiles(sl, half, i0, at_ref, w_ref):
        tok = pl.ds(pl.multiple_of(sl * tw + half * PEER_RT, PEER_RT), PEER_RT)
        cols = slice(half * PEER_RT, (half + 1) * PEER_RT)
        gmats = [jnp.zeros(packed, BF16) for _ in range(ng)]
        for h in range(PEER_HEADS):
            e2 = e2_s[h, :, tok].reshape(packed)
            r2 = r2_s[h, :, tok].reshape(packed)
            for k in range(ng):
                i = i0 + k
                e1 = jnp.broadcast_to(e1_ref[0, h, i:i + 1, tok], (16, PEER_RT))[None]
                cut = jnp.broadcast_to(cut_ref[0, h, i:i + 1, tok], (16, PEER_RT))[None]
                gmats[k] = gmats[k] + e1 * jnp.where(r2 < cut, e2, jnp.zeros_like(e2))
        for k in range(ng):
            rows = slice((i0 + k) * PEER_NKEYS, (i0 + k + 1) * PEER_NKEYS)
            act = _gelu_tanh(at_ref[rows, cols]).astype(BF16)
            w_ref[rows, cols] = gmats[k].reshape(PEER_NKEYS, PEER_RT) * act

    def stage(sl, at_cur, w_cur, at_next, w_prev):
        activations(lax.rem(sl + 1, nsl), at_next)
        for half in range(2):
            for i0 in range(0, nb, ng):
                gate_tiles(sl, half, i0, at_cur, w_cur)
        project(lax.rem(sl + nsl - 1, nsl), w_prev, (sl > 0).astype(F32))

    activations(0, at_a)

    def body(j, carry):
        stage(2 * j, at_a, w_a, at_b, w_b)
        stage(2 * j + 1, at_b, w_b, at_a, w_a)
        return carry

    lax.fori_loop(0, nsl // 2, body, 0)
    project(nsl - 1, w_b, None)

    @pl.when(e == pl.num_programs(2) - 1)
    def _():
        z = DEEPNORM_ALPHA * x_ref[0] + gate_ref[0] * acc_s[:, :tm].T
        o_ref[0] = _layer_norm_rows(z, g_ref[...], b_ref[...])


def peer_dense(xm, u_all, vt_all, l, e1, cut, e2, r2, x3, mod3, gate_chunk, ln_g, ln_b, tm=1024):
    nseg, seg, d = x3.shape
    ne = u_all.shape[1]
    nb = PEER_CE // PEER_NKEYS
    tp = tm + PEER_RT
    tok = lambda s, i, e: (s, i, 0)
    chunk = pl.BlockSpec((1, PEER_HEADS, nb, tm), lambda s, i, e: (s, 0, e, i))
    full = pl.BlockSpec((1, PEER_HEADS, PEER_NKEYS, tm), lambda s, i, e: (s, 0, 0, i))
    return pl.pallas_call(
        functools.partial(_peer_dense_kernel, tm=tm),
        grid=(nseg, seg // tm, ne // PEER_CE),
        in_specs=[pl.BlockSpec((1, tm, d), tok),
                  pl.BlockSpec((1, PEER_CE, d), lambda s, i, e: (l, e, 0)),
                  pl.BlockSpec((1, 1, d, PEER_CE), lambda s, i, e: (l, e, 0, 0)),
                  chunk, chunk, full, full,
                  pl.BlockSpec((1, tm, d), tok),
                  pl.BlockSpec((1, 1, d), lambda s, i, e: (s, 0, gate_chunk)),
                  pl.BlockSpec((1, d), lambda s, i, e: (0, 0)),
                  pl.BlockSpec((1, d), lambda s, i, e: (0, 0))],
        out_specs=pl.BlockSpec((1, tm, d), tok),
        out_shape=jax.ShapeDtypeStruct((nseg, seg, d), F32),
        scratch_shapes=[pltpu.VMEM((d, tp), F32),
                        pltpu.VMEM((PEER_CE, 2 * PEER_RT), F32), pltpu.VMEM((PEER_CE, 2 * PEER_RT), F32),
                        pltpu.VMEM((PEER_CE, 2 * PEER_RT), BF16), pltpu.VMEM((PEER_CE, 2 * PEER_RT), BF16),
                        pltpu.VMEM((PEER_HEADS, PEER_NKEYS, tp), BF16), pltpu.VMEM((PEER_HEADS, PEER_NKEYS, tp), BF16)],
        compiler_params=_params(("arbitrary", "arbitrary", "arbitrary")),
        name="peer_dense",
    )(xm, u_all, vt_all, e1, cut, e2, r2, x3, mod3, ln_g.reshape(1, d), ln_b.reshape(1, d))


def peer_layer(x3, mod3, l, wq, subkeys, u_all, vt_all, ln_g, ln_b):
    xm, e1, cut, e2, r2 = peer_route(x3, mod3, 3, wq.astype(BF16), subkeys)
    return peer_dense(xm, u_all, vt_all, l, e1, cut, e2, r2, x3, mod3, 5, ln_g, ln_b)


def _pad_cols(w, n):
    return jnp.pad(w, ((0, 0), (0, n - w.shape[1])))


def _stream(prompt_part, sample_part):
    return jnp.concatenate([prompt_part.reshape(1, -1, prompt_part.shape[-1]), sample_part], axis=0)


def _head_major(a, heads):
    b, t, _ = a.shape
    return jnp.transpose(a.reshape(b, t, heads, -1), (0, 2, 1, 3))


def _token_major(a):
    b, h, t, dh = a.shape
    return jnp.transpose(a, (0, 2, 1, 3)).reshape(b, t, h * dh)


MLSTM_CHUNK = 256
GLA_CHUNK = 64
NPROJ = 3200


def mlstm_layer(x3, mod3, bp, lp, st_c, st_n, st_m, w_in, b_gate, norm_w, w_out, ln_g, ln_b):
    nseg, seg, _ = x3.shape
    bs = nseg - 1
    p = mod_matmul(x3, mod3, 0, _pad_cols(w_in, NPROJ).astype(BF16))
    graw = p[:, :, 3072:3088]
    gp = graw[0].reshape(bp, lp, 16)
    zc = jnp.zeros((bp, 8, M_DK, M_DV), F32)
    zn = jnp.zeros((bp, 8, M_DK), F32)
    hfp, hbp, c_new, n_new, m_new = mlstm_scan(p.reshape(nseg * bp, lp, NPROJ), 0, bp, lp, gp,
                                               jnp.swapaxes(gp, 1, 2), b_gate, zc, zn, zn, min(MLSTM_CHUNK, lp),
                                               nseg * bp, 0)
    gs = graw[1:]
    hf, hb, _, _, _ = mlstm_scan(p, 1, bs, seg, gs, jnp.swapaxes(gs, 1, 2), b_gate,
                                 st_c.reshape(bs, 8, M_DK, M_DV), st_n.reshape(bs, 8, M_DK),
                                 jnp.broadcast_to(st_m.reshape(bs, 8, 1), (bs, 8, M_DK)), MLSTM_CHUNK,
                                 nseg, 1, prev=(hfp, hbp))
    x3 = outproj_ln("mlstm", (hf, hb), x3, mod3, 2, w_out.astype(BF16), ln_g, ln_b,
                    norm_w=norm_w, og=p, og_col=2)
    return (x3, c_new.reshape(bp, 2, M_HEADS, M_DK, M_DV), n_new.reshape(bp, 2, M_HEADS, M_DK),
            m_new[:, :, 0].reshape(bp, 2, M_HEADS))


def gla_layer(x3, mod3, bp, lp, st_s, w_in, w_gate2, b_gate2, norm_w, w_out, ln_g, ln_b):
    nseg, seg, _ = x3.shape
    bs = nseg - 1
    p = mod_matmul(x3, mod3, 0, _pad_cols(w_in, NPROJ).astype(BF16))
    gr = p[:, :, 3072:3104]
    zs = jnp.zeros((bp, 8, G_DV, G_DK), F32)
    ofp, obp, s_new = gla_scan(p.reshape(nseg * bp, lp, NPROJ), 0, bp, lp, gr[0].reshape(bp, lp, 32),
                               w_gate2, b_gate2, zs, GLA_CHUNK, nseg * bp, 0)
    s0t = jnp.swapaxes(st_s.reshape(bs, 8, G_DK, G_DV), -1, -2)
    of, ob, _ = gla_scan(p, 1, bs, seg, gr[1:], w_gate2, b_gate2, s0t, GLA_CHUNK, nseg, 1, prev=(ofp, obp))
    x3 = outproj_ln("gla", (of, ob), x3, mod3, 2, w_out.astype(BF16), ln_g, ln_b,
                    norm_w=jnp.tile(norm_w, G_HEADS), og=p, og_col=2)
    return x3, jnp.swapaxes(s_new, -1, -2).reshape(bp, 2, G_HEADS, G_DK, G_DV)


def na_layer(x3, mod3, bp, lp, cache_k, cache_v, w_in, rpb, w_out, ln_g, ln_b):
    nseg, seg, _ = x3.shape
    bs = nseg - 1
    hd = NA_HEADS * NA_HD
    p = mod_matmul(x3, mod3, 0, w_in.astype(BF16))
    pp = p[0].reshape(bp, lp, 3 * hd)
    hm = lambda a: _head_major(a, NA_HEADS).astype(BF16)
    yp = attention(hm(pp[..., :hd]), hm(pp[..., hd:2 * hd]), hm(pp[..., 2 * hd:]), lp)
    ps = p[1:]
    ys = na_attention(hm(ps[..., :hd]), hm(ps[..., hd:2 * hd]), hm(ps[..., 2 * hd:]),
                      hm(cache_k.reshape(bs, -1, hd)), hm(cache_v.reshape(bs, -1, hd)), na_bias_table(rpb))
    x3 = outproj_ln("plain", _stream(_token_major(yp), _token_major(ys)), x3, mod3, 2, w_out.astype(BF16), ln_g, ln_b)
    return (x3, pp[..., hd:2 * hd].reshape(bp, lp, NA_HEADS, NA_HD), pp[..., 2 * hd:].reshape(bp, lp, NA_HEADS, NA_HD))


def _rope_rotated_cols(w):
    q = MLA_ROPE // 4
    return jnp.concatenate([-w[..., q:2 * q], w[..., :q], -w[..., 3 * q:], w[..., 2 * q:3 * q]], axis=-1)


def _rope_tables(ts):
    ra = MLA_ROPE // 2
    t = np.arange(ts)
    inv = 1.0 / (ROPE_BASE ** (np.arange(0, ra, 2, dtype=np.float32) / ra))
    ang_r = (t // GRID_W).astype(np.float32)[:, None] * inv[None, :]
    ang_c = (t % GRID_W).astype(np.float32)[:, None] * inv[None, :]
    ang = np.concatenate([ang_r, ang_r, ang_c, ang_c], axis=-1).astype(np.float32)
    return jnp.cos(jnp.asarray(ang)), jnp.sin(jnp.asarray(ang))


def mla_layer(x3, mod3, bp, lp, cache_ckv, cache_kpe, w_in, q_norm, w_qup, kv_norm, w_kvup, w_out, ln_g, ln_b):
    nseg, seg, _ = x3.shape
    bs = nseg - 1
    nq = MLA_Q_LORA + MLA_KV_LORA
    w_ext = jnp.concatenate([w_in, _rope_rotated_cols(w_in[:, nq:])], axis=1)
    p = mod_matmul(x3, mod3, 0, _pad_cols(w_ext, 896).astype(BF16))
    cos_t, sin_t = _rope_tables(seg)
    cos3 = jnp.concatenate([jnp.ones((1, seg, MLA_ROPE), F32), jnp.broadcast_to(cos_t, (bs, seg, MLA_ROPE))], 0)
    sin3 = jnp.concatenate([jnp.zeros((1, seg, MLA_ROPE), F32), jnp.broadcast_to(sin_t, (bs, seg, MLA_ROPE))], 0)
    wq = w_qup.reshape(MLA_Q_LORA, MLA_HEADS, MLA_NOPE + MLA_ROPE)
    wq_rope = wq[:, :, MLA_NOPE:]
    w_q3 = jnp.concatenate([wq[:, :, :MLA_NOPE].reshape(MLA_Q_LORA, -1), wq_rope.reshape(MLA_Q_LORA, -1),
                            _rope_rotated_cols(wq_rope).reshape(MLA_Q_LORA, -1)], axis=1).astype(BF16)
    q_all = mla_q(p, q_norm, w_q3, jnp.tile(cos3, (1, 1, MLA_HEADS)), jnp.tile(sin3, (1, 1, MLA_HEADS)))
    wkv = w_kvup.reshape(MLA_KV_LORA, MLA_HEADS, MLA_NOPE + MLA_VD)
    w_kv2 = jnp.concatenate([wkv[:, :, :MLA_NOPE].reshape(MLA_KV_LORA, -1),
                             wkv[:, :, MLA_NOPE:].reshape(MLA_KV_LORA, -1)], axis=1).astype(BF16)
    ckvn, kpe, kv = mla_kv(p, kv_norm, w_kv2, cos3, sin3)
    kvc = matmul(cache_ckv.reshape(-1, MLA_KV_LORA), w_kv2, 512).reshape(bs, -1, w_kv2.shape[1])
    nn = MLA_HEADS * MLA_NOPE

    def heads(q_rows, kv_rows, kpe_rows):
        b, t, _ = q_rows.shape
        tk = kv_rows.shape[1]
        qh = jnp.concatenate([q_rows[..., :nn].reshape(b, t, MLA_HEADS, MLA_NOPE),
                              q_rows[..., nn:].reshape(b, t, MLA_HEADS, MLA_ROPE)], -1)
        kh = jnp.concatenate([kv_rows[..., :nn].reshape(b, tk, MLA_HEADS, MLA_NOPE),
                              jnp.broadcast_to(kpe_rows[:, :, None, :], (b, tk, MLA_HEADS, MLA_ROPE))], -1)
        vh = kv_rows[..., nn:].reshape(b, tk, MLA_HEADS, MLA_VD)
        tr = lambda a: jnp.transpose(a, (0, 2, 1, 3)).astype(BF16)
        return tr(qh), tr(kh), tr(vh)

    yp = attention(*heads(q_all[0].reshape(bp, lp, -1), kv[0].reshape(bp, lp, -1), kpe[0].reshape(bp, lp, -1)), lp)
    ys = attention(*heads(q_all[1:], jnp.concatenate([kv[1:], kvc], 1), jnp.concatenate([kpe[1:], cache_kpe], 1)), 256)
    x3 = outproj_ln("plain", _stream(_token_major(yp), _token_major(ys)), x3, mod3, 2, w_out.astype(BF16), ln_g, ln_b)
    return x3, ckvn[0].reshape(bp, lp, MLA_KV_LORA), kpe[0].reshape(bp, lp, MLA_ROPE)


def kernel(x_prompt, x_sample, c, c_ctx, state_mlstm_C, state_mlstm_n, state_mlstm_m, state_gla_S, cache_na_k, cache_na_v, cache_mla_ckv, cache_mla_kpe, ada_w, ada_b, ln_mix_g, ln_mix_b, ln_ffn_g, ln_ffn_b, mlstm_w_in, mlstm_b_gate, mlstm_norm_w, mlstm_w_out, gla_w_in, gla_w_gate2, gla_b_gate2, gla_norm_w, gla_w_out, na_w_in, na_rpb, na_w_out, mla_w_in, mla_q_norm, mla_w_qup, mla_kv_norm, mla_w_kvup, mla_w_out, peer_w_q, peer_subkeys, peer_u, peer_v):
    bp, lp, d = x_prompt.shape
    bs, ts, _ = x_sample.shape
    assert bp * lp == ts and bs + 1 <= 8
    x3 = _stream(x_prompt, x_sample)
    cond8 = jnp.zeros((8, d), F32).at[0].set(c_ctx).at[1:1 + bs].set(c)
    mods = adaln_all(cond8, ada_w, ada_b)
    u_all = peer_u.astype(BF16)
    vt_all = jnp.swapaxes(peer_v.reshape(DEPTH, -1, PEER_CE, d), 2, 3).astype(BF16)
    outs = {}
    for l in range(DEPTH):
        mod3 = mods[l].reshape(8, 1, ADA_CHUNKS * d)
        kind = l % 4
        if kind == 0:
            x3, outs["C"], outs["n"], outs["m"] = mlstm_layer(
                x3, mod3, bp, lp, state_mlstm_C, state_mlstm_n, state_mlstm_m, mlstm_w_in, mlstm_b_gate,
                mlstm_norm_w, mlstm_w_out, ln_mix_g[l], ln_mix_b[l])
        elif kind == 1:
            x3, outs["S"] = gla_layer(x3, mod3, bp, lp, state_gla_S, gla_w_in, gla_w_gate2, gla_b_gate2,
                                      gla_norm_w, gla_w_out, ln_mix_g[l], ln_mix_b[l])
        elif kind == 2:
            x3, outs["nk"], outs["nv"] = na_layer(x3, mod3, bp, lp, cache_na_k, cache_na_v, na_w_in, na_rpb,
                                                  na_w_out, ln_mix_g[l], ln_mix_b[l])
        else:
            x3, outs["ckv"], outs["kpe"] = mla_layer(x3, mod3, bp, lp, cache_mla_ckv, cache_mla_kpe, mla_w_in,
                                                     mla_q_norm, mla_w_qup, mla_kv_norm, mla_w_kvup, mla_w_out,
                                                     ln_mix_g[l], ln_mix_b[l])
        x3 = peer_layer(x3, mod3, l, peer_w_q[l], peer_subkeys[l], u_all, vt_all, ln_ffn_g[l], ln_ffn_b[l])
    return (x3[0].reshape(bp, lp, d), x3[1:], outs["C"], outs["n"], outs["m"], outs["S"], outs["nk"], outs["nv"],
            outs["ckv"], outs["kpe"])
```

```python
import functools

import numpy as np
import jax
import jax.numpy as jnp
from jax import lax
from jax.experimental import pallas as pl
from jax.experimental.pallas import tpu as pltpu

D_MODEL = 1024
DEPTH = 4
GRID_W = 64
DEEPNORM_ALPHA = (2.0 * DEPTH) ** 0.25
ADA_CHUNKS = 6
NORM_EPS = 1e-5
SEG = 4096
NSEG = 3

M_HEADS, M_DK, M_DV = 4, 128, 256
G_HEADS, G_DK, G_DV = 4, 128, 256
G_GATE_RANK = 16
G_GATE_NORM = 16.0
NA_HEADS, NA_HD, NA_ROWS, NA_COLS = 16, 64, 8, 16
MLA_HEADS, MLA_Q_LORA, MLA_KV_LORA, MLA_NOPE, MLA_ROPE, MLA_VD = 16, 512, 256, 64, 32, 64
ROPE_BASE = 10000.0
PEER_HEADS, PEER_NKEYS, PEER_HALF, PEER_TOPK = 8, 128, 128, 16

V7X_VMEM_LIMIT = 56 * 1024 * 1024
F32 = jnp.float32
BF16 = jnp.bfloat16
NEG_INF = float("-inf")


def _params(sem, vmem=V7X_VMEM_LIMIT):
    return pltpu.CompilerParams(dimension_semantics=sem, vmem_limit_bytes=vmem)


def _dot(a, b, dims=((1,), (0,))):
    return lax.dot_general(a, b, (dims, ((), ())), preferred_element_type=F32)


def _split3(a):
    hi = a.astype(BF16)
    r1 = a - hi.astype(F32)
    mid = r1.astype(BF16)
    lo = (r1 - mid.astype(F32)).astype(BF16)
    return hi, mid, lo


def _dot_exact_lhs(m01, a):
    hi, mid, lo = _split3(a)
    return _dot(m01, hi) + _dot(m01, mid) + _dot(m01, lo)


def _dot_exact_rhs(a, m01):
    hi, mid, lo = _split3(a)
    return _dot(hi, m01) + _dot(mid, m01) + _dot(lo, m01)


def _log_sigmoid(x):
    return jnp.minimum(x, 0.0) - jnp.log(1.0 + jnp.exp(-jnp.abs(x)))


def _sigmoid(x):
    return 1.0 / (1.0 + jnp.exp(-x))


def _adaln_kernel(c_ref, w_ref, b_ref, o_ref):
    cv = c_ref[...]
    a = cv * _sigmoid(cv)
    o_ref[0] = lax.dot_general(a, w_ref[0], (((1,), (0,)), ((), ())), precision=lax.Precision.HIGHEST,
                               preferred_element_type=F32) + b_ref[0]


def adaln_all(cond8, ada_w, ada_b):
    tn = 1024
    n = ada_w.shape[-1]
    return pl.pallas_call(
        _adaln_kernel,
        grid=(DEPTH, n // tn),
        in_specs=[pl.BlockSpec((8, D_MODEL), lambda l, j: (0, 0)),
                  pl.BlockSpec((1, D_MODEL, tn), lambda l, j: (l, 0, j)),
                  pl.BlockSpec((1, 1, tn), lambda l, j: (l, 0, j))],
        out_specs=pl.BlockSpec((1, 8, tn), lambda l, j: (l, 0, j)),
        out_shape=jax.ShapeDtypeStruct((DEPTH, 8, n), F32),
        compiler_params=_params(("arbitrary", "arbitrary")),
        name="adaln",
    )(cond8, ada_w, ada_b.reshape(DEPTH, 1, n))


def _modmm_kernel(x_ref, sh_ref, sc_ref, w_ref, o_ref, xm_ref):
    @pl.when(pl.program_id(2) == 0)
    def _():
        xm_ref[...] = (x_ref[0] * (1.0 + sc_ref[0]) + sh_ref[0]).astype(BF16)

    o_ref[0] = _dot(xm_ref[...], w_ref[...]).astype(o_ref.dtype)


def mod_matmul(x3, mod3, shift_chunk, w_bf16, tm=512, tn=None, out_dtype=F32):
    nseg, seg, d = x3.shape
    n = w_bf16.shape[1]
    tn = n if tn is None else tn
    return pl.pallas_call(
        _modmm_kernel,
        grid=(nseg, seg // tm, n // tn),
        in_specs=[pl.BlockSpec((1, tm, d), lambda s, i, j: (s, i, 0)),
                  pl.BlockSpec((1, 1, d), lambda s, i, j: (s, 0, shift_chunk)),
                  pl.BlockSpec((1, 1, d), lambda s, i, j: (s, 0, shift_chunk + 1)),
                  pl.BlockSpec((d, tn), lambda s, i, j: (0, j))],
        out_specs=pl.BlockSpec((1, tm, tn), lambda s, i, j: (s, i, j)),
        out_shape=jax.ShapeDtypeStruct((nseg, seg, n), out_dtype),
        scratch_shapes=[pltpu.VMEM((tm, d), BF16)],
        compiler_params=_params(("arbitrary", "arbitrary", "arbitrary")),
        name="mod_matmul",
    )(x3, mod3, mod3, w_bf16)


def _layer_norm_rows(y, g, b):
    mu = jnp.mean(y, axis=-1, keepdims=True)
    yc = y - mu
    var = jnp.mean(yc * yc, axis=-1, keepdims=True)
    return yc * lax.rsqrt(var + NORM_EPS) * g + b


def _outproj_kernel(*refs, mode):
    if mode == "plain":
        y_ref, x_ref, gate_ref, w_ref, g_ref, b_ref, o_ref = refs
        yin = y_ref[0].astype(BF16)
    else:
        ya_ref, yb_ref, og_ref, nw_ref, x_ref, gate_ref, w_ref, g_ref, b_ref, o_ref = refs
        hs = ya_ref[0] + yb_ref[0]
        og = og_ref[0]
        parts = []
        for h in range(4):
            seg = hs[:, h * 256:(h + 1) * 256]
            nw = nw_ref[:, h * 256:(h + 1) * 256]
            if mode == "mlstm":
                mu = jnp.mean(seg, axis=-1, keepdims=True)
                sc = seg - mu
                var = jnp.mean(sc * sc, axis=-1, keepdims=True)
                parts.append(sc * lax.rsqrt(var + NORM_EPS) * nw)
            else:
                ms = jnp.mean(seg * seg, axis=-1, keepdims=True)
                parts.append(seg * lax.rsqrt(ms + NORM_EPS) * nw)
        hn = jnp.concatenate(parts, axis=-1)
        act = _sigmoid(og) if mode == "mlstm" else og * _sigmoid(og)
        yin = (act * hn).astype(BF16)
    y = _dot(yin, w_ref[...])
    z = DEEPNORM_ALPHA * x_ref[0] + gate_ref[0] * y
    o_ref[0] = _layer_norm_rows(z, g_ref[...], b_ref[...])


def outproj_ln(mode, ys, x3, mod3, gate_chunk, w_bf16, ln_g, ln_b, norm_w=None, og=None, og_col=0, tm=512):
    nseg, seg, d = x3.shape
    k = w_bf16.shape[0]
    tok = lambda s, i: (s, i, 0)
    if mode == "plain":
        args = [ys]
        specs = [pl.BlockSpec((1, tm, k), tok)]
    else:
        args = [ys[0], ys[1], og, norm_w.reshape(1, k)]
        specs = [pl.BlockSpec((1, tm, k), tok), pl.BlockSpec((1, tm, k), tok),
                 pl.BlockSpec((1, tm, k), lambda s, i: (s, i, og_col)),
                 pl.BlockSpec((1, k), lambda s, i: (0, 0))]
    args += [x3, mod3, w_bf16, ln_g.reshape(1, d), ln_b.reshape(1, d)]
    specs += [pl.BlockSpec((1, tm, d), tok),
              pl.BlockSpec((1, 1, d), lambda s, i: (s, 0, gate_chunk)),
              pl.BlockSpec((k, d), lambda s, i: (0, 0)),
              pl.BlockSpec((1, d), lambda s, i: (0, 0)),
              pl.BlockSpec((1, d), lambda s, i: (0, 0))]
    return pl.pallas_call(
        functools.partial(_outproj_kernel, mode=mode),
        grid=(nseg, seg // tm),
        in_specs=specs,
        out_specs=pl.BlockSpec((1, tm, d), tok),
        out_shape=jax.ShapeDtypeStruct((nseg, seg, d), F32),
        compiler_params=_params(("arbitrary", "arbitrary")),
        name="outproj_ln_" + mode,
    )(*args)


def _tri(n, lower):
    r = lax.broadcasted_iota(jnp.int32, (n, n), 0)
    c = lax.broadcasted_iota(jnp.int32, (n, n), 1)
    return (c <= r) if lower else (c >= r)


def _mlstm_kernel(pf_ref, pb_ref, gf_ref, gb_ref, gtf_ref, gtb_ref, bias_ref, biast_ref,
                  c0_ref, n0_ref, m0_ref, hf_ref, hb_ref, co_ref, no_ref, mo_ref,
                  c_s, n_s, m_s, *, L):
    c = pl.program_id(1)

    @pl.when(c == 0)
    def _():
        c_s[...] = c0_ref[0]
        n_s[...] = n0_ref[0]
        m_s[...] = m0_ref[0]

    dirs = [(pf_ref, gf_ref, gtf_ref, hf_ref), (pb_ref, gb_ref, gtb_ref, hb_ref)]
    masks = [_tri(L, lower=True), _tri(L, lower=False)]
    gates = []
    for d in range(2):
        g = dirs[d][1][0] + bias_ref[...]
        gt = dirs[d][2][0] + biast_ref[...]
        lf_c = _log_sigmoid(g[:, d * 8 + 4:d * 8 + 8])
        lf_r = _log_sigmoid(gt[d * 8 + 4:d * 8 + 8, :])
        b_c = _dot_exact_lhs(masks[d].astype(BF16), lf_c)
        b_r = _dot_exact_rhs(lf_r, masks[1 - d].astype(BF16))
        gates.append((g[:, d * 8:d * 8 + 4], gt[d * 8:d * 8 + 4, :], b_c, b_r))
    units = [(d, h) for d in range(2) for h in range(M_HEADS)]

    def unit_inputs(d, h):
        p_ref = dirs[d][0]
        q = p_ref[0, :, h * M_DK:(h + 1) * M_DK]
        k = p_ref[0, :, 512 + h * M_DK:512 + (h + 1) * M_DK] * (M_DK ** -0.5)
        v = p_ref[0, :, 1024 + h * M_DV:1024 + (h + 1) * M_DV].astype(BF16)
        li_c, li_r, b_c, b_r = gates[d]
        return q, k, v, li_c[:, h:h + 1], li_r[h:h + 1, :], b_c[:, h:h + 1], b_r[h:h + 1, :]

    qks = []
    for d, h in units:
        q, k, _, _, _, _, _ = unit_inputs(d, h)
        qks.append(_dot(q.astype(BF16), k.astype(BF16), ((1,), (1,))))
    smats, eis, mts = [], [], []
    for u, (d, h) in enumerate(units):
        _, _, _, _, lir, bc, br = unit_inputs(d, h)
        m_prev = m_s[u:u + 1, 0:1]
        dmat = jnp.where(masks[d], bc - br + lir, NEG_INF)
        inter = bc + m_prev
        mt = jnp.maximum(inter, jnp.max(dmat, axis=-1, keepdims=True))
        smats.append(qks[u] * jnp.exp(dmat - mt))
        eis.append(jnp.exp(inter - mt))
        mts.append(mt)
    nums = []
    for u, (d, h) in enumerate(units):
        q, _, v, _, _, _, _ = unit_inputs(d, h)
        nums.append(_dot(smats[u].astype(BF16), v) + eis[u] * _dot(q.astype(BF16), c_s[u].astype(BF16)))
    for u, (d, h) in enumerate(units):
        q, _, _, _, _, _, _ = unit_inputs(d, h)
        den = (jnp.sum(smats[u], axis=-1, keepdims=True)
               + eis[u] * jnp.sum(q * n_s[u:u + 1, :], axis=-1, keepdims=True))
        dirs[d][3][0, :, h * M_DV:(h + 1) * M_DV] = nums[u] / jnp.maximum(jnp.abs(den), jnp.exp(-mts[u]))
    for u, (d, h) in enumerate(units):
        _, k, v, lic, lir, bc, br = unit_inputs(d, h)
        last = L - 1 if d == 0 else 0
        m_prev = m_s[u:u + 1, 0:1]
        tot = br[:, last:last + 1]
        g_c = tot - bc + lic
        g_r = tot - br + lir
        m_new = jnp.maximum(tot + m_prev, jnp.max(g_r, axis=-1, keepdims=True))
        kw = k * jnp.exp(g_c - m_new)
        dec = jnp.exp(tot + m_prev - m_new)
        c_s[u] = dec * c_s[u] + _dot(kw.astype(BF16), v, ((0,), (0,)))
        n_s[u:u + 1, :] = dec * n_s[u:u + 1, :] + jnp.sum(kw, axis=0, keepdims=True)
        m_s[u:u + 1, :] = jnp.broadcast_to(m_new, (1, 128))

    @pl.when(c == pl.num_programs(1) - 1)
    def _():
        co_ref[0] = c_s[...]
        no_ref[0] = n_s[...]
        mo_ref[0] = m_s[...]


def _skip_refs(body, start, n):
    def kernel(*refs):
        return body(*refs[:start], *refs[start + n:])
    return kernel


def _stream_out_args(out_rows, t, width, prev):
    shape = jax.ShapeDtypeStruct((out_rows, t, width), F32)
    if prev is None:
        return shape, [], []
    return shape, [a.reshape(out_rows, t, width) for a in prev], [pl.BlockSpec(memory_space=pl.ANY)] * len(prev)


def mlstm_scan(p, b0, nb, t, g, gt, bias, c0, n0, m0, L, out_rows, ob0, prev=None):
    nc = t // L
    hshape, prev_args, prev_specs = _stream_out_args(out_rows, t, M_HEADS * M_DV, prev)
    n_in = 11
    fwd = lambda b, c: (b + b0, c, 0)
    bwd = lambda b, c: (b + b0, nc - 1 - c, 0)
    st4 = lambda b, c: (b, 0, 0, 0)
    st3 = lambda b, c: (b, 0, 0)
    return pl.pallas_call(
        _skip_refs(functools.partial(_mlstm_kernel, L=L), n_in, len(prev_args)),
        grid=(nb, nc),
        input_output_aliases={n_in + i: i for i in range(len(prev_args))},
        in_specs=[pl.BlockSpec((1, L, 2048), fwd), pl.BlockSpec((1, L, 2048), bwd),
                  pl.BlockSpec((1, L, 16), lambda b, c: (b, c, 0)),
                  pl.BlockSpec((1, L, 16), lambda b, c: (b, nc - 1 - c, 0)),
                  pl.BlockSpec((1, 16, L), lambda b, c: (b, 0, c)),
                  pl.BlockSpec((1, 16, L), lambda b, c: (b, 0, nc - 1 - c)),
                  pl.BlockSpec((1, 16), lambda b, c: (0, 0)),
                  pl.BlockSpec((16, 1), lambda b, c: (0, 0)),
                  pl.BlockSpec((1, 8, M_DK, M_DV), st4),
                  pl.BlockSpec((1, 8, M_DK), st3),
                  pl.BlockSpec((1, 8, M_DK), st3)] + prev_specs,
        out_specs=[pl.BlockSpec((1, L, 1024), lambda b, c: (b + ob0, c, 0)),
                   pl.BlockSpec((1, L, 1024), lambda b, c: (b + ob0, nc - 1 - c, 0)),
                   pl.BlockSpec((1, 8, M_DK, M_DV), st4),
                   pl.BlockSpec((1, 8, M_DK), st3),
                   pl.BlockSpec((1, 8, M_DK), st3)],
        out_shape=[hshape, hshape,
                   jax.ShapeDtypeStruct((nb, 8, M_DK, M_DV), F32),
                   jax.ShapeDtypeStruct((nb, 8, M_DK), F32),
                   jax.ShapeDtypeStruct((nb, 8, M_DK), F32)],
        scratch_shapes=[pltpu.VMEM((8, M_DK, M_DV), F32), pltpu.VMEM((8, M_DK), F32),
                        pltpu.VMEM((8, M_DK), F32)],
        compiler_params=_params(("arbitrary", "arbitrary")),
        name="mlstm_scan",
    )(p, p, g, g, gt, gt, bias.reshape(1, 16), bias.reshape(16, 1), c0, n0, m0, *prev_args)


def _gla_kernel(pf_ref, pb_ref, gf_ref, gb_ref, w2_ref, b2_ref, s0_ref, of_ref, ob_ref, so_ref, s_s, *, L):
    c = pl.program_id(1)

    @pl.when(c == 0)
    def _():
        s_s[...] = s0_ref[0]

    dirs = [(pf_ref, gf_ref, of_ref), (pb_ref, gb_ref, ob_ref)]
    masks = [_tri(L, lower=True), _tri(L, lower=False)]
    bcs = []
    for d in range(2):
        gr = dirs[d][1][0][:, d * G_GATE_RANK:(d + 1) * G_GATE_RANK]
        pre = lax.dot_general(gr, w2_ref[d], (((1,), (0,)), ((), ())), precision=lax.Precision.HIGHEST,
                              preferred_element_type=F32) + b2_ref[d]
        la = _log_sigmoid(pre) * (1.0 / G_GATE_NORM)
        bcs.append(_dot_exact_lhs(masks[d].astype(BF16), la))
    units = [(d, h) for d in range(2) for h in range(G_HEADS)]

    def unit_inputs(d, h):
        p_ref = dirs[d][0]
        q = p_ref[0, :, h * G_DK:(h + 1) * G_DK] * (G_DK ** -0.5)
        k = p_ref[0, :, 512 + h * G_DK:512 + (h + 1) * G_DK]
        v = p_ref[0, :, 1024 + h * G_DV:1024 + (h + 1) * G_DV].astype(BF16)
        return q, k, v, bcs[d][:, h * G_DK:(h + 1) * G_DK]

    qds, a_mats = [], []
    for d, h in units:
        q, k, _, bc = unit_inputs(d, h)
        qd = (q * jnp.exp(bc)).astype(BF16)
        kd = (k * jnp.exp(-bc)).astype(BF16)
        qds.append(qd)
        a_mats.append(jnp.where(masks[d], _dot(qd, kd, ((1,), (1,))), 0.0).astype(BF16))
    for u, (d, h) in enumerate(units):
        _, _, v, _ = unit_inputs(d, h)
        dirs[d][2][0, :, h * G_DV:(h + 1) * G_DV] = (_dot(a_mats[u], v)
                                                     + _dot(qds[u], s_s[u].astype(BF16), ((1,), (1,))))
    for u, (d, h) in enumerate(units):
        _, k, v, bc = unit_inputs(d, h)
        last = L - 1 if d == 0 else 0
        bl = bc[last:last + 1, :]
        kl = (k * jnp.exp(bl - bc)).astype(BF16)
        s_s[u] = s_s[u] * jnp.exp(bl) + _dot(v, kl, ((0,), (0,)))

    @pl.when(c == pl.num_programs(1) - 1)
    def _():
        so_ref[0] = s_s[...]


def gla_scan(p, b0, nb, t, gr, w2, b2, s0t, L, out_rows, ob0, prev=None):
    nc = t // L
    oshape, prev_args, prev_specs = _stream_out_args(out_rows, t, G_HEADS * G_DV, prev)
    n_in = 7
    st4 = lambda b, c: (b, 0, 0, 0)
    return pl.pallas_call(
        _skip_refs(functools.partial(_gla_kernel, L=L), n_in, len(prev_args)),
        grid=(nb, nc),
        input_output_aliases={n_in + i: i for i in range(len(prev_args))},
        in_specs=[pl.BlockSpec((1, L, 2048), lambda b, c: (b + b0, c, 0)),
                  pl.BlockSpec((1, L, 2048), lambda b, c: (b + b0, nc - 1 - c, 0)),
                  pl.BlockSpec((1, L, 32), lambda b, c: (b, c, 0)),
                  pl.BlockSpec((1, L, 32), lambda b, c: (b, nc - 1 - c, 0)),
                  pl.BlockSpec((2, G_GATE_RANK, 512), lambda b, c: (0, 0, 0)),
                  pl.BlockSpec((2, 1, 512), lambda b, c: (0, 0, 0)),
                  pl.BlockSpec((1, 8, G_DV, G_DK), st4)] + prev_specs,
        out_specs=[pl.BlockSpec((1, L, 1024), lambda b, c: (b + ob0, c, 0)),
                   pl.BlockSpec((1, L, 1024), lambda b, c: (b + ob0, nc - 1 - c, 0)),
                   pl.BlockSpec((1, 8, G_DV, G_DK), st4)],
        out_shape=[oshape, oshape, jax.ShapeDtypeStruct((nb, 8, G_DV, G_DK), F32)],
        scratch_shapes=[pltpu.VMEM((8, G_DV, G_DK), F32)],
        compiler_params=_params(("arbitrary", "arbitrary")),
        name="gla_scan",
    )(p, p, gr, gr, w2, b2.reshape(2, 1, 512), s0t, *prev_args)


ATTN_HEADS_PER_STEP = 2


def _attn_kernel(q_ref, k_ref, v_ref, o_ref, *, scale):
    scores = [_dot(q_ref[0, j], k_ref[0, j], ((1,), (1,))) * scale for j in range(ATTN_HEADS_PER_STEP)]
    for j, s in enumerate(scores):
        m = jnp.max(s, axis=-1, keepdims=True)
        p = jnp.exp(s - m)
        l = jnp.sum(p, axis=-1, keepdims=True)
        o_ref[0, j] = _dot(p.astype(BF16), v_ref[0, j]) / l


def attention(q, k, v, tq):
    b, h, lq, dq = q.shape
    lk, dv = k.shape[2], v.shape[3]
    hb = ATTN_HEADS_PER_STEP
    return pl.pallas_call(
        functools.partial(_attn_kernel, scale=dq ** -0.5),
        grid=(b, h // hb, lq // tq),
        in_specs=[pl.BlockSpec((1, hb, tq, dq), lambda b, h, i: (b, h, i, 0)),
                  pl.BlockSpec((1, hb, lk, dq), lambda b, h, i: (b, h, 0, 0)),
                  pl.BlockSpec((1, hb, lk, dv), lambda b, h, i: (b, h, 0, 0))],
        out_specs=pl.BlockSpec((1, hb, tq, dv), lambda b, h, i: (b, h, i, 0)),
        out_shape=jax.ShapeDtypeStruct((b, h, lq, dv), F32),
        compiler_params=_params(("arbitrary", "arbitrary", "arbitrary")),
        name="attention",
    )(q, k, v)


NA_RB = 8


def _na_kernel(q_ref, k_ref, v_ref, kc_ref, vc_ref, bias_ref, o_ref, *, rows):
    j = pl.program_id(2)
    scale = NA_HD ** -0.5
    s_ctx_all = _dot(q_ref[0, 0], kc_ref[0, 0], ((1,), (1,))) * scale
    offs, s_locs = [], []
    for a in range(NA_RB):
        r = j * NA_RB + a
        start = jnp.clip(r - NA_ROWS // 2, 0, rows - NA_ROWS)
        dr0 = start - r + (NA_ROWS - 1)
        offs.append(pl.multiple_of(start * GRID_W, GRID_W))
        qa = q_ref[0, 0, a * GRID_W:(a + 1) * GRID_W, :]
        kl = k_ref[0, 0, pl.ds(offs[a], NA_ROWS * GRID_W), :]
        s_locs.append(_dot(qa, kl, ((1,), (1,))) * scale + bias_ref[0, dr0])
    p_locs, p_ctxs, ls = [], [], []
    for a in range(NA_RB):
        s_ctx = s_ctx_all[a * GRID_W:(a + 1) * GRID_W, :]
        m = jnp.maximum(jnp.max(s_locs[a], axis=-1, keepdims=True), jnp.max(s_ctx, axis=-1, keepdims=True))
        p_loc = jnp.exp(s_locs[a] - m)
        p_ctx = jnp.exp(s_ctx - m)
        ls.append(jnp.sum(p_loc, axis=-1, keepdims=True) + jnp.sum(p_ctx, axis=-1, keepdims=True))
        p_locs.append(p_loc.astype(BF16))
        p_ctxs.append(p_ctx.astype(BF16))
    o_ctx_all = _dot(jnp.concatenate(p_ctxs, axis=0), vc_ref[0, 0])
    for a in range(NA_RB):
        vl = v_ref[0, 0, pl.ds(offs[a], NA_ROWS * GRID_W), :]
        o = _dot(p_locs[a], vl) + o_ctx_all[a * GRID_W:(a + 1) * GRID_W, :]
        o_ref[0, 0, a * GRID_W:(a + 1) * GRID_W, :] = o / ls[a]


def na_bias_table(rpb):
    cq = np.arange(GRID_W)[:, None]
    ck = np.arange(GRID_W)[None, :]
    cs = np.clip(cq - NA_COLS // 2, 0, GRID_W - NA_COLS)
    ok = (ck >= cs) & (ck < cs + NA_COLS)
    dc = np.clip(ck - cq, -(NA_COLS - 1), NA_COLS - 1) + (NA_COLS - 1)
    t = jnp.where(ok[None, None], rpb.astype(F32)[:, :, dc], NEG_INF)
    rows = np.arange(NA_ROWS)[:, None] + np.arange(NA_ROWS)[None, :]
    tf = t[:, rows]
    return jnp.transpose(tf, (0, 1, 3, 2, 4)).reshape(NA_HEADS, NA_ROWS, GRID_W, NA_ROWS * GRID_W)


def na_attention(q, k, v, kc, vc, bias):
    b, h, t, dh = q.shape
    lc = kc.shape[2]
    rows = t // GRID_W
    full = lambda b, h, j: (b, h, 0, 0)
    return pl.pallas_call(
        functools.partial(_na_kernel, rows=rows),
        grid=(b, h, rows // NA_RB),
        in_specs=[pl.BlockSpec((1, 1, NA_RB * GRID_W, dh), lambda b, h, j: (b, h, j, 0)),
                  pl.BlockSpec((1, 1, t, dh), full), pl.BlockSpec((1, 1, t, dh), full),
                  pl.BlockSpec((1, 1, lc, dh), full), pl.BlockSpec((1, 1, lc, dh), full),
                  pl.BlockSpec((1, NA_ROWS, GRID_W, NA_ROWS * GRID_W), lambda b, h, j: (h, 0, 0, 0))],
        out_specs=pl.BlockSpec((1, 1, NA_RB * GRID_W, dh), lambda b, h, j: (b, h, j, 0)),
        out_shape=jax.ShapeDtypeStruct((b, h, t, dh), F32),
        compiler_params=_params(("arbitrary", "arbitrary", "arbitrary")),
        name="na_attention",
    )(q, k, v, kc, vc, bias)


def _rms_rows(x, g):
    return x * lax.rsqrt(jnp.mean(x * x, axis=-1, keepdims=True) + NORM_EPS) * g


def _mla_q_kernel(cq_ref, g_ref, w_ref, cos_ref, sin_ref, o_ref):
    r = _dot(_rms_rows(cq_ref[0], g_ref[...]).astype(BF16), w_ref[...])
    nn = MLA_HEADS * MLA_NOPE
    nr = MLA_HEADS * MLA_ROPE
    o_ref[0, :, :nn] = r[:, :nn]
    o_ref[0, :, nn:] = r[:, nn:nn + nr] * cos_ref[0] + r[:, nn + nr:] * sin_ref[0]


def mla_q(p, q_norm, w_q3, cos_q, sin_q, tm=512):
    nseg, seg, _ = p.shape
    nout = MLA_HEADS * (MLA_NOPE + MLA_ROPE)
    nr = MLA_HEADS * MLA_ROPE
    tok = lambda s, i: (s, i, 0)
    return pl.pallas_call(
        _mla_q_kernel,
        grid=(nseg, seg // tm),
        in_specs=[pl.BlockSpec((1, tm, MLA_Q_LORA), tok),
                  pl.BlockSpec((1, MLA_Q_LORA), lambda s, i: (0, 0)),
                  pl.BlockSpec(w_q3.shape, lambda s, i: (0, 0)),
                  pl.BlockSpec((1, tm, nr), tok), pl.BlockSpec((1, tm, nr), tok)],
        out_specs=pl.BlockSpec((1, tm, nout), tok),
        out_shape=jax.ShapeDtypeStruct((nseg, seg, nout), F32),
        compiler_params=_params(("arbitrary", "arbitrary")),
        name="mla_q",
    )(p, q_norm.reshape(1, -1), w_q3, cos_q, sin_q)


def _mla_kv_kernel(ckv_ref, kpe_ref, g_ref, w_ref, cos_ref, sin_ref, ckvn_ref, kpeo_ref, kv_ref):
    cn = _rms_rows(ckv_ref[0], g_ref[...])
    ckvn_ref[0] = cn
    kv_ref[0] = _dot(cn.astype(BF16), w_ref[...])
    kp = kpe_ref[0]
    kpeo_ref[0] = kp[:, :MLA_ROPE] * cos_ref[0] + kp[:, MLA_ROPE:2 * MLA_ROPE] * sin_ref[0]


def mla_kv(p, kv_norm, w_kv, cos_k, sin_k, tm=512):
    nseg, seg, _ = p.shape
    nkv = w_kv.shape[1]
    tok = lambda s, i: (s, i, 0)
    return pl.pallas_call(
        _mla_kv_kernel,
        grid=(nseg, seg // tm),
        in_specs=[pl.BlockSpec((1, tm, MLA_KV_LORA), lambda s, i: (s, i, MLA_Q_LORA // MLA_KV_LORA)),
                  pl.BlockSpec((1, tm, 128), lambda s, i: (s, i, (MLA_Q_LORA + MLA_KV_LORA) // 128)),
                  pl.BlockSpec((1, MLA_KV_LORA), lambda s, i: (0, 0)),
                  pl.BlockSpec(w_kv.shape, lambda s, i: (0, 0)),
                  pl.BlockSpec((1, tm, MLA_ROPE), tok), pl.BlockSpec((1, tm, MLA_ROPE), tok)],
        out_specs=[pl.BlockSpec((1, tm, MLA_KV_LORA), tok), pl.BlockSpec((1, tm, MLA_ROPE), tok),
                   pl.BlockSpec((1, tm, nkv), tok)],
        out_shape=[jax.ShapeDtypeStruct((nseg, seg, MLA_KV_LORA), F32),
                   jax.ShapeDtypeStruct((nseg, seg, MLA_ROPE), F32),
                   jax.ShapeDtypeStruct((nseg, seg, nkv), F32)],
        compiler_params=_params(("arbitrary", "arbitrary")),
        name="mla_kv",
    )(p, p, kv_norm.reshape(1, -1), w_kv, cos_k, sin_k)


def _mm_kernel(a_ref, w_ref, o_ref):
    o_ref[...] = _dot(a_ref[...].astype(BF16), w_ref[...])


def matmul(a, w_bf16, tm):
    m, k = a.shape
    n = w_bf16.shape[1]
    return pl.pallas_call(
        _mm_kernel,
        grid=(m // tm,),
        in_specs=[pl.BlockSpec((tm, k), lambda i: (i, 0)), pl.BlockSpec((k, n), lambda i: (0, 0))],
        out_specs=pl.BlockSpec((tm, n), lambda i: (i, 0)),
        out_shape=jax.ShapeDtypeStruct((m, n), F32),
        compiler_params=_params(("arbitrary",)),
        name="matmul",
    )(a, w_bf16)


PEER_RT = 128
NOT_TOP = 99.0
RANK_CODE = 2.0 ** 100


def _top16(s, exact):
    vals = []
    if exact:
        key = lax.broadcasted_iota(jnp.int32, s.shape, 0).astype(F32)
        rank = jnp.full(s.shape, NOT_TOP, F32)
        for r in range(PEER_TOPK):
            m = jnp.max(s, axis=0, keepdims=True)
            hit = key == jnp.min(jnp.where(s == m, key, 1e9), axis=0, keepdims=True)
            rank = jnp.where(hit, float(r), rank)
            s = jnp.where(hit, NEG_INF, s)
            vals.append(m)
        return vals, rank
    for r in range(PEER_TOPK):
        m = jnp.max(s, axis=0, keepdims=True)
        s = jnp.where(s == m, -RANK_CODE * (1.0 + r / 32.0), s)
        vals.append(m)
    return vals, jnp.where(s <= -0.5 * RANK_CODE, s * (-32.0 / RANK_CODE) - 32.0, NOT_TOP)


def _pair_topk(av, bv, exact):
    n = av[0].shape[-1]
    a_lo, a_hi = jnp.concatenate(av[:8], 0), jnp.concatenate(av[8:], 0)
    b_lo, b_hi = jnp.concatenate(bv[:8], 0), jnp.concatenate(bv[8:], 0)
    row = lax.broadcasted_iota(jnp.int32, (8, n), 0).astype(F32)

    no_pos = 1e8

    def rows_b(a, b_blk, boff, nvalid):
        ok = row < nvalid
        return jnp.where(ok, av[a] + b_blk, NEG_INF), jnp.where(ok, a * 16.0 + boff + row, no_pos)

    def rows_a(b, a_blk, aoff, lo, hi):
        ok = (row >= lo) & (row < hi)
        return jnp.where(ok, a_blk + bv[b], NEG_INF), jnp.where(ok, (aoff + row) * 16.0 + b, no_pos)

    groups = [rows_b(0, b_lo, 0, 8), rows_b(0, b_hi, 8, 8), rows_b(1, b_lo, 0, 8), rows_b(2, b_lo, 0, 5),
              rows_b(3, b_lo, 0, 4), rows_a(0, a_lo, 0, 4, 8), rows_a(0, a_hi, 8, 0, 8),
              rows_a(1, a_lo, 0, 4, 8), rows_a(2, a_lo, 0, 4, 5)]
    cands = [g[0] for g in groups]
    poss = [g[1] for g in groups]
    sels = [jnp.zeros((8, n), F32) for _ in groups]
    top = av[0] + bv[0]
    z = jnp.zeros((1, n), F32)
    for _ in range(PEER_TOPK):
        m = functools.reduce(jnp.maximum, cands)
        m = jnp.max(m, axis=0, keepdims=True)
        if exact:
            first = functools.reduce(jnp.minimum, [jnp.where(c == m, p, 1e9) for c, p in zip(cands, poss)])
            first = jnp.min(first, axis=0, keepdims=True)
            hits = [p == first for p in poss]
            cands = [jnp.where(hh, NEG_INF, c) for hh, c in zip(hits, cands)]
            sels = [jnp.where(hh, 1.0, s) for hh, s in zip(hits, sels)]
        else:
            cands = [jnp.where(c == m, -RANK_CODE, c) for c in cands]
        z = z + jnp.exp(m - top)
    if not exact:
        sels = [jnp.where(c == -RANK_CODE, 1.0, 0.0) for c in cands]
    cnt = lambda x: jnp.sum(x, axis=0, keepdims=True)
    cut_lo = sels[5] + sels[7] + sels[8]
    for a, c in enumerate([cnt(sels[0]) + cnt(sels[1]), cnt(sels[2]), cnt(sels[3]), cnt(sels[4])]):
        cut_lo = cut_lo + jnp.where(row == a, c, 0.0)
    return cut_lo, sels[6], z, cnt(cut_lo) + cnt(sels[6])


def _peer_route_kernel(x_ref, sh_ref, sc_ref, wq_ref, sk_ref, xm_ref, e1_ref, cut_ref, e2_ref, r2_ref, q_s, *, tm):
    xm = (x_ref[0] * (1.0 + sc_ref[0]) + sh_ref[0]).astype(BF16)
    xm_ref[0] = xm
    q = _dot(xm, wq_ref[...])
    for hp in range(2 * PEER_HEADS):
        q_s[hp] = q[:, hp * PEER_HALF:(hp + 1) * PEER_HALF]

    def route(h, tok, exact):
        def scores(hp):
            return lax.dot_general(sk_ref[hp], q_s[hp, tok, :], (((1,), (1,)), ((), ())),
                                   precision=lax.Precision.HIGHEST, preferred_element_type=F32)

        s1, s2 = scores(2 * h), scores(2 * h + 1)
        av, rank1 = _top16(s1, exact)
        bv, rank2 = _top16(s2, exact)
        cut_lo, cut_hi, z, nsel = _pair_topk(av, bv, exact)
        cut = jnp.zeros_like(s1)
        for r in range(PEER_TOPK):
            src = cut_lo if r < 8 else cut_hi
            cut = jnp.where(rank1 == float(r), src[r % 8:r % 8 + 1, :], cut)
        e1_ref[0, h, :, tok] = (jnp.exp(s1 - av[0]) / z).astype(BF16)
        cut_ref[0, h, :, tok] = cut.astype(BF16)
        e2_ref[0, h, :, tok] = jnp.exp(s2 - bv[0]).astype(BF16)
        r2_ref[0, h, :, tok] = rank2.astype(BF16)
        ranked = lambda rk: jnp.sum(jnp.where(rk < PEER_TOPK, 1.0, 0.0), axis=0, keepdims=True)
        return ranked(rank1), ranked(rank2), nsel

    def body(h, carry):
        toks = [pl.ds(t0, PEER_RT) for t0 in range(0, tm, PEER_RT)]
        counts = [route(h, tok, exact=False) for tok in toks]
        for tok, cnts in zip(toks, counts):
            bad = functools.reduce(jnp.maximum, [jnp.abs(cn - PEER_TOPK) for cn in cnts])

            @pl.when(jnp.max(bad) > 0.0)
            def _():
                route(h, tok, exact=True)
        return carry

    lax.fori_loop(0, PEER_HEADS, body, 0)


def peer_route(x3, mod3, shift_chunk, wq_bf16, subkeys, tm=512):
    nseg, seg, d = x3.shape
    tok = lambda s, i: (s, i, 0)
    rshape = jax.ShapeDtypeStruct((nseg, PEER_HEADS, PEER_NKEYS, seg), BF16)
    rspec = pl.BlockSpec((1, PEER_HEADS, PEER_NKEYS, tm), lambda s, i: (s, 0, 0, i))
    return pl.pallas_call(
        functools.partial(_peer_route_kernel, tm=tm),
        grid=(nseg, seg // tm),
        in_specs=[pl.BlockSpec((1, tm, d), tok),
                  pl.BlockSpec((1, 1, d), lambda s, i: (s, 0, shift_chunk)),
                  pl.BlockSpec((1, 1, d), lambda s, i: (s, 0, shift_chunk + 1)),
                  pl.BlockSpec(wq_bf16.shape, lambda s, i: (0, 0)),
                  pl.BlockSpec((2 * PEER_HEADS, PEER_NKEYS, PEER_HALF), lambda s, i: (0, 0, 0))],
        out_specs=[pl.BlockSpec((1, tm, d), tok), rspec, rspec, rspec, rspec],
        out_shape=[jax.ShapeDtypeStruct((nseg, seg, d), BF16)] + [rshape] * 4,
        scratch_shapes=[pltpu.VMEM((2 * PEER_HEADS, tm, PEER_HALF), F32)],
        compiler_params=_params(("arbitrary", "arbitrary")),
        name="peer_route",
    )(x3, mod3, mod3, wq_bf16, subkeys.reshape(2 * PEER_HEADS, PEER_NKEYS, PEER_HALF))


PEER_CE = 1024


def _gelu_tanh(x):
    return 0.5 * x * (1.0 + jnp.tanh(0.7978845608028654 * (x + 0.044715 * x * x * x)))


def _peer_dense_kernel(xm_ref, u_ref, vt_ref, e1_ref, cut_ref, e2_ref, r2_ref, x_ref, gate_ref, g_ref, b_ref,
                       o_ref, acc_s, at_s, w_s, e2_s, r2_s, *, tm):
    e = pl.program_id(2)
    nb = PEER_CE // PEER_NKEYS
    ntt = tm // PEER_RT

    @pl.when(e == 0)
    def _():
        acc_s[...] = jnp.zeros_like(acc_s)
        e2_s[:, :, :tm] = e2_ref[0]
        r2_s[:, :, :tm] = r2_ref[0]

    packed = (PEER_NKEYS // 16, 16, PEER_RT)
    ng = 2

    def gate_tiles(tt, i0):
        tok = slice(tt * PEER_RT, (tt + 1) * PEER_RT)
        gmats = [jnp.zeros(packed, BF16) for _ in range(ng)]
        for h in range(PEER_HEADS):
            e2 = e2_s[h, :, tok].reshape(packed)
            r2 = r2_s[h, :, tok].reshape(packed)
            for k in range(ng):
                i = i0 + k
                e1 = jnp.broadcast_to(e1_ref[0, h, i:i + 1, tok], (16, PEER_RT))[None]
                cut = jnp.broadcast_to(cut_ref[0, h, i:i + 1, tok], (16, PEER_RT))[None]
                gmats[k] = gmats[k] + e1 * jnp.where(r2 < cut, e2, jnp.zeros_like(e2))
        for k in range(ng):
            rows = slice((i0 + k) * PEER_NKEYS, (i0 + k + 1) * PEER_NKEYS)
            act = _gelu_tanh(at_s[rows, tok]).astype(BF16)
            w_s[rows, tok] = gmats[k].reshape(PEER_NKEYS, PEER_RT) * act

    at_s[:, :tm] = _dot(u_ref[0], xm_ref[0], ((1,), (1,)))
    for tt in range(ntt):
        for i0 in range(0, nb, ng):
            gate_tiles(tt, i0)
    acc_s[:, :tm] += _dot(vt_ref[0, 0], w_s[:, :tm])

    @pl.when(e == pl.num_programs(2) - 1)
    def _():
        z = DEEPNORM_ALPHA * x_ref[0] + gate_ref[0] * acc_s[:, :tm].T
        o_ref[0] = _layer_norm_rows(z, g_ref[...], b_ref[...])


def peer_dense(xm, u_all, vt_all, l, e1, cut, e2, r2, x3, mod3, gate_chunk, ln_g, ln_b, tm=1024):
    nseg, seg, d = x3.shape
    ne = u_all.shape[1]
    nb = PEER_CE // PEER_NKEYS
    tp = tm + PEER_RT
    tok = lambda s, i, e: (s, i, 0)
    chunk = pl.BlockSpec((1, PEER_HEADS, nb, tm), lambda s, i, e: (s, 0, e, i))
    full = pl.BlockSpec((1, PEER_HEADS, PEER_NKEYS, tm), lambda s, i, e: (s, 0, 0, i))
    return pl.pallas_call(
        functools.partial(_peer_dense_kernel, tm=tm),
        grid=(nseg, seg // tm, ne // PEER_CE),
        in_specs=[pl.BlockSpec((1, tm, d), tok),
                  pl.BlockSpec((1, PEER_CE, d), lambda s, i, e: (l, e, 0)),
                  pl.BlockSpec((1, 1, d, PEER_CE), lambda s, i, e: (l, e, 0, 0)),
                  chunk, chunk, full, full,
                  pl.BlockSpec((1, tm, d), tok),
                  pl.BlockSpec((1, 1, d), lambda s, i, e: (s, 0, gate_chunk)),
                  pl.BlockSpec((1, d), lambda s, i, e: (0, 0)),
                  pl.BlockSpec((1, d), lambda s, i, e: (0, 0))],
        out_specs=pl.BlockSpec((1, tm, d), tok),
        out_shape=jax.ShapeDtypeStruct((nseg, seg, d), F32),
        scratch_shapes=[pltpu.VMEM((d, tp), F32), pltpu.VMEM((PEER_CE, tp), F32), pltpu.VMEM((PEER_CE, tp), BF16),
                        pltpu.VMEM((PEER_HEADS, PEER_NKEYS, tp), BF16), pltpu.VMEM((PEER_HEADS, PEER_NKEYS, tp), BF16)],
        compiler_params=_params(("arbitrary", "arbitrary", "arbitrary")),
        name="peer_dense",
    )(xm, u_all, vt_all, e1, cut, e2, r2, x3, mod3, ln_g.reshape(1, d), ln_b.reshape(1, d))


def peer_layer(x3, mod3, l, wq, subkeys, u_all, vt_all, ln_g, ln_b):
    xm, e1, cut, e2, r2 = peer_route(x3, mod3, 3, wq.astype(BF16), subkeys)
    return peer_dense(xm, u_all, vt_all, l, e1, cut, e2, r2, x3, mod3, 5, ln_g, ln_b)


def _pad_cols(w, n):
    return jnp.pad(w, ((0, 0), (0, n - w.shape[1])))


def _stream(prompt_part, sample_part):
    return jnp.concatenate([prompt_part.reshape(1, -1, prompt_part.shape[-1]), sample_part], axis=0)


def _head_major(a, heads):
    b, t, _ = a.shape
    return jnp.transpose(a.reshape(b, t, heads, -1), (0, 2, 1, 3))


def _token_major(a):
    b, h, t, dh = a.shape
    return jnp.transpose(a, (0, 2, 1, 3)).reshape(b, t, h * dh)


MLSTM_CHUNK = 256
GLA_CHUNK = 64
NPROJ = 3200


def mlstm_layer(x3, mod3, bp, lp, st_c, st_n, st_m, w_in, b_gate, norm_w, w_out, ln_g, ln_b):
    nseg, seg, _ = x3.shape
    bs = nseg - 1
    p = mod_matmul(x3, mod3, 0, _pad_cols(w_in, NPROJ).astype(BF16))
    graw = p[:, :, 3072:3088]
    gp = graw[0].reshape(bp, lp, 16)
    zc = jnp.zeros((bp, 8, M_DK, M_DV), F32)
    zn = jnp.zeros((bp, 8, M_DK), F32)
    hfp, hbp, c_new, n_new, m_new = mlstm_scan(p.reshape(nseg * bp, lp, NPROJ), 0, bp, lp, gp,
                                               jnp.swapaxes(gp, 1, 2), b_gate, zc, zn, zn, min(MLSTM_CHUNK, lp),
                                               nseg * bp, 0)
    gs = graw[1:]
    hf, hb, _, _, _ = mlstm_scan(p, 1, bs, seg, gs, jnp.swapaxes(gs, 1, 2), b_gate,
                                 st_c.reshape(bs, 8, M_DK, M_DV), st_n.reshape(bs, 8, M_DK),
                                 jnp.broadcast_to(st_m.reshape(bs, 8, 1), (bs, 8, M_DK)), MLSTM_CHUNK,
                                 nseg, 1, prev=(hfp, hbp))
    x3 = outproj_ln("mlstm", (hf, hb), x3, mod3, 2, w_out.astype(BF16), ln_g, ln_b,
                    norm_w=norm_w, og=p, og_col=2)
    return (x3, c_new.reshape(bp, 2, M_HEADS, M_DK, M_DV), n_new.reshape(bp, 2, M_HEADS, M_DK),
            m_new[:, :, 0].reshape(bp, 2, M_HEADS))


def gla_layer(x3, mod3, bp, lp, st_s, w_in, w_gate2, b_gate2, norm_w, w_out, ln_g, ln_b):
    nseg, seg, _ = x3.shape
    bs = nseg - 1
    p = mod_matmul(x3, mod3, 0, _pad_cols(w_in, NPROJ).astype(BF16))
    gr = p[:, :, 3072:3104]
    zs = jnp.zeros((bp, 8, G_DV, G_DK), F32)
    ofp, obp, s_new = gla_scan(p.reshape(nseg * bp, lp, NPROJ), 0, bp, lp, gr[0].reshape(bp, lp, 32),
                               w_gate2, b_gate2, zs, GLA_CHUNK, nseg * bp, 0)
    s0t = jnp.swapaxes(st_s.reshape(bs, 8, G_DK, G_DV), -1, -2)
    of, ob, _ = gla_scan(p, 1, bs, seg, gr[1:], w_gate2, b_gate2, s0t, GLA_CHUNK, nseg, 1, prev=(ofp, obp))
    x3 = outproj_ln("gla", (of, ob), x3, mod3, 2, w_out.astype(BF16), ln_g, ln_b,
                    norm_w=jnp.tile(norm_w, G_HEADS), og=p, og_col=2)
    return x3, jnp.swapaxes(s_new, -1, -2).reshape(bp, 2, G_HEADS, G_DK, G_DV)


def na_layer(x3, mod3, bp, lp, cache_k, cache_v, w_in, rpb, w_out, ln_g, ln_b):
    nseg, seg, _ = x3.shape
    bs = nseg - 1
    hd = NA_HEADS * NA_HD
    p = mod_matmul(x3, mod3, 0, w_in.astype(BF16))
    pp = p[0].reshape(bp, lp, 3 * hd)
    hm = lambda a: _head_major(a, NA_HEADS).astype(BF16)
    yp = attention(hm(pp[..., :hd]), hm(pp[..., hd:2 * hd]), hm(pp[..., 2 * hd:]), lp)
    ps = p[1:]
    ys = na_attention(hm(ps[..., :hd]), hm(ps[..., hd:2 * hd]), hm(ps[..., 2 * hd:]),
                      hm(cache_k.reshape(bs, -1, hd)), hm(cache_v.reshape(bs, -1, hd)), na_bias_table(rpb))
    x3 = outproj_ln("plain", _stream(_token_major(yp), _token_major(ys)), x3, mod3, 2, w_out.astype(BF16), ln_g, ln_b)
    return (x3, pp[..., hd:2 * hd].reshape(bp, lp, NA_HEADS, NA_HD), pp[..., 2 * hd:].reshape(bp, lp, NA_HEADS, NA_HD))


def _rope_rotated_cols(w):
    q = MLA_ROPE // 4
    return jnp.concatenate([-w[..., q:2 * q], w[..., :q], -w[..., 3 * q:], w[..., 2 * q:3 * q]], axis=-1)


def _rope_tables(ts):
    ra = MLA_ROPE // 2
    t = np.arange(ts)
    inv = 1.0 / (ROPE_BASE ** (np.arange(0, ra, 2, dtype=np.float32) / ra))
    ang_r = (t // GRID_W).astype(np.float32)[:, None] * inv[None, :]
    ang_c = (t % GRID_W).astype(np.float32)[:, None] * inv[None, :]
    ang = np.concatenate([ang_r, ang_r, ang_c, ang_c], axis=-1).astype(np.float32)
    return jnp.cos(jnp.asarray(ang)), jnp.sin(jnp.asarray(ang))


def mla_layer(x3, mod3, bp, lp, cache_ckv, cache_kpe, w_in, q_norm, w_qup, kv_norm, w_kvup, w_out, ln_g, ln_b):
    nseg, seg, _ = x3.shape
    bs = nseg - 1
    nq = MLA_Q_LORA + MLA_KV_LORA
    w_ext = jnp.concatenate([w_in, _rope_rotated_cols(w_in[:, nq:])], axis=1)
    p = mod_matmul(x3, mod3, 0, _pad_cols(w_ext, 896).astype(BF16))
    cos_t, sin_t = _rope_tables(seg)
    cos3 = jnp.concatenate([jnp.ones((1, seg, MLA_ROPE), F32), jnp.broadcast_to(cos_t, (bs, seg, MLA_ROPE))], 0)
    sin3 = jnp.concatenate([jnp.zeros((1, seg, MLA_ROPE), F32), jnp.broadcast_to(sin_t, (bs, seg, MLA_ROPE))], 0)
    wq = w_qup.reshape(MLA_Q_LORA, MLA_HEADS, MLA_NOPE + MLA_ROPE)
    wq_rope = wq[:, :, MLA_NOPE:]
    w_q3 = jnp.concatenate([wq[:, :, :MLA_NOPE].reshape(MLA_Q_LORA, -1), wq_rope.reshape(MLA_Q_LORA, -1),
                            _rope_rotated_cols(wq_rope).reshape(MLA_Q_LORA, -1)], axis=1).astype(BF16)
    q_all = mla_q(p, q_norm, w_q3, jnp.tile(cos3, (1, 1, MLA_HEADS)), jnp.tile(sin3, (1, 1, MLA_HEADS)))
    wkv = w_kvup.reshape(MLA_KV_LORA, MLA_HEADS, MLA_NOPE + MLA_VD)
    w_kv2 = jnp.concatenate([wkv[:, :, :MLA_NOPE].reshape(MLA_KV_LORA, -1),
                             wkv[:, :, MLA_NOPE:].reshape(MLA_KV_LORA, -1)], axis=1).astype(BF16)
    ckvn, kpe, kv = mla_kv(p, kv_norm, w_kv2, cos3, sin3)
    kvc = matmul(cache_ckv.reshape(-1, MLA_KV_LORA), w_kv2, 512).reshape(bs, -1, w_kv2.shape[1])
    nn = MLA_HEADS * MLA_NOPE

    def heads(q_rows, kv_rows, kpe_rows):
        b, t, _ = q_rows.shape
        tk = kv_rows.shape[1]
        qh = jnp.concatenate([q_rows[..., :nn].reshape(b, t, MLA_HEADS, MLA_NOPE),
                              q_rows[..., nn:].reshape(b, t, MLA_HEADS, MLA_ROPE)], -1)
        kh = jnp.concatenate([kv_rows[..., :nn].reshape(b, tk, MLA_HEADS, MLA_NOPE),
                              jnp.broadcast_to(kpe_rows[:, :, None, :], (b, tk, MLA_HEADS, MLA_ROPE))], -1)
        vh = kv_rows[..., nn:].reshape(b, tk, MLA_HEADS, MLA_VD)
        tr = lambda a: jnp.transpose(a, (0, 2, 1, 3)).astype(BF16)
        return tr(qh), tr(kh), tr(vh)

    yp = attention(*heads(q_all[0].reshape(bp, lp, -1), kv[0].reshape(bp, lp, -1), kpe[0].reshape(bp, lp, -1)), lp)
    ys = attention(*heads(q_all[1:], jnp.concatenate([kv[1:], kvc], 1), jnp.concatenate([kpe[1:], cache_kpe], 1)), 256)
    x3 = outproj_ln("plain", _stream(_token_major(yp), _token_major(ys)), x3, mod3, 2, w_out.astype(BF16), ln_g, ln_b)
    return x3, ckvn[0].reshape(bp, lp, MLA_KV_LORA), kpe[0].reshape(bp, lp, MLA_ROPE)


def kernel(x_prompt, x_sample, c, c_ctx, state_mlstm_C, state_mlstm_n, state_mlstm_m, state_gla_S, cache_na_k, cache_na_v, cache_mla_ckv, cache_mla_kpe, ada_w, ada_b, ln_mix_g, ln_mix_b, ln_ffn_g, ln_ffn_b, mlstm_w_in, mlstm_b_gate, mlstm_norm_w, mlstm_w_out, gla_w_in, gla_w_gate2, gla_b_gate2, gla_norm_w, gla_w_out, na_w_in, na_rpb, na_w_out, mla_w_in, mla_q_norm, mla_w_qup, mla_kv_norm, mla_w_kvup, mla_w_out, peer_w_q, peer_subkeys, peer_u, peer_v):
    bp, lp, d = x_prompt.shape
    bs, ts, _ = x_sample.shape
    assert bp * lp == ts and bs + 1 <= 8
    x3 = _stream(x_prompt, x_sample)
    cond8 = jnp.zeros((8, d), F32).at[0].set(c_ctx).at[1:1 + bs].set(c)
    mods = adaln_all(cond8, ada_w, ada_b)
    u_all = peer_u.astype(BF16)
    vt_all = jnp.swapaxes(peer_v.reshape(DEPTH, -1, PEER_CE, d), 2, 3).astype(BF16)
    outs = {}
    for l in range(DEPTH):
        mod3 = mods[l].reshape(8, 1, ADA_CHUNKS * d)
        kind = l % 4
        if kind == 0:
            x3, outs["C"], outs["n"], outs["m"] = mlstm_layer(
                x3, mod3, bp, lp, state_mlstm_C, state_mlstm_n, state_mlstm_m, mlstm_w_in, mlstm_b_gate,
                mlstm_norm_w, mlstm_w_out, ln_mix_g[l], ln_mix_b[l])
        elif kind == 1:
            x3, outs["S"] = gla_layer(x3, mod3, bp, lp, state_gla_S, gla_w_in, gla_w_gate2, gla_b_gate2,
                                      gla_norm_w, gla_w_out, ln_mix_g[l], ln_mix_b[l])
        elif kind == 2:
            x3, outs["nk"], outs["nv"] = na_layer(x3, mod3, bp, lp, cache_na_k, cache_na_v, na_w_in, na_rpb,
                                                  na_w_out, ln_mix_g[l], ln_mix_b[l])
        else:
            x3, outs["ckv"], outs["kpe"] = mla_layer(x3, mod3, bp, lp, cache_mla_ckv, cache_mla_kpe, mla_w_in,
                                                     mla_q_norm, mla_w_qup, mla_kv_norm, mla_w_kvup, mla_w_out,
                                                     ln_mix_g[l], ln_mix_b[l])
        x3 = peer_layer(x3, mod3, l, peer_w_q[l], peer_subkeys[l], u_all, vt_all, ln_ffn_g[l], ln_ffn_b[l])
    return (x3[0].reshape(bp, lp, d), x3[1:], outs["C"], outs["n"], outs["m"], outs["S"], outs["nk"], outs["nv"],
            outs["ckv"], outs["kpe"])
```

```python
import functools

import numpy as np
import jax
import jax.numpy as jnp
from jax import lax
from jax.experimental import pallas as pl
from jax.experimental.pallas import tpu as pltpu

D_MODEL = 1024
DEPTH = 4
GRID_W = 64
DEEPNORM_ALPHA = (2.0 * DEPTH) ** 0.25
ADA_CHUNKS = 6
NORM_EPS = 1e-5
SEG = 4096
NSEG = 3

M_HEADS, M_DK, M_DV = 4, 128, 256
G_HEADS, G_DK, G_DV = 4, 128, 256
G_GATE_RANK = 16
G_GATE_NORM = 16.0
NA_HEADS, NA_HD, NA_ROWS, NA_COLS = 16, 64, 8, 16
MLA_HEADS, MLA_Q_LORA, MLA_KV_LORA, MLA_NOPE, MLA_ROPE, MLA_VD = 16, 512, 256, 64, 32, 64
ROPE_BASE = 10000.0
PEER_HEADS, PEER_NKEYS, PEER_HALF, PEER_TOPK = 8, 128, 128, 16

V7X_VMEM_LIMIT = 56 * 1024 * 1024
F32 = jnp.float32
BF16 = jnp.bfloat16
NEG_INF = float("-inf")


def _params(sem, vmem=V7X_VMEM_LIMIT):
    return pltpu.CompilerParams(dimension_semantics=sem, vmem_limit_bytes=vmem)


def _dot(a, b, dims=((1,), (0,))):
    return lax.dot_general(a, b, (dims, ((), ())), preferred_element_type=F32)


def _split3(a):
    hi = a.astype(BF16)
    r1 = a - hi.astype(F32)
    mid = r1.astype(BF16)
    lo = (r1 - mid.astype(F32)).astype(BF16)
    return hi, mid, lo


def _dot_exact_lhs(m01, a):
    hi, mid, lo = _split3(a)
    return _dot(m01, hi) + _dot(m01, mid) + _dot(m01, lo)


def _dot_exact_rhs(a, m01):
    hi, mid, lo = _split3(a)
    return _dot(hi, m01) + _dot(mid, m01) + _dot(lo, m01)


def _log_sigmoid(x):
    return jnp.minimum(x, 0.0) - jnp.log(1.0 + jnp.exp(-jnp.abs(x)))


def _sigmoid(x):
    return 1.0 / (1.0 + jnp.exp(-x))


def _adaln_kernel(c_ref, w_ref, b_ref, o_ref):
    cv = c_ref[...]
    a = cv * _sigmoid(cv)
    o_ref[0] = lax.dot_general(a, w_ref[0], (((1,), (0,)), ((), ())), precision=lax.Precision.HIGHEST,
                               preferred_element_type=F32) + b_ref[0]


def adaln_all(cond8, ada_w, ada_b):
    tn = 1024
    n = ada_w.shape[-1]
    return pl.pallas_call(
        _adaln_kernel,
        grid=(DEPTH, n // tn),
        in_specs=[pl.BlockSpec((8, D_MODEL), lambda l, j: (0, 0)),
                  pl.BlockSpec((1, D_MODEL, tn), lambda l, j: (l, 0, j)),
                  pl.BlockSpec((1, 1, tn), lambda l, j: (l, 0, j))],
        out_specs=pl.BlockSpec((1, 8, tn), lambda l, j: (l, 0, j)),
        out_shape=jax.ShapeDtypeStruct((DEPTH, 8, n), F32),
        compiler_params=_params(("arbitrary", "arbitrary")),
        name="adaln",
    )(cond8, ada_w, ada_b.reshape(DEPTH, 1, n))


def _modmm_kernel(x_ref, sh_ref, sc_ref, w_ref, o_ref, xm_ref):
    @pl.when(pl.program_id(2) == 0)
    def _():
        xm_ref[...] = (x_ref[0] * (1.0 + sc_ref[0]) + sh_ref[0]).astype(BF16)

    o_ref[0] = _dot(xm_ref[...], w_ref[...]).astype(o_ref.dtype)


def mod_matmul(x3, mod3, shift_chunk, w_bf16, tm=512, tn=None, out_dtype=F32):
    nseg, seg, d = x3.shape
    n = w_bf16.shape[1]
    tn = n if tn is None else tn
    return pl.pallas_call(
        _modmm_kernel,
        grid=(nseg, seg // tm, n // tn),
        in_specs=[pl.BlockSpec((1, tm, d), lambda s, i, j: (s, i, 0)),
                  pl.BlockSpec((1, 1, d), lambda s, i, j: (s, 0, shift_chunk)),
                  pl.BlockSpec((1, 1, d), lambda s, i, j: (s, 0, shift_chunk + 1)),
                  pl.BlockSpec((d, tn), lambda s, i, j: (0, j))],
        out_specs=pl.BlockSpec((1, tm, tn), lambda s, i, j: (s, i, j)),
        out_shape=jax.ShapeDtypeStruct((nseg, seg, n), out_dtype),
        scratch_shapes=[pltpu.VMEM((tm, d), BF16)],
        compiler_params=_params(("arbitrary", "arbitrary", "arbitrary")),
        name="mod_matmul",
    )(x3, mod3, mod3, w_bf16)


def _layer_norm_rows(y, g, b):
    mu = jnp.mean(y, axis=-1, keepdims=True)
    yc = y - mu
    var = jnp.mean(yc * yc, axis=-1, keepdims=True)
    return yc * lax.rsqrt(var + NORM_EPS) * g + b


def _outproj_kernel(*refs, mode):
    if mode == "plain":
        y_ref, x_ref, gate_ref, w_ref, g_ref, b_ref, o_ref = refs
        yin = y_ref[0].astype(BF16)
    else:
        ya_ref, yb_ref, og_ref, nw_ref, x_ref, gate_ref, w_ref, g_ref, b_ref, o_ref = refs
        hs = ya_ref[0] + yb_ref[0]
        og = og_ref[0]
        parts = []
        for h in range(4):
            seg = hs[:, h * 256:(h + 1) * 256]
            nw = nw_ref[:, h * 256:(h + 1) * 256]
            if mode == "mlstm":
                mu = jnp.mean(seg, axis=-1, keepdims=True)
                sc = seg - mu
                var = jnp.mean(sc * sc, axis=-1, keepdims=True)
                parts.append(sc * lax.rsqrt(var + NORM_EPS) * nw)
            else:
                ms = jnp.mean(seg * seg, axis=-1, keepdims=True)
                parts.append(seg * lax.rsqrt(ms + NORM_EPS) * nw)
        hn = jnp.concatenate(parts, axis=-1)
        act = _sigmoid(og) if mode == "mlstm" else og * _sigmoid(og)
        yin = (act * hn).astype(BF16)
    y = _dot(yin, w_ref[...])
    z = DEEPNORM_ALPHA * x_ref[0] + gate_ref[0] * y
    o_ref[0] = _layer_norm_rows(z, g_ref[...], b_ref[...])


def outproj_ln(mode, ys, x3, mod3, gate_chunk, w_bf16, ln_g, ln_b, norm_w=None, og=None, og_col=0, tm=512):
    nseg, seg, d = x3.shape
    k = w_bf16.shape[0]
    tok = lambda s, i: (s, i, 0)
    if mode == "plain":
        args = [ys]
        specs = [pl.BlockSpec((1, tm, k), tok)]
    else:
        args = [ys[0], ys[1], og, norm_w.reshape(1, k)]
        specs = [pl.BlockSpec((1, tm, k), tok), pl.BlockSpec((1, tm, k), tok),
                 pl.BlockSpec((1, tm, k), lambda s, i: (s, i, og_col)),
                 pl.BlockSpec((1, k), lambda s, i: (0, 0))]
    args += [x3, mod3, w_bf16, ln_g.reshape(1, d), ln_b.reshape(1, d)]
    specs += [pl.BlockSpec((1, tm, d), tok),
              pl.BlockSpec((1, 1, d), lambda s, i: (s, 0, gate_chunk)),
              pl.BlockSpec((k, d), lambda s, i: (0, 0)),
              pl.BlockSpec((1, d), lambda s, i: (0, 0)),
              pl.BlockSpec((1, d), lambda s, i: (0, 0))]
    return pl.pallas_call(
        functools.partial(_outproj_kernel, mode=mode),
        grid=(nseg, seg // tm),
        in_specs=specs,
        out_specs=pl.BlockSpec((1, tm, d), tok),
        out_shape=jax.ShapeDtypeStruct((nseg, seg, d), F32),
        compiler_params=_params(("arbitrary", "arbitrary")),
        name="outproj_ln_" + mode,
    )(*args)


def _tri(n, lower):
    r = lax.broadcasted_iota(jnp.int32, (n, n), 0)
    c = lax.broadcasted_iota(jnp.int32, (n, n), 1)
    return (c <= r) if lower else (c >= r)


def _mlstm_kernel(rowf_ref, rowb_ref, seq_ref, first_ref, last_ref,
                  pf_ref, pb_ref, gf_ref, gb_ref, gtf_ref, gtb_ref, bias_ref, biast_ref,
                  c0_ref, n0_ref, m0_ref, hf_ref, hb_ref, co_ref, no_ref, mo_ref,
                  c_s, n_s, m_s, *, L):
    item = pl.program_id(0)

    @pl.when(first_ref[item] == 1)
    def _():
        c_s[...] = c0_ref[0]
        n_s[...] = n0_ref[0]
        m_s[...] = m0_ref[0]

    dirs = [(pf_ref, gf_ref, gtf_ref, hf_ref), (pb_ref, gb_ref, gtb_ref, hb_ref)]
    masks = [_tri(L, lower=True), _tri(L, lower=False)]
    gates = []
    for d in range(2):
        g = dirs[d][1][0] + bias_ref[...]
        gt = dirs[d][2][0] + biast_ref[...]
        lf_c = _log_sigmoid(g[:, d * 8 + 4:d * 8 + 8])
        lf_r = _log_sigmoid(gt[d * 8 + 4:d * 8 + 8, :])
        b_c = _dot_exact_lhs(masks[d].astype(BF16), lf_c)
        b_r = _dot_exact_rhs(lf_r, masks[1 - d].astype(BF16))
        gates.append((g[:, d * 8:d * 8 + 4], gt[d * 8:d * 8 + 4, :], b_c, b_r))
    units = [(d, h) for d in range(2) for h in range(M_HEADS)]

    def unit_inputs(d, h):
        p_ref = dirs[d][0]
        q = p_ref[0, :, h * M_DK:(h + 1) * M_DK]
        k = p_ref[0, :, 512 + h * M_DK:512 + (h + 1) * M_DK] * (M_DK ** -0.5)
        v = p_ref[0, :, 1024 + h * M_DV:1024 + (h + 1) * M_DV].astype(BF16)
        li_c, li_r, b_c, b_r = gates[d]
        return q, k, v, li_c[:, h:h + 1], li_r[h:h + 1, :], b_c[:, h:h + 1], b_r[h:h + 1, :]

    qks = []
    for d, h in units:
        q, k, _, _, _, _, _ = unit_inputs(d, h)
        qks.append(_dot(q.astype(BF16), k.astype(BF16), ((1,), (1,))))
    smats, eis, mts = [], [], []
    for u, (d, h) in enumerate(units):
        _, _, _, _, lir, bc, br = unit_inputs(d, h)
        m_prev = m_s[u:u + 1, 0:1]
        dmat = jnp.where(masks[d], bc - br + lir, NEG_INF)
        inter = bc + m_prev
        mt = jnp.maximum(inter, jnp.max(dmat, axis=-1, keepdims=True))
        smats.append(qks[u] * jnp.exp(dmat - mt))
        eis.append(jnp.exp(inter - mt))
        mts.append(mt)
    nums = []
    for u, (d, h) in enumerate(units):
        q, _, v, _, _, _, _ = unit_inputs(d, h)
        nums.append(_dot(smats[u].astype(BF16), v) + eis[u] * _dot(q.astype(BF16), c_s[u].astype(BF16)))
    for u, (d, h) in enumerate(units):
        q, _, _, _, _, _, _ = unit_inputs(d, h)
        den = (jnp.sum(smats[u], axis=-1, keepdims=True)
               + eis[u] * jnp.sum(q * n_s[u:u + 1, :], axis=-1, keepdims=True))
        dirs[d][3][0, :, h * M_DV:(h + 1) * M_DV] = nums[u] / jnp.maximum(jnp.abs(den), jnp.exp(-mts[u]))
    for u, (d, h) in enumerate(units):
        _, k, v, lic, lir, bc, br = unit_inputs(d, h)
        last = L - 1 if d == 0 else 0
        m_prev = m_s[u:u + 1, 0:1]
        tot = br[:, last:last + 1]
        g_c = tot - bc + lic
        g_r = tot - br + lir
        m_new = jnp.maximum(tot + m_prev, jnp.max(g_r, axis=-1, keepdims=True))
        kw = k * jnp.exp(g_c - m_new)
        dec = jnp.exp(tot + m_prev - m_new)
        c_s[u] = dec * c_s[u] + _dot(kw.astype(BF16), v, ((0,), (0,)))
        n_s[u:u + 1, :] = dec * n_s[u:u + 1, :] + jnp.sum(kw, axis=0, keepdims=True)
        m_s[u:u + 1, :] = jnp.broadcast_to(m_new, (1, 128))

    @pl.when(last_ref[item] == 1)
    def _():
        co_ref[0] = c_s[...]
        no_ref[0] = n_s[...]
        mo_ref[0] = m_s[...]


def _scan_items(seq_lens, L):
    rowf, rowb, seq, first, last = [], [], [], [], []
    base = 0
    for s, t in enumerate(seq_lens):
        nc = t // L
        for c in range(nc):
            rowf.append(base + c)
            rowb.append(base + nc - 1 - c)
            seq.append(s)
            first.append(int(c == 0))
            last.append(int(c == nc - 1))
        base += nc
    return [jnp.asarray(np.array(a, np.int32)) for a in (rowf, rowb, seq, first, last)]


def mlstm_scan(p, seq_lens, g, bias, c0, n0, m0, L):
    items = _scan_items(seq_lens, L)
    nseq = len(seq_lens)
    rows = p.shape[0]
    hshape = jax.ShapeDtypeStruct((rows, L, M_HEADS * M_DV), F32)
    fwd = lambda i, rf, rb, sq, fs, ls: (rf[i], 0, 0)
    bwd = lambda i, rf, rb, sq, fs, ls: (rb[i], 0, 0)
    st4 = lambda i, rf, rb, sq, fs, ls: (sq[i], 0, 0, 0)
    st3 = lambda i, rf, rb, sq, fs, ls: (sq[i], 0, 0)
    fix = lambda i, rf, rb, sq, fs, ls: (0, 0)
    gt = jnp.swapaxes(g, 1, 2)
    return pl.pallas_call(
        functools.partial(_mlstm_kernel, L=L),
        grid_spec=pltpu.PrefetchScalarGridSpec(
            num_scalar_prefetch=5,
            grid=(items[0].shape[0],),
            in_specs=[pl.BlockSpec((1, L, 2048), fwd), pl.BlockSpec((1, L, 2048), bwd),
                      pl.BlockSpec((1, L, 16), fwd), pl.BlockSpec((1, L, 16), bwd),
                      pl.BlockSpec((1, 16, L), fwd), pl.BlockSpec((1, 16, L), bwd),
                      pl.BlockSpec((1, 16), fix), pl.BlockSpec((16, 1), fix),
                      pl.BlockSpec((1, 8, M_DK, M_DV), st4),
                      pl.BlockSpec((1, 8, M_DK), st3),
                      pl.BlockSpec((1, 8, M_DK), st3)],
            out_specs=[pl.BlockSpec((1, L, 1024), fwd), pl.BlockSpec((1, L, 1024), bwd),
                       pl.BlockSpec((1, 8, M_DK, M_DV), st4),
                       pl.BlockSpec((1, 8, M_DK), st3),
                       pl.BlockSpec((1, 8, M_DK), st3)],
            scratch_shapes=[pltpu.VMEM((8, M_DK, M_DV), F32), pltpu.VMEM((8, M_DK), F32),
                            pltpu.VMEM((8, M_DK), F32)]),
        out_shape=[hshape, hshape,
                   jax.ShapeDtypeStruct((nseq, 8, M_DK, M_DV), F32),
                   jax.ShapeDtypeStruct((nseq, 8, M_DK), F32),
                   jax.ShapeDtypeStruct((nseq, 8, M_DK), F32)],
        compiler_params=_params(("arbitrary",)),
        name="mlstm_scan",
    )(*items, p, p, g, g, gt, gt, bias.reshape(1, 16), bias.reshape(16, 1), c0, n0, m0)


def _gla_kernel(rowf_ref, rowb_ref, seq_ref, first_ref, last_ref,
                pf_ref, pb_ref, gf_ref, gb_ref, w2_ref, b2_ref, s0_ref, of_ref, ob_ref, so_ref, s_s, *, L):
    item = pl.program_id(0)

    @pl.when(first_ref[item] == 1)
    def _():
        s_s[...] = s0_ref[0]

    dirs = [(pf_ref, gf_ref, of_ref), (pb_ref, gb_ref, ob_ref)]
    masks = [_tri(L, lower=True), _tri(L, lower=False)]
    bcs = []
    for d in range(2):
        gr = dirs[d][1][0][:, d * G_GATE_RANK:(d + 1) * G_GATE_RANK]
        pre = lax.dot_general(gr, w2_ref[d], (((1,), (0,)), ((), ())), precision=lax.Precision.HIGHEST,
                              preferred_element_type=F32) + b2_ref[d]
        la = _log_sigmoid(pre) * (1.0 / G_GATE_NORM)
        bcs.append(_dot_exact_lhs(masks[d].astype(BF16), la))
    units = [(d, h) for d in range(2) for h in range(G_HEADS)]

    def unit_inputs(d, h):
        p_ref = dirs[d][0]
        q = p_ref[0, :, h * G_DK:(h + 1) * G_DK] * (G_DK ** -0.5)
        k = p_ref[0, :, 512 + h * G_DK:512 + (h + 1) * G_DK]
        v = p_ref[0, :, 1024 + h * G_DV:1024 + (h + 1) * G_DV].astype(BF16)
        return q, k, v, bcs[d][:, h * G_DK:(h + 1) * G_DK]

    qds, a_mats = [], []
    for d, h in units:
        q, k, _, bc = unit_inputs(d, h)
        qd = (q * jnp.exp(bc)).astype(BF16)
        kd = (k * jnp.exp(-bc)).astype(BF16)
        qds.append(qd)
        a_mats.append(jnp.where(masks[d], _dot(qd, kd, ((1,), (1,))), 0.0).astype(BF16))
    for u, (d, h) in enumerate(units):
        _, _, v, _ = unit_inputs(d, h)
        dirs[d][2][0, :, h * G_DV:(h + 1) * G_DV] = (_dot(a_mats[u], v)
                                                     + _dot(qds[u], s_s[u].astype(BF16), ((1,), (1,))))
    for u, (d, h) in enumerate(units):
        _, k, v, bc = unit_inputs(d, h)
        last = L - 1 if d == 0 else 0
        bl = bc[last:last + 1, :]
        kl = (k * jnp.exp(bl - bc)).astype(BF16)
        s_s[u] = s_s[u] * jnp.exp(bl) + _dot(v, kl, ((0,), (0,)))

    @pl.when(last_ref[item] == 1)
    def _():
        so_ref[0] = s_s[...]


def gla_scan(p, seq_lens, gr, w2, b2, s0t, L):
    items = _scan_items(seq_lens, L)
    nseq = len(seq_lens)
    oshape = jax.ShapeDtypeStruct((p.shape[0], L, G_HEADS * G_DV), F32)
    fwd = lambda i, rf, rb, sq, fs, ls: (rf[i], 0, 0)
    bwd = lambda i, rf, rb, sq, fs, ls: (rb[i], 0, 0)
    st4 = lambda i, rf, rb, sq, fs, ls: (sq[i], 0, 0, 0)
    fix3 = lambda i, rf, rb, sq, fs, ls: (0, 0, 0)
    return pl.pallas_call(
        functools.partial(_gla_kernel, L=L),
        grid_spec=pltpu.PrefetchScalarGridSpec(
            num_scalar_prefetch=5,
            grid=(items[0].shape[0],),
            in_specs=[pl.BlockSpec((1, L, 2048), fwd), pl.BlockSpec((1, L, 2048), bwd),
                      pl.BlockSpec((1, L, 32), fwd), pl.BlockSpec((1, L, 32), bwd),
                      pl.BlockSpec((2, G_GATE_RANK, 512), fix3),
                      pl.BlockSpec((2, 1, 512), fix3),
                      pl.BlockSpec((1, 8, G_DV, G_DK), st4)],
            out_specs=[pl.BlockSpec((1, L, 1024), fwd), pl.BlockSpec((1, L, 1024), bwd),
                       pl.BlockSpec((1, 8, G_DV, G_DK), st4)],
            scratch_shapes=[pltpu.VMEM((8, G_DV, G_DK), F32)]),
        out_shape=[oshape, oshape, jax.ShapeDtypeStruct((nseq, 8, G_DV, G_DK), F32)],
        compiler_params=_params(("arbitrary",)),
        name="gla_scan",
    )(*items, p, p, gr, gr, w2, b2.reshape(2, 1, 512), s0t)


ATTN_HEADS_PER_STEP = 2


def _attn_kernel(q_ref, k_ref, v_ref, o_ref, *, scale):
    scores = [_dot(q_ref[0, j], k_ref[0, j], ((1,), (1,))) * scale for j in range(ATTN_HEADS_PER_STEP)]
    for j, s in enumerate(scores):
        m = jnp.max(s, axis=-1, keepdims=True)
        p = jnp.exp(s - m)
        l = jnp.sum(p, axis=-1, keepdims=True)
        o_ref[0, j] = _dot(p.astype(BF16), v_ref[0, j]) / l


def attention(q, k, v, tq):
    b, h, lq, dq = q.shape
    lk, dv = k.shape[2], v.shape[3]
    hb = ATTN_HEADS_PER_STEP
    return pl.pallas_call(
        functools.partial(_attn_kernel, scale=dq ** -0.5),
        grid=(b, h // hb, lq // tq),
        in_specs=[pl.BlockSpec((1, hb, tq, dq), lambda b, h, i: (b, h, i, 0)),
                  pl.BlockSpec((1, hb, lk, dq), lambda b, h, i: (b, h, 0, 0)),
                  pl.BlockSpec((1, hb, lk, dv), lambda b, h, i: (b, h, 0, 0))],
        out_specs=pl.BlockSpec((1, hb, tq, dv), lambda b, h, i: (b, h, i, 0)),
        out_shape=jax.ShapeDtypeStruct((b, h, lq, dv), F32),
        compiler_params=_params(("arbitrary", "arbitrary", "arbitrary")),
        name="attention",
    )(q, k, v)


NA_RB = 8


def _na_kernel(q_ref, k_ref, v_ref, kc_ref, vc_ref, bias_ref, o_ref, *, rows):
    j = pl.program_id(2)
    scale = NA_HD ** -0.5
    s_ctx_all = _dot(q_ref[0, 0], kc_ref[0, 0], ((1,), (1,))) * scale
    offs, s_locs = [], []
    for a in range(NA_RB):
        r = j * NA_RB + a
        start = jnp.clip(r - NA_ROWS // 2, 0, rows - NA_ROWS)
        dr0 = start - r + (NA_ROWS - 1)
        offs.append(pl.multiple_of(start * GRID_W, GRID_W))
        qa = q_ref[0, 0, a * GRID_W:(a + 1) * GRID_W, :]
        kl = k_ref[0, 0, pl.ds(offs[a], NA_ROWS * GRID_W), :]
        s_locs.append(_dot(qa, kl, ((1,), (1,))) * scale + bias_ref[0, dr0])
    p_locs, p_ctxs, ls = [], [], []
    for a in range(NA_RB):
        s_ctx = s_ctx_all[a * GRID_W:(a + 1) * GRID_W, :]
        m = jnp.maximum(jnp.max(s_locs[a], axis=-1, keepdims=True), jnp.max(s_ctx, axis=-1, keepdims=True))
        p_loc = jnp.exp(s_locs[a] - m)
        p_ctx = jnp.exp(s_ctx - m)
        ls.append(jnp.sum(p_loc, axis=-1, keepdims=True) + jnp.sum(p_ctx, axis=-1, keepdims=True))
        p_locs.append(p_loc.astype(BF16))
        p_ctxs.append(p_ctx.astype(BF16))
    o_ctx_all = _dot(jnp.concatenate(p_ctxs, axis=0), vc_ref[0, 0])
    for a in range(NA_RB):
        vl = v_ref[0, 0, pl.ds(offs[a], NA_ROWS * GRID_W), :]
        o = _dot(p_locs[a], vl) + o_ctx_all[a * GRID_W:(a + 1) * GRID_W, :]
        o_ref[0, 0, a * GRID_W:(a + 1) * GRID_W, :] = o / ls[a]


def na_bias_table(rpb):
    cq = np.arange(GRID_W)[:, None]
    ck = np.arange(GRID_W)[None, :]
    cs = np.clip(cq - NA_COLS // 2, 0, GRID_W - NA_COLS)
    ok = (ck >= cs) & (ck < cs + NA_COLS)
    dc = np.clip(ck - cq, -(NA_COLS - 1), NA_COLS - 1) + (NA_COLS - 1)
    t = jnp.where(ok[None, None], rpb.astype(F32)[:, :, dc], NEG_INF)
    rows = np.arange(NA_ROWS)[:, None] + np.arange(NA_ROWS)[None, :]
    tf = t[:, rows]
    return jnp.transpose(tf, (0, 1, 3, 2, 4)).reshape(NA_HEADS, NA_ROWS, GRID_W, NA_ROWS * GRID_W)


def na_attention(q, k, v, kc, vc, bias):
    b, h, t, dh = q.shape
    lc = kc.shape[2]
    rows = t // GRID_W
    full = lambda b, h, j: (b, h, 0, 0)
    return pl.pallas_call(
        functools.partial(_na_kernel, rows=rows),
        grid=(b, h, rows // NA_RB),
        in_specs=[pl.BlockSpec((1, 1, NA_RB * GRID_W, dh), lambda b, h, j: (b, h, j, 0)),
                  pl.BlockSpec((1, 1, t, dh), full), pl.BlockSpec((1, 1, t, dh), full),
                  pl.BlockSpec((1, 1, lc, dh), full), pl.BlockSpec((1, 1, lc, dh), full),
                  pl.BlockSpec((1, NA_ROWS, GRID_W, NA_ROWS * GRID_W), lambda b, h, j: (h, 0, 0, 0))],
        out_specs=pl.BlockSpec((1, 1, NA_RB * GRID_W, dh), lambda b, h, j: (b, h, j, 0)),
        out_shape=jax.ShapeDtypeStruct((b, h, t, dh), F32),
        compiler_params=_params(("arbitrary", "arbitrary", "arbitrary")),
        name="na_attention",
    )(q, k, v, kc, vc, bias)


def _rms_rows(x, g):
    return x * lax.rsqrt(jnp.mean(x * x, axis=-1, keepdims=True) + NORM_EPS) * g


def _mla_q_kernel(cq_ref, g_ref, w_ref, cos_ref, sin_ref, o_ref):
    r = _dot(_rms_rows(cq_ref[0], g_ref[...]).astype(BF16), w_ref[...])
    nn = MLA_HEADS * MLA_NOPE
    nr = MLA_HEADS * MLA_ROPE
    o_ref[0, :, :nn] = r[:, :nn]
    o_ref[0, :, nn:] = r[:, nn:nn + nr] * cos_ref[0] + r[:, nn + nr:] * sin_ref[0]


def mla_q(p, q_norm, w_q3, cos_q, sin_q, tm=512):
    nseg, seg, _ = p.shape
    nout = MLA_HEADS * (MLA_NOPE + MLA_ROPE)
    nr = MLA_HEADS * MLA_ROPE
    tok = lambda s, i: (s, i, 0)
    return pl.pallas_call(
        _mla_q_kernel,
        grid=(nseg, seg // tm),
        in_specs=[pl.BlockSpec((1, tm, MLA_Q_LORA), tok),
                  pl.BlockSpec((1, MLA_Q_LORA), lambda s, i: (0, 0)),
                  pl.BlockSpec(w_q3.shape, lambda s, i: (0, 0)),
                  pl.BlockSpec((1, tm, nr), tok), pl.BlockSpec((1, tm, nr), tok)],
        out_specs=pl.BlockSpec((1, tm, nout), tok),
        out_shape=jax.ShapeDtypeStruct((nseg, seg, nout), F32),
        compiler_params=_params(("arbitrary", "arbitrary")),
        name="mla_q",
    )(p, q_norm.reshape(1, -1), w_q3, cos_q, sin_q)


def _mla_kv_kernel(ckv_ref, kpe_ref, g_ref, w_ref, cos_ref, sin_ref, ckvn_ref, kpeo_ref, kv_ref):
    cn = _rms_rows(ckv_ref[0], g_ref[...])
    ckvn_ref[0] = cn
    kv_ref[0] = _dot(cn.astype(BF16), w_ref[...])
    kp = kpe_ref[0]
    kpeo_ref[0] = kp[:, :MLA_ROPE] * cos_ref[0] + kp[:, MLA_ROPE:2 * MLA_ROPE] * sin_ref[0]


def mla_kv(p, kv_norm, w_kv, cos_k, sin_k, tm=512):
    nseg, seg, _ = p.shape
    nkv = w_kv.shape[1]
    tok = lambda s, i: (s, i, 0)
    return pl.pallas_call(
        _mla_kv_kernel,
        grid=(nseg, seg // tm),
        in_specs=[pl.BlockSpec((1, tm, MLA_KV_LORA), lambda s, i: (s, i, MLA_Q_LORA // MLA_KV_LORA)),
                  pl.BlockSpec((1, tm, 128), lambda s, i: (s, i, (MLA_Q_LORA + MLA_KV_LORA) // 128)),
                  pl.BlockSpec((1, MLA_KV_LORA), lambda s, i: (0, 0)),
                  pl.BlockSpec(w_kv.shape, lambda s, i: (0, 0)),
                  pl.BlockSpec((1, tm, MLA_ROPE), tok), pl.BlockSpec((1, tm, MLA_ROPE), tok)],
        out_specs=[pl.BlockSpec((1, tm, MLA_KV_LORA), tok), pl.BlockSpec((1, tm, MLA_ROPE), tok),
                   pl.BlockSpec((1, tm, nkv), tok)],
        out_shape=[jax.ShapeDtypeStruct((nseg, seg, MLA_KV_LORA), F32),
                   jax.ShapeDtypeStruct((nseg, seg, MLA_ROPE), F32),
                   jax.ShapeDtypeStruct((nseg, seg, nkv), F32)],
        compiler_params=_params(("arbitrary", "arbitrary")),
        name="mla_kv",
    )(p, p, kv_norm.reshape(1, -1), w_kv, cos_k, sin_k)


def _mm_kernel(a_ref, w_ref, o_ref):
    o_ref[...] = _dot(a_ref[...].astype(BF16), w_ref[...])


def matmul(a, w_bf16, tm):
    m, k = a.shape
    n = w_bf16.shape[1]
    return pl.pallas_call(
        _mm_kernel,
        grid=(m // tm,),
        in_specs=[pl.BlockSpec((tm, k), lambda i: (i, 0)), pl.BlockSpec((k, n), lambda i: (0, 0))],
        out_specs=pl.BlockSpec((tm, n), lambda i: (i, 0)),
        out_shape=jax.ShapeDtypeStruct((m, n), F32),
        compiler_params=_params(("arbitrary",)),
        name="matmul",
    )(a, w_bf16)


PEER_RT = 128
NOT_TOP = 99.0
RANK_CODE = 2.0 ** 100


def _top16(s, exact):
    vals = []
    if exact:
        key = lax.broadcasted_iota(jnp.int32, s.shape, 0).astype(F32)
        rank = jnp.full(s.shape, NOT_TOP, F32)
        for r in range(PEER_TOPK):
            m = jnp.max(s, axis=0, keepdims=True)
            hit = key == jnp.min(jnp.where(s == m, key, 1e9), axis=0, keepdims=True)
            rank = jnp.where(hit, float(r), rank)
            s = jnp.where(hit, NEG_INF, s)
            vals.append(m)
        return vals, rank
    for r in range(PEER_TOPK):
        m = jnp.max(s, axis=0, keepdims=True)
        s = jnp.where(s == m, -RANK_CODE * (1.0 + r / 32.0), s)
        vals.append(m)
    return vals, jnp.where(s <= -0.5 * RANK_CODE, s * (-32.0 / RANK_CODE) - 32.0, NOT_TOP)


def _pair_topk(av, bv, exact):
    n = av[0].shape[-1]
    a_lo, a_hi = jnp.concatenate(av[:8], 0), jnp.concatenate(av[8:], 0)
    b_lo, b_hi = jnp.concatenate(bv[:8], 0), jnp.concatenate(bv[8:], 0)
    row = lax.broadcasted_iota(jnp.int32, (8, n), 0).astype(F32)

    no_pos = 1e8

    def rows_b(a, b_blk, boff, nvalid):
        ok = row < nvalid
        return jnp.where(ok, av[a] + b_blk, NEG_INF), jnp.where(ok, a * 16.0 + boff + row, no_pos)

    def rows_a(b, a_blk, aoff, lo, hi):
        ok = (row >= lo) & (row < hi)
        return jnp.where(ok, a_blk + bv[b], NEG_INF), jnp.where(ok, (aoff + row) * 16.0 + b, no_pos)

    groups = [rows_b(0, b_lo, 0, 8), rows_b(0, b_hi, 8, 8), rows_b(1, b_lo, 0, 8), rows_b(2, b_lo, 0, 5),
              rows_b(3, b_lo, 0, 4), rows_a(0, a_lo, 0, 4, 8), rows_a(0, a_hi, 8, 0, 8),
              rows_a(1, a_lo, 0, 4, 8), rows_a(2, a_lo, 0, 4, 5)]
    cands = [g[0] for g in groups]
    poss = [g[1] for g in groups]
    sels = [jnp.zeros((8, n), F32) for _ in groups]
    top = av[0] + bv[0]
    z = jnp.zeros((1, n), F32)
    for _ in range(PEER_TOPK):
        m = functools.reduce(jnp.maximum, cands)
        m = jnp.max(m, axis=0, keepdims=True)
        if exact:
            first = functools.reduce(jnp.minimum, [jnp.where(c == m, p, 1e9) for c, p in zip(cands, poss)])
            first = jnp.min(first, axis=0, keepdims=True)
            hits = [p == first for p in poss]
            cands = [jnp.where(hh, NEG_INF, c) for hh, c in zip(hits, cands)]
            sels = [jnp.where(hh, 1.0, s) for hh, s in zip(hits, sels)]
        else:
            cands = [jnp.where(c == m, -RANK_CODE, c) for c in cands]
        z = z + jnp.exp(m - top)
    if not exact:
        sels = [jnp.where(c == -RANK_CODE, 1.0, 0.0) for c in cands]
    cnt = lambda x: jnp.sum(x, axis=0, keepdims=True)
    cut_lo = sels[5] + sels[7] + sels[8]
    for a, c in enumerate([cnt(sels[0]) + cnt(sels[1]), cnt(sels[2]), cnt(sels[3]), cnt(sels[4])]):
        cut_lo = cut_lo + jnp.where(row == a, c, 0.0)
    return cut_lo, sels[6], z, cnt(cut_lo) + cnt(sels[6])


def _peer_route_kernel(x_ref, sh_ref, sc_ref, wq_ref, sk_ref, xm_ref, e1_ref, cut_ref, e2_ref, r2_ref, q_s, *, tm):
    xm = (x_ref[0] * (1.0 + sc_ref[0]) + sh_ref[0]).astype(BF16)
    xm_ref[0] = xm
    q = _dot(xm, wq_ref[...])
    for hp in range(2 * PEER_HEADS):
        q_s[hp] = q[:, hp * PEER_HALF:(hp + 1) * PEER_HALF]

    def route(h, tok, exact):
        def scores(hp):
            return lax.dot_general(sk_ref[hp], q_s[hp, tok, :], (((1,), (1,)), ((), ())),
                                   precision=lax.Precision.HIGHEST, preferred_element_type=F32)

        s1, s2 = scores(2 * h), scores(2 * h + 1)
        av, rank1 = _top16(s1, exact)
        bv, rank2 = _top16(s2, exact)
        cut_lo, cut_hi, z, nsel = _pair_topk(av, bv, exact)
        cut = jnp.zeros_like(s1)
        for r in range(PEER_TOPK):
            src = cut_lo if r < 8 else cut_hi
            cut = jnp.where(rank1 == float(r), src[r % 8:r % 8 + 1, :], cut)
        e1_ref[0, h, :, tok] = (jnp.exp(s1 - av[0]) / z).astype(BF16)
        cut_ref[0, h, :, tok] = cut.astype(BF16)
        e2_ref[0, h, :, tok] = jnp.exp(s2 - bv[0]).astype(BF16)
        r2_ref[0, h, :, tok] = rank2.astype(BF16)
        ranked = lambda rk: jnp.sum(jnp.where(rk < PEER_TOPK, 1.0, 0.0), axis=0, keepdims=True)
        return ranked(rank1), ranked(rank2), nsel

    def body(h, carry):
        toks = [pl.ds(t0, PEER_RT) for t0 in range(0, tm, PEER_RT)]
        counts = [route(h, tok, exact=False) for tok in toks]
        for tok, cnts in zip(toks, counts):
            bad = functools.reduce(jnp.maximum, [jnp.abs(cn - PEER_TOPK) for cn in cnts])

            @pl.when(jnp.max(bad) > 0.0)
            def _():
                route(h, tok, exact=True)
        return carry

    lax.fori_loop(0, PEER_HEADS, body, 0)


def peer_route(x3, mod3, shift_chunk, wq_bf16, subkeys, tm=512):
    nseg, seg, d = x3.shape
    tok = lambda s, i: (s, i, 0)
    rshape = jax.ShapeDtypeStruct((nseg, PEER_HEADS, PEER_NKEYS, seg), BF16)
    rspec = pl.BlockSpec((1, PEER_HEADS, PEER_NKEYS, tm), lambda s, i: (s, 0, 0, i))
    return pl.pallas_call(
        functools.partial(_peer_route_kernel, tm=tm),
        grid=(nseg, seg // tm),
        in_specs=[pl.BlockSpec((1, tm, d), tok),
                  pl.BlockSpec((1, 1, d), lambda s, i: (s, 0, shift_chunk)),
                  pl.BlockSpec((1, 1, d), lambda s, i: (s, 0, shift_chunk + 1)),
                  pl.BlockSpec(wq_bf16.shape, lambda s, i: (0, 0)),
                  pl.BlockSpec((2 * PEER_HEADS, PEER_NKEYS, PEER_HALF), lambda s, i: (0, 0, 0))],
        out_specs=[pl.BlockSpec((1, tm, d), tok), rspec, rspec, rspec, rspec],
        out_shape=[jax.ShapeDtypeStruct((nseg, seg, d), BF16)] + [rshape] * 4,
        scratch_shapes=[pltpu.VMEM((2 * PEER_HEADS, tm, PEER_HALF), F32)],
        compiler_params=_params(("arbitrary", "arbitrary")),
        name="peer_route",
    )(x3, mod3, mod3, wq_bf16, subkeys.reshape(2 * PEER_HEADS, PEER_NKEYS, PEER_HALF))


PEER_CE = 1024


def _gelu_tanh(x):
    return 0.5 * x * (1.0 + jnp.tanh(0.7978845608028654 * (x + 0.044715 * x * x * x)))


def _peer_dense_kernel(xm_ref, u_ref, vt_ref, e1_ref, cut_ref, e2_ref, r2_ref, x_ref, gate_ref, g_ref, b_ref,
                       o_ref, acc_s, at_s, w_s, e2_s, r2_s, *, tm):
    e = pl.program_id(2)
    nb = PEER_CE // PEER_NKEYS
    ntt = tm // PEER_RT

    @pl.when(e == 0)
    def _():
        acc_s[...] = jnp.zeros_like(acc_s)
        e2_s[:, :, :tm] = e2_ref[0]
        r2_s[:, :, :tm] = r2_ref[0]

    packed = (PEER_NKEYS // 16, 16, PEER_RT)
    ng = 2

    def gate_tiles(tt, i0):
        tok = slice(tt * PEER_RT, (tt + 1) * PEER_RT)
        gmats = [jnp.zeros(packed, BF16) for _ in range(ng)]
        for h in range(PEER_HEADS):
            e2 = e2_s[h, :, tok].reshape(packed)
            r2 = r2_s[h, :, tok].reshape(packed)
            for k in range(ng):
                i = i0 + k
                e1 = jnp.broadcast_to(e1_ref[0, h, i:i + 1, tok], (16, PEER_RT))[None]
                cut = jnp.broadcast_to(cut_ref[0, h, i:i + 1, tok], (16, PEER_RT))[None]
                gmats[k] = gmats[k] + e1 * jnp.where(r2 < cut, e2, jnp.zeros_like(e2))
        for k in range(ng):
            rows = slice((i0 + k) * PEER_NKEYS, (i0 + k + 1) * PEER_NKEYS)
            act = _gelu_tanh(at_s[rows, tok]).astype(BF16)
            w_s[rows, tok] = gmats[k].reshape(PEER_NKEYS, PEER_RT) * act

    at_s[:, :tm] = _dot(u_ref[0], xm_ref[0], ((1,), (1,)))
    for tt in range(ntt):
        for i0 in range(0, nb, ng):
            gate_tiles(tt, i0)
    acc_s[:, :tm] += _dot(vt_ref[0, 0], w_s[:, :tm])

    @pl.when(e == pl.num_programs(2) - 1)
    def _():
        z = DEEPNORM_ALPHA * x_ref[0] + gate_ref[0] * acc_s[:, :tm].T
        o_ref[0] = _layer_norm_rows(z, g_ref[...], b_ref[...])


def peer_dense(xm, u_all, vt_all, l, e1, cut, e2, r2, x3, mod3, gate_chunk, ln_g, ln_b, tm=1024):
    nseg, seg, d = x3.shape
    ne = u_all.shape[1]
    nb = PEER_CE // PEER_NKEYS
    tp = tm + PEER_RT
    tok = lambda s, i, e: (s, i, 0)
    chunk = pl.BlockSpec((1, PEER_HEADS, nb, tm), lambda s, i, e: (s, 0, e, i))
    full = pl.BlockSpec((1, PEER_HEADS, PEER_NKEYS, tm), lambda s, i, e: (s, 0, 0, i))
    return pl.pallas_call(
        functools.partial(_peer_dense_kernel, tm=tm),
        grid=(nseg, seg // tm, ne // PEER_CE),
        in_specs=[pl.BlockSpec((1, tm, d), tok),
                  pl.BlockSpec((1, PEER_CE, d), lambda s, i, e: (l, e, 0)),
                  pl.BlockSpec((1, 1, d, PEER_CE), lambda s, i, e: (l, e, 0, 0)),
                  chunk, chunk, full, full,
                  pl.BlockSpec((1, tm, d), tok),
                  pl.BlockSpec((1, 1, d), lambda s, i, e: (s, 0, gate_chunk)),
                  pl.BlockSpec((1, d), lambda s, i, e: (0, 0)),
                  pl.BlockSpec((1, d), lambda s, i, e: (0, 0))],
        out_specs=pl.BlockSpec((1, tm, d), tok),
        out_shape=jax.ShapeDtypeStruct((nseg, seg, d), F32),
        scratch_shapes=[pltpu.VMEM((d, tp), F32), pltpu.VMEM((PEER_CE, tp), F32), pltpu.VMEM((PEER_CE, tp), BF16),
                        pltpu.VMEM((PEER_HEADS, PEER_NKEYS, tp), BF16), pltpu.VMEM((PEER_HEADS, PEER_NKEYS, tp), BF16)],
        compiler_params=_params(("arbitrary", "arbitrary", "arbitrary")),
        name="peer_dense",
    )(xm, u_all, vt_all, e1, cut, e2, r2, x3, mod3, ln_g.reshape(1, d), ln_b.reshape(1, d))


def peer_layer(x3, mod3, l, wq, subkeys, u_all, vt_all, ln_g, ln_b):
    xm, e1, cut, e2, r2 = peer_route(x3, mod3, 3, wq.astype(BF16), subkeys)
    return peer_dense(xm, u_all, vt_all, l, e1, cut, e2, r2, x3, mod3, 5, ln_g, ln_b)


def _pad_cols(w, n):
    return jnp.pad(w, ((0, 0), (0, n - w.shape[1])))


def _stream(prompt_part, sample_part):
    return jnp.concatenate([prompt_part.reshape(1, -1, prompt_part.shape[-1]), sample_part], axis=0)


def _head_major(a, heads):
    b, t, _ = a.shape
    return jnp.transpose(a.reshape(b, t, heads, -1), (0, 2, 1, 3))


def _token_major(a):
    b, h, t, dh = a.shape
    return jnp.transpose(a, (0, 2, 1, 3)).reshape(b, t, h * dh)


MLSTM_CHUNK = 256
GLA_CHUNK = 64
NPROJ = 3200


def mlstm_layer(x3, mod3, bp, lp, st_c, st_n, st_m, w_in, b_gate, norm_w, w_out, ln_g, ln_b):
    nseg, seg, _ = x3.shape
    bs = nseg - 1
    p = mod_matmul(x3, mod3, 0, _pad_cols(w_in, NPROJ).astype(BF16))
    L = min(MLSTM_CHUNK, lp)
    rows = nseg * seg // L
    seq_lens = [lp] * bp + [seg] * bs
    c0 = jnp.concatenate([jnp.zeros((bp, 8, M_DK, M_DV), F32), st_c.reshape(bs, 8, M_DK, M_DV)], 0)
    n0 = jnp.concatenate([jnp.zeros((bp, 8, M_DK), F32), st_n.reshape(bs, 8, M_DK)], 0)
    m0 = jnp.concatenate([jnp.zeros((bp, 8, M_DK), F32),
                          jnp.broadcast_to(st_m.reshape(bs, 8, 1), (bs, 8, M_DK))], 0)
    hf, hb, c_new, n_new, m_new = mlstm_scan(p.reshape(rows, L, NPROJ), seq_lens,
                                             p[:, :, 3072:3088].reshape(rows, L, 16), b_gate, c0, n0, m0, L)
    x3 = outproj_ln("mlstm", (hf.reshape(nseg, seg, -1), hb.reshape(nseg, seg, -1)), x3, mod3, 2,
                    w_out.astype(BF16), ln_g, ln_b, norm_w=norm_w, og=p, og_col=2)
    return (x3, c_new[:bp].reshape(bp, 2, M_HEADS, M_DK, M_DV), n_new[:bp].reshape(bp, 2, M_HEADS, M_DK),
            m_new[:bp, :, 0].reshape(bp, 2, M_HEADS))


def gla_layer(x3, mod3, bp, lp, st_s, w_in, w_gate2, b_gate2, norm_w, w_out, ln_g, ln_b):
    nseg, seg, _ = x3.shape
    bs = nseg - 1
    p = mod_matmul(x3, mod3, 0, _pad_cols(w_in, NPROJ).astype(BF16))
    L = GLA_CHUNK
    rows = nseg * seg // L
    seq_lens = [lp] * bp + [seg] * bs
    s0t = jnp.concatenate([jnp.zeros((bp, 8, G_DV, G_DK), F32),
                           jnp.swapaxes(st_s.reshape(bs, 8, G_DK, G_DV), -1, -2)], 0)
    of, ob, s_new = gla_scan(p.reshape(rows, L, NPROJ), seq_lens, p[:, :, 3072:3104].reshape(rows, L, 32),
                             w_gate2, b_gate2, s0t, L)
    x3 = outproj_ln("gla", (of.reshape(nseg, seg, -1), ob.reshape(nseg, seg, -1)), x3, mod3, 2,
                    w_out.astype(BF16), ln_g, ln_b, norm_w=jnp.tile(norm_w, G_HEADS), og=p, og_col=2)
    return x3, jnp.swapaxes(s_new[:bp], -1, -2).reshape(bp, 2, G_HEADS, G_DK, G_DV)


def na_layer(x3, mod3, bp, lp, cache_k, cache_v, w_in, rpb, w_out, ln_g, ln_b):
    nseg, seg, _ = x3.shape
    bs = nseg - 1
    hd = NA_HEADS * NA_HD
    p = mod_matmul(x3, mod3, 0, w_in.astype(BF16))
    pp = p[0].reshape(bp, lp, 3 * hd)
    hm = lambda a: _head_major(a, NA_HEADS).astype(BF16)
    yp = attention(hm(pp[..., :hd]), hm(pp[..., hd:2 * hd]), hm(pp[..., 2 * hd:]), lp)
    ps = p[1:]
    ys = na_attention(hm(ps[..., :hd]), hm(ps[..., hd:2 * hd]), hm(ps[..., 2 * hd:]),
                      hm(cache_k.reshape(bs, -1, hd)), hm(cache_v.reshape(bs, -1, hd)), na_bias_table(rpb))
    x3 = outproj_ln("plain", _stream(_token_major(yp), _token_major(ys)), x3, mod3, 2, w_out.astype(BF16), ln_g, ln_b)
    return (x3, pp[..., hd:2 * hd].reshape(bp, lp, NA_HEADS, NA_HD), pp[..., 2 * hd:].reshape(bp, lp, NA_HEADS, NA_HD))


def _rope_rotated_cols(w):
    q = MLA_ROPE // 4
    return jnp.concatenate([-w[..., q:2 * q], w[..., :q], -w[..., 3 * q:], w[..., 2 * q:3 * q]], axis=-1)


def _rope_tables(ts):
    ra = MLA_ROPE // 2
    t = np.arange(ts)
    inv = 1.0 / (ROPE_BASE ** (np.arange(0, ra, 2, dtype=np.float32) / ra))
    ang_r = (t // GRID_W).astype(np.float32)[:, None] * inv[None, :]
    ang_c = (t % GRID_W).astype(np.float32)[:, None] * inv[None, :]
    ang = np.concatenate([ang_r, ang_r, ang_c, ang_c], axis=-1).astype(np.float32)
    return jnp.cos(jnp.asarray(ang)), jnp.sin(jnp.asarray(ang))


def mla_layer(x3, mod3, bp, lp, cache_ckv, cache_kpe, w_in, q_norm, w_qup, kv_norm, w_kvup, w_out, ln_g, ln_b):
    nseg, seg, _ = x3.shape
    bs = nseg - 1
    nq = MLA_Q_LORA + MLA_KV_LORA
    w_ext = jnp.concatenate([w_in, _rope_rotated_cols(w_in[:, nq:])], axis=1)
    p = mod_matmul(x3, mod3, 0, _pad_cols(w_ext, 896).astype(BF16))
    cos_t, sin_t = _rope_tables(seg)
    cos3 = jnp.concatenate([jnp.ones((1, seg, MLA_ROPE), F32), jnp.broadcast_to(cos_t, (bs, seg, MLA_ROPE))], 0)
    sin3 = jnp.concatenate([jnp.zeros((1, seg, MLA_ROPE), F32), jnp.broadcast_to(sin_t, (bs, seg, MLA_ROPE))], 0)
    wq = w_qup.reshape(MLA_Q_LORA, MLA_HEADS, MLA_NOPE + MLA_ROPE)
    wq_rope = wq[:, :, MLA_NOPE:]
    w_q3 = jnp.concatenate([wq[:, :, :MLA_NOPE].reshape(MLA_Q_LORA, -1), wq_rope.reshape(MLA_Q_LORA, -1),
                            _rope_rotated_cols(wq_rope).reshape(MLA_Q_LORA, -1)], axis=1).astype(BF16)
    q_all = mla_q(p, q_norm, w_q3, jnp.tile(cos3, (1, 1, MLA_HEADS)), jnp.tile(sin3, (1, 1, MLA_HEADS)))
    wkv = w_kvup.reshape(MLA_KV_LORA, MLA_HEADS, MLA_NOPE + MLA_VD)
    w_kv2 = jnp.concatenate([wkv[:, :, :MLA_NOPE].reshape(MLA_KV_LORA, -1),
                             wkv[:, :, MLA_NOPE:].reshape(MLA_KV_LORA, -1)], axis=1).astype(BF16)
    ckvn, kpe, kv = mla_kv(p, kv_norm, w_kv2, cos3, sin3)
    kvc = matmul(cache_ckv.reshape(-1, MLA_KV_LORA), w_kv2, 512).reshape(bs, -1, w_kv2.shape[1])
    nn = MLA_HEADS * MLA_NOPE

    def heads(q_rows, kv_rows, kpe_rows):
        b, t, _ = q_rows.shape
        tk = kv_rows.shape[1]
        qh = jnp.concatenate([q_rows[..., :nn].reshape(b, t, MLA_HEADS, MLA_NOPE),
                              q_rows[..., nn:].reshape(b, t, MLA_HEADS, MLA_ROPE)], -1)
        kh = jnp.concatenate([kv_rows[..., :nn].reshape(b, tk, MLA_HEADS, MLA_NOPE),
                              jnp.broadcast_to(kpe_rows[:, :, None, :], (b, tk, MLA_HEADS, MLA_ROPE))], -1)
        vh = kv_rows[..., nn:].reshape(b, tk, MLA_HEADS, MLA_VD)
        tr = lambda a: jnp.transpose(a, (0, 2, 1, 3)).astype(BF16)
        return tr(qh), tr(kh), tr(vh)

    yp = attention(*heads(q_all[0].reshape(bp, lp, -1), kv[0].reshape(bp, lp, -1), kpe[0].reshape(bp, lp, -1)), lp)
    ys = attention(*heads(q_all[1:], jnp.concatenate([kv[1:], kvc], 1), jnp.concatenate([kpe[1:], cache_kpe], 1)), 256)
    x3 = outproj_ln("plain", _stream(_token_major(yp), _token_major(ys)), x3, mod3, 2, w_out.astype(BF16), ln_g, ln_b)
    return x3, ckvn[0].reshape(bp, lp, MLA_KV_LORA), kpe[0].reshape(bp, lp, MLA_ROPE)


def kernel(x_prompt, x_sample, c, c_ctx, state_mlstm_C, state_mlstm_n, state_mlstm_m, state_gla_S, cache_na_k, cache_na_v, cache_mla_ckv, cache_mla_kpe, ada_w, ada_b, ln_mix_g, ln_mix_b, ln_ffn_g, ln_ffn_b, mlstm_w_in, mlstm_b_gate, mlstm_norm_w, mlstm_w_out, gla_w_in, gla_w_gate2, gla_b_gate2, gla_norm_w, gla_w_out, na_w_in, na_rpb, na_w_out, mla_w_in, mla_q_norm, mla_w_qup, mla_kv_norm, mla_w_kvup, mla_w_out, peer_w_q, peer_subkeys, peer_u, peer_v):
    bp, lp, d = x_prompt.shape
    bs, ts, _ = x_sample.shape
    assert bp * lp == ts and bs + 1 <= 8
    x3 = _stream(x_prompt, x_sample)
    cond8 = jnp.zeros((8, d), F32).at[0].set(c_ctx).at[1:1 + bs].set(c)
    mods = adaln_all(cond8, ada_w, ada_b)
    u_all = peer_u.astype(BF16)
    vt_all = jnp.swapaxes(peer_v.reshape(DEPTH, -1, PEER_CE, d), 2, 3).astype(BF16)
    outs = {}
    for l in range(DEPTH):
        mod3 = mods[l].reshape(8, 1, ADA_CHUNKS * d)
        kind = l % 4
        if kind == 0:
            x3, outs["C"], outs["n"], outs["m"] = mlstm_layer(
                x3, mod3, bp, lp, state_mlstm_C, state_mlstm_n, state_mlstm_m, mlstm_w_in, mlstm_b_gate,
                mlstm_norm_w, mlstm_w_out, ln_mix_g[l], ln_mix_b[l])
        elif kind == 1:
            x3, outs["S"] = gla_layer(x3, mod3, bp, lp, state_gla_S, gla_w_in, gla_w_gate2, gla_b_gate2,
                                      gla_norm_w, gla_w_out, ln_mix_g[l], ln_mix_b[l])
        elif kind == 2:
            x3, outs["nk"], outs["nv"] = na_layer(x3, mod3, bp, lp, cache_na_k, cache_na_v, na_w_in, na_rpb,
                                                  na_w_out, ln_mix_g[l], ln_mix_b[l])
        else:
            x3, outs["ckv"], outs["kpe"] = mla_layer(x3, mod3, bp, lp, cache_mla_ckv, cache_mla_kpe, mla_w_in,
                                                     mla_q_norm, mla_w_qup, mla_kv_norm, mla_w_kvup, mla_w_out,
                                                     ln_mix_g[l], ln_mix_b[l])
        x3 = peer_layer(x3, mod3, l, peer_w_q[l], peer_subkeys[l], u_all, vt_all, ln_ffn_g[l], ln_ffn_b[l])
    return (x3[0].reshape(bp, lp, d), x3[1:], outs["C"], outs["n"], outs["m"], outs["S"], outs["nk"], outs["nv"],
            outs["ckv"], outs["kpe"])
```

```python
import functools

import numpy as np
import jax
import jax.numpy as jnp
from jax import lax
from jax.experimental import pallas as pl
from jax.experimental.pallas import tpu as pltpu

D_MODEL = 1024
DEPTH = 4
GRID_W = 64
DEEPNORM_ALPHA = (2.0 * DEPTH) ** 0.25
ADA_CHUNKS = 6
NORM_EPS = 1e-5
SEG = 4096
NSEG = 3

M_HEADS, M_DK, M_DV = 4, 128, 256
G_HEADS, G_DK, G_DV = 4, 128, 256
G_GATE_RANK = 16
G_GATE_NORM = 16.0
NA_HEADS, NA_HD, NA_ROWS, NA_COLS = 16, 64, 8, 16
MLA_HEADS, MLA_Q_LORA, MLA_KV_LORA, MLA_NOPE, MLA_ROPE, MLA_VD = 16, 512, 256, 64, 32, 64
ROPE_BASE = 10000.0
PEER_HEADS, PEER_NKEYS, PEER_HALF, PEER_TOPK = 8, 128, 128, 16

V7X_VMEM_LIMIT = 56 * 1024 * 1024
F32 = jnp.float32
BF16 = jnp.bfloat16
NEG_INF = float("-inf")


def _params(sem, vmem=V7X_VMEM_LIMIT):
    return pltpu.CompilerParams(dimension_semantics=sem, vmem_limit_bytes=vmem)


def _dot(a, b, dims=((1,), (0,))):
    return lax.dot_general(a, b, (dims, ((), ())), preferred_element_type=F32)


def _split3(a):
    hi = a.astype(BF16)
    r1 = a - hi.astype(F32)
    mid = r1.astype(BF16)
    lo = (r1 - mid.astype(F32)).astype(BF16)
    return hi, mid, lo


def _dot_exact_lhs(m01, a):
    hi, mid, lo = _split3(a)
    return _dot(m01, hi) + _dot(m01, mid) + _dot(m01, lo)


def _dot_exact_rhs(a, m01):
    hi, mid, lo = _split3(a)
    return _dot(hi, m01) + _dot(mid, m01) + _dot(lo, m01)


def _log_sigmoid(x):
    return jnp.minimum(x, 0.0) - jnp.log(1.0 + jnp.exp(-jnp.abs(x)))


def _sigmoid(x):
    return 1.0 / (1.0 + jnp.exp(-x))


def _adaln_kernel(c_ref, w_ref, b_ref, o_ref):
    cv = c_ref[...]
    a = cv * _sigmoid(cv)
    o_ref[0] = lax.dot_general(a, w_ref[0], (((1,), (0,)), ((), ())), precision=lax.Precision.HIGHEST,
                               preferred_element_type=F32) + b_ref[0]


def adaln_all(cond8, ada_w, ada_b):
    tn = 1024
    n = ada_w.shape[-1]
    return pl.pallas_call(
        _adaln_kernel,
        grid=(DEPTH, n // tn),
        in_specs=[pl.BlockSpec((8, D_MODEL), lambda l, j: (0, 0)),
                  pl.BlockSpec((1, D_MODEL, tn), lambda l, j: (l, 0, j)),
                  pl.BlockSpec((1, 1, tn), lambda l, j: (l, 0, j))],
        out_specs=pl.BlockSpec((1, 8, tn), lambda l, j: (l, 0, j)),
        out_shape=jax.ShapeDtypeStruct((DEPTH, 8, n), F32),
        compiler_params=_params(("arbitrary", "arbitrary")),
        name="adaln",
    )(cond8, ada_w, ada_b.reshape(DEPTH, 1, n))


def _modmm_kernel(x_ref, sh_ref, sc_ref, w_ref, o_ref, xm_ref):
    @pl.when(pl.program_id(2) == 0)
    def _():
        xm_ref[...] = (x_ref[0] * (1.0 + sc_ref[0]) + sh_ref[0]).astype(BF16)

    o_ref[0] = _dot(xm_ref[...], w_ref[...]).astype(o_ref.dtype)


def mod_matmul(x3, mod3, shift_chunk, w_bf16, tm=512, tn=None, out_dtype=F32):
    nseg, seg, d = x3.shape
    n = w_bf16.shape[1]
    tn = n if tn is None else tn
    return pl.pallas_call(
        _modmm_kernel,
        grid=(nseg, seg // tm, n // tn),
        in_specs=[pl.BlockSpec((1, tm, d), lambda s, i, j: (s, i, 0)),
                  pl.BlockSpec((1, 1, d), lambda s, i, j: (s, 0, shift_chunk)),
                  pl.BlockSpec((1, 1, d), lambda s, i, j: (s, 0, shift_chunk + 1)),
                  pl.BlockSpec((d, tn), lambda s, i, j: (0, j))],
        out_specs=pl.BlockSpec((1, tm, tn), lambda s, i, j: (s, i, j)),
        out_shape=jax.ShapeDtypeStruct((nseg, seg, n), out_dtype),
        scratch_shapes=[pltpu.VMEM((tm, d), BF16)],
        compiler_params=_params(("arbitrary", "arbitrary", "arbitrary")),
        name="mod_matmul",
    )(x3, mod3, mod3, w_bf16)


def _layer_norm_rows(y, g, b):
    mu = jnp.mean(y, axis=-1, keepdims=True)
    yc = y - mu
    var = jnp.mean(yc * yc, axis=-1, keepdims=True)
    return yc * lax.rsqrt(var + NORM_EPS) * g + b


def _outproj_kernel(*refs, mode):
    if mode == "plain":
        y_ref, x_ref, gate_ref, w_ref, g_ref, b_ref, o_ref = refs
        yin = y_ref[0].astype(BF16)
    else:
        ya_ref, yb_ref, og_ref, nw_ref, x_ref, gate_ref, w_ref, g_ref, b_ref, o_ref = refs
        hs = ya_ref[0] + yb_ref[0]
        og = og_ref[0]
        parts = []
        for h in range(4):
            seg = hs[:, h * 256:(h + 1) * 256]
            nw = nw_ref[:, h * 256:(h + 1) * 256]
            if mode == "mlstm":
                mu = jnp.mean(seg, axis=-1, keepdims=True)
                sc = seg - mu
                var = jnp.mean(sc * sc, axis=-1, keepdims=True)
                parts.append(sc * lax.rsqrt(var + NORM_EPS) * nw)
            else:
                ms = jnp.mean(seg * seg, axis=-1, keepdims=True)
                parts.append(seg * lax.rsqrt(ms + NORM_EPS) * nw)
        hn = jnp.concatenate(parts, axis=-1)
        act = _sigmoid(og) if mode == "mlstm" else og * _sigmoid(og)
        yin = (act * hn).astype(BF16)
    y = _dot(yin, w_ref[...])
    z = DEEPNORM_ALPHA * x_ref[0] + gate_ref[0] * y
    o_ref[0] = _layer_norm_rows(z, g_ref[...], b_ref[...])


def outproj_ln(mode, ys, x3, mod3, gate_chunk, w_bf16, ln_g, ln_b, norm_w=None, og=None, og_col=0, tm=512):
    nseg, seg, d = x3.shape
    k = w_bf16.shape[0]
    tok = lambda s, i: (s, i, 0)
    if mode == "plain":
        args = [ys]
        specs = [pl.BlockSpec((1, tm, k), tok)]
    else:
        args = [ys[0], ys[1], og, norm_w.reshape(1, k)]
        specs = [pl.BlockSpec((1, tm, k), tok), pl.BlockSpec((1, tm, k), tok),
                 pl.BlockSpec((1, tm, k), lambda s, i: (s, i, og_col)),
                 pl.BlockSpec((1, k), lambda s, i: (0, 0))]
    args += [x3, mod3, w_bf16, ln_g.reshape(1, d), ln_b.reshape(1, d)]
    specs += [pl.BlockSpec((1, tm, d), tok),
              pl.BlockSpec((1, 1, d), lambda s, i: (s, 0, gate_chunk)),
              pl.BlockSpec((k, d), lambda s, i: (0, 0)),
              pl.BlockSpec((1, d), lambda s, i: (0, 0)),
              pl.BlockSpec((1, d), lambda s, i: (0, 0))]
    return pl.pallas_call(
        functools.partial(_outproj_kernel, mode=mode),
        grid=(nseg, seg // tm),
        in_specs=specs,
        out_specs=pl.BlockSpec((1, tm, d), tok),
        out_shape=jax.ShapeDtypeStruct((nseg, seg, d), F32),
        compiler_params=_params(("arbitrary", "arbitrary")),
        name="outproj_ln_" + mode,
    )(*args)


def _tri(n, lower):
    r = lax.broadcasted_iota(jnp.int32, (n, n), 0)
    c = lax.broadcasted_iota(jnp.int32, (n, n), 1)
    return (c <= r) if lower else (c >= r)


def _mlstm_kernel(rowf_ref, rowb_ref, seq_ref, first_ref, last_ref,
                  pf_ref, pb_ref, gf_ref, gb_ref, gtf_ref, gtb_ref, bias_ref, biast_ref,
                  c0_ref, n0_ref, m0_ref, hf_ref, hb_ref, co_ref, no_ref, mo_ref,
                  c_s, n_s, m_s, *, L):
    item = pl.program_id(0)

    @pl.when(first_ref[item] == 1)
    def _():
        c_s[...] = c0_ref[0]
        n_s[...] = n0_ref[0]
        m_s[...] = m0_ref[0]

    dirs = [(pf_ref, gf_ref, gtf_ref, hf_ref), (pb_ref, gb_ref, gtb_ref, hb_ref)]
    masks = [_tri(L, lower=True), _tri(L, lower=False)]
    gates = []
    for d in range(2):
        g = dirs[d][1][0] + bias_ref[...]
        gt = dirs[d][2][0] + biast_ref[...]
        lf_c = _log_sigmoid(g[:, d * 8 + 4:d * 8 + 8])
        lf_r = _log_sigmoid(gt[d * 8 + 4:d * 8 + 8, :])
        b_c = _dot_exact_lhs(masks[d].astype(BF16), lf_c)
        b_r = _dot_exact_rhs(lf_r, masks[1 - d].astype(BF16))
        gates.append((g[:, d * 8:d * 8 + 4], gt[d * 8:d * 8 + 4, :], b_c, b_r))
    units = [(d, h) for d in range(2) for h in range(M_HEADS)]

    def unit_inputs(d, h):
        p_ref = dirs[d][0]
        q = p_ref[0, :, h * M_DK:(h + 1) * M_DK]
        k = p_ref[0, :, 512 + h * M_DK:512 + (h + 1) * M_DK] * (M_DK ** -0.5)
        v = p_ref[0, :, 1024 + h * M_DV:1024 + (h + 1) * M_DV].astype(BF16)
        li_c, li_r, b_c, b_r = gates[d]
        return q, k, v, li_c[:, h:h + 1], li_r[h:h + 1, :], b_c[:, h:h + 1], b_r[h:h + 1, :]

    qks = []
    for d, h in units:
        q, k, _, _, _, _, _ = unit_inputs(d, h)
        qks.append(_dot(q.astype(BF16), k.astype(BF16), ((1,), (1,))))
    smats, eis, mts = [], [], []
    for u, (d, h) in enumerate(units):
        _, _, _, _, lir, bc, br = unit_inputs(d, h)
        m_prev = m_s[u:u + 1, 0:1]
        dmat = jnp.where(masks[d], bc - br + lir, NEG_INF)
        inter = bc + m_prev
        mt = jnp.maximum(inter, jnp.max(dmat, axis=-1, keepdims=True))
        smats.append(qks[u] * jnp.exp(dmat - mt))
        eis.append(jnp.exp(inter - mt))
        mts.append(mt)
    nums = []
    for u, (d, h) in enumerate(units):
        q, _, v, _, _, _, _ = unit_inputs(d, h)
        nums.append(_dot(smats[u].astype(BF16), v) + eis[u] * _dot(q.astype(BF16), c_s[u].astype(BF16)))
    for u, (d, h) in enumerate(units):
        q, _, _, _, _, _, _ = unit_inputs(d, h)
        den = (jnp.sum(smats[u], axis=-1, keepdims=True)
               + eis[u] * jnp.sum(q * n_s[u:u + 1, :], axis=-1, keepdims=True))
        dirs[d][3][0, :, h * M_DV:(h + 1) * M_DV] = nums[u] / jnp.maximum(jnp.abs(den), jnp.exp(-mts[u]))
    for u, (d, h) in enumerate(units):
        _, k, v, lic, lir, bc, br = unit_inputs(d, h)
        last = L - 1 if d == 0 else 0
        m_prev = m_s[u:u + 1, 0:1]
        tot = br[:, last:last + 1]
        g_c = tot - bc + lic
        g_r = tot - br + lir
        m_new = jnp.maximum(tot + m_prev, jnp.max(g_r, axis=-1, keepdims=True))
        kw = k * jnp.exp(g_c - m_new)
        dec = jnp.exp(tot + m_prev - m_new)
        c_s[u] = dec * c_s[u] + _dot(kw.astype(BF16), v, ((0,), (0,)))
        n_s[u:u + 1, :] = dec * n_s[u:u + 1, :] + jnp.sum(kw, axis=0, keepdims=True)
        m_s[u:u + 1, :] = jnp.broadcast_to(m_new, (1, 128))

    @pl.when(last_ref[item] == 1)
    def _():
        co_ref[0] = c_s[...]
        no_ref[0] = n_s[...]
        mo_ref[0] = m_s[...]


def _scan_items(seq_lens, L):
    rowf, rowb, seq, first, last = [], [], [], [], []
    base = 0
    for s, t in enumerate(seq_lens):
        nc = t // L
        for c in range(nc):
            rowf.append(base + c)
            rowb.append(base + nc - 1 - c)
            seq.append(s)
            first.append(int(c == 0))
            last.append(int(c == nc - 1))
        base += nc
    return [jnp.asarray(np.array(a, np.int32)) for a in (rowf, rowb, seq, first, last)]


def mlstm_scan(p, seq_lens, g, bias, c0, n0, m0, L):
    items = _scan_items(seq_lens, L)
    nseq = len(seq_lens)
    rows = p.shape[0]
    hshape = jax.ShapeDtypeStruct((rows, L, M_HEADS * M_DV), F32)
    fwd = lambda i, rf, rb, sq, fs, ls: (rf[i], 0, 0)
    bwd = lambda i, rf, rb, sq, fs, ls: (rb[i], 0, 0)
    st4 = lambda i, rf, rb, sq, fs, ls: (sq[i], 0, 0, 0)
    st3 = lambda i, rf, rb, sq, fs, ls: (sq[i], 0, 0)
    fix = lambda i, rf, rb, sq, fs, ls: (0, 0)
    gt = jnp.swapaxes(g, 1, 2)
    return pl.pallas_call(
        functools.partial(_mlstm_kernel, L=L),
        grid_spec=pltpu.PrefetchScalarGridSpec(
            num_scalar_prefetch=5,
            grid=(items[0].shape[0],),
            in_specs=[pl.BlockSpec((1, L, 2048), fwd), pl.BlockSpec((1, L, 2048), bwd),
                      pl.BlockSpec((1, L, 16), fwd), pl.BlockSpec((1, L, 16), bwd),
                      pl.BlockSpec((1, 16, L), fwd), pl.BlockSpec((1, 16, L), bwd),
                      pl.BlockSpec((1, 16), fix), pl.BlockSpec((16, 1), fix),
                      pl.BlockSpec((1, 8, M_DK, M_DV), st4),
                      pl.BlockSpec((1, 8, M_DK), st3),
                      pl.BlockSpec((1, 8, M_DK), st3)],
            out_specs=[pl.BlockSpec((1, L, 1024), fwd), pl.BlockSpec((1, L, 1024), bwd),
                       pl.BlockSpec((1, 8, M_DK, M_DV), st4),
                       pl.BlockSpec((1, 8, M_DK), st3),
                       pl.BlockSpec((1, 8, M_DK), st3)],
            scratch_shapes=[pltpu.VMEM((8, M_DK, M_DV), F32), pltpu.VMEM((8, M_DK), F32),
                            pltpu.VMEM((8, M_DK), F32)]),
        out_shape=[hshape, hshape,
                   jax.ShapeDtypeStruct((nseq, 8, M_DK, M_DV), F32),
                   jax.ShapeDtypeStruct((nseq, 8, M_DK), F32),
                   jax.ShapeDtypeStruct((nseq, 8, M_DK), F32)],
        compiler_params=_params(("arbitrary",)),
        name="mlstm_scan",
    )(*items, p, p, g, g, gt, gt, bias.reshape(1, 16), bias.reshape(16, 1), c0, n0, m0)


def _gla_kernel(rowf_ref, rowb_ref, seq_ref, first_ref, last_ref,
                pf_ref, pb_ref, gf_ref, gb_ref, w2_ref, b2_ref, s0_ref, of_ref, ob_ref, so_ref, s_s, *, L):
    item = pl.program_id(0)

    @pl.when(first_ref[item] == 1)
    def _():
        s_s[...] = s0_ref[0]

    dirs = [(pf_ref, gf_ref, of_ref), (pb_ref, gb_ref, ob_ref)]
    masks = [_tri(L, lower=True), _tri(L, lower=False)]
    bcs = []
    for d in range(2):
        gr = dirs[d][1][0][:, d * G_GATE_RANK:(d + 1) * G_GATE_RANK]
        pre = lax.dot_general(gr, w2_ref[d], (((1,), (0,)), ((), ())), precision=lax.Precision.HIGHEST,
                              preferred_element_type=F32) + b2_ref[d]
        la = _log_sigmoid(pre) * (1.0 / G_GATE_NORM)
        bcs.append(_dot_exact_lhs(masks[d].astype(BF16), la))
    units = [(d, h) for d in range(2) for h in range(G_HEADS)]

    def unit_inputs(d, h):
        p_ref = dirs[d][0]
        q = p_ref[0, :, h * G_DK:(h + 1) * G_DK] * (G_DK ** -0.5)
        k = p_ref[0, :, 512 + h * G_DK:512 + (h + 1) * G_DK]
        v = p_ref[0, :, 1024 + h * G_DV:1024 + (h + 1) * G_DV].astype(BF16)
        return q, k, v, bcs[d][:, h * G_DK:(h + 1) * G_DK]

    qds, a_mats = [], []
    for d, h in units:
        q, k, _, bc = unit_inputs(d, h)
        qd = (q * jnp.exp(bc)).astype(BF16)
        kd = (k * jnp.exp(-bc)).astype(BF16)
        qds.append(qd)
        a_mats.append(jnp.where(masks[d], _dot(qd, kd, ((1,), (1,))), 0.0).astype(BF16))
    for u, (d, h) in enumerate(units):
        _, _, v, _ = unit_inputs(d, h)
        dirs[d][2][0, :, h * G_DV:(h + 1) * G_DV] = (_dot(a_mats[u], v)
                                                     + _dot(qds[u], s_s[u].astype(BF16), ((1,), (1,))))
    for u, (d, h) in enumerate(units):
        _, k, v, bc = unit_inputs(d, h)
        last = L - 1 if d == 0 else 0
        bl = bc[last:last + 1, :]
        kl = (k * jnp.exp(bl - bc)).astype(BF16)
        s_s[u] = s_s[u] * jnp.exp(bl) + _dot(v, kl, ((0,), (0,)))

    @pl.when(last_ref[item] == 1)
    def _():
        so_ref[0] = s_s[...]


def gla_scan(p, seq_lens, gr, w2, b2, s0t, L):
    items = _scan_items(seq_lens, L)
    nseq = len(seq_lens)
    oshape = jax.ShapeDtypeStruct((p.shape[0], L, G_HEADS * G_DV), F32)
    fwd = lambda i, rf, rb, sq, fs, ls: (rf[i], 0, 0)
    bwd = lambda i, rf, rb, sq, fs, ls: (rb[i], 0, 0)
    st4 = lambda i, rf, rb, sq, fs, ls: (sq[i], 0, 0, 0)
    fix3 = lambda i, rf, rb, sq, fs, ls: (0, 0, 0)
    return pl.pallas_call(
        functools.partial(_gla_kernel, L=L),
        grid_spec=pltpu.PrefetchScalarGridSpec(
            num_scalar_prefetch=5,
            grid=(items[0].shape[0],),
            in_specs=[pl.BlockSpec((1, L, 2048), fwd), pl.BlockSpec((1, L, 2048), bwd),
                      pl.BlockSpec((1, L, 32), fwd), pl.BlockSpec((1, L, 32), bwd),
                      pl.BlockSpec((2, G_GATE_RANK, 512), fix3),
                      pl.BlockSpec((2, 1, 512), fix3),
                      pl.BlockSpec((1, 8, G_DV, G_DK), st4)],
            out_specs=[pl.BlockSpec((1, L, 1024), fwd), pl.BlockSpec((1, L, 1024), bwd),
                       pl.BlockSpec((1, 8, G_DV, G_DK), st4)],
            scratch_shapes=[pltpu.VMEM((8, G_DV, G_DK), F32)]),
        out_shape=[oshape, oshape, jax.ShapeDtypeStruct((nseq, 8, G_DV, G_DK), F32)],
        compiler_params=_params(("arbitrary",)),
        name="gla_scan",
    )(*items, p, p, gr, gr, w2, b2.reshape(2, 1, 512), s0t)


ATTN_HEADS_PER_STEP = 2


def _attn_kernel(q_ref, k_ref, v_ref, o_ref, *, scale):
    scores = [_dot(q_ref[0, j], k_ref[0, j], ((1,), (1,))) * scale for j in range(ATTN_HEADS_PER_STEP)]
    for j, s in enumerate(scores):
        m = jnp.max(s, axis=-1, keepdims=True)
        p = jnp.exp(s - m)
        l = jnp.sum(p, axis=-1, keepdims=True)
        o_ref[0, j] = _dot(p.astype(BF16), v_ref[0, j]) / l


def attention(q, k, v, tq):
    b, h, lq, dq = q.shape
    lk, dv = k.shape[2], v.shape[3]
    hb = ATTN_HEADS_PER_STEP
    return pl.pallas_call(
        functools.partial(_attn_kernel, scale=dq ** -0.5),
        grid=(b, h // hb, lq // tq),
        in_specs=[pl.BlockSpec((1, hb, tq, dq), lambda b, h, i: (b, h, i, 0)),
                  pl.BlockSpec((1, hb, lk, dq), lambda b, h, i: (b, h, 0, 0)),
                  pl.BlockSpec((1, hb, lk, dv), lambda b, h, i: (b, h, 0, 0))],
        out_specs=pl.BlockSpec((1, hb, tq, dv), lambda b, h, i: (b, h, i, 0)),
        out_shape=jax.ShapeDtypeStruct((b, h, lq, dv), F32),
        compiler_params=_params(("arbitrary", "arbitrary", "arbitrary")),
        name="attention",
    )(q, k, v)


NA_RB = 8


def _na_kernel(q_ref, k_ref, v_ref, kc_ref, vc_ref, bias_ref, o_ref, *, rows):
    j = pl.program_id(2)
    scale = NA_HD ** -0.5
    s_ctx_all = _dot(q_ref[0, 0], kc_ref[0, 0], ((1,), (1,))) * scale
    offs, s_locs = [], []
    for a in range(NA_RB):
        r = j * NA_RB + a
        start = jnp.clip(r - NA_ROWS // 2, 0, rows - NA_ROWS)
        dr0 = start - r + (NA_ROWS - 1)
        offs.append(pl.multiple_of(start * GRID_W, GRID_W))
        qa = q_ref[0, 0, a * GRID_W:(a + 1) * GRID_W, :]
        kl = k_ref[0, 0, pl.ds(offs[a], NA_ROWS * GRID_W), :]
        s_locs.append(_dot(qa, kl, ((1,), (1,))) * scale + bias_ref[0, dr0])
    p_locs, p_ctxs, ls = [], [], []
    for a in range(NA_RB):
        s_ctx = s_ctx_all[a * GRID_W:(a + 1) * GRID_W, :]
        m = jnp.maximum(jnp.max(s_locs[a], axis=-1, keepdims=True), jnp.max(s_ctx, axis=-1, keepdims=True))
        p_loc = jnp.exp(s_locs[a] - m)
        p_ctx = jnp.exp(s_ctx - m)
        ls.append(jnp.sum(p_loc, axis=-1, keepdims=True) + jnp.sum(p_ctx, axis=-1, keepdims=True))
        p_locs.append(p_loc.astype(BF16))
        p_ctxs.append(p_ctx.astype(BF16))
    o_ctx_all = _dot(jnp.concatenate(p_ctxs, axis=0), vc_ref[0, 0])
    for a in range(NA_RB):
        vl = v_ref[0, 0, pl.ds(offs[a], NA_ROWS * GRID_W), :]
        o = _dot(p_locs[a], vl) + o_ctx_all[a * GRID_W:(a + 1) * GRID_W, :]
        o_ref[0, 0, a * GRID_W:(a + 1) * GRID_W, :] = o / ls[a]


def na_bias_table(rpb):
    cq = np.arange(GRID_W)[:, None]
    ck = np.arange(GRID_W)[None, :]
    cs = np.clip(cq - NA_COLS // 2, 0, GRID_W - NA_COLS)
    ok = (ck >= cs) & (ck < cs + NA_COLS)
    dc = np.clip(ck - cq, -(NA_COLS - 1), NA_COLS - 1) + (NA_COLS - 1)
    t = jnp.where(ok[None, None], rpb.astype(F32)[:, :, dc], NEG_INF)
    rows = np.arange(NA_ROWS)[:, None] + np.arange(NA_ROWS)[None, :]
    tf = t[:, rows]
    return jnp.transpose(tf, (0, 1, 3, 2, 4)).reshape(NA_HEADS, NA_ROWS, GRID_W, NA_ROWS * GRID_W)


def na_attention(q, k, v, kc, vc, bias):
    b, h, t, dh = q.shape
    lc = kc.shape[2]
    rows = t // GRID_W
    full = lambda b, h, j: (b, h, 0, 0)
    return pl.pallas_call(
        functools.partial(_na_kernel, rows=rows),
        grid=(b, h, rows // NA_RB),
        in_specs=[pl.BlockSpec((1, 1, NA_RB * GRID_W, dh), lambda b, h, j: (b, h, j, 0)),
                  pl.BlockSpec((1, 1, t, dh), full), pl.BlockSpec((1, 1, t, dh), full),
                  pl.BlockSpec((1, 1, lc, dh), full), pl.BlockSpec((1, 1, lc, dh), full),
                  pl.BlockSpec((1, NA_ROWS, GRID_W, NA_ROWS * GRID_W), lambda b, h, j: (h, 0, 0, 0))],
        out_specs=pl.BlockSpec((1, 1, NA_RB * GRID_W, dh), lambda b, h, j: (b, h, j, 0)),
        out_shape=jax.ShapeDtypeStruct((b, h, t, dh), F32),
        compiler_params=_params(("arbitrary", "arbitrary", "arbitrary")),
        name="na_attention",
    )(q, k, v, kc, vc, bias)


def _rms_rows(x, g):
    return x * lax.rsqrt(jnp.mean(x * x, axis=-1, keepdims=True) + NORM_EPS) * g


def _mla_q_kernel(cq_ref, g_ref, w_ref, cos_ref, sin_ref, o_ref):
    r = _dot(_rms_rows(cq_ref[0], g_ref[...]).astype(BF16), w_ref[...])
    nn = MLA_HEADS * MLA_NOPE
    nr = MLA_HEADS * MLA_ROPE
    o_ref[0, :, :nn] = r[:, :nn]
    o_ref[0, :, nn:] = r[:, nn:nn + nr] * cos_ref[0] + r[:, nn + nr:] * sin_ref[0]


def mla_q(p, q_norm, w_q3, cos_q, sin_q, tm=512):
    nseg, seg, _ = p.shape
    nout = MLA_HEADS * (MLA_NOPE + MLA_ROPE)
    nr = MLA_HEADS * MLA_ROPE
    tok = lambda s, i: (s, i, 0)
    return pl.pallas_call(
        _mla_q_kernel,
        grid=(nseg, seg // tm),
        in_specs=[pl.BlockSpec((1, tm, MLA_Q_LORA), tok),
                  pl.BlockSpec((1, MLA_Q_LORA), lambda s, i: (0, 0)),
                  pl.BlockSpec(w_q3.shape, lambda s, i: (0, 0)),
                  pl.BlockSpec((1, tm, nr), tok), pl.BlockSpec((1, tm, nr), tok)],
        out_specs=pl.BlockSpec((1, tm, nout), tok),
        out_shape=jax.ShapeDtypeStruct((nseg, seg, nout), F32),
        compiler_params=_params(("arbitrary", "arbitrary")),
        name="mla_q",
    )(p, q_norm.reshape(1, -1), w_q3, cos_q, sin_q)


def _mla_kv_kernel(ckv_ref, kpe_ref, g_ref, w_ref, cos_ref, sin_ref, ckvn_ref, kpeo_ref, kv_ref):
    cn = _rms_rows(ckv_ref[0], g_ref[...])
    ckvn_ref[0] = cn
    kv_ref[0] = _dot(cn.astype(BF16), w_ref[...])
    kp = kpe_ref[0]
    kpeo_ref[0] = kp[:, :MLA_ROPE] * cos_ref[0] + kp[:, MLA_ROPE:2 * MLA_ROPE] * sin_ref[0]


def mla_kv(p, kv_norm, w_kv, cos_k, sin_k, tm=512):
    nseg, seg, _ = p.shape
    nkv = w_kv.shape[1]
    tok = lambda s, i: (s, i, 0)
    return pl.pallas_call(
        _mla_kv_kernel,
        grid=(nseg, seg // tm),
        in_specs=[pl.BlockSpec((1, tm, MLA_KV_LORA), lambda s, i: (s, i, MLA_Q_LORA // MLA_KV_LORA)),
                  pl.BlockSpec((1, tm, 128), lambda s, i: (s, i, (MLA_Q_LORA + MLA_KV_LORA) // 128)),
                  pl.BlockSpec((1, MLA_KV_LORA), lambda s, i: (0, 0)),
                  pl.BlockSpec(w_kv.shape, lambda s, i: (0, 0)),
                  pl.BlockSpec((1, tm, MLA_ROPE), tok), pl.BlockSpec((1, tm, MLA_ROPE), tok)],
        out_specs=[pl.BlockSpec((1, tm, MLA_KV_LORA), tok), pl.BlockSpec((1, tm, MLA_ROPE), tok),
                   pl.BlockSpec((1, tm, nkv), tok)],
        out_shape=[jax.ShapeDtypeStruct((nseg, seg, MLA_KV_LORA), F32),
                   jax.ShapeDtypeStruct((nseg, seg, MLA_ROPE), F32),
                   jax.ShapeDtypeStruct((nseg, seg, nkv), F32)],
        compiler_params=_params(("arbitrary", "arbitrary")),
        name="mla_kv",
    )(p, p, kv_norm.reshape(1, -1), w_kv, cos_k, sin_k)


def _mm_kernel(a_ref, w_ref, o_ref):
    o_ref[...] = _dot(a_ref[...].astype(BF16), w_ref[...])


def matmul(a, w_bf16, tm):
    m, k = a.shape
    n = w_bf16.shape[1]
    return pl.pallas_call(
        _mm_kernel,
        grid=(m // tm,),
        in_specs=[pl.BlockSpec((tm, k), lambda i: (i, 0)), pl.BlockSpec((k, n), lambda i: (0, 0))],
        out_specs=pl.BlockSpec((tm, n), lambda i: (i, 0)),
        out_shape=jax.ShapeDtypeStruct((m, n), F32),
        compiler_params=_params(("arbitrary",)),
        name="matmul",
    )(a, w_bf16)


PEER_RT = 128
NOT_TOP = 99.0
RANK_CODE = 2.0 ** 100


def _top16(s, exact):
    vals = []
    if exact:
        key = lax.broadcasted_iota(jnp.int32, s.shape, 0).astype(F32)
        rank = jnp.full(s.shape, NOT_TOP, F32)
        for r in range(PEER_TOPK):
            m = jnp.max(s, axis=0, keepdims=True)
            hit = key == jnp.min(jnp.where(s == m, key, 1e9), axis=0, keepdims=True)
            rank = jnp.where(hit, float(r), rank)
            s = jnp.where(hit, NEG_INF, s)
            vals.append(m)
        return vals, rank
    for r in range(PEER_TOPK):
        m = jnp.max(s, axis=0, keepdims=True)
        s = jnp.where(s == m, -RANK_CODE * (1.0 + r / 32.0), s)
        vals.append(m)
    return vals, jnp.where(s <= -0.5 * RANK_CODE, s * (-32.0 / RANK_CODE) - 32.0, NOT_TOP)


def _pair_topk(av, bv, exact):
    n = av[0].shape[-1]
    a_lo, a_hi = jnp.concatenate(av[:8], 0), jnp.concatenate(av[8:], 0)
    b_lo, b_hi = jnp.concatenate(bv[:8], 0), jnp.concatenate(bv[8:], 0)
    row = lax.broadcasted_iota(jnp.int32, (8, n), 0).astype(F32)

    no_pos = 1e8

    def rows_b(a, b_blk, boff, nvalid):
        ok = row < nvalid
        return jnp.where(ok, av[a] + b_blk, NEG_INF), jnp.where(ok, a * 16.0 + boff + row, no_pos)

    def rows_a(b, a_blk, aoff, lo, hi):
        ok = (row >= lo) & (row < hi)
        return jnp.where(ok, a_blk + bv[b], NEG_INF), jnp.where(ok, (aoff + row) * 16.0 + b, no_pos)

    groups = [rows_b(0, b_lo, 0, 8), rows_b(0, b_hi, 8, 8), rows_b(1, b_lo, 0, 8), rows_b(2, b_lo, 0, 5),
              rows_b(3, b_lo, 0, 4), rows_a(0, a_lo, 0, 4, 8), rows_a(0, a_hi, 8, 0, 8),
              rows_a(1, a_lo, 0, 4, 8), rows_a(2, a_lo, 0, 4, 5)]
    cands = [g[0] for g in groups]
    poss = [g[1] for g in groups]
    sels = [jnp.zeros((8, n), F32) for _ in groups]
    top = av[0] + bv[0]
    z = jnp.zeros((1, n), F32)
    for _ in range(PEER_TOPK):
        m = functools.reduce(jnp.maximum, cands)
        m = jnp.max(m, axis=0, keepdims=True)
        if exact:
            first = functools.reduce(jnp.minimum, [jnp.where(c == m, p, 1e9) for c, p in zip(cands, poss)])
            first = jnp.min(first, axis=0, keepdims=True)
            hits = [p == first for p in poss]
            cands = [jnp.where(hh, NEG_INF, c) for hh, c in zip(hits, cands)]
            sels = [jnp.where(hh, 1.0, s) for hh, s in zip(hits, sels)]
        else:
            cands = [jnp.where(c == m, -RANK_CODE, c) for c in cands]
        z = z + jnp.exp(m - top)
    if not exact:
        sels = [jnp.where(c == -RANK_CODE, 1.0, 0.0) for c in cands]
    cnt = lambda x: jnp.sum(x, axis=0, keepdims=True)
    cut_lo = sels[5] + sels[7] + sels[8]
    for a, c in enumerate([cnt(sels[0]) + cnt(sels[1]), cnt(sels[2]), cnt(sels[3]), cnt(sels[4])]):
        cut_lo = cut_lo + jnp.where(row == a, c, 0.0)
    return cut_lo, sels[6], z, cnt(cut_lo) + cnt(sels[6])


def _peer_route_kernel(x_ref, sh_ref, sc_ref, wq_ref, sk_ref, xm_ref, e1_ref, cut_ref, e2_ref, r2_ref, q_s, *, tm):
    xm = (x_ref[0] * (1.0 + sc_ref[0]) + sh_ref[0]).astype(BF16)
    xm_ref[0] = xm
    q = _dot(xm, wq_ref[...])
    for hp in range(2 * PEER_HEADS):
        q_s[hp] = q[:, hp * PEER_HALF:(hp + 1) * PEER_HALF]

    def route(h, tok, exact):
        def scores(hp):
            return lax.dot_general(sk_ref[hp], q_s[hp, tok, :], (((1,), (1,)), ((), ())),
                                   precision=lax.Precision.HIGHEST, preferred_element_type=F32)

        s1, s2 = scores(2 * h), scores(2 * h + 1)
        av, rank1 = _top16(s1, exact)
        bv, rank2 = _top16(s2, exact)
        cut_lo, cut_hi, z, nsel = _pair_topk(av, bv, exact)
        cut = jnp.zeros_like(s1)
        for r in range(PEER_TOPK):
            src = cut_lo if r < 8 else cut_hi
            cut = jnp.where(rank1 == float(r), src[r % 8:r % 8 + 1, :], cut)
        e1_ref[0, h, :, tok] = (jnp.exp(s1 - av[0]) / z).astype(BF16)
        cut_ref[0, h, :, tok] = cut.astype(BF16)
        e2_ref[0, h, :, tok] = jnp.exp(s2 - bv[0]).astype(BF16)
        r2_ref[0, h, :, tok] = rank2.astype(BF16)
        ranked = lambda rk: jnp.sum(jnp.where(rk < PEER_TOPK, 1.0, 0.0), axis=0, keepdims=True)
        return ranked(rank1), ranked(rank2), nsel

    def body(h, carry):
        toks = [pl.ds(t0, PEER_RT) for t0 in range(0, tm, PEER_RT)]
        counts = [route(h, tok, exact=False) for tok in toks]
        for tok, cnts in zip(toks, counts):
            bad = functools.reduce(jnp.maximum, [jnp.abs(cn - PEER_TOPK) for cn in cnts])

            @pl.when(jnp.max(bad) > 0.0)
            def _():
                route(h, tok, exact=True)
        return carry

    lax.fori_loop(0, PEER_HEADS, body, 0)


def peer_route(x3, mod3, shift_chunk, wq_bf16, subkeys, tm=512):
    nseg, seg, d = x3.shape
    tok = lambda s, i: (s, i, 0)
    rshape = jax.ShapeDtypeStruct((nseg, PEER_HEADS, PEER_NKEYS, seg), BF16)
    rspec = pl.BlockSpec((1, PEER_HEADS, PEER_NKEYS, tm), lambda s, i: (s, 0, 0, i))
    return pl.pallas_call(
        functools.partial(_peer_route_kernel, tm=tm),
        grid=(nseg, seg // tm),
        in_specs=[pl.BlockSpec((1, tm, d), tok),
                  pl.BlockSpec((1, 1, d), lambda s, i: (s, 0, shift_chunk)),
                  pl.BlockSpec((1, 1, d), lambda s, i: (s, 0, shift_chunk + 1)),
                  pl.BlockSpec(wq_bf16.shape, lambda s, i: (0, 0)),
                  pl.BlockSpec((2 * PEER_HEADS, PEER_NKEYS, PEER_HALF), lambda s, i: (0, 0, 0))],
        out_specs=[pl.BlockSpec((1, tm, d), tok), rspec, rspec, rspec, rspec],
        out_shape=[jax.ShapeDtypeStruct((nseg, seg, d), BF16)] + [rshape] * 4,
        scratch_shapes=[pltpu.VMEM((2 * PEER_HEADS, tm, PEER_HALF), F32)],
        compiler_params=_params(("arbitrary", "arbitrary")),
        name="peer_route",
    )(x3, mod3, mod3, wq_bf16, subkeys.reshape(2 * PEER_HEADS, PEER_NKEYS, PEER_HALF))


PEER_CE = 1024


def _gelu_tanh(x):
    return 0.5 * x * (1.0 + jnp.tanh(0.7978845608028654 * (x + 0.044715 * x * x * x)))


def _peer_dense_kernel(xm_ref, u_ref, vt_ref, e1_ref, cut_ref, e2_ref, r2_ref, x_ref, gate_ref, g_ref, b_ref,
                       o_ref, acc_s, at_s, w_a, w_b, e2_s, r2_s, *, tm):
    e = pl.program_id(2)
    nb = PEER_CE // PEER_NKEYS
    ntt = tm // PEER_RT

    @pl.when((pl.program_id(0) == 0) & (pl.program_id(1) == 0) & (e == 0))
    def _():
        w_a[...] = jnp.zeros_like(w_a)
        w_b[...] = jnp.zeros_like(w_b)

    @pl.when(e == 0)
    def _():
        acc_s[...] = jnp.zeros_like(acc_s)
        e2_s[:, :, :tm] = e2_ref[0]
        r2_s[:, :, :tm] = r2_ref[0]

    packed = (PEER_NKEYS // 16, 16, PEER_RT)
    ng = 2

    def gate_tiles(tt, i0, w_s):
        tok = slice(tt * PEER_RT, (tt + 1) * PEER_RT)
        gmats = [jnp.zeros(packed, BF16) for _ in range(ng)]
        for h in range(PEER_HEADS):
            e2 = e2_s[h, :, tok].reshape(packed)
            r2 = r2_s[h, :, tok].reshape(packed)
            for k in range(ng):
                i = i0 + k
                e1 = jnp.broadcast_to(e1_ref[0, h, i:i + 1, tok], (16, PEER_RT))[None]
                cut = jnp.broadcast_to(cut_ref[0, h, i:i + 1, tok], (16, PEER_RT))[None]
                gmats[k] = gmats[k] + e1 * jnp.where(r2 < cut, e2, jnp.zeros_like(e2))
        for k in range(ng):
            rows = slice((i0 + k) * PEER_NKEYS, (i0 + k + 1) * PEER_NKEYS)
            act = _gelu_tanh(at_s[rows, tok]).astype(BF16)
            w_s[rows, tok] = gmats[k].reshape(PEER_NKEYS, PEER_RT) * act

    def step(w_cur, w_prev):
        at_s[:, :tm] = _dot(u_ref[0], xm_ref[0], ((1,), (1,)))
        acc_s[:, :tm] += _dot(vt_ref[0, 0], w_prev[:, :tm]) * (e > 0).astype(F32)
        for tt in range(ntt):
            for i0 in range(0, nb, ng):
                gate_tiles(tt, i0, w_cur)

    @pl.when(e % 2 == 0)
    def _():
        step(w_a, w_b)

    @pl.when(e % 2 == 1)
    def _():
        step(w_b, w_a)

    @pl.when(e == pl.num_programs(2) - 1)
    def _():
        z = DEEPNORM_ALPHA * x_ref[0] + gate_ref[0] * acc_s[:, :tm].T
        o_ref[0] = _layer_norm_rows(z, g_ref[...], b_ref[...])


def peer_dense(xm, u_all, vt_all, l, e1, cut, e2, r2, x3, mod3, gate_chunk, ln_g, ln_b, tm=1024):
    nseg, seg, d = x3.shape
    ne = u_all.shape[1]
    nb = PEER_CE // PEER_NKEYS
    tp = tm + PEER_RT
    nch = ne // PEER_CE
    tok = lambda s, i, e: (s, i, 0)
    cur = lambda e: jnp.minimum(e, nch - 1)
    chunk = pl.BlockSpec((1, PEER_HEADS, nb, tm), lambda s, i, e: (s, 0, cur(e), i))
    full = pl.BlockSpec((1, PEER_HEADS, PEER_NKEYS, tm), lambda s, i, e: (s, 0, 0, i), pipeline_mode=pl.Buffered(1))
    return pl.pallas_call(
        functools.partial(_peer_dense_kernel, tm=tm),
        grid=(nseg, seg // tm, nch + 1),
        in_specs=[pl.BlockSpec((1, tm, d), tok),
                  pl.BlockSpec((1, PEER_CE, d), lambda s, i, e: (l, cur(e), 0)),
                  pl.BlockSpec((1, 1, d, PEER_CE), lambda s, i, e: (l, jnp.maximum(e - 1, 0), 0, 0)),
                  chunk, chunk, full, full,
                  pl.BlockSpec((1, tm, d), tok),
                  pl.BlockSpec((1, 1, d), lambda s, i, e: (s, 0, gate_chunk)),
                  pl.BlockSpec((1, d), lambda s, i, e: (0, 0)),
                  pl.BlockSpec((1, d), lambda s, i, e: (0, 0))],
        out_specs=pl.BlockSpec((1, tm, d), tok),
        out_shape=jax.ShapeDtypeStruct((nseg, seg, d), F32),
        scratch_shapes=[pltpu.VMEM((d, tp), F32), pltpu.VMEM((PEER_CE, tp), F32),
                        pltpu.VMEM((PEER_CE, tp), BF16), pltpu.VMEM((PEER_CE, tp), BF16),
                        pltpu.VMEM((PEER_HEADS, PEER_NKEYS, tp), BF16), pltpu.VMEM((PEER_HEADS, PEER_NKEYS, tp), BF16)],
        compiler_params=_params(("arbitrary", "arbitrary", "arbitrary")),
        name="peer_dense",
    )(xm, u_all, vt_all, e1, cut, e2, r2, x3, mod3, ln_g.reshape(1, d), ln_b.reshape(1, d))


def peer_layer(x3, mod3, l, wq, subkeys, u_all, vt_all, ln_g, ln_b):
    xm, e1, cut, e2, r2 = peer_route(x3, mod3, 3, wq.astype(BF16), subkeys)
    return peer_dense(xm, u_all, vt_all, l, e1, cut, e2, r2, x3, mod3, 5, ln_g, ln_b)


def _pad_cols(w, n):
    return jnp.pad(w, ((0, 0), (0, n - w.shape[1])))


def _stream(prompt_part, sample_part):
    return jnp.concatenate([prompt_part.reshape(1, -1, prompt_part.shape[-1]), sample_part], axis=0)


def _head_major(a, heads):
    b, t, _ = a.shape
    return jnp.transpose(a.reshape(b, t, heads, -1), (0, 2, 1, 3))


def _token_major(a):
    b, h, t, dh = a.shape
    return jnp.transpose(a, (0, 2, 1, 3)).reshape(b, t, h * dh)


MLSTM_CHUNK = 256
GLA_CHUNK = 64
NPROJ = 3200


def mlstm_layer(x3, mod3, bp, lp, st_c, st_n, st_m, w_in, b_gate, norm_w, w_out, ln_g, ln_b):
    nseg, seg, _ = x3.shape
    bs = nseg - 1
    p = mod_matmul(x3, mod3, 0, _pad_cols(w_in, NPROJ).astype(BF16))
    L = min(MLSTM_CHUNK, lp)
    rows = nseg * seg // L
    seq_lens = [lp] * bp + [seg] * bs
    c0 = jnp.concatenate([jnp.zeros((bp, 8, M_DK, M_DV), F32), st_c.reshape(bs, 8, M_DK, M_DV)], 0)
    n0 = jnp.concatenate([jnp.zeros((bp, 8, M_DK), F32), st_n.reshape(bs, 8, M_DK)], 0)
    m0 = jnp.concatenate([jnp.zeros((bp, 8, M_DK), F32),
                          jnp.broadcast_to(st_m.reshape(bs, 8, 1), (bs, 8, M_DK))], 0)
    hf, hb, c_new, n_new, m_new = mlstm_scan(p.reshape(rows, L, NPROJ), seq_lens,
                                             p[:, :, 3072:3088].reshape(rows, L, 16), b_gate, c0, n0, m0, L)
    x3 = outproj_ln("mlstm", (hf.reshape(nseg, seg, -1), hb.reshape(nseg, seg, -1)), x3, mod3, 2,
                    w_out.astype(BF16), ln_g, ln_b, norm_w=norm_w, og=p, og_col=2)
    return (x3, c_new[:bp].reshape(bp, 2, M_HEADS, M_DK, M_DV), n_new[:bp].reshape(bp, 2, M_HEADS, M_DK),
            m_new[:bp, :, 0].reshape(bp, 2, M_HEADS))


def gla_layer(x3, mod3, bp, lp, st_s, w_in, w_gate2, b_gate2, norm_w, w_out, ln_g, ln_b):
    nseg, seg, _ = x3.shape
    bs = nseg - 1
    p = mod_matmul(x3, mod3, 0, _pad_cols(w_in, NPROJ).astype(BF16))
    L = GLA_CHUNK
    rows = nseg * seg // L
    seq_lens = [lp] * bp + [seg] * bs
    s0t = jnp.concatenate([jnp.zeros((bp, 8, G_DV, G_DK), F32),
                           jnp.swapaxes(st_s.reshape(bs, 8, G_DK, G_DV), -1, -2)], 0)
    of, ob, s_new = gla_scan(p.reshape(rows, L, NPROJ), seq_lens, p[:, :, 3072:3104].reshape(rows, L, 32),
                             w_gate2, b_gate2, s0t, L)
    x3 = outproj_ln("gla", (of.reshape(nseg, seg, -1), ob.reshape(nseg, seg, -1)), x3, mod3, 2,
                    w_out.astype(BF16), ln_g, ln_b, norm_w=jnp.tile(norm_w, G_HEADS), og=p, og_col=2)
    return x3, jnp.swapaxes(s_new[:bp], -1, -2).reshape(bp, 2, G_HEADS, G_DK, G_DV)


def na_layer(x3, mod3, bp, lp, cache_k, cache_v, w_in, rpb, w_out, ln_g, ln_b):
    nseg, seg, _ = x3.shape
    bs = nseg - 1
    hd = NA_HEADS * NA_HD
    p = mod_matmul(x3, mod3, 0, w_in.astype(BF16))
    pp = p[0].reshape(bp, lp, 3 * hd)
    hm = lambda a: _head_major(a, NA_HEADS).astype(BF16)
    yp = attention(hm(pp[..., :hd]), hm(pp[..., hd:2 * hd]), hm(pp[..., 2 * hd:]), lp)
    ps = p[1:]
    ys = na_attention(hm(ps[..., :hd]), hm(ps[..., hd:2 * hd]), hm(ps[..., 2 * hd:]),
                      hm(cache_k.reshape(bs, -1, hd)), hm(cache_v.reshape(bs, -1, hd)), na_bias_table(rpb))
    x3 = outproj_ln("plain", _stream(_token_major(yp), _token_major(ys)), x3, mod3, 2, w_out.astype(BF16), ln_g, ln_b)
    return (x3, pp[..., hd:2 * hd].reshape(bp, lp, NA_HEADS, NA_HD), pp[..., 2 * hd:].reshape(bp, lp, NA_HEADS, NA_HD))


def _rope_rotated_cols(w):
    q = MLA_ROPE // 4
    return jnp.concatenate([-w[..., q:2 * q], w[..., :q], -w[..., 3 * q:], w[..., 2 * q:3 * q]], axis=-1)


def _rope_tables(ts):
    ra = MLA_ROPE // 2
    t = np.arange(ts)
    inv = 1.0 / (ROPE_BASE ** (np.arange(0, ra, 2, dtype=np.float32) / ra))
    ang_r = (t // GRID_W).astype(np.float32)[:, None] * inv[None, :]
    ang_c = (t % GRID_W).astype(np.float32)[:, None] * inv[None, :]
    ang = np.concatenate([ang_r, ang_r, ang_c, ang_c], axis=-1).astype(np.float32)
    return jnp.cos(jnp.asarray(ang)), jnp.sin(jnp.asarray(ang))


def mla_layer(x3, mod3, bp, lp, cache_ckv, cache_kpe, w_in, q_norm, w_qup, kv_norm, w_kvup, w_out, ln_g, ln_b):
    nseg, seg, _ = x3.shape
    bs = nseg - 1
    nq = MLA_Q_LORA + MLA_KV_LORA
    w_ext = jnp.concatenate([w_in, _rope_rotated_cols(w_in[:, nq:])], axis=1)
    p = mod_matmul(x3, mod3, 0, _pad_cols(w_ext, 896).astype(BF16))
    cos_t, sin_t = _rope_tables(seg)
    cos3 = jnp.concatenate([jnp.ones((1, seg, MLA_ROPE), F32), jnp.broadcast_to(cos_t, (bs, seg, MLA_ROPE))], 0)
    sin3 = jnp.concatenate([jnp.zeros((1, seg, MLA_ROPE), F32), jnp.broadcast_to(sin_t, (bs, seg, MLA_ROPE))], 0)
    wq = w_qup.reshape(MLA_Q_LORA, MLA_HEADS, MLA_NOPE + MLA_ROPE)
    wq_rope = wq[:, :, MLA_NOPE:]
    w_q3 = jnp.concatenate([wq[:, :, :MLA_NOPE].reshape(MLA_Q_LORA, -1), wq_rope.reshape(MLA_Q_LORA, -1),
                            _rope_rotated_cols(wq_rope).reshape(MLA_Q_LORA, -1)], axis=1).astype(BF16)
    q_all = mla_q(p, q_norm, w_q3, jnp.tile(cos3, (1, 1, MLA_HEADS)), jnp.tile(sin3, (1, 1, MLA_HEADS)))
    wkv = w_kvup.reshape(MLA_KV_LORA, MLA_HEADS, MLA_NOPE + MLA_VD)
    w_kv2 = jnp.concatenate([wkv[:, :, :MLA_NOPE].reshape(MLA_KV_LORA, -1),
                             wkv[:, :, MLA_NOPE:].reshape(MLA_KV_LORA, -1)], axis=1).astype(BF16)
    ckvn, kpe, kv = mla_kv(p, kv_norm, w_kv2, cos3, sin3)
    kvc = matmul(cache_ckv.reshape(-1, MLA_KV_LORA), w_kv2, 512).reshape(bs, -1, w_kv2.shape[1])
    nn = MLA_HEADS * MLA_NOPE

    def heads(q_rows, kv_rows, kpe_rows):
        b, t, _ = q_rows.shape
        tk = kv_rows.shape[1]
        qh = jnp.concatenate([q_rows[..., :nn].reshape(b, t, MLA_HEADS, MLA_NOPE),
                              q_rows[..., nn:].reshape(b, t, MLA_HEADS, MLA_ROPE)], -1)
        kh = jnp.concatenate([kv_rows[..., :nn].reshape(b, tk, MLA_HEADS, MLA_NOPE),
                              jnp.broadcast_to(kpe_rows[:, :, None, :], (b, tk, MLA_HEADS, MLA_ROPE))], -1)
        vh = kv_rows[..., nn:].reshape(b, tk, MLA_HEADS, MLA_VD)
        tr = lambda a: jnp.transpose(a, (0, 2, 1, 3)).astype(BF16)
        return tr(qh), tr(kh), tr(vh)

    yp = attention(*heads(q_all[0].reshape(bp, lp, -1), kv[0].reshape(bp, lp, -1), kpe[0].reshape(bp, lp, -1)), lp)
    ys = attention(*heads(q_all[1:], jnp.concatenate([kv[1:], kvc], 1), jnp.concatenate([kpe[1:], cache_kpe], 1)), 256)
    x3 = outproj_ln("plain", _stream(_token_major(yp), _token_major(ys)), x3, mod3, 2, w_out.astype(BF16), ln_g, ln_b)
    return x3, ckvn[0].reshape(bp, lp, MLA_KV_LORA), kpe[0].reshape(bp, lp, MLA_ROPE)


def kernel(x_prompt, x_sample, c, c_ctx, state_mlstm_C, state_mlstm_n, state_mlstm_m, state_gla_S, cache_na_k, cache_na_v, cache_mla_ckv, cache_mla_kpe, ada_w, ada_b, ln_mix_g, ln_mix_b, ln_ffn_g, ln_ffn_b, mlstm_w_in, mlstm_b_gate, mlstm_norm_w, mlstm_w_out, gla_w_in, gla_w_gate2, gla_b_gate2, gla_norm_w, gla_w_out, na_w_in, na_rpb, na_w_out, mla_w_in, mla_q_norm, mla_w_qup, mla_kv_norm, mla_w_kvup, mla_w_out, peer_w_q, peer_subkeys, peer_u, peer_v):
    bp, lp, d = x_prompt.shape
    bs, ts, _ = x_sample.shape
    assert bp * lp == ts and bs + 1 <= 8
    x3 = _stream(x_prompt, x_sample)
    cond8 = jnp.zeros((8, d), F32).at[0].set(c_ctx).at[1:1 + bs].set(c)
    mods = adaln_all(cond8, ada_w, ada_b)
    u_all = peer_u.astype(BF16)
    vt_all = jnp.swapaxes(peer_v.reshape(DEPTH, -1, PEER_CE, d), 2, 3).astype(BF16)
    outs = {}
    for l in range(DEPTH):
        mod3 = mods[l].reshape(8, 1, ADA_CHUNKS * d)
        kind = l % 4
        if kind == 0:
            x3, outs["C"], outs["n"], outs["m"] = mlstm_layer(
                x3, mod3, bp, lp, state_mlstm_C, state_mlstm_n, state_mlstm_m, mlstm_w_in, mlstm_b_gate,
                mlstm_norm_w, mlstm_w_out, ln_mix_g[l], ln_mix_b[l])
        elif kind == 1:
            x3, outs["S"] = gla_layer(x3, mod3, bp, lp, state_gla_S, gla_w_in, gla_w_gate2, gla_b_gate2,
                                      gla_norm_w, gla_w_out, ln_mix_g[l], ln_mix_b[l])
        elif kind == 2:
            x3, outs["nk"], outs["nv"] = na_layer(x3, mod3, bp, lp, cache_na_k, cache_na_v, na_w_in, na_rpb,
                                                  na_w_out, ln_mix_g[l], ln_mix_b[l])
        else:
            x3, outs["ckv"], outs["kpe"] = mla_layer(x3, mod3, bp, lp, cache_mla_ckv, cache_mla_kpe, mla_w_in,
                                                     mla_q_norm, mla_w_qup, mla_kv_norm, mla_w_kvup, mla_w_out,
                                                     ln_mix_g[l], ln_mix_b[l])
        x3 = peer_layer(x3, mod3, l, peer_w_q[l], peer_subkeys[l], u_all, vt_all, ln_ffn_g[l], ln_ffn_b[l])
    return (x3[0].reshape(bp, lp, d), x3[1:], outs["C"], outs["n"], outs["m"], outs["S"], outs["nk"], outs["nv"],
            outs["ckv"], outs["kpe"])
```

```python
import functools

import numpy as np
import jax
import jax.numpy as jnp
from jax import lax
from jax.experimental import pallas as pl
from jax.experimental.pallas import tpu as pltpu

D_MODEL = 1024
DEPTH = 4
GRID_W = 64
DEEPNORM_ALPHA = (2.0 * DEPTH) ** 0.25
ADA_CHUNKS = 6
NORM_EPS = 1e-5

M_HEADS, M_DK, M_DV = 4, 128, 256
G_HEADS, G_DK, G_DV = 4, 128, 256
K_OFF = M_HEADS * M_DK
V_OFF = 2 * M_HEADS * M_DK
QKV_COLS = V_OFF + M_HEADS * M_DV
HID_COLS = M_HEADS * M_DV
GATE_OFF = QKV_COLS + HID_COLS
G_GATE_RANK = 16
G_GATE_NORM = 16.0
NA_HEADS, NA_HD, NA_ROWS, NA_COLS = 16, 64, 8, 16
MLA_HEADS, MLA_Q_LORA, MLA_KV_LORA, MLA_NOPE, MLA_ROPE, MLA_VD = 16, 512, 256, 64, 32, 64
ROPE_BASE = 10000.0
PEER_HEADS, PEER_NKEYS, PEER_HALF, PEER_TOPK = 8, 128, 128, 16

V7X_VMEM_LIMIT = 56 * 1024 * 1024
F32 = jnp.float32
BF16 = jnp.bfloat16
NEG_INF = float("-inf")


def _params(sem, vmem=V7X_VMEM_LIMIT):
    return pltpu.CompilerParams(dimension_semantics=sem, vmem_limit_bytes=vmem)


def _dot(a, b, dims=((1,), (0,))):
    return lax.dot_general(a, b, (dims, ((), ())), preferred_element_type=F32)


def _split3(a):
    hi = a.astype(BF16)
    r1 = a - hi.astype(F32)
    mid = r1.astype(BF16)
    lo = (r1 - mid.astype(F32)).astype(BF16)
    return hi, mid, lo


def _dot_exact_lhs(m01, a):
    hi, mid, lo = _split3(a)
    return _dot(m01, hi) + _dot(m01, mid) + _dot(m01, lo)


def _dot_exact_rhs(a, m01):
    hi, mid, lo = _split3(a)
    return _dot(hi, m01) + _dot(mid, m01) + _dot(lo, m01)


def _log_sigmoid(x):
    return jnp.minimum(x, 0.0) - jnp.log(1.0 + jnp.exp(-jnp.abs(x)))


def _sigmoid(x):
    return 1.0 / (1.0 + jnp.exp(-x))


def _adaln_kernel(c_ref, w_ref, b_ref, o_ref):
    cv = c_ref[...]
    a = cv * _sigmoid(cv)
    o_ref[0] = lax.dot_general(a, w_ref[0], (((1,), (0,)), ((), ())), precision=lax.Precision.HIGHEST,
                               preferred_element_type=F32) + b_ref[0]


def adaln_all(cond8, ada_w, ada_b):
    tn = 1024
    n = ada_w.shape[-1]
    return pl.pallas_call(
        _adaln_kernel,
        grid=(DEPTH, n // tn),
        in_specs=[pl.BlockSpec((8, D_MODEL), lambda l, j: (0, 0)),
                  pl.BlockSpec((1, D_MODEL, tn), lambda l, j: (l, 0, j)),
                  pl.BlockSpec((1, 1, tn), lambda l, j: (l, 0, j))],
        out_specs=pl.BlockSpec((1, 8, tn), lambda l, j: (l, 0, j)),
        out_shape=jax.ShapeDtypeStruct((DEPTH, 8, n), F32),
        compiler_params=_params(("arbitrary", "arbitrary")),
        name="adaln",
    )(cond8, ada_w, ada_b.reshape(DEPTH, 1, n))


def _modmm_kernel(x_ref, sh_ref, sc_ref, w_ref, o_ref, xm_ref):
    @pl.when(pl.program_id(2) == 0)
    def _():
        xm_ref[...] = (x_ref[0] * (1.0 + sc_ref[0]) + sh_ref[0]).astype(BF16)

    o_ref[0] = _dot(xm_ref[...], w_ref[...]).astype(o_ref.dtype)


def mod_matmul(x3, mod3, shift_chunk, w_bf16, tm=512, tn=None, out_dtype=F32):
    nseg, seg, d = x3.shape
    n = w_bf16.shape[1]
    tn = n if tn is None else tn
    return pl.pallas_call(
        _modmm_kernel,
        grid=(nseg, seg // tm, n // tn),
        in_specs=[pl.BlockSpec((1, tm, d), lambda s, i, j: (s, i, 0)),
                  pl.BlockSpec((1, 1, d), lambda s, i, j: (s, 0, shift_chunk)),
                  pl.BlockSpec((1, 1, d), lambda s, i, j: (s, 0, shift_chunk + 1)),
                  pl.BlockSpec((d, tn), lambda s, i, j: (0, j))],
        out_specs=pl.BlockSpec((1, tm, tn), lambda s, i, j: (s, i, j)),
        out_shape=jax.ShapeDtypeStruct((nseg, seg, n), out_dtype),
        scratch_shapes=[pltpu.VMEM((tm, d), BF16)],
        compiler_params=_params(("arbitrary", "arbitrary", "arbitrary")),
        name="mod_matmul",
    )(x3, mod3, mod3, w_bf16)


def _layer_norm_rows(y, g, b):
    mu = jnp.mean(y, axis=-1, keepdims=True)
    yc = y - mu
    var = jnp.mean(yc * yc, axis=-1, keepdims=True)
    return yc * lax.rsqrt(var + NORM_EPS) * g + b


def _outproj_kernel(*refs, mode):
    if mode == "plain":
        y_ref, x_ref, gate_ref, w_ref, g_ref, b_ref, o_ref = refs
        yin = y_ref[0].astype(BF16)
    else:
        ya_ref, yb_ref, og_ref, nw_ref, x_ref, gate_ref, w_ref, g_ref, b_ref, o_ref = refs
        hs = ya_ref[0] + yb_ref[0]
        og = og_ref[0]
        parts = []
        for h in range(4):
            seg = hs[:, h * 256:(h + 1) * 256]
            nw = nw_ref[:, h * 256:(h + 1) * 256]
            if mode == "mlstm":
                mu = jnp.mean(seg, axis=-1, keepdims=True)
                sc = seg - mu
                var = jnp.mean(sc * sc, axis=-1, keepdims=True)
                parts.append(sc * lax.rsqrt(var + NORM_EPS) * nw)
            else:
                ms = jnp.mean(seg * seg, axis=-1, keepdims=True)
                parts.append(seg * lax.rsqrt(ms + NORM_EPS) * nw)
        hn = jnp.concatenate(parts, axis=-1)
        act = _sigmoid(og) if mode == "mlstm" else og * _sigmoid(og)
        yin = (act * hn).astype(BF16)
    y = _dot(yin, w_ref[...])
    z = DEEPNORM_ALPHA * x_ref[0] + gate_ref[0] * y
    o_ref[0] = _layer_norm_rows(z, g_ref[...], b_ref[...])


def outproj_ln(mode, ys, x3, mod3, gate_chunk, w_bf16, ln_g, ln_b, norm_w=None, og=None, og_col=0, tm=512):
    nseg, seg, d = x3.shape
    k = w_bf16.shape[0]
    tok = lambda s, i: (s, i, 0)
    if mode == "plain":
        args = [ys]
        specs = [pl.BlockSpec((1, tm, k), tok)]
    else:
        args = [ys[0], ys[1], og, norm_w.reshape(1, k)]
        specs = [pl.BlockSpec((1, tm, k), tok), pl.BlockSpec((1, tm, k), tok),
                 pl.BlockSpec((1, tm, k), lambda s, i: (s, i, og_col)),
                 pl.BlockSpec((1, k), lambda s, i: (0, 0))]
    args += [x3, mod3, w_bf16, ln_g.reshape(1, d), ln_b.reshape(1, d)]
    specs += [pl.BlockSpec((1, tm, d), tok),
              pl.BlockSpec((1, 1, d), lambda s, i: (s, 0, gate_chunk)),
              pl.BlockSpec((k, d), lambda s, i: (0, 0)),
              pl.BlockSpec((1, d), lambda s, i: (0, 0)),
              pl.BlockSpec((1, d), lambda s, i: (0, 0))]
    return pl.pallas_call(
        functools.partial(_outproj_kernel, mode=mode),
        grid=(nseg, seg // tm),
        in_specs=specs,
        out_specs=pl.BlockSpec((1, tm, d), tok),
        out_shape=jax.ShapeDtypeStruct((nseg, seg, d), F32),
        compiler_params=_params(("arbitrary", "arbitrary")),
        name="outproj_ln_" + mode,
    )(*args)


def _tri(n, lower):
    r = lax.broadcasted_iota(jnp.int32, (n, n), 0)
    c = lax.broadcasted_iota(jnp.int32, (n, n), 1)
    return (c <= r) if lower else (c >= r)


def _mlstm_kernel(rowf_ref, rowb_ref, seq_ref, first_ref, last_ref,
                  pf_ref, pb_ref, gf_ref, gb_ref, gtf_ref, gtb_ref, bias_ref, biast_ref,
                  c0_ref, n0_ref, m0_ref, hf_ref, hb_ref, co_ref, no_ref, mo_ref,
                  c_s, n_s, m_s, *, L):
    item = pl.program_id(0)

    @pl.when(first_ref[item] == 1)
    def _():
        c_s[...] = c0_ref[0]
        n_s[...] = n0_ref[0]
        m_s[...] = m0_ref[0]

    dirs = [(pf_ref, gf_ref, gtf_ref, hf_ref), (pb_ref, gb_ref, gtb_ref, hb_ref)]
    masks = [_tri(L, lower=True), _tri(L, lower=False)]
    gates = []
    for d in range(2):
        g = dirs[d][1][0] + bias_ref[...]
        gt = dirs[d][2][0] + biast_ref[...]
        lf_c = _log_sigmoid(g[:, d * 8 + 4:d * 8 + 8])
        lf_r = _log_sigmoid(gt[d * 8 + 4:d * 8 + 8, :])
        b_c = _dot_exact_lhs(masks[d].astype(BF16), lf_c)
        b_r = _dot_exact_rhs(lf_r, masks[1 - d].astype(BF16))
        gates.append((g[:, d * 8:d * 8 + 4], gt[d * 8:d * 8 + 4, :], b_c, b_r))
    units = [(d, h) for d in range(2) for h in range(M_HEADS)]

    def unit_inputs(d, h):
        p_ref = dirs[d][0]
        q = p_ref[0, :, h * M_DK:(h + 1) * M_DK]
        k = p_ref[0, :, K_OFF + h * M_DK:K_OFF + (h + 1) * M_DK] * (M_DK ** -0.5)
        v = p_ref[0, :, V_OFF + h * M_DV:V_OFF + (h + 1) * M_DV].astype(BF16)
        li_c, li_r, b_c, b_r = gates[d]
        return q, k, v, li_c[:, h:h + 1], li_r[h:h + 1, :], b_c[:, h:h + 1], b_r[h:h + 1, :]

    qks = []
    for d, h in units:
        q, k, _, _, _, _, _ = unit_inputs(d, h)
        qks.append(_dot(q.astype(BF16), k.astype(BF16), ((1,), (1,))))
    smats, eis, mts = [], [], []
    for u, (d, h) in enumerate(units):
        _, _, _, _, lir, bc, br = unit_inputs(d, h)
        m_prev = m_s[u:u + 1, 0:1]
        dmat = jnp.where(masks[d], bc - br + lir, NEG_INF)
        inter = bc + m_prev
        mt = jnp.maximum(inter, jnp.max(dmat, axis=-1, keepdims=True))
        smats.append(qks[u] * jnp.exp(dmat - mt))
        eis.append(jnp.exp(inter - mt))
        mts.append(mt)
    nums = []
    for u, (d, h) in enumerate(units):
        q, _, v, _, _, _, _ = unit_inputs(d, h)
        nums.append(_dot(smats[u].astype(BF16), v) + eis[u] * _dot(q.astype(BF16), c_s[u].astype(BF16)))
    for u, (d, h) in enumerate(units):
        q, _, _, _, _, _, _ = unit_inputs(d, h)
        den = (jnp.sum(smats[u], axis=-1, keepdims=True)
               + eis[u] * jnp.sum(q * n_s[u:u + 1, :], axis=-1, keepdims=True))
        dirs[d][3][0, :, h * M_DV:(h + 1) * M_DV] = nums[u] / jnp.maximum(jnp.abs(den), jnp.exp(-mts[u]))
    for u, (d, h) in enumerate(units):
        _, k, v, lic, lir, bc, br = unit_inputs(d, h)
        last = L - 1 if d == 0 else 0
        m_prev = m_s[u:u + 1, 0:1]
        tot = br[:, last:last + 1]
        g_c = tot - bc + lic
        g_r = tot - br + lir
        m_new = jnp.maximum(tot + m_prev, jnp.max(g_r, axis=-1, keepdims=True))
        kw = k * jnp.exp(g_c - m_new)
        dec = jnp.exp(tot + m_prev - m_new)
        c_s[u] = dec * c_s[u] + _dot(kw.astype(BF16), v, ((0,), (0,)))
        n_s[u:u + 1, :] = dec * n_s[u:u + 1, :] + jnp.sum(kw, axis=0, keepdims=True)
        m_s[u:u + 1, :] = jnp.broadcast_to(m_new, (1, 128))

    @pl.when(last_ref[item] == 1)
    def _():
        co_ref[0] = c_s[...]
        no_ref[0] = n_s[...]
        mo_ref[0] = m_s[...]


def _scan_items(seq_lens, L):
    rowf, rowb, seq, first, last = [], [], [], [], []
    base = 0
    for s, t in enumerate(seq_lens):
        nc = t // L
        for c in range(nc):
            rowf.append(base + c)
            rowb.append(base + nc - 1 - c)
            seq.append(s)
            first.append(int(c == 0))
            last.append(int(c == nc - 1))
        base += nc
    return [jnp.asarray(np.array(a, np.int32)) for a in (rowf, rowb, seq, first, last)]


def mlstm_scan(p, seq_lens, g, bias, c0, n0, m0, L):
    items = _scan_items(seq_lens, L)
    nseq = len(seq_lens)
    rows = p.shape[0]
    hshape = jax.ShapeDtypeStruct((rows, L, M_HEADS * M_DV), F32)
    fwd = lambda i, rf, rb, sq, fs, ls: (rf[i], 0, 0)
    bwd = lambda i, rf, rb, sq, fs, ls: (rb[i], 0, 0)
    st4 = lambda i, rf, rb, sq, fs, ls: (sq[i], 0, 0, 0)
    st3 = lambda i, rf, rb, sq, fs, ls: (sq[i], 0, 0)
    fix = lambda i, rf, rb, sq, fs, ls: (0, 0)
    gt = jnp.swapaxes(g, 1, 2)
    return pl.pallas_call(
        functools.partial(_mlstm_kernel, L=L),
        grid_spec=pltpu.PrefetchScalarGridSpec(
            num_scalar_prefetch=5,
            grid=(items[0].shape[0],),
            in_specs=[pl.BlockSpec((1, L, QKV_COLS), fwd), pl.BlockSpec((1, L, QKV_COLS), bwd),
                      pl.BlockSpec((1, L, 16), fwd), pl.BlockSpec((1, L, 16), bwd),
                      pl.BlockSpec((1, 16, L), fwd), pl.BlockSpec((1, 16, L), bwd),
                      pl.BlockSpec((1, 16), fix), pl.BlockSpec((16, 1), fix),
                      pl.BlockSpec((1, 8, M_DK, M_DV), st4),
                      pl.BlockSpec((1, 8, M_DK), st3),
                      pl.BlockSpec((1, 8, M_DK), st3)],
            out_specs=[pl.BlockSpec((1, L, HID_COLS), fwd), pl.BlockSpec((1, L, HID_COLS), bwd),
                       pl.BlockSpec((1, 8, M_DK, M_DV), st4),
                       pl.BlockSpec((1, 8, M_DK), st3),
                       pl.BlockSpec((1, 8, M_DK), st3)],
            scratch_shapes=[pltpu.VMEM((8, M_DK, M_DV), F32), pltpu.VMEM((8, M_DK), F32),
                            pltpu.VMEM((8, M_DK), F32)]),
        out_shape=[hshape, hshape,
                   jax.ShapeDtypeStruct((nseq, 8, M_DK, M_DV), F32),
                   jax.ShapeDtypeStruct((nseq, 8, M_DK), F32),
                   jax.ShapeDtypeStruct((nseq, 8, M_DK), F32)],
        compiler_params=_params(("arbitrary",)),
        name="mlstm_scan",
    )(*items, p, p, g, g, gt, gt, bias.reshape(1, 16), bias.reshape(16, 1), c0, n0, m0)


def _gla_kernel(rowf_ref, rowb_ref, seq_ref, first_ref, last_ref,
                pf_ref, pb_ref, gf_ref, gb_ref, w2_ref, b2_ref, s0_ref, of_ref, ob_ref, so_ref, s_s, *, L):
    item = pl.program_id(0)

    @pl.when(first_ref[item] == 1)
    def _():
        s_s[...] = s0_ref[0]

    dirs = [(pf_ref, gf_ref, of_ref), (pb_ref, gb_ref, ob_ref)]
    masks = [_tri(L, lower=True), _tri(L, lower=False)]
    bcs = []
    for d in range(2):
        gr = dirs[d][1][0][:, d * G_GATE_RANK:(d + 1) * G_GATE_RANK]
        pre = lax.dot_general(gr, w2_ref[d], (((1,), (0,)), ((), ())), precision=lax.Precision.HIGHEST,
                              preferred_element_type=F32) + b2_ref[d]
        la = _log_sigmoid(pre) * (1.0 / G_GATE_NORM)
        bcs.append(_dot_exact_lhs(masks[d].astype(BF16), la))
    units = [(d, h) for d in range(2) for h in range(G_HEADS)]

    def unit_inputs(d, h):
        p_ref = dirs[d][0]
        q = p_ref[0, :, h * G_DK:(h + 1) * G_DK] * (G_DK ** -0.5)
        k = p_ref[0, :, K_OFF + h * G_DK:K_OFF + (h + 1) * G_DK]
        v = p_ref[0, :, V_OFF + h * G_DV:V_OFF + (h + 1) * G_DV].astype(BF16)
        return q, k, v, bcs[d][:, h * G_DK:(h + 1) * G_DK]

    qds, a_mats = [], []
    for d, h in units:
        q, k, _, bc = unit_inputs(d, h)
        qd = (q * jnp.exp(bc)).astype(BF16)
        kd = (k * jnp.exp(-bc)).astype(BF16)
        qds.append(qd)
        a_mats.append(jnp.where(masks[d], _dot(qd, kd, ((1,), (1,))), 0.0).astype(BF16))
    for u, (d, h) in enumerate(units):
        _, _, v, _ = unit_inputs(d, h)
        dirs[d][2][0, :, h * G_DV:(h + 1) * G_DV] = (_dot(a_mats[u], v)
                                                     + _dot(qds[u], s_s[u].astype(BF16), ((1,), (1,))))
    for u, (d, h) in enumerate(units):
        _, k, v, bc = unit_inputs(d, h)
        last = L - 1 if d == 0 else 0
        bl = bc[last:last + 1, :]
        kl = (k * jnp.exp(bl - bc)).astype(BF16)
        s_s[u] = s_s[u] * jnp.exp(bl) + _dot(v, kl, ((0,), (0,)))

    @pl.when(last_ref[item] == 1)
    def _():
        so_ref[0] = s_s[...]


def gla_scan(p, seq_lens, gr, w2, b2, s0t, L):
    items = _scan_items(seq_lens, L)
    nseq = len(seq_lens)
    oshape = jax.ShapeDtypeStruct((p.shape[0], L, G_HEADS * G_DV), F32)
    fwd = lambda i, rf, rb, sq, fs, ls: (rf[i], 0, 0)
    bwd = lambda i, rf, rb, sq, fs, ls: (rb[i], 0, 0)
    st4 = lambda i, rf, rb, sq, fs, ls: (sq[i], 0, 0, 0)
    fix3 = lambda i, rf, rb, sq, fs, ls: (0, 0, 0)
    return pl.pallas_call(
        functools.partial(_gla_kernel, L=L),
        grid_spec=pltpu.PrefetchScalarGridSpec(
            num_scalar_prefetch=5,
            grid=(items[0].shape[0],),
            in_specs=[pl.BlockSpec((1, L, QKV_COLS), fwd), pl.BlockSpec((1, L, QKV_COLS), bwd),
                      pl.BlockSpec((1, L, 32), fwd), pl.BlockSpec((1, L, 32), bwd),
                      pl.BlockSpec((2, G_GATE_RANK, 512), fix3),
                      pl.BlockSpec((2, 1, 512), fix3),
                      pl.BlockSpec((1, 8, G_DV, G_DK), st4)],
            out_specs=[pl.BlockSpec((1, L, HID_COLS), fwd), pl.BlockSpec((1, L, HID_COLS), bwd),
                       pl.BlockSpec((1, 8, G_DV, G_DK), st4)],
            scratch_shapes=[pltpu.VMEM((8, G_DV, G_DK), F32)]),
        out_shape=[oshape, oshape, jax.ShapeDtypeStruct((nseq, 8, G_DV, G_DK), F32)],
        compiler_params=_params(("arbitrary",)),
        name="gla_scan",
    )(*items, p, p, gr, gr, w2, b2.reshape(2, 1, 512), s0t)


ATTN_HEADS_PER_STEP = 2


def _attn_kernel(q_ref, k_ref, v_ref, o_ref, *, scale):
    scores = [_dot(q_ref[0, j], k_ref[0, j], ((1,), (1,))) * scale for j in range(ATTN_HEADS_PER_STEP)]
    for j, s in enumerate(scores):
        m = jnp.max(s, axis=-1, keepdims=True)
        p = jnp.exp(s - m)
        l = jnp.sum(p, axis=-1, keepdims=True)
        o_ref[0, j] = _dot(p.astype(BF16), v_ref[0, j]) / l


def attention(q, k, v, tq):
    b, h, lq, dq = q.shape
    lk, dv = k.shape[2], v.shape[3]
    hb = ATTN_HEADS_PER_STEP
    return pl.pallas_call(
        functools.partial(_attn_kernel, scale=dq ** -0.5),
        grid=(b, h // hb, lq // tq),
        in_specs=[pl.BlockSpec((1, hb, tq, dq), lambda b, h, i: (b, h, i, 0)),
                  pl.BlockSpec((1, hb, lk, dq), lambda b, h, i: (b, h, 0, 0)),
                  pl.BlockSpec((1, hb, lk, dv), lambda b, h, i: (b, h, 0, 0))],
        out_specs=pl.BlockSpec((1, hb, tq, dv), lambda b, h, i: (b, h, i, 0)),
        out_shape=jax.ShapeDtypeStruct((b, h, lq, dv), F32),
        compiler_params=_params(("arbitrary", "arbitrary", "arbitrary")),
        name="attention",
    )(q, k, v)


NA_RB = 8


def _na_kernel(q_ref, k_ref, v_ref, kc_ref, vc_ref, bias_ref, o_ref, *, rows):
    j = pl.program_id(2)
    scale = NA_HD ** -0.5
    s_ctx_all = _dot(q_ref[0, 0], kc_ref[0, 0], ((1,), (1,))) * scale
    offs, s_locs = [], []
    for a in range(NA_RB):
        r = j * NA_RB + a
        start = jnp.clip(r - NA_ROWS // 2, 0, rows - NA_ROWS)
        dr0 = start - r + (NA_ROWS - 1)
        offs.append(pl.multiple_of(start * GRID_W, GRID_W))
        qa = q_ref[0, 0, a * GRID_W:(a + 1) * GRID_W, :]
        kl = k_ref[0, 0, pl.ds(offs[a], NA_ROWS * GRID_W), :]
        s_locs.append(_dot(qa, kl, ((1,), (1,))) * scale + bias_ref[0, dr0])
    p_locs, p_ctxs, ls = [], [], []
    for a in range(NA_RB):
        s_ctx = s_ctx_all[a * GRID_W:(a + 1) * GRID_W, :]
        m = jnp.maximum(jnp.max(s_locs[a], axis=-1, keepdims=True), jnp.max(s_ctx, axis=-1, keepdims=True))
        p_loc = jnp.exp(s_locs[a] - m)
        p_ctx = jnp.exp(s_ctx - m)
        ls.append(jnp.sum(p_loc, axis=-1, keepdims=True) + jnp.sum(p_ctx, axis=-1, keepdims=True))
        p_locs.append(p_loc.astype(BF16))
        p_ctxs.append(p_ctx.astype(BF16))
    o_ctx_all = _dot(jnp.concatenate(p_ctxs, axis=0), vc_ref[0, 0])
    for a in range(NA_RB):
        vl = v_ref[0, 0, pl.ds(offs[a], NA_ROWS * GRID_W), :]
        o = _dot(p_locs[a], vl) + o_ctx_all[a * GRID_W:(a + 1) * GRID_W, :]
        o_ref[0, 0, a * GRID_W:(a + 1) * GRID_W, :] = o / ls[a]


def na_bias_table(rpb):
    cq = np.arange(GRID_W)[:, None]
    ck = np.arange(GRID_W)[None, :]
    cs = np.clip(cq - NA_COLS // 2, 0, GRID_W - NA_COLS)
    ok = (ck >= cs) & (ck < cs + NA_COLS)
    dc = np.clip(ck - cq, -(NA_COLS - 1), NA_COLS - 1) + (NA_COLS - 1)
    t = jnp.where(ok[None, None], rpb.astype(F32)[:, :, dc], NEG_INF)
    rows = np.arange(NA_ROWS)[:, None] + np.arange(NA_ROWS)[None, :]
    tf = t[:, rows]
    return jnp.transpose(tf, (0, 1, 3, 2, 4)).reshape(NA_HEADS, NA_ROWS, GRID_W, NA_ROWS * GRID_W)


def na_attention(q, k, v, kc, vc, bias):
    b, h, t, dh = q.shape
    lc = kc.shape[2]
    rows = t // GRID_W
    full = lambda b, h, j: (b, h, 0, 0)
    return pl.pallas_call(
        functools.partial(_na_kernel, rows=rows),
        grid=(b, h, rows // NA_RB),
        in_specs=[pl.BlockSpec((1, 1, NA_RB * GRID_W, dh), lambda b, h, j: (b, h, j, 0)),
                  pl.BlockSpec((1, 1, t, dh), full), pl.BlockSpec((1, 1, t, dh), full),
                  pl.BlockSpec((1, 1, lc, dh), full), pl.BlockSpec((1, 1, lc, dh), full),
                  pl.BlockSpec((1, NA_ROWS, GRID_W, NA_ROWS * GRID_W), lambda b, h, j: (h, 0, 0, 0))],
        out_specs=pl.BlockSpec((1, 1, NA_RB * GRID_W, dh), lambda b, h, j: (b, h, j, 0)),
        out_shape=jax.ShapeDtypeStruct((b, h, t, dh), F32),
        compiler_params=_params(("arbitrary", "arbitrary", "arbitrary")),
        name="na_attention",
    )(q, k, v, kc, vc, bias)


def _rms_rows(x, g):
    return x * lax.rsqrt(jnp.mean(x * x, axis=-1, keepdims=True) + NORM_EPS) * g


def _mla_q_kernel(cq_ref, g_ref, w_ref, cos_ref, sin_ref, o_ref):
    r = _dot(_rms_rows(cq_ref[0], g_ref[...]).astype(BF16), w_ref[...])
    nn = MLA_HEADS * MLA_NOPE
    nr = MLA_HEADS * MLA_ROPE
    o_ref[0, :, :nn] = r[:, :nn]
    o_ref[0, :, nn:] = r[:, nn:nn + nr] * cos_ref[0] + r[:, nn + nr:] * sin_ref[0]


def mla_q(p, q_norm, w_q3, cos_q, sin_q, tm=512):
    nseg, seg, _ = p.shape
    nout = MLA_HEADS * (MLA_NOPE + MLA_ROPE)
    nr = MLA_HEADS * MLA_ROPE
    tok = lambda s, i: (s, i, 0)
    return pl.pallas_call(
        _mla_q_kernel,
        grid=(nseg, seg // tm),
        in_specs=[pl.BlockSpec((1, tm, MLA_Q_LORA), tok),
                  pl.BlockSpec((1, MLA_Q_LORA), lambda s, i: (0, 0)),
                  pl.BlockSpec(w_q3.shape, lambda s, i: (0, 0)),
                  pl.BlockSpec((1, tm, nr), tok), pl.BlockSpec((1, tm, nr), tok)],
        out_specs=pl.BlockSpec((1, tm, nout), tok),
        out_shape=jax.ShapeDtypeStruct((nseg, seg, nout), F32),
        compiler_params=_params(("arbitrary", "arbitrary")),
        name="mla_q",
    )(p, q_norm.reshape(1, -1), w_q3, cos_q, sin_q)


def _mla_kv_kernel(ckv_ref, kpe_ref, g_ref, w_ref, cos_ref, sin_ref, ckvn_ref, kpeo_ref, kv_ref):
    cn = _rms_rows(ckv_ref[0], g_ref[...])
    ckvn_ref[0] = cn
    kv_ref[0] = _dot(cn.astype(BF16), w_ref[...])
    kp = kpe_ref[0]
    kpeo_ref[0] = kp[:, :MLA_ROPE] * cos_ref[0] + kp[:, MLA_ROPE:2 * MLA_ROPE] * sin_ref[0]


def mla_kv(p, kv_norm, w_kv, cos_k, sin_k, tm=512):
    nseg, seg, _ = p.shape
    nkv = w_kv.shape[1]
    tok = lambda s, i: (s, i, 0)
    return pl.pallas_call(
        _mla_kv_kernel,
        grid=(nseg, seg // tm),
        in_specs=[pl.BlockSpec((1, tm, MLA_KV_LORA), lambda s, i: (s, i, MLA_Q_LORA // MLA_KV_LORA)),
                  pl.BlockSpec((1, tm, 128), lambda s, i: (s, i, (MLA_Q_LORA + MLA_KV_LORA) // 128)),
                  pl.BlockSpec((1, MLA_KV_LORA), lambda s, i: (0, 0)),
                  pl.BlockSpec(w_kv.shape, lambda s, i: (0, 0)),
                  pl.BlockSpec((1, tm, MLA_ROPE), tok), pl.BlockSpec((1, tm, MLA_ROPE), tok)],
        out_specs=[pl.BlockSpec((1, tm, MLA_KV_LORA), tok), pl.BlockSpec((1, tm, MLA_ROPE), tok),
                   pl.BlockSpec((1, tm, nkv), tok)],
        out_shape=[jax.ShapeDtypeStruct((nseg, seg, MLA_KV_LORA), F32),
                   jax.ShapeDtypeStruct((nseg, seg, MLA_ROPE), F32),
                   jax.ShapeDtypeStruct((nseg, seg, nkv), F32)],
        compiler_params=_params(("arbitrary", "arbitrary")),
        name="mla_kv",
    )(p, p, kv_norm.reshape(1, -1), w_kv, cos_k, sin_k)


def _mm_kernel(a_ref, w_ref, o_ref):
    o_ref[...] = _dot(a_ref[...].astype(BF16), w_ref[...])


def matmul(a, w_bf16, tm):
    m, k = a.shape
    n = w_bf16.shape[1]
    return pl.pallas_call(
        _mm_kernel,
        grid=(m // tm,),
        in_specs=[pl.BlockSpec((tm, k), lambda i: (i, 0)), pl.BlockSpec((k, n), lambda i: (0, 0))],
        out_specs=pl.BlockSpec((tm, n), lambda i: (i, 0)),
        out_shape=jax.ShapeDtypeStruct((m, n), F32),
        compiler_params=_params(("arbitrary",)),
        name="matmul",
    )(a, w_bf16)


PEER_RT = 128
NOT_TOP = 99.0
RANK_CODE = 2.0 ** 100


def _top16(s, exact):
    vals = []
    if exact:
        key = lax.broadcasted_iota(jnp.int32, s.shape, 0).astype(F32)
        rank = jnp.full(s.shape, NOT_TOP, F32)
        for r in range(PEER_TOPK):
            m = jnp.max(s, axis=0, keepdims=True)
            hit = key == jnp.min(jnp.where(s == m, key, 1e9), axis=0, keepdims=True)
            rank = jnp.where(hit, float(r), rank)
            s = jnp.where(hit, NEG_INF, s)
            vals.append(m)
        return vals, rank
    for r in range(PEER_TOPK):
        m = jnp.max(s, axis=0, keepdims=True)
        s = jnp.where(s == m, -RANK_CODE * (1.0 + r / 32.0), s)
        vals.append(m)
    return vals, jnp.where(s <= -0.5 * RANK_CODE, s * (-32.0 / RANK_CODE) - 32.0, NOT_TOP)


def _pair_topk(av, bv, exact):
    n = av[0].shape[-1]
    a_lo, a_hi = jnp.concatenate(av[:8], 0), jnp.concatenate(av[8:], 0)
    b_lo, b_hi = jnp.concatenate(bv[:8], 0), jnp.concatenate(bv[8:], 0)
    row = lax.broadcasted_iota(jnp.int32, (8, n), 0).astype(F32)

    no_pos = 1e8

    def rows_b(a, b_blk, boff, nvalid):
        ok = row < nvalid
        return jnp.where(ok, av[a] + b_blk, NEG_INF), jnp.where(ok, a * 16.0 + boff + row, no_pos)

    def rows_a(b, a_blk, aoff, lo, hi):
        ok = (row >= lo) & (row < hi)
        return jnp.where(ok, a_blk + bv[b], NEG_INF), jnp.where(ok, (aoff + row) * 16.0 + b, no_pos)

    groups = [rows_b(0, b_lo, 0, 8), rows_b(0, b_hi, 8, 8), rows_b(1, b_lo, 0, 8), rows_b(2, b_lo, 0, 5),
              rows_b(3, b_lo, 0, 4), rows_a(0, a_lo, 0, 4, 8), rows_a(0, a_hi, 8, 0, 8),
              rows_a(1, a_lo, 0, 4, 8), rows_a(2, a_lo, 0, 4, 5)]
    cands = [g[0] for g in groups]
    poss = [g[1] for g in groups]
    sels = [jnp.zeros((8, n), F32) for _ in groups]
    top = av[0] + bv[0]
    z = jnp.zeros((1, n), F32)
    for _ in range(PEER_TOPK):
        m = functools.reduce(jnp.maximum, cands)
        m = jnp.max(m, axis=0, keepdims=True)
        if exact:
            first = functools.reduce(jnp.minimum, [jnp.where(c == m, p, 1e9) for c, p in zip(cands, poss)])
            first = jnp.min(first, axis=0, keepdims=True)
            hits = [p == first for p in poss]
            cands = [jnp.where(hh, NEG_INF, c) for hh, c in zip(hits, cands)]
            sels = [jnp.where(hh, 1.0, s) for hh, s in zip(hits, sels)]
        else:
            cands = [jnp.where(c == m, -RANK_CODE, c) for c in cands]
        z = z + jnp.exp(m - top)
    if not exact:
        sels = [jnp.where(c == -RANK_CODE, 1.0, 0.0) for c in cands]
    cnt = lambda x: jnp.sum(x, axis=0, keepdims=True)
    cut_lo = sels[5] + sels[7] + sels[8]
    for a, c in enumerate([cnt(sels[0]) + cnt(sels[1]), cnt(sels[2]), cnt(sels[3]), cnt(sels[4])]):
        cut_lo = cut_lo + jnp.where(row == a, c, 0.0)
    return cut_lo, sels[6], z, cnt(cut_lo) + cnt(sels[6])


def _peer_route_kernel(x_ref, sh_ref, sc_ref, wq_ref, sk_ref, xm_ref, e1_ref, cut_ref, e2_ref, r2_ref, q_s, *, tm):
    xm = (x_ref[0] * (1.0 + sc_ref[0]) + sh_ref[0]).astype(BF16)
    xm_ref[0] = xm
    q = _dot(xm, wq_ref[...])
    for hp in range(2 * PEER_HEADS):
        q_s[hp] = q[:, hp * PEER_HALF:(hp + 1) * PEER_HALF]

    def route(h, tok, exact):
        def scores(hp):
            return lax.dot_general(sk_ref[hp], q_s[hp, tok, :], (((1,), (1,)), ((), ())),
                                   precision=lax.Precision.HIGHEST, preferred_element_type=F32)

        s1, s2 = scores(2 * h), scores(2 * h + 1)
        av, rank1 = _top16(s1, exact)
        bv, rank2 = _top16(s2, exact)
        cut_lo, cut_hi, z, nsel = _pair_topk(av, bv, exact)
        cut = jnp.zeros_like(s1)
        for r in range(PEER_TOPK):
            src = cut_lo if r < 8 else cut_hi
            cut = jnp.where(rank1 == float(r), src[r % 8:r % 8 + 1, :], cut)
        e1_ref[0, h, :, tok] = (jnp.exp(s1 - av[0]) / z).astype(BF16)
        cut_ref[0, h, :, tok] = cut.astype(BF16)
        e2_ref[0, h, :, tok] = jnp.exp(s2 - bv[0]).astype(BF16)
        r2_ref[0, h, :, tok] = rank2.astype(BF16)
        ranked = lambda rk: jnp.sum(jnp.where(rk < PEER_TOPK, 1.0, 0.0), axis=0, keepdims=True)
        return ranked(rank1), ranked(rank2), nsel

    def body(h, carry):
        toks = [pl.ds(t0, PEER_RT) for t0 in range(0, tm, PEER_RT)]
        counts = [route(h, tok, exact=False) for tok in toks]
        for tok, cnts in zip(toks, counts):
            bad = functools.reduce(jnp.maximum, [jnp.abs(cn - PEER_TOPK) for cn in cnts])

            @pl.when(jnp.max(bad) > 0.0)
            def _():
                route(h, tok, exact=True)
        return carry

    lax.fori_loop(0, PEER_HEADS, body, 0)


def peer_route(x3, mod3, shift_chunk, wq_bf16, subkeys, tm=512):
    nseg, seg, d = x3.shape
    tok = lambda s, i: (s, i, 0)
    rshape = jax.ShapeDtypeStruct((nseg, PEER_HEADS, PEER_NKEYS, seg), BF16)
    rspec = pl.BlockSpec((1, PEER_HEADS, PEER_NKEYS, tm), lambda s, i: (s, 0, 0, i))
    return pl.pallas_call(
        functools.partial(_peer_route_kernel, tm=tm),
        grid=(nseg, seg // tm),
        in_specs=[pl.BlockSpec((1, tm, d), tok),
                  pl.BlockSpec((1, 1, d), lambda s, i: (s, 0, shift_chunk)),
                  pl.BlockSpec((1, 1, d), lambda s, i: (s, 0, shift_chunk + 1)),
                  pl.BlockSpec(wq_bf16.shape, lambda s, i: (0, 0)),
                  pl.BlockSpec((2 * PEER_HEADS, PEER_NKEYS, PEER_HALF), lambda s, i: (0, 0, 0))],
        out_specs=[pl.BlockSpec((1, tm, d), tok), rspec, rspec, rspec, rspec],
        out_shape=[jax.ShapeDtypeStruct((nseg, seg, d), BF16)] + [rshape] * 4,
        scratch_shapes=[pltpu.VMEM((2 * PEER_HEADS, tm, PEER_HALF), F32)],
        compiler_params=_params(("arbitrary", "arbitrary")),
        name="peer_route",
    )(x3, mod3, mod3, wq_bf16, subkeys.reshape(2 * PEER_HEADS, PEER_NKEYS, PEER_HALF))


PEER_CE = 1024


def _gelu_tanh(x):
    return 0.5 * x * (1.0 + jnp.tanh(0.7978845608028654 * (x + 0.044715 * x * x * x)))


def _peer_dense_kernel(xm_ref, u_ref, vt_ref, e1_ref, cut_ref, e2_ref, r2_ref, x_ref, gate_ref, g_ref, b_ref,
                       o_ref, acc_s, at_s, w_s, e2_s, r2_s, *, tm):
    e = pl.program_id(2)
    nb = PEER_CE // PEER_NKEYS
    ntt = tm // PEER_RT

    @pl.when(e == 0)
    def _():
        acc_s[...] = jnp.zeros_like(acc_s)
        e2_s[:, :, :tm] = e2_ref[0]
        r2_s[:, :, :tm] = r2_ref[0]

    packed = (PEER_NKEYS // 16, 16, PEER_RT)
    ng = 2

    def gate_tiles(tt, i0):
        tok = slice(tt * PEER_RT, (tt + 1) * PEER_RT)
        gmats = [jnp.zeros(packed, BF16) for _ in range(ng)]
        for h in range(PEER_HEADS):
            e2 = e2_s[h, :, tok].reshape(packed)
            r2 = r2_s[h, :, tok].reshape(packed)
            for k in range(ng):
                i = i0 + k
                e1 = jnp.broadcast_to(e1_ref[0, h, i:i + 1, tok], (16, PEER_RT))[None]
                cut = jnp.broadcast_to(cut_ref[0, h, i:i + 1, tok], (16, PEER_RT))[None]
                gmats[k] = gmats[k] + e1 * jnp.where(r2 < cut, e2, jnp.zeros_like(e2))
        for k in range(ng):
            rows = slice((i0 + k) * PEER_NKEYS, (i0 + k + 1) * PEER_NKEYS)
            act = _gelu_tanh(at_s[rows, tok]).astype(BF16)
            w_s[rows, tok] = gmats[k].reshape(PEER_NKEYS, PEER_RT) * act

    at_s[:, :tm] = _dot(u_ref[0], xm_ref[0], ((1,), (1,)))
    for tt in range(ntt):
        for i0 in range(0, nb, ng):
            gate_tiles(tt, i0)
    acc_s[:, :tm] += _dot(vt_ref[0, 0], w_s[:, :tm])

    @pl.when(e == pl.num_programs(2) - 1)
    def _():
        z = DEEPNORM_ALPHA * x_ref[0] + gate_ref[0] * acc_s[:, :tm].T
        o_ref[0] = _layer_norm_rows(z, g_ref[...], b_ref[...])


def peer_dense(xm, u_all, vt_all, l, e1, cut, e2, r2, x3, mod3, gate_chunk, ln_g, ln_b, tm=1024):
    nseg, seg, d = x3.shape
    ne = u_all.shape[1]
    nb = PEER_CE // PEER_NKEYS
    tp = tm + PEER_RT
    tok = lambda s, i, e: (s, i, 0)
    chunk = pl.BlockSpec((1, PEER_HEADS, nb, tm), lambda s, i, e: (s, 0, e, i))
    full = pl.BlockSpec((1, PEER_HEADS, PEER_NKEYS, tm), lambda s, i, e: (s, 0, 0, i))
    return pl.pallas_call(
        functools.partial(_peer_dense_kernel, tm=tm),
        grid=(nseg, seg // tm, ne // PEER_CE),
        in_specs=[pl.BlockSpec((1, tm, d), tok),
                  pl.BlockSpec((1, PEER_CE, d), lambda s, i, e: (l, e, 0)),
                  pl.BlockSpec((1, 1, d, PEER_CE), lambda s, i, e: (l, e, 0, 0)),
                  chunk, chunk, full, full,
                  pl.BlockSpec((1, tm, d), tok),
                  pl.BlockSpec((1, 1, d), lambda s, i, e: (s, 0, gate_chunk)),
                  pl.BlockSpec((1, d), lambda s, i, e: (0, 0)),
                  pl.BlockSpec((1, d), lambda s, i, e: (0, 0))],
        out_specs=pl.BlockSpec((1, tm, d), tok),
        out_shape=jax.ShapeDtypeStruct((nseg, seg, d), F32),
        scratch_shapes=[pltpu.VMEM((d, tp), F32), pltpu.VMEM((PEER_CE, tp), F32), pltpu.VMEM((PEER_CE, tp), BF16),
                        pltpu.VMEM((PEER_HEADS, PEER_NKEYS, tp), BF16), pltpu.VMEM((PEER_HEADS, PEER_NKEYS, tp), BF16)],
        compiler_params=_params(("arbitrary", "arbitrary", "arbitrary")),
        name="peer_dense",
    )(xm, u_all, vt_all, e1, cut, e2, r2, x3, mod3, ln_g.reshape(1, d), ln_b.reshape(1, d))


def peer_layer(x3, mod3, l, wq, subkeys, u_all, vt_all, ln_g, ln_b):
    xm, e1, cut, e2, r2 = peer_route(x3, mod3, 3, wq.astype(BF16), subkeys)
    return peer_dense(xm, u_all, vt_all, l, e1, cut, e2, r2, x3, mod3, 5, ln_g, ln_b)


def _pad_cols(w, n):
    return jnp.pad(w, ((0, 0), (0, n - w.shape[1])))


def _stream(prompt_part, sample_part):
    return jnp.concatenate([prompt_part.reshape(1, -1, prompt_part.shape[-1]), sample_part], axis=0)


def _head_major(a, heads):
    b, t, _ = a.shape
    return jnp.transpose(a.reshape(b, t, heads, -1), (0, 2, 1, 3))


def _token_major(a):
    b, h, t, dh = a.shape
    return jnp.transpose(a, (0, 2, 1, 3)).reshape(b, t, h * dh)


MLSTM_CHUNK = 256
GLA_CHUNK = 64
NPROJ = 3200


def mlstm_layer(x3, mod3, bp, lp, st_c, st_n, st_m, w_in, b_gate, norm_w, w_out, ln_g, ln_b):
    nseg, seg, _ = x3.shape
    bs = nseg - 1
    p = mod_matmul(x3, mod3, 0, _pad_cols(w_in, NPROJ).astype(BF16))
    L = min(MLSTM_CHUNK, lp)
    rows = nseg * seg // L
    seq_lens = [lp] * bp + [seg] * bs
    c0 = jnp.concatenate([jnp.zeros((bp, 8, M_DK, M_DV), F32), st_c.reshape(bs, 8, M_DK, M_DV)], 0)
    n0 = jnp.concatenate([jnp.zeros((bp, 8, M_DK), F32), st_n.reshape(bs, 8, M_DK)], 0)
    m0 = jnp.concatenate([jnp.zeros((bp, 8, M_DK), F32),
                          jnp.broadcast_to(st_m.reshape(bs, 8, 1), (bs, 8, M_DK))], 0)
    hf, hb, c_new, n_new, m_new = mlstm_scan(p.reshape(rows, L, NPROJ), seq_lens,
                                             p[:, :, GATE_OFF:GATE_OFF + 4 * M_HEADS].reshape(rows, L, 4 * M_HEADS), b_gate, c0, n0, m0, L)
    x3 = outproj_ln("mlstm", (hf.reshape(nseg, seg, -1), hb.reshape(nseg, seg, -1)), x3, mod3, 2,
                    w_out.astype(BF16), ln_g, ln_b, norm_w=norm_w, og=p, og_col=2)
    return (x3, c_new[:bp].reshape(bp, 2, M_HEADS, M_DK, M_DV), n_new[:bp].reshape(bp, 2, M_HEADS, M_DK),
            m_new[:bp, :, 0].reshape(bp, 2, M_HEADS))


def gla_layer(x3, mod3, bp, lp, st_s, w_in, w_gate2, b_gate2, norm_w, w_out, ln_g, ln_b):
    nseg, seg, _ = x3.shape
    bs = nseg - 1
    p = mod_matmul(x3, mod3, 0, _pad_cols(w_in, NPROJ).astype(BF16))
    L = GLA_CHUNK
    rows = nseg * seg // L
    seq_lens = [lp] * bp + [seg] * bs
    s0t = jnp.concatenate([jnp.zeros((bp, 8, G_DV, G_DK), F32),
                           jnp.swapaxes(st_s.reshape(bs, 8, G_DK, G_DV), -1, -2)], 0)
    of, ob, s_new = gla_scan(p.reshape(rows, L, NPROJ), seq_lens, p[:, :, GATE_OFF:GATE_OFF + 2 * G_GATE_RANK].reshape(rows, L, 2 * G_GATE_RANK),
                             w_gate2, b_gate2, s0t, L)
    x3 = outproj_ln("gla", (of.reshape(nseg, seg, -1), ob.reshape(nseg, seg, -1)), x3, mod3, 2,
                    w_out.astype(BF16), ln_g, ln_b, norm_w=jnp.tile(norm_w, G_HEADS), og=p, og_col=2)
    return x3, jnp.swapaxes(s_new[:bp], -1, -2).reshape(bp, 2, G_HEADS, G_DK, G_DV)


def na_layer(x3, mod3, bp, lp, cache_k, cache_v, w_in, rpb, w_out, ln_g, ln_b):
    nseg, seg, _ = x3.shape
    bs = nseg - 1
    hd = NA_HEADS * NA_HD
    p = mod_matmul(x3, mod3, 0, w_in.astype(BF16))
    pp = p[0].reshape(bp, lp, 3 * hd)
    hm = lambda a: _head_major(a, NA_HEADS).astype(BF16)
    yp = attention(hm(pp[..., :hd]), hm(pp[..., hd:2 * hd]), hm(pp[..., 2 * hd:]), lp)
    ps = p[1:]
    ys = na_attention(hm(ps[..., :hd]), hm(ps[..., hd:2 * hd]), hm(ps[..., 2 * hd:]),
                      hm(cache_k.reshape(bs, -1, hd)), hm(cache_v.reshape(bs, -1, hd)), na_bias_table(rpb))
    x3 = outproj_ln("plain", _stream(_token_major(yp), _token_major(ys)), x3, mod3, 2, w_out.astype(BF16), ln_g, ln_b)
    return (x3, pp[..., hd:2 * hd].reshape(bp, lp, NA_HEADS, NA_HD), pp[..., 2 * hd:].reshape(bp, lp, NA_HEADS, NA_HD))


def _rope_rotated_cols(w):
    q = MLA_ROPE // 4
    return jnp.concatenate([-w[..., q:2 * q], w[..., :q], -w[..., 3 * q:], w[..., 2 * q:3 * q]], axis=-1)


def _rope_tables(ts):
    ra = MLA_ROPE // 2
    t = np.arange(ts)
    inv = 1.0 / (ROPE_BASE ** (np.arange(0, ra, 2, dtype=np.float32) / ra))
    ang_r = (t // GRID_W).astype(np.float32)[:, None] * inv[None, :]
    ang_c = (t % GRID_W).astype(np.float32)[:, None] * inv[None, :]
    ang = np.concatenate([ang_r, ang_r, ang_c, ang_c], axis=-1).astype(np.float32)
    return jnp.cos(jnp.asarray(ang)), jnp.sin(jnp.asarray(ang))


def mla_layer(x3, mod3, bp, lp, cache_ckv, cache_kpe, w_in, q_norm, w_qup, kv_norm, w_kvup, w_out, ln_g, ln_b):
    nseg, seg, _ = x3.shape
    bs = nseg - 1
    nq = MLA_Q_LORA + MLA_KV_LORA
    w_ext = jnp.concatenate([w_in, _rope_rotated_cols(w_in[:, nq:])], axis=1)
    p = mod_matmul(x3, mod3, 0, _pad_cols(w_ext, 896).astype(BF16))
    cos_t, sin_t = _rope_tables(seg)
    cos3 = jnp.concatenate([jnp.ones((1, seg, MLA_ROPE), F32), jnp.broadcast_to(cos_t, (bs, seg, MLA_ROPE))], 0)
    sin3 = jnp.concatenate([jnp.zeros((1, seg, MLA_ROPE), F32), jnp.broadcast_to(sin_t, (bs, seg, MLA_ROPE))], 0)
    wq = w_qup.reshape(MLA_Q_LORA, MLA_HEADS, MLA_NOPE + MLA_ROPE)
    wq_rope = wq[:, :, MLA_NOPE:]
    w_q3 = jnp.concatenate([wq[:, :, :MLA_NOPE].reshape(MLA_Q_LORA, -1), wq_rope.reshape(MLA_Q_LORA, -1),
                            _rope_rotated_cols(wq_rope).reshape(MLA_Q_LORA, -1)], axis=1).astype(BF16)
    q_all = mla_q(p, q_norm, w_q3, jnp.tile(cos3, (1, 1, MLA_HEADS)), jnp.tile(sin3, (1, 1, MLA_HEADS)))
    wkv = w_kvup.reshape(MLA_KV_LORA, MLA_HEADS, MLA_NOPE + MLA_VD)
    w_kv2 = jnp.concatenate([wkv[:, :, :MLA_NOPE].reshape(MLA_KV_LORA, -1),
                             wkv[:, :, MLA_NOPE:].reshape(MLA_KV_LORA, -1)], axis=1).astype(BF16)
    ckvn, kpe, kv = mla_kv(p, kv_norm, w_kv2, cos3, sin3)
    kvc = matmul(cache_ckv.reshape(-1, MLA_KV_LORA), w_kv2, 512).reshape(bs, -1, w_kv2.shape[1])
    nn = MLA_HEADS * MLA_NOPE

    def heads(q_rows, kv_rows, kpe_rows):
        b, t, _ = q_rows.shape
        tk = kv_rows.shape[1]
        qh = jnp.concatenate([q_rows[..., :nn].reshape(b, t, MLA_HEADS, MLA_NOPE),
                              q_rows[..., nn:].reshape(b, t, MLA_HEADS, MLA_ROPE)], -1)
        kh = jnp.concatenate([kv_rows[..., :nn].reshape(b, tk, MLA_HEADS, MLA_NOPE),
                              jnp.broadcast_to(kpe_rows[:, :, None, :], (b, tk, MLA_HEADS, MLA_ROPE))], -1)
        vh = kv_rows[..., nn:].reshape(b, tk, MLA_HEADS, MLA_VD)
        tr = lambda a: jnp.transpose(a, (0, 2, 1, 3)).astype(BF16)
        return tr(qh), tr(kh), tr(vh)

    yp = attention(*heads(q_all[0].reshape(bp, lp, -1), kv[0].reshape(bp, lp, -1), kpe[0].reshape(bp, lp, -1)), lp)
    ys = attention(*heads(q_all[1:], jnp.concatenate([kv[1:], kvc], 1), jnp.concatenate([kpe[1:], cache_kpe], 1)), 256)
    x3 = outproj_ln("plain", _stream(_token_major(yp), _token_major(ys)), x3, mod3, 2, w_out.astype(BF16), ln_g, ln_b)
    return x3, ckvn[0].reshape(bp, lp, MLA_KV_LORA), kpe[0].reshape(bp, lp, MLA_ROPE)


def kernel(x_prompt, x_sample, c, c_ctx, state_mlstm_C, state_mlstm_n, state_mlstm_m, state_gla_S, cache_na_k, cache_na_v, cache_mla_ckv, cache_mla_kpe, ada_w, ada_b, ln_mix_g, ln_mix_b, ln_ffn_g, ln_ffn_b, mlstm_w_in, mlstm_b_gate, mlstm_norm_w, mlstm_w_out, gla_w_in, gla_w_gate2, gla_b_gate2, gla_norm_w, gla_w_out, na_w_in, na_rpb, na_w_out, mla_w_in, mla_q_norm, mla_w_qup, mla_kv_norm, mla_w_kvup, mla_w_out, peer_w_q, peer_subkeys, peer_u, peer_v):
    bp, lp, d = x_prompt.shape
    bs, ts, _ = x_sample.shape
    assert bp * lp == ts and bs + 1 <= 8
    x3 = _stream(x_prompt, x_sample)
    cond8 = jnp.zeros((8, d), F32).at[0].set(c_ctx).at[1:1 + bs].set(c)
    mods = adaln_all(cond8, ada_w, ada_b)
    u_all = peer_u.astype(BF16)
    vt_all = jnp.swapaxes(peer_v.reshape(DEPTH, -1, PEER_CE, d), 2, 3).astype(BF16)
    outs = {}
    for l in range(DEPTH):
        mod3 = mods[l].reshape(8, 1, ADA_CHUNKS * d)
        kind = l % 4
        if kind == 0:
            x3, outs["C"], outs["n"], outs["m"] = mlstm_layer(
                x3, mod3, bp, lp, state_mlstm_C, state_mlstm_n, state_mlstm_m, mlstm_w_in, mlstm_b_gate,
                mlstm_norm_w, mlstm_w_out, ln_mix_g[l], ln_mix_b[l])
        elif kind == 1:
            x3, outs["S"] = gla_layer(x3, mod3, bp, lp, state_gla_S, gla_w_in, gla_w_gate2, gla_b_gate2,
                                      gla_norm_w, gla_w_out, ln_mix_g[l], ln_mix_b[l])
        elif kind == 2:
            x3, outs["nk"], outs["nv"] = na_layer(x3, mod3, bp, lp, cache_na_k, cache_na_v, na_w_in, na_rpb,
                                                  na_w_out, ln_mix_g[l], ln_mix_b[l])
        else:
            x3, outs["ckv"], outs["kpe"] = mla_layer(x3, mod3, bp, lp, cache_mla_ckv, cache_mla_kpe, mla_w_in,
                                                     mla_q_norm, mla_w_qup, mla_kv_norm, mla_w_kvup, mla_w_out,
                                                     ln_mix_g[l], ln_mix_b[l])
        x3 = peer_layer(x3, mod3, l, peer_w_q[l], peer_subkeys[l], u_all, vt_all, ln_ffn_g[l], ln_ffn_b[l])
    return (x3[0].reshape(bp, lp, d), x3[1:], outs["C"], outs["n"], outs["m"], outs["S"], outs["nk"], outs["nv"],
            outs["ckv"], outs["kpe"])
```

```python
import functools

import numpy as np
import jax
import jax.numpy as jnp
from jax import lax
from jax.experimental import pallas as pl
from jax.experimental.pallas import tpu as pltpu

D_MODEL = 1024
DEPTH = 4
GRID_W = 64
DEEPNORM_ALPHA = (2.0 * DEPTH) ** 0.25
ADA_CHUNKS = 6
NORM_EPS = 1e-5

M_HEADS, M_DK, M_DV = 4, 128, 256
G_HEADS, G_DK, G_DV = 4, 128, 256
K_OFF = M_HEADS * M_DK
V_OFF = 2 * M_HEADS * M_DK
QKV_COLS = V_OFF + M_HEADS * M_DV
HID_COLS = M_HEADS * M_DV
GATE_OFF = QKV_COLS + HID_COLS
G_GATE_RANK = 16
G_GATE_NORM = 16.0
NA_HEADS, NA_HD, NA_ROWS, NA_COLS = 16, 64, 8, 16
MLA_HEADS, MLA_Q_LORA, MLA_KV_LORA, MLA_NOPE, MLA_ROPE, MLA_VD = 16, 512, 256, 64, 32, 64
ROPE_BASE = 10000.0
PEER_HEADS, PEER_NKEYS, PEER_HALF, PEER_TOPK = 8, 128, 128, 16

V7X_VMEM_LIMIT = 56 * 1024 * 1024
F32 = jnp.float32
BF16 = jnp.bfloat16
NEG_INF = float("-inf")


def _params(sem, vmem=V7X_VMEM_LIMIT):
    return pltpu.CompilerParams(dimension_semantics=sem, vmem_limit_bytes=vmem)


def _dot(a, b, dims=((1,), (0,))):
    return lax.dot_general(a, b, (dims, ((), ())), preferred_element_type=F32)


def _split3(a):
    hi = a.astype(BF16)
    r1 = a - hi.astype(F32)
    mid = r1.astype(BF16)
    lo = (r1 - mid.astype(F32)).astype(BF16)
    return hi, mid, lo


def _dot_exact_lhs(m01, a):
    hi, mid, lo = _split3(a)
    return _dot(m01, hi) + _dot(m01, mid) + _dot(m01, lo)


def _dot_exact_rhs(a, m01):
    hi, mid, lo = _split3(a)
    return _dot(hi, m01) + _dot(mid, m01) + _dot(lo, m01)


def _log_sigmoid(x):
    return jnp.minimum(x, 0.0) - jnp.log(1.0 + jnp.exp(-jnp.abs(x)))


def _sigmoid(x):
    return 1.0 / (1.0 + jnp.exp(-x))


def _adaln_kernel(c_ref, w_ref, b_ref, o_ref):
    cv = c_ref[...]
    a = cv * _sigmoid(cv)
    o_ref[0] = lax.dot_general(a, w_ref[0], (((1,), (0,)), ((), ())), precision=lax.Precision.HIGHEST,
                               preferred_element_type=F32) + b_ref[0]


def adaln_all(cond8, ada_w, ada_b):
    tn = 1024
    n = ada_w.shape[-1]
    return pl.pallas_call(
        _adaln_kernel,
        grid=(DEPTH, n // tn),
        in_specs=[pl.BlockSpec((8, D_MODEL), lambda l, j: (0, 0)),
                  pl.BlockSpec((1, D_MODEL, tn), lambda l, j: (l, 0, j)),
                  pl.BlockSpec((1, 1, tn), lambda l, j: (l, 0, j))],
        out_specs=pl.BlockSpec((1, 8, tn), lambda l, j: (l, 0, j)),
        out_shape=jax.ShapeDtypeStruct((DEPTH, 8, n), F32),
        compiler_params=_params(("arbitrary", "arbitrary")),
        name="adaln",
    )(cond8, ada_w, ada_b.reshape(DEPTH, 1, n))


def _modmm_kernel(x_ref, sh_ref, sc_ref, w_ref, o_ref, xm_ref):
    @pl.when(pl.program_id(2) == 0)
    def _():
        xm_ref[...] = (x_ref[0] * (1.0 + sc_ref[0]) + sh_ref[0]).astype(BF16)

    o_ref[0] = _dot(xm_ref[...], w_ref[...]).astype(o_ref.dtype)


def mod_matmul(x3, mod3, shift_chunk, w_bf16, tm=512, tn=None, out_dtype=F32):
    nseg, seg, d = x3.shape
    n = w_bf16.shape[1]
    tn = n if tn is None else tn
    return pl.pallas_call(
        _modmm_kernel,
        grid=(nseg, seg // tm, n // tn),
        in_specs=[pl.BlockSpec((1, tm, d), lambda s, i, j: (s, i, 0)),
                  pl.BlockSpec((1, 1, d), lambda s, i, j: (s, 0, shift_chunk)),
                  pl.BlockSpec((1, 1, d), lambda s, i, j: (s, 0, shift_chunk + 1)),
                  pl.BlockSpec((d, tn), lambda s, i, j: (0, j))],
        out_specs=pl.BlockSpec((1, tm, tn), lambda s, i, j: (s, i, j)),
        out_shape=jax.ShapeDtypeStruct((nseg, seg, n), out_dtype),
        scratch_shapes=[pltpu.VMEM((tm, d), BF16)],
        compiler_params=_params(("arbitrary", "arbitrary", "arbitrary")),
        name="mod_matmul",
    )(x3, mod3, mod3, w_bf16)


def _layer_norm_rows(y, g, b):
    mu = jnp.mean(y, axis=-1, keepdims=True)
    yc = y - mu
    var = jnp.mean(yc * yc, axis=-1, keepdims=True)
    return yc * lax.rsqrt(var + NORM_EPS) * g + b


def _outproj_kernel(*refs, mode):
    if mode == "plain":
        y_ref, x_ref, gate_ref, w_ref, g_ref, b_ref, o_ref = refs
        yin = y_ref[0].astype(BF16)
    else:
        ya_ref, yb_ref, og_ref, nw_ref, x_ref, gate_ref, w_ref, g_ref, b_ref, o_ref = refs
        hs = ya_ref[0] + yb_ref[0]
        og = og_ref[0]
        parts = []
        for h in range(4):
            seg = hs[:, h * 256:(h + 1) * 256]
            nw = nw_ref[:, h * 256:(h + 1) * 256]
            if mode == "mlstm":
                mu = jnp.mean(seg, axis=-1, keepdims=True)
                sc = seg - mu
                var = jnp.mean(sc * sc, axis=-1, keepdims=True)
                parts.append(sc * lax.rsqrt(var + NORM_EPS) * nw)
            else:
                ms = jnp.mean(seg * seg, axis=-1, keepdims=True)
                parts.append(seg * lax.rsqrt(ms + NORM_EPS) * nw)
        hn = jnp.concatenate(parts, axis=-1)
        act = _sigmoid(og) if mode == "mlstm" else og * _sigmoid(og)
        yin = (act * hn).astype(BF16)
    y = _dot(yin, w_ref[...])
    z = DEEPNORM_ALPHA * x_ref[0] + gate_ref[0] * y
    o_ref[0] = _layer_norm_rows(z, g_ref[...], b_ref[...])


def outproj_ln(mode, ys, x3, mod3, gate_chunk, w_bf16, ln_g, ln_b, norm_w=None, og=None, og_col=0, tm=512):
    nseg, seg, d = x3.shape
    k = w_bf16.shape[0]
    tok = lambda s, i: (s, i, 0)
    if mode == "plain":
        args = [ys]
        specs = [pl.BlockSpec((1, tm, k), tok)]
    else:
        args = [ys[0], ys[1], og, norm_w.reshape(1, k)]
        specs = [pl.BlockSpec((1, tm, k), tok), pl.BlockSpec((1, tm, k), tok),
                 pl.BlockSpec((1, tm, k), lambda s, i: (s, i, og_col)),
                 pl.BlockSpec((1, k), lambda s, i: (0, 0))]
    args += [x3, mod3, w_bf16, ln_g.reshape(1, d), ln_b.reshape(1, d)]
    specs += [pl.BlockSpec((1, tm, d), tok),
              pl.BlockSpec((1, 1, d), lambda s, i: (s, 0, gate_chunk)),
              pl.BlockSpec((k, d), lambda s, i: (0, 0)),
              pl.BlockSpec((1, d), lambda s, i: (0, 0)),
              pl.BlockSpec((1, d), lambda s, i: (0, 0))]
    return pl.pallas_call(
        functools.partial(_outproj_kernel, mode=mode),
        grid=(nseg, seg // tm),
        in_specs=specs,
        out_specs=pl.BlockSpec((1, tm, d), tok),
        out_shape=jax.ShapeDtypeStruct((nseg, seg, d), F32),
        compiler_params=_params(("arbitrary", "arbitrary")),
        name="outproj_ln_" + mode,
    )(*args)


def _tri(n, lower):
    r = lax.broadcasted_iota(jnp.int32, (n, n), 0)
    c = lax.broadcasted_iota(jnp.int32, (n, n), 1)
    return (c <= r) if lower else (c >= r)


def _mlstm_kernel(rowf_ref, rowb_ref, seq_ref, first_ref, last_ref,
                  pf_ref, pb_ref, gf_ref, gb_ref, gtf_ref, gtb_ref, bias_ref, biast_ref,
                  c0_ref, n0_ref, m0_ref, hf_ref, hb_ref, co_ref, no_ref, mo_ref,
                  c_s, n_s, m_s, *, L):
    item = pl.program_id(0)

    @pl.when(first_ref[item] == 1)
    def _():
        c_s[...] = c0_ref[0]
        n_s[...] = n0_ref[0]
        m_s[...] = m0_ref[0]

    dirs = [(pf_ref, gf_ref, gtf_ref, hf_ref), (pb_ref, gb_ref, gtb_ref, hb_ref)]
    masks = [_tri(L, lower=True), _tri(L, lower=False)]
    gates = []
    for d in range(2):
        g = dirs[d][1][0] + bias_ref[...]
        gt = dirs[d][2][0] + biast_ref[...]
        lf_c = _log_sigmoid(g[:, d * 8 + 4:d * 8 + 8])
        lf_r = _log_sigmoid(gt[d * 8 + 4:d * 8 + 8, :])
        b_c = _dot_exact_lhs(masks[d].astype(BF16), lf_c)
        b_r = _dot_exact_rhs(lf_r, masks[1 - d].astype(BF16))
        gates.append((g[:, d * 8:d * 8 + 4], gt[d * 8:d * 8 + 4, :], b_c, b_r))
    units = [(d, h) for d in range(2) for h in range(M_HEADS)]

    def unit_inputs(d, h):
        p_ref = dirs[d][0]
        q = p_ref[0, :, h * M_DK:(h + 1) * M_DK]
        k = p_ref[0, :, K_OFF + h * M_DK:K_OFF + (h + 1) * M_DK] * (M_DK ** -0.5)
        v = p_ref[0, :, V_OFF + h * M_DV:V_OFF + (h + 1) * M_DV].astype(BF16)
        li_c, li_r, b_c, b_r = gates[d]
        return q, k, v, li_c[:, h:h + 1], li_r[h:h + 1, :], b_c[:, h:h + 1], b_r[h:h + 1, :]

    qks = []
    for d, h in units:
        q, k, _, _, _, _, _ = unit_inputs(d, h)
        qks.append(_dot(q.astype(BF16), k.astype(BF16), ((1,), (1,))))
    smats, eis, mts = [], [], []
    for u, (d, h) in enumerate(units):
        _, _, _, _, lir, bc, br = unit_inputs(d, h)
        m_prev = m_s[u:u + 1, 0:1]
        dmat = jnp.where(masks[d], bc - br + lir, NEG_INF)
        inter = bc + m_prev
        mt = jnp.maximum(inter, jnp.max(dmat, axis=-1, keepdims=True))
        smats.append(qks[u] * jnp.exp(dmat - mt))
        eis.append(jnp.exp(inter - mt))
        mts.append(mt)
    nums = []
    for u, (d, h) in enumerate(units):
        q, _, v, _, _, _, _ = unit_inputs(d, h)
        nums.append(_dot(smats[u].astype(BF16), v) + eis[u] * _dot(q.astype(BF16), c_s[u].astype(BF16)))
    for u, (d, h) in enumerate(units):
        q, _, _, _, _, _, _ = unit_inputs(d, h)
        den = (jnp.sum(smats[u], axis=-1, keepdims=True)
               + eis[u] * jnp.sum(q * n_s[u:u + 1, :], axis=-1, keepdims=True))
        dirs[d][3][0, :, h * M_DV:(h + 1) * M_DV] = nums[u] / jnp.maximum(jnp.abs(den), jnp.exp(-mts[u]))
    for u, (d, h) in enumerate(units):
        _, k, v, lic, lir, bc, br = unit_inputs(d, h)
        last = L - 1 if d == 0 else 0
        m_prev = m_s[u:u + 1, 0:1]
        tot = br[:, last:last + 1]
        g_c = tot - bc + lic
        g_r = tot - br + lir
        m_new = jnp.maximum(tot + m_prev, jnp.max(g_r, axis=-1, keepdims=True))
        kw = k * jnp.exp(g_c - m_new)
        dec = jnp.exp(tot + m_prev - m_new)
        c_s[u] = dec * c_s[u] + _dot(kw.astype(BF16), v, ((0,), (0,)))
        n_s[u:u + 1, :] = dec * n_s[u:u + 1, :] + jnp.sum(kw, axis=0, keepdims=True)
        m_s[u:u + 1, :] = jnp.broadcast_to(m_new, (1, 128))

    @pl.when(last_ref[item] == 1)
    def _():
        co_ref[0] = c_s[...]
        no_ref[0] = n_s[...]
        mo_ref[0] = m_s[...]


def _scan_items(seq_lens, L):
    rowf, rowb, seq, first, last = [], [], [], [], []
    base = 0
    for s, t in enumerate(seq_lens):
        nc = t // L
        for c in range(nc):
            rowf.append(base + c)
            rowb.append(base + nc - 1 - c)
            seq.append(s)
            first.append(int(c == 0))
            last.append(int(c == nc - 1))
        base += nc
    return [jnp.asarray(np.array(a, np.int32)) for a in (rowf, rowb, seq, first, last)]


def mlstm_scan(p, seq_lens, g, bias, c0, n0, m0, L):
    items = _scan_items(seq_lens, L)
    nseq = len(seq_lens)
    rows = p.shape[0]
    hshape = jax.ShapeDtypeStruct((rows, L, M_HEADS * M_DV), F32)
    fwd = lambda i, rf, rb, sq, fs, ls: (rf[i], 0, 0)
    bwd = lambda i, rf, rb, sq, fs, ls: (rb[i], 0, 0)
    st4 = lambda i, rf, rb, sq, fs, ls: (sq[i], 0, 0, 0)
    st3 = lambda i, rf, rb, sq, fs, ls: (sq[i], 0, 0)
    fix = lambda i, rf, rb, sq, fs, ls: (0, 0)
    gt = jnp.swapaxes(g, 1, 2)
    return pl.pallas_call(
        functools.partial(_mlstm_kernel, L=L),
        grid_spec=pltpu.PrefetchScalarGridSpec(
            num_scalar_prefetch=5,
            grid=(items[0].shape[0],),
            in_specs=[pl.BlockSpec((1, L, QKV_COLS), fwd), pl.BlockSpec((1, L, QKV_COLS), bwd),
                      pl.BlockSpec((1, L, 16), fwd), pl.BlockSpec((1, L, 16), bwd),
                      pl.BlockSpec((1, 16, L), fwd), pl.BlockSpec((1, 16, L), bwd),
                      pl.BlockSpec((1, 16), fix), pl.BlockSpec((16, 1), fix),
                      pl.BlockSpec((1, 8, M_DK, M_DV), st4),
                      pl.BlockSpec((1, 8, M_DK), st3),
                      pl.BlockSpec((1, 8, M_DK), st3)],
            out_specs=[pl.BlockSpec((1, L, HID_COLS), fwd), pl.BlockSpec((1, L, HID_COLS), bwd),
                       pl.BlockSpec((1, 8, M_DK, M_DV), st4),
                       pl.BlockSpec((1, 8, M_DK), st3),
                       pl.BlockSpec((1, 8, M_DK), st3)],
            scratch_shapes=[pltpu.VMEM((8, M_DK, M_DV), F32), pltpu.VMEM((8, M_DK), F32),
                            pltpu.VMEM((8, M_DK), F32)]),
        out_shape=[hshape, hshape,
                   jax.ShapeDtypeStruct((nseq, 8, M_DK, M_DV), F32),
                   jax.ShapeDtypeStruct((nseq, 8, M_DK), F32),
                   jax.ShapeDtypeStruct((nseq, 8, M_DK), F32)],
        compiler_params=_params(("arbitrary",)),
        name="mlstm_scan",
    )(*items, p, p, g, g, gt, gt, bias.reshape(1, 16), bias.reshape(16, 1), c0, n0, m0)


def _gla_kernel(rowf_ref, rowb_ref, seq_ref, first_ref, last_ref,
                pf_ref, pb_ref, gf_ref, gb_ref, w2_ref, b2_ref, s0_ref, of_ref, ob_ref, so_ref, s_s, *, L):
    item = pl.program_id(0)

    @pl.when(first_ref[item] == 1)
    def _():
        s_s[...] = s0_ref[0]

    dirs = [(pf_ref, gf_ref, of_ref), (pb_ref, gb_ref, ob_ref)]
    masks = [_tri(L, lower=True), _tri(L, lower=False)]
    bcs = []
    for d in range(2):
        gr = dirs[d][1][0][:, d * G_GATE_RANK:(d + 1) * G_GATE_RANK]
        pre = lax.dot_general(gr, w2_ref[d], (((1,), (0,)), ((), ())), precision=lax.Precision.HIGHEST,
                              preferred_element_type=F32) + b2_ref[d]
        la = _log_sigmoid(pre) * (1.0 / G_GATE_NORM)
        bcs.append(_dot_exact_lhs(masks[d].astype(BF16), la))
    units = [(d, h) for d in range(2) for h in range(G_HEADS)]

    def unit_inputs(d, h):
        p_ref = dirs[d][0]
        q = p_ref[0, :, h * G_DK:(h + 1) * G_DK] * (G_DK ** -0.5)
        k = p_ref[0, :, K_OFF + h * G_DK:K_OFF + (h + 1) * G_DK]
        v = p_ref[0, :, V_OFF + h * G_DV:V_OFF + (h + 1) * G_DV].astype(BF16)
        return q, k, v, bcs[d][:, h * G_DK:(h + 1) * G_DK]

    qds, a_mats = [], []
    for d, h in units:
        q, k, _, bc = unit_inputs(d, h)
        qd = (q * jnp.exp(bc)).astype(BF16)
        kd = (k * jnp.exp(-bc)).astype(BF16)
        qds.append(qd)
        a_mats.append(jnp.where(masks[d], _dot(qd, kd, ((1,), (1,))), 0.0).astype(BF16))
    for u, (d, h) in enumerate(units):
        _, _, v, _ = unit_inputs(d, h)
        dirs[d][2][0, :, h * G_DV:(h + 1) * G_DV] = (_dot(a_mats[u], v)
                                                     + _dot(qds[u], s_s[u].astype(BF16), ((1,), (1,))))
    for u, (d, h) in enumerate(units):
        _, k, v, bc = unit_inputs(d, h)
        last = L - 1 if d == 0 else 0
        bl = bc[last:last + 1, :]
        kl = (k * jnp.exp(bl - bc)).astype(BF16)
        s_s[u] = s_s[u] * jnp.exp(bl) + _dot(v, kl, ((0,), (0,)))

    @pl.when(last_ref[item] == 1)
    def _():
        so_ref[0] = s_s[...]


def gla_scan(p, seq_lens, gr, w2, b2, s0t, L):
    items = _scan_items(seq_lens, L)
    nseq = len(seq_lens)
    oshape = jax.ShapeDtypeStruct((p.shape[0], L, G_HEADS * G_DV), F32)
    fwd = lambda i, rf, rb, sq, fs, ls: (rf[i], 0, 0)
    bwd = lambda i, rf, rb, sq, fs, ls: (rb[i], 0, 0)
    st4 = lambda i, rf, rb, sq, fs, ls: (sq[i], 0, 0, 0)
    fix3 = lambda i, rf, rb, sq, fs, ls: (0, 0, 0)
    return pl.pallas_call(
        functools.partial(_gla_kernel, L=L),
        grid_spec=pltpu.PrefetchScalarGridSpec(
            num_scalar_prefetch=5,
            grid=(items[0].shape[0],),
            in_specs=[pl.BlockSpec((1, L, QKV_COLS), fwd), pl.BlockSpec((1, L, QKV_COLS), bwd),
                      pl.BlockSpec((1, L, 32), fwd), pl.BlockSpec((1, L, 32), bwd),
                      pl.BlockSpec((2, G_GATE_RANK, 512), fix3),
                      pl.BlockSpec((2, 1, 512), fix3),
                      pl.BlockSpec((1, 8, G_DV, G_DK), st4)],
            out_specs=[pl.BlockSpec((1, L, HID_COLS), fwd), pl.BlockSpec((1, L, HID_COLS), bwd),
                       pl.BlockSpec((1, 8, G_DV, G_DK), st4)],
            scratch_shapes=[pltpu.VMEM((8, G_DV, G_DK), F32)]),
        out_shape=[oshape, oshape, jax.ShapeDtypeStruct((nseq, 8, G_DV, G_DK), F32)],
        compiler_params=_params(("arbitrary",)),
        name="gla_scan",
    )(*items, p, p, gr, gr, w2, b2.reshape(2, 1, 512), s0t)


ATTN_HEADS_PER_STEP = 2


def _attn_kernel(q_ref, k_ref, v_ref, o_ref, *, scale):
    scores = [_dot(q_ref[0, j], k_ref[0, j], ((1,), (1,))) * scale for j in range(ATTN_HEADS_PER_STEP)]
    for j, s in enumerate(scores):
        m = jnp.max(s, axis=-1, keepdims=True)
        p = jnp.exp(s - m)
        l = jnp.sum(p, axis=-1, keepdims=True)
        o_ref[0, j] = _dot(p.astype(BF16), v_ref[0, j]) / l


def attention(q, k, v, tq):
    b, h, lq, dq = q.shape
    lk, dv = k.shape[2], v.shape[3]
    hb = ATTN_HEADS_PER_STEP
    return pl.pallas_call(
        functools.partial(_attn_kernel, scale=dq ** -0.5),
        grid=(b, h // hb, lq // tq),
        in_specs=[pl.BlockSpec((1, hb, tq, dq), lambda b, h, i: (b, h, i, 0)),
                  pl.BlockSpec((1, hb, lk, dq), lambda b, h, i: (b, h, 0, 0)),
                  pl.BlockSpec((1, hb, lk, dv), lambda b, h, i: (b, h, 0, 0))],
        out_specs=pl.BlockSpec((1, hb, tq, dv), lambda b, h, i: (b, h, i, 0)),
        out_shape=jax.ShapeDtypeStruct((b, h, lq, dv), F32),
        compiler_params=_params(("arbitrary", "arbitrary", "arbitrary")),
        name="attention",
    )(q, k, v)


NA_RB = 16


def _na_kernel(q_ref, k_ref, v_ref, kc_ref, vc_ref, bias_ref, o_ref, *, rows):
    j = pl.program_id(2)
    scale = NA_HD ** -0.5
    s_ctx_all = _dot(q_ref[0, 0], kc_ref[0, 0], ((1,), (1,))) * scale
    offs, s_locs = [], []
    for a in range(NA_RB):
        r = j * NA_RB + a
        start = jnp.clip(r - NA_ROWS // 2, 0, rows - NA_ROWS)
        dr0 = start - r + (NA_ROWS - 1)
        offs.append(pl.multiple_of(start * GRID_W, GRID_W))
        qa = q_ref[0, 0, a * GRID_W:(a + 1) * GRID_W, :]
        kl = k_ref[0, 0, pl.ds(offs[a], NA_ROWS * GRID_W), :]
        s_locs.append(_dot(qa, kl, ((1,), (1,))) * scale + bias_ref[0, dr0])
    p_locs, p_ctxs, ls = [], [], []
    for a in range(NA_RB):
        s_ctx = s_ctx_all[a * GRID_W:(a + 1) * GRID_W, :]
        m = jnp.maximum(jnp.max(s_locs[a], axis=-1, keepdims=True), jnp.max(s_ctx, axis=-1, keepdims=True))
        p_loc = jnp.exp(s_locs[a] - m)
        p_ctx = jnp.exp(s_ctx - m)
        ls.append(jnp.sum(p_loc, axis=-1, keepdims=True) + jnp.sum(p_ctx, axis=-1, keepdims=True))
        p_locs.append(p_loc.astype(BF16))
        p_ctxs.append(p_ctx.astype(BF16))
    o_ctx_all = _dot(jnp.concatenate(p_ctxs, axis=0), vc_ref[0, 0])
    for a in range(NA_RB):
        vl = v_ref[0, 0, pl.ds(offs[a], NA_ROWS * GRID_W), :]
        o = _dot(p_locs[a], vl) + o_ctx_all[a * GRID_W:(a + 1) * GRID_W, :]
        o_ref[0, 0, a * GRID_W:(a + 1) * GRID_W, :] = o / ls[a]


def na_bias_table(rpb):
    cq = np.arange(GRID_W)[:, None]
    ck = np.arange(GRID_W)[None, :]
    cs = np.clip(cq - NA_COLS // 2, 0, GRID_W - NA_COLS)
    ok = (ck >= cs) & (ck < cs + NA_COLS)
    dc = np.clip(ck - cq, -(NA_COLS - 1), NA_COLS - 1) + (NA_COLS - 1)
    t = jnp.where(ok[None, None], rpb.astype(F32)[:, :, dc], NEG_INF)
    rows = np.arange(NA_ROWS)[:, None] + np.arange(NA_ROWS)[None, :]
    tf = t[:, rows]
    return jnp.transpose(tf, (0, 1, 3, 2, 4)).reshape(NA_HEADS, NA_ROWS, GRID_W, NA_ROWS * GRID_W)


def na_attention(q, k, v, kc, vc, bias):
    b, h, t, dh = q.shape
    lc = kc.shape[2]
    rows = t // GRID_W
    full = lambda b, h, j: (b, h, 0, 0)
    return pl.pallas_call(
        functools.partial(_na_kernel, rows=rows),
        grid=(b, h, rows // NA_RB),
        in_specs=[pl.BlockSpec((1, 1, NA_RB * GRID_W, dh), lambda b, h, j: (b, h, j, 0)),
                  pl.BlockSpec((1, 1, t, dh), full), pl.BlockSpec((1, 1, t, dh), full),
                  pl.BlockSpec((1, 1, lc, dh), full), pl.BlockSpec((1, 1, lc, dh), full),
                  pl.BlockSpec((1, NA_ROWS, GRID_W, NA_ROWS * GRID_W), lambda b, h, j: (h, 0, 0, 0))],
        out_specs=pl.BlockSpec((1, 1, NA_RB * GRID_W, dh), lambda b, h, j: (b, h, j, 0)),
        out_shape=jax.ShapeDtypeStruct((b, h, t, dh), F32),
        compiler_params=_params(("arbitrary", "arbitrary", "arbitrary")),
        name="na_attention",
    )(q, k, v, kc, vc, bias)


def _rms_rows(x, g):
    return x * lax.rsqrt(jnp.mean(x * x, axis=-1, keepdims=True) + NORM_EPS) * g


def _mla_q_kernel(cq_ref, g_ref, w_ref, cos_ref, sin_ref, o_ref):
    r = _dot(_rms_rows(cq_ref[0], g_ref[...]).astype(BF16), w_ref[...])
    nn = MLA_HEADS * MLA_NOPE
    nr = MLA_HEADS * MLA_ROPE
    o_ref[0, :, :nn] = r[:, :nn]
    o_ref[0, :, nn:] = r[:, nn:nn + nr] * cos_ref[0] + r[:, nn + nr:] * sin_ref[0]


def mla_q(p, q_norm, w_q3, cos_q, sin_q, tm=512):
    nseg, seg, _ = p.shape
    nout = MLA_HEADS * (MLA_NOPE + MLA_ROPE)
    nr = MLA_HEADS * MLA_ROPE
    tok = lambda s, i: (s, i, 0)
    return pl.pallas_call(
        _mla_q_kernel,
        grid=(nseg, seg // tm),
        in_specs=[pl.BlockSpec((1, tm, MLA_Q_LORA), tok),
                  pl.BlockSpec((1, MLA_Q_LORA), lambda s, i: (0, 0)),
                  pl.BlockSpec(w_q3.shape, lambda s, i: (0, 0)),
                  pl.BlockSpec((1, tm, nr), tok), pl.BlockSpec((1, tm, nr), tok)],
        out_specs=pl.BlockSpec((1, tm, nout), tok),
        out_shape=jax.ShapeDtypeStruct((nseg, seg, nout), F32),
        compiler_params=_params(("arbitrary", "arbitrary")),
        name="mla_q",
    )(p, q_norm.reshape(1, -1), w_q3, cos_q, sin_q)


def _mla_kv_kernel(ckv_ref, kpe_ref, g_ref, w_ref, cos_ref, sin_ref, ckvn_ref, kpeo_ref, kv_ref):
    cn = _rms_rows(ckv_ref[0], g_ref[...])
    ckvn_ref[0] = cn
    kv_ref[0] = _dot(cn.astype(BF16), w_ref[...])
    kp = kpe_ref[0]
    kpeo_ref[0] = kp[:, :MLA_ROPE] * cos_ref[0] + kp[:, MLA_ROPE:2 * MLA_ROPE] * sin_ref[0]


def mla_kv(p, kv_norm, w_kv, cos_k, sin_k, tm=512):
    nseg, seg, _ = p.shape
    nkv = w_kv.shape[1]
    tok = lambda s, i: (s, i, 0)
    return pl.pallas_call(
        _mla_kv_kernel,
        grid=(nseg, seg // tm),
        in_specs=[pl.BlockSpec((1, tm, MLA_KV_LORA), lambda s, i: (s, i, MLA_Q_LORA // MLA_KV_LORA)),
                  pl.BlockSpec((1, tm, 128), lambda s, i: (s, i, (MLA_Q_LORA + MLA_KV_LORA) // 128)),
                  pl.BlockSpec((1, MLA_KV_LORA), lambda s, i: (0, 0)),
                  pl.BlockSpec(w_kv.shape, lambda s, i: (0, 0)),
                  pl.BlockSpec((1, tm, MLA_ROPE), tok), pl.BlockSpec((1, tm, MLA_ROPE), tok)],
        out_specs=[pl.BlockSpec((1, tm, MLA_KV_LORA), tok), pl.BlockSpec((1, tm, MLA_ROPE), tok),
                   pl.BlockSpec((1, tm, nkv), tok)],
        out_shape=[jax.ShapeDtypeStruct((nseg, seg, MLA_KV_LORA), F32),
                   jax.ShapeDtypeStruct((nseg, seg, MLA_ROPE), F32),
                   jax.ShapeDtypeStruct((nseg, seg, nkv), F32)],
        compiler_params=_params(("arbitrary", "arbitrary")),
        name="mla_kv",
    )(p, p, kv_norm.reshape(1, -1), w_kv, cos_k, sin_k)


def _mm_kernel(a_ref, w_ref, o_ref):
    o_ref[...] = _dot(a_ref[...].astype(BF16), w_ref[...])


def matmul(a, w_bf16, tm):
    m, k = a.shape
    n = w_bf16.shape[1]
    return pl.pallas_call(
        _mm_kernel,
        grid=(m // tm,),
        in_specs=[pl.BlockSpec((tm, k), lambda i: (i, 0)), pl.BlockSpec((k, n), lambda i: (0, 0))],
        out_specs=pl.BlockSpec((tm, n), lambda i: (i, 0)),
        out_shape=jax.ShapeDtypeStruct((m, n), F32),
        compiler_params=_params(("arbitrary",)),
        name="matmul",
    )(a, w_bf16)


PEER_RT = 128
NOT_TOP = 99.0
RANK_CODE = 2.0 ** 100


def _top16(s, exact):
    vals = []
    if exact:
        key = lax.broadcasted_iota(jnp.int32, s.shape, 0).astype(F32)
        rank = jnp.full(s.shape, NOT_TOP, F32)
        for r in range(PEER_TOPK):
            m = jnp.max(s, axis=0, keepdims=True)
            hit = key == jnp.min(jnp.where(s == m, key, 1e9), axis=0, keepdims=True)
            rank = jnp.where(hit, float(r), rank)
            s = jnp.where(hit, NEG_INF, s)
            vals.append(m)
        return vals, rank
    for r in range(PEER_TOPK):
        m = jnp.max(s, axis=0, keepdims=True)
        s = jnp.where(s == m, -RANK_CODE * (1.0 + r / 32.0), s)
        vals.append(m)
    return vals, jnp.where(s <= -0.5 * RANK_CODE, s * (-32.0 / RANK_CODE) - 32.0, NOT_TOP)


def _pair_topk(av, bv, exact):
    n = av[0].shape[-1]
    a_lo, a_hi = jnp.concatenate(av[:8], 0), jnp.concatenate(av[8:], 0)
    b_lo, b_hi = jnp.concatenate(bv[:8], 0), jnp.concatenate(bv[8:], 0)
    row = lax.broadcasted_iota(jnp.int32, (8, n), 0).astype(F32)

    no_pos = 1e8

    def rows_b(a, b_blk, boff, nvalid):
        ok = row < nvalid
        return jnp.where(ok, av[a] + b_blk, NEG_INF), jnp.where(ok, a * 16.0 + boff + row, no_pos)

    def rows_a(b, a_blk, aoff, lo, hi):
        ok = (row >= lo) & (row < hi)
        return jnp.where(ok, a_blk + bv[b], NEG_INF), jnp.where(ok, (aoff + row) * 16.0 + b, no_pos)

    groups = [rows_b(0, b_lo, 0, 8), rows_b(0, b_hi, 8, 8), rows_b(1, b_lo, 0, 8), rows_b(2, b_lo, 0, 5),
              rows_b(3, b_lo, 0, 4), rows_a(0, a_lo, 0, 4, 8), rows_a(0, a_hi, 8, 0, 8),
              rows_a(1, a_lo, 0, 4, 8), rows_a(2, a_lo, 0, 4, 5)]
    cands = [g[0] for g in groups]
    poss = [g[1] for g in groups]
    sels = [jnp.zeros((8, n), F32) for _ in groups]
    top = av[0] + bv[0]
    z = jnp.zeros((1, n), F32)
    for _ in range(PEER_TOPK):
        m = functools.reduce(jnp.maximum, cands)
        m = jnp.max(m, axis=0, keepdims=True)
        if exact:
            first = functools.reduce(jnp.minimum, [jnp.where(c == m, p, 1e9) for c, p in zip(cands, poss)])
            first = jnp.min(first, axis=0, keepdims=True)
            hits = [p == first for p in poss]
            cands = [jnp.where(hh, NEG_INF, c) for hh, c in zip(hits, cands)]
            sels = [jnp.where(hh, 1.0, s) for hh, s in zip(hits, sels)]
        else:
            cands = [jnp.where(c == m, -RANK_CODE, c) for c in cands]
        z = z + jnp.exp(m - top)
    if not exact:
        sels = [jnp.where(c == -RANK_CODE, 1.0, 0.0) for c in cands]
    cnt = lambda x: jnp.sum(x, axis=0, keepdims=True)
    cut_lo = sels[5] + sels[7] + sels[8]
    for a, c in enumerate([cnt(sels[0]) + cnt(sels[1]), cnt(sels[2]), cnt(sels[3]), cnt(sels[4])]):
        cut_lo = cut_lo + jnp.where(row == a, c, 0.0)
    return cut_lo, sels[6], z, cnt(cut_lo) + cnt(sels[6])


def _peer_route_kernel(x_ref, sh_ref, sc_ref, wq_ref, sk_ref, xm_ref, e1_ref, cut_ref, e2_ref, r2_ref, q_s, *, tm):
    xm = (x_ref[0] * (1.0 + sc_ref[0]) + sh_ref[0]).astype(BF16)
    xm_ref[0] = xm
    q = _dot(xm, wq_ref[...])
    for hp in range(2 * PEER_HEADS):
        q_s[hp] = q[:, hp * PEER_HALF:(hp + 1) * PEER_HALF]

    def route(h, tok, exact):
        def scores(hp):
            return lax.dot_general(sk_ref[hp], q_s[hp, tok, :], (((1,), (1,)), ((), ())),
                                   precision=lax.Precision.HIGHEST, preferred_element_type=F32)

        s1, s2 = scores(2 * h), scores(2 * h + 1)
        av, rank1 = _top16(s1, exact)
        bv, rank2 = _top16(s2, exact)
        cut_lo, cut_hi, z, nsel = _pair_topk(av, bv, exact)
        cut = jnp.zeros_like(s1)
        for r in range(PEER_TOPK):
            src = cut_lo if r < 8 else cut_hi
            cut = jnp.where(rank1 == float(r), src[r % 8:r % 8 + 1, :], cut)
        e1_ref[0, h, :, tok] = (jnp.exp(s1 - av[0]) / z).astype(BF16)
        cut_ref[0, h, :, tok] = cut.astype(BF16)
        e2_ref[0, h, :, tok] = jnp.exp(s2 - bv[0]).astype(BF16)
        r2_ref[0, h, :, tok] = rank2.astype(BF16)
        ranked = lambda rk: jnp.sum(jnp.where(rk < PEER_TOPK, 1.0, 0.0), axis=0, keepdims=True)
        return ranked(rank1), ranked(rank2), nsel

    def body(h, carry):
        toks = [pl.ds(t0, PEER_RT) for t0 in range(0, tm, PEER_RT)]
        counts = [route(h, tok, exact=False) for tok in toks]
        for tok, cnts in zip(toks, counts):
            bad = functools.reduce(jnp.maximum, [jnp.abs(cn - PEER_TOPK) for cn in cnts])

            @pl.when(jnp.max(bad) > 0.0)
            def _():
                route(h, tok, exact=True)
        return carry

    lax.fori_loop(0, PEER_HEADS, body, 0)


def peer_route(x3, mod3, shift_chunk, wq_bf16, subkeys, tm=512):
    nseg, seg, d = x3.shape
    tok = lambda s, i: (s, i, 0)
    rshape = jax.ShapeDtypeStruct((nseg, PEER_HEADS, PEER_NKEYS, seg), BF16)
    rspec = pl.BlockSpec((1, PEER_HEADS, PEER_NKEYS, tm), lambda s, i: (s, 0, 0, i))
    return pl.pallas_call(
        functools.partial(_peer_route_kernel, tm=tm),
        grid=(nseg, seg // tm),
        in_specs=[pl.BlockSpec((1, tm, d), tok),
                  pl.BlockSpec((1, 1, d), lambda s, i: (s, 0, shift_chunk)),
                  pl.BlockSpec((1, 1, d), lambda s, i: (s, 0, shift_chunk + 1)),
                  pl.BlockSpec(wq_bf16.shape, lambda s, i: (0, 0)),
                  pl.BlockSpec((2 * PEER_HEADS, PEER_NKEYS, PEER_HALF), lambda s, i: (0, 0, 0))],
        out_specs=[pl.BlockSpec((1, tm, d), tok), rspec, rspec, rspec, rspec],
        out_shape=[jax.ShapeDtypeStruct((nseg, seg, d), BF16)] + [rshape] * 4,
        scratch_shapes=[pltpu.VMEM((2 * PEER_HEADS, tm, PEER_HALF), F32)],
        compiler_params=_params(("arbitrary", "arbitrary")),
        name="peer_route",
    )(x3, mod3, mod3, wq_bf16, subkeys.reshape(2 * PEER_HEADS, PEER_NKEYS, PEER_HALF))


PEER_CE = 1024


def _gelu_tanh(x):
    return 0.5 * x * (1.0 + jnp.tanh(0.7978845608028654 * (x + 0.044715 * x * x * x)))


def _peer_dense_kernel(xm_ref, u_ref, vt_ref, e1_ref, cut_ref, e2_ref, r2_ref, x_ref, gate_ref, g_ref, b_ref,
                       o_ref, acc_s, at_s, w_s, e2_s, r2_s, *, tm):
    e = pl.program_id(2)
    nb = PEER_CE // PEER_NKEYS
    ntt = tm // PEER_RT

    @pl.when(e == 0)
    def _():
        acc_s[...] = jnp.zeros_like(acc_s)
        e2_s[:, :, :tm] = e2_ref[0]
        r2_s[:, :, :tm] = r2_ref[0]

    packed = (PEER_NKEYS // 16, 16, PEER_RT)
    ng = 2

    def gate_tiles(tt, i0):
        tok = slice(tt * PEER_RT, (tt + 1) * PEER_RT)
        gmats = [jnp.zeros(packed, BF16) for _ in range(ng)]
        for h in range(PEER_HEADS):
            e2 = e2_s[h, :, tok].reshape(packed)
            r2 = r2_s[h, :, tok].reshape(packed)
            for k in range(ng):
                i = i0 + k
                e1 = jnp.broadcast_to(e1_ref[0, h, i:i + 1, tok], (16, PEER_RT))[None]
                cut = jnp.broadcast_to(cut_ref[0, h, i:i + 1, tok], (16, PEER_RT))[None]
                gmats[k] = gmats[k] + e1 * jnp.where(r2 < cut, e2, jnp.zeros_like(e2))
        for k in range(ng):
            rows = slice((i0 + k) * PEER_NKEYS, (i0 + k + 1) * PEER_NKEYS)
            act = _gelu_tanh(at_s[rows, tok]).astype(BF16)
            w_s[rows, tok] = gmats[k].reshape(PEER_NKEYS, PEER_RT) * act

    at_s[:, :tm] = _dot(u_ref[0], xm_ref[0], ((1,), (1,)))
    for tt in range(ntt):
        for i0 in range(0, nb, ng):
            gate_tiles(tt, i0)
    acc_s[:, :tm] += _dot(vt_ref[0, 0], w_s[:, :tm])

    @pl.when(e == pl.num_programs(2) - 1)
    def _():
        z = DEEPNORM_ALPHA * x_ref[0] + gate_ref[0] * acc_s[:, :tm].T
        o_ref[0] = _layer_norm_rows(z, g_ref[...], b_ref[...])


def peer_dense(xm, u_all, vt_all, l, e1, cut, e2, r2, x3, mod3, gate_chunk, ln_g, ln_b, tm=1024):
    nseg, seg, d = x3.shape
    ne = u_all.shape[1]
    nb = PEER_CE // PEER_NKEYS
    tp = tm + PEER_RT
    tok = lambda s, i, e: (s, i, 0)
    chunk = pl.BlockSpec((1, PEER_HEADS, nb, tm), lambda s, i, e: (s, 0, e, i))
    full = pl.BlockSpec((1, PEER_HEADS, PEER_NKEYS, tm), lambda s, i, e: (s, 0, 0, i))
    return pl.pallas_call(
        functools.partial(_peer_dense_kernel, tm=tm),
        grid=(nseg, seg // tm, ne // PEER_CE),
        in_specs=[pl.BlockSpec((1, tm, d), tok),
                  pl.BlockSpec((1, PEER_CE, d), lambda s, i, e: (l, e, 0)),
                  pl.BlockSpec((1, 1, d, PEER_CE), lambda s, i, e: (l, e, 0, 0)),
                  chunk, chunk, full, full,
                  pl.BlockSpec((1, tm, d), tok),
                  pl.BlockSpec((1, 1, d), lambda s, i, e: (s, 0, gate_chunk)),
                  pl.BlockSpec((1, d), lambda s, i, e: (0, 0)),
                  pl.BlockSpec((1, d), lambda s, i, e: (0, 0))],
        out_specs=pl.BlockSpec((1, tm, d), tok),
        out_shape=jax.ShapeDtypeStruct((nseg, seg, d), F32),
        scratch_shapes=[pltpu.VMEM((d, tp), F32), pltpu.VMEM((PEER_CE, tp), F32), pltpu.VMEM((PEER_CE, tp), BF16),
                        pltpu.VMEM((PEER_HEADS, PEER_NKEYS, tp), BF16), pltpu.VMEM((PEER_HEADS, PEER_NKEYS, tp), BF16)],
        compiler_params=_params(("arbitrary", "arbitrary", "arbitrary")),
        name="peer_dense",
    )(xm, u_all, vt_all, e1, cut, e2, r2, x3, mod3, ln_g.reshape(1, d), ln_b.reshape(1, d))


def peer_layer(x3, mod3, l, wq, subkeys, u_all, vt_all, ln_g, ln_b):
    xm, e1, cut, e2, r2 = peer_route(x3, mod3, 3, wq.astype(BF16), subkeys)
    return peer_dense(xm, u_all, vt_all, l, e1, cut, e2, r2, x3, mod3, 5, ln_g, ln_b)


def _pad_cols(w, n):
    return jnp.pad(w, ((0, 0), (0, n - w.shape[1])))


def _stream(prompt_part, sample_part):
    return jnp.concatenate([prompt_part.reshape(1, -1, prompt_part.shape[-1]), sample_part], axis=0)


def _head_major(a, heads):
    b, t, _ = a.shape
    return jnp.transpose(a.reshape(b, t, heads, -1), (0, 2, 1, 3))


def _token_major(a):
    b, h, t, dh = a.shape
    return jnp.transpose(a, (0, 2, 1, 3)).reshape(b, t, h * dh)


MLSTM_CHUNK = 256
GLA_CHUNK = 64
NPROJ = 3200


def mlstm_layer(x3, mod3, bp, lp, st_c, st_n, st_m, w_in, b_gate, norm_w, w_out, ln_g, ln_b):
    nseg, seg, _ = x3.shape
    bs = nseg - 1
    p = mod_matmul(x3, mod3, 0, _pad_cols(w_in, NPROJ).astype(BF16))
    L = min(MLSTM_CHUNK, lp)
    rows = nseg * seg // L
    seq_lens = [lp] * bp + [seg] * bs
    c0 = jnp.concatenate([jnp.zeros((bp, 8, M_DK, M_DV), F32), st_c.reshape(bs, 8, M_DK, M_DV)], 0)
    n0 = jnp.concatenate([jnp.zeros((bp, 8, M_DK), F32), st_n.reshape(bs, 8, M_DK)], 0)
    m0 = jnp.concatenate([jnp.zeros((bp, 8, M_DK), F32),
                          jnp.broadcast_to(st_m.reshape(bs, 8, 1), (bs, 8, M_DK))], 0)
    hf, hb, c_new, n_new, m_new = mlstm_scan(p.reshape(rows, L, NPROJ), seq_lens,
                                             p[:, :, GATE_OFF:GATE_OFF + 4 * M_HEADS].reshape(rows, L, 4 * M_HEADS), b_gate, c0, n0, m0, L)
    x3 = outproj_ln("mlstm", (hf.reshape(nseg, seg, -1), hb.reshape(nseg, seg, -1)), x3, mod3, 2,
                    w_out.astype(BF16), ln_g, ln_b, norm_w=norm_w, og=p, og_col=2)
    return (x3, c_new[:bp].reshape(bp, 2, M_HEADS, M_DK, M_DV), n_new[:bp].reshape(bp, 2, M_HEADS, M_DK),
            m_new[:bp, :, 0].reshape(bp, 2, M_HEADS))


def gla_layer(x3, mod3, bp, lp, st_s, w_in, w_gate2, b_gate2, norm_w, w_out, ln_g, ln_b):
    nseg, seg, _ = x3.shape
    bs = nseg - 1
    p = mod_matmul(x3, mod3, 0, _pad_cols(w_in, NPROJ).astype(BF16))
    L = GLA_CHUNK
    rows = nseg * seg // L
    seq_lens = [lp] * bp + [seg] * bs
    s0t = jnp.concatenate([jnp.zeros((bp, 8, G_DV, G_DK), F32),
                           jnp.swapaxes(st_s.reshape(bs, 8, G_DK, G_DV), -1, -2)], 0)
    of, ob, s_new = gla_scan(p.reshape(rows, L, NPROJ), seq_lens, p[:, :, GATE_OFF:GATE_OFF + 2 * G_GATE_RANK].reshape(rows, L, 2 * G_GATE_RANK),
                             w_gate2, b_gate2, s0t, L)
    x3 = outproj_ln("gla", (of.reshape(nseg, seg, -1), ob.reshape(nseg, seg, -1)), x3, mod3, 2,
                    w_out.astype(BF16), ln_g, ln_b, norm_w=jnp.tile(norm_w, G_HEADS), og=p, og_col=2)
    return x3, jnp.swapaxes(s_new[:bp], -1, -2).reshape(bp, 2, G_HEADS, G_DK, G_DV)


def na_layer(x3, mod3, bp, lp, cache_k, cache_v, w_in, rpb, w_out, ln_g, ln_b):
    nseg, seg, _ = x3.shape
    bs = nseg - 1
    hd = NA_HEADS * NA_HD
    p = mod_matmul(x3, mod3, 0, w_in.astype(BF16))
    pp = p[0].reshape(bp, lp, 3 * hd)
    hm = lambda a: _head_major(a, NA_HEADS).astype(BF16)
    yp = attention(hm(pp[..., :hd]), hm(pp[..., hd:2 * hd]), hm(pp[..., 2 * hd:]), lp)
    ps = p[1:]
    ys = na_attention(hm(ps[..., :hd]), hm(ps[..., hd:2 * hd]), hm(ps[..., 2 * hd:]),
                      hm(cache_k.reshape(bs, -1, hd)), hm(cache_v.reshape(bs, -1, hd)), na_bias_table(rpb))
    x3 = outproj_ln("plain", _stream(_token_major(yp), _token_major(ys)), x3, mod3, 2, w_out.astype(BF16), ln_g, ln_b)
    return (x3, pp[..., hd:2 * hd].reshape(bp, lp, NA_HEADS, NA_HD), pp[..., 2 * hd:].reshape(bp, lp, NA_HEADS, NA_HD))


def _rope_rotated_cols(w):
    q = MLA_ROPE // 4
    return jnp.concatenate([-w[..., q:2 * q], w[..., :q], -w[..., 3 * q:], w[..., 2 * q:3 * q]], axis=-1)


def _rope_tables(ts):
    ra = MLA_ROPE // 2
    t = np.arange(ts)
    inv = 1.0 / (ROPE_BASE ** (np.arange(0, ra, 2, dtype=np.float32) / ra))
    ang_r = (t // GRID_W).astype(np.float32)[:, None] * inv[None, :]
    ang_c = (t % GRID_W).astype(np.float32)[:, None] * inv[None, :]
    ang = np.concatenate([ang_r, ang_r, ang_c, ang_c], axis=-1).astype(np.float32)
    return jnp.cos(jnp.asarray(ang)), jnp.sin(jnp.asarray(ang))


def mla_layer(x3, mod3, bp, lp, cache_ckv, cache_kpe, w_in, q_norm, w_qup, kv_norm, w_kvup, w_out, ln_g, ln_b):
    nseg, seg, _ = x3.shape
    bs = nseg - 1
    nq = MLA_Q_LORA + MLA_KV_LORA
    w_ext = jnp.concatenate([w_in, _rope_rotated_cols(w_in[:, nq:])], axis=1)
    p = mod_matmul(x3, mod3, 0, _pad_cols(w_ext, 896).astype(BF16))
    cos_t, sin_t = _rope_tables(seg)
    cos3 = jnp.concatenate([jnp.ones((1, seg, MLA_ROPE), F32), jnp.broadcast_to(cos_t, (bs, seg, MLA_ROPE))], 0)
    sin3 = jnp.concatenate([jnp.zeros((1, seg, MLA_ROPE), F32), jnp.broadcast_to(sin_t, (bs, seg, MLA_ROPE))], 0)
    wq = w_qup.reshape(MLA_Q_LORA, MLA_HEADS, MLA_NOPE + MLA_ROPE)
    wq_rope = wq[:, :, MLA_NOPE:]
    w_q3 = jnp.concatenate([wq[:, :, :MLA_NOPE].reshape(MLA_Q_LORA, -1), wq_rope.reshape(MLA_Q_LORA, -1),
                            _rope_rotated_cols(wq_rope).reshape(MLA_Q_LORA, -1)], axis=1).astype(BF16)
    q_all = mla_q(p, q_norm, w_q3, jnp.tile(cos3, (1, 1, MLA_HEADS)), jnp.tile(sin3, (1, 1, MLA_HEADS)))
    wkv = w_kvup.reshape(MLA_KV_LORA, MLA_HEADS, MLA_NOPE + MLA_VD)
    w_kv2 = jnp.concatenate([wkv[:, :, :MLA_NOPE].reshape(MLA_KV_LORA, -1),
                             wkv[:, :, MLA_NOPE:].reshape(MLA_KV_LORA, -1)], axis=1).astype(BF16)
    ckvn, kpe, kv = mla_kv(p, kv_norm, w_kv2, cos3, sin3)
    kvc = matmul(cache_ckv.reshape(-1, MLA_KV_LORA), w_kv2, 512).reshape(bs, -1, w_kv2.shape[1])
    nn = MLA_HEADS * MLA_NOPE

    def heads(q_rows, kv_rows, kpe_rows):
        b, t, _ = q_rows.shape
        tk = kv_rows.shape[1]
        qh = jnp.concatenate([q_rows[..., :nn].reshape(b, t, MLA_HEADS, MLA_NOPE),
                              q_rows[..., nn:].reshape(b, t, MLA_HEADS, MLA_ROPE)], -1)
        kh = jnp.concatenate([kv_rows[..., :nn].reshape(b, tk, MLA_HEADS, MLA_NOPE),
                              jnp.broadcast_to(kpe_rows[:, :, None, :], (b, tk, MLA_HEADS, MLA_ROPE))], -1)
        vh = kv_rows[..., nn:].reshape(b, tk, MLA_HEADS, MLA_VD)
        tr = lambda a: jnp.transpose(a, (0, 2, 1, 3)).astype(BF16)
        return tr(qh), tr(kh), tr(vh)

    yp = attention(*heads(q_all[0].reshape(bp, lp, -1), kv[0].reshape(bp, lp, -1), kpe[0].reshape(bp, lp, -1)), lp)
    ys = attention(*heads(q_all[1:], jnp.concatenate([kv[1:], kvc], 1), jnp.concatenate([kpe[1:], cache_kpe], 1)), 256)
    x3 = outproj_ln("plain", _stream(_token_major(yp), _token_major(ys)), x3, mod3, 2, w_out.astype(BF16), ln_g, ln_b)
    return x3, ckvn[0].reshape(bp, lp, MLA_KV_LORA), kpe[0].reshape(bp, lp, MLA_ROPE)


def kernel(x_prompt, x_sample, c, c_ctx, state_mlstm_C, state_mlstm_n, state_mlstm_m, state_gla_S, cache_na_k, cache_na_v, cache_mla_ckv, cache_mla_kpe, ada_w, ada_b, ln_mix_g, ln_mix_b, ln_ffn_g, ln_ffn_b, mlstm_w_in, mlstm_b_gate, mlstm_norm_w, mlstm_w_out, gla_w_in, gla_w_gate2, gla_b_gate2, gla_norm_w, gla_w_out, na_w_in, na_rpb, na_w_out, mla_w_in, mla_q_norm, mla_w_qup, mla_kv_norm, mla_w_kvup, mla_w_out, peer_w_q, peer_subkeys, peer_u, peer_v):
    bp, lp, d = x_prompt.shape
    bs, ts, _ = x_sample.shape
    assert bp * lp == ts and bs + 1 <= 8
    x3 = _stream(x_prompt, x_sample)
    cond8 = jnp.zeros((8, d), F32).at[0].set(c_ctx).at[1:1 + bs].set(c)
    mods = adaln_all(cond8, ada_w, ada_b)
    u_all = peer_u.astype(BF16)
    vt_all = jnp.swapaxes(peer_v.reshape(DEPTH, -1, PEER_CE, d), 2, 3).astype(BF16)
    outs = {}
    for l in range(DEPTH):
        mod3 = mods[l].reshape(8, 1, ADA_CHUNKS * d)
        kind = l % 4
        if kind == 0:
            x3, outs["C"], outs["n"], outs["m"] = mlstm_layer(
                x3, mod3, bp, lp, state_mlstm_C, state_mlstm_n, state_mlstm_m, mlstm_w_in, mlstm_b_gate,
                mlstm_norm_w, mlstm_w_out, ln_mix_g[l], ln_mix_b[l])
        elif kind == 1:
            x3, outs["S"] = gla_layer(x3, mod3, bp, lp, state_gla_S, gla_w_in, gla_w_gate2, gla_b_gate2,
                                      gla_norm_w, gla_w_out, ln_mix_g[l], ln_mix_b[l])
        elif kind == 2:
            x3, outs["nk"], outs["nv"] = na_layer(x3, mod3, bp, lp, cache_na_k, cache_na_v, na_w_in, na_rpb,
                                                  na_w_out, ln_mix_g[l], ln_mix_b[l])
        else:
            x3, outs["ckv"], outs["kpe"] = mla_layer(x3, mod3, bp, lp, cache_mla_ckv, cache_mla_kpe, mla_w_in,
                                                     mla_q_norm, mla_w_qup, mla_kv_norm, mla_w_kvup, mla_w_out,
                                                     ln_mix_g[l], ln_mix_b[l])
        x3 = peer_layer(x3, mod3, l, peer_w_q[l], peer_subkeys[l], u_all, vt_all, ln_ffn_g[l], ln_ffn_b[l])
    return (x3[0].reshape(bp, lp, d), x3[1:], outs["C"], outs["n"], outs["m"], outs["S"], outs["nk"], outs["nv"],
            outs["ckv"], outs["kpe"])
```

```python
import functools

import numpy as np
import jax
import jax.numpy as jnp
from jax import lax
from jax.experimental import pallas as pl
from jax.experimental.pallas import tpu as pltpu

D_MODEL = 1024
DEPTH = 4
GRID_W = 64
DEEPNORM_ALPHA = (2.0 * DEPTH) ** 0.25
ADA_CHUNKS = 6
NORM_EPS = 1e-5

M_HEADS, M_DK, M_DV = 4, 128, 256
G_HEADS, G_DK, G_DV = 4, 128, 256
K_OFF = M_HEADS * M_DK
V_OFF = 2 * M_HEADS * M_DK
QKV_COLS = V_OFF + M_HEADS * M_DV
HID_COLS = M_HEADS * M_DV
GATE_OFF = QKV_COLS + HID_COLS
G_GATE_RANK = 16
G_GATE_NORM = 16.0
NA_HEADS, NA_HD, NA_ROWS, NA_COLS = 16, 64, 8, 16
MLA_HEADS, MLA_Q_LORA, MLA_KV_LORA, MLA_NOPE, MLA_ROPE, MLA_VD = 16, 512, 256, 64, 32, 64
ROPE_BASE = 10000.0
PEER_HEADS, PEER_NKEYS, PEER_HALF, PEER_TOPK = 8, 128, 128, 16

V7X_VMEM_LIMIT = 56 * 1024 * 1024
F32 = jnp.float32
BF16 = jnp.bfloat16
NEG_INF = float("-inf")


def _params(sem, vmem=V7X_VMEM_LIMIT):
    return pltpu.CompilerParams(dimension_semantics=sem, vmem_limit_bytes=vmem)


def _dot(a, b, dims=((1,), (0,))):
    return lax.dot_general(a, b, (dims, ((), ())), preferred_element_type=F32)


def _split3(a):
    hi = a.astype(BF16)
    r1 = a - hi.astype(F32)
    mid = r1.astype(BF16)
    lo = (r1 - mid.astype(F32)).astype(BF16)
    return hi, mid, lo


def _dot_exact_lhs(m01, a):
    hi, mid, lo = _split3(a)
    return _dot(m01, hi) + _dot(m01, mid) + _dot(m01, lo)


def _dot_exact_rhs(a, m01):
    hi, mid, lo = _split3(a)
    return _dot(hi, m01) + _dot(mid, m01) + _dot(lo, m01)


def _log_sigmoid(x):
    return jnp.minimum(x, 0.0) - jnp.log(1.0 + jnp.exp(-jnp.abs(x)))


def _sigmoid(x):
    return 1.0 / (1.0 + jnp.exp(-x))


def _adaln_kernel(c_ref, w_ref, b_ref, o_ref):
    cv = c_ref[...]
    a = cv * _sigmoid(cv)
    o_ref[0] = lax.dot_general(a, w_ref[0], (((1,), (0,)), ((), ())), precision=lax.Precision.HIGHEST,
                               preferred_element_type=F32) + b_ref[0]


def adaln_all(cond8, ada_w, ada_b):
    tn = 1024
    n = ada_w.shape[-1]
    return pl.pallas_call(
        _adaln_kernel,
        grid=(DEPTH, n // tn),
        in_specs=[pl.BlockSpec((8, D_MODEL), lambda l, j: (0, 0)),
                  pl.BlockSpec((1, D_MODEL, tn), lambda l, j: (l, 0, j)),
                  pl.BlockSpec((1, 1, tn), lambda l, j: (l, 0, j))],
        out_specs=pl.BlockSpec((1, 8, tn), lambda l, j: (l, 0, j)),
        out_shape=jax.ShapeDtypeStruct((DEPTH, 8, n), F32),
        compiler_params=_params(("arbitrary", "arbitrary")),
        name="adaln",
    )(cond8, ada_w, ada_b.reshape(DEPTH, 1, n))


def _modmm_kernel(x_ref, sh_ref, sc_ref, w_ref, o_ref, xm_ref):
    @pl.when(pl.program_id(2) == 0)
    def _():
        xm_ref[...] = (x_ref[0] * (1.0 + sc_ref[0]) + sh_ref[0]).astype(BF16)

    o_ref[0] = _dot(xm_ref[...], w_ref[...]).astype(o_ref.dtype)


def mod_matmul(x3, mod3, shift_chunk, w_bf16, tm=512, tn=None, out_dtype=F32):
    nseg, seg, d = x3.shape
    n = w_bf16.shape[1]
    tn = n if tn is None else tn
    return pl.pallas_call(
        _modmm_kernel,
        grid=(nseg, seg // tm, n // tn),
        in_specs=[pl.BlockSpec((1, tm, d), lambda s, i, j: (s, i, 0)),
                  pl.BlockSpec((1, 1, d), lambda s, i, j: (s, 0, shift_chunk)),
                  pl.BlockSpec((1, 1, d), lambda s, i, j: (s, 0, shift_chunk + 1)),
                  pl.BlockSpec((d, tn), lambda s, i, j: (0, j))],
        out_specs=pl.BlockSpec((1, tm, tn), lambda s, i, j: (s, i, j)),
        out_shape=jax.ShapeDtypeStruct((nseg, seg, n), out_dtype),
        scratch_shapes=[pltpu.VMEM((tm, d), BF16)],
        compiler_params=_params(("arbitrary", "arbitrary", "arbitrary")),
        name="mod_matmul",
    )(x3, mod3, mod3, w_bf16)


def _layer_norm_rows(y, g, b):
    mu = jnp.mean(y, axis=-1, keepdims=True)
    yc = y - mu
    var = jnp.mean(yc * yc, axis=-1, keepdims=True)
    return yc * lax.rsqrt(var + NORM_EPS) * g + b


def _outproj_kernel(*refs, mode):
    if mode == "plain":
        y_ref, x_ref, gate_ref, w_ref, g_ref, b_ref, o_ref = refs
        yin = y_ref[0].astype(BF16)
    else:
        ya_ref, yb_ref, og_ref, nw_ref, x_ref, gate_ref, w_ref, g_ref, b_ref, o_ref = refs
        hs = ya_ref[0] + yb_ref[0]
        og = og_ref[0]
        parts = []
        for h in range(4):
            seg = hs[:, h * 256:(h + 1) * 256]
            nw = nw_ref[:, h * 256:(h + 1) * 256]
            if mode == "mlstm":
                mu = jnp.mean(seg, axis=-1, keepdims=True)
                sc = seg - mu
                var = jnp.mean(sc * sc, axis=-1, keepdims=True)
                parts.append(sc * lax.rsqrt(var + NORM_EPS) * nw)
            else:
                ms = jnp.mean(seg * seg, axis=-1, keepdims=True)
                parts.append(seg * lax.rsqrt(ms + NORM_EPS) * nw)
        hn = jnp.concatenate(parts, axis=-1)
        act = _sigmoid(og) if mode == "mlstm" else og * _sigmoid(og)
        yin = (act * hn).astype(BF16)
    y = _dot(yin, w_ref[...])
    z = DEEPNORM_ALPHA * x_ref[0] + gate_ref[0] * y
    o_ref[0] = _layer_norm_rows(z, g_ref[...], b_ref[...])


def outproj_ln(mode, ys, x3, mod3, gate_chunk, w_bf16, ln_g, ln_b, norm_w=None, og=None, og_col=0, tm=512):
    nseg, seg, d = x3.shape
    k = w_bf16.shape[0]
    tok = lambda s, i: (s, i, 0)
    if mode == "plain":
        args = [ys]
        specs = [pl.BlockSpec((1, tm, k), tok)]
    else:
        args = [ys[0], ys[1], og, norm_w.reshape(1, k)]
        specs = [pl.BlockSpec((1, tm, k), tok), pl.BlockSpec((1, tm, k), tok),
                 pl.BlockSpec((1, tm, k), lambda s, i: (s, i, og_col)),
                 pl.BlockSpec((1, k), lambda s, i: (0, 0))]
    args += [x3, mod3, w_bf16, ln_g.reshape(1, d), ln_b.reshape(1, d)]
    specs += [pl.BlockSpec((1, tm, d), tok),
              pl.BlockSpec((1, 1, d), lambda s, i: (s, 0, gate_chunk)),
              pl.BlockSpec((k, d), lambda s, i: (0, 0)),
              pl.BlockSpec((1, d), lambda s, i: (0, 0)),
              pl.BlockSpec((1, d), lambda s, i: (0, 0))]
    return pl.pallas_call(
        functools.partial(_outproj_kernel, mode=mode),
        grid=(nseg, seg // tm),
        in_specs=specs,
        out_specs=pl.BlockSpec((1, tm, d), tok),
        out_shape=jax.ShapeDtypeStruct((nseg, seg, d), F32),
        compiler_params=_params(("arbitrary", "arbitrary")),
        name="outproj_ln_" + mode,
    )(*args)


def _tri(n, lower):
    r = lax.broadcasted_iota(jnp.int32, (n, n), 0)
    c = lax.broadcasted_iota(jnp.int32, (n, n), 1)
    return (c <= r) if lower else (c >= r)


def _mlstm_kernel(rowf_ref, rowb_ref, seq_ref, first_ref, last_ref,
                  pf_ref, pb_ref, gf_ref, gb_ref, gtf_ref, gtb_ref, bias_ref, biast_ref,
                  c0_ref, n0_ref, m0_ref, hf_ref, hb_ref, co_ref, no_ref, mo_ref,
                  c_s, n_s, m_s, *, L):
    item = pl.program_id(0)

    @pl.when(first_ref[item] == 1)
    def _():
        c_s[...] = c0_ref[0]
        n_s[...] = n0_ref[0]
        m_s[...] = m0_ref[0]

    dirs = [(pf_ref, gf_ref, gtf_ref, hf_ref), (pb_ref, gb_ref, gtb_ref, hb_ref)]
    masks = [_tri(L, lower=True), _tri(L, lower=False)]
    gates = []
    for d in range(2):
        g = dirs[d][1][0] + bias_ref[...]
        gt = dirs[d][2][0] + biast_ref[...]
        lf_c = _log_sigmoid(g[:, d * 8 + 4:d * 8 + 8])
        lf_r = _log_sigmoid(gt[d * 8 + 4:d * 8 + 8, :])
        b_c = _dot_exact_lhs(masks[d].astype(BF16), lf_c)
        b_r = _dot_exact_rhs(lf_r, masks[1 - d].astype(BF16))
        gates.append((g[:, d * 8:d * 8 + 4], gt[d * 8:d * 8 + 4, :], b_c, b_r))
    units = [(d, h) for d in range(2) for h in range(M_HEADS)]

    def unit_inputs(d, h):
        p_ref = dirs[d][0]
        q = p_ref[0, :, h * M_DK:(h + 1) * M_DK]
        k = p_ref[0, :, K_OFF + h * M_DK:K_OFF + (h + 1) * M_DK] * (M_DK ** -0.5)
        v = p_ref[0, :, V_OFF + h * M_DV:V_OFF + (h + 1) * M_DV].astype(BF16)
        li_c, li_r, b_c, b_r = gates[d]
        return q, k, v, li_c[:, h:h + 1], li_r[h:h + 1, :], b_c[:, h:h + 1], b_r[h:h + 1, :]

    qks = []
    for d, h in units:
        q, k, _, _, _, _, _ = unit_inputs(d, h)
        qks.append(_dot(q.astype(BF16), k.astype(BF16), ((1,), (1,))))
    smats, eis, mts = [], [], []
    for u, (d, h) in enumerate(units):
        _, _, _, _, lir, bc, br = unit_inputs(d, h)
        m_prev = m_s[u:u + 1, 0:1]
        dmat = jnp.where(masks[d], bc - br + lir, NEG_INF)
        inter = bc + m_prev
        mt = jnp.maximum(inter, jnp.max(dmat, axis=-1, keepdims=True))
        smats.append(qks[u] * jnp.exp(dmat - mt))
        eis.append(jnp.exp(inter - mt))
        mts.append(mt)
    nums = []
    for u, (d, h) in enumerate(units):
        q, _, v, _, _, _, _ = unit_inputs(d, h)
        nums.append(_dot(smats[u].astype(BF16), v) + eis[u] * _dot(q.astype(BF16), c_s[u].astype(BF16)))
    for u, (d, h) in enumerate(units):
        q, _, _, _, _, _, _ = unit_inputs(d, h)
        den = (jnp.sum(smats[u], axis=-1, keepdims=True)
               + eis[u] * jnp.sum(q * n_s[u:u + 1, :], axis=-1, keepdims=True))
        dirs[d][3][0, :, h * M_DV:(h + 1) * M_DV] = nums[u] / jnp.maximum(jnp.abs(den), jnp.exp(-mts[u]))
    for u, (d, h) in enumerate(units):
        _, k, v, lic, lir, bc, br = unit_inputs(d, h)
        last = L - 1 if d == 0 else 0
        m_prev = m_s[u:u + 1, 0:1]
        tot = br[:, last:last + 1]
        g_c = tot - bc + lic
        g_r = tot - br + lir
        m_new = jnp.maximum(tot + m_prev, jnp.max(g_r, axis=-1, keepdims=True))
        kw = k * jnp.exp(g_c - m_new)
        dec = jnp.exp(tot + m_prev - m_new)
        c_s[u] = dec * c_s[u] + _dot(kw.astype(BF16), v, ((0,), (0,)))
        n_s[u:u + 1, :] = dec * n_s[u:u + 1, :] + jnp.sum(kw, axis=0, keepdims=True)
        m_s[u:u + 1, :] = jnp.broadcast_to(m_new, (1, 128))

    @pl.when(last_ref[item] == 1)
    def _():
        co_ref[0] = c_s[...]
        no_ref[0] = n_s[...]
        mo_ref[0] = m_s[...]


def _scan_items(seq_lens, L):
    rowf, rowb, seq, first, last = [], [], [], [], []
    base = 0
    for s, t in enumerate(seq_lens):
        nc = t // L
        for c in range(nc):
            rowf.append(base + c)
            rowb.append(base + nc - 1 - c)
            seq.append(s)
            first.append(int(c == 0))
            last.append(int(c == nc - 1))
        base += nc
    return [jnp.asarray(np.array(a, np.int32)) for a in (rowf, rowb, seq, first, last)]


def mlstm_scan(p, seq_lens, g, bias, c0, n0, m0, L):
    items = _scan_items(seq_lens, L)
    nseq = len(seq_lens)
    rows = p.shape[0]
    hshape = jax.ShapeDtypeStruct((rows, L, M_HEADS * M_DV), F32)
    fwd = lambda i, rf, rb, sq, fs, ls: (rf[i], 0, 0)
    bwd = lambda i, rf, rb, sq, fs, ls: (rb[i], 0, 0)
    st4 = lambda i, rf, rb, sq, fs, ls: (sq[i], 0, 0, 0)
    st3 = lambda i, rf, rb, sq, fs, ls: (sq[i], 0, 0)
    fix = lambda i, rf, rb, sq, fs, ls: (0, 0)
    gt = jnp.swapaxes(g, 1, 2)
    return pl.pallas_call(
        functools.partial(_mlstm_kernel, L=L),
        grid_spec=pltpu.PrefetchScalarGridSpec(
            num_scalar_prefetch=5,
            grid=(items[0].shape[0],),
            in_specs=[pl.BlockSpec((1, L, QKV_COLS), fwd), pl.BlockSpec((1, L, QKV_COLS), bwd),
                      pl.BlockSpec((1, L, 16), fwd), pl.BlockSpec((1, L, 16), bwd),
                      pl.BlockSpec((1, 16, L), fwd), pl.BlockSpec((1, 16, L), bwd),
                      pl.BlockSpec((1, 16), fix), pl.BlockSpec((16, 1), fix),
                      pl.BlockSpec((1, 8, M_DK, M_DV), st4),
                      pl.BlockSpec((1, 8, M_DK), st3),
                      pl.BlockSpec((1, 8, M_DK), st3)],
            out_specs=[pl.BlockSpec((1, L, HID_COLS), fwd), pl.BlockSpec((1, L, HID_COLS), bwd),
                       pl.BlockSpec((1, 8, M_DK, M_DV), st4),
                       pl.BlockSpec((1, 8, M_DK), st3),
                       pl.BlockSpec((1, 8, M_DK), st3)],
            scratch_shapes=[pltpu.VMEM((8, M_DK, M_DV), F32), pltpu.VMEM((8, M_DK), F32),
                            pltpu.VMEM((8, M_DK), F32)]),
        out_shape=[hshape, hshape,
                   jax.ShapeDtypeStruct((nseq, 8, M_DK, M_DV), F32),
                   jax.ShapeDtypeStruct((nseq, 8, M_DK), F32),
                   jax.ShapeDtypeStruct((nseq, 8, M_DK), F32)],
        compiler_params=_params(("arbitrary",)),
        name="mlstm_scan",
    )(*items, p, p, g, g, gt, gt, bias.reshape(1, 16), bias.reshape(16, 1), c0, n0, m0)


def _gla_kernel(rowf_ref, rowb_ref, seq_ref, first_ref, last_ref,
                pf_ref, pb_ref, gf_ref, gb_ref, w2_ref, b2_ref, s0_ref, of_ref, ob_ref, so_ref, s_s, *, L):
    item = pl.program_id(0)

    @pl.when(first_ref[item] == 1)
    def _():
        s_s[...] = s0_ref[0]

    dirs = [(pf_ref, gf_ref, of_ref), (pb_ref, gb_ref, ob_ref)]
    masks = [_tri(L, lower=True), _tri(L, lower=False)]
    bcs = []
    for d in range(2):
        gr = dirs[d][1][0][:, d * G_GATE_RANK:(d + 1) * G_GATE_RANK]
        pre = lax.dot_general(gr, w2_ref[d], (((1,), (0,)), ((), ())), precision=lax.Precision.HIGHEST,
                              preferred_element_type=F32) + b2_ref[d]
        la = _log_sigmoid(pre) * (1.0 / G_GATE_NORM)
        bcs.append(_dot_exact_lhs(masks[d].astype(BF16), la))
    units = [(d, h) for d in range(2) for h in range(G_HEADS)]

    def unit_inputs(d, h):
        p_ref = dirs[d][0]
        q = p_ref[0, :, h * G_DK:(h + 1) * G_DK] * (G_DK ** -0.5)
        k = p_ref[0, :, K_OFF + h * G_DK:K_OFF + (h + 1) * G_DK]
        v = p_ref[0, :, V_OFF + h * G_DV:V_OFF + (h + 1) * G_DV].astype(BF16)
        return q, k, v, bcs[d][:, h * G_DK:(h + 1) * G_DK]

    qds, a_mats = [], []
    for d, h in units:
        q, k, _, bc = unit_inputs(d, h)
        qd = (q * jnp.exp(bc)).astype(BF16)
        kd = (k * jnp.exp(-bc)).astype(BF16)
        qds.append(qd)
        a_mats.append(jnp.where(masks[d], _dot(qd, kd, ((1,), (1,))), 0.0).astype(BF16))
    for u, (d, h) in enumerate(units):
        _, _, v, _ = unit_inputs(d, h)
        dirs[d][2][0, :, h * G_DV:(h + 1) * G_DV] = (_dot(a_mats[u], v)
                                                     + _dot(qds[u], s_s[u].astype(BF16), ((1,), (1,))))
    for u, (d, h) in enumerate(units):
        _, k, v, bc = unit_inputs(d, h)
        last = L - 1 if d == 0 else 0
        bl = bc[last:last + 1, :]
        kl = (k * jnp.exp(bl - bc)).astype(BF16)
        s_s[u] = s_s[u] * jnp.exp(bl) + _dot(v, kl, ((0,), (0,)))

    @pl.when(last_ref[item] == 1)
    def _():
        so_ref[0] = s_s[...]


def gla_scan(p, seq_lens, gr, w2, b2, s0t, L):
    items = _scan_items(seq_lens, L)
    nseq = len(seq_lens)
    oshape = jax.ShapeDtypeStruct((p.shape[0], L, G_HEADS * G_DV), F32)
    fwd = lambda i, rf, rb, sq, fs, ls: (rf[i], 0, 0)
    bwd = lambda i, rf, rb, sq, fs, ls: (rb[i], 0, 0)
    st4 = lambda i, rf, rb, sq, fs, ls: (sq[i], 0, 0, 0)
    fix3 = lambda i, rf, rb, sq, fs, ls: (0, 0, 0)
    return pl.pallas_call(
        functools.partial(_gla_kernel, L=L),
        grid_spec=pltpu.PrefetchScalarGridSpec(
            num_scalar_prefetch=5,
            grid=(items[0].shape[0],),
            in_specs=[pl.BlockSpec((1, L, QKV_COLS), fwd), pl.BlockSpec((1, L, QKV_COLS), bwd),
                      pl.BlockSpec((1, L, 32), fwd), pl.BlockSpec((1, L, 32), bwd),
                      pl.BlockSpec((2, G_GATE_RANK, 512), fix3),
                      pl.BlockSpec((2, 1, 512), fix3),
                      pl.BlockSpec((1, 8, G_DV, G_DK), st4)],
            out_specs=[pl.BlockSpec((1, L, HID_COLS), fwd), pl.BlockSpec((1, L, HID_COLS), bwd),
                       pl.BlockSpec((1, 8, G_DV, G_DK), st4)],
            scratch_shapes=[pltpu.VMEM((8, G_DV, G_DK), F32)]),
        out_shape=[oshape, oshape, jax.ShapeDtypeStruct((nseq, 8, G_DV, G_DK), F32)],
        compiler_params=_params(("arbitrary",)),
        name="gla_scan",
    )(*items, p, p, gr, gr, w2, b2.reshape(2, 1, 512), s0t)


ATTN_HEADS_PER_STEP = 2


def _attn_kernel(q_ref, k_ref, v_ref, o_ref, *, scale):
    scores = [_dot(q_ref[0, j], k_ref[0, j], ((1,), (1,))) * scale for j in range(ATTN_HEADS_PER_STEP)]
    for j, s in enumerate(scores):
        m = jnp.max(s, axis=-1, keepdims=True)
        p = jnp.exp(s - m)
        l = jnp.sum(p, axis=-1, keepdims=True)
        o_ref[0, j] = _dot(p.astype(BF16), v_ref[0, j]) / l


def attention(q, k, v, tq):
    b, h, lq, dq = q.shape
    lk, dv = k.shape[2], v.shape[3]
    hb = ATTN_HEADS_PER_STEP
    return pl.pallas_call(
        functools.partial(_attn_kernel, scale=dq ** -0.5),
        grid=(b, h // hb, lq // tq),
        in_specs=[pl.BlockSpec((1, hb, tq, dq), lambda b, h, i: (b, h, i, 0)),
                  pl.BlockSpec((1, hb, lk, dq), lambda b, h, i: (b, h, 0, 0)),
                  pl.BlockSpec((1, hb, lk, dv), lambda b, h, i: (b, h, 0, 0))],
        out_specs=pl.BlockSpec((1, hb, tq, dv), lambda b, h, i: (b, h, i, 0)),
        out_shape=jax.ShapeDtypeStruct((b, h, lq, dv), F32),
        compiler_params=_params(("arbitrary", "arbitrary", "arbitrary")),
        name="attention",
    )(q, k, v)


NA_RB = 16


def _na_kernel(q_ref, k_ref, v_ref, kc_ref, vc_ref, bias_ref, o_ref, *, rows):
    j = pl.program_id(2)
    scale = NA_HD ** -0.5
    s_ctx_all = _dot(q_ref[0, 0], kc_ref[0, 0], ((1,), (1,))) * scale
    offs, s_locs = [], []
    for a in range(NA_RB):
        r = j * NA_RB + a
        start = jnp.clip(r - NA_ROWS // 2, 0, rows - NA_ROWS)
        dr0 = start - r + (NA_ROWS - 1)
        offs.append(pl.multiple_of(start * GRID_W, GRID_W))
        qa = q_ref[0, 0, a * GRID_W:(a + 1) * GRID_W, :]
        kl = k_ref[0, 0, pl.ds(offs[a], NA_ROWS * GRID_W), :]
        s_locs.append(_dot(qa, kl, ((1,), (1,))) * scale + bias_ref[0, dr0])
    p_locs, p_ctxs, ls = [], [], []
    for a in range(NA_RB):
        s_ctx = s_ctx_all[a * GRID_W:(a + 1) * GRID_W, :]
        m = jnp.maximum(jnp.max(s_locs[a], axis=-1, keepdims=True), jnp.max(s_ctx, axis=-1, keepdims=True))
        p_loc = jnp.exp(s_locs[a] - m)
        p_ctx = jnp.exp(s_ctx - m)
        ls.append(jnp.sum(p_loc, axis=-1, keepdims=True) + jnp.sum(p_ctx, axis=-1, keepdims=True))
        p_locs.append(p_loc.astype(BF16))
        p_ctxs.append(p_ctx.astype(BF16))
    o_ctx_all = _dot(jnp.concatenate(p_ctxs, axis=0), vc_ref[0, 0])
    for a in range(NA_RB):
        vl = v_ref[0, 0, pl.ds(offs[a], NA_ROWS * GRID_W), :]
        o = _dot(p_locs[a], vl) + o_ctx_all[a * GRID_W:(a + 1) * GRID_W, :]
        o_ref[0, 0, a * GRID_W:(a + 1) * GRID_W, :] = o / ls[a]


def na_bias_table(rpb):
    cq = np.arange(GRID_W)[:, None]
    ck = np.arange(GRID_W)[None, :]
    cs = np.clip(cq - NA_COLS // 2, 0, GRID_W - NA_COLS)
    ok = (ck >= cs) & (ck < cs + NA_COLS)
    dc = np.clip(ck - cq, -(NA_COLS - 1), NA_COLS - 1) + (NA_COLS - 1)
    t = jnp.where(ok[None, None], rpb.astype(F32)[:, :, dc], NEG_INF)
    rows = np.arange(NA_ROWS)[:, None] + np.arange(NA_ROWS)[None, :]
    tf = t[:, rows]
    return jnp.transpose(tf, (0, 1, 3, 2, 4)).reshape(NA_HEADS, NA_ROWS, GRID_W, NA_ROWS * GRID_W)


def na_attention(q, k, v, kc, vc, bias):
    b, h, t, dh = q.shape
    lc = kc.shape[2]
    rows = t // GRID_W
    full = lambda b, h, j: (b, h, 0, 0)
    return pl.pallas_call(
        functools.partial(_na_kernel, rows=rows),
        grid=(b, h, rows // NA_RB),
        in_specs=[pl.BlockSpec((1, 1, NA_RB * GRID_W, dh), lambda b, h, j: (b, h, j, 0)),
                  pl.BlockSpec((1, 1, t, dh), full), pl.BlockSpec((1, 1, t, dh), full),
                  pl.BlockSpec((1, 1, lc, dh), full), pl.BlockSpec((1, 1, lc, dh), full),
                  pl.BlockSpec((1, NA_ROWS, GRID_W, NA_ROWS * GRID_W), lambda b, h, j: (h, 0, 0, 0))],
        out_specs=pl.BlockSpec((1, 1, NA_RB * GRID_W, dh), lambda b, h, j: (b, h, j, 0)),
        out_shape=jax.ShapeDtypeStruct((b, h, t, dh), F32),
        compiler_params=_params(("arbitrary", "arbitrary", "arbitrary")),
        name="na_attention",
    )(q, k, v, kc, vc, bias)


def _rms_rows(x, g):
    return x * lax.rsqrt(jnp.mean(x * x, axis=-1, keepdims=True) + NORM_EPS) * g


def _mla_q_kernel(cq_ref, g_ref, w_ref, cos_ref, sin_ref, o_ref):
    r = _dot(_rms_rows(cq_ref[0], g_ref[...]).astype(BF16), w_ref[...])
    nn = MLA_HEADS * MLA_NOPE
    nr = MLA_HEADS * MLA_ROPE
    o_ref[0, :, :nn] = r[:, :nn]
    o_ref[0, :, nn:] = r[:, nn:nn + nr] * cos_ref[0] + r[:, nn + nr:] * sin_ref[0]


def mla_q(p, q_norm, w_q3, cos_q, sin_q, tm=512):
    nseg, seg, _ = p.shape
    nout = MLA_HEADS * (MLA_NOPE + MLA_ROPE)
    nr = MLA_HEADS * MLA_ROPE
    tok = lambda s, i: (s, i, 0)
    return pl.pallas_call(
        _mla_q_kernel,
        grid=(nseg, seg // tm),
        in_specs=[pl.BlockSpec((1, tm, MLA_Q_LORA), tok),
                  pl.BlockSpec((1, MLA_Q_LORA), lambda s, i: (0, 0)),
                  pl.BlockSpec(w_q3.shape, lambda s, i: (0, 0)),
                  pl.BlockSpec((1, tm, nr), tok), pl.BlockSpec((1, tm, nr), tok)],
        out_specs=pl.BlockSpec((1, tm, nout), tok),
        out_shape=jax.ShapeDtypeStruct((nseg, seg, nout), F32),
        compiler_params=_params(("arbitrary", "arbitrary")),
        name="mla_q",
    )(p, q_norm.reshape(1, -1), w_q3, cos_q, sin_q)


def _mla_kv_kernel(ckv_ref, kpe_ref, g_ref, w_ref, cos_ref, sin_ref, ckvn_ref, kpeo_ref, kv_ref):
    cn = _rms_rows(ckv_ref[0], g_ref[...])
    ckvn_ref[0] = cn
    kv_ref[0] = _dot(cn.astype(BF16), w_ref[...])
    kp = kpe_ref[0]
    kpeo_ref[0] = kp[:, :MLA_ROPE] * cos_ref[0] + kp[:, MLA_ROPE:2 * MLA_ROPE] * sin_ref[0]


def mla_kv(p, kv_norm, w_kv, cos_k, sin_k, tm=512):
    nseg, seg, _ = p.shape
    nkv = w_kv.shape[1]
    tok = lambda s, i: (s, i, 0)
    return pl.pallas_call(
        _mla_kv_kernel,
        grid=(nseg, seg // tm),
        in_specs=[pl.BlockSpec((1, tm, MLA_KV_LORA), lambda s, i: (s, i, MLA_Q_LORA // MLA_KV_LORA)),
                  pl.BlockSpec((1, tm, 128), lambda s, i: (s, i, (MLA_Q_LORA + MLA_KV_LORA) // 128)),
                  pl.BlockSpec((1, MLA_KV_LORA), lambda s, i: (0, 0)),
                  pl.BlockSpec(w_kv.shape, lambda s, i: (0, 0)),
                  pl.BlockSpec((1, tm, MLA_ROPE), tok), pl.BlockSpec((1, tm, MLA_ROPE), tok)],
        out_specs=[pl.BlockSpec((1, tm, MLA_KV_LORA), tok), pl.BlockSpec((1, tm, MLA_ROPE), tok),
                   pl.BlockSpec((1, tm, nkv), tok)],
        out_shape=[jax.ShapeDtypeStruct((nseg, seg, MLA_KV_LORA), F32),
                   jax.ShapeDtypeStruct((nseg, seg, MLA_ROPE), F32),
                   jax.ShapeDtypeStruct((nseg, seg, nkv), F32)],
        compiler_params=_params(("arbitrary", "arbitrary")),
        name="mla_kv",
    )(p, p, kv_norm.reshape(1, -1), w_kv, cos_k, sin_k)


def _mm_kernel(a_ref, w_ref, o_ref):
    o_ref[...] = _dot(a_ref[...].astype(BF16), w_ref[...])


def matmul(a, w_bf16, tm):
    m, k = a.shape
    n = w_bf16.shape[1]
    return pl.pallas_call(
        _mm_kernel,
        grid=(m // tm,),
        in_specs=[pl.BlockSpec((tm, k), lambda i: (i, 0)), pl.BlockSpec((k, n), lambda i: (0, 0))],
        out_specs=pl.BlockSpec((tm, n), lambda i: (i, 0)),
        out_shape=jax.ShapeDtypeStruct((m, n), F32),
        compiler_params=_params(("arbitrary",)),
        name="matmul",
    )(a, w_bf16)


PEER_RT = 128
NOT_TOP = 99.0
RANK_CODE = 2.0 ** 100


def _top16(s, exact):
    vals = []
    if exact:
        key = lax.broadcasted_iota(jnp.int32, s.shape, 0).astype(F32)
        rank = jnp.full(s.shape, NOT_TOP, F32)
        for r in range(PEER_TOPK):
            m = jnp.max(s, axis=0, keepdims=True)
            hit = key == jnp.min(jnp.where(s == m, key, 1e9), axis=0, keepdims=True)
            rank = jnp.where(hit, float(r), rank)
            s = jnp.where(hit, NEG_INF, s)
            vals.append(m)
        return vals, rank
    for r in range(PEER_TOPK):
        m = jnp.max(s, axis=0, keepdims=True)
        s = jnp.where(s == m, -RANK_CODE * (1.0 + r / 32.0), s)
        vals.append(m)
    return vals, jnp.where(s <= -0.5 * RANK_CODE, s * (-32.0 / RANK_CODE) - 32.0, NOT_TOP)


def _pair_topk(av, bv, exact):
    n = av[0].shape[-1]
    a_lo, a_hi = jnp.concatenate(av[:8], 0), jnp.concatenate(av[8:], 0)
    b_lo, b_hi = jnp.concatenate(bv[:8], 0), jnp.concatenate(bv[8:], 0)
    row = lax.broadcasted_iota(jnp.int32, (8, n), 0).astype(F32)

    no_pos = 1e8

    def rows_b(a, b_blk, boff, nvalid):
        ok = row < nvalid
        return jnp.where(ok, av[a] + b_blk, NEG_INF), jnp.where(ok, a * 16.0 + boff + row, no_pos)

    def rows_a(b, a_blk, aoff, lo, hi):
        ok = (row >= lo) & (row < hi)
        return jnp.where(ok, a_blk + bv[b], NEG_INF), jnp.where(ok, (aoff + row) * 16.0 + b, no_pos)

    groups = [rows_b(0, b_lo, 0, 8), rows_b(0, b_hi, 8, 8), rows_b(1, b_lo, 0, 8), rows_b(2, b_lo, 0, 5),
              rows_b(3, b_lo, 0, 4), rows_a(0, a_lo, 0, 4, 8), rows_a(0, a_hi, 8, 0, 8),
              rows_a(1, a_lo, 0, 4, 8), rows_a(2, a_lo, 0, 4, 5)]
    cands = [g[0] for g in groups]
    poss = [g[1] for g in groups]
    sels = [jnp.zeros((8, n), F32) for _ in groups]
    top = av[0] + bv[0]
    z = jnp.zeros((1, n), F32)
    for _ in range(PEER_TOPK):
        m = functools.reduce(jnp.maximum, cands)
        m = jnp.max(m, axis=0, keepdims=True)
        if exact:
            first = functools.reduce(jnp.minimum, [jnp.where(c == m, p, 1e9) for c, p in zip(cands, poss)])
            first = jnp.min(first, axis=0, keepdims=True)
            hits = [p == first for p in poss]
            cands = [jnp.where(hh, NEG_INF, c) for hh, c in zip(hits, cands)]
            sels = [jnp.where(hh, 1.0, s) for hh, s in zip(hits, sels)]
        else:
            cands = [jnp.where(c == m, -RANK_CODE, c) for c in cands]
        z = z + jnp.exp(m - top)
    if not exact:
        sels = [jnp.where(c == -RANK_CODE, 1.0, 0.0) for c in cands]
    cnt = lambda x: jnp.sum(x, axis=0, keepdims=True)
    cut_lo = sels[5] + sels[7] + sels[8]
    for a, c in enumerate([cnt(sels[0]) + cnt(sels[1]), cnt(sels[2]), cnt(sels[3]), cnt(sels[4])]):
        cut_lo = cut_lo + jnp.where(row == a, c, 0.0)
    return cut_lo, sels[6], z, cnt(cut_lo) + cnt(sels[6])


def _peer_route_kernel(x_ref, sh_ref, sc_ref, wq_ref, sk_ref, xm_ref, e1_ref, cut_ref, e2_ref, r2_ref, q_s, *, tm):
    xm = (x_ref[0] * (1.0 + sc_ref[0]) + sh_ref[0]).astype(BF16)
    xm_ref[0] = xm
    q = _dot(xm, wq_ref[...])
    for hp in range(2 * PEER_HEADS):
        q_s[hp] = q[:, hp * PEER_HALF:(hp + 1) * PEER_HALF]

    def route(h, tok, exact):
        def scores(hp):
            return lax.dot_general(sk_ref[hp], q_s[hp, tok, :], (((1,), (1,)), ((), ())),
                                   precision=lax.Precision.HIGHEST, preferred_element_type=F32)

        s1, s2 = scores(2 * h), scores(2 * h + 1)
        av, rank1 = _top16(s1, exact)
        bv, rank2 = _top16(s2, exact)
        cut_lo, cut_hi, z, nsel = _pair_topk(av, bv, exact)
        ridx = jnp.clip(rank1, 0.0, NOT_TOP).astype(jnp.int32)
        parts = []
        for v in range(PEER_NKEYS // 8):
            rk = ridx[v * 8:(v + 1) * 8]
            lo = jnp.take_along_axis(cut_lo, rk & 7, axis=0)
            hi = jnp.take_along_axis(cut_hi, rk & 7, axis=0)
            parts.append(jnp.where(rk < 8, lo, jnp.where(rk < PEER_TOPK, hi, 0.0)))
        cut = jnp.concatenate(parts, axis=0)
        e1_ref[0, h, :, tok] = (jnp.exp(s1 - av[0]) / z).astype(BF16)
        cut_ref[0, h, :, tok] = cut.astype(BF16)
        e2_ref[0, h, :, tok] = jnp.exp(s2 - bv[0]).astype(BF16)
        r2_ref[0, h, :, tok] = rank2.astype(BF16)
        ranked = lambda rk: jnp.sum(jnp.where(rk < PEER_TOPK, 1.0, 0.0), axis=0, keepdims=True)
        return ranked(rank1), ranked(rank2), nsel

    def body(h, carry):
        toks = [pl.ds(t0, PEER_RT) for t0 in range(0, tm, PEER_RT)]
        counts = [route(h, tok, exact=False) for tok in toks]
        for tok, cnts in zip(toks, counts):
            bad = functools.reduce(jnp.maximum, [jnp.abs(cn - PEER_TOPK) for cn in cnts])

            @pl.when(jnp.max(bad) > 0.0)
            def _():
                route(h, tok, exact=True)
        return carry

    lax.fori_loop(0, PEER_HEADS, body, 0)


def peer_route(x3, mod3, shift_chunk, wq_bf16, subkeys, tm=512):
    nseg, seg, d = x3.shape
    tok = lambda s, i: (s, i, 0)
    rshape = jax.ShapeDtypeStruct((nseg, PEER_HEADS, PEER_NKEYS, seg), BF16)
    rspec = pl.BlockSpec((1, PEER_HEADS, PEER_NKEYS, tm), lambda s, i: (s, 0, 0, i))
    return pl.pallas_call(
        functools.partial(_peer_route_kernel, tm=tm),
        grid=(nseg, seg // tm),
        in_specs=[pl.BlockSpec((1, tm, d), tok),
                  pl.BlockSpec((1, 1, d), lambda s, i: (s, 0, shift_chunk)),
                  pl.BlockSpec((1, 1, d), lambda s, i: (s, 0, shift_chunk + 1)),
                  pl.BlockSpec(wq_bf16.shape, lambda s, i: (0, 0)),
                  pl.BlockSpec((2 * PEER_HEADS, PEER_NKEYS, PEER_HALF), lambda s, i: (0, 0, 0))],
        out_specs=[pl.BlockSpec((1, tm, d), tok), rspec, rspec, rspec, rspec],
        out_shape=[jax.ShapeDtypeStruct((nseg, seg, d), BF16)] + [rshape] * 4,
        scratch_shapes=[pltpu.VMEM((2 * PEER_HEADS, tm, PEER_HALF), F32)],
        compiler_params=_params(("arbitrary", "arbitrary")),
        name="peer_route",
    )(x3, mod3, mod3, wq_bf16, subkeys.reshape(2 * PEER_HEADS, PEER_NKEYS, PEER_HALF))


PEER_CE = 1024


def _gelu_tanh(x):
    return 0.5 * x * (1.0 + jnp.tanh(0.7978845608028654 * (x + 0.044715 * x * x * x)))


def _peer_dense_kernel(xm_ref, u_ref, vt_ref, e1_ref, cut_ref, e2_ref, r2_ref, x_ref, gate_ref, g_ref, b_ref,
                       o_ref, acc_s, at_s, w_s, e2_s, r2_s, *, tm):
    e = pl.program_id(2)
    nb = PEER_CE // PEER_NKEYS
    ntt = tm // PEER_RT

    @pl.when(e == 0)
    def _():
        acc_s[...] = jnp.zeros_like(acc_s)
        e2_s[:, :, :tm] = e2_ref[0]
        r2_s[:, :, :tm] = r2_ref[0]

    packed = (PEER_NKEYS // 16, 16, PEER_RT)
    ng = 2

    def gate_tiles(tt, i0):
        tok = slice(tt * PEER_RT, (tt + 1) * PEER_RT)
        gmats = [jnp.zeros(packed, BF16) for _ in range(ng)]
        for h in range(PEER_HEADS):
            e2 = e2_s[h, :, tok].reshape(packed)
            r2 = r2_s[h, :, tok].reshape(packed)
            for k in range(ng):
                i = i0 + k
                e1 = jnp.broadcast_to(e1_ref[0, h, i:i + 1, tok], (16, PEER_RT))[None]
                cut = jnp.broadcast_to(cut_ref[0, h, i:i + 1, tok], (16, PEER_RT))[None]
                gmats[k] = gmats[k] + e1 * jnp.where(r2 < cut, e2, jnp.zeros_like(e2))
        for k in range(ng):
            rows = slice((i0 + k) * PEER_NKEYS, (i0 + k + 1) * PEER_NKEYS)
            act = _gelu_tanh(at_s[rows, tok]).astype(BF16)
            w_s[rows, tok] = gmats[k].reshape(PEER_NKEYS, PEER_RT) * act

    at_s[:, :tm] = _dot(u_ref[0], xm_ref[0], ((1,), (1,)))
    for tt in range(ntt):
        for i0 in range(0, nb, ng):
            gate_tiles(tt, i0)
    acc_s[:, :tm] += _dot(vt_ref[0, 0], w_s[:, :tm])

    @pl.when(e == pl.num_programs(2) - 1)
    def _():
        z = DEEPNORM_ALPHA * x_ref[0] + gate_ref[0] * acc_s[:, :tm].T
        o_ref[0] = _layer_norm_rows(z, g_ref[...], b_ref[...])


def peer_dense(xm, u_all, vt_all, l, e1, cut, e2, r2, x3, mod3, gate_chunk, ln_g, ln_b, tm=1024):
    nseg, seg, d = x3.shape
    ne = u_all.shape[1]
    nb = PEER_CE // PEER_NKEYS
    tp = tm + PEER_RT
    tok = lambda s, i, e: (s, i, 0)
    chunk = pl.BlockSpec((1, PEER_HEADS, nb, tm), lambda s, i, e: (s, 0, e, i))
    full = pl.BlockSpec((1, PEER_HEADS, PEER_NKEYS, tm), lambda s, i, e: (s, 0, 0, i))
    return pl.pallas_call(
        functools.partial(_peer_dense_kernel, tm=tm),
        grid=(nseg, seg // tm, ne // PEER_CE),
        in_specs=[pl.BlockSpec((1, tm, d), tok),
                  pl.BlockSpec((1, PEER_CE, d), lambda s, i, e: (l, e, 0)),
                  pl.BlockSpec((1, 1, d, PEER_CE), lambda s, i, e: (l, e, 0, 0)),
                  chunk, chunk, full, full,
                  pl.BlockSpec((1, tm, d), tok),
                  pl.BlockSpec((1, 1, d), lambda s, i, e: (s, 0, gate_chunk)),
                  pl.BlockSpec((1, d), lambda s, i, e: (0, 0)),
                  pl.BlockSpec((1, d), lambda s, i, e: (0, 0))],
        out_specs=pl.BlockSpec((1, tm, d), tok),
        out_shape=jax.ShapeDtypeStruct((nseg, seg, d), F32),
        scratch_shapes=[pltpu.VMEM((d, tp), F32), pltpu.VMEM((PEER_CE, tp), F32), pltpu.VMEM((PEER_CE, tp), BF16),
                        pltpu.VMEM((PEER_HEADS, PEER_NKEYS, tp), BF16), pltpu.VMEM((PEER_HEADS, PEER_NKEYS, tp), BF16)],
        compiler_params=_params(("arbitrary", "arbitrary", "arbitrary")),
        name="peer_dense",
    )(xm, u_all, vt_all, e1, cut, e2, r2, x3, mod3, ln_g.reshape(1, d), ln_b.reshape(1, d))


def peer_layer(x3, mod3, l, wq, subkeys, u_all, vt_all, ln_g, ln_b):
    xm, e1, cut, e2, r2 = peer_route(x3, mod3, 3, wq.astype(BF16), subkeys)
    return peer_dense(xm, u_all, vt_all, l, e1, cut, e2, r2, x3, mod3, 5, ln_g, ln_b)


def _pad_cols(w, n):
    return jnp.pad(w, ((0, 0), (0, n - w.shape[1])))


def _stream(prompt_part, sample_part):
    return jnp.concatenate([prompt_part.reshape(1, -1, prompt_part.shape[-1]), sample_part], axis=0)


def _head_major(a, heads):
    b, t, _ = a.shape
    return jnp.transpose(a.reshape(b, t, heads, -1), (0, 2, 1, 3))


def _token_major(a):
    b, h, t, dh = a.shape
    return jnp.transpose(a, (0, 2, 1, 3)).reshape(b, t, h * dh)


MLSTM_CHUNK = 256
GLA_CHUNK = 64
NPROJ = 3200


def mlstm_layer(x3, mod3, bp, lp, st_c, st_n, st_m, w_in, b_gate, norm_w, w_out, ln_g, ln_b):
    nseg, seg, _ = x3.shape
    bs = nseg - 1
    p = mod_matmul(x3, mod3, 0, _pad_cols(w_in, NPROJ).astype(BF16))
    L = min(MLSTM_CHUNK, lp)
    rows = nseg * seg // L
    seq_lens = [lp] * bp + [seg] * bs
    c0 = jnp.concatenate([jnp.zeros((bp, 8, M_DK, M_DV), F32), st_c.reshape(bs, 8, M_DK, M_DV)], 0)
    n0 = jnp.concatenate([jnp.zeros((bp, 8, M_DK), F32), st_n.reshape(bs, 8, M_DK)], 0)
    m0 = jnp.concatenate([jnp.zeros((bp, 8, M_DK), F32),
                          jnp.broadcast_to(st_m.reshape(bs, 8, 1), (bs, 8, M_DK))], 0)
    hf, hb, c_new, n_new, m_new = mlstm_scan(p.reshape(rows, L, NPROJ), seq_lens,
                                             p[:, :, GATE_OFF:GATE_OFF + 4 * M_HEADS].reshape(rows, L, 4 * M_HEADS), b_gate, c0, n0, m0, L)
    x3 = outproj_ln("mlstm", (hf.reshape(nseg, seg, -1), hb.reshape(nseg, seg, -1)), x3, mod3, 2,
                    w_out.astype(BF16), ln_g, ln_b, norm_w=norm_w, og=p, og_col=2)
    return (x3, c_new[:bp].reshape(bp, 2, M_HEADS, M_DK, M_DV), n_new[:bp].reshape(bp, 2, M_HEADS, M_DK),
            m_new[:bp, :, 0].reshape(bp, 2, M_HEADS))


def gla_layer(x3, mod3, bp, lp, st_s, w_in, w_gate2, b_gate2, norm_w, w_out, ln_g, ln_b):
    nseg, seg, _ = x3.shape
    bs = nseg - 1
    p = mod_matmul(x3, mod3, 0, _pad_cols(w_in, NPROJ).astype(BF16))
    L = GLA_CHUNK
    rows = nseg * seg // L
    seq_lens = [lp] * bp + [seg] * bs
    s0t = jnp.concatenate([jnp.zeros((bp, 8, G_DV, G_DK), F32),
                           jnp.swapaxes(st_s.reshape(bs, 8, G_DK, G_DV), -1, -2)], 0)
    of, ob, s_new = gla_scan(p.reshape(rows, L, NPROJ), seq_lens, p[:, :, GATE_OFF:GATE_OFF + 2 * G_GATE_RANK].reshape(rows, L, 2 * G_GATE_RANK),
                             w_gate2, b_gate2, s0t, L)
    x3 = outproj_ln("gla", (of.reshape(nseg, seg, -1), ob.reshape(nseg, seg, -1)), x3, mod3, 2,
                    w_out.astype(BF16), ln_g, ln_b, norm_w=jnp.tile(norm_w, G_HEADS), og=p, og_col=2)
    return x3, jnp.swapaxes(s_new[:bp], -1, -2).reshape(bp, 2, G_HEADS, G_DK, G_DV)


def na_layer(x3, mod3, bp, lp, cache_k, cache_v, w_in, rpb, w_out, ln_g, ln_b):
    nseg, seg, _ = x3.shape
    bs = nseg - 1
    hd = NA_HEADS * NA_HD
    p = mod_matmul(x3, mod3, 0, w_in.astype(BF16))
    pp = p[0].reshape(bp, lp, 3 * hd)
    hm = lambda a: _head_major(a, NA_HEADS).astype(BF16)
    yp = attention(hm(pp[..., :hd]), hm(pp[..., hd:2 * hd]), hm(pp[..., 2 * hd:]), lp)
    ps = p[1:]
    ys = na_attention(hm(ps[..., :hd]), hm(ps[..., hd:2 * hd]), hm(ps[..., 2 * hd:]),
                      hm(cache_k.reshape(bs, -1, hd)), hm(cache_v.reshape(bs, -1, hd)), na_bias_table(rpb))
    x3 = outproj_ln("plain", _stream(_token_major(yp), _token_major(ys)), x3, mod3, 2, w_out.astype(BF16), ln_g, ln_b)
    return (x3, pp[..., hd:2 * hd].reshape(bp, lp, NA_HEADS, NA_HD), pp[..., 2 * hd:].reshape(bp, lp, NA_HEADS, NA_HD))


def _rope_rotated_cols(w):
    q = MLA_ROPE // 4
    return jnp.concatenate([-w[..., q:2 * q], w[..., :q], -w[..., 3 * q:], w[..., 2 * q:3 * q]], axis=-1)


def _rope_tables(ts):
    ra = MLA_ROPE // 2
    t = np.arange(ts)
    inv = 1.0 / (ROPE_BASE ** (np.arange(0, ra, 2, dtype=np.float32) / ra))
    ang_r = (t // GRID_W).astype(np.float32)[:, None] * inv[None, :]
    ang_c = (t % GRID_W).astype(np.float32)[:, None] * inv[None, :]
    ang = np.concatenate([ang_r, ang_r, ang_c, ang_c], axis=-1).astype(np.float32)
    return jnp.cos(jnp.asarray(ang)), jnp.sin(jnp.asarray(ang))


def mla_layer(x3, mod3, bp, lp, cache_ckv, cache_kpe, w_in, q_norm, w_qup, kv_norm, w_kvup, w_out, ln_g, ln_b):
    nseg, seg, _ = x3.shape
    bs = nseg - 1
    nq = MLA_Q_LORA + MLA_KV_LORA
    w_ext = jnp.concatenate([w_in, _rope_rotated_cols(w_in[:, nq:])], axis=1)
    p = mod_matmul(x3, mod3, 0, _pad_cols(w_ext, 896).astype(BF16))
    cos_t, sin_t = _rope_tables(seg)
    cos3 = jnp.concatenate([jnp.ones((1, seg, MLA_ROPE), F32), jnp.broadcast_to(cos_t, (bs, seg, MLA_ROPE))], 0)
    sin3 = jnp.concatenate([jnp.zeros((1, seg, MLA_ROPE), F32), jnp.broadcast_to(sin_t, (bs, seg, MLA_ROPE))], 0)
    wq = w_qup.reshape(MLA_Q_LORA, MLA_HEADS, MLA_NOPE + MLA_ROPE)
    wq_rope = wq[:, :, MLA_NOPE:]
    w_q3 = jnp.concatenate([wq[:, :, :MLA_NOPE].reshape(MLA_Q_LORA, -1), wq_rope.reshape(MLA_Q_LORA, -1),
                            _rope_rotated_cols(wq_rope).reshape(MLA_Q_LORA, -1)], axis=1).astype(BF16)
    q_all = mla_q(p, q_norm, w_q3, jnp.tile(cos3, (1, 1, MLA_HEADS)), jnp.tile(sin3, (1, 1, MLA_HEADS)))
    wkv = w_kvup.reshape(MLA_KV_LORA, MLA_HEADS, MLA_NOPE + MLA_VD)
    w_kv2 = jnp.concatenate([wkv[:, :, :MLA_NOPE].reshape(MLA_KV_LORA, -1),
                             wkv[:, :, MLA_NOPE:].reshape(MLA_KV_LORA, -1)], axis=1).astype(BF16)
    ckvn, kpe, kv = mla_kv(p, kv_norm, w_kv2, cos3, sin3)
    kvc = matmul(cache_ckv.reshape(-1, MLA_KV_LORA), w_kv2, 512).reshape(bs, -1, w_kv2.shape[1])
    nn = MLA_HEADS * MLA_NOPE

    def heads(q_rows, kv_rows, kpe_rows):
        b, t, _ = q_rows.shape
        tk = kv_rows.shape[1]
        qh = jnp.concatenate([q_rows[..., :nn].reshape(b, t, MLA_HEADS, MLA_NOPE),
                              q_rows[..., nn:].reshape(b, t, MLA_HEADS, MLA_ROPE)], -1)
        kh = jnp.concatenate([kv_rows[..., :nn].reshape(b, tk, MLA_HEADS, MLA_NOPE),
                              jnp.broadcast_to(kpe_rows[:, :, None, :], (b, tk, MLA_HEADS, MLA_ROPE))], -1)
        vh = kv_rows[..., nn:].reshape(b, tk, MLA_HEADS, MLA_VD)
        tr = lambda a: jnp.transpose(a, (0, 2, 1, 3)).astype(BF16)
        return tr(qh), tr(kh), tr(vh)

    yp = attention(*heads(q_all[0].reshape(bp, lp, -1), kv[0].reshape(bp, lp, -1), kpe[0].reshape(bp, lp, -1)), lp)
    ys = attention(*heads(q_all[1:], jnp.concatenate([kv[1:], kvc], 1), jnp.concatenate([kpe[1:], cache_kpe], 1)), 256)
    x3 = outproj_ln("plain", _stream(_token_major(yp), _token_major(ys)), x3, mod3, 2, w_out.astype(BF16), ln_g, ln_b)
    return x3, ckvn[0].reshape(bp, lp, MLA_KV_LORA), kpe[0].reshape(bp, lp, MLA_ROPE)


def kernel(x_prompt, x_sample, c, c_ctx, state_mlstm_C, state_mlstm_n, state_mlstm_m, state_gla_S, cache_na_k, cache_na_v, cache_mla_ckv, cache_mla_kpe, ada_w, ada_b, ln_mix_g, ln_mix_b, ln_ffn_g, ln_ffn_b, mlstm_w_in, mlstm_b_gate, mlstm_norm_w, mlstm_w_out, gla_w_in, gla_w_gate2, gla_b_gate2, gla_norm_w, gla_w_out, na_w_in, na_rpb, na_w_out, mla_w_in, mla_q_norm, mla_w_qup, mla_kv_norm, mla_w_kvup, mla_w_out, peer_w_q, peer_subkeys, peer_u, peer_v):
    bp, lp, d = x_prompt.shape
    bs, ts, _ = x_sample.shape
    assert bp * lp == ts and bs + 1 <= 8
    x3 = _stream(x_prompt, x_sample)
    cond8 = jnp.zeros((8, d), F32).at[0].set(c_ctx).at[1:1 + bs].set(c)
    mods = adaln_all(cond8, ada_w, ada_b)
    u_all = peer_u.astype(BF16)
    vt_all = jnp.swapaxes(peer_v.reshape(DEPTH, -1, PEER_CE, d), 2, 3).astype(BF16)
    outs = {}
    for l in range(DEPTH):
        mod3 = mods[l].reshape(8, 1, ADA_CHUNKS * d)
        kind = l % 4
        if kind == 0:
            x3, outs["C"], outs["n"], outs["m"] = mlstm_layer(
                x3, mod3, bp, lp, state_mlstm_C, state_mlstm_n, state_mlstm_m, mlstm_w_in, mlstm_b_gate,
                mlstm_norm_w, mlstm_w_out, ln_mix_g[l], ln_mix_b[l])
        elif kind == 1:
            x3, outs["S"] = gla_layer(x3, mod3, bp, lp, state_gla_S, gla_w_in, gla_w_gate2, gla_b_gate2,
                                      gla_norm_w, gla_w_out, ln_mix_g[l], ln_mix_b[l])
        elif kind == 2:
            x3, outs["nk"], outs["nv"] = na_layer(x3, mod3, bp, lp, cache_na_k, cache_na_v, na_w_in, na_rpb,
                                                  na_w_out, ln_mix_g[l], ln_mix_b[l])
        else:
            x3, outs["ckv"], outs["kpe"] = mla_layer(x3, mod3, bp, lp, cache_mla_ckv, cache_mla_kpe, mla_w_in,
                                                     mla_q_norm, mla_w_qup, mla_kv_norm, mla_w_kvup, mla_w_out,
                                                     ln_mix_g[l], ln_mix_b[l])
        x3 = peer_layer(x3, mod3, l, peer_w_q[l], peer_subkeys[l], u_all, vt_all, ln_ffn_g[l], ln_ffn_b[l])
    return (x3[0].reshape(bp, lp, d), x3[1:], outs["C"], outs["n"], outs["m"], outs["S"], outs["nk"], outs["nv"],
            outs["ckv"], outs["kpe"])
```

```python
import functools

import numpy as np
import jax
import jax.numpy as jnp
from jax import lax
from jax.experimental import pallas as pl
from jax.experimental.pallas import tpu as pltpu

D_MODEL = 1024
DEPTH = 4
GRID_W = 64
DEEPNORM_ALPHA = (2.0 * DEPTH) ** 0.25
ADA_CHUNKS = 6
NORM_EPS = 1e-5

M_HEADS, M_DK, M_DV = 4, 128, 256
G_HEADS, G_DK, G_DV = 4, 128, 256
K_OFF = M_HEADS * M_DK
V_OFF = 2 * M_HEADS * M_DK
QKV_COLS = V_OFF + M_HEADS * M_DV
HID_COLS = M_HEADS * M_DV
GATE_OFF = QKV_COLS + HID_COLS
G_GATE_RANK = 16
G_GATE_NORM = 16.0
NA_HEADS, NA_HD, NA_ROWS, NA_COLS = 16, 64, 8, 16
MLA_HEADS, MLA_Q_LORA, MLA_KV_LORA, MLA_NOPE, MLA_ROPE, MLA_VD = 16, 512, 256, 64, 32, 64
ROPE_BASE = 10000.0
PEER_HEADS, PEER_NKEYS, PEER_HALF, PEER_TOPK = 8, 128, 128, 16

V7X_VMEM_LIMIT = 56 * 1024 * 1024
F32 = jnp.float32
BF16 = jnp.bfloat16
NEG_INF = float("-inf")


def _params(sem, vmem=V7X_VMEM_LIMIT):
    return pltpu.CompilerParams(dimension_semantics=sem, vmem_limit_bytes=vmem)


def _dot(a, b, dims=((1,), (0,))):
    return lax.dot_general(a, b, (dims, ((), ())), preferred_element_type=F32)


def _split3(a):
    hi = a.astype(BF16)
    r1 = a - hi.astype(F32)
    mid = r1.astype(BF16)
    lo = (r1 - mid.astype(F32)).astype(BF16)
    return hi, mid, lo


def _dot_exact_lhs(m01, a):
    hi, mid, lo = _split3(a)
    return _dot(m01, hi) + _dot(m01, mid) + _dot(m01, lo)


def _dot_exact_rhs(a, m01):
    hi, mid, lo = _split3(a)
    return _dot(hi, m01) + _dot(mid, m01) + _dot(lo, m01)


def _log_sigmoid(x):
    return jnp.minimum(x, 0.0) - jnp.log(1.0 + jnp.exp(-jnp.abs(x)))


def _sigmoid(x):
    return 1.0 / (1.0 + jnp.exp(-x))


def _adaln_kernel(c_ref, w_ref, b_ref, o_ref):
    cv = c_ref[...]
    a = cv * _sigmoid(cv)
    o_ref[0] = lax.dot_general(a, w_ref[0], (((1,), (0,)), ((), ())), precision=lax.Precision.HIGHEST,
                               preferred_element_type=F32) + b_ref[0]


def adaln_all(cond8, ada_w, ada_b):
    tn = 1024
    n = ada_w.shape[-1]
    return pl.pallas_call(
        _adaln_kernel,
        grid=(DEPTH, n // tn),
        in_specs=[pl.BlockSpec((8, D_MODEL), lambda l, j: (0, 0)),
                  pl.BlockSpec((1, D_MODEL, tn), lambda l, j: (l, 0, j)),
                  pl.BlockSpec((1, 1, tn), lambda l, j: (l, 0, j))],
        out_specs=pl.BlockSpec((1, 8, tn), lambda l, j: (l, 0, j)),
        out_shape=jax.ShapeDtypeStruct((DEPTH, 8, n), F32),
        compiler_params=_params(("arbitrary", "arbitrary")),
        name="adaln",
    )(cond8, ada_w, ada_b.reshape(DEPTH, 1, n))


def _modmm_kernel(x_ref, sh_ref, sc_ref, w_ref, o_ref, xm_ref):
    @pl.when(pl.program_id(2) == 0)
    def _():
        xm_ref[...] = (x_ref[0] * (1.0 + sc_ref[0]) + sh_ref[0]).astype(BF16)

    o_ref[0] = _dot(xm_ref[...], w_ref[...]).astype(o_ref.dtype)


def mod_matmul(x3, mod3, shift_chunk, w_bf16, tm=512, tn=None, out_dtype=F32):
    nseg, seg, d = x3.shape
    n = w_bf16.shape[1]
    tn = n if tn is None else tn
    return pl.pallas_call(
        _modmm_kernel,
        grid=(nseg, seg // tm, n // tn),
        in_specs=[pl.BlockSpec((1, tm, d), lambda s, i, j: (s, i, 0)),
                  pl.BlockSpec((1, 1, d), lambda s, i, j: (s, 0, shift_chunk)),
                  pl.BlockSpec((1, 1, d), lambda s, i, j: (s, 0, shift_chunk + 1)),
                  pl.BlockSpec((d, tn), lambda s, i, j: (0, j))],
        out_specs=pl.BlockSpec((1, tm, tn), lambda s, i, j: (s, i, j)),
        out_shape=jax.ShapeDtypeStruct((nseg, seg, n), out_dtype),
        scratch_shapes=[pltpu.VMEM((tm, d), BF16)],
        compiler_params=_params(("arbitrary", "arbitrary", "arbitrary")),
        name="mod_matmul",
    )(x3, mod3, mod3, w_bf16)


def _layer_norm_rows(y, g, b):
    mu = jnp.mean(y, axis=-1, keepdims=True)
    yc = y - mu
    var = jnp.mean(yc * yc, axis=-1, keepdims=True)
    return yc * lax.rsqrt(var + NORM_EPS) * g + b


def _outproj_kernel(*refs, mode):
    if mode == "plain":
        y_ref, x_ref, gate_ref, w_ref, g_ref, b_ref, o_ref = refs
        yin = y_ref[0].astype(BF16)
    else:
        ya_ref, yb_ref, og_ref, nw_ref, x_ref, gate_ref, w_ref, g_ref, b_ref, o_ref = refs
        hs = ya_ref[0] + yb_ref[0]
        og = og_ref[0]
        parts = []
        for h in range(4):
            seg = hs[:, h * 256:(h + 1) * 256]
            nw = nw_ref[:, h * 256:(h + 1) * 256]
            if mode == "mlstm":
                mu = jnp.mean(seg, axis=-1, keepdims=True)
                sc = seg - mu
                var = jnp.mean(sc * sc, axis=-1, keepdims=True)
                parts.append(sc * lax.rsqrt(var + NORM_EPS) * nw)
            else:
                ms = jnp.mean(seg * seg, axis=-1, keepdims=True)
                parts.append(seg * lax.rsqrt(ms + NORM_EPS) * nw)
        hn = jnp.concatenate(parts, axis=-1)
        act = _sigmoid(og) if mode == "mlstm" else og * _sigmoid(og)
        yin = (act * hn).astype(BF16)
    y = _dot(yin, w_ref[...])
    z = DEEPNORM_ALPHA * x_ref[0] + gate_ref[0] * y
    o_ref[0] = _layer_norm_rows(z, g_ref[...], b_ref[...])


def outproj_ln(mode, ys, x3, mod3, gate_chunk, w_bf16, ln_g, ln_b, norm_w=None, og=None, og_col=0, tm=512):
    nseg, seg, d = x3.shape
    k = w_bf16.shape[0]
    tok = lambda s, i: (s, i, 0)
    if mode == "plain":
        args = [ys]
        specs = [pl.BlockSpec((1, tm, k), tok)]
    else:
        args = [ys[0], ys[1], og, norm_w.reshape(1, k)]
        specs = [pl.BlockSpec((1, tm, k), tok), pl.BlockSpec((1, tm, k), tok),
                 pl.BlockSpec((1, tm, k), lambda s, i: (s, i, og_col)),
                 pl.BlockSpec((1, k), lambda s, i: (0, 0))]
    args += [x3, mod3, w_bf16, ln_g.reshape(1, d), ln_b.reshape(1, d)]
    specs += [pl.BlockSpec((1, tm, d), tok),
              pl.BlockSpec((1, 1, d), lambda s, i: (s, 0, gate_chunk)),
              pl.BlockSpec((k, d), lambda s, i: (0, 0)),
              pl.BlockSpec((1, d), lambda s, i: (0, 0)),
              pl.BlockSpec((1, d), lambda s, i: (0, 0))]
    return pl.pallas_call(
        functools.partial(_outproj_kernel, mode=mode),
        grid=(nseg, seg // tm),
        in_specs=specs,
        out_specs=pl.BlockSpec((1, tm, d), tok),
        out_shape=jax.ShapeDtypeStruct((nseg, seg, d), F32),
        compiler_params=_params(("arbitrary", "arbitrary")),
        name="outproj_ln_" + mode,
    )(*args)


def _tri(n, lower):
    r = lax.broadcasted_iota(jnp.int32, (n, n), 0)
    c = lax.broadcasted_iota(jnp.int32, (n, n), 1)
    return (c <= r) if lower else (c >= r)


def _mlstm_kernel(rowf_ref, rowb_ref, seq_ref, first_ref, last_ref,
                  pf_ref, pb_ref, gf_ref, gb_ref, gtf_ref, gtb_ref, bias_ref, biast_ref,
                  c0_ref, n0_ref, m0_ref, hf_ref, hb_ref, co_ref, no_ref, mo_ref,
                  c_s, n_s, m_s, *, L):
    item = pl.program_id(0)

    @pl.when(first_ref[item] == 1)
    def _():
        c_s[...] = c0_ref[0]
        n_s[...] = n0_ref[0]
        m_s[...] = m0_ref[0]

    dirs = [(pf_ref, gf_ref, gtf_ref, hf_ref), (pb_ref, gb_ref, gtb_ref, hb_ref)]
    masks = [_tri(L, lower=True), _tri(L, lower=False)]
    gates = []
    for d in range(2):
        g = dirs[d][1][0] + bias_ref[...]
        gt = dirs[d][2][0] + biast_ref[...]
        lf_c = _log_sigmoid(g[:, d * 8 + 4:d * 8 + 8])
        lf_r = _log_sigmoid(gt[d * 8 + 4:d * 8 + 8, :])
        b_c = _dot_exact_lhs(masks[d].astype(BF16), lf_c)
        b_r = _dot_exact_rhs(lf_r, masks[1 - d].astype(BF16))
        gates.append((g[:, d * 8:d * 8 + 4], gt[d * 8:d * 8 + 4, :], b_c, b_r))
    units = [(d, h) for d in range(2) for h in range(M_HEADS)]

    def unit_inputs(d, h):
        p_ref = dirs[d][0]
        q = p_ref[0, :, h * M_DK:(h + 1) * M_DK]
        k = p_ref[0, :, K_OFF + h * M_DK:K_OFF + (h + 1) * M_DK] * (M_DK ** -0.5)
        v = p_ref[0, :, V_OFF + h * M_DV:V_OFF + (h + 1) * M_DV].astype(BF16)
        li_c, li_r, b_c, b_r = gates[d]
        return q, k, v, li_c[:, h:h + 1], li_r[h:h + 1, :], b_c[:, h:h + 1], b_r[h:h + 1, :]

    qks = []
    for d, h in units:
        q, k, _, _, _, _, _ = unit_inputs(d, h)
        qks.append(_dot(q.astype(BF16), k.astype(BF16), ((1,), (1,))))
    smats, eis, mts = [], [], []
    for u, (d, h) in enumerate(units):
        _, _, _, _, lir, bc, br = unit_inputs(d, h)
        m_prev = m_s[u:u + 1, 0:1]
        dmat = jnp.where(masks[d], bc - br + lir, NEG_INF)
        inter = bc + m_prev
        mt = jnp.maximum(inter, jnp.max(dmat, axis=-1, keepdims=True))
        smats.append(qks[u] * jnp.exp(dmat - mt))
        eis.append(jnp.exp(inter - mt))
        mts.append(mt)
    nums = []
    for u, (d, h) in enumerate(units):
        q, _, v, _, _, _, _ = unit_inputs(d, h)
        nums.append(_dot(smats[u].astype(BF16), v) + eis[u] * _dot(q.astype(BF16), c_s[u].astype(BF16)))
    for u, (d, h) in enumerate(units):
        q, _, _, _, _, _, _ = unit_inputs(d, h)
        den = (jnp.sum(smats[u], axis=-1, keepdims=True)
               + eis[u] * jnp.sum(q * n_s[u:u + 1, :], axis=-1, keepdims=True))
        dirs[d][3][0, :, h * M_DV:(h + 1) * M_DV] = nums[u] / jnp.maximum(jnp.abs(den), jnp.exp(-mts[u]))
    for u, (d, h) in enumerate(units):
        _, k, v, lic, lir, bc, br = unit_inputs(d, h)
        last = L - 1 if d == 0 else 0
        m_prev = m_s[u:u + 1, 0:1]
        tot = br[:, last:last + 1]
        g_c = tot - bc + lic
        g_r = tot - br + lir
        m_new = jnp.maximum(tot + m_prev, jnp.max(g_r, axis=-1, keepdims=True))
        kw = k * jnp.exp(g_c - m_new)
        dec = jnp.exp(tot + m_prev - m_new)
        c_s[u] = dec * c_s[u] + _dot(kw.astype(BF16), v, ((0,), (0,)))
        n_s[u:u + 1, :] = dec * n_s[u:u + 1, :] + jnp.sum(kw, axis=0, keepdims=True)
        m_s[u:u + 1, :] = jnp.broadcast_to(m_new, (1, 128))

    @pl.when(last_ref[item] == 1)
    def _():
        co_ref[0] = c_s[...]
        no_ref[0] = n_s[...]
        mo_ref[0] = m_s[...]


def _scan_items(seq_lens, L):
    rowf, rowb, seq, first, last = [], [], [], [], []
    base = 0
    for s, t in enumerate(seq_lens):
        nc = t // L
        for c in range(nc):
            rowf.append(base + c)
            rowb.append(base + nc - 1 - c)
            seq.append(s)
            first.append(int(c == 0))
            last.append(int(c == nc - 1))
        base += nc
    return [jnp.asarray(np.array(a, np.int32)) for a in (rowf, rowb, seq, first, last)]


def mlstm_scan(p, seq_lens, g, bias, c0, n0, m0, L):
    items = _scan_items(seq_lens, L)
    nseq = len(seq_lens)
    rows = p.shape[0]
    hshape = jax.ShapeDtypeStruct((rows, L, M_HEADS * M_DV), F32)
    fwd = lambda i, rf, rb, sq, fs, ls: (rf[i], 0, 0)
    bwd = lambda i, rf, rb, sq, fs, ls: (rb[i], 0, 0)
    st4 = lambda i, rf, rb, sq, fs, ls: (sq[i], 0, 0, 0)
    st3 = lambda i, rf, rb, sq, fs, ls: (sq[i], 0, 0)
    fix = lambda i, rf, rb, sq, fs, ls: (0, 0)
    gt = jnp.swapaxes(g, 1, 2)
    return pl.pallas_call(
        functools.partial(_mlstm_kernel, L=L),
        grid_spec=pltpu.PrefetchScalarGridSpec(
            num_scalar_prefetch=5,
            grid=(items[0].shape[0],),
            in_specs=[pl.BlockSpec((1, L, QKV_COLS), fwd), pl.BlockSpec((1, L, QKV_COLS), bwd),
                      pl.BlockSpec((1, L, 16), fwd), pl.BlockSpec((1, L, 16), bwd),
                      pl.BlockSpec((1, 16, L), fwd), pl.BlockSpec((1, 16, L), bwd),
                      pl.BlockSpec((1, 16), fix), pl.BlockSpec((16, 1), fix),
                      pl.BlockSpec((1, 8, M_DK, M_DV), st4),
                      pl.BlockSpec((1, 8, M_DK), st3),
                      pl.BlockSpec((1, 8, M_DK), st3)],
            out_specs=[pl.BlockSpec((1, L, HID_COLS), fwd), pl.BlockSpec((1, L, HID_COLS), bwd),
                       pl.BlockSpec((1, 8, M_DK, M_DV), st4),
                       pl.BlockSpec((1, 8, M_DK), st3),
                       pl.BlockSpec((1, 8, M_DK), st3)],
            scratch_shapes=[pltpu.VMEM((8, M_DK, M_DV), F32), pltpu.VMEM((8, M_DK), F32),
                            pltpu.VMEM((8, M_DK), F32)]),
        out_shape=[hshape, hshape,
                   jax.ShapeDtypeStruct((nseq, 8, M_DK, M_DV), F32),
                   jax.ShapeDtypeStruct((nseq, 8, M_DK), F32),
                   jax.ShapeDtypeStruct((nseq, 8, M_DK), F32)],
        compiler_params=_params(("arbitrary",)),
        name="mlstm_scan",
    )(*items, p, p, g, g, gt, gt, bias.reshape(1, 16), bias.reshape(16, 1), c0, n0, m0)


def _gla_kernel(rowf_ref, rowb_ref, seq_ref, first_ref, last_ref,
                pf_ref, pb_ref, gf_ref, gb_ref, w2_ref, b2_ref, s0_ref, of_ref, ob_ref, so_ref, s_s, *, L):
    item = pl.program_id(0)

    @pl.when(first_ref[item] == 1)
    def _():
        s_s[...] = s0_ref[0]

    dirs = [(pf_ref, gf_ref, of_ref), (pb_ref, gb_ref, ob_ref)]
    masks = [_tri(L, lower=True), _tri(L, lower=False)]
    bcs = []
    for d in range(2):
        gr = dirs[d][1][0][:, d * G_GATE_RANK:(d + 1) * G_GATE_RANK]
        pre = lax.dot_general(gr, w2_ref[d], (((1,), (0,)), ((), ())), precision=lax.Precision.HIGHEST,
                              preferred_element_type=F32) + b2_ref[d]
        la = _log_sigmoid(pre) * (1.0 / G_GATE_NORM)
        bcs.append(_dot_exact_lhs(masks[d].astype(BF16), la))
    units = [(d, h) for d in range(2) for h in range(G_HEADS)]

    def unit_inputs(d, h):
        p_ref = dirs[d][0]
        q = p_ref[0, :, h * G_DK:(h + 1) * G_DK] * (G_DK ** -0.5)
        k = p_ref[0, :, K_OFF + h * G_DK:K_OFF + (h + 1) * G_DK]
        v = p_ref[0, :, V_OFF + h * G_DV:V_OFF + (h + 1) * G_DV].astype(BF16)
        return q, k, v, bcs[d][:, h * G_DK:(h + 1) * G_DK]

    qds, a_mats = [], []
    for d, h in units:
        q, k, _, bc = unit_inputs(d, h)
        qd = (q * jnp.exp(bc)).astype(BF16)
        kd = (k * jnp.exp(-bc)).astype(BF16)
        qds.append(qd)
        a_mats.append(jnp.where(masks[d], _dot(qd, kd, ((1,), (1,))), 0.0).astype(BF16))
    for u, (d, h) in enumerate(units):
        _, _, v, _ = unit_inputs(d, h)
        dirs[d][2][0, :, h * G_DV:(h + 1) * G_DV] = (_dot(a_mats[u], v)
                                                     + _dot(qds[u], s_s[u].astype(BF16), ((1,), (1,))))
    for u, (d, h) in enumerate(units):
        _, k, v, bc = unit_inputs(d, h)
        last = L - 1 if d == 0 else 0
        bl = bc[last:last + 1, :]
        kl = (k * jnp.exp(bl - bc)).astype(BF16)
        s_s[u] = s_s[u] * jnp.exp(bl) + _dot(v, kl, ((0,), (0,)))

    @pl.when(last_ref[item] == 1)
    def _():
        so_ref[0] = s_s[...]


def gla_scan(p, seq_lens, gr, w2, b2, s0t, L):
    items = _scan_items(seq_lens, L)
    nseq = len(seq_lens)
    oshape = jax.ShapeDtypeStruct((p.shape[0], L, G_HEADS * G_DV), F32)
    fwd = lambda i, rf, rb, sq, fs, ls: (rf[i], 0, 0)
    bwd = lambda i, rf, rb, sq, fs, ls: (rb[i], 0, 0)
    st4 = lambda i, rf, rb, sq, fs, ls: (sq[i], 0, 0, 0)
    fix3 = lambda i, rf, rb, sq, fs, ls: (0, 0, 0)
    return pl.pallas_call(
        functools.partial(_gla_kernel, L=L),
        grid_spec=pltpu.PrefetchScalarGridSpec(
            num_scalar_prefetch=5,
            grid=(items[0].shape[0],),
            in_specs=[pl.BlockSpec((1, L, QKV_COLS), fwd), pl.BlockSpec((1, L, QKV_COLS), bwd),
                      pl.BlockSpec((1, L, 32), fwd), pl.BlockSpec((1, L, 32), bwd),
                      pl.BlockSpec((2, G_GATE_RANK, 512), fix3),
                      pl.BlockSpec((2, 1, 512), fix3),
                      pl.BlockSpec((1, 8, G_DV, G_DK), st4)],
            out_specs=[pl.BlockSpec((1, L, HID_COLS), fwd), pl.BlockSpec((1, L, HID_COLS), bwd),
                       pl.BlockSpec((1, 8, G_DV, G_DK), st4)],
            scratch_shapes=[pltpu.VMEM((8, G_DV, G_DK), F32)]),
        out_shape=[oshape, oshape, jax.ShapeDtypeStruct((nseq, 8, G_DV, G_DK), F32)],
        compiler_params=_params(("arbitrary",)),
        name="gla_scan",
    )(*items, p, p, gr, gr, w2, b2.reshape(2, 1, 512), s0t)


ATTN_HEADS_PER_STEP = 2


def _attn_kernel(q_ref, k_ref, v_ref, o_ref, *, scale):
    scores = [_dot(q_ref[0, j], k_ref[0, j], ((1,), (1,))) * scale for j in range(ATTN_HEADS_PER_STEP)]
    for j, s in enumerate(scores):
        m = jnp.max(s, axis=-1, keepdims=True)
        p = jnp.exp(s - m)
        l = jnp.sum(p, axis=-1, keepdims=True)
        o_ref[0, j] = _dot(p.astype(BF16), v_ref[0, j]) / l


def attention(q, k, v, tq):
    b, h, lq, dq = q.shape
    lk, dv = k.shape[2], v.shape[3]
    hb = ATTN_HEADS_PER_STEP
    return pl.pallas_call(
        functools.partial(_attn_kernel, scale=dq ** -0.5),
        grid=(b, h // hb, lq // tq),
        in_specs=[pl.BlockSpec((1, hb, tq, dq), lambda b, h, i: (b, h, i, 0)),
                  pl.BlockSpec((1, hb, lk, dq), lambda b, h, i: (b, h, 0, 0)),
                  pl.BlockSpec((1, hb, lk, dv), lambda b, h, i: (b, h, 0, 0))],
        out_specs=pl.BlockSpec((1, hb, tq, dv), lambda b, h, i: (b, h, i, 0)),
        out_shape=jax.ShapeDtypeStruct((b, h, lq, dv), F32),
        compiler_params=_params(("arbitrary", "arbitrary", "arbitrary")),
        name="attention",
    )(q, k, v)


NA_RB = 16


def _na_kernel(q_ref, k_ref, v_ref, kc_ref, vc_ref, bias_ref, o_ref, *, rows):
    j = pl.program_id(2)
    scale = NA_HD ** -0.5
    s_ctx_all = _dot(q_ref[0, 0], kc_ref[0, 0], ((1,), (1,))) * scale
    offs, s_locs = [], []
    for a in range(NA_RB):
        r = j * NA_RB + a
        start = jnp.clip(r - NA_ROWS // 2, 0, rows - NA_ROWS)
        dr0 = start - r + (NA_ROWS - 1)
        offs.append(pl.multiple_of(start * GRID_W, GRID_W))
        qa = q_ref[0, 0, a * GRID_W:(a + 1) * GRID_W, :]
        kl = k_ref[0, 0, pl.ds(offs[a], NA_ROWS * GRID_W), :]
        s_locs.append(_dot(qa, kl, ((1,), (1,))) * scale + bias_ref[0, dr0])
    p_locs, p_ctxs, ls = [], [], []
    for a in range(NA_RB):
        s_ctx = s_ctx_all[a * GRID_W:(a + 1) * GRID_W, :]
        m = jnp.maximum(jnp.max(s_locs[a], axis=-1, keepdims=True), jnp.max(s_ctx, axis=-1, keepdims=True))
        p_loc = jnp.exp(s_locs[a] - m)
        p_ctx = jnp.exp(s_ctx - m)
        ls.append(jnp.sum(p_loc, axis=-1, keepdims=True) + jnp.sum(p_ctx, axis=-1, keepdims=True))
        p_locs.append(p_loc.astype(BF16))
        p_ctxs.append(p_ctx.astype(BF16))
    o_ctx_all = _dot(jnp.concatenate(p_ctxs, axis=0), vc_ref[0, 0])
    for a in range(NA_RB):
        vl = v_ref[0, 0, pl.ds(offs[a], NA_ROWS * GRID_W), :]
        o = _dot(p_locs[a], vl) + o_ctx_all[a * GRID_W:(a + 1) * GRID_W, :]
        o_ref[0, 0, a * GRID_W:(a + 1) * GRID_W, :] = o / ls[a]


def na_bias_table(rpb):
    cq = np.arange(GRID_W)[:, None]
    ck = np.arange(GRID_W)[None, :]
    cs = np.clip(cq - NA_COLS // 2, 0, GRID_W - NA_COLS)
    ok = (ck >= cs) & (ck < cs + NA_COLS)
    dc = np.clip(ck - cq, -(NA_COLS - 1), NA_COLS - 1) + (NA_COLS - 1)
    t = jnp.where(ok[None, None], rpb.astype(F32)[:, :, dc], NEG_INF)
    rows = np.arange(NA_ROWS)[:, None] + np.arange(NA_ROWS)[None, :]
    tf = t[:, rows]
    return jnp.transpose(tf, (0, 1, 3, 2, 4)).reshape(NA_HEADS, NA_ROWS, GRID_W, NA_ROWS * GRID_W)


def na_attention(q, k, v, kc, vc, bias):
    b, h, t, dh = q.shape
    lc = kc.shape[2]
    rows = t // GRID_W
    full = lambda b, h, j: (b, h, 0, 0)
    return pl.pallas_call(
        functools.partial(_na_kernel, rows=rows),
        grid=(b, h, rows // NA_RB),
        in_specs=[pl.BlockSpec((1, 1, NA_RB * GRID_W, dh), lambda b, h, j: (b, h, j, 0)),
                  pl.BlockSpec((1, 1, t, dh), full), pl.BlockSpec((1, 1, t, dh), full),
                  pl.BlockSpec((1, 1, lc, dh), full), pl.BlockSpec((1, 1, lc, dh), full),
                  pl.BlockSpec((1, NA_ROWS, GRID_W, NA_ROWS * GRID_W), lambda b, h, j: (h, 0, 0, 0))],
        out_specs=pl.BlockSpec((1, 1, NA_RB * GRID_W, dh), lambda b, h, j: (b, h, j, 0)),
        out_shape=jax.ShapeDtypeStruct((b, h, t, dh), F32),
        compiler_params=_params(("arbitrary", "arbitrary", "arbitrary")),
        name="na_attention",
    )(q, k, v, kc, vc, bias)


def _rms_rows(x, g):
    return x * lax.rsqrt(jnp.mean(x * x, axis=-1, keepdims=True) + NORM_EPS) * g


def _mla_q_kernel(cq_ref, g_ref, w_ref, cos_ref, sin_ref, o_ref):
    r = _dot(_rms_rows(cq_ref[0], g_ref[...]).astype(BF16), w_ref[...])
    nn = MLA_HEADS * MLA_NOPE
    nr = MLA_HEADS * MLA_ROPE
    o_ref[0, :, :nn] = r[:, :nn]
    o_ref[0, :, nn:] = r[:, nn:nn + nr] * cos_ref[0] + r[:, nn + nr:] * sin_ref[0]


def mla_q(p, q_norm, w_q3, cos_q, sin_q, tm=512):
    nseg, seg, _ = p.shape
    nout = MLA_HEADS * (MLA_NOPE + MLA_ROPE)
    nr = MLA_HEADS * MLA_ROPE
    tok = lambda s, i: (s, i, 0)
    return pl.pallas_call(
        _mla_q_kernel,
        grid=(nseg, seg // tm),
        in_specs=[pl.BlockSpec((1, tm, MLA_Q_LORA), tok),
                  pl.BlockSpec((1, MLA_Q_LORA), lambda s, i: (0, 0)),
                  pl.BlockSpec(w_q3.shape, lambda s, i: (0, 0)),
                  pl.BlockSpec((1, tm, nr), tok), pl.BlockSpec((1, tm, nr), tok)],
        out_specs=pl.BlockSpec((1, tm, nout), tok),
        out_shape=jax.ShapeDtypeStruct((nseg, seg, nout), F32),
        compiler_params=_params(("arbitrary", "arbitrary")),
        name="mla_q",
    )(p, q_norm.reshape(1, -1), w_q3, cos_q, sin_q)


def _mla_kv_kernel(ckv_ref, kpe_ref, g_ref, w_ref, cos_ref, sin_ref, ckvn_ref, kpeo_ref, kv_ref):
    cn = _rms_rows(ckv_ref[0], g_ref[...])
    ckvn_ref[0] = cn
    kv_ref[0] = _dot(cn.astype(BF16), w_ref[...])
    kp = kpe_ref[0]
    kpeo_ref[0] = kp[:, :MLA_ROPE] * cos_ref[0] + kp[:, MLA_ROPE:2 * MLA_ROPE] * sin_ref[0]


def mla_kv(p, kv_norm, w_kv, cos_k, sin_k, tm=512):
    nseg, seg, _ = p.shape
    nkv = w_kv.shape[1]
    tok = lambda s, i: (s, i, 0)
    return pl.pallas_call(
        _mla_kv_kernel,
        grid=(nseg, seg // tm),
        in_specs=[pl.BlockSpec((1, tm, MLA_KV_LORA), lambda s, i: (s, i, MLA_Q_LORA // MLA_KV_LORA)),
                  pl.BlockSpec((1, tm, 128), lambda s, i: (s, i, (MLA_Q_LORA + MLA_KV_LORA) // 128)),
                  pl.BlockSpec((1, MLA_KV_LORA), lambda s, i: (0, 0)),
                  pl.BlockSpec(w_kv.shape, lambda s, i: (0, 0)),
                  pl.BlockSpec((1, tm, MLA_ROPE), tok), pl.BlockSpec((1, tm, MLA_ROPE), tok)],
        out_specs=[pl.BlockSpec((1, tm, MLA_KV_LORA), tok), pl.BlockSpec((1, tm, MLA_ROPE), tok),
                   pl.BlockSpec((1, tm, nkv), tok)],
        out_shape=[jax.ShapeDtypeStruct((nseg, seg, MLA_KV_LORA), F32),
                   jax.ShapeDtypeStruct((nseg, seg, MLA_ROPE), F32),
                   jax.ShapeDtypeStruct((nseg, seg, nkv), F32)],
        compiler_params=_params(("arbitrary", "arbitrary")),
        name="mla_kv",
    )(p, p, kv_norm.reshape(1, -1), w_kv, cos_k, sin_k)


def _mm_kernel(a_ref, w_ref, o_ref):
    o_ref[...] = _dot(a_ref[...].astype(BF16), w_ref[...])


def matmul(a, w_bf16, tm):
    m, k = a.shape
    n = w_bf16.shape[1]
    return pl.pallas_call(
        _mm_kernel,
        grid=(m // tm,),
        in_specs=[pl.BlockSpec((tm, k), lambda i: (i, 0)), pl.BlockSpec((k, n), lambda i: (0, 0))],
        out_specs=pl.BlockSpec((tm, n), lambda i: (i, 0)),
        out_shape=jax.ShapeDtypeStruct((m, n), F32),
        compiler_params=_params(("arbitrary",)),
        name="matmul",
    )(a, w_bf16)


PEER_RT = 128
NOT_TOP = 99.0
RANK_CODE = 2.0 ** 100


def _top16(s, exact):
    vals = []
    if exact:
        key = lax.broadcasted_iota(jnp.int32, s.shape, 0).astype(F32)
        rank = jnp.full(s.shape, NOT_TOP, F32)
        for r in range(PEER_TOPK):
            m = jnp.max(s, axis=0, keepdims=True)
            hit = key == jnp.min(jnp.where(s == m, key, 1e9), axis=0, keepdims=True)
            rank = jnp.where(hit, float(r), rank)
            s = jnp.where(hit, NEG_INF, s)
            vals.append(m)
        return vals, rank
    for r in range(PEER_TOPK):
        m = jnp.max(s, axis=0, keepdims=True)
        s = jnp.where(s == m, -RANK_CODE * (1.0 + r / 32.0), s)
        vals.append(m)
    return vals, jnp.where(s <= -0.5 * RANK_CODE, s * (-32.0 / RANK_CODE) - 32.0, NOT_TOP)


def _pair_topk(av, bv, exact):
    n = av[0].shape[-1]
    a_lo, a_hi = jnp.concatenate(av[:8], 0), jnp.concatenate(av[8:], 0)
    b_lo, b_hi = jnp.concatenate(bv[:8], 0), jnp.concatenate(bv[8:], 0)
    row = lax.broadcasted_iota(jnp.int32, (8, n), 0).astype(F32)

    no_pos = 1e8

    def rows_b(a, b_blk, boff, nvalid):
        ok = row < nvalid
        return jnp.where(ok, av[a] + b_blk, NEG_INF), jnp.where(ok, a * 16.0 + boff + row, no_pos)

    def rows_a(b, a_blk, aoff, lo, hi):
        ok = (row >= lo) & (row < hi)
        return jnp.where(ok, a_blk + bv[b], NEG_INF), jnp.where(ok, (aoff + row) * 16.0 + b, no_pos)

    groups = [rows_b(0, b_lo, 0, 8), rows_b(0, b_hi, 8, 8), rows_b(1, b_lo, 0, 8), rows_b(2, b_lo, 0, 5),
              rows_b(3, b_lo, 0, 4), rows_a(0, a_lo, 0, 4, 8), rows_a(0, a_hi, 8, 0, 8),
              rows_a(1, a_lo, 0, 4, 8), rows_a(2, a_lo, 0, 4, 5)]
    cands = [g[0] for g in groups]
    poss = [g[1] for g in groups]
    sels = [jnp.zeros((8, n), F32) for _ in groups]
    top = av[0] + bv[0]
    z = jnp.zeros((1, n), F32)
    for _ in range(PEER_TOPK):
        m = functools.reduce(jnp.maximum, cands)
        m = jnp.max(m, axis=0, keepdims=True)
        if exact:
            first = functools.reduce(jnp.minimum, [jnp.where(c == m, p, 1e9) for c, p in zip(cands, poss)])
            first = jnp.min(first, axis=0, keepdims=True)
            hits = [p == first for p in poss]
            cands = [jnp.where(hh, NEG_INF, c) for hh, c in zip(hits, cands)]
            sels = [jnp.where(hh, 1.0, s) for hh, s in zip(hits, sels)]
        else:
            cands = [jnp.where(c == m, -RANK_CODE, c) for c in cands]
        z = z + jnp.exp(m - top)
    if not exact:
        sels = [jnp.where(c == -RANK_CODE, 1.0, 0.0) for c in cands]
    cnt = lambda x: jnp.sum(x, axis=0, keepdims=True)
    cut_lo = sels[5] + sels[7] + sels[8]
    for a, c in enumerate([cnt(sels[0]) + cnt(sels[1]), cnt(sels[2]), cnt(sels[3]), cnt(sels[4])]):
        cut_lo = cut_lo + jnp.where(row == a, c, 0.0)
    return cut_lo, sels[6], z, cnt(cut_lo) + cnt(sels[6])


def _peer_route_kernel(x_ref, sh_ref, sc_ref, wq_ref, sk_ref, e1_ref, cut_ref, e2_ref, r2_ref, q_s, *, tm):
    xm = (x_ref[0] * (1.0 + sc_ref[0]) + sh_ref[0]).astype(BF16)
    q = _dot(xm, wq_ref[...])
    for hp in range(2 * PEER_HEADS):
        q_s[hp] = q[:, hp * PEER_HALF:(hp + 1) * PEER_HALF]

    def route(h, tok, exact):
        def scores(hp):
            return lax.dot_general(sk_ref[hp], q_s[hp, tok, :], (((1,), (1,)), ((), ())),
                                   precision=lax.Precision.HIGHEST, preferred_element_type=F32)

        s1, s2 = scores(2 * h), scores(2 * h + 1)
        av, rank1 = _top16(s1, exact)
        bv, rank2 = _top16(s2, exact)
        cut_lo, cut_hi, z, nsel = _pair_topk(av, bv, exact)
        ridx = jnp.clip(rank1, 0.0, NOT_TOP).astype(jnp.int32)
        parts = []
        for v in range(PEER_NKEYS // 8):
            rk = ridx[v * 8:(v + 1) * 8]
            lo = jnp.take_along_axis(cut_lo, rk & 7, axis=0)
            hi = jnp.take_along_axis(cut_hi, rk & 7, axis=0)
            parts.append(jnp.where(rk < 8, lo, jnp.where(rk < PEER_TOPK, hi, 0.0)))
        cut = jnp.concatenate(parts, axis=0)
        e1_ref[0, h, :, tok] = (jnp.exp(s1 - av[0]) / z).astype(BF16)
        cut_ref[0, h, :, tok] = cut.astype(BF16)
        e2_ref[0, h, :, tok] = jnp.exp(s2 - bv[0]).astype(BF16)
        r2_ref[0, h, :, tok] = rank2.astype(BF16)
        ranked = lambda rk: jnp.sum(jnp.where(rk < PEER_TOPK, 1.0, 0.0), axis=0, keepdims=True)
        return ranked(rank1), ranked(rank2), nsel

    def body(h, carry):
        toks = [pl.ds(t0, PEER_RT) for t0 in range(0, tm, PEER_RT)]
        counts = [route(h, tok, exact=False) for tok in toks]
        for tok, cnts in zip(toks, counts):
            bad = functools.reduce(jnp.maximum, [jnp.abs(cn - PEER_TOPK) for cn in cnts])

            @pl.when(jnp.max(bad) > 0.0)
            def _():
                route(h, tok, exact=True)
        return carry

    lax.fori_loop(0, PEER_HEADS, body, 0)


def peer_route(x3, mod3, shift_chunk, wq_bf16, subkeys, tm=512):
    nseg, seg, d = x3.shape
    tok = lambda s, i: (s, i, 0)
    rshape = jax.ShapeDtypeStruct((nseg, PEER_HEADS, PEER_NKEYS, seg), BF16)
    rspec = pl.BlockSpec((1, PEER_HEADS, PEER_NKEYS, tm), lambda s, i: (s, 0, 0, i))
    return pl.pallas_call(
        functools.partial(_peer_route_kernel, tm=tm),
        grid=(nseg, seg // tm),
        in_specs=[pl.BlockSpec((1, tm, d), tok),
                  pl.BlockSpec((1, 1, d), lambda s, i: (s, 0, shift_chunk)),
                  pl.BlockSpec((1, 1, d), lambda s, i: (s, 0, shift_chunk + 1)),
                  pl.BlockSpec(wq_bf16.shape, lambda s, i: (0, 0)),
                  pl.BlockSpec((2 * PEER_HEADS, PEER_NKEYS, PEER_HALF), lambda s, i: (0, 0, 0))],
        out_specs=[rspec, rspec, rspec, rspec],
        out_shape=[rshape] * 4,
        scratch_shapes=[pltpu.VMEM((2 * PEER_HEADS, tm, PEER_HALF), F32)],
        compiler_params=_params(("arbitrary", "arbitrary")),
        name="peer_route",
    )(x3, mod3, mod3, wq_bf16, subkeys.reshape(2 * PEER_HEADS, PEER_NKEYS, PEER_HALF))


PEER_CE = 1024


def _gelu_tanh(x):
    return 0.5 * x * (1.0 + jnp.tanh(0.7978845608028654 * (x + 0.044715 * x * x * x)))


def _peer_dense_kernel(sh_ref, sc_ref, u_ref, vt_ref, e1_ref, cut_ref, e2_ref, r2_ref, x_ref, gate_ref, g_ref, b_ref,
                       o_ref, acc_s, at_s, w_s, e2_s, r2_s, xm_s, *, tm):
    e = pl.program_id(2)
    nb = PEER_CE // PEER_NKEYS
    ntt = tm // PEER_RT

    @pl.when(e == 0)
    def _():
        xm_s[...] = (x_ref[0] * (1.0 + sc_ref[0]) + sh_ref[0]).astype(BF16)
        acc_s[...] = jnp.zeros_like(acc_s)
        e2_s[:, :, :tm] = e2_ref[0]
        r2_s[:, :, :tm] = r2_ref[0]

    packed = (PEER_NKEYS // 16, 16, PEER_RT)
    ng = 2

    def gate_tiles(tt, i0):
        tok = slice(tt * PEER_RT, (tt + 1) * PEER_RT)
        gmats = [jnp.zeros(packed, BF16) for _ in range(ng)]
        for h in range(PEER_HEADS):
            e2 = e2_s[h, :, tok].reshape(packed)
            r2 = r2_s[h, :, tok].reshape(packed)
            for k in range(ng):
                i = i0 + k
                e1 = jnp.broadcast_to(e1_ref[0, h, i:i + 1, tok], (16, PEER_RT))[None]
                cut = jnp.broadcast_to(cut_ref[0, h, i:i + 1, tok], (16, PEER_RT))[None]
                gmats[k] = gmats[k] + e1 * jnp.where(r2 < cut, e2, jnp.zeros_like(e2))
        for k in range(ng):
            rows = slice((i0 + k) * PEER_NKEYS, (i0 + k + 1) * PEER_NKEYS)
            act = _gelu_tanh(at_s[rows, tok]).astype(BF16)
            w_s[rows, tok] = gmats[k].reshape(PEER_NKEYS, PEER_RT) * act

    at_s[:, :tm] = _dot(u_ref[0], xm_s[...], ((1,), (1,)))
    for tt in range(ntt):
        for i0 in range(0, nb, ng):
            gate_tiles(tt, i0)
    acc_s[:, :tm] += _dot(vt_ref[0, 0], w_s[:, :tm])

    @pl.when(e == pl.num_programs(2) - 1)
    def _():
        z = DEEPNORM_ALPHA * x_ref[0] + gate_ref[0] * acc_s[:, :tm].T
        o_ref[0] = _layer_norm_rows(z, g_ref[...], b_ref[...])


def peer_dense(shift_chunk, u_all, vt_all, l, e1, cut, e2, r2, x3, mod3, gate_chunk, ln_g, ln_b, tm=1024):
    nseg, seg, d = x3.shape
    ne = u_all.shape[1]
    nb = PEER_CE // PEER_NKEYS
    tp = tm + PEER_RT
    tok = lambda s, i, e: (s, i, 0)
    chunk = pl.BlockSpec((1, PEER_HEADS, nb, tm), lambda s, i, e: (s, 0, e, i))
    full = pl.BlockSpec((1, PEER_HEADS, PEER_NKEYS, tm), lambda s, i, e: (s, 0, 0, i))
    return pl.pallas_call(
        functools.partial(_peer_dense_kernel, tm=tm),
        grid=(nseg, seg // tm, ne // PEER_CE),
        in_specs=[pl.BlockSpec((1, 1, d), lambda s, i, e: (s, 0, shift_chunk)),
                  pl.BlockSpec((1, 1, d), lambda s, i, e: (s, 0, shift_chunk + 1)),
                  pl.BlockSpec((1, PEER_CE, d), lambda s, i, e: (l, e, 0)),
                  pl.BlockSpec((1, 1, d, PEER_CE), lambda s, i, e: (l, e, 0, 0)),
                  chunk, chunk, full, full,
                  pl.BlockSpec((1, tm, d), tok),
                  pl.BlockSpec((1, 1, d), lambda s, i, e: (s, 0, gate_chunk)),
                  pl.BlockSpec((1, d), lambda s, i, e: (0, 0)),
                  pl.BlockSpec((1, d), lambda s, i, e: (0, 0))],
        out_specs=pl.BlockSpec((1, tm, d), tok),
        out_shape=jax.ShapeDtypeStruct((nseg, seg, d), F32),
        scratch_shapes=[pltpu.VMEM((d, tp), F32), pltpu.VMEM((PEER_CE, tp), F32), pltpu.VMEM((PEER_CE, tp), BF16),
                        pltpu.VMEM((PEER_HEADS, PEER_NKEYS, tp), BF16), pltpu.VMEM((PEER_HEADS, PEER_NKEYS, tp), BF16),
                        pltpu.VMEM((tm, d), BF16)],
        compiler_params=_params(("arbitrary", "arbitrary", "arbitrary")),
        name="peer_dense",
    )(mod3, mod3, u_all, vt_all, e1, cut, e2, r2, x3, mod3, ln_g.reshape(1, d), ln_b.reshape(1, d))


def peer_layer(x3, mod3, l, wq, subkeys, u_all, vt_all, ln_g, ln_b):
    e1, cut, e2, r2 = peer_route(x3, mod3, 3, wq.astype(BF16), subkeys)
    return peer_dense(3, u_all, vt_all, l, e1, cut, e2, r2, x3, mod3, 5, ln_g, ln_b)


def _pad_cols(w, n):
    return jnp.pad(w, ((0, 0), (0, n - w.shape[1])))


def _stream(prompt_part, sample_part):
    return jnp.concatenate([prompt_part.reshape(1, -1, prompt_part.shape[-1]), sample_part], axis=0)


def _head_major(a, heads):
    b, t, _ = a.shape
    return jnp.transpose(a.reshape(b, t, heads, -1), (0, 2, 1, 3))


def _token_major(a):
    b, h, t, dh = a.shape
    return jnp.transpose(a, (0, 2, 1, 3)).reshape(b, t, h * dh)


MLSTM_CHUNK = 256
GLA_CHUNK = 64
NPROJ = 3200


def mlstm_layer(x3, mod3, bp, lp, st_c, st_n, st_m, w_in, b_gate, norm_w, w_out, ln_g, ln_b):
    nseg, seg, _ = x3.shape
    bs = nseg - 1
    p = mod_matmul(x3, mod3, 0, _pad_cols(w_in, NPROJ).astype(BF16))
    L = min(MLSTM_CHUNK, lp)
    rows = nseg * seg // L
    seq_lens = [lp] * bp + [seg] * bs
    c0 = jnp.concatenate([jnp.zeros((bp, 8, M_DK, M_DV), F32), st_c.reshape(bs, 8, M_DK, M_DV)], 0)
    n0 = jnp.concatenate([jnp.zeros((bp, 8, M_DK), F32), st_n.reshape(bs, 8, M_DK)], 0)
    m0 = jnp.concatenate([jnp.zeros((bp, 8, M_DK), F32),
                          jnp.broadcast_to(st_m.reshape(bs, 8, 1), (bs, 8, M_DK))], 0)
    hf, hb, c_new, n_new, m_new = mlstm_scan(p.reshape(rows, L, NPROJ), seq_lens,
                                             p[:, :, GATE_OFF:GATE_OFF + 4 * M_HEADS].reshape(rows, L, 4 * M_HEADS), b_gate, c0, n0, m0, L)
    x3 = outproj_ln("mlstm", (hf.reshape(nseg, seg, -1), hb.reshape(nseg, seg, -1)), x3, mod3, 2,
                    w_out.astype(BF16), ln_g, ln_b, norm_w=norm_w, og=p, og_col=2)
    return (x3, c_new[:bp].reshape(bp, 2, M_HEADS, M_DK, M_DV), n_new[:bp].reshape(bp, 2, M_HEADS, M_DK),
            m_new[:bp, :, 0].reshape(bp, 2, M_HEADS))


def gla_layer(x3, mod3, bp, lp, st_s, w_in, w_gate2, b_gate2, norm_w, w_out, ln_g, ln_b):
    nseg, seg, _ = x3.shape
    bs = nseg - 1
    p = mod_matmul(x3, mod3, 0, _pad_cols(w_in, NPROJ).astype(BF16))
    L = GLA_CHUNK
    rows = nseg * seg // L
    seq_lens = [lp] * bp + [seg] * bs
    s0t = jnp.concatenate([jnp.zeros((bp, 8, G_DV, G_DK), F32),
                           jnp.swapaxes(st_s.reshape(bs, 8, G_DK, G_DV), -1, -2)], 0)
    of, ob, s_new = gla_scan(p.reshape(rows, L, NPROJ), seq_lens, p[:, :, GATE_OFF:GATE_OFF + 2 * G_GATE_RANK].reshape(rows, L, 2 * G_GATE_RANK),
                             w_gate2, b_gate2, s0t, L)
    x3 = outproj_ln("gla", (of.reshape(nseg, seg, -1), ob.reshape(nseg, seg, -1)), x3, mod3, 2,
                    w_out.astype(BF16), ln_g, ln_b, norm_w=jnp.tile(norm_w, G_HEADS), og=p, og_col=2)
    return x3, jnp.swapaxes(s_new[:bp], -1, -2).reshape(bp, 2, G_HEADS, G_DK, G_DV)


def na_layer(x3, mod3, bp, lp, cache_k, cache_v, w_in, rpb, w_out, ln_g, ln_b):
    nseg, seg, _ = x3.shape
    bs = nseg - 1
    hd = NA_HEADS * NA_HD
    p = mod_matmul(x3, mod3, 0, w_in.astype(BF16))
    pp = p[0].reshape(bp, lp, 3 * hd)
    hm = lambda a: _head_major(a, NA_HEADS).astype(BF16)
    yp = attention(hm(pp[..., :hd]), hm(pp[..., hd:2 * hd]), hm(pp[..., 2 * hd:]), lp)
    ps = p[1:]
    ys = na_attention(hm(ps[..., :hd]), hm(ps[..., hd:2 * hd]), hm(ps[..., 2 * hd:]),
                      hm(cache_k.reshape(bs, -1, hd)), hm(cache_v.reshape(bs, -1, hd)), na_bias_table(rpb))
    x3 = outproj_ln("plain", _stream(_token_major(yp), _token_major(ys)), x3, mod3, 2, w_out.astype(BF16), ln_g, ln_b)
    return (x3, pp[..., hd:2 * hd].reshape(bp, lp, NA_HEADS, NA_HD), pp[..., 2 * hd:].reshape(bp, lp, NA_HEADS, NA_HD))


def _rope_rotated_cols(w):
    q = MLA_ROPE // 4
    return jnp.concatenate([-w[..., q:2 * q], w[..., :q], -w[..., 3 * q:], w[..., 2 * q:3 * q]], axis=-1)


def _rope_tables(ts):
    ra = MLA_ROPE // 2
    t = np.arange(ts)
    inv = 1.0 / (ROPE_BASE ** (np.arange(0, ra, 2, dtype=np.float32) / ra))
    ang_r = (t // GRID_W).astype(np.float32)[:, None] * inv[None, :]
    ang_c = (t % GRID_W).astype(np.float32)[:, None] * inv[None, :]
    ang = np.concatenate([ang_r, ang_r, ang_c, ang_c], axis=-1).astype(np.float32)
    return jnp.cos(jnp.asarray(ang)), jnp.sin(jnp.asarray(ang))


def mla_layer(x3, mod3, bp, lp, cache_ckv, cache_kpe, w_in, q_norm, w_qup, kv_norm, w_kvup, w_out, ln_g, ln_b):
    nseg, seg, _ = x3.shape
    bs = nseg - 1
    nq = MLA_Q_LORA + MLA_KV_LORA
    w_ext = jnp.concatenate([w_in, _rope_rotated_cols(w_in[:, nq:])], axis=1)
    p = mod_matmul(x3, mod3, 0, _pad_cols(w_ext, 896).astype(BF16))
    cos_t, sin_t = _rope_tables(seg)
    cos3 = jnp.concatenate([jnp.ones((1, seg, MLA_ROPE), F32), jnp.broadcast_to(cos_t, (bs, seg, MLA_ROPE))], 0)
    sin3 = jnp.concatenate([jnp.zeros((1, seg, MLA_ROPE), F32), jnp.broadcast_to(sin_t, (bs, seg, MLA_ROPE))], 0)
    wq = w_qup.reshape(MLA_Q_LORA, MLA_HEADS, MLA_NOPE + MLA_ROPE)
    wq_rope = wq[:, :, MLA_NOPE:]
    w_q3 = jnp.concatenate([wq[:, :, :MLA_NOPE].reshape(MLA_Q_LORA, -1), wq_rope.reshape(MLA_Q_LORA, -1),
                            _rope_rotated_cols(wq_rope).reshape(MLA_Q_LORA, -1)], axis=1).astype(BF16)
    q_all = mla_q(p, q_norm, w_q3, jnp.tile(cos3, (1, 1, MLA_HEADS)), jnp.tile(sin3, (1, 1, MLA_HEADS)))
    wkv = w_kvup.reshape(MLA_KV_LORA, MLA_HEADS, MLA_NOPE + MLA_VD)
    w_kv2 = jnp.concatenate([wkv[:, :, :MLA_NOPE].reshape(MLA_KV_LORA, -1),
                             wkv[:, :, MLA_NOPE:].reshape(MLA_KV_LORA, -1)], axis=1).astype(BF16)
    ckvn, kpe, kv = mla_kv(p, kv_norm, w_kv2, cos3, sin3)
    kvc = matmul(cache_ckv.reshape(-1, MLA_KV_LORA), w_kv2, 512).reshape(bs, -1, w_kv2.shape[1])
    nn = MLA_HEADS * MLA_NOPE

    def heads(q_rows, kv_rows, kpe_rows):
        b, t, _ = q_rows.shape
        tk = kv_rows.shape[1]
        qh = jnp.concatenate([q_rows[..., :nn].reshape(b, t, MLA_HEADS, MLA_NOPE),
                              q_rows[..., nn:].reshape(b, t, MLA_HEADS, MLA_ROPE)], -1)
        kh = jnp.concatenate([kv_rows[..., :nn].reshape(b, tk, MLA_HEADS, MLA_NOPE),
                              jnp.broadcast_to(kpe_rows[:, :, None, :], (b, tk, MLA_HEADS, MLA_ROPE))], -1)
        vh = kv_rows[..., nn:].reshape(b, tk, MLA_HEADS, MLA_VD)
        tr = lambda a: jnp.transpose(a, (0, 2, 1, 3)).astype(BF16)
        return tr(qh), tr(kh), tr(vh)

    yp = attention(*heads(q_all[0].reshape(bp, lp, -1), kv[0].reshape(bp, lp, -1), kpe[0].reshape(bp, lp, -1)), lp)
    ys = attention(*heads(q_all[1:], jnp.concatenate([kv[1:], kvc], 1), jnp.concatenate([kpe[1:], cache_kpe], 1)), 256)
    x3 = outproj_ln("plain", _stream(_token_major(yp), _token_major(ys)), x3, mod3, 2, w_out.astype(BF16), ln_g, ln_b)
    return x3, ckvn[0].reshape(bp, lp, MLA_KV_LORA), kpe[0].reshape(bp, lp, MLA_ROPE)


def kernel(x_prompt, x_sample, c, c_ctx, state_mlstm_C, state_mlstm_n, state_mlstm_m, state_gla_S, cache_na_k, cache_na_v, cache_mla_ckv, cache_mla_kpe, ada_w, ada_b, ln_mix_g, ln_mix_b, ln_ffn_g, ln_ffn_b, mlstm_w_in, mlstm_b_gate, mlstm_norm_w, mlstm_w_out, gla_w_in, gla_w_gate2, gla_b_gate2, gla_norm_w, gla_w_out, na_w_in, na_rpb, na_w_out, mla_w_in, mla_q_norm, mla_w_qup, mla_kv_norm, mla_w_kvup, mla_w_out, peer_w_q, peer_subkeys, peer_u, peer_v):
    bp, lp, d = x_prompt.shape
    bs, ts, _ = x_sample.shape
    assert bp * lp == ts and bs + 1 <= 8
    x3 = _stream(x_prompt, x_sample)
    cond8 = jnp.zeros((8, d), F32).at[0].set(c_ctx).at[1:1 + bs].set(c)
    mods = adaln_all(cond8, ada_w, ada_b)
    u_all = peer_u.astype(BF16)
    vt_all = jnp.swapaxes(peer_v.reshape(DEPTH, -1, PEER_CE, d), 2, 3).astype(BF16)
    outs = {}
    for l in range(DEPTH):
        mod3 = mods[l].reshape(8, 1, ADA_CHUNKS * d)
        kind = l % 4
        if kind == 0:
            x3, outs["C"], outs["n"], outs["m"] = mlstm_layer(
                x3, mod3, bp, lp, state_mlstm_C, state_mlstm_n, state_mlstm_m, mlstm_w_in, mlstm_b_gate,
                mlstm_norm_w, mlstm_w_out, ln_mix_g[l], ln_mix_b[l])
        elif kind == 1:
            x3, outs["S"] = gla_layer(x3, mod3, bp, lp, state_gla_S, gla_w_in, gla_w_gate2, gla_b_gate2,
                                      gla_norm_w, gla_w_out, ln_mix_g[l], ln_mix_b[l])
        elif kind == 2:
            x3, outs["nk"], outs["nv"] = na_layer(x3, mod3, bp, lp, cache_na_k, cache_na_v, na_w_in, na_rpb,
                                                  na_w_out, ln_mix_g[l], ln_mix_b[l])
        else:
            x3, outs["ckv"], outs["kpe"] = mla_layer(x3, mod3, bp, lp, cache_mla_ckv, cache_mla_kpe, mla_w_in,
                                                     mla_q_norm, mla_w_qup, mla_kv_norm, mla_w_kvup, mla_w_out,
                                                     ln_mix_g[l], ln_mix_b[l])
        x3 = peer_layer(x3, mod3, l, peer_w_q[l], peer_subkeys[l], u_all, vt_all, ln_ffn_g[l], ln_ffn_b[l])
    return (x3[0].reshape(bp, lp, d), x3[1:], outs["C"], outs["n"], outs["m"], outs["S"], outs["nk"], outs["nv"],
            outs["ckv"], outs["kpe"])
```
